```python
import math
import jax, jax.numpy as jnp
from jax import lax
import numpy as np

D_MODEL = 1024
BATCH = 32
SEQ = 2048
DEPTH = 1

MEM_LEN = 256
EPS = 1e-6
D_FF = 2816
FFN_RES_WEIGHT = 0.5
CONV_A_WIDTH = D_MODEL
CONV_A_K = 3
SSM_D_INNER = 2 * D_MODEL
SSM_HEAD_DIM = 64
SSM_HEADS = SSM_D_INNER // SSM_HEAD_DIM
SSM_GROUPS = 4
SSM_STATE = 128
SSM_CONV_K = 4
SSM_CHUNK = 128
SSM_CONV_CH = SSM_D_INNER + 2 * SSM_GROUPS * SSM_STATE
XATTN_HEADS = 4
XATTN_HEAD_DIM = D_MODEL // XATTN_HEADS
XATTN_SCALE = 1.0 / math.sqrt(XATTN_HEAD_DIM)
IN_SIZES = (CONV_A_WIDTH, CONV_A_WIDTH, CONV_A_WIDTH,
            SSM_D_INNER, SSM_CONV_CH, SSM_HEADS,
            D_MODEL, D_MODEL)
D_IN_PROJ = sum(IN_SIZES)
IN_SPLITS = tuple(int(v) for v in np.cumsum(IN_SIZES)[:-1])

kernel_name = "hybrid_shortconv_ssd_gated_macaron"


def rmsnorm(x, g):
    xf = x.astype(jnp.float32)
    y = xf * lax.rsqrt(jnp.mean(xf * xf, axis=-1, keepdims=True) + EPS)
    return (y * g.astype(jnp.float32)).astype(x.dtype)


def swiglu(u, w_gate_up, w_down):
    gate, up = jnp.split(u @ w_gate_up, 2, axis=-1)
    return (jax.nn.silu(gate) * up) @ w_down


def causal_dwconv(x, w):
    k, c = w.shape
    return lax.conv_general_dilated(
        x, w[:, None, :].astype(x.dtype), window_strides=(1,),
        padding=[(k - 1, 0)], dimension_numbers=('NWC', 'WIO', 'NWC'),
        feature_group_count=c)


def short_conv_branch(b_gate, c_gate, v, conv_w, w_out):
    return (b_gate * causal_dwconv(c_gate * v, conv_w)) @ w_out


def ssd_chunked(xh, dt, a, bm, cm):
    b, s, h, p = xh.shape
    g, n = bm.shape[-2:]
    k = h // g
    l = SSM_CHUNK
    c = s // l
    x = (xh.astype(jnp.float32) * dt[..., None]).reshape(b, c, l, g, k, p)
    la = (dt * a).reshape(b, c, l, g, k).transpose(0, 1, 3, 4, 2)
    acs = jnp.cumsum(la, axis=-1)
    bc = bm.astype(jnp.float32).reshape(b, c, l, g, n)
    cc = cm.astype(jnp.float32).reshape(b, c, l, g, n)
    seg = acs[..., :, None] - acs[..., None, :]
    causal = jnp.tril(jnp.ones((l, l), dtype=bool))
    decay = jnp.exp(jnp.where(causal, seg, -jnp.inf))
    cb = jnp.einsum('bclgn,bcsgn->bcgls', cc, bc)
    y_diag = jnp.einsum('bcgls,bcgkls,bcsgkp->bclgkp', cb, decay, x)
    decay_to_end = jnp.exp(acs[..., -1:] - acs)
    states = jnp.einsum('bclgn,bcgkl,bclgkp->bcgkpn', bc, decay_to_end, x)
    chunk_decay = jnp.exp(acs[..., -1])

    def step(carry, inp):
        st, dec = inp
        return carry * dec[..., None, None] + st, carry

    init = jnp.zeros((b, g, k, p, n), jnp.float32)
    _, prev = lax.scan(step, init, (jnp.moveaxis(states, 1, 0), jnp.moveaxis(chunk_decay, 1, 0)))
    prev = jnp.moveaxis(prev, 0, 1)
    y_off = jnp.einsum('bclgn,bcgkpn,bcgkl->bclgkp', cc, prev, jnp.exp(acs))
    return (y_diag + y_off).reshape(b, s, h, p)


def mamba2_branch(z, xbc, dt_raw, conv_w, conv_b, dt_bias, a_log, d_skip, norm_g, w_out):
    xbc = jax.nn.silu(causal_dwconv(xbc, conv_w) + conv_b.astype(xbc.dtype))
    xs, bm, cm = jnp.split(xbc, (SSM_D_INNER, SSM_D_INNER + SSM_GROUPS * SSM_STATE), axis=-1)
    b, s, _ = xs.shape
    xh = xs.reshape(b, s, SSM_HEADS, SSM_HEAD_DIM)
    dt = jax.nn.softplus(dt_raw.astype(jnp.float32) + dt_bias.astype(jnp.float32))
    a = -jnp.exp(a_log.astype(jnp.float32))
    y = ssd_chunked(xh, dt, a,
                    bm.reshape(b, s, SSM_GROUPS, SSM_STATE),
                    cm.reshape(b, s, SSM_GROUPS, SSM_STATE))
    y = y + d_skip.astype(jnp.float32)[:, None] * xh.astype(jnp.float32)
    yg = (y.reshape(b, s, SSM_D_INNER) * jax.nn.silu(z.astype(jnp.float32)))
    yg = yg.reshape(b, s, SSM_GROUPS, SSM_D_INNER // SSM_GROUPS)
    yg = yg * lax.rsqrt(jnp.mean(yg * yg, axis=-1, keepdims=True) + EPS)
    y = (yg.reshape(b, s, SSM_D_INNER) * norm_g.astype(jnp.float32)).astype(z.dtype)
    return y @ w_out


def memory_cross_attention(u, mem_n, w_q, w_kv, w_o):
    b, s, _ = u.shape
    m = mem_n.shape[1]
    q = (u @ w_q).reshape(b, s, XATTN_HEADS, XATTN_HEAD_DIM)
    k, v = jnp.split(mem_n @ w_kv, 2, axis=-1)
    k = k.reshape(b, m, XATTN_HEADS, XATTN_HEAD_DIM)
    v = v.reshape(b, m, XATTN_HEADS, XATTN_HEAD_DIM)
    scores = jnp.einsum('bshd,bmhd->bhsm', q, k).astype(jnp.float32) * XATTN_SCALE
    probs = jax.nn.softmax(scores, axis=-1).astype(v.dtype)
    o = jnp.einsum('bhsm,bmhd->bshd', probs, v).reshape(b, s, D_MODEL)
    return o @ w_o


def _fwd_setup_inputs(seed: int = 0) -> dict:
    key = jax.random.key(seed)
    ks = iter(jax.random.split(key, 40))

    def w(shape, fan_in):
        return jax.random.normal(next(ks), shape, jnp.float32) * (fan_in ** -0.5)

    def gain(shape):
        return 1.0 + 0.02 * jax.random.normal(next(ks), shape, jnp.float32)

    L = DEPTH
    x = jax.random.normal(next(ks), (BATCH, SEQ, D_MODEL), jnp.float32)
    mem = jax.random.normal(next(ks), (BATCH, MEM_LEN, D_MODEL), jnp.float32)
    dt0 = jnp.exp(jax.random.uniform(next(ks), (L, SSM_HEADS), jnp.float32,
                                     math.log(1e-3), math.log(1e-1)))
    dt_bias = dt0 + jnp.log(-jnp.expm1(-dt0))
    a_log = jnp.log(jax.random.uniform(next(ks), (L, SSM_HEADS), jnp.float32, 1.0, 16.0))
    return {
        "x": x,
        "mem": mem,
        "ffn1_norm": gain((L, D_MODEL)),
        "ffn1_w_gate_up": w((L, D_MODEL, 2 * D_FF), D_MODEL),
        "ffn1_w_down": w((L, D_FF, D_MODEL), D_FF),
        "mix_norm": gain((L, D_MODEL)),
        "w_in": w((L, D_MODEL, D_IN_PROJ), D_MODEL),
        "conv_a_w": w((L, CONV_A_K, CONV_A_WIDTH), CONV_A_K),
        "w_out_a": w((L, CONV_A_WIDTH, D_MODEL), CONV_A_WIDTH),
        "ssm_conv_w": w((L, SSM_CONV_K, SSM_CONV_CH), SSM_CONV_K),
        "ssm_conv_b": 0.02 * jax.random.normal(next(ks), (L, SSM_CONV_CH), jnp.float32),
        "ssm_dt_bias": dt_bias,
        "ssm_a_log": a_log,
        "ssm_d": gain((L, SSM_HEADS)),
        "ssm_norm": gain((L, SSM_D_INNER)),
        "w_out_ssm": w((L, SSM_D_INNER, D_MODEL), SSM_D_INNER),
        "w_mix_out": w((L, D_MODEL, D_MODEL), D_MODEL),
        "xattn_norm": gain((L, D_MODEL)),
        "mem_norm": gain((L, D_MODEL)),
        "w_q": w((L, D_MODEL, D_MODEL), D_MODEL),
        "w_kv": w((L, D_MODEL, 2 * D_MODEL), D_MODEL),
        "w_o_x": w((L, D_MODEL, D_MODEL), D_MODEL),
        "ffn2_norm": gain((L, D_MODEL)),
        "ffn2_w_gate_up": w((L, D_MODEL, 2 * D_FF), D_MODEL),
        "ffn2_w_down": w((L, D_FF, D_MODEL), D_FF),
        "final_norm": gain((D_MODEL,)),
    }


def _fwd_reference(x, mem, ffn1_norm, ffn1_w_gate_up, ffn1_w_down, mix_norm, w_in, conv_a_w,
              w_out_a, ssm_conv_w, ssm_conv_b, ssm_dt_bias, ssm_a_log, ssm_d, ssm_norm,
              w_out_ssm, w_mix_out, xattn_norm, mem_norm, w_q, w_kv, w_o_x,
              ffn2_norm, ffn2_w_gate_up, ffn2_w_down, final_norm):
    h = x
    for i in range(DEPTH):
        h = h + FFN_RES_WEIGHT * swiglu(rmsnorm(h, ffn1_norm[i]), ffn1_w_gate_up[i], ffn1_w_down[i])
        u = rmsnorm(h, mix_norm[i])
        proj = u @ w_in[i]
        a_b, a_c, a_v, z, xbc, dt_raw, g_a, g_b = jnp.split(proj, IN_SPLITS, axis=-1)
        y_a = short_conv_branch(a_b, a_c, a_v, conv_a_w[i], w_out_a[i])
        y_b = mamba2_branch(z, xbc, dt_raw, ssm_conv_w[i], ssm_conv_b[i], ssm_dt_bias[i],
                            ssm_a_log[i], ssm_d[i], ssm_norm[i], w_out_ssm[i])
        merged = jax.nn.sigmoid(g_a) * y_a + jax.nn.sigmoid(g_b) * y_b
        h = h + merged @ w_mix_out[i]
        h = h + memory_cross_attention(rmsnorm(h, xattn_norm[i]), rmsnorm(mem, mem_norm[i]),
                                       w_q[i], w_kv[i], w_o_x[i])
        h = h + FFN_RES_WEIGHT * swiglu(rmsnorm(h, ffn2_norm[i]), ffn2_w_gate_up[i], ffn2_w_down[i])
    return rmsnorm(h, final_norm)


import jax as _jax
import jax.numpy as _jnp

TWIN_FORMAT = 'train_step'
FWD_PARAMS = ['x', 'mem', 'ffn1_norm', 'ffn1_w_gate_up', 'ffn1_w_down', 'mix_norm', 'w_in', 'conv_a_w', 'w_out_a', 'ssm_conv_w', 'ssm_conv_b', 'ssm_dt_bias', 'ssm_a_log', 'ssm_d', 'ssm_norm', 'w_out_ssm', 'w_mix_out', 'xattn_norm', 'mem_norm', 'w_q', 'w_kv', 'w_o_x', 'ffn2_norm', 'ffn2_w_gate_up', 'ffn2_w_down', 'final_norm']
TWIN_WEIGHTS = ['ffn1_norm', 'ffn1_w_gate_up', 'ffn1_w_down', 'mix_norm', 'w_in', 'conv_a_w', 'w_out_a', 'ssm_conv_w', 'ssm_conv_b', 'ssm_dt_bias', 'ssm_a_log', 'ssm_d', 'ssm_norm', 'w_out_ssm', 'w_mix_out', 'xattn_norm', 'mem_norm', 'w_q', 'w_kv', 'w_o_x', 'ffn2_norm', 'ffn2_w_gate_up', 'ffn2_w_down', 'final_norm']
TWIN_DIFF_INPUT = 'x'
TWIN_INPUTS = ['x', 'mem', 'ffn1_norm', 'ffn1_w_gate_up', 'ffn1_w_down', 'mix_norm', 'w_in', 'conv_a_w', 'w_out_a', 'ssm_conv_w', 'ssm_conv_b', 'ssm_dt_bias', 'ssm_a_log', 'ssm_d', 'ssm_norm', 'w_out_ssm', 'w_mix_out', 'xattn_norm', 'mem_norm', 'w_q', 'w_kv', 'w_o_x', 'ffn2_norm', 'ffn2_w_gate_up', 'ffn2_w_down', 'final_norm', 'loss_target', 'm_ffn1_norm', 'm_ffn1_w_gate_up', 'm_ffn1_w_down', 'm_mix_norm', 'm_w_in', 'm_conv_a_w', 'm_w_out_a', 'm_ssm_conv_w', 'm_ssm_conv_b', 'm_ssm_dt_bias', 'm_ssm_a_log', 'm_ssm_d', 'm_ssm_norm', 'm_w_out_ssm', 'm_w_mix_out', 'm_xattn_norm', 'm_mem_norm', 'm_w_q', 'm_w_kv', 'm_w_o_x', 'm_ffn2_norm', 'm_ffn2_w_gate_up', 'm_ffn2_w_down', 'm_final_norm', 'v_ffn1_norm', 'v_ffn1_w_gate_up', 'v_ffn1_w_down', 'v_mix_norm', 'v_w_in', 'v_conv_a_w', 'v_w_out_a', 'v_ssm_conv_w', 'v_ssm_conv_b', 'v_ssm_dt_bias', 'v_ssm_a_log', 'v_ssm_d', 'v_ssm_norm', 'v_w_out_ssm', 'v_w_mix_out', 'v_xattn_norm', 'v_mem_norm', 'v_w_q', 'v_w_kv', 'v_w_o_x', 'v_ffn2_norm', 'v_ffn2_w_gate_up', 'v_ffn2_w_down', 'v_final_norm']
TWIN_OUTPUTS = ['loss', 'grad_x', 'grad_ffn1_norm', 'grad_ffn1_w_gate_up', 'grad_ffn1_w_down', 'grad_mix_norm', 'grad_w_in', 'grad_conv_a_w', 'grad_w_out_a', 'grad_ssm_conv_w', 'grad_ssm_conv_b', 'grad_ssm_dt_bias', 'grad_ssm_a_log', 'grad_ssm_d', 'grad_ssm_norm', 'grad_w_out_ssm', 'grad_w_mix_out', 'grad_xattn_norm', 'grad_mem_norm', 'grad_w_q', 'grad_w_kv', 'grad_w_o_x', 'grad_ffn2_norm', 'grad_ffn2_w_gate_up', 'grad_ffn2_w_down', 'grad_final_norm', 'delta_ffn1_norm', 'delta_ffn1_w_gate_up', 'delta_ffn1_w_down', 'delta_mix_norm', 'delta_w_in', 'delta_conv_a_w', 'delta_w_out_a', 'delta_ssm_conv_w', 'delta_ssm_conv_b', 'delta_ssm_dt_bias', 'delta_ssm_a_log', 'delta_ssm_d', 'delta_ssm_norm', 'delta_w_out_ssm', 'delta_w_mix_out', 'delta_xattn_norm', 'delta_mem_norm', 'delta_w_q', 'delta_w_kv', 'delta_w_o_x', 'delta_ffn2_norm', 'delta_ffn2_w_gate_up', 'delta_ffn2_w_down', 'delta_final_norm', 'new_m_ffn1_norm', 'new_m_ffn1_w_gate_up', 'new_m_ffn1_w_down', 'new_m_mix_norm', 'new_m_w_in', 'new_m_conv_a_w', 'new_m_w_out_a', 'new_m_ssm_conv_w', 'new_m_ssm_conv_b', 'new_m_ssm_dt_bias', 'new_m_ssm_a_log', 'new_m_ssm_d', 'new_m_ssm_norm', 'new_m_w_out_ssm', 'new_m_w_mix_out', 'new_m_xattn_norm', 'new_m_mem_norm', 'new_m_w_q', 'new_m_w_kv', 'new_m_w_o_x', 'new_m_ffn2_norm', 'new_m_ffn2_w_gate_up', 'new_m_ffn2_w_down', 'new_m_final_norm', 'new_v_ffn1_norm', 'new_v_ffn1_w_gate_up', 'new_v_ffn1_w_down', 'new_v_mix_norm', 'new_v_w_in', 'new_v_conv_a_w', 'new_v_w_out_a', 'new_v_ssm_conv_w', 'new_v_ssm_conv_b', 'new_v_ssm_dt_bias', 'new_v_ssm_a_log', 'new_v_ssm_d', 'new_v_ssm_norm', 'new_v_w_out_ssm', 'new_v_w_mix_out', 'new_v_xattn_norm', 'new_v_mem_norm', 'new_v_w_q', 'new_v_w_kv', 'new_v_w_o_x', 'new_v_ffn2_norm', 'new_v_ffn2_w_gate_up', 'new_v_ffn2_w_down', 'new_v_final_norm']
TWIN_LEAF_KINDS = {'loss': 'loss', 'grad_x': 'grad_x', 'grad_ffn1_norm': 'grad_w', 'grad_ffn1_w_gate_up': 'grad_w', 'grad_ffn1_w_down': 'grad_w', 'grad_mix_norm': 'grad_w', 'grad_w_in': 'grad_w', 'grad_conv_a_w': 'grad_w', 'grad_w_out_a': 'grad_w', 'grad_ssm_conv_w': 'grad_w', 'grad_ssm_conv_b': 'grad_w', 'grad_ssm_dt_bias': 'grad_w', 'grad_ssm_a_log': 'grad_w', 'grad_ssm_d': 'grad_w', 'grad_ssm_norm': 'grad_w', 'grad_w_out_ssm': 'grad_w', 'grad_w_mix_out': 'grad_w', 'grad_xattn_norm': 'grad_w', 'grad_mem_norm': 'grad_w', 'grad_w_q': 'grad_w', 'grad_w_kv': 'grad_w', 'grad_w_o_x': 'grad_w', 'grad_ffn2_norm': 'grad_w', 'grad_ffn2_w_gate_up': 'grad_w', 'grad_ffn2_w_down': 'grad_w', 'grad_final_norm': 'grad_w', 'delta_ffn1_norm': 'delta_w', 'delta_ffn1_w_gate_up': 'delta_w', 'delta_ffn1_w_down': 'delta_w', 'delta_mix_norm': 'delta_w', 'delta_w_in': 'delta_w', 'delta_conv_a_w': 'delta_w', 'delta_w_out_a': 'delta_w', 'delta_ssm_conv_w': 'delta_w', 'delta_ssm_conv_b': 'delta_w', 'delta_ssm_dt_bias': 'delta_w', 'delta_ssm_a_log': 'delta_w', 'delta_ssm_d': 'delta_w', 'delta_ssm_norm': 'delta_w', 'delta_w_out_ssm': 'delta_w', 'delta_w_mix_out': 'delta_w', 'delta_xattn_norm': 'delta_w', 'delta_mem_norm': 'delta_w', 'delta_w_q': 'delta_w', 'delta_w_kv': 'delta_w', 'delta_w_o_x': 'delta_w', 'delta_ffn2_norm': 'delta_w', 'delta_ffn2_w_gate_up': 'delta_w', 'delta_ffn2_w_down': 'delta_w', 'delta_final_norm': 'delta_w', 'new_m_ffn1_norm': 'new_m', 'new_m_ffn1_w_gate_up': 'new_m', 'new_m_ffn1_w_down': 'new_m', 'new_m_mix_norm': 'new_m', 'new_m_w_in': 'new_m', 'new_m_conv_a_w': 'new_m', 'new_m_w_out_a': 'new_m', 'new_m_ssm_conv_w': 'new_m', 'new_m_ssm_conv_b': 'new_m', 'new_m_ssm_dt_bias': 'new_m', 'new_m_ssm_a_log': 'new_m', 'new_m_ssm_d': 'new_m', 'new_m_ssm_norm': 'new_m', 'new_m_w_out_ssm': 'new_m', 'new_m_w_mix_out': 'new_m', 'new_m_xattn_norm': 'new_m', 'new_m_mem_norm': 'new_m', 'new_m_w_q': 'new_m', 'new_m_w_kv': 'new_m', 'new_m_w_o_x': 'new_m', 'new_m_ffn2_norm': 'new_m', 'new_m_ffn2_w_gate_up': 'new_m', 'new_m_ffn2_w_down': 'new_m', 'new_m_final_norm': 'new_m', 'new_v_ffn1_norm': 'new_v', 'new_v_ffn1_w_gate_up': 'new_v', 'new_v_ffn1_w_down': 'new_v', 'new_v_mix_norm': 'new_v', 'new_v_w_in': 'new_v', 'new_v_conv_a_w': 'new_v', 'new_v_w_out_a': 'new_v', 'new_v_ssm_conv_w': 'new_v', 'new_v_ssm_conv_b': 'new_v', 'new_v_ssm_dt_bias': 'new_v', 'new_v_ssm_a_log': 'new_v', 'new_v_ssm_d': 'new_v', 'new_v_ssm_norm': 'new_v', 'new_v_w_out_ssm': 'new_v', 'new_v_w_mix_out': 'new_v', 'new_v_xattn_norm': 'new_v', 'new_v_mem_norm': 'new_v', 'new_v_w_q': 'new_v', 'new_v_w_kv': 'new_v', 'new_v_w_o_x': 'new_v', 'new_v_ffn2_norm': 'new_v', 'new_v_ffn2_w_gate_up': 'new_v', 'new_v_ffn2_w_down': 'new_v', 'new_v_final_norm': 'new_v'}


def _forward(args):
    return _fwd_reference(*[args[k] for k in FWD_PARAMS])


def _output_shape():
    out = _jax.eval_shape(lambda: _forward(_fwd_setup_inputs(0)))
    return out.shape, out.dtype

N_MICROBATCH = 1
ADAM_LR = 0.001
ADAM_B1 = 0.9
ADAM_B2 = 0.999
ADAM_EPS = 1e-08
ADAM_WD = 0.01
ADAM_STEP = 10
PER_EXAMPLE_BATCH_AXIS = {'x': 0, 'mem': 0, 'loss_target': 0}
SHARED_INPUTS = []
_WEIGHT_DTYPES = {'ffn1_norm': _jnp.float32, 'ffn1_w_gate_up': _jnp.float32, 'ffn1_w_down': _jnp.float32, 'mix_norm': _jnp.float32, 'w_in': _jnp.float32, 'conv_a_w': _jnp.float32, 'w_out_a': _jnp.float32, 'ssm_conv_w': _jnp.float32, 'ssm_conv_b': _jnp.float32, 'ssm_dt_bias': _jnp.float32, 'ssm_a_log': _jnp.float32, 'ssm_d': _jnp.float32, 'ssm_norm': _jnp.float32, 'w_out_ssm': _jnp.float32, 'w_mix_out': _jnp.float32, 'xattn_norm': _jnp.float32, 'mem_norm': _jnp.float32, 'w_q': _jnp.float32, 'w_kv': _jnp.float32, 'w_o_x': _jnp.float32, 'ffn2_norm': _jnp.float32, 'ffn2_w_gate_up': _jnp.float32, 'ffn2_w_down': _jnp.float32, 'final_norm': _jnp.float32}
MOMENT_SCALE = {'ffn1_norm': 1.320900e-01, 'ffn1_w_gate_up': 5.715808e-02, 'ffn1_w_down': 9.323039e-02, 'mix_norm': 2.589943e-01, 'w_in': 7.975669e-02, 'conv_a_w': 1.105252e-01, 'w_out_a': 1.079686e-01, 'ssm_conv_w': 6.594381e-02, 'ssm_conv_b': 9.841822e-02, 'ssm_dt_bias': 1.619875e-01, 'ssm_a_log': 3.155823e-01, 'ssm_d': 3.639525e-01, 'ssm_norm': 8.492871e-02, 'w_out_ssm': 1.077555e-01, 'w_mix_out': 1.525200e-01, 'xattn_norm': 2.247974e-02, 'mem_norm': 3.237157e-02, 'w_q': 2.081131e-02, 'w_kv': 2.095963e-02, 'w_o_x': 2.102550e-02, 'ffn2_norm': 8.304356e-02, 'ffn2_w_gate_up': 3.447349e-02, 'ffn2_w_down': 5.640820e-02, 'final_norm': 6.388493e+01}


def _to_microbatches(a, axis):
    t = _jnp.moveaxis(a, axis, 0)
    t = t.reshape((N_MICROBATCH, t.shape[0] // N_MICROBATCH) + t.shape[1:])
    return _jnp.moveaxis(t, 1, axis + 1)


def setup_inputs(seed: int = 0) -> dict:
    inp = _fwd_setup_inputs(seed)
    key = _jax.random.fold_in(_jax.random.key(seed), 7919)
    shape, _ = _output_shape()
    out = dict(inp)
    out["loss_target"] = _jax.random.normal(_jax.random.fold_in(key, 0), shape, _jnp.float32)
    for i, name in enumerate(TWIN_WEIGHTS):
        w = inp[name].astype(_jnp.float32)
        if MOMENT_SCALE is None:
            s = _jnp.sqrt(_jnp.mean(_jnp.square(w)) + 1e-30)
        else:
            s = MOMENT_SCALE[name]
        km, kv = _jax.random.split(_jax.random.fold_in(key, i + 1))
        out[name] = w
        out["m_" + name] = s * _jax.random.normal(km, w.shape, _jnp.float32)
        out["v_" + name] = (s * s) * _jax.random.uniform(kv, w.shape, _jnp.float32, 0.5, 1.5)
    if N_MICROBATCH > 1:
        for name, axis in PER_EXAMPLE_BATCH_AXIS.items():
            out[name] = _to_microbatches(out[name], axis)
    return {'x': out['x'], 'mem': out['mem'], 'ffn1_norm': out['ffn1_norm'], 'ffn1_w_gate_up': out['ffn1_w_gate_up'], 'ffn1_w_down': out['ffn1_w_down'], 'mix_norm': out['mix_norm'], 'w_in': out['w_in'], 'conv_a_w': out['conv_a_w'], 'w_out_a': out['w_out_a'], 'ssm_conv_w': out['ssm_conv_w'], 'ssm_conv_b': out['ssm_conv_b'], 'ssm_dt_bias': out['ssm_dt_bias'], 'ssm_a_log': out['ssm_a_log'], 'ssm_d': out['ssm_d'], 'ssm_norm': out['ssm_norm'], 'w_out_ssm': out['w_out_ssm'], 'w_mix_out': out['w_mix_out'], 'xattn_norm': out['xattn_norm'], 'mem_norm': out['mem_norm'], 'w_q': out['w_q'], 'w_kv': out['w_kv'], 'w_o_x': out['w_o_x'], 'ffn2_norm': out['ffn2_norm'], 'ffn2_w_gate_up': out['ffn2_w_gate_up'], 'ffn2_w_down': out['ffn2_w_down'], 'final_norm': out['final_norm'], 'loss_target': out['loss_target'], 'm_ffn1_norm': out['m_ffn1_norm'], 'm_ffn1_w_gate_up': out['m_ffn1_w_gate_up'], 'm_ffn1_w_down': out['m_ffn1_w_down'], 'm_mix_norm': out['m_mix_norm'], 'm_w_in': out['m_w_in'], 'm_conv_a_w': out['m_conv_a_w'], 'm_w_out_a': out['m_w_out_a'], 'm_ssm_conv_w': out['m_ssm_conv_w'], 'm_ssm_conv_b': out['m_ssm_conv_b'], 'm_ssm_dt_bias': out['m_ssm_dt_bias'], 'm_ssm_a_log': out['m_ssm_a_log'], 'm_ssm_d': out['m_ssm_d'], 'm_ssm_norm': out['m_ssm_norm'], 'm_w_out_ssm': out['m_w_out_ssm'], 'm_w_mix_out': out['m_w_mix_out'], 'm_xattn_norm': out['m_xattn_norm'], 'm_mem_norm': out['m_mem_norm'], 'm_w_q': out['m_w_q'], 'm_w_kv': out['m_w_kv'], 'm_w_o_x': out['m_w_o_x'], 'm_ffn2_norm': out['m_ffn2_norm'], 'm_ffn2_w_gate_up': out['m_ffn2_w_gate_up'], 'm_ffn2_w_down': out['m_ffn2_w_down'], 'm_final_norm': out['m_final_norm'], 'v_ffn1_norm': out['v_ffn1_norm'], 'v_ffn1_w_gate_up': out['v_ffn1_w_gate_up'], 'v_ffn1_w_down': out['v_ffn1_w_down'], 'v_mix_norm': out['v_mix_norm'], 'v_w_in': out['v_w_in'], 'v_conv_a_w': out['v_conv_a_w'], 'v_w_out_a': out['v_w_out_a'], 'v_ssm_conv_w': out['v_ssm_conv_w'], 'v_ssm_conv_b': out['v_ssm_conv_b'], 'v_ssm_dt_bias': out['v_ssm_dt_bias'], 'v_ssm_a_log': out['v_ssm_a_log'], 'v_ssm_d': out['v_ssm_d'], 'v_ssm_norm': out['v_ssm_norm'], 'v_w_out_ssm': out['v_w_out_ssm'], 'v_w_mix_out': out['v_w_mix_out'], 'v_xattn_norm': out['v_xattn_norm'], 'v_mem_norm': out['v_mem_norm'], 'v_w_q': out['v_w_q'], 'v_w_kv': out['v_w_kv'], 'v_w_o_x': out['v_w_o_x'], 'v_ffn2_norm': out['v_ffn2_norm'], 'v_ffn2_w_gate_up': out['v_ffn2_w_gate_up'], 'v_ffn2_w_down': out['v_ffn2_w_down'], 'v_final_norm': out['v_final_norm']}


def _loss(weights, diff, rest, loss_target):
    with _jax.named_scope("forward"):
        args = {**rest, TWIN_DIFF_INPUT: diff, **{k: w.astype(_WEIGHT_DTYPES[k]) for k, w in weights.items()}}
        y = _forward(args)
    with _jax.named_scope("loss_head"):
        err = _jnp.square(y.astype(_jnp.float32) - loss_target)
        return 0.5 * _jnp.sum(_jnp.mean(err, axis=-1)) if err.ndim else 0.5 * err


def _adamw(w, g, m, v):
    m = ADAM_B1 * m + (1.0 - ADAM_B1) * g
    v = ADAM_B2 * v + (1.0 - ADAM_B2) * _jnp.square(g)
    m_hat = m / (1.0 - ADAM_B1 ** ADAM_STEP)
    v_hat = v / (1.0 - ADAM_B2 ** ADAM_STEP)
    delta = -ADAM_LR * (m_hat / (_jnp.sqrt(v_hat) + ADAM_EPS) + ADAM_WD * w)
    return delta, m, v


def reference(x, mem, ffn1_norm, ffn1_w_gate_up, ffn1_w_down, mix_norm, w_in, conv_a_w, w_out_a, ssm_conv_w, ssm_conv_b, ssm_dt_bias, ssm_a_log, ssm_d, ssm_norm, w_out_ssm, w_mix_out, xattn_norm, mem_norm, w_q, w_kv, w_o_x, ffn2_norm, ffn2_w_gate_up, ffn2_w_down, final_norm, loss_target, m_ffn1_norm, m_ffn1_w_gate_up, m_ffn1_w_down, m_mix_norm, m_w_in, m_conv_a_w, m_w_out_a, m_ssm_conv_w, m_ssm_conv_b, m_ssm_dt_bias, m_ssm_a_log, m_ssm_d, m_ssm_norm, m_w_out_ssm, m_w_mix_out, m_xattn_norm, m_mem_norm, m_w_q, m_w_kv, m_w_o_x, m_ffn2_norm, m_ffn2_w_gate_up, m_ffn2_w_down, m_final_norm, v_ffn1_norm, v_ffn1_w_gate_up, v_ffn1_w_down, v_mix_norm, v_w_in, v_conv_a_w, v_w_out_a, v_ssm_conv_w, v_ssm_conv_b, v_ssm_dt_bias, v_ssm_a_log, v_ssm_d, v_ssm_norm, v_w_out_ssm, v_w_mix_out, v_xattn_norm, v_mem_norm, v_w_q, v_w_kv, v_w_o_x, v_ffn2_norm, v_ffn2_w_gate_up, v_ffn2_w_down, v_final_norm):
    given = dict(x=x, mem=mem, ffn1_norm=ffn1_norm, ffn1_w_gate_up=ffn1_w_gate_up, ffn1_w_down=ffn1_w_down, mix_norm=mix_norm, w_in=w_in, conv_a_w=conv_a_w, w_out_a=w_out_a, ssm_conv_w=ssm_conv_w, ssm_conv_b=ssm_conv_b, ssm_dt_bias=ssm_dt_bias, ssm_a_log=ssm_a_log, ssm_d=ssm_d, ssm_norm=ssm_norm, w_out_ssm=w_out_ssm, w_mix_out=w_mix_out, xattn_norm=xattn_norm, mem_norm=mem_norm, w_q=w_q, w_kv=w_kv, w_o_x=w_o_x, ffn2_norm=ffn2_norm, ffn2_w_gate_up=ffn2_w_gate_up, ffn2_w_down=ffn2_w_down, final_norm=final_norm, loss_target=loss_target, m_ffn1_norm=m_ffn1_norm, m_ffn1_w_gate_up=m_ffn1_w_gate_up, m_ffn1_w_down=m_ffn1_w_down, m_mix_norm=m_mix_norm, m_w_in=m_w_in, m_conv_a_w=m_conv_a_w, m_w_out_a=m_w_out_a, m_ssm_conv_w=m_ssm_conv_w, m_ssm_conv_b=m_ssm_conv_b, m_ssm_dt_bias=m_ssm_dt_bias, m_ssm_a_log=m_ssm_a_log, m_ssm_d=m_ssm_d, m_ssm_norm=m_ssm_norm, m_w_out_ssm=m_w_out_ssm, m_w_mix_out=m_w_mix_out, m_xattn_norm=m_xattn_norm, m_mem_norm=m_mem_norm, m_w_q=m_w_q, m_w_kv=m_w_kv, m_w_o_x=m_w_o_x, m_ffn2_norm=m_ffn2_norm, m_ffn2_w_gate_up=m_ffn2_w_gate_up, m_ffn2_w_down=m_ffn2_w_down, m_final_norm=m_final_norm, v_ffn1_norm=v_ffn1_norm, v_ffn1_w_gate_up=v_ffn1_w_gate_up, v_ffn1_w_down=v_ffn1_w_down, v_mix_norm=v_mix_norm, v_w_in=v_w_in, v_conv_a_w=v_conv_a_w, v_w_out_a=v_w_out_a, v_ssm_conv_w=v_ssm_conv_w, v_ssm_conv_b=v_ssm_conv_b, v_ssm_dt_bias=v_ssm_dt_bias, v_ssm_a_log=v_ssm_a_log, v_ssm_d=v_ssm_d, v_ssm_norm=v_ssm_norm, v_w_out_ssm=v_w_out_ssm, v_w_mix_out=v_w_mix_out, v_xattn_norm=v_xattn_norm, v_mem_norm=v_mem_norm, v_w_q=v_w_q, v_w_kv=v_w_kv, v_w_o_x=v_w_o_x, v_ffn2_norm=v_ffn2_norm, v_ffn2_w_gate_up=v_ffn2_w_gate_up, v_ffn2_w_down=v_ffn2_w_down, v_final_norm=v_final_norm)
    weights = {n: given[n] for n in TWIN_WEIGHTS}
    shared = {n: given[n] for n in SHARED_INPUTS}
    per_example = {n: given[n] for n in ['x', 'mem']}
    grad_fn = _jax.value_and_grad(_loss, argnums=(0, 1))

    def one_microbatch(ex, loss_target):
        ex = dict(ex)
        diff = ex.pop(TWIN_DIFF_INPUT)
        return grad_fn(weights, diff, {**shared, **ex}, loss_target)

    if N_MICROBATCH == 1:
        loss, (grad_w, grad_x) = one_microbatch(per_example, given["loss_target"])
    else:
        def body(carry, xs):
            loss_sum, grad_sum = carry
            l_k, (gw_k, gx_k) = one_microbatch(xs[0], xs[1])
            with _jax.named_scope("update"):
                return (loss_sum + l_k, _jax.tree.map(_jnp.add, grad_sum, gw_k)), gx_k

        init = (_jnp.zeros((), _jnp.float32), _jax.tree.map(_jnp.zeros_like, weights))
        (loss, grad_w), grad_x = _jax.lax.scan(body, init, (per_example, given["loss_target"]))
    with _jax.named_scope("update"):
        delta_w, new_m, new_v = {}, {}, {}
        for n in TWIN_WEIGHTS:
            delta_w[n], new_m[n], new_v[n] = _adamw(weights[n], grad_w[n], given["m_" + n], given["v_" + n])
    return (loss, grad_x, *[grad_w[n] for n in TWIN_WEIGHTS], *[delta_w[n] for n in TWIN_WEIGHTS],
            *[new_m[n] for n in TWIN_WEIGHTS], *[new_v[n] for n in TWIN_WEIGHTS])
```

```python
import functools
import math

import jax
import jax.numpy as jnp
from jax import lax
from jax.experimental import pallas as pl
from jax.experimental.pallas import tpu as pltpu

F32 = jnp.float32
BF16 = jnp.bfloat16
MXU = jnp.bfloat16
HI = lax.Precision.HIGHEST

D = 1024
DFF = 2816
DI = 2048
NH, HD, NG, NS, CH = 32, 64, 4, 128, 128
GW = DI // NG
XH, XD = 4, 256
EPS = 1e-6
NEG = -1e30
O_AB, O_AC, O_AV, O_Z, O_XBC, O_GA, O_GB, O_DT, NPP = 0, 1024, 2048, 3072, 5120, 8192, 9216, 10240, 10368
NIN = 10272
ADAM_LR, ADAM_B1, ADAM_B2, ADAM_EPS, ADAM_WD, ADAM_STEP = 0.001, 0.9, 0.999, 1e-08, 0.01, 10
VMEM_LIMIT = 56 * 1024 * 1024
MESH = pl.DeviceIdType.MESH
CHIP_FLIPS = ((1, 0), (0, 1), (1, 1))


def _cp(*sem):
    return pltpu.CompilerParams(dimension_semantics=sem, vmem_limit_bytes=VMEM_LIMIT)


def _tile(n, pref, align=128):
    if n <= pref:
        return n
    t = (pref // align) * align
    while t >= align:
        if n % t == 0:
            return t
        t -= align
    raise ValueError((n, pref))


def _dot(a, b, dims, prec=None):
    return lax.dot_general(a, b, (dims, ((), ())), preferred_element_type=F32, precision=prec)


def _nn(a, b, prec=None):
    return _dot(a, b, ((1,), (0,)), prec)


def _nt(a, b):
    return _dot(a, b, ((1,), (1,)))


def _tn(a, b):
    return _dot(a, b, ((0,), (0,)))


def _sig(x):
    return jax.nn.sigmoid(x)


def _mm(a, b, mode, *, name, tm=512, tn=512, tk=1024, out_dtype=F32, scale=None, residual=None):
    if mode == "nn":
        (M, K), (K2, N) = a.shape, b.shape
    elif mode == "nt":
        (M, K), (N, K2) = a.shape, b.shape
    else:
        (K, M), (K2, N) = a.shape, b.shape
    assert K == K2, (name, a.shape, b.shape)
    tm, tn, tk = _tile(M, tm), _tile(N, tn), _tile(K, tk)
    nk = K // tk
    if mode == "nn":
        a_spec = pl.BlockSpec((tm, tk), lambda i, j, k: (i, k))
        b_spec = pl.BlockSpec((tk, tn), lambda i, j, k: (k, j))
        dims = ((1,), (0,))
    elif mode == "nt":
        a_spec = pl.BlockSpec((tm, tk), lambda i, j, k: (i, k))
        b_spec = pl.BlockSpec((tn, tk), lambda i, j, k: (j, k))
        dims = ((1,), (1,))
    else:
        a_spec = pl.BlockSpec((tk, tm), lambda i, j, k: (k, i))
        b_spec = pl.BlockSpec((tk, tn), lambda i, j, k: (k, j))
        dims = ((0,), (0,))
    o_spec = pl.BlockSpec((tm, tn), lambda i, j, k: (i, j))
    has_res = residual is not None

    def finish(acc, r_ref, o_ref):
        if scale is not None:
            acc = acc * scale
        if has_res:
            acc = acc + r_ref[...]
        o_ref[...] = acc.astype(out_dtype)

    def body(*refs):
        a_ref, b_ref = refs[0], refs[1]
        r_ref = refs[2] if has_res else None
        o_ref = refs[3] if has_res else refs[2]
        part = _dot(a_ref[...].astype(MXU), b_ref[...].astype(MXU), dims)
        if nk == 1:
            finish(part, r_ref, o_ref)
            return
        acc_ref = refs[-1]
        k = pl.program_id(2)

        @pl.when(k == 0)
        def _():
            acc_ref[...] = part

        @pl.when(k > 0)
        def _():
            acc_ref[...] += part

        @pl.when(k == nk - 1)
        def _():
            finish(acc_ref[...], r_ref, o_ref)

    ins = [a, b] + ([residual] if has_res else [])
    in_specs = [a_spec, b_spec] + ([o_spec] if has_res else [])
    return pl.pallas_call(
        body, grid=(M // tm, N // tn, nk), in_specs=in_specs, out_specs=o_spec,
        out_shape=jax.ShapeDtypeStruct((M, N), out_dtype),
        scratch_shapes=[pltpu.VMEM((tm, tn), F32)] if nk > 1 else [],
        compiler_params=_cp("parallel", "parallel", "arbitrary"), name=name)(*ins)


def _norm_fwd(x, g, *, name):
    T, d = x.shape
    tr = _tile(T, 512, 8)

    def body(x_ref, g_ref, o_ref):
        xv = x_ref[...]
        r = lax.rsqrt(jnp.mean(xv * xv, axis=-1, keepdims=True) + EPS)
        o_ref[...] = (xv * r * g_ref[...]).astype(BF16)

    return pl.pallas_call(
        body, grid=(T // tr,),
        in_specs=[pl.BlockSpec((tr, d), lambda i: (i, 0)), pl.BlockSpec((1, d), lambda i: (0, 0))],
        out_specs=pl.BlockSpec((tr, d), lambda i: (i, 0)),
        out_shape=jax.ShapeDtypeStruct((T, d), BF16), compiler_params=_cp("parallel"), name=name)(x, g)


def _norm_bwd(x, g, dn, dres, *, name):
    T, d = x.shape
    tr = _tile(T, 512, 8)
    has_res = dres is not None

    def body(*refs):
        x_ref, g_ref, dn_ref = refs[:3]
        dr_ref = refs[3] if has_res else None
        dx_ref, dg_ref = refs[-2], refs[-1]

        @pl.when(pl.program_id(0) == 0)
        def _():
            dg_ref[...] = jnp.zeros_like(dg_ref)

        xv = x_ref[...]
        dnv = dn_ref[...].astype(F32)
        r = lax.rsqrt(jnp.mean(xv * xv, axis=-1, keepdims=True) + EPS)
        xh = xv * r
        dg_ref[...] += jnp.sum(dnv * xh, axis=0, keepdims=True)
        dxh = dnv * g_ref[...]
        dx = r * (dxh - xh * jnp.mean(dxh * xh, axis=-1, keepdims=True))
        if has_res:
            dx = dx + dr_ref[...]
        dx_ref[...] = dx

    row = pl.BlockSpec((tr, d), lambda i: (i, 0))
    vec = pl.BlockSpec((1, d), lambda i: (0, 0))
    ins = [x, g, dn] + ([dres] if has_res else [])
    return pl.pallas_call(
        body, grid=(T // tr,), in_specs=[row, vec, row] + ([row] if has_res else []),
        out_specs=[row, vec],
        out_shape=[jax.ShapeDtypeStruct((T, d), F32), jax.ShapeDtypeStruct((1, d), F32)],
        compiler_params=_cp("arbitrary"), name=name)(*ins)


def _final_loss(h, g, target, *, name):
    T, d = h.shape
    tr = _tile(T, 512, 8)

    def body(h_ref, g_ref, t_ref, l_ref, dh_ref, dg_ref):
        @pl.when(pl.program_id(0) == 0)
        def _():
            l_ref[...] = jnp.zeros_like(l_ref)
            dg_ref[...] = jnp.zeros_like(dg_ref)

        xv = h_ref[...]
        r = lax.rsqrt(jnp.mean(xv * xv, axis=-1, keepdims=True) + EPS)
        xh = xv * r
        e = xh * g_ref[...] - t_ref[...]
        l_ref[...] += jnp.sum(e * e, axis=0, keepdims=True)
        dy = e * (1.0 / d)
        dg_ref[...] += jnp.sum(dy * xh, axis=0, keepdims=True)
        dxh = dy * g_ref[...]
        dh_ref[...] = r * (dxh - xh * jnp.mean(dxh * xh, axis=-1, keepdims=True))

    row = pl.BlockSpec((tr, d), lambda i: (i, 0))
    vec = pl.BlockSpec((1, d), lambda i: (0, 0))
    return pl.pallas_call(
        body, grid=(T // tr,), in_specs=[row, vec, row], out_specs=[vec, row, vec],
        out_shape=[jax.ShapeDtypeStruct((1, d), F32), jax.ShapeDtypeStruct((T, d), F32),
                   jax.ShapeDtypeStruct((1, d), F32)],
        compiler_params=_cp("arbitrary"), name=name)(h, g, target)


def _swiglu_fwd(gu, *, name):
    T, two_f = gu.shape
    f = two_f // 2
    tr, tc = _tile(T, 512, 8), _tile(f, 256)
    nf = f // tc

    def body(g_ref, u_ref, o_ref):
        gv = g_ref[...]
        o_ref[...] = (gv * _sig(gv) * u_ref[...]).astype(BF16)

    return pl.pallas_call(
        body, grid=(T // tr, nf),
        in_specs=[pl.BlockSpec((tr, tc), lambda i, j: (i, j)), pl.BlockSpec((tr, tc), lambda i, j: (i, j + nf))],
        out_specs=pl.BlockSpec((tr, tc), lambda i, j: (i, j)),
        out_shape=jax.ShapeDtypeStruct((T, f), BF16), compiler_params=_cp("parallel", "parallel"), name=name)(gu, gu)


def _swiglu_bwd(gu, da, *, name):
    T, two_f = gu.shape
    f = two_f // 2
    tr, tc = _tile(T, 512, 8), _tile(f, 256)
    nf = f // tc

    def body(g_ref, u_ref, da_ref, o_ref):
        j = pl.program_id(1)
        gv, dav = g_ref[...], da_ref[...]
        s = _sig(gv)
        d_gate = dav * u_ref[...] * (s * (1.0 + gv * (1.0 - s)))
        d_up = dav * (gv * s)
        o_ref[...] = jnp.where(j < nf, d_gate, d_up).astype(BF16)

    return pl.pallas_call(
        body, grid=(T // tr, 2 * nf),
        in_specs=[pl.BlockSpec((tr, tc), lambda i, j: (i, j % nf)),
                  pl.BlockSpec((tr, tc), lambda i, j: (i, j % nf + nf)),
                  pl.BlockSpec((tr, tc), lambda i, j: (i, j % nf))],
        out_specs=pl.BlockSpec((tr, tc), lambda i, j: (i, j)),
        out_shape=jax.ShapeDtypeStruct((T, two_f), BF16), compiler_params=_cp("parallel", "parallel"),
        name=name)(gu, gu, da)


def _shift_down(x, d, ri):
    if d == 0:
        return x
    return jnp.where(ri >= d, pltpu.roll(x, d, 0), 0.0)


def _shift_up(x, d, ri):
    if d == 0:
        return x
    s = x.shape[0]
    return jnp.where(ri < s - d, pltpu.roll(x, s - d, 0), 0.0)


def _conv_taps(x, w_ref, ktaps, ri):
    acc = None
    for k in range(ktaps):
        t = w_ref[k:k + 1, :] * _shift_down(x, ktaps - 1 - k, ri)
        acc = t if acc is None else acc + t
    return acc


def _conv_a_fwd(pp, w8, bl, s, *, name):
    tc = 256
    nb = D // tc

    def body(b_ref, c_ref, v_ref, w_ref, o_ref):
        ri = lax.broadcasted_iota(jnp.int32, (s, tc), 0)
        cv = c_ref[...] * v_ref[...]
        o_ref[...] = (b_ref[...] * _conv_taps(cv, w_ref, 3, ri)).astype(BF16)

    def col(off):
        return pl.BlockSpec((s, tc), lambda b, j: (b, off // tc + j))

    return pl.pallas_call(
        body, grid=(bl, nb),
        in_specs=[col(O_AB), col(O_AC), col(O_AV), pl.BlockSpec((8, tc), lambda b, j: (0, j))],
        out_specs=pl.BlockSpec((s, tc), lambda b, j: (b, j)),
        out_shape=jax.ShapeDtypeStruct((bl * s, D), BF16), compiler_params=_cp("parallel", "parallel"),
        name=name)(pp, pp, pp, w8)


def _conv_a_bwd(pp, w8, dya, bl, s, *, name):
    tc = 256
    nb = D // tc

    def body(b_ref, c_ref, v_ref, w_ref, dy_ref, db_ref, dc_ref, dv_ref, dw_ref):
        @pl.when(pl.program_id(1) == 0)
        def _():
            dw_ref[...] = jnp.zeros_like(dw_ref)

        ri = lax.broadcasted_iota(jnp.int32, (s, tc), 0)
        cvec, vvec, dy = c_ref[...], v_ref[...], dy_ref[...]
        cv = cvec * vvec
        db_ref[...] = (dy * _conv_taps(cv, w_ref, 3, ri)).astype(BF16)
        dconv = dy * b_ref[...]
        dcv = None
        for k in range(3):
            t = w_ref[k:k + 1, :] * _shift_up(dconv, 2 - k, ri)
            dcv = t if dcv is None else dcv + t
            dw_ref[k:k + 1, :] += jnp.sum(dconv * _shift_down(cv, 2 - k, ri), axis=0, keepdims=True)
        dc_ref[...] = (dcv * vvec).astype(BF16)
        dv_ref[...] = (dcv * cvec).astype(BF16)

    def col(off):
        return pl.BlockSpec((s, tc), lambda j, b: (b, off // tc + j))

    plain = pl.BlockSpec((s, tc), lambda j, b: (b, j))
    wspec = pl.BlockSpec((8, tc), lambda j, b: (0, j))
    act = jax.ShapeDtypeStruct((bl * s, D), BF16)
    return pl.pallas_call(
        body, grid=(nb, bl), in_specs=[col(O_AB), col(O_AC), col(O_AV), wspec, plain],
        out_specs=[plain, plain, plain, wspec],
        out_shape=[act, act, act, jax.ShapeDtypeStruct((8, D), F32)],
        compiler_params=_cp("parallel", "arbitrary"), name=name)(pp, pp, pp, w8, dya)


def _conv_ssm_fwd(pp, w8, bias, bl, s, *, name):
    tc = 256
    width = DI + 2 * NG * NS
    nb = width // tc

    def body(x_ref, w_ref, b_ref, o_ref):
        ri = lax.broadcasted_iota(jnp.int32, (s, tc), 0)
        pre = _conv_taps(x_ref[...], w_ref, 4, ri) + b_ref[...]
        o_ref[...] = pre * _sig(pre)

    return pl.pallas_call(
        body, grid=(bl, nb),
        in_specs=[pl.BlockSpec((s, tc), lambda b, j: (b, O_XBC // tc + j)),
                  pl.BlockSpec((8, tc), lambda b, j: (0, j)), pl.BlockSpec((1, tc), lambda b, j: (0, j))],
        out_specs=pl.BlockSpec((s, tc), lambda b, j: (b, j)),
        out_shape=jax.ShapeDtypeStruct((bl * s, width), F32), compiler_params=_cp("parallel", "parallel"),
        name=name)(pp, w8, bias)


def _conv_ssm_bwd(pp, w8, bias, dxc, ch_off, bl, s, *, name):
    n = dxc.shape[1]
    tc = 256
    nb = n // tc
    o0 = ch_off // tc

    def body(x_ref, w_ref, b_ref, d_ref, dx_ref, dw_ref, db_ref):
        @pl.when(pl.program_id(1) == 0)
        def _():
            dw_ref[...] = jnp.zeros_like(dw_ref)
            db_ref[...] = jnp.zeros_like(db_ref)

        ri = lax.broadcasted_iota(jnp.int32, (s, tc), 0)
        xv = x_ref[...]
        pre = _conv_taps(xv, w_ref, 4, ri) + b_ref[...]
        sg = _sig(pre)
        dpre = d_ref[...] * (sg * (1.0 + pre * (1.0 - sg)))
        db_ref[...] += jnp.sum(dpre, axis=0, keepdims=True)
        dx = None
        for k in range(4):
            t = w_ref[k:k + 1, :] * _shift_up(dpre, 3 - k, ri)
            dx = t if dx is None else dx + t
            dw_ref[k:k + 1, :] += jnp.sum(dpre * _shift_down(xv, 3 - k, ri), axis=0, keepdims=True)
        dx_ref[...] = dx.astype(BF16)

    plain = pl.BlockSpec((s, tc), lambda j, b: (b, j))
    return pl.pallas_call(
        body, grid=(nb, bl),
        in_specs=[pl.BlockSpec((s, tc), lambda j, b: (b, O_XBC // tc + o0 + j)),
                  pl.BlockSpec((8, tc), lambda j, b: (0, o0 + j)), pl.BlockSpec((1, tc), lambda j, b: (0, o0 + j)),
                  plain],
        out_specs=[plain, pl.BlockSpec((8, tc), lambda j, b: (0, j)), pl.BlockSpec((1, tc), lambda j, b: (0, j))],
        out_shape=[jax.ShapeDtypeStruct((bl * s, n), BF16), jax.ShapeDtypeStruct((8, n), F32),
                   jax.ShapeDtypeStruct((1, n), F32)],
        compiler_params=_cp("parallel", "arbitrary"), name=name)(pp, w8, bias, dxc)


def _softplus(x):
    return jnp.maximum(x, 0.0) + jnp.log1p(jnp.exp(-jnp.abs(x)))


def _dt_fwd(pp, bias128, *, name):
    T = pp.shape[0]
    tr = _tile(T, 1024, 8)

    def body(x_ref, b_ref, o_ref):
        lane = lax.broadcasted_iota(jnp.int32, (tr, 128), 1)
        o_ref[...] = jnp.where(lane < NH, _softplus(x_ref[...] + b_ref[...]), 0.0)

    return pl.pallas_call(
        body, grid=(T // tr,),
        in_specs=[pl.BlockSpec((tr, 128), lambda i: (i, O_DT // 128)), pl.BlockSpec((1, 128), lambda i: (0, 0))],
        out_specs=pl.BlockSpec((tr, 128), lambda i: (i, 0)),
        out_shape=jax.ShapeDtypeStruct((T, 128), F32), compiler_params=_cp("parallel"), name=name)(pp, bias128)


def _dt_bwd(pp, bias128, ddt, *, name):
    T = pp.shape[0]
    tr = _tile(T, 1024, 8)

    def body(x_ref, b_ref, d_ref, o_ref, db_ref):
        @pl.when(pl.program_id(0) == 0)
        def _():
            db_ref[...] = jnp.zeros_like(db_ref)

        lane = lax.broadcasted_iota(jnp.int32, (tr, 128), 1)
        dr = jnp.where(lane < NH, d_ref[...] * _sig(x_ref[...] + b_ref[...]), 0.0)
        db_ref[...] += jnp.sum(dr, axis=0, keepdims=True)
        o_ref[...] = dr.astype(BF16)

    row = pl.BlockSpec((tr, 128), lambda i: (i, 0))
    vec = pl.BlockSpec((1, 128), lambda i: (0, 0))
    return pl.pallas_call(
        body, grid=(T // tr,),
        in_specs=[pl.BlockSpec((tr, 128), lambda i: (i, O_DT // 128)), vec, row],
        out_specs=[row, vec],
        out_shape=[jax.ShapeDtypeStruct((T, 128), BF16), jax.ShapeDtypeStruct((1, 128), F32)],
        compiler_params=_cp("arbitrary"), name=name)(pp, bias128, ddt)


def _ssd_common(dt, dtt, arow, acol):
    ri = lax.broadcasted_iota(jnp.int32, (CH, CH), 0)
    ci = lax.broadcasted_iota(jnp.int32, (CH, CH), 1)
    tril = ri >= ci
    triu = ri <= ci
    acs_col = _nn(tril.astype(F32), dt * arow, HI)
    acs_row = _nn(dtt * acol, triu.astype(F32), HI)
    return tril, triu, acs_col, acs_row


def _pair_terms(q, dt, acs_col, acs_row, tril, triu, lo, with_t):
    ha, hb = 2 * q, 2 * q + 1
    col_a, col_b = acs_col[:, ha:ha + 1], acs_col[:, hb:hb + 1]
    row_a, row_b = acs_row[ha:ha + 1, :], acs_row[hb:hb + 1, :]
    last_a, last_b = acs_col[CH - 1:CH, ha:ha + 1], acs_col[CH - 1:CH, hb:hb + 1]
    out = dict(
        dtsel=jnp.where(lo, dt[:, ha:ha + 1], dt[:, hb:hb + 1]),
        d_a=jnp.exp(jnp.where(tril, col_a - row_a, NEG)), d_b=jnp.exp(jnp.where(tril, col_b - row_b, NEG)),
        esel=jnp.where(lo, jnp.exp(col_a), jnp.exp(col_b)),
        fsel=jnp.where(lo, jnp.exp(last_a - col_a), jnp.exp(last_b - col_b)),
        g_a=jnp.exp(last_a), g_b=jnp.exp(last_b))
    if with_t:
        out["dt_a"] = jnp.exp(jnp.where(triu, row_a - col_a, NEG))
        out["dt_b"] = jnp.exp(jnp.where(triu, row_b - col_b, NEG))
    return out


def _ssd_fwd(xc, pp, dtg, dtt, arow, acol, dexp, ng, bl, s, *, name):
    nc = s // CH
    T = bl * s

    def body(xs_ref, bm_ref, cm_ref, z_ref, dt_ref, dtt_ref, arow_ref, acol_ref, dexp_ref, ng_ref,
             y_ref, yn_ref, prev_ref, st_ref):
        @pl.when(pl.program_id(2) == 0)
        def _():
            st_ref[...] = jnp.zeros_like(st_ref)

        dt = dt_ref[...]
        tril, triu, acs_col, acs_row = _ssd_common(dt, dtt_ref[...], -jnp.exp(arow_ref[...]), -jnp.exp(acol_ref[...]))
        bm, cm = bm_ref[...].astype(MXU), cm_ref[...].astype(MXU)
        cb = _nt(cm, bm)
        lo = lax.broadcasted_iota(jnp.int32, (CH, 128), 1) < HD
        sub_lo = lax.broadcasted_iota(jnp.int32, (128, NS), 0) < HD
        for q in range(4):
            t = _pair_terms(q, dt, acs_col, acs_row, tril, triu, lo, False)
            x = xs_ref[:, 128 * q:128 * (q + 1)]
            xd = x * t["dtsel"]
            y = (_nn((cb * t["d_a"]).astype(MXU), jnp.where(lo, xd, 0.0).astype(MXU))
                 + _nn((cb * t["d_b"]).astype(MXU), jnp.where(lo, 0.0, xd).astype(MXU)))
            prev = st_ref[q]
            prev_ref[q] = prev
            y = y + t["esel"] * _nt(cm, prev.astype(MXU))
            st_ref[q] = prev * jnp.where(sub_lo, t["g_a"], t["g_b"]) + _tn((xd * t["fsel"]).astype(MXU), bm)
            y_ref[:, 128 * q:128 * (q + 1)] = y + dexp_ref[:, 128 * q:128 * (q + 1)] * x
        zv = z_ref[...]
        yg = y_ref[...] * (zv * _sig(zv))
        r = lax.rsqrt(jnp.mean(yg * yg, axis=-1, keepdims=True) + EPS)
        yn_ref[...] = (yg * r * ng_ref[...]).astype(BF16)

    def row(width, off_blocks):
        return pl.BlockSpec((CH, width), lambda g, b, c: (b * nc + c, off_blocks + g))

    return pl.pallas_call(
        body, grid=(NG, bl, nc),
        in_specs=[row(GW, 0), row(NS, DI // NS), row(NS, DI // NS + NG), row(GW, O_Z // GW), row(128, 0),
                  pl.BlockSpec((None, 8, CH), lambda g, b, c: (g, 0, b * nc + c)),
                  pl.BlockSpec((1, 128), lambda g, b, c: (0, g)),
                  pl.BlockSpec((None, 8, 1), lambda g, b, c: (g, 0, 0)),
                  pl.BlockSpec((1, GW), lambda g, b, c: (0, g)), pl.BlockSpec((1, GW), lambda g, b, c: (0, g))],
        out_specs=[row(GW, 0), row(GW, 0),
                   pl.BlockSpec((None, 4, 128, NS), lambda g, b, c: (b * nc + c, g, 0, 0))],
        out_shape=[jax.ShapeDtypeStruct((T, DI), F32), jax.ShapeDtypeStruct((T, DI), BF16),
                   jax.ShapeDtypeStruct((bl * nc, 16, 128, NS), F32)],
        scratch_shapes=[pltpu.VMEM((4, 128, NS), F32)],
        compiler_params=_cp("parallel", "parallel", "arbitrary"), name=name,
    )(xc, xc, xc, pp, dtg, dtt, arow, acol, dexp, ng)


def _ssd_bwd(dyn, y, xc, pp, dtg, dtt, arow, acol, dexp, ng, prev, bl, s, *, name):
    nc = s // CH
    T = bl * s

    def rsum(v):
        return jnp.sum(v, axis=1, keepdims=True)

    def asum(v):
        return jnp.sum(rsum(v), axis=0, keepdims=True)

    def body(dyn_ref, y_ref, xs_ref, bm_ref, cm_ref, z_ref, dt_ref, dtt_ref, arow_ref, acol_ref, dexp_ref, ng_ref,
             prev_ref, dz_ref, dxs_ref, db_ref, dc_ref, ddt_ref, dng_ref, dd_ref, dal_ref, dst_ref):
        @pl.when((pl.program_id(1) == 0) & (pl.program_id(2) == 0))
        def _():
            dng_ref[...] = jnp.zeros_like(dng_ref)
            dd_ref[...] = jnp.zeros_like(dd_ref)
            dal_ref[...] = jnp.zeros_like(dal_ref)

        @pl.when(pl.program_id(2) == 0)
        def _():
            dst_ref[...] = jnp.zeros_like(dst_ref)

        yv, zv = y_ref[...], z_ref[...]
        sz = _sig(zv)
        silu = zv * sz
        yg = yv * silu
        r = lax.rsqrt(jnp.mean(yg * yg, axis=-1, keepdims=True) + EPS)
        yh = yg * r
        dynv = dyn_ref[...]
        dng_ref[...] += jnp.sum(dynv * yh, axis=0, keepdims=True)
        dyh = dynv * ng_ref[...]
        dyg = r * (dyh - yh * jnp.mean(dyh * yh, axis=-1, keepdims=True))
        dz_ref[...] = (dyg * yv * (sz * (1.0 + zv * (1.0 - sz)))).astype(BF16)
        dxs_ref[...] = dyg * silu
        dd_ref[...] += jnp.sum(dxs_ref[...] * xs_ref[...], axis=0, keepdims=True)

        dt = dt_ref[...]
        arow_v = -jnp.exp(arow_ref[...])
        tril, triu, acs_col, acs_row = _ssd_common(dt, dtt_ref[...], arow_v, -jnp.exp(acol_ref[...]))
        bm, cm = bm_ref[...].astype(MXU), cm_ref[...].astype(MXU)
        cb, cbt = _nt(cm, bm), _nt(bm, cm)
        lane = lax.broadcasted_iota(jnp.int32, (CH, 128), 1)
        rowi = lax.broadcasted_iota(jnp.int32, (CH, 128), 0)
        lo = lane < HD
        sub_lo = lax.broadcasted_iota(jnp.int32, (128, NS), 0) < HD
        dcb = jnp.zeros((CH, CH), F32)
        dcbt = jnp.zeros((CH, CH), F32)
        dc_acc = jnp.zeros((CH, NS), F32)
        db_acc = jnp.zeros((CH, NS), F32)
        dacs = jnp.zeros((CH, 128), F32)
        ddtx = jnp.zeros((CH, 128), F32)
        for q in range(4):
            ha, hb = 2 * q, 2 * q + 1
            t = _pair_terms(q, dt, acs_col, acs_row, tril, triu, lo, True)
            x = xs_ref[:, 128 * q:128 * (q + 1)]
            dy = dxs_ref[:, 128 * q:128 * (q + 1)]
            xd = x * t["dtsel"]
            xd_m, dy_m = xd.astype(MXU), dy.astype(MXU)
            xd_lo, xd_hi = jnp.where(lo, xd, 0.0).astype(MXU), jnp.where(lo, 0.0, xd).astype(MXU)
            dy_lo, dy_hi = jnp.where(lo, dy, 0.0).astype(MXU), jnp.where(lo, 0.0, dy).astype(MXU)
            m_a, m_b = cb * t["d_a"], cb * t["d_b"]
            mt_a, mt_b = cbt * t["dt_a"], cbt * t["dt_b"]
            prev = prev_ref[q]
            dnext = dst_ref[q]
            prev_m, dnext_m = prev.astype(MXU), dnext.astype(MXU)
            bds = _nt(bm, dnext_m)
            dxd = _nn(mt_a.astype(MXU), dy_lo) + _nn(mt_b.astype(MXU), dy_hi) + t["fsel"] * bds
            yoff = t["esel"] * _nt(cm, prev_m)
            dye = dy * t["esel"]
            dye_m = dye.astype(MXU)
            xdf_m = (xd * t["fsel"]).astype(MXU)
            dst_ref[q] = dnext * jnp.where(sub_lo, t["g_a"], t["g_b"]) + _tn(dye_m, cm)
            dm_a, dm_b = _nt(dy_lo, xd_m), _nt(dy_hi, xd_m)
            dmt_a, dmt_b = _nt(xd_lo, dy_m), _nt(xd_hi, dy_m)
            dcb = dcb + dm_a * t["d_a"] + dm_b * t["d_b"]
            dcbt = dcbt + dmt_a * t["dt_a"] + dmt_b * t["dt_b"]
            tyf = dy * yoff - t["fsel"] * xd * bds
            tf = t["fsel"] * xd * bds
            ra = rsum(dm_a * m_a) - rsum(dmt_a * mt_a) + rsum(jnp.where(lo, tyf, 0.0))
            rb = rsum(dm_b * m_b) - rsum(dmt_b * mt_b) + rsum(jnp.where(lo, 0.0, tyf))
            dpp = dnext * prev
            ea = asum(jnp.where(lo, tf, 0.0)) + t["g_a"] * asum(jnp.where(sub_lo, dpp, 0.0))
            eb = asum(jnp.where(lo, 0.0, tf)) + t["g_b"] * asum(jnp.where(sub_lo, 0.0, dpp))
            is_last = rowi == CH - 1
            dacs = (dacs + jnp.where(lane == ha, ra + jnp.where(is_last, ea, 0.0), 0.0)
                    + jnp.where(lane == hb, rb + jnp.where(is_last, eb, 0.0), 0.0))
            tx = dxd * x
            ddtx = (ddtx + jnp.where(lane == ha, rsum(jnp.where(lo, tx, 0.0)), 0.0)
                    + jnp.where(lane == hb, rsum(jnp.where(lo, 0.0, tx)), 0.0))
            dxs_ref[:, 128 * q:128 * (q + 1)] = dxd * t["dtsel"] + dexp_ref[:, 128 * q:128 * (q + 1)] * dy
            dc_acc = dc_acc + _nn(dye_m, prev_m)
            db_acc = db_acc + _nn(xdf_m, dnext_m)
        dc_ref[...] = dc_acc + _nn(dcb.astype(MXU), bm)
        db_ref[...] = db_acc + _nn(dcbt.astype(MXU), cm)
        dla = _nn(triu.astype(F32), dacs, HI)
        ddt_ref[...] = dla * arow_v + ddtx
        dal_ref[...] += jnp.sum(dla * dt, axis=0, keepdims=True) * arow_v

    def row(width, off_blocks):
        return pl.BlockSpec((CH, width), lambda g, b, c: (b * nc + nc - 1 - c, off_blocks + g))

    gvec = pl.BlockSpec((1, GW), lambda g, b, c: (0, g))
    hvec = pl.BlockSpec((1, 128), lambda g, b, c: (0, g))
    return pl.pallas_call(
        body, grid=(NG, bl, nc),
        in_specs=[row(GW, 0), row(GW, 0), row(GW, 0), row(NS, DI // NS), row(NS, DI // NS + NG), row(GW, O_Z // GW),
                  row(128, 0), pl.BlockSpec((None, 8, CH), lambda g, b, c: (g, 0, b * nc + nc - 1 - c)),
                  hvec, pl.BlockSpec((None, 8, 1), lambda g, b, c: (g, 0, 0)), gvec, gvec,
                  pl.BlockSpec((None, 4, 128, NS), lambda g, b, c: (b * nc + nc - 1 - c, g, 0, 0))],
        out_specs=[row(GW, 0), row(GW, 0), row(NS, 0), row(NS, 0), row(128, 0), gvec, gvec, hvec],
        out_shape=[jax.ShapeDtypeStruct((T, DI), BF16), jax.ShapeDtypeStruct((T, DI), F32),
                   jax.ShapeDtypeStruct((T, NG * NS), F32), jax.ShapeDtypeStruct((T, NG * NS), F32),
                   jax.ShapeDtypeStruct((T, NG * 128), F32), jax.ShapeDtypeStruct((1, DI), F32),
                   jax.ShapeDtypeStruct((1, DI), F32), jax.ShapeDtypeStruct((1, NG * 128), F32)],
        scratch_shapes=[pltpu.VMEM((4, 128, NS), F32)],
        compiler_params=_cp("arbitrary", "arbitrary", "arbitrary"), name=name,
    )(dyn, y, xc, xc, xc, pp, dtg, dtt, arow, acol, dexp, ng, prev)


def _merge_fwd(pp, ya, yb, *, name):
    T = ya.shape[0]
    tr = _tile(T, 512, 8)

    def body(ga_ref, gb_ref, ya_ref, yb_ref, o_ref):
        o_ref[...] = (_sig(ga_ref[...]) * ya_ref[...] + _sig(gb_ref[...]) * yb_ref[...]).astype(BF16)

    row = pl.BlockSpec((tr, D), lambda i: (i, 0))
    return pl.pallas_call(
        body, grid=(T // tr,),
        in_specs=[pl.BlockSpec((tr, D), lambda i: (i, O_GA // D)), pl.BlockSpec((tr, D), lambda i: (i, O_GB // D)),
                  row, row],
        out_specs=row, out_shape=jax.ShapeDtypeStruct((T, D), BF16), compiler_params=_cp("parallel"),
        name=name)(pp, pp, ya, yb)


def _merge_bwd(pp, ya, yb, dm, *, name):
    T = ya.shape[0]
    tr = _tile(T, 512, 8)

    def body(ga_ref, gb_ref, ya_ref, yb_ref, dm_ref, dya_ref, dyb_ref, dga_ref, dgb_ref):
        sa, sb, dmv = _sig(ga_ref[...]), _sig(gb_ref[...]), dm_ref[...]
        dya_ref[...] = (dmv * sa).astype(BF16)
        dyb_ref[...] = (dmv * sb).astype(BF16)
        dga_ref[...] = (dmv * ya_ref[...] * (sa * (1.0 - sa))).astype(BF16)
        dgb_ref[...] = (dmv * yb_ref[...] * (sb * (1.0 - sb))).astype(BF16)

    row = pl.BlockSpec((tr, D), lambda i: (i, 0))
    act = jax.ShapeDtypeStruct((T, D), BF16)
    return pl.pallas_call(
        body, grid=(T // tr,),
        in_specs=[pl.BlockSpec((tr, D), lambda i: (i, O_GA // D)), pl.BlockSpec((tr, D), lambda i: (i, O_GB // D)),
                  row, row, row],
        out_specs=[row, row, row, row], out_shape=[act, act, act, act], compiler_params=_cp("parallel"),
        name=name)(pp, pp, ya, yb, dm)


def _softmax_rows(sc):
    e = jnp.exp(sc - jnp.max(sc, axis=-1, keepdims=True))
    return e / jnp.sum(e, axis=-1, keepdims=True)


def _attn_fwd(q, kv, bl, s, *, name):
    m = kv.shape[0] // bl
    tq = _tile(s, 512)
    nq = s // tq
    scale = 1.0 / math.sqrt(XD)

    def body(q_ref, k_ref, v_ref, o_ref):
        p = _softmax_rows(_nt(q_ref[...], k_ref[...]) * scale)
        o_ref[...] = _nn(p.astype(MXU), v_ref[...]).astype(BF16)

    qspec = pl.BlockSpec((tq, XD), lambda b, h, i: (b * nq + i, h))
    return pl.pallas_call(
        body, grid=(bl, XH, nq),
        in_specs=[qspec, pl.BlockSpec((m, XD), lambda b, h, i: (b, h)),
                  pl.BlockSpec((m, XD), lambda b, h, i: (b, XH + h))],
        out_specs=qspec, out_shape=jax.ShapeDtypeStruct((bl * s, D), BF16),
        compiler_params=_cp("parallel", "parallel", "parallel"), name=name)(q, kv, kv)


def _attn_bwd(q, kv, do, bl, s, *, name):
    m = kv.shape[0] // bl
    tq = _tile(s, 512)
    nq = s // tq
    scale = 1.0 / math.sqrt(XD)

    def body(q_ref, k_ref, v_ref, do_ref, dq_ref, dk_ref, dv_ref):
        @pl.when(pl.program_id(2) == 0)
        def _():
            dk_ref[...] = jnp.zeros_like(dk_ref)
            dv_ref[...] = jnp.zeros_like(dv_ref)

        qv, kvv, vv, dov = q_ref[...], k_ref[...], v_ref[...], do_ref[...]
        p = _softmax_rows(_nt(qv, kvv) * scale)
        dp = _nt(dov, vv)
        ds = (p * (dp - jnp.sum(dp * p, axis=-1, keepdims=True)) * scale).astype(MXU)
        dq_ref[...] = _nn(ds, kvv).astype(BF16)
        dk_ref[...] += _tn(ds, qv)
        dv_ref[...] += _tn(p.astype(MXU), dov)

    qspec = pl.BlockSpec((tq, XD), lambda b, h, i: (b * nq + i, h))
    kspec = pl.BlockSpec((m, XD), lambda b, h, i: (b, h))
    return pl.pallas_call(
        body, grid=(bl, XH, nq),
        in_specs=[qspec, kspec, pl.BlockSpec((m, XD), lambda b, h, i: (b, XH + h)), qspec],
        out_specs=[qspec, kspec, kspec],
        out_shape=[jax.ShapeDtypeStruct((bl * s, D), BF16), jax.ShapeDtypeStruct((bl * m, D), F32),
                   jax.ShapeDtypeStruct((bl * m, D), F32)],
        compiler_params=_cp("parallel", "parallel", "arbitrary"), name=name)(q, kv, kv, do)


def _row_tile(r, c, max_elems=512 * 1024, align=16):
    best = None
    for t in range(align, r + 1, align):
        if r % t == 0 and t * c <= max_elems:
            best = t
    return best if best is not None else r


def _addn(a, others, *, name, also_bf16=False):
    r, c = a.shape
    tr = _row_tile(r, c)
    n = len(others)

    def body(*refs):
        acc = refs[0][...].astype(F32)
        for o_ref in refs[1:1 + n]:
            acc = acc + o_ref[...].astype(F32)
        refs[1 + n][...] = acc
        if also_bf16:
            refs[2 + n][...] = acc.astype(BF16)

    spec = pl.BlockSpec((tr, c), lambda i: (i, 0))
    shapes = [jax.ShapeDtypeStruct((r, c), F32)] + ([jax.ShapeDtypeStruct((r, c), BF16)] if also_bf16 else [])
    out = pl.pallas_call(
        body, grid=(r // tr,), in_specs=[spec] * (1 + n), out_specs=[spec] * len(shapes), out_shape=shapes,
        compiler_params=_cp("parallel"), name=name)(a, *others)
    return out if also_bf16 else out[0]


def _sum_leading(a, *, name):
    n, r, c = a.shape

    def body(a_ref, o_ref):
        acc = a_ref[0]
        for i in range(1, n):
            acc = acc + a_ref[i]
        o_ref[...] = acc

    return pl.pallas_call(body, out_shape=jax.ShapeDtypeStruct((r, c), F32), name=name)(a)


def _adamw(w, g, m, v, *, name):
    r, c = w.shape
    tr = _row_tile(r, c, align=8)
    c1 = 1.0 - ADAM_B1 ** ADAM_STEP
    c2 = 1.0 - ADAM_B2 ** ADAM_STEP

    def body(w_ref, g_ref, m_ref, v_ref, d_ref, mo_ref, vo_ref):
        gv = g_ref[...]
        m2 = ADAM_B1 * m_ref[...] + (1.0 - ADAM_B1) * gv
        v2 = ADAM_B2 * v_ref[...] + (1.0 - ADAM_B2) * (gv * gv)
        d_ref[...] = -ADAM_LR * ((m2 / c1) / (jnp.sqrt(v2 / c2) + ADAM_EPS) + ADAM_WD * w_ref[...])
        mo_ref[...] = m2
        vo_ref[...] = v2

    spec = pl.BlockSpec((tr, c), lambda i: (i, 0))
    shp = jax.ShapeDtypeStruct((r, c), F32)
    return pl.pallas_call(
        body, grid=(r // tr,), in_specs=[spec] * 4, out_specs=[spec] * 3, out_shape=[shp] * 3,
        compiler_params=_cp("parallel"), name=name)(w, g, m, v)


def _flip(i, d):
    return 1 - i if d else i


def _comm(name, ins, out_shapes, n_remote, n_local, plan, aliases=None):
    n_in, n_out = len(ins), len(out_shapes)

    def body(*refs):
        in_refs, out_refs = refs[:n_in], refs[n_in:n_in + n_out]
        send_sems, recv_sems = refs[n_in + n_out], refs[n_in + n_out + 1]
        x, y, c = lax.axis_index("x"), lax.axis_index("y"), lax.axis_index("c")
        remote, local = plan(in_refs, out_refs, x, y, c)
        assert len(remote) == n_remote and len(local) == n_local
        copies = []
        if n_local:
            loc_sems = refs[n_in + n_out + 2]
            copies += [pltpu.make_async_copy(s_, d_, loc_sems.at[i]) for i, (s_, d_) in enumerate(local)]
        copies += [pltpu.make_async_remote_copy(src_ref=s_, dst_ref=d_, send_sem=send_sems.at[i],
                                                recv_sem=recv_sems.at[i], device_id=dev, device_id_type=MESH)
                   for i, (s_, d_, dev) in enumerate(remote)]
        for cp in copies:
            cp.start()
        for cp in copies:
            cp.wait()

    hbm = pl.BlockSpec(memory_space=pl.ANY)
    scratch = [pltpu.SemaphoreType.DMA((n_remote,)), pltpu.SemaphoreType.DMA((n_remote,))]
    if n_local:
        scratch.append(pltpu.SemaphoreType.DMA((n_local,)))
    return pl.pallas_call(
        body, in_specs=[hbm] * n_in, out_specs=[hbm] * n_out, out_shape=out_shapes, scratch_shapes=scratch,
        input_output_aliases=aliases or {}, compiler_params=pltpu.CompilerParams(has_side_effects=True),
        name=name)(*ins)


def _gather_weights(shards):
    n = len(shards)
    outs = [jax.ShapeDtypeStruct((4,) + s.shape, s.dtype) for s in shards]

    def plan_ici(in_refs, out_refs, x, y, c):
        k = 2 * x + y
        remote, local = [], []
        for w_ref, o_ref in zip(in_refs, out_refs):
            h = w_ref.shape[0] // 2
            rows = pl.ds(c * h, h)
            local.append((w_ref.at[rows], o_ref.at[k, rows]))
            for dx, dy in CHIP_FLIPS:
                remote.append((w_ref.at[rows], o_ref.at[k, rows], (_flip(x, dx), _flip(y, dy), c)))
        return remote, local

    def plan_d2d(in_refs, out_refs, x, y, c):
        remote = []
        for o_ref in out_refs:
            h = o_ref.shape[1] // 2
            half = o_ref.at[pl.ds(0, 4), pl.ds(c * h, h)]
            remote.append((half, half, (x, y, 1 - c)))
        return remote, []

    part = _comm("gather_w_ici", shards, outs, 3 * n, n, plan_ici)
    return _comm("gather_w_d2d", part, outs, n, 0, plan_d2d, aliases={i: i for i in range(n)})


def _reduce_scatter_grads(grads):
    n = len(grads)
    x, y, c = lax.axis_index("x"), lax.axis_index("y"), lax.axis_index("c")
    halves = [g.shape[1] // 2 for g in grads]
    mine = [lax.dynamic_slice_in_dim(g, c * h, h, axis=1) for g, h in zip(grads, halves)]
    send_a = [lax.dynamic_slice_in_dim(g, (1 - c) * h, h, axis=1).astype(BF16) for g, h in zip(grads, halves)]

    def plan_a(in_refs, out_refs, x_, y_, c_):
        return [(i_, o_, (x_, y_, 1 - c_)) for i_, o_ in zip(in_refs, out_refs)], []

    recv_a = _comm("rs_pair", send_a, [jax.ShapeDtypeStruct(s.shape, BF16) for s in send_a], n, 0, plan_a)
    pair, pair_b = [], []
    for i, (mi, ra) in enumerate(zip(mine, recv_a)):
        four, h, cols = mi.shape
        p32, p16 = _addn(mi.reshape(four * h, cols), [ra.reshape(four * h, cols)], name=f"rs_pair_sum_{i}",
                         also_bf16=True)
        pair.append(p32.reshape(four, h, cols))
        pair_b.append(p16.reshape(four, h, cols))

    def plan_b(in_refs, out_refs, x_, y_, c_):
        remote = []
        for i_, o_ in zip(in_refs, out_refs):
            for j, (dx, dy) in enumerate(CHIP_FLIPS):
                fx, fy = _flip(x_, dx), _flip(y_, dy)
                remote.append((i_.at[2 * fx + fy], o_.at[j], (fx, fy, c_)))
        return remote, []

    recv_b = _comm("rs_chips", pair_b, [jax.ShapeDtypeStruct((3,) + p.shape[1:], BF16) for p in pair_b], 3 * n, 0,
                   plan_b)
    k = 2 * x + y
    tot = [_addn(lax.dynamic_index_in_dim(p, k, 0, keepdims=False), [rb[0], rb[1], rb[2]], name=f"rs_chip_sum_{i}")
           for i, (p, rb) in enumerate(zip(pair, recv_b))]

    def plan_c(in_refs, out_refs, x_, y_, c_):
        remote, local = [], []
        for i_, o_ in zip(in_refs, out_refs):
            rows = pl.ds(c_ * i_.shape[0], i_.shape[0])
            local.append((i_, o_.at[rows]))
            remote.append((i_, o_.at[rows], (x_, y_, 1 - c_)))
        return remote, local

    return _comm("rs_halves", tot, [jax.ShapeDtypeStruct((2 * t.shape[0], t.shape[1]), F32) for t in tot], n, n,
                 plan_c)


def _gather_all(vec, *, name):
    out = jax.ShapeDtypeStruct((8,) + vec.shape, vec.dtype)

    def plan(in_refs, out_refs, x, y, c):
        me = 4 * x + 2 * y + c
        remote = [(in_refs[0], out_refs[0].at[me], (_flip(x, dx), _flip(y, dy), _flip(c, dc)))
                  for dx in (0, 1) for dy in (0, 1) for dc in (0, 1) if (dx, dy, dc) != (0, 0, 0)]
        return remote, [(in_refs[0], out_refs[0].at[me])]

    return _comm(name, [vec], [out], 7, 1, plan)[0]


def _pack(parts):
    flat = [p.reshape(-1).astype(F32) for p in parts]
    total = sum(f.shape[0] for f in flat)
    n = -(-total // 1024) * 128
    vec = jnp.concatenate(flat + [jnp.zeros((8 * n - total,), F32)]).reshape(8, n)
    offs, o = [], 0
    for f in flat:
        offs.append((o, f.shape[0]))
        o += f.shape[0]
    return vec, offs


def _unpack(vec, offs, shapes):
    flat = vec.reshape(-1)
    return [flat[o:o + n].reshape(s) for (o, n), s in zip(offs, shapes)]


BIG = (("ffn1_w_gate_up", "col"), ("ffn1_w_down", "row"), ("w_in", "col"), ("w_out_a", "row"), ("w_out_ssm", "row"),
       ("w_mix_out", "row"), ("w_q", "row"), ("w_kv", "col"), ("w_o_x", "row"), ("ffn2_w_gate_up", "col"),
       ("ffn2_w_down", "row"))
SMALL = ("ffn1_norm", "mix_norm", "conv_a_w", "ssm_conv_w", "ssm_conv_b", "ssm_dt_bias", "ssm_a_log", "ssm_d",
         "ssm_norm", "xattn_norm", "mem_norm", "ffn2_norm", "final_norm")
WEIGHTS = ("ffn1_norm", "ffn1_w_gate_up", "ffn1_w_down", "mix_norm", "w_in", "conv_a_w", "w_out_a", "ssm_conv_w",
           "ssm_conv_b", "ssm_dt_bias", "ssm_a_log", "ssm_d", "ssm_norm", "w_out_ssm", "w_mix_out", "xattn_norm",
           "mem_norm", "w_q", "w_kv", "w_o_x", "ffn2_norm", "ffn2_w_gate_up", "ffn2_w_down", "final_norm")


def _full_weight(g, kind):
    four, r, cols = g.shape
    if kind == "row":
        return g.reshape(four * r, cols)
    return jnp.transpose(g, (1, 0, 2)).reshape(r, four * cols)


def _shard_major(dw, kind):
    if kind == "row":
        return dw.reshape(4, dw.shape[0] // 4, dw.shape[1])
    return jnp.transpose(dw.reshape(dw.shape[0], 4, dw.shape[1] // 4), (1, 0, 2))


def _pad_rows8(w):
    return jnp.concatenate([w, jnp.zeros((8 - w.shape[0], w.shape[1]), w.dtype)], axis=0)


def _group_lanes(v):
    r = v.shape[0]
    return jnp.pad(v.reshape(r, NG, NH // NG), ((0, 0), (0, 0), (0, 128 - NH // NG))).reshape(r, NG * 128)


def _ungroup_lanes(v):
    r = v.shape[0]
    return v.reshape(r, NG, 128)[:, :, :NH // NG].reshape(r, NH)


def _local_step(wfull, small, x, mem, target):
    bl, s, _ = x.shape
    T = bl * s
    x2, t2 = x.reshape(T, D), target.reshape(T, D)
    mem2 = mem.reshape(-1, D)
    g = {}

    def gain(name):
        return small[name].reshape(1, -1)

    def ffn_fwd(h, norm, wgu, wd, tag):
        n = _norm_fwd(h, gain(norm), name=f"{tag}_norm")
        gu = _mm(n, wfull[wgu], "nn", name=f"{tag}_gate_up")
        a = _swiglu_fwd(gu, name=f"{tag}_swiglu")
        out = _mm(a, wfull[wd], "nn", tn=1024, tk=DFF, scale=0.5, residual=h, name=f"{tag}_down")
        return out, (n, gu, a)

    def ffn_bwd(dh, h, norm, wgu, wd, saved, tag):
        n, gu, a = saved
        da = _mm(dh, wfull[wd], "nt", tn=DFF, tk=1024, scale=0.5, name=f"{tag}_d_act")
        g[wd] = _mm(a, dh, "tn", tm=1408, tn=1024, tk=512, scale=0.5, name=f"{tag}_d_w_down")
        dgu = _swiglu_bwd(gu, da, name=f"{tag}_d_swiglu")
        dn = _mm(dgu, wfull[wgu], "nt", tn=1024, tk=1408, name=f"{tag}_d_norm_out")
        g[wgu] = _mm(n, dgu, "tn", tm=1024, tn=512, tk=512, name=f"{tag}_d_w_gate_up")
        dh_in, g[norm] = _norm_bwd(h, gain(norm), dn, dh, name=f"{tag}_d_norm")
        return dh_in

    h1, ffn1_saved = ffn_fwd(x2, "ffn1_norm", "ffn1_w_gate_up", "ffn1_w_down", "ffn1")
    u = _norm_fwd(h1, gain("mix_norm"), name="mix_norm")
    pp = _mm(u, wfull["w_in"], "nn", tn=1152, name="in_proj")
    wa8 = _pad_rows8(small["conv_a_w"])
    ws8 = _pad_rows8(small["ssm_conv_w"])
    conv_b = gain("ssm_conv_b")
    bias128 = jnp.pad(gain("ssm_dt_bias"), ((0, 0), (0, 128 - NH)))
    ya_pre = _conv_a_fwd(pp, wa8, bl, s, name="conv_a")
    xc = _conv_ssm_fwd(pp, ws8, conv_b, bl, s, name="conv_ssm")
    dt = _dt_fwd(pp, bias128, name="dt")
    dtg = _group_lanes(dt[:, :NH])
    dtt = dt[:, :NH].T.reshape(NG, NH // NG, T)
    alog = gain("ssm_a_log")
    arow, acol = _group_lanes(alog), alog.reshape(NG, NH // NG, 1)
    dexp = jnp.repeat(gain("ssm_d"), HD, axis=1)
    ng = gain("ssm_norm")
    y, yn, prev = _ssd_fwd(xc, pp, dtg, dtt, arow, acol, dexp, ng, bl, s, name="ssd")
    ya = _mm(ya_pre, wfull["w_out_a"], "nn", tn=1024, name="out_a")
    yb = _mm(yn, wfull["w_out_ssm"], "nn", tn=1024, tk=DI, name="out_ssm")
    merged = _merge_fwd(pp, ya, yb, name="merge")
    h2 = _mm(merged, wfull["w_mix_out"], "nn", tn=1024, residual=h1, name="mix_out")
    un = _norm_fwd(h2, gain("xattn_norm"), name="xattn_norm")
    q = _mm(un, wfull["w_q"], "nn", tn=1024, out_dtype=BF16, name="q_proj")
    mn = _norm_fwd(mem2, gain("mem_norm"), name="mem_norm")
    kv = _mm(mn, wfull["w_kv"], "nn", tn=1024, out_dtype=BF16, name="kv_proj")
    o = _attn_fwd(q, kv, bl, s, name="attn")
    h3 = _mm(o, wfull["w_o_x"], "nn", tn=1024, residual=h2, name="attn_out")
    h4, ffn2_saved = ffn_fwd(h3, "ffn2_norm", "ffn2_w_gate_up", "ffn2_w_down", "ffn2")
    sq_err, dh4, dgf = _final_loss(h4, gain("final_norm"), t2, name="final_loss")
    g["final_norm"] = dgf

    dh3 = ffn_bwd(dh4, h3, "ffn2_norm", "ffn2_w_gate_up", "ffn2_w_down", ffn2_saved, "ffn2")
    do = _mm(dh3, wfull["w_o_x"], "nt", tn=1024, out_dtype=BF16, name="d_attn_o")
    g["w_o_x"] = _mm(o, dh3, "tn", tm=1024, tn=1024, tk=512, name="d_w_o_x")
    dq, dk, dv = _attn_bwd(q, kv, do, bl, s, name="d_attn")
    dun = _mm(dq, wfull["w_q"], "nt", tn=1024, name="d_xattn_norm_out")
    g["w_q"] = _mm(un, dq, "tn", tm=1024, tn=1024, tk=512, name="d_w_q")
    dkv = jnp.concatenate([dk, dv], axis=1)
    dmn = _mm(dkv, wfull["w_kv"], "nt", tn=1024, tk=2 * D, name="d_mem_norm_out")
    g["w_kv"] = _mm(mn, dkv, "tn", tm=1024, tn=1024, tk=512, name="d_w_kv")
    _, g["mem_norm"] = _norm_bwd(mem2, gain("mem_norm"), dmn, None, name="d_mem_norm")
    dh2, g["xattn_norm"] = _norm_bwd(h2, gain("xattn_norm"), dun, dh3, name="d_xattn_norm")
    dmerged = _mm(dh2, wfull["w_mix_out"], "nt", tn=1024, name="d_merged")
    g["w_mix_out"] = _mm(merged, dh2, "tn", tm=1024, tn=1024, tk=512, name="d_w_mix_out")
    dya, dyb, dga, dgb = _merge_bwd(pp, ya, yb, dmerged, name="d_merge")
    dya_pre = _mm(dya, wfull["w_out_a"], "nt", tn=1024, name="d_conv_a_out")
    g["w_out_a"] = _mm(ya_pre, dya, "tn", tm=1024, tn=1024, tk=512, name="d_w_out_a")
    dyn = _mm(dyb, wfull["w_out_ssm"], "nt", tn=1024, name="d_ssd_out")
    g["w_out_ssm"] = _mm(yn, dyb, "tn", tm=1024, tn=1024, tk=512, name="d_w_out_ssm")
    d_ab, d_ac, d_av, dwa8 = _conv_a_bwd(pp, wa8, dya_pre, bl, s, name="d_conv_a")
    g["conv_a_w"] = dwa8[:3]
    dz, dxs, dbm, dcm, ddtg, g["ssm_norm"], ddexp, dalg = _ssd_bwd(
        dyn, y, xc, pp, dtg, dtt, arow, acol, dexp, ng, prev, bl, s, name="d_ssd")
    g["ssm_d"] = ddexp.reshape(NH, HD).sum(axis=1).reshape(1, NH)
    g["ssm_a_log"] = _ungroup_lanes(dalg)
    conv_parts = [_conv_ssm_bwd(pp, ws8, conv_b, dpart, off, bl, s, name=f"d_conv_ssm_{tag}")
                  for dpart, off, tag in ((dxs, 0, "x"), (dbm, DI, "b"), (dcm, DI + NG * NS, "c"))]
    g["ssm_conv_w"] = jnp.concatenate([p[1] for p in conv_parts], axis=1)[:4]
    g["ssm_conv_b"] = jnp.concatenate([p[2] for p in conv_parts], axis=1)
    ddt = jnp.pad(_ungroup_lanes(ddtg), ((0, 0), (0, 128 - NH)))
    ddt_raw, dbias = _dt_bwd(pp, bias128, ddt, name="d_dt")
    g["ssm_dt_bias"] = dbias[:, :NH]
    dpp = jnp.concatenate([d_ab, d_ac, d_av, dz] + [p[0] for p in conv_parts] + [dga, dgb, ddt_raw], axis=1)
    du = _mm(dpp, wfull["w_in"], "nt", tn=1024, tk=1152, name="d_mix_norm_out")
    g["w_in"] = _mm(u, dpp, "tn", tm=1024, tn=1152, tk=512, name="d_w_in")
    dh1, g["mix_norm"] = _norm_bwd(h1, gain("mix_norm"), du, dh2, name="d_mix_norm")
    dx = ffn_bwd(dh1, x2, "ffn1_norm", "ffn1_w_gate_up", "ffn1_w_down", ffn1_saved, "ffn1")
    return sq_err, dx, g


def _pad_w_in(w):
    return jnp.concatenate([w[:, :O_GA], w[:, O_GA + NH:], w[:, O_GA:O_GA + NH],
                            jnp.zeros((w.shape[0], NPP - NIN), w.dtype)], axis=1)


def _unpad_w_in(w):
    return jnp.concatenate([w[:, :O_GA], w[:, O_DT:O_DT + NH], w[:, O_GA:O_DT]], axis=1)


def kernel(x, mem, ffn1_norm, ffn1_w_gate_up, ffn1_w_down, mix_norm, w_in, conv_a_w, w_out_a, ssm_conv_w, ssm_conv_b, ssm_dt_bias, ssm_a_log, ssm_d, ssm_norm, w_out_ssm, w_mix_out, xattn_norm, mem_norm, w_q, w_kv, w_o_x, ffn2_norm, ffn2_w_gate_up, ffn2_w_down, final_norm, loss_target, m_ffn1_norm, m_ffn1_w_gate_up, m_ffn1_w_down, m_mix_norm, m_w_in, m_conv_a_w, m_w_out_a, m_ssm_conv_w, m_ssm_conv_b, m_ssm_dt_bias, m_ssm_a_log, m_ssm_d, m_ssm_norm, m_w_out_ssm, m_w_mix_out, m_xattn_norm, m_mem_norm, m_w_q, m_w_kv, m_w_o_x, m_ffn2_norm, m_ffn2_w_gate_up, m_ffn2_w_down, m_final_norm, v_ffn1_norm, v_ffn1_w_gate_up, v_ffn1_w_down, v_mix_norm, v_w_in, v_conv_a_w, v_w_out_a, v_ssm_conv_w, v_ssm_conv_b, v_ssm_dt_bias, v_ssm_a_log, v_ssm_d, v_ssm_norm, v_w_out_ssm, v_w_mix_out, v_xattn_norm, v_mem_norm, v_w_q, v_w_kv, v_w_o_x, v_ffn2_norm, v_ffn2_w_gate_up, v_ffn2_w_down, v_final_norm):
    a = dict(locals())
    xi, yi = lax.axis_index("x"), lax.axis_index("y")
    k = 2 * xi + yi

    shards = [a[n][0].astype(BF16) for n, _ in BIG]
    gathered = _gather_weights(shards)
    wfull = {n: _full_weight(gw, kind) for (n, kind), gw in zip(BIG, gathered)}
    wfull["w_in"] = _pad_w_in(wfull["w_in"])
    conv_vec, conv_offs = _pack([a["conv_a_w"], a["ssm_conv_w"]])
    conv_all = _gather_all(conv_vec, name="gather_conv_w")
    conv_sh = [_unpack(conv_all[2 * kk], conv_offs, [a["conv_a_w"].shape[1:], a["ssm_conv_w"].shape[1:]])
               for kk in range(4)]
    small = {n: a[n] for n in SMALL}
    small["conv_a_w"] = jnp.concatenate([cs[0] for cs in conv_sh], axis=1)
    small["ssm_conv_w"] = jnp.concatenate([cs[1] for cs in conv_sh], axis=1)

    sq_err, dx, g = _local_step(wfull, small, x, mem, loss_target)
    loss = lax.psum(0.5 / D * jnp.sum(sq_err), ("x", "y", "c"))

    g["w_in"] = _unpad_w_in(g["w_in"])
    reduced = _reduce_scatter_grads([_shard_major(g[n], kind) for n, kind in BIG])
    out = {}
    for (n, _), gr in zip(BIG, reduced):
        shp = a[n].shape
        d_, m_, v_ = _adamw(a[n][0], gr, a["m_" + n][0], a["v_" + n][0], name=f"adamw_{n}")
        out[n] = tuple(t.reshape(shp) for t in (gr, d_, m_, v_))

    full_shapes = [g[n].shape for n in SMALL]
    gvec, goffs = _pack([g[n] for n in SMALL])
    gsum = _sum_leading(_gather_all(gvec, name="gather_small_grads"), name="sum_small_grads")
    gsmall = dict(zip(SMALL, _unpack(gsum, goffs, full_shapes)))
    for n in ("conv_a_w", "ssm_conv_w"):
        width = a[n].shape[2]
        gsmall[n] = lax.dynamic_slice_in_dim(gsmall[n], k * width, width, axis=1)
    local_shapes = [a[n].shape for n in SMALL]
    packs = [_pack([t[n] for n in SMALL]) for t in
             ({n: a[n] for n in SMALL}, gsmall, {n: a["m_" + n] for n in SMALL}, {n: a["v_" + n] for n in SMALL})]
    offs = packs[0][1]
    res = _adamw(*[p[0] for p in packs], name="adamw_small")
    unp = [_unpack(r, offs, local_shapes) for r in res]
    for i, n in enumerate(SMALL):
        out[n] = (gsmall[n].reshape(a[n].shape), unp[0][i], unp[1][i], unp[2][i])

    grad_x = dx.reshape(x.shape)
    return (loss, grad_x, *[out[n][0] for n in WEIGHTS], *[out[n][1] for n in WEIGHTS],
            *[out[n][2] for n in WEIGHTS], *[out[n][3] for n in WEIGHTS])
```

```python
import functools
import math

import jax
import jax.numpy as jnp
from jax import lax
from jax.experimental import pallas as pl
from jax.experimental.pallas import tpu as pltpu

F32 = jnp.float32
BF16 = jnp.bfloat16
MXU = jnp.bfloat16
HI = lax.Precision.HIGHEST

D = 1024
DFF = 2816
DI = 2048
NH, HD, NG, NS, CH = 32, 64, 4, 128, 128
GW = DI // NG
XH, XD = 4, 256
EPS = 1e-6
NEG = -1e30
O_AB, O_AC, O_AV, O_Z, O_XBC, O_GA, O_GB, O_DT, NPP = 0, 1024, 2048, 3072, 5120, 8192, 9216, 10240, 10368
NIN = 10272
FFN_RES = 0.5
ADAM_LR, ADAM_B1, ADAM_B2, ADAM_EPS, ADAM_WD, ADAM_STEP = 0.001, 0.9, 0.999, 1e-08, 0.01, 10
VMEM_LIMIT = 56 * 1024 * 1024
MESH = pl.DeviceIdType.MESH
CHIP_FLIPS = ((1, 0), (0, 1), (1, 1))


def _cp(*sem):
    return pltpu.CompilerParams(dimension_semantics=sem, vmem_limit_bytes=VMEM_LIMIT)


def _tile(n, pref, align=128):
    if n <= pref:
        return n
    t = (pref // align) * align
    while t >= align:
        if n % t == 0:
            return t
        t -= align
    raise ValueError((n, pref))


def _dot(a, b, dims, prec=None):
    return lax.dot_general(a, b, (dims, ((), ())), preferred_element_type=F32, precision=prec)


def _nn(a, b, prec=None):
    return _dot(a, b, ((1,), (0,)), prec)


def _nt(a, b):
    return _dot(a, b, ((1,), (1,)))


def _tn(a, b):
    return _dot(a, b, ((0,), (0,)))


def _sig(x):
    return jax.nn.sigmoid(x)


def _mm(a, b, mode, *, name, tm=1024, tn=1024, tk=1024, out_dtype=F32, scale=None, residual=None, a2=None):
    if mode == "nn":
        (M, K), (K2, N) = a.shape, b.shape
    elif mode == "nt":
        (M, K), (N, K2) = a.shape, b.shape
        if a2 is not None:
            assert a2.shape == a.shape
            K2 = K2 // 2
    else:
        (K, M), (K2, N) = a.shape, b.shape
    assert K == K2, (name, a.shape, b.shape)
    tm, tn, tk = _tile(M, tm), _tile(N, tn), _tile(K, tk)
    nk = K // tk
    if mode == "nn":
        a_spec = pl.BlockSpec((tm, tk), lambda i, j, k: (i, k))
        b_spec = pl.BlockSpec((tk, tn), lambda i, j, k: (k, j))
        dims = ((1,), (0,))
    elif mode == "nt":
        a_spec = pl.BlockSpec((tm, tk), lambda i, j, k: (i, k))
        b_spec = pl.BlockSpec((tn, tk), lambda i, j, k: (j, k))
        dims = ((1,), (1,))
    else:
        a_spec = pl.BlockSpec((tk, tm), lambda i, j, k: (k, i))
        b_spec = pl.BlockSpec((tk, tn), lambda i, j, k: (k, j))
        dims = ((0,), (0,))
    o_spec = pl.BlockSpec((tm, tn), lambda i, j, k: (i, j))
    has_res = residual is not None

    def finish(acc, r_ref, o_ref):
        if scale is not None:
            acc = acc * scale
        if has_res:
            acc = acc + r_ref[...]
        o_ref[...] = acc.astype(out_dtype)

    dual = a2 is not None
    n_in = 2 + 2 * dual + has_res

    def body(*refs):
        a_ref, b_ref = refs[0], refs[1]
        r_ref = refs[n_in - 1] if has_res else None
        o_ref = refs[n_in]
        part = _dot(a_ref[...].astype(MXU), b_ref[...].astype(MXU), dims)
        if dual:
            part = part + _dot(refs[2][...].astype(MXU), refs[3][...].astype(MXU), dims)
        if nk == 1:
            finish(part, r_ref, o_ref)
            return
        acc_ref = refs[-1]
        k = pl.program_id(2)

        @pl.when(k == 0)
        def _():
            acc_ref[...] = part

        @pl.when(k > 0)
        def _():
            acc_ref[...] += part

        @pl.when(k == nk - 1)
        def _():
            finish(acc_ref[...], r_ref, o_ref)

    ins, in_specs = [a, b], [a_spec, b_spec]
    if dual:
        ins += [a2, b]
        in_specs += [a_spec, pl.BlockSpec((tn, tk), lambda i, j, k: (j, k + nk))]
    if has_res:
        ins.append(residual)
        in_specs.append(o_spec)
    return pl.pallas_call(
        body, grid=(M // tm, N // tn, nk), in_specs=in_specs, out_specs=o_spec,
        out_shape=jax.ShapeDtypeStruct((M, N), out_dtype),
        scratch_shapes=[pltpu.VMEM((tm, tn), F32)] if nk > 1 else [],
        compiler_params=_cp("parallel", "parallel", "arbitrary"), name=name)(*ins)


def _norm_fwd(x, g, *, name):
    T, d = x.shape
    tr = _tile(T, 512, 8)

    def body(x_ref, g_ref, o_ref):
        xv = x_ref[...]
        r = lax.rsqrt(jnp.mean(xv * xv, axis=-1, keepdims=True) + EPS)
        o_ref[...] = (xv * r * g_ref[...]).astype(BF16)

    return pl.pallas_call(
        body, grid=(T // tr,),
        in_specs=[pl.BlockSpec((tr, d), lambda i: (i, 0)), pl.BlockSpec((1, d), lambda i: (0, 0))],
        out_specs=pl.BlockSpec((tr, d), lambda i: (i, 0)),
        out_shape=jax.ShapeDtypeStruct((T, d), BF16), compiler_params=_cp("parallel"), name=name)(x, g)


def _norm_bwd(x, g, dn, dres, *, name):
    T, d = x.shape
    tr = _tile(T, 512, 8)
    has_res = dres is not None

    def body(*refs):
        x_ref, g_ref, dn_ref = refs[:3]
        dr_ref = refs[3] if has_res else None
        dx_ref, dg_ref = refs[-2], refs[-1]

        @pl.when(pl.program_id(0) == 0)
        def _():
            dg_ref[...] = jnp.zeros_like(dg_ref)

        xv = x_ref[...]
        dnv = dn_ref[...].astype(F32)
        r = lax.rsqrt(jnp.mean(xv * xv, axis=-1, keepdims=True) + EPS)
        xh = xv * r
        dg_ref[...] += jnp.sum(dnv * xh, axis=0, keepdims=True)
        dxh = dnv * g_ref[...]
        dx = r * (dxh - xh * jnp.mean(dxh * xh, axis=-1, keepdims=True))
        if has_res:
            dx = dx + dr_ref[...]
        dx_ref[...] = dx

    row = pl.BlockSpec((tr, d), lambda i: (i, 0))
    vec = pl.BlockSpec((1, d), lambda i: (0, 0))
    ins = [x, g, dn] + ([dres] if has_res else [])
    return pl.pallas_call(
        body, grid=(T // tr,), in_specs=[row, vec, row] + ([row] if has_res else []),
        out_specs=[row, vec],
        out_shape=[jax.ShapeDtypeStruct((T, d), F32), jax.ShapeDtypeStruct((1, d), F32)],
        compiler_params=_cp("arbitrary"), name=name)(*ins)


def _final_loss(h, g, target, *, name):
    T, d = h.shape
    tr = _tile(T, 512, 8)

    def body(h_ref, g_ref, t_ref, l_ref, dh_ref, dg_ref):
        @pl.when(pl.program_id(0) == 0)
        def _():
            l_ref[...] = jnp.zeros_like(l_ref)
            dg_ref[...] = jnp.zeros_like(dg_ref)

        xv = h_ref[...]
        r = lax.rsqrt(jnp.mean(xv * xv, axis=-1, keepdims=True) + EPS)
        xh = xv * r
        e = xh * g_ref[...] - t_ref[...]
        l_ref[...] += jnp.sum(e * e, axis=0, keepdims=True)
        dy = e * (1.0 / d)
        dg_ref[...] += jnp.sum(dy * xh, axis=0, keepdims=True)
        dxh = dy * g_ref[...]
        dh_ref[...] = r * (dxh - xh * jnp.mean(dxh * xh, axis=-1, keepdims=True))

    row = pl.BlockSpec((tr, d), lambda i: (i, 0))
    vec = pl.BlockSpec((1, d), lambda i: (0, 0))
    return pl.pallas_call(
        body, grid=(T // tr,), in_specs=[row, vec, row], out_specs=[vec, row, vec],
        out_shape=[jax.ShapeDtypeStruct((1, d), F32), jax.ShapeDtypeStruct((T, d), F32),
                   jax.ShapeDtypeStruct((1, d), F32)],
        compiler_params=_cp("arbitrary"), name=name)(h, g, target)


def _gate_up_fwd(n, wgu, *, name):
    T, d = n.shape
    f = wgu.shape[1] // 2
    tm, tn = _tile(T, 512, 8), _tile(f, 1408)
    nf = f // tn

    def body(n_ref, wg_ref, wu_ref, g_ref, u_ref, a_ref):
        nv = n_ref[...].astype(MXU)
        gv = _nn(nv, wg_ref[...].astype(MXU))
        uv = _nn(nv, wu_ref[...].astype(MXU))
        g_ref[...] = gv.astype(BF16)
        u_ref[...] = uv.astype(BF16)
        a_ref[...] = (gv * _sig(gv) * uv).astype(BF16)

    out = pl.BlockSpec((tm, tn), lambda i, j: (i, j))
    act = jax.ShapeDtypeStruct((T, f), BF16)
    return pl.pallas_call(
        body, grid=(T // tm, nf),
        in_specs=[pl.BlockSpec((tm, d), lambda i, j: (i, 0)), pl.BlockSpec((d, tn), lambda i, j: (0, j)),
                  pl.BlockSpec((d, tn), lambda i, j: (0, j + nf))],
        out_specs=[out, out, out], out_shape=[act, act, act], compiler_params=_cp("parallel", "parallel"),
        name=name)(n, wgu, wgu)


def _act_bwd(dh, wd, gate, up, scale, *, name):
    T, d = dh.shape
    f = wd.shape[0]
    tm, tn = _tile(T, 512, 8), _tile(f, 1408)

    def body(dh_ref, wd_ref, g_ref, u_ref, dg_ref, du_ref):
        da = scale * _nt(dh_ref[...].astype(MXU), wd_ref[...].astype(MXU))
        gv, uv = g_ref[...].astype(F32), u_ref[...].astype(F32)
        s = _sig(gv)
        dg_ref[...] = (da * uv * (s * (1.0 + gv * (1.0 - s)))).astype(BF16)
        du_ref[...] = (da * (gv * s)).astype(BF16)

    tile = pl.BlockSpec((tm, tn), lambda i, j: (i, j))
    act = jax.ShapeDtypeStruct((T, f), BF16)
    return pl.pallas_call(
        body, grid=(T // tm, f // tn),
        in_specs=[pl.BlockSpec((tm, d), lambda i, j: (i, 0)), pl.BlockSpec((tn, d), lambda i, j: (j, 0)), tile, tile],
        out_specs=[tile, tile], out_shape=[act, act], compiler_params=_cp("parallel", "parallel"),
        name=name)(dh, wd, gate, up)


def _shift_down(x, d, ri):
    if d == 0:
        return x
    return jnp.where(ri >= d, pltpu.roll(x, d, 0), 0.0)


def _shift_up(x, d, ri):
    if d == 0:
        return x
    s = x.shape[0]
    return jnp.where(ri < s - d, pltpu.roll(x, s - d, 0), 0.0)


def _conv_taps(x, w_ref, ktaps, ri):
    acc = None
    for k in range(ktaps):
        t = w_ref[k:k + 1, :] * _shift_down(x, ktaps - 1 - k, ri)
        acc = t if acc is None else acc + t
    return acc


def _conv_a_fwd(pp, w8, bl, s, *, name):
    tc = 256
    nb = D // tc

    def body(b_ref, c_ref, v_ref, w_ref, o_ref):
        ri = lax.broadcasted_iota(jnp.int32, (s, tc), 0)
        cv = c_ref[...] * v_ref[...]
        o_ref[...] = (b_ref[...] * _conv_taps(cv, w_ref, 3, ri)).astype(BF16)

    def col(off):
        return pl.BlockSpec((s, tc), lambda b, j: (b, off // tc + j))

    return pl.pallas_call(
        body, grid=(bl, nb),
        in_specs=[col(O_AB), col(O_AC), col(O_AV), pl.BlockSpec((8, tc), lambda b, j: (0, j))],
        out_specs=pl.BlockSpec((s, tc), lambda b, j: (b, j)),
        out_shape=jax.ShapeDtypeStruct((bl * s, D), BF16), compiler_params=_cp("parallel", "parallel"),
        name=name)(pp, pp, pp, w8)


def _conv_a_bwd(pp, w8, dya, bl, s, *, name):
    tc = 256
    nb = D // tc

    def body(b_ref, c_ref, v_ref, w_ref, dy_ref, db_ref, dc_ref, dv_ref, dw_ref):
        @pl.when(pl.program_id(1) == 0)
        def _():
            dw_ref[...] = jnp.zeros_like(dw_ref)

        ri = lax.broadcasted_iota(jnp.int32, (s, tc), 0)
        cvec, vvec, dy = c_ref[...], v_ref[...], dy_ref[...]
        cv = cvec * vvec
        db_ref[...] = (dy * _conv_taps(cv, w_ref, 3, ri)).astype(BF16)
        dconv = dy * b_ref[...]
        dcv = None
        for k in range(3):
            t = w_ref[k:k + 1, :] * _shift_up(dconv, 2 - k, ri)
            dcv = t if dcv is None else dcv + t
            dw_ref[k:k + 1, :] += jnp.sum(dconv * _shift_down(cv, 2 - k, ri), axis=0, keepdims=True)
        dc_ref[...] = (dcv * vvec).astype(BF16)
        dv_ref[...] = (dcv * cvec).astype(BF16)

    def col(off):
        return pl.BlockSpec((s, tc), lambda j, b: (b, off // tc + j))

    plain = pl.BlockSpec((s, tc), lambda j, b: (b, j))
    wspec = pl.BlockSpec((8, tc), lambda j, b: (0, j))
    act = jax.ShapeDtypeStruct((bl * s, D), BF16)
    return pl.pallas_call(
        body, grid=(nb, bl), in_specs=[col(O_AB), col(O_AC), col(O_AV), wspec, plain],
        out_specs=[plain, plain, plain, wspec],
        out_shape=[act, act, act, jax.ShapeDtypeStruct((8, D), F32)],
        compiler_params=_cp("parallel", "arbitrary"), name=name)(pp, pp, pp, w8, dya)


def _conv_ssm_fwd(pp, w8, bias, bl, s, *, name):
    tc = 256
    width = DI + 2 * NG * NS
    nb = width // tc

    def body(x_ref, w_ref, b_ref, o_ref):
        ri = lax.broadcasted_iota(jnp.int32, (s, tc), 0)
        pre = _conv_taps(x_ref[...], w_ref, 4, ri) + b_ref[...]
        o_ref[...] = pre * _sig(pre)

    return pl.pallas_call(
        body, grid=(bl, nb),
        in_specs=[pl.BlockSpec((s, tc), lambda b, j: (b, O_XBC // tc + j)),
                  pl.BlockSpec((8, tc), lambda b, j: (0, j)), pl.BlockSpec((1, tc), lambda b, j: (0, j))],
        out_specs=pl.BlockSpec((s, tc), lambda b, j: (b, j)),
        out_shape=jax.ShapeDtypeStruct((bl * s, width), F32), compiler_params=_cp("parallel", "parallel"),
        name=name)(pp, w8, bias)


def _conv_ssm_bwd(pp, w8, bias, dxc, ch_off, bl, s, *, name):
    n = dxc.shape[1]
    tc = 256
    nb = n // tc
    o0 = ch_off // tc

    def body(x_ref, w_ref, b_ref, d_ref, dx_ref, dw_ref, db_ref):
        @pl.when(pl.program_id(1) == 0)
        def _():
            dw_ref[...] = jnp.zeros_like(dw_ref)
            db_ref[...] = jnp.zeros_like(db_ref)

        ri = lax.broadcasted_iota(jnp.int32, (s, tc), 0)
        xv = x_ref[...]
        pre = _conv_taps(xv, w_ref, 4, ri) + b_ref[...]
        sg = _sig(pre)
        dpre = d_ref[...] * (sg * (1.0 + pre * (1.0 - sg)))
        db_ref[...] += jnp.sum(dpre, axis=0, keepdims=True)
        dx = None
        for k in range(4):
            t = w_ref[k:k + 1, :] * _shift_up(dpre, 3 - k, ri)
            dx = t if dx is None else dx + t
            dw_ref[k:k + 1, :] += jnp.sum(dpre * _shift_down(xv, 3 - k, ri), axis=0, keepdims=True)
        dx_ref[...] = dx.astype(BF16)

    plain = pl.BlockSpec((s, tc), lambda j, b: (b, j))
    return pl.pallas_call(
        body, grid=(nb, bl),
        in_specs=[pl.BlockSpec((s, tc), lambda j, b: (b, O_XBC // tc + o0 + j)),
                  pl.BlockSpec((8, tc), lambda j, b: (0, o0 + j)), pl.BlockSpec((1, tc), lambda j, b: (0, o0 + j)),
                  plain],
        out_specs=[plain, pl.BlockSpec((8, tc), lambda j, b: (0, j)), pl.BlockSpec((1, tc), lambda j, b: (0, j))],
        out_shape=[jax.ShapeDtypeStruct((bl * s, n), BF16), jax.ShapeDtypeStruct((8, n), F32),
                   jax.ShapeDtypeStruct((1, n), F32)],
        compiler_params=_cp("parallel", "arbitrary"), name=name)(pp, w8, bias, dxc)


def _softplus(x):
    return jnp.maximum(x, 0.0) + jnp.log1p(jnp.exp(-jnp.abs(x)))


def _dt_fwd(pp, bias128, *, name):
    T = pp.shape[0]
    tr = _tile(T, 1024, 8)

    def body(x_ref, b_ref, o_ref):
        lane = lax.broadcasted_iota(jnp.int32, (tr, 128), 1)
        o_ref[...] = jnp.where(lane < NH, _softplus(x_ref[...] + b_ref[...]), 0.0)

    return pl.pallas_call(
        body, grid=(T // tr,),
        in_specs=[pl.BlockSpec((tr, 128), lambda i: (i, O_DT // 128)), pl.BlockSpec((1, 128), lambda i: (0, 0))],
        out_specs=pl.BlockSpec((tr, 128), lambda i: (i, 0)),
        out_shape=jax.ShapeDtypeStruct((T, 128), F32), compiler_params=_cp("parallel"), name=name)(pp, bias128)


def _dt_bwd(pp, bias128, ddt, *, name):
    T = pp.shape[0]
    tr = _tile(T, 1024, 8)

    def body(x_ref, b_ref, d_ref, o_ref, db_ref):
        @pl.when(pl.program_id(0) == 0)
        def _():
            db_ref[...] = jnp.zeros_like(db_ref)

        lane = lax.broadcasted_iota(jnp.int32, (tr, 128), 1)
        dr = jnp.where(lane < NH, d_ref[...] * _sig(x_ref[...] + b_ref[...]), 0.0)
        db_ref[...] += jnp.sum(dr, axis=0, keepdims=True)
        o_ref[...] = dr.astype(BF16)

    row = pl.BlockSpec((tr, 128), lambda i: (i, 0))
    vec = pl.BlockSpec((1, 128), lambda i: (0, 0))
    return pl.pallas_call(
        body, grid=(T // tr,),
        in_specs=[pl.BlockSpec((tr, 128), lambda i: (i, O_DT // 128)), vec, row],
        out_specs=[row, vec],
        out_shape=[jax.ShapeDtypeStruct((T, 128), BF16), jax.ShapeDtypeStruct((1, 128), F32)],
        compiler_params=_cp("arbitrary"), name=name)(pp, bias128, ddt)


def _ssd_common(dt, dtt, arow, acol):
    ri = lax.broadcasted_iota(jnp.int32, (CH, CH), 0)
    ci = lax.broadcasted_iota(jnp.int32, (CH, CH), 1)
    tril = ri >= ci
    triu = ri <= ci
    acs_col = _nn(tril.astype(F32), dt * arow, HI)
    acs_row = _nn(dtt * acol, triu.astype(F32), HI)
    return tril, triu, acs_col, acs_row


def _pair_terms(q, dt, acs_col, acs_row, tril, triu, lo, with_t):
    ha, hb = 2 * q, 2 * q + 1
    col_a, col_b = acs_col[:, ha:ha + 1], acs_col[:, hb:hb + 1]
    row_a, row_b = acs_row[ha:ha + 1, :], acs_row[hb:hb + 1, :]
    last_a, last_b = acs_col[CH - 1:CH, ha:ha + 1], acs_col[CH - 1:CH, hb:hb + 1]
    out = dict(
        dtsel=jnp.where(lo, dt[:, ha:ha + 1], dt[:, hb:hb + 1]),
        d_a=jnp.exp(jnp.where(tril, col_a - row_a, NEG)), d_b=jnp.exp(jnp.where(tril, col_b - row_b, NEG)),
        esel=jnp.where(lo, jnp.exp(col_a), jnp.exp(col_b)),
        fsel=jnp.where(lo, jnp.exp(last_a - col_a), jnp.exp(last_b - col_b)),
        g_a=jnp.exp(last_a), g_b=jnp.exp(last_b))
    if with_t:
        out["dt_a"] = jnp.exp(jnp.where(triu, row_a - col_a, NEG))
        out["dt_b"] = jnp.exp(jnp.where(triu, row_b - col_b, NEG))
    return out


def _ssd_fwd(xc, pp, dtg, dtt, arow, acol, dexp, ng, bl, s, *, name):
    nc = s // CH
    T = bl * s

    def body(xs_ref, bm_ref, cm_ref, z_ref, dt_ref, dtt_ref, arow_ref, acol_ref, dexp_ref, ng_ref,
             y_ref, yn_ref, prev_ref, st_ref):
        @pl.when(pl.program_id(2) == 0)
        def _():
            st_ref[...] = jnp.zeros_like(st_ref)

        dt = dt_ref[...]
        tril, triu, acs_col, acs_row = _ssd_common(dt, dtt_ref[...], -jnp.exp(arow_ref[...]), -jnp.exp(acol_ref[...]))
        bm, cm = bm_ref[...].astype(MXU), cm_ref[...].astype(MXU)
        cb = _nt(cm, bm)
        lo = lax.broadcasted_iota(jnp.int32, (CH, 128), 1) < HD
        sub_lo = lax.broadcasted_iota(jnp.int32, (128, NS), 0) < HD
        for q in range(4):
            t = _pair_terms(q, dt, acs_col, acs_row, tril, triu, lo, False)
            x = xs_ref[:, 128 * q:128 * (q + 1)]
            xd = x * t["dtsel"]
            y = (_nn((cb * t["d_a"]).astype(MXU), jnp.where(lo, xd, 0.0).astype(MXU))
                 + _nn((cb * t["d_b"]).astype(MXU), jnp.where(lo, 0.0, xd).astype(MXU)))
            prev = st_ref[q]
            prev_ref[q] = prev
            y = y + t["esel"] * _nt(cm, prev.astype(MXU))
            st_ref[q] = prev * jnp.where(sub_lo, t["g_a"], t["g_b"]) + _tn((xd * t["fsel"]).astype(MXU), bm)
            y_ref[:, 128 * q:128 * (q + 1)] = y + dexp_ref[:, 128 * q:128 * (q + 1)] * x
        zv = z_ref[...]
        yg = y_ref[...] * (zv * _sig(zv))
        r = lax.rsqrt(jnp.mean(yg * yg, axis=-1, keepdims=True) + EPS)
        yn_ref[...] = (yg * r * ng_ref[...]).astype(BF16)

    def row(width, off_blocks):
        return pl.BlockSpec((CH, width), lambda g, b, c: (b * nc + c, off_blocks + g))

    return pl.pallas_call(
        body, grid=(NG, bl, nc),
        in_specs=[row(GW, 0), row(NS, DI // NS), row(NS, DI // NS + NG), row(GW, O_Z // GW), row(128, 0),
                  pl.BlockSpec((None, 8, CH), lambda g, b, c: (g, 0, b * nc + c)),
                  pl.BlockSpec((1, 128), lambda g, b, c: (0, g)),
                  pl.BlockSpec((None, 8, 1), lambda g, b, c: (g, 0, 0)),
                  pl.BlockSpec((1, GW), lambda g, b, c: (0, g)), pl.BlockSpec((1, GW), lambda g, b, c: (0, g))],
        out_specs=[row(GW, 0), row(GW, 0),
                   pl.BlockSpec((None, 4, 128, NS), lambda g, b, c: (b * nc + c, g, 0, 0))],
        out_shape=[jax.ShapeDtypeStruct((T, DI), F32), jax.ShapeDtypeStruct((T, DI), BF16),
                   jax.ShapeDtypeStruct((bl * nc, 16, 128, NS), F32)],
        scratch_shapes=[pltpu.VMEM((4, 128, NS), F32)],
        compiler_params=_cp("parallel", "parallel", "arbitrary"), name=name,
    )(xc, xc, xc, pp, dtg, dtt, arow, acol, dexp, ng)


def _ssd_bwd(dyn, y, xc, pp, dtg, dtt, arow, acol, dexp, ng, prev, bl, s, *, name):
    nc = s // CH
    T = bl * s

    def rsum(v):
        return jnp.sum(v, axis=1, keepdims=True)

    def asum(v):
        return jnp.sum(rsum(v), axis=0, keepdims=True)

    def body(dyn_ref, y_ref, xs_ref, bm_ref, cm_ref, z_ref, dt_ref, dtt_ref, arow_ref, acol_ref, dexp_ref, ng_ref,
             prev_ref, dz_ref, dxs_ref, db_ref, dc_ref, ddt_ref, dng_ref, dd_ref, dal_ref, dst_ref):
        @pl.when((pl.program_id(1) == 0) & (pl.program_id(2) == 0))
        def _():
            dng_ref[...] = jnp.zeros_like(dng_ref)
            dd_ref[...] = jnp.zeros_like(dd_ref)
            dal_ref[...] = jnp.zeros_like(dal_ref)

        @pl.when(pl.program_id(2) == 0)
        def _():
            dst_ref[...] = jnp.zeros_like(dst_ref)

        yv, zv = y_ref[...], z_ref[...]
        sz = _sig(zv)
        silu = zv * sz
        yg = yv * silu
        r = lax.rsqrt(jnp.mean(yg * yg, axis=-1, keepdims=True) + EPS)
        yh = yg * r
        dynv = dyn_ref[...]
        dng_ref[...] += jnp.sum(dynv * yh, axis=0, keepdims=True)
        dyh = dynv * ng_ref[...]
        dyg = r * (dyh - yh * jnp.mean(dyh * yh, axis=-1, keepdims=True))
        dz_ref[...] = (dyg * yv * (sz * (1.0 + zv * (1.0 - sz)))).astype(BF16)
        dxs_ref[...] = dyg * silu
        dd_ref[...] += jnp.sum(dxs_ref[...] * xs_ref[...], axis=0, keepdims=True)

        dt = dt_ref[...]
        arow_v = -jnp.exp(arow_ref[...])
        tril, triu, acs_col, acs_row = _ssd_common(dt, dtt_ref[...], arow_v, -jnp.exp(acol_ref[...]))
        bm, cm = bm_ref[...].astype(MXU), cm_ref[...].astype(MXU)
        cb, cbt = _nt(cm, bm), _nt(bm, cm)
        lane = lax.broadcasted_iota(jnp.int32, (CH, 128), 1)
        rowi = lax.broadcasted_iota(jnp.int32, (CH, 128), 0)
        lo = lane < HD
        sub_lo = lax.broadcasted_iota(jnp.int32, (128, NS), 0) < HD
        dcb = jnp.zeros((CH, CH), F32)
        dcbt = jnp.zeros((CH, CH), F32)
        dc_acc = jnp.zeros((CH, NS), F32)
        db_acc = jnp.zeros((CH, NS), F32)
        dacs = jnp.zeros((CH, 128), F32)
        ddtx = jnp.zeros((CH, 128), F32)
        for q in range(4):
            ha, hb = 2 * q, 2 * q + 1
            t = _pair_terms(q, dt, acs_col, acs_row, tril, triu, lo, True)
            x = xs_ref[:, 128 * q:128 * (q + 1)]
            dy = dxs_ref[:, 128 * q:128 * (q + 1)]
            xd = x * t["dtsel"]
            xd_m, dy_m = xd.astype(MXU), dy.astype(MXU)
            xd_lo, xd_hi = jnp.where(lo, xd, 0.0).astype(MXU), jnp.where(lo, 0.0, xd).astype(MXU)
            dy_lo, dy_hi = jnp.where(lo, dy, 0.0).astype(MXU), jnp.where(lo, 0.0, dy).astype(MXU)
            m_a, m_b = cb * t["d_a"], cb * t["d_b"]
            mt_a, mt_b = cbt * t["dt_a"], cbt * t["dt_b"]
            prev = prev_ref[q]
            dnext = dst_ref[q]
            prev_m, dnext_m = prev.astype(MXU), dnext.astype(MXU)
            bds = _nt(bm, dnext_m)
            dxd = _nn(mt_a.astype(MXU), dy_lo) + _nn(mt_b.astype(MXU), dy_hi) + t["fsel"] * bds
            yoff = t["esel"] * _nt(cm, prev_m)
            dye = dy * t["esel"]
            dye_m = dye.astype(MXU)
            xdf_m = (xd * t["fsel"]).astype(MXU)
            dst_ref[q] = dnext * jnp.where(sub_lo, t["g_a"], t["g_b"]) + _tn(dye_m, cm)
            dm_a, dm_b = _nt(dy_lo, xd_m), _nt(dy_hi, xd_m)
            dmt_a, dmt_b = _nt(xd_lo, dy_m), _nt(xd_hi, dy_m)
            dcb = dcb + dm_a * t["d_a"] + dm_b * t["d_b"]
            dcbt = dcbt + dmt_a * t["dt_a"] + dmt_b * t["dt_b"]
            tyf = dy * yoff - t["fsel"] * xd * bds
            tf = t["fsel"] * xd * bds
            ra = rsum(dm_a * m_a) - rsum(dmt_a * mt_a) + rsum(jnp.where(lo, tyf, 0.0))
            rb = rsum(dm_b * m_b) - rsum(dmt_b * mt_b) + rsum(jnp.where(lo, 0.0, tyf))
            dpp = dnext * prev
            ea = asum(jnp.where(lo, tf, 0.0)) + t["g_a"] * asum(jnp.where(sub_lo, dpp, 0.0))
            eb = asum(jnp.where(lo, 0.0, tf)) + t["g_b"] * asum(jnp.where(sub_lo, 0.0, dpp))
            is_last = rowi == CH - 1
            dacs = (dacs + jnp.where(lane == ha, ra + jnp.where(is_last, ea, 0.0), 0.0)
                    + jnp.where(lane == hb, rb + jnp.where(is_last, eb, 0.0), 0.0))
            tx = dxd * x
            ddtx = (ddtx + jnp.where(lane == ha, rsum(jnp.where(lo, tx, 0.0)), 0.0)
                    + jnp.where(lane == hb, rsum(jnp.where(lo, 0.0, tx)), 0.0))
            dxs_ref[:, 128 * q:128 * (q + 1)] = dxd * t["dtsel"] + dexp_ref[:, 128 * q:128 * (q + 1)] * dy
            dc_acc = dc_acc + _nn(dye_m, prev_m)
            db_acc = db_acc + _nn(xdf_m, dnext_m)
        dc_ref[...] = dc_acc + _nn(dcb.astype(MXU), bm)
        db_ref[...] = db_acc + _nn(dcbt.astype(MXU), cm)
        dla = _nn(triu.astype(F32), dacs, HI)
        ddt_ref[...] = dla * arow_v + ddtx
        dal_ref[...] += jnp.sum(dla * dt, axis=0, keepdims=True) * arow_v

    def row(width, off_blocks):
        return pl.BlockSpec((CH, width), lambda g, b, c: (b * nc + nc - 1 - c, off_blocks + g))

    gvec = pl.BlockSpec((1, GW), lambda g, b, c: (0, g))
    hvec = pl.BlockSpec((1, 128), lambda g, b, c: (0, g))
    return pl.pallas_call(
        body, grid=(NG, bl, nc),
        in_specs=[row(GW, 0), row(GW, 0), row(GW, 0), row(NS, DI // NS), row(NS, DI // NS + NG), row(GW, O_Z // GW),
                  row(128, 0), pl.BlockSpec((None, 8, CH), lambda g, b, c: (g, 0, b * nc + nc - 1 - c)),
                  hvec, pl.BlockSpec((None, 8, 1), lambda g, b, c: (g, 0, 0)), gvec, gvec,
                  pl.BlockSpec((None, 4, 128, NS), lambda g, b, c: (b * nc + nc - 1 - c, g, 0, 0))],
        out_specs=[row(GW, 0), row(GW, 0), row(NS, 0), row(NS, 0), row(128, 0), gvec, gvec, hvec],
        out_shape=[jax.ShapeDtypeStruct((T, DI), BF16), jax.ShapeDtypeStruct((T, DI), F32),
                   jax.ShapeDtypeStruct((T, NG * NS), F32), jax.ShapeDtypeStruct((T, NG * NS), F32),
                   jax.ShapeDtypeStruct((T, NG * 128), F32), jax.ShapeDtypeStruct((1, DI), F32),
                   jax.ShapeDtypeStruct((1, DI), F32), jax.ShapeDtypeStruct((1, NG * 128), F32)],
        scratch_shapes=[pltpu.VMEM((4, 128, NS), F32)],
        compiler_params=_cp("arbitrary", "arbitrary", "arbitrary"), name=name,
    )(dyn, y, xc, xc, xc, pp, dtg, dtt, arow, acol, dexp, ng, prev)


def _merge_fwd(pp, ya, yb, *, name):
    T = ya.shape[0]
    tr = _tile(T, 512, 8)

    def body(ga_ref, gb_ref, ya_ref, yb_ref, o_ref):
        o_ref[...] = (_sig(ga_ref[...]) * ya_ref[...] + _sig(gb_ref[...]) * yb_ref[...]).astype(BF16)

    row = pl.BlockSpec((tr, D), lambda i: (i, 0))
    return pl.pallas_call(
        body, grid=(T // tr,),
        in_specs=[pl.BlockSpec((tr, D), lambda i: (i, O_GA // D)), pl.BlockSpec((tr, D), lambda i: (i, O_GB // D)),
                  row, row],
        out_specs=row, out_shape=jax.ShapeDtypeStruct((T, D), BF16), compiler_params=_cp("parallel"),
        name=name)(pp, pp, ya, yb)


def _merge_bwd(pp, ya, yb, dm, *, name):
    T = ya.shape[0]
    tr = _tile(T, 512, 8)

    def body(ga_ref, gb_ref, ya_ref, yb_ref, dm_ref, dya_ref, dyb_ref, dga_ref, dgb_ref):
        sa, sb, dmv = _sig(ga_ref[...]), _sig(gb_ref[...]), dm_ref[...]
        dya_ref[...] = (dmv * sa).astype(BF16)
        dyb_ref[...] = (dmv * sb).astype(BF16)
        dga_ref[...] = (dmv * ya_ref[...] * (sa * (1.0 - sa))).astype(BF16)
        dgb_ref[...] = (dmv * yb_ref[...] * (sb * (1.0 - sb))).astype(BF16)

    row = pl.BlockSpec((tr, D), lambda i: (i, 0))
    act = jax.ShapeDtypeStruct((T, D), BF16)
    return pl.pallas_call(
        body, grid=(T // tr,),
        in_specs=[pl.BlockSpec((tr, D), lambda i: (i, O_GA // D)), pl.BlockSpec((tr, D), lambda i: (i, O_GB // D)),
                  row, row, row],
        out_specs=[row, row, row, row], out_shape=[act, act, act, act], compiler_params=_cp("parallel"),
        name=name)(pp, pp, ya, yb, dm)


def _softmax_rows(sc):
    e = jnp.exp(sc - jnp.max(sc, axis=-1, keepdims=True))
    return e / jnp.sum(e, axis=-1, keepdims=True)


def _attn_fwd(q, kv, bl, s, *, name):
    m = kv.shape[0] // bl
    tq = _tile(s, 512)
    nq = s // tq
    scale = 1.0 / math.sqrt(XD)

    def body(q_ref, k_ref, v_ref, o_ref):
        p = _softmax_rows(_nt(q_ref[...], k_ref[...]) * scale)
        o_ref[...] = _nn(p.astype(MXU), v_ref[...]).astype(BF16)

    qspec = pl.BlockSpec((tq, XD), lambda b, h, i: (b * nq + i, h))
    return pl.pallas_call(
        body, grid=(bl, XH, nq),
        in_specs=[qspec, pl.BlockSpec((m, XD), lambda b, h, i: (b, h)),
                  pl.BlockSpec((m, XD), lambda b, h, i: (b, XH + h))],
        out_specs=qspec, out_shape=jax.ShapeDtypeStruct((bl * s, D), BF16),
        compiler_params=_cp("parallel", "parallel", "parallel"), name=name)(q, kv, kv)


def _attn_bwd(q, kv, do, bl, s, *, name):
    m = kv.shape[0] // bl
    tq = _tile(s, 512)
    nq = s // tq
    scale = 1.0 / math.sqrt(XD)

    def body(q_ref, k_ref, v_ref, do_ref, dq_ref, dk_ref, dv_ref):
        @pl.when(pl.program_id(2) == 0)
        def _():
            dk_ref[...] = jnp.zeros_like(dk_ref)
            dv_ref[...] = jnp.zeros_like(dv_ref)

        qv, kvv, vv, dov = q_ref[...], k_ref[...], v_ref[...], do_ref[...]
        p = _softmax_rows(_nt(qv, kvv) * scale)
        dp = _nt(dov, vv)
        ds = (p * (dp - jnp.sum(dp * p, axis=-1, keepdims=True)) * scale).astype(MXU)
        dq_ref[...] = _nn(ds, kvv).astype(BF16)
        dk_ref[...] += _tn(ds, qv)
        dv_ref[...] += _tn(p.astype(MXU), dov)

    qspec = pl.BlockSpec((tq, XD), lambda b, h, i: (b * nq + i, h))
    kspec = pl.BlockSpec((m, XD), lambda b, h, i: (b, h))
    return pl.pallas_call(
        body, grid=(bl, XH, nq),
        in_specs=[qspec, kspec, pl.BlockSpec((m, XD), lambda b, h, i: (b, XH + h)), qspec],
        out_specs=[qspec, kspec, kspec],
        out_shape=[jax.ShapeDtypeStruct((bl * s, D), BF16), jax.ShapeDtypeStruct((bl * m, D), F32),
                   jax.ShapeDtypeStruct((bl * m, D), F32)],
        compiler_params=_cp("parallel", "parallel", "arbitrary"), name=name)(q, kv, kv, do)


def _row_tile(r, c, max_elems=512 * 1024, align=16):
    best = None
    for t in range(align, r + 1, align):
        if r % t == 0 and t * c <= max_elems:
            best = t
    return best if best is not None else r


def _addn(a, others, *, name, also_bf16=False):
    r, c = a.shape
    tr = _row_tile(r, c)
    n = len(others)

    def body(*refs):
        acc = refs[0][...].astype(F32)
        for o_ref in refs[1:1 + n]:
            acc = acc + o_ref[...].astype(F32)
        refs[1 + n][...] = acc
        if also_bf16:
            refs[2 + n][...] = acc.astype(BF16)

    spec = pl.BlockSpec((tr, c), lambda i: (i, 0))
    shapes = [jax.ShapeDtypeStruct((r, c), F32)] + ([jax.ShapeDtypeStruct((r, c), BF16)] if also_bf16 else [])
    out = pl.pallas_call(
        body, grid=(r // tr,), in_specs=[spec] * (1 + n), out_specs=[spec] * len(shapes), out_shape=shapes,
        compiler_params=_cp("parallel"), name=name)(a, *others)
    return out if also_bf16 else out[0]


def _sum_leading(a, *, name):
    n, r, c = a.shape

    def body(a_ref, o_ref):
        acc = a_ref[0]
        for i in range(1, n):
            acc = acc + a_ref[i]
        o_ref[...] = acc

    return pl.pallas_call(body, out_shape=jax.ShapeDtypeStruct((r, c), F32), name=name)(a)


def _adamw_math(wv, gv, mv, vv):
    m2 = ADAM_B1 * mv + (1.0 - ADAM_B1) * gv
    v2 = ADAM_B2 * vv + (1.0 - ADAM_B2) * (gv * gv)
    m_hat = m2 / (1.0 - ADAM_B1 ** ADAM_STEP)
    v_hat = v2 / (1.0 - ADAM_B2 ** ADAM_STEP)
    return -ADAM_LR * (m_hat / (jnp.sqrt(v_hat) + ADAM_EPS) + ADAM_WD * wv), m2, v2


def _adamw(w, g, m, v, *, name):
    r, c = w.shape
    tr = _row_tile(r, c, align=8)

    def body(w_ref, g_ref, m_ref, v_ref, d_ref, mo_ref, vo_ref):
        d_ref[...], mo_ref[...], vo_ref[...] = _adamw_math(w_ref[...], g_ref[...], m_ref[...], v_ref[...])

    spec = pl.BlockSpec((tr, c), lambda i: (i, 0))
    shp = jax.ShapeDtypeStruct((r, c), F32)
    return pl.pallas_call(
        body, grid=(r // tr,), in_specs=[spec] * 4, out_specs=[spec] * 3, out_shape=[shp] * 3,
        compiler_params=_cp("parallel"), name=name)(w, g, m, v)


def _adamw_halves(w, g_mine, g_other, m, v, c, *, name):
    r, cols = w.shape
    h = r // 2
    tr = _row_tile(h, cols, align=8)
    nh = h // tr

    def body(c_ref, w_ref, gm_ref, go_ref, m_ref, v_ref, g_ref, d_ref, mo_ref, vo_ref):
        gv = jnp.where(pl.program_id(0) // nh == c_ref[0], gm_ref[...], go_ref[...])
        g_ref[...] = gv
        d_ref[...], mo_ref[...], vo_ref[...] = _adamw_math(w_ref[...], gv, m_ref[...], v_ref[...])

    full = pl.BlockSpec((tr, cols), lambda i, c_: (i, 0))
    half = pl.BlockSpec((tr, cols), lambda i, c_: (i % nh, 0))
    shp = jax.ShapeDtypeStruct((r, cols), F32)
    return pl.pallas_call(
        body,
        grid_spec=pltpu.PrefetchScalarGridSpec(num_scalar_prefetch=1, grid=(2 * nh,),
                                               in_specs=[full, half, half, full, full], out_specs=[full] * 4),
        out_shape=[shp] * 4, compiler_params=_cp("parallel"), name=name,
    )(jnp.reshape(c, (1,)).astype(jnp.int32), w, g_mine, g_other, m, v)


def _flip(i, d):
    return 1 - i if d else i


def _comm(name, ins, out_shapes, n_remote, n_local, plan, aliases=None):
    n_in, n_out = len(ins), len(out_shapes)

    def body(*refs):
        in_refs, out_refs = refs[:n_in], refs[n_in:n_in + n_out]
        send_sems, recv_sems = refs[n_in + n_out], refs[n_in + n_out + 1]
        x, y, c = lax.axis_index("x"), lax.axis_index("y"), lax.axis_index("c")
        remote, local = plan(in_refs, out_refs, x, y, c)
        assert len(remote) == n_remote and len(local) == n_local
        copies = []
        if n_local:
            loc_sems = refs[n_in + n_out + 2]
            copies += [pltpu.make_async_copy(s_, d_, loc_sems.at[i]) for i, (s_, d_) in enumerate(local)]
        copies += [pltpu.make_async_remote_copy(src_ref=s_, dst_ref=d_, send_sem=send_sems.at[i],
                                                recv_sem=recv_sems.at[i], device_id=dev, device_id_type=MESH)
                   for i, (s_, d_, dev) in enumerate(remote)]
        for cp in copies:
            cp.start()
        for cp in copies:
            cp.wait()

    hbm = pl.BlockSpec(memory_space=pl.ANY)
    scratch = [pltpu.SemaphoreType.DMA((n_remote,)), pltpu.SemaphoreType.DMA((n_remote,))]
    if n_local:
        scratch.append(pltpu.SemaphoreType.DMA((n_local,)))
    return pl.pallas_call(
        body, in_specs=[hbm] * n_in, out_specs=[hbm] * n_out, out_shape=out_shapes, scratch_shapes=scratch,
        input_output_aliases=aliases or {}, compiler_params=pltpu.CompilerParams(has_side_effects=True),
        name=name)(*ins)


def _gather_weights(shards):
    n = len(shards)
    outs = [jax.ShapeDtypeStruct((4,) + s.shape, s.dtype) for s in shards]

    def plan_ici(in_refs, out_refs, x, y, c):
        k = 2 * x + y
        remote, local = [], []
        for w_ref, o_ref in zip(in_refs, out_refs):
            h = w_ref.shape[0] // 2
            rows = pl.ds(c * h, h)
            local.append((w_ref.at[rows], o_ref.at[k, rows]))
            for dx, dy in CHIP_FLIPS:
                remote.append((w_ref.at[rows], o_ref.at[k, rows], (_flip(x, dx), _flip(y, dy), c)))
        return remote, local

    def plan_d2d(in_refs, out_refs, x, y, c):
        remote = []
        for o_ref in out_refs:
            h = o_ref.shape[1] // 2
            half = o_ref.at[pl.ds(0, 4), pl.ds(c * h, h)]
            remote.append((half, half, (x, y, 1 - c)))
        return remote, []

    part = _comm("gather_w_ici", shards, outs, 3 * n, n, plan_ici)
    return _comm("gather_w_d2d", part, outs, n, 0, plan_d2d, aliases={i: i for i in range(n)})


def _reduce_scatter_grads(grads):
    n = len(grads)
    x, y, c = lax.axis_index("x"), lax.axis_index("y"), lax.axis_index("c")
    halves = [g.shape[1] // 2 for g in grads]
    mine = [lax.dynamic_slice_in_dim(g, c * h, h, axis=1) for g, h in zip(grads, halves)]
    send_a = [lax.dynamic_slice_in_dim(g, (1 - c) * h, h, axis=1).astype(BF16) for g, h in zip(grads, halves)]

    def plan_a(in_refs, out_refs, x_, y_, c_):
        return [(i_, o_, (x_, y_, 1 - c_)) for i_, o_ in zip(in_refs, out_refs)], []

    recv_a = _comm("rs_pair", send_a, [jax.ShapeDtypeStruct(s.shape, BF16) for s in send_a], n, 0, plan_a)
    pair, pair_b = [], []
    for i, (mi, ra) in enumerate(zip(mine, recv_a)):
        four, h, cols = mi.shape
        p32, p16 = _addn(mi.reshape(four * h, cols), [ra.reshape(four * h, cols)], name=f"rs_pair_sum_{i}",
                         also_bf16=True)
        pair.append(p32.reshape(four, h, cols))
        pair_b.append(p16.reshape(four, h, cols))

    def plan_b(in_refs, out_refs, x_, y_, c_):
        remote = []
        for i_, o_ in zip(in_refs, out_refs):
            for j, (dx, dy) in enumerate(CHIP_FLIPS):
                fx, fy = _flip(x_, dx), _flip(y_, dy)
                remote.append((i_.at[2 * fx + fy], o_.at[j], (fx, fy, c_)))
        return remote, []

    recv_b = _comm("rs_chips", pair_b, [jax.ShapeDtypeStruct((3,) + p.shape[1:], BF16) for p in pair_b], 3 * n, 0,
                   plan_b)
    k = 2 * x + y
    tot = [_addn(lax.dynamic_index_in_dim(p, k, 0, keepdims=False), [rb[0], rb[1], rb[2]], name=f"rs_chip_sum_{i}")
           for i, (p, rb) in enumerate(zip(pair, recv_b))]

    other = _comm("rs_halves", tot, [jax.ShapeDtypeStruct(t.shape, F32) for t in tot], n, 0, plan_a)
    return tot, other


def _gather_all(vec, *, name):
    out = jax.ShapeDtypeStruct((8,) + vec.shape, vec.dtype)

    def plan(in_refs, out_refs, x, y, c):
        me = 4 * x + 2 * y + c
        remote = [(in_refs[0], out_refs[0].at[me], (_flip(x, dx), _flip(y, dy), _flip(c, dc)))
                  for dx in (0, 1) for dy in (0, 1) for dc in (0, 1) if (dx, dy, dc) != (0, 0, 0)]
        return remote, [(in_refs[0], out_refs[0].at[me])]

    return _comm(name, [vec], [out], 7, 1, plan)[0]


def _pack(parts):
    flat = [p.reshape(-1).astype(F32) for p in parts]
    total = sum(f.shape[0] for f in flat)
    n = -(-total // 1024) * 128
    vec = jnp.concatenate(flat + [jnp.zeros((8 * n - total,), F32)]).reshape(8, n)
    offs, o = [], 0
    for f in flat:
        offs.append((o, f.shape[0]))
        o += f.shape[0]
    return vec, offs


def _unpack(vec, offs, shapes):
    flat = vec.reshape(-1)
    return [flat[o:o + n].reshape(s) for (o, n), s in zip(offs, shapes)]


BIG = (("ffn1_w_gate_up", "col"), ("ffn1_w_down", "row"), ("w_in", "col"), ("w_out_a", "row"), ("w_out_ssm", "row"),
       ("w_mix_out", "row"), ("w_q", "row"), ("w_kv", "col"), ("w_o_x", "row"), ("ffn2_w_gate_up", "col"),
       ("ffn2_w_down", "row"))
SMALL = ("ffn1_norm", "mix_norm", "conv_a_w", "ssm_conv_w", "ssm_conv_b", "ssm_dt_bias", "ssm_a_log", "ssm_d",
         "ssm_norm", "xattn_norm", "mem_norm", "ffn2_norm", "final_norm")
WEIGHTS = ("ffn1_norm", "ffn1_w_gate_up", "ffn1_w_down", "mix_norm", "w_in", "conv_a_w", "w_out_a", "ssm_conv_w",
           "ssm_conv_b", "ssm_dt_bias", "ssm_a_log", "ssm_d", "ssm_norm", "w_out_ssm", "w_mix_out", "xattn_norm",
           "mem_norm", "w_q", "w_kv", "w_o_x", "ffn2_norm", "ffn2_w_gate_up", "ffn2_w_down", "final_norm")


def _full_weight(g, kind):
    four, r, cols = g.shape
    if kind == "row":
        return g.reshape(four * r, cols)
    return jnp.transpose(g, (1, 0, 2)).reshape(r, four * cols)


def _shard_major(dw, kind):
    if isinstance(dw, tuple):
        return jnp.concatenate([_shard_major(p, "col2") for p in dw], axis=0)
    if kind == "col2":
        return jnp.transpose(dw.reshape(dw.shape[0], 2, dw.shape[1] // 2), (1, 0, 2))
    if kind == "row":
        return dw.reshape(4, dw.shape[0] // 4, dw.shape[1])
    return jnp.transpose(dw.reshape(dw.shape[0], 4, dw.shape[1] // 4), (1, 0, 2))


def _pad_rows8(w):
    return jnp.concatenate([w, jnp.zeros((8 - w.shape[0], w.shape[1]), w.dtype)], axis=0)


def _group_lanes(v):
    r = v.shape[0]
    return jnp.pad(v.reshape(r, NG, NH // NG), ((0, 0), (0, 0), (0, 128 - NH // NG))).reshape(r, NG * 128)


def _ungroup_lanes(v):
    r = v.shape[0]
    return v.reshape(r, NG, 128)[:, :, :NH // NG].reshape(r, NH)


def _local_step(wfull, small, x, mem, target):
    bl, s, _ = x.shape
    T = bl * s
    x2, t2 = x.reshape(T, D), target.reshape(T, D)
    mem2 = mem.reshape(-1, D)
    g = {}

    def gain(name):
        return small[name].reshape(1, -1)

    def ffn_fwd(h, norm, wgu, wd, tag):
        n = _norm_fwd(h, gain(norm), name=f"{tag}_norm")
        gate, up, a = _gate_up_fwd(n, wfull[wgu], name=f"{tag}_gate_up")
        out = _mm(a, wfull[wd], "nn", tk=DFF, scale=FFN_RES, residual=h, name=f"{tag}_down")
        return out, (n, gate, up, a)

    def ffn_bwd(dh, h, norm, wgu, wd, saved, tag):
        n, gate, up, a = saved
        dgate, dup = _act_bwd(dh, wfull[wd], gate, up, FFN_RES, name=f"{tag}_d_act")
        g[wd] = _mm(a, dh, "tn", tm=1408, scale=FFN_RES, name=f"{tag}_d_w_down")
        dn = _mm(dgate, wfull[wgu], "nt", a2=dup, tk=1408, name=f"{tag}_d_norm_out")
        g[wgu] = (_mm(n, dgate, "tn", tn=1408, name=f"{tag}_d_w_gate"), _mm(n, dup, "tn", tn=1408, name=f"{tag}_d_w_up"))
        dh_in, g[norm] = _norm_bwd(h, gain(norm), dn, dh, name=f"{tag}_d_norm")
        return dh_in

    h1, ffn1_saved = ffn_fwd(x2, "ffn1_norm", "ffn1_w_gate_up", "ffn1_w_down", "ffn1")
    u = _norm_fwd(h1, gain("mix_norm"), name="mix_norm")
    pp = _mm(u, wfull["w_in"], "nn", tn=1152, name="in_proj")
    wa8 = _pad_rows8(small["conv_a_w"])
    ws8 = _pad_rows8(small["ssm_conv_w"])
    conv_b = gain("ssm_conv_b")
    bias128 = jnp.pad(gain("ssm_dt_bias"), ((0, 0), (0, 128 - NH)))
    ya_pre = _conv_a_fwd(pp, wa8, bl, s, name="conv_a")
    xc = _conv_ssm_fwd(pp, ws8, conv_b, bl, s, name="conv_ssm")
    dt = _dt_fwd(pp, bias128, name="dt")
    dtg = _group_lanes(dt[:, :NH])
    dtt = dt[:, :NH].T.reshape(NG, NH // NG, T)
    alog = gain("ssm_a_log")
    arow, acol = _group_lanes(alog), alog.reshape(NG, NH // NG, 1)
    dexp = jnp.repeat(gain("ssm_d"), HD, axis=1)
    ng = gain("ssm_norm")
    y, yn, prev = _ssd_fwd(xc, pp, dtg, dtt, arow, acol, dexp, ng, bl, s, name="ssd")
    ya = _mm(ya_pre, wfull["w_out_a"], "nn", tn=1024, name="out_a")
    yb = _mm(yn, wfull["w_out_ssm"], "nn", tn=1024, tk=DI, name="out_ssm")
    merged = _merge_fwd(pp, ya, yb, name="merge")
    h2 = _mm(merged, wfull["w_mix_out"], "nn", tn=1024, residual=h1, name="mix_out")
    un = _norm_fwd(h2, gain("xattn_norm"), name="xattn_norm")
    q = _mm(un, wfull["w_q"], "nn", tn=1024, out_dtype=BF16, name="q_proj")
    mn = _norm_fwd(mem2, gain("mem_norm"), name="mem_norm")
    kv = _mm(mn, wfull["w_kv"], "nn", tn=1024, out_dtype=BF16, name="kv_proj")
    o = _attn_fwd(q, kv, bl, s, name="attn")
    h3 = _mm(o, wfull["w_o_x"], "nn", tn=1024, residual=h2, name="attn_out")
    h4, ffn2_saved = ffn_fwd(h3, "ffn2_norm", "ffn2_w_gate_up", "ffn2_w_down", "ffn2")
    sq_err, dh4, dgf = _final_loss(h4, gain("final_norm"), t2, name="final_loss")
    g["final_norm"] = dgf

    dh3 = ffn_bwd(dh4, h3, "ffn2_norm", "ffn2_w_gate_up", "ffn2_w_down", ffn2_saved, "ffn2")
    do = _mm(dh3, wfull["w_o_x"], "nt", tn=1024, out_dtype=BF16, name="d_attn_o")
    g["w_o_x"] = _mm(o, dh3, "tn",name="d_w_o_x")
    dq, dk, dv = _attn_bwd(q, kv, do, bl, s, name="d_attn")
    dun = _mm(dq, wfull["w_q"], "nt", tn=1024, name="d_xattn_norm_out")
    g["w_q"] = _mm(un, dq, "tn",name="d_w_q")
    dkv = jnp.concatenate([dk, dv], axis=1)
    dmn = _mm(dkv, wfull["w_kv"], "nt", tn=1024, tk=2 * D, name="d_mem_norm_out")
    g["w_kv"] = _mm(mn, dkv, "tn",name="d_w_kv")
    _, g["mem_norm"] = _norm_bwd(mem2, gain("mem_norm"), dmn, None, name="d_mem_norm")
    dh2, g["xattn_norm"] = _norm_bwd(h2, gain("xattn_norm"), dun, dh3, name="d_xattn_norm")
    dmerged = _mm(dh2, wfull["w_mix_out"], "nt", tn=1024, name="d_merged")
    g["w_mix_out"] = _mm(merged, dh2, "tn",name="d_w_mix_out")
    dya, dyb, dga, dgb = _merge_bwd(pp, ya, yb, dmerged, name="d_merge")
    dya_pre = _mm(dya, wfull["w_out_a"], "nt", tn=1024, name="d_conv_a_out")
    g["w_out_a"] = _mm(ya_pre, dya, "tn",name="d_w_out_a")
    dyn = _mm(dyb, wfull["w_out_ssm"], "nt", tn=DI, name="d_ssd_out")
    g["w_out_ssm"] = _mm(yn, dyb, "tn",name="d_w_out_ssm")
    d_ab, d_ac, d_av, dwa8 = _conv_a_bwd(pp, wa8, dya_pre, bl, s, name="d_conv_a")
    g["conv_a_w"] = dwa8[:3]
    dz, dxs, dbm, dcm, ddtg, g["ssm_norm"], ddexp, dalg = _ssd_bwd(
        dyn, y, xc, pp, dtg, dtt, arow, acol, dexp, ng, prev, bl, s, name="d_ssd")
    g["ssm_d"] = ddexp.reshape(NH, HD).sum(axis=1).reshape(1, NH)
    g["ssm_a_log"] = _ungroup_lanes(dalg)
    conv_parts = [_conv_ssm_bwd(pp, ws8, conv_b, dpart, off, bl, s, name=f"d_conv_ssm_{tag}")
                  for dpart, off, tag in ((dxs, 0, "x"), (dbm, DI, "b"), (dcm, DI + NG * NS, "c"))]
    g["ssm_conv_w"] = jnp.concatenate([p[1] for p in conv_parts], axis=1)[:4]
    g["ssm_conv_b"] = jnp.concatenate([p[2] for p in conv_parts], axis=1)
    ddt = jnp.pad(_ungroup_lanes(ddtg), ((0, 0), (0, 128 - NH)))
    ddt_raw, dbias = _dt_bwd(pp, bias128, ddt, name="d_dt")
    g["ssm_dt_bias"] = dbias[:, :NH]
    dpp = jnp.concatenate([d_ab, d_ac, d_av, dz] + [p[0] for p in conv_parts] + [dga, dgb, ddt_raw], axis=1)
    du = _mm(dpp, wfull["w_in"], "nt", tk=3456, name="d_mix_norm_out")
    g["w_in"] = _mm(u, dpp, "tn", tn=1152, name="d_w_in")
    dh1, g["mix_norm"] = _norm_bwd(h1, gain("mix_norm"), du, dh2, name="d_mix_norm")
    dx = ffn_bwd(dh1, x2, "ffn1_norm", "ffn1_w_gate_up", "ffn1_w_down", ffn1_saved, "ffn1")
    return sq_err, dx, g


def _pad_w_in(w):
    return jnp.concatenate([w[:, :O_GA], w[:, O_GA + NH:], w[:, O_GA:O_GA + NH],
                            jnp.zeros((w.shape[0], NPP - NIN), w.dtype)], axis=1)


def _unpad_w_in(w):
    return jnp.concatenate([w[:, :O_GA], w[:, O_DT:O_DT + NH], w[:, O_GA:O_DT]], axis=1)


def kernel(x, mem, ffn1_norm, ffn1_w_gate_up, ffn1_w_down, mix_norm, w_in, conv_a_w, w_out_a, ssm_conv_w, ssm_conv_b, ssm_dt_bias, ssm_a_log, ssm_d, ssm_norm, w_out_ssm, w_mix_out, xattn_norm, mem_norm, w_q, w_kv, w_o_x, ffn2_norm, ffn2_w_gate_up, ffn2_w_down, final_norm, loss_target, m_ffn1_norm, m_ffn1_w_gate_up, m_ffn1_w_down, m_mix_norm, m_w_in, m_conv_a_w, m_w_out_a, m_ssm_conv_w, m_ssm_conv_b, m_ssm_dt_bias, m_ssm_a_log, m_ssm_d, m_ssm_norm, m_w_out_ssm, m_w_mix_out, m_xattn_norm, m_mem_norm, m_w_q, m_w_kv, m_w_o_x, m_ffn2_norm, m_ffn2_w_gate_up, m_ffn2_w_down, m_final_norm, v_ffn1_norm, v_ffn1_w_gate_up, v_ffn1_w_down, v_mix_norm, v_w_in, v_conv_a_w, v_w_out_a, v_ssm_conv_w, v_ssm_conv_b, v_ssm_dt_bias, v_ssm_a_log, v_ssm_d, v_ssm_norm, v_w_out_ssm, v_w_mix_out, v_xattn_norm, v_mem_norm, v_w_q, v_w_kv, v_w_o_x, v_ffn2_norm, v_ffn2_w_gate_up, v_ffn2_w_down, v_final_norm):
    a = dict(locals())
    xi, yi = lax.axis_index("x"), lax.axis_index("y")
    k = 2 * xi + yi

    shards = [a[n][0].astype(BF16) for n, _ in BIG]
    gathered = _gather_weights(shards)
    wfull = {n: _full_weight(gw, kind) for (n, kind), gw in zip(BIG, gathered)}
    wfull["w_in"] = _pad_w_in(wfull["w_in"])
    conv_vec, conv_offs = _pack([a["conv_a_w"], a["ssm_conv_w"]])
    conv_all = _gather_all(conv_vec, name="gather_conv_w")
    conv_sh = [_unpack(conv_all[2 * kk], conv_offs, [a["conv_a_w"].shape[1:], a["ssm_conv_w"].shape[1:]])
               for kk in range(4)]
    small = {n: a[n] for n in SMALL}
    small["conv_a_w"] = jnp.concatenate([cs[0] for cs in conv_sh], axis=1)
    small["ssm_conv_w"] = jnp.concatenate([cs[1] for cs in conv_sh], axis=1)

    sq_err, dx, g = _local_step(wfull, small, x, mem, loss_target)
    loss = lax.psum(0.5 / D * jnp.sum(sq_err), ("x", "y", "c"))

    g["w_in"] = _unpad_w_in(g["w_in"])
    g_mine, g_other = _reduce_scatter_grads([_shard_major(g[n], kind) for n, kind in BIG])
    ci = lax.axis_index("c")
    out = {}
    for (n, _), gm, go in zip(BIG, g_mine, g_other):
        res = _adamw_halves(a[n][0], gm, go, a["m_" + n][0], a["v_" + n][0], ci, name=f"adamw_{n}")
        out[n] = tuple(t.reshape(a[n].shape) for t in res)

    full_shapes = [g[n].shape for n in SMALL]
    gvec, goffs = _pack([g[n] for n in SMALL])
    gsum = _sum_leading(_gather_all(gvec, name="gather_small_grads"), name="sum_small_grads")
    gsmall = dict(zip(SMALL, _unpack(gsum, goffs, full_shapes)))
    for n in ("conv_a_w", "ssm_conv_w"):
        width = a[n].shape[2]
        gsmall[n] = lax.dynamic_slice_in_dim(gsmall[n], k * width, width, axis=1)
    local_shapes = [a[n].shape for n in SMALL]
    packs = [_pack([t[n] for n in SMALL]) for t in
             ({n: a[n] for n in SMALL}, gsmall, {n: a["m_" + n] for n in SMALL}, {n: a["v_" + n] for n in SMALL})]
    offs = packs[0][1]
    res = _adamw(*[p[0] for p in packs], name="adamw_small")
    unp = [_unpack(r, offs, local_shapes) for r in res]
    for i, n in enumerate(SMALL):
        out[n] = (gsmall[n].reshape(a[n].shape), unp[0][i], unp[1][i], unp[2][i])

    grad_x = dx.reshape(x.shape)
    return (loss, grad_x, *[out[n][0] for n in WEIGHTS], *[out[n][1] for n in WEIGHTS],
            *[out[n][2] for n in WEIGHTS], *[out[n][3] for n in WEIGHTS])
```

```python
import functools
import math

import jax
import jax.numpy as jnp
from jax import lax
from jax.experimental import pallas as pl
from jax.experimental.pallas import tpu as pltpu

F32 = jnp.float32
BF16 = jnp.bfloat16
MXU = jnp.bfloat16
HI = lax.Precision.HIGHEST

D = 1024
DFF = 2816
DI = 2048
NH, HD, NG, NS, CH = 32, 64, 4, 128, 128
GW = DI // NG
XH, XD = 4, 256
EPS = 1e-6
NEG = -1e30
O_AB, O_AC, O_AV, O_Z, O_XBC, O_GA, O_GB, O_DT, NPP = 0, 1024, 2048, 3072, 5120, 8192, 9216, 10240, 10368
NIN = 10272
FFN_RES = 0.5
ADAM_LR, ADAM_B1, ADAM_B2, ADAM_EPS, ADAM_WD, ADAM_STEP = 0.001, 0.9, 0.999, 1e-08, 0.01, 10
VMEM_LIMIT = 56 * 1024 * 1024
MESH = pl.DeviceIdType.MESH
CHIP_FLIPS = ((1, 0), (0, 1), (1, 1))


def _cp(*sem):
    return pltpu.CompilerParams(dimension_semantics=sem, vmem_limit_bytes=VMEM_LIMIT)


def _tile(n, pref, align=128):
    if n <= pref:
        return n
    t = (pref // align) * align
    while t >= align:
        if n % t == 0:
            return t
        t -= align
    raise ValueError((n, pref))


def _dot(a, b, dims, prec=None):
    return lax.dot_general(a, b, (dims, ((), ())), preferred_element_type=F32, precision=prec)


def _nn(a, b, prec=None):
    return _dot(a, b, ((1,), (0,)), prec)


def _nt(a, b):
    return _dot(a, b, ((1,), (1,)))


def _tn(a, b):
    return _dot(a, b, ((0,), (0,)))


def _sig(x):
    return jax.nn.sigmoid(x)


def _mm(a, b, mode, *, name, tm=1024, tn=1024, tk=1024, out_dtype=F32, scale=None, residual=None, a2=None):
    if mode == "nn":
        (M, K), (K2, N) = a.shape, b.shape
    elif mode == "nt":
        (M, K), (N, K2) = a.shape, b.shape
        if a2 is not None:
            assert a2.shape == a.shape
            K2 = K2 // 2
    else:
        (K, M), (K2, N) = a.shape, b.shape
    assert K == K2, (name, a.shape, b.shape)
    tm, tn, tk = _tile(M, tm), _tile(N, tn), _tile(K, tk)
    nk = K // tk
    if mode == "nn":
        a_spec = pl.BlockSpec((tm, tk), lambda i, j, k: (i, k))
        b_spec = pl.BlockSpec((tk, tn), lambda i, j, k: (k, j))
        dims = ((1,), (0,))
    elif mode == "nt":
        a_spec = pl.BlockSpec((tm, tk), lambda i, j, k: (i, k))
        b_spec = pl.BlockSpec((tn, tk), lambda i, j, k: (j, k))
        dims = ((1,), (1,))
    else:
        a_spec = pl.BlockSpec((tk, tm), lambda i, j, k: (k, i))
        b_spec = pl.BlockSpec((tk, tn), lambda i, j, k: (k, j))
        dims = ((0,), (0,))
    o_spec = pl.BlockSpec((tm, tn), lambda i, j, k: (i, j))
    has_res = residual is not None

    def finish(acc, r_ref, o_ref):
        if scale is not None:
            acc = acc * scale
        if has_res:
            acc = acc + r_ref[...]
        o_ref[...] = acc.astype(out_dtype)

    dual = a2 is not None
    n_in = 2 + 2 * dual + has_res

    def body(*refs):
        a_ref, b_ref = refs[0], refs[1]
        r_ref = refs[n_in - 1] if has_res else None
        o_ref = refs[n_in]
        part = _dot(a_ref[...].astype(MXU), b_ref[...].astype(MXU), dims)
        if dual:
            part = part + _dot(refs[2][...].astype(MXU), refs[3][...].astype(MXU), dims)
        if nk == 1:
            finish(part, r_ref, o_ref)
            return
        acc_ref = refs[-1]
        k = pl.program_id(2)

        @pl.when(k == 0)
        def _():
            acc_ref[...] = part

        @pl.when(k > 0)
        def _():
            acc_ref[...] += part

        @pl.when(k == nk - 1)
        def _():
            finish(acc_ref[...], r_ref, o_ref)

    ins, in_specs = [a, b], [a_spec, b_spec]
    if dual:
        ins += [a2, b]
        in_specs += [a_spec, pl.BlockSpec((tn, tk), lambda i, j, k: (j, k + nk))]
    if has_res:
        ins.append(residual)
        in_specs.append(o_spec)
    return pl.pallas_call(
        body, grid=(M // tm, N // tn, nk), in_specs=in_specs, out_specs=o_spec,
        out_shape=jax.ShapeDtypeStruct((M, N), out_dtype),
        scratch_shapes=[pltpu.VMEM((tm, tn), F32)] if nk > 1 else [],
        compiler_params=_cp("parallel", "parallel", "arbitrary"), name=name)(*ins)


def _norm_fwd(x, g, *, name):
    T, d = x.shape
    tr = _tile(T, 512, 8)

    def body(x_ref, g_ref, o_ref):
        xv = x_ref[...]
        r = lax.rsqrt(jnp.mean(xv * xv, axis=-1, keepdims=True) + EPS)
        o_ref[...] = (xv * r * g_ref[...]).astype(BF16)

    return pl.pallas_call(
        body, grid=(T // tr,),
        in_specs=[pl.BlockSpec((tr, d), lambda i: (i, 0)), pl.BlockSpec((1, d), lambda i: (0, 0))],
        out_specs=pl.BlockSpec((tr, d), lambda i: (i, 0)),
        out_shape=jax.ShapeDtypeStruct((T, d), BF16), compiler_params=_cp("parallel"), name=name)(x, g)


def _norm_bwd(x, g, dn, dres, *, name):
    T, d = x.shape
    tr = _tile(T, 512, 8)
    has_res = dres is not None

    def body(*refs):
        x_ref, g_ref, dn_ref = refs[:3]
        dr_ref = refs[3] if has_res else None
        dx_ref, dg_ref = refs[-2], refs[-1]

        @pl.when(pl.program_id(0) == 0)
        def _():
            dg_ref[...] = jnp.zeros_like(dg_ref)

        xv = x_ref[...]
        dnv = dn_ref[...].astype(F32)
        r = lax.rsqrt(jnp.mean(xv * xv, axis=-1, keepdims=True) + EPS)
        xh = xv * r
        dg_ref[...] += jnp.sum(dnv * xh, axis=0, keepdims=True)
        dxh = dnv * g_ref[...]
        dx = r * (dxh - xh * jnp.mean(dxh * xh, axis=-1, keepdims=True))
        if has_res:
            dx = dx + dr_ref[...]
        dx_ref[...] = dx

    row = pl.BlockSpec((tr, d), lambda i: (i, 0))
    vec = pl.BlockSpec((1, d), lambda i: (0, 0))
    ins = [x, g, dn] + ([dres] if has_res else [])
    return pl.pallas_call(
        body, grid=(T // tr,), in_specs=[row, vec, row] + ([row] if has_res else []),
        out_specs=[row, vec],
        out_shape=[jax.ShapeDtypeStruct((T, d), F32), jax.ShapeDtypeStruct((1, d), F32)],
        compiler_params=_cp("arbitrary"), name=name)(*ins)


def _final_loss(h, g, target, *, name):
    T, d = h.shape
    tr = _tile(T, 512, 8)

    def body(h_ref, g_ref, t_ref, l_ref, dh_ref, dg_ref):
        @pl.when(pl.program_id(0) == 0)
        def _():
            l_ref[...] = jnp.zeros_like(l_ref)
            dg_ref[...] = jnp.zeros_like(dg_ref)

        xv = h_ref[...]
        r = lax.rsqrt(jnp.mean(xv * xv, axis=-1, keepdims=True) + EPS)
        xh = xv * r
        e = xh * g_ref[...] - t_ref[...]
        l_ref[...] += jnp.sum(e * e, axis=0, keepdims=True)
        dy = e * (1.0 / d)
        dg_ref[...] += jnp.sum(dy * xh, axis=0, keepdims=True)
        dxh = dy * g_ref[...]
        dh_ref[...] = r * (dxh - xh * jnp.mean(dxh * xh, axis=-1, keepdims=True))

    row = pl.BlockSpec((tr, d), lambda i: (i, 0))
    vec = pl.BlockSpec((1, d), lambda i: (0, 0))
    return pl.pallas_call(
        body, grid=(T // tr,), in_specs=[row, vec, row], out_specs=[vec, row, vec],
        out_shape=[jax.ShapeDtypeStruct((1, d), F32), jax.ShapeDtypeStruct((T, d), F32),
                   jax.ShapeDtypeStruct((1, d), F32)],
        compiler_params=_cp("arbitrary"), name=name)(h, g, target)


def _gate_up_fwd(n, wgu, *, name):
    T, d = n.shape
    f = wgu.shape[1] // 2
    tm, tn = _tile(T, 512, 8), _tile(f, 1408)
    nf = f // tn

    def body(n_ref, wg_ref, wu_ref, g_ref, u_ref, a_ref):
        nv = n_ref[...].astype(MXU)
        gv = _nn(nv, wg_ref[...].astype(MXU))
        uv = _nn(nv, wu_ref[...].astype(MXU))
        g_ref[...] = gv.astype(BF16)
        u_ref[...] = uv.astype(BF16)
        a_ref[...] = (gv * _sig(gv) * uv).astype(BF16)

    out = pl.BlockSpec((tm, tn), lambda i, j: (i, j))
    act = jax.ShapeDtypeStruct((T, f), BF16)
    return pl.pallas_call(
        body, grid=(T // tm, nf),
        in_specs=[pl.BlockSpec((tm, d), lambda i, j: (i, 0)), pl.BlockSpec((d, tn), lambda i, j: (0, j)),
                  pl.BlockSpec((d, tn), lambda i, j: (0, j + nf))],
        out_specs=[out, out, out], out_shape=[act, act, act], compiler_params=_cp("parallel", "parallel"),
        name=name)(n, wgu, wgu)


def _act_bwd(dh, wd, gate, up, scale, *, name):
    T, d = dh.shape
    f = wd.shape[0]
    tm, tn = _tile(T, 512, 8), _tile(f, 1408)

    def body(dh_ref, wd_ref, g_ref, u_ref, dg_ref, du_ref):
        da = scale * _nt(dh_ref[...].astype(MXU), wd_ref[...].astype(MXU))
        gv, uv = g_ref[...].astype(F32), u_ref[...].astype(F32)
        s = _sig(gv)
        dg_ref[...] = (da * uv * (s * (1.0 + gv * (1.0 - s)))).astype(BF16)
        du_ref[...] = (da * (gv * s)).astype(BF16)

    tile = pl.BlockSpec((tm, tn), lambda i, j: (i, j))
    act = jax.ShapeDtypeStruct((T, f), BF16)
    return pl.pallas_call(
        body, grid=(T // tm, f // tn),
        in_specs=[pl.BlockSpec((tm, d), lambda i, j: (i, 0)), pl.BlockSpec((tn, d), lambda i, j: (j, 0)), tile, tile],
        out_specs=[tile, tile], out_shape=[act, act], compiler_params=_cp("parallel", "parallel"),
        name=name)(dh, wd, gate, up)


def _shift_down(x, d, ri):
    if d == 0:
        return x
    return jnp.where(ri >= d, pltpu.roll(x, d, 0), 0.0)


def _shift_up(x, d, ri):
    if d == 0:
        return x
    s = x.shape[0]
    return jnp.where(ri < s - d, pltpu.roll(x, s - d, 0), 0.0)


def _conv_taps(x, w_ref, ktaps, ri):
    acc = None
    for k in range(ktaps):
        t = w_ref[k:k + 1, :] * _shift_down(x, ktaps - 1 - k, ri)
        acc = t if acc is None else acc + t
    return acc


def _conv_a_fwd(pp, w8, bl, s, *, name):
    tc = 256
    nb = D // tc

    def body(b_ref, c_ref, v_ref, w_ref, o_ref):
        ri = lax.broadcasted_iota(jnp.int32, (s, tc), 0)
        cv = c_ref[...] * v_ref[...]
        o_ref[...] = (b_ref[...] * _conv_taps(cv, w_ref, 3, ri)).astype(BF16)

    def col(off):
        return pl.BlockSpec((s, tc), lambda b, j: (b, off // tc + j))

    return pl.pallas_call(
        body, grid=(bl, nb),
        in_specs=[col(O_AB), col(O_AC), col(O_AV), pl.BlockSpec((8, tc), lambda b, j: (0, j))],
        out_specs=pl.BlockSpec((s, tc), lambda b, j: (b, j)),
        out_shape=jax.ShapeDtypeStruct((bl * s, D), BF16), compiler_params=_cp("parallel", "parallel"),
        name=name)(pp, pp, pp, w8)


def _conv_a_bwd(pp, w8, dya, bl, s, *, name):
    tc = 256
    nb = D // tc

    def body(b_ref, c_ref, v_ref, w_ref, dy_ref, db_ref, dc_ref, dv_ref, dw_ref):
        @pl.when(pl.program_id(1) == 0)
        def _():
            dw_ref[...] = jnp.zeros_like(dw_ref)

        ri = lax.broadcasted_iota(jnp.int32, (s, tc), 0)
        cvec, vvec, dy = c_ref[...], v_ref[...], dy_ref[...]
        cv = cvec * vvec
        db_ref[...] = (dy * _conv_taps(cv, w_ref, 3, ri)).astype(BF16)
        dconv = dy * b_ref[...]
        dcv = None
        for k in range(3):
            t = w_ref[k:k + 1, :] * _shift_up(dconv, 2 - k, ri)
            dcv = t if dcv is None else dcv + t
            dw_ref[k:k + 1, :] += jnp.sum(dconv * _shift_down(cv, 2 - k, ri), axis=0, keepdims=True)
        dc_ref[...] = (dcv * vvec).astype(BF16)
        dv_ref[...] = (dcv * cvec).astype(BF16)

    def col(off):
        return pl.BlockSpec((s, tc), lambda j, b: (b, off // tc + j))

    plain = pl.BlockSpec((s, tc), lambda j, b: (b, j))
    wspec = pl.BlockSpec((8, tc), lambda j, b: (0, j))
    act = jax.ShapeDtypeStruct((bl * s, D), BF16)
    return pl.pallas_call(
        body, grid=(nb, bl), in_specs=[col(O_AB), col(O_AC), col(O_AV), wspec, plain],
        out_specs=[plain, plain, plain, wspec],
        out_shape=[act, act, act, jax.ShapeDtypeStruct((8, D), F32)],
        compiler_params=_cp("parallel", "arbitrary"), name=name)(pp, pp, pp, w8, dya)


def _conv_ssm_fwd(pp, w8, bias, bl, s, *, name):
    tc = 256
    width = DI + 2 * NG * NS
    nb = width // tc

    def body(x_ref, w_ref, b_ref, o_ref):
        ri = lax.broadcasted_iota(jnp.int32, (s, tc), 0)
        pre = _conv_taps(x_ref[...], w_ref, 4, ri) + b_ref[...]
        o_ref[...] = pre * _sig(pre)

    return pl.pallas_call(
        body, grid=(bl, nb),
        in_specs=[pl.BlockSpec((s, tc), lambda b, j: (b, O_XBC // tc + j)),
                  pl.BlockSpec((8, tc), lambda b, j: (0, j)), pl.BlockSpec((1, tc), lambda b, j: (0, j))],
        out_specs=pl.BlockSpec((s, tc), lambda b, j: (b, j)),
        out_shape=jax.ShapeDtypeStruct((bl * s, width), F32), compiler_params=_cp("parallel", "parallel"),
        name=name)(pp, w8, bias)


def _conv_ssm_bwd(pp, w8, bias, dxc, ch_off, bl, s, *, name):
    n = dxc.shape[1]
    tc = 256
    nb = n // tc
    o0 = ch_off // tc

    def body(x_ref, w_ref, b_ref, d_ref, dx_ref, dw_ref, db_ref):
        @pl.when(pl.program_id(1) == 0)
        def _():
            dw_ref[...] = jnp.zeros_like(dw_ref)
            db_ref[...] = jnp.zeros_like(db_ref)

        ri = lax.broadcasted_iota(jnp.int32, (s, tc), 0)
        xv = x_ref[...]
        pre = _conv_taps(xv, w_ref, 4, ri) + b_ref[...]
        sg = _sig(pre)
        dpre = d_ref[...] * (sg * (1.0 + pre * (1.0 - sg)))
        db_ref[...] += jnp.sum(dpre, axis=0, keepdims=True)
        dx = None
        for k in range(4):
            t = w_ref[k:k + 1, :] * _shift_up(dpre, 3 - k, ri)
            dx = t if dx is None else dx + t
            dw_ref[k:k + 1, :] += jnp.sum(dpre * _shift_down(xv, 3 - k, ri), axis=0, keepdims=True)
        dx_ref[...] = dx.astype(BF16)

    plain = pl.BlockSpec((s, tc), lambda j, b: (b, j))
    return pl.pallas_call(
        body, grid=(nb, bl),
        in_specs=[pl.BlockSpec((s, tc), lambda j, b: (b, O_XBC // tc + o0 + j)),
                  pl.BlockSpec((8, tc), lambda j, b: (0, o0 + j)), pl.BlockSpec((1, tc), lambda j, b: (0, o0 + j)),
                  plain],
        out_specs=[plain, pl.BlockSpec((8, tc), lambda j, b: (0, j)), pl.BlockSpec((1, tc), lambda j, b: (0, j))],
        out_shape=[jax.ShapeDtypeStruct((bl * s, n), BF16), jax.ShapeDtypeStruct((8, n), F32),
                   jax.ShapeDtypeStruct((1, n), F32)],
        compiler_params=_cp("parallel", "arbitrary"), name=name)(pp, w8, bias, dxc)


def _softplus(x):
    return jnp.maximum(x, 0.0) + jnp.log1p(jnp.exp(-jnp.abs(x)))


def _dt_fwd(pp, bias128, *, name):
    T = pp.shape[0]
    tr = _tile(T, 1024, 8)

    def body(x_ref, b_ref, o_ref):
        lane = lax.broadcasted_iota(jnp.int32, (tr, 128), 1)
        o_ref[...] = jnp.where(lane < NH, _softplus(x_ref[...] + b_ref[...]), 0.0)

    return pl.pallas_call(
        body, grid=(T // tr,),
        in_specs=[pl.BlockSpec((tr, 128), lambda i: (i, O_DT // 128)), pl.BlockSpec((1, 128), lambda i: (0, 0))],
        out_specs=pl.BlockSpec((tr, 128), lambda i: (i, 0)),
        out_shape=jax.ShapeDtypeStruct((T, 128), F32), compiler_params=_cp("parallel"), name=name)(pp, bias128)


def _dt_bwd(pp, bias128, ddt, *, name):
    T = pp.shape[0]
    tr = _tile(T, 1024, 8)

    def body(x_ref, b_ref, d_ref, o_ref, db_ref):
        @pl.when(pl.program_id(0) == 0)
        def _():
            db_ref[...] = jnp.zeros_like(db_ref)

        lane = lax.broadcasted_iota(jnp.int32, (tr, 128), 1)
        dr = jnp.where(lane < NH, d_ref[...] * _sig(x_ref[...] + b_ref[...]), 0.0)
        db_ref[...] += jnp.sum(dr, axis=0, keepdims=True)
        o_ref[...] = dr.astype(BF16)

    row = pl.BlockSpec((tr, 128), lambda i: (i, 0))
    vec = pl.BlockSpec((1, 128), lambda i: (0, 0))
    return pl.pallas_call(
        body, grid=(T // tr,),
        in_specs=[pl.BlockSpec((tr, 128), lambda i: (i, O_DT // 128)), vec, row],
        out_specs=[row, vec],
        out_shape=[jax.ShapeDtypeStruct((T, 128), BF16), jax.ShapeDtypeStruct((1, 128), F32)],
        compiler_params=_cp("arbitrary"), name=name)(pp, bias128, ddt)


def _ssd_common(dt, dtt, arow, acol):
    ri = lax.broadcasted_iota(jnp.int32, (CH, CH), 0)
    ci = lax.broadcasted_iota(jnp.int32, (CH, CH), 1)
    tril = ri >= ci
    triu = ri <= ci
    acs_col = _nn(tril.astype(F32), dt * arow, HI)
    acs_row = _nn(dtt * acol, triu.astype(F32), HI)
    return tril, triu, acs_col, acs_row


def _pair_terms(q, dt, acs_col, acs_row, tril, triu, lo, with_t):
    ha, hb = 2 * q, 2 * q + 1
    col_a, col_b = acs_col[:, ha:ha + 1], acs_col[:, hb:hb + 1]
    row_a, row_b = acs_row[ha:ha + 1, :], acs_row[hb:hb + 1, :]
    last_a, last_b = acs_col[CH - 1:CH, ha:ha + 1], acs_col[CH - 1:CH, hb:hb + 1]
    out = dict(
        dtsel=jnp.where(lo, dt[:, ha:ha + 1], dt[:, hb:hb + 1]),
        d_a=jnp.exp(jnp.where(tril, col_a - row_a, NEG)), d_b=jnp.exp(jnp.where(tril, col_b - row_b, NEG)),
        esel=jnp.where(lo, jnp.exp(col_a), jnp.exp(col_b)),
        fsel=jnp.where(lo, jnp.exp(last_a - col_a), jnp.exp(last_b - col_b)),
        g_a=jnp.exp(last_a), g_b=jnp.exp(last_b))
    if with_t:
        out["dt_a"] = jnp.exp(jnp.where(triu, row_a - col_a, NEG))
        out["dt_b"] = jnp.exp(jnp.where(triu, row_b - col_b, NEG))
    return out


def _ssd_fwd(xc, pp, dtg, dtt, arow, acol, dexp, ng, bl, s, *, name):
    nc = s // CH
    T = bl * s

    def body(xs_ref, bm_ref, cm_ref, z_ref, dt_ref, dtt_ref, arow_ref, acol_ref, dexp_ref, ng_ref,
             y_ref, yn_ref, prev_ref, st_ref):
        @pl.when(pl.program_id(2) == 0)
        def _():
            st_ref[...] = jnp.zeros_like(st_ref)

        dt = dt_ref[...]
        tril, triu, acs_col, acs_row = _ssd_common(dt, dtt_ref[...], -jnp.exp(arow_ref[...]), -jnp.exp(acol_ref[...]))
        bm, cm = bm_ref[...].astype(MXU), cm_ref[...].astype(MXU)
        cb = _nt(cm, bm)
        lo = lax.broadcasted_iota(jnp.int32, (CH, 128), 1) < HD
        sub_lo = lax.broadcasted_iota(jnp.int32, (128, NS), 0) < HD
        for q in range(4):
            t = _pair_terms(q, dt, acs_col, acs_row, tril, triu, lo, False)
            x = xs_ref[:, 128 * q:128 * (q + 1)]
            xd = x * t["dtsel"]
            y = (_nn((cb * t["d_a"]).astype(MXU), jnp.where(lo, xd, 0.0).astype(MXU))
                 + _nn((cb * t["d_b"]).astype(MXU), jnp.where(lo, 0.0, xd).astype(MXU)))
            prev = st_ref[q]
            prev_ref[q] = prev
            y = y + t["esel"] * _nt(cm, prev.astype(MXU))
            st_ref[q] = prev * jnp.where(sub_lo, t["g_a"], t["g_b"]) + _tn((xd * t["fsel"]).astype(MXU), bm)
            y_ref[:, 128 * q:128 * (q + 1)] = y + dexp_ref[:, 128 * q:128 * (q + 1)] * x
        zv = z_ref[...]
        yg = y_ref[...] * (zv * _sig(zv))
        r = lax.rsqrt(jnp.mean(yg * yg, axis=-1, keepdims=True) + EPS)
        yn_ref[...] = (yg * r * ng_ref[...]).astype(BF16)

    def row(width, off_blocks):
        return pl.BlockSpec((CH, width), lambda g, b, c: (b * nc + c, off_blocks + g))

    return pl.pallas_call(
        body, grid=(NG, bl, nc),
        in_specs=[row(GW, 0), row(NS, DI // NS), row(NS, DI // NS + NG), row(GW, O_Z // GW), row(128, 0),
                  pl.BlockSpec((None, 8, CH), lambda g, b, c: (g, 0, b * nc + c)),
                  pl.BlockSpec((1, 128), lambda g, b, c: (0, g)),
                  pl.BlockSpec((None, 8, 1), lambda g, b, c: (g, 0, 0)),
                  pl.BlockSpec((1, GW), lambda g, b, c: (0, g)), pl.BlockSpec((1, GW), lambda g, b, c: (0, g))],
        out_specs=[row(GW, 0), row(GW, 0),
                   pl.BlockSpec((None, 4, 128, NS), lambda g, b, c: (b * nc + c, g, 0, 0))],
        out_shape=[jax.ShapeDtypeStruct((T, DI), F32), jax.ShapeDtypeStruct((T, DI), BF16),
                   jax.ShapeDtypeStruct((bl * nc, 16, 128, NS), F32)],
        scratch_shapes=[pltpu.VMEM((4, 128, NS), F32)],
        compiler_params=_cp("parallel", "parallel", "arbitrary"), name=name,
    )(xc, xc, xc, pp, dtg, dtt, arow, acol, dexp, ng)


def _ssd_bwd(dyn, y, xc, pp, dtg, dtt, arow, acol, dexp, ng, prev, bl, s, *, name):
    nc = s // CH
    T = bl * s

    def rsum(v):
        return jnp.sum(v, axis=1, keepdims=True)

    def asum(v):
        return jnp.sum(rsum(v), axis=0, keepdims=True)

    def body(dyn_ref, y_ref, xs_ref, bm_ref, cm_ref, z_ref, dt_ref, dtt_ref, arow_ref, acol_ref, dexp_ref, ng_ref,
             prev_ref, dz_ref, dxs_ref, db_ref, dc_ref, ddt_ref, dng_ref, dd_ref, dal_ref, dst_ref):
        @pl.when((pl.program_id(1) == 0) & (pl.program_id(2) == 0))
        def _():
            dng_ref[...] = jnp.zeros_like(dng_ref)
            dd_ref[...] = jnp.zeros_like(dd_ref)
            dal_ref[...] = jnp.zeros_like(dal_ref)

        @pl.when(pl.program_id(2) == 0)
        def _():
            dst_ref[...] = jnp.zeros_like(dst_ref)

        yv, zv = y_ref[...], z_ref[...]
        sz = _sig(zv)
        silu = zv * sz
        yg = yv * silu
        r = lax.rsqrt(jnp.mean(yg * yg, axis=-1, keepdims=True) + EPS)
        yh = yg * r
        dynv = dyn_ref[...]
        dng_ref[...] += jnp.sum(dynv * yh, axis=0, keepdims=True)
        dyh = dynv * ng_ref[...]
        dyg = r * (dyh - yh * jnp.mean(dyh * yh, axis=-1, keepdims=True))
        dz_ref[...] = (dyg * yv * (sz * (1.0 + zv * (1.0 - sz)))).astype(BF16)
        dxs_ref[...] = dyg * silu
        dd_ref[...] += jnp.sum(dxs_ref[...] * xs_ref[...], axis=0, keepdims=True)

        dt = dt_ref[...]
        arow_v = -jnp.exp(arow_ref[...])
        tril, triu, acs_col, acs_row = _ssd_common(dt, dtt_ref[...], arow_v, -jnp.exp(acol_ref[...]))
        bm, cm = bm_ref[...].astype(MXU), cm_ref[...].astype(MXU)
        cb, cbt = _nt(cm, bm), _nt(bm, cm)
        lane = lax.broadcasted_iota(jnp.int32, (CH, 128), 1)
        rowi = lax.broadcasted_iota(jnp.int32, (CH, 128), 0)
        lo = lane < HD
        sub_lo = lax.broadcasted_iota(jnp.int32, (128, NS), 0) < HD
        dcb = jnp.zeros((CH, CH), F32)
        dcbt = jnp.zeros((CH, CH), F32)
        dc_acc = jnp.zeros((CH, NS), F32)
        db_acc = jnp.zeros((CH, NS), F32)
        dacs = jnp.zeros((CH, 128), F32)
        ddtx = jnp.zeros((CH, 128), F32)
        for q in range(4):
            ha, hb = 2 * q, 2 * q + 1
            t = _pair_terms(q, dt, acs_col, acs_row, tril, triu, lo, True)
            x = xs_ref[:, 128 * q:128 * (q + 1)]
            dy = dxs_ref[:, 128 * q:128 * (q + 1)]
            xd = x * t["dtsel"]
            xd_m, dy_m = xd.astype(MXU), dy.astype(MXU)
            xd_lo, xd_hi = jnp.where(lo, xd, 0.0).astype(MXU), jnp.where(lo, 0.0, xd).astype(MXU)
            dy_lo, dy_hi = jnp.where(lo, dy, 0.0).astype(MXU), jnp.where(lo, 0.0, dy).astype(MXU)
            m_a, m_b = cb * t["d_a"], cb * t["d_b"]
            mt_a, mt_b = cbt * t["dt_a"], cbt * t["dt_b"]
            prev = prev_ref[q]
            dnext = dst_ref[q]
            prev_m, dnext_m = prev.astype(MXU), dnext.astype(MXU)
            bds = _nt(bm, dnext_m)
            dxd = _nn(mt_a.astype(MXU), dy_lo) + _nn(mt_b.astype(MXU), dy_hi) + t["fsel"] * bds
            yoff = t["esel"] * _nt(cm, prev_m)
            dye = dy * t["esel"]
            dye_m = dye.astype(MXU)
            xdf_m = (xd * t["fsel"]).astype(MXU)
            dst_ref[q] = dnext * jnp.where(sub_lo, t["g_a"], t["g_b"]) + _tn(dye_m, cm)
            dm_a, dm_b = _nt(dy_lo, xd_m), _nt(dy_hi, xd_m)
            dmt_a, dmt_b = _nt(xd_lo, dy_m), _nt(xd_hi, dy_m)
            dcb = dcb + dm_a * t["d_a"] + dm_b * t["d_b"]
            dcbt = dcbt + dmt_a * t["dt_a"] + dmt_b * t["dt_b"]
            tyf = dy * yoff - t["fsel"] * xd * bds
            tf = t["fsel"] * xd * bds
            ra = rsum(dm_a * m_a) - rsum(dmt_a * mt_a) + rsum(jnp.where(lo, tyf, 0.0))
            rb = rsum(dm_b * m_b) - rsum(dmt_b * mt_b) + rsum(jnp.where(lo, 0.0, tyf))
            dpp = dnext * prev
            ea = asum(jnp.where(lo, tf, 0.0)) + t["g_a"] * asum(jnp.where(sub_lo, dpp, 0.0))
            eb = asum(jnp.where(lo, 0.0, tf)) + t["g_b"] * asum(jnp.where(sub_lo, 0.0, dpp))
            is_last = rowi == CH - 1
            dacs = (dacs + jnp.where(lane == ha, ra + jnp.where(is_last, ea, 0.0), 0.0)
                    + jnp.where(lane == hb, rb + jnp.where(is_last, eb, 0.0), 0.0))
            tx = dxd * x
            ddtx = (ddtx + jnp.where(lane == ha, rsum(jnp.where(lo, tx, 0.0)), 0.0)
                    + jnp.where(lane == hb, rsum(jnp.where(lo, 0.0, tx)), 0.0))
            dxs_ref[:, 128 * q:128 * (q + 1)] = dxd * t["dtsel"] + dexp_ref[:, 128 * q:128 * (q + 1)] * dy
            dc_acc = dc_acc + _nn(dye_m, prev_m)
            db_acc = db_acc + _nn(xdf_m, dnext_m)
        dc_ref[...] = dc_acc + _nn(dcb.astype(MXU), bm)
        db_ref[...] = db_acc + _nn(dcbt.astype(MXU), cm)
        dla = _nn(triu.astype(F32), dacs, HI)
        ddt_ref[...] = dla * arow_v + ddtx
        dal_ref[...] += jnp.sum(dla * dt, axis=0, keepdims=True) * arow_v

    def row(width, off_blocks):
        return pl.BlockSpec((CH, width), lambda g, b, c: (b * nc + nc - 1 - c, off_blocks + g))

    gvec = pl.BlockSpec((1, GW), lambda g, b, c: (0, g))
    hvec = pl.BlockSpec((1, 128), lambda g, b, c: (0, g))
    return pl.pallas_call(
        body, grid=(NG, bl, nc),
        in_specs=[row(GW, 0), row(GW, 0), row(GW, 0), row(NS, DI // NS), row(NS, DI // NS + NG), row(GW, O_Z // GW),
                  row(128, 0), pl.BlockSpec((None, 8, CH), lambda g, b, c: (g, 0, b * nc + nc - 1 - c)),
                  hvec, pl.BlockSpec((None, 8, 1), lambda g, b, c: (g, 0, 0)), gvec, gvec,
                  pl.BlockSpec((None, 4, 128, NS), lambda g, b, c: (b * nc + nc - 1 - c, g, 0, 0))],
        out_specs=[row(GW, 0), row(GW, 0), row(NS, 0), row(NS, 0), row(128, 0), gvec, gvec, hvec],
        out_shape=[jax.ShapeDtypeStruct((T, DI), BF16), jax.ShapeDtypeStruct((T, DI), F32),
                   jax.ShapeDtypeStruct((T, NG * NS), F32), jax.ShapeDtypeStruct((T, NG * NS), F32),
                   jax.ShapeDtypeStruct((T, NG * 128), F32), jax.ShapeDtypeStruct((1, DI), F32),
                   jax.ShapeDtypeStruct((1, DI), F32), jax.ShapeDtypeStruct((1, NG * 128), F32)],
        scratch_shapes=[pltpu.VMEM((4, 128, NS), F32)],
        compiler_params=_cp("arbitrary", "arbitrary", "arbitrary"), name=name,
    )(dyn, y, xc, xc, xc, pp, dtg, dtt, arow, acol, dexp, ng, prev)


def _merge_fwd(pp, ya, yb, *, name):
    T = ya.shape[0]
    tr = _tile(T, 512, 8)

    def body(ga_ref, gb_ref, ya_ref, yb_ref, o_ref):
        o_ref[...] = (_sig(ga_ref[...]) * ya_ref[...] + _sig(gb_ref[...]) * yb_ref[...]).astype(BF16)

    row = pl.BlockSpec((tr, D), lambda i: (i, 0))
    return pl.pallas_call(
        body, grid=(T // tr,),
        in_specs=[pl.BlockSpec((tr, D), lambda i: (i, O_GA // D)), pl.BlockSpec((tr, D), lambda i: (i, O_GB // D)),
                  row, row],
        out_specs=row, out_shape=jax.ShapeDtypeStruct((T, D), BF16), compiler_params=_cp("parallel"),
        name=name)(pp, pp, ya, yb)


def _merge_bwd(pp, ya, yb, dm, *, name):
    T = ya.shape[0]
    tr = _tile(T, 512, 8)

    def body(ga_ref, gb_ref, ya_ref, yb_ref, dm_ref, dya_ref, dyb_ref, dga_ref, dgb_ref):
        sa, sb, dmv = _sig(ga_ref[...]), _sig(gb_ref[...]), dm_ref[...]
        dya_ref[...] = (dmv * sa).astype(BF16)
        dyb_ref[...] = (dmv * sb).astype(BF16)
        dga_ref[...] = (dmv * ya_ref[...] * (sa * (1.0 - sa))).astype(BF16)
        dgb_ref[...] = (dmv * yb_ref[...] * (sb * (1.0 - sb))).astype(BF16)

    row = pl.BlockSpec((tr, D), lambda i: (i, 0))
    act = jax.ShapeDtypeStruct((T, D), BF16)
    return pl.pallas_call(
        body, grid=(T // tr,),
        in_specs=[pl.BlockSpec((tr, D), lambda i: (i, O_GA // D)), pl.BlockSpec((tr, D), lambda i: (i, O_GB // D)),
                  row, row, row],
        out_specs=[row, row, row, row], out_shape=[act, act, act, act], compiler_params=_cp("parallel"),
        name=name)(pp, pp, ya, yb, dm)


def _softmax_rows(sc):
    e = jnp.exp(sc - jnp.max(sc, axis=-1, keepdims=True))
    return e / jnp.sum(e, axis=-1, keepdims=True)


def _attn_fwd(q, kv, bl, s, *, name):
    m = kv.shape[0] // bl
    tq = _tile(s, 512)
    nq = s // tq
    scale = 1.0 / math.sqrt(XD)

    def body(q_ref, k_ref, v_ref, o_ref):
        p = _softmax_rows(_nt(q_ref[...], k_ref[...]) * scale)
        o_ref[...] = _nn(p.astype(MXU), v_ref[...]).astype(BF16)

    qspec = pl.BlockSpec((tq, XD), lambda b, h, i: (b * nq + i, h))
    return pl.pallas_call(
        body, grid=(bl, XH, nq),
        in_specs=[qspec, pl.BlockSpec((m, XD), lambda b, h, i: (b, h)),
                  pl.BlockSpec((m, XD), lambda b, h, i: (b, XH + h))],
        out_specs=qspec, out_shape=jax.ShapeDtypeStruct((bl * s, D), BF16),
        compiler_params=_cp("parallel", "parallel", "parallel"), name=name)(q, kv, kv)


def _attn_bwd(q, kv, do, bl, s, *, name):
    m = kv.shape[0] // bl
    tq = _tile(s, 512)
    nq = s // tq
    scale = 1.0 / math.sqrt(XD)

    def body(q_ref, k_ref, v_ref, do_ref, dq_ref, dk_ref, dv_ref):
        @pl.when(pl.program_id(2) == 0)
        def _():
            dk_ref[...] = jnp.zeros_like(dk_ref)
            dv_ref[...] = jnp.zeros_like(dv_ref)

        qv, kvv, vv, dov = q_ref[...], k_ref[...], v_ref[...], do_ref[...]
        p = _softmax_rows(_nt(qv, kvv) * scale)
        dp = _nt(dov, vv)
        ds = (p * (dp - jnp.sum(dp * p, axis=-1, keepdims=True)) * scale).astype(MXU)
        dq_ref[...] = _nn(ds, kvv).astype(BF16)
        dk_ref[...] += _tn(ds, qv)
        dv_ref[...] += _tn(p.astype(MXU), dov)

    qspec = pl.BlockSpec((tq, XD), lambda b, h, i: (b * nq + i, h))
    kspec = pl.BlockSpec((m, XD), lambda b, h, i: (b, h))
    return pl.pallas_call(
        body, grid=(bl, XH, nq),
        in_specs=[qspec, kspec, pl.BlockSpec((m, XD), lambda b, h, i: (b, XH + h)), qspec],
        out_specs=[qspec, kspec, kspec],
        out_shape=[jax.ShapeDtypeStruct((bl * s, D), BF16), jax.ShapeDtypeStruct((bl * m, D), F32),
                   jax.ShapeDtypeStruct((bl * m, D), F32)],
        compiler_params=_cp("parallel", "parallel", "arbitrary"), name=name)(q, kv, kv, do)


def _row_tile(r, c, max_elems=512 * 1024, align=16):
    best = None
    for t in range(align, r + 1, align):
        if r % t == 0 and t * c <= max_elems:
            best = t
    return best if best is not None else r


def _addn(a, others, *, name, also_bf16=False):
    r, c = a.shape
    tr = _row_tile(r, c)
    n = len(others)

    def body(*refs):
        acc = refs[0][...].astype(F32)
        for o_ref in refs[1:1 + n]:
            acc = acc + o_ref[...].astype(F32)
        refs[1 + n][...] = acc
        if also_bf16:
            refs[2 + n][...] = acc.astype(BF16)

    spec = pl.BlockSpec((tr, c), lambda i: (i, 0))
    shapes = [jax.ShapeDtypeStruct((r, c), F32)] + ([jax.ShapeDtypeStruct((r, c), BF16)] if also_bf16 else [])
    out = pl.pallas_call(
        body, grid=(r // tr,), in_specs=[spec] * (1 + n), out_specs=[spec] * len(shapes), out_shape=shapes,
        compiler_params=_cp("parallel"), name=name)(a, *others)
    return out if also_bf16 else out[0]


def _sum_leading(a, *, name):
    n, r, c = a.shape

    def body(a_ref, o_ref):
        acc = a_ref[0]
        for i in range(1, n):
            acc = acc + a_ref[i]
        o_ref[...] = acc

    return pl.pallas_call(body, out_shape=jax.ShapeDtypeStruct((r, c), F32), name=name)(a)


def _adamw_math(wv, gv, mv, vv):
    m2 = ADAM_B1 * mv + (1.0 - ADAM_B1) * gv
    v2 = ADAM_B2 * vv + (1.0 - ADAM_B2) * (gv * gv)
    m_hat = m2 / (1.0 - ADAM_B1 ** ADAM_STEP)
    v_hat = v2 / (1.0 - ADAM_B2 ** ADAM_STEP)
    return -ADAM_LR * (m_hat / (jnp.sqrt(v_hat) + ADAM_EPS) + ADAM_WD * wv), m2, v2


def _adamw(w, g, m, v, *, name):
    r, c = w.shape
    tr = _row_tile(r, c, align=8)

    def body(w_ref, g_ref, m_ref, v_ref, d_ref, mo_ref, vo_ref):
        d_ref[...], mo_ref[...], vo_ref[...] = _adamw_math(w_ref[...], g_ref[...], m_ref[...], v_ref[...])

    spec = pl.BlockSpec((tr, c), lambda i: (i, 0))
    shp = jax.ShapeDtypeStruct((r, c), F32)
    return pl.pallas_call(
        body, grid=(r // tr,), in_specs=[spec] * 4, out_specs=[spec] * 3, out_shape=[shp] * 3,
        compiler_params=_cp("parallel"), name=name)(w, g, m, v)


def _adamw_halves(w, g_mine, g_other, m, v, c, *, name):
    r, cols = w.shape
    h = r // 2
    tr = _row_tile(h, cols, align=8)
    nh = h // tr

    def body(c_ref, w_ref, gm_ref, go_ref, m_ref, v_ref, g_ref, d_ref, mo_ref, vo_ref):
        gv = jnp.where(pl.program_id(0) // nh == c_ref[0], gm_ref[...], go_ref[...])
        g_ref[...] = gv
        d_ref[...], mo_ref[...], vo_ref[...] = _adamw_math(w_ref[...], gv, m_ref[...], v_ref[...])

    full = pl.BlockSpec((tr, cols), lambda i, c_: (i, 0))
    half = pl.BlockSpec((tr, cols), lambda i, c_: (i % nh, 0))
    shp = jax.ShapeDtypeStruct((r, cols), F32)
    return pl.pallas_call(
        body,
        grid_spec=pltpu.PrefetchScalarGridSpec(num_scalar_prefetch=1, grid=(2 * nh,),
                                               in_specs=[full, half, half, full, full], out_specs=[full] * 4),
        out_shape=[shp] * 4, compiler_params=_cp("parallel"), name=name,
    )(jnp.reshape(c, (1,)).astype(jnp.int32), w, g_mine, g_other, m, v)


def _flip(i, d):
    return 1 - i if d else i


def _comm(name, ins, out_shapes, n_remote, n_local, plan, aliases=None):
    n_in, n_out = len(ins), len(out_shapes)

    def body(*refs):
        in_refs, out_refs = refs[:n_in], refs[n_in:n_in + n_out]
        send_sems, recv_sems = refs[n_in + n_out], refs[n_in + n_out + 1]
        x, y, c = lax.axis_index("x"), lax.axis_index("y"), lax.axis_index("c")
        remote, local = plan(in_refs, out_refs, x, y, c)
        assert len(remote) == n_remote and len(local) == n_local
        copies = []
        if n_local:
            loc_sems = refs[n_in + n_out + 2]
            copies += [pltpu.make_async_copy(s_, d_, loc_sems.at[i]) for i, (s_, d_) in enumerate(local)]
        copies += [pltpu.make_async_remote_copy(src_ref=s_, dst_ref=d_, send_sem=send_sems.at[i],
                                                recv_sem=recv_sems.at[i], device_id=dev, device_id_type=MESH)
                   for i, (s_, d_, dev) in enumerate(remote)]
        for cp in copies:
            cp.start()
        for cp in copies:
            cp.wait()

    hbm = pl.BlockSpec(memory_space=pl.ANY)
    scratch = [pltpu.SemaphoreType.DMA((n_remote,)), pltpu.SemaphoreType.DMA((n_remote,))]
    if n_local:
        scratch.append(pltpu.SemaphoreType.DMA((n_local,)))
    return pl.pallas_call(
        body, in_specs=[hbm] * n_in, out_specs=[hbm] * n_out, out_shape=out_shapes, scratch_shapes=scratch,
        input_output_aliases=aliases or {}, compiler_params=pltpu.CompilerParams(has_side_effects=True),
        name=name)(*ins)


HBM_SPEC = pl.BlockSpec(memory_space=pltpu.HBM)
SEM_SPEC = pl.BlockSpec(memory_space=pltpu.SEMAPHORE)
DATAFLOW = pltpu.SideEffectType.DATAFLOW_SIDE_EFFECTING


def _remote_copies(plan, srcs, lands, send_sems, recv_sems, n_copies):
    x, y, c = lax.axis_index("x"), lax.axis_index("y"), lax.axis_index("c")
    copies = plan(srcs, lands, x, y, c)
    assert len(copies) == n_copies
    return [pltpu.make_async_remote_copy(src_ref=s_, dst_ref=d_, send_sem=send_sems.at[i], recv_sem=recv_sems.at[i],
                                         device_id=dev, device_id_type=MESH) for i, (s_, d_, dev) in enumerate(copies)]


def _split_start(name, srcs, lands, n_copies, plan):
    ns, nb = len(srcs), len(srcs) + len(lands)

    def body(*refs):
        for cp in _remote_copies(plan, refs[:ns], refs[ns:nb], refs[nb], refs[nb + 1], n_copies):
            cp.start()
        refs[-1][...] = jnp.zeros_like(refs[-1])

    arrays = [pltpu.with_memory_space_constraint(a_, pltpu.HBM) for a_ in list(srcs) + list(lands)]
    out = pl.pallas_call(
        body, name=name,
        out_shape=(pltpu.SemaphoreType.DMA((n_copies,)), pltpu.SemaphoreType.DMA((n_copies,)),
                   *[pltpu.HBM(a_.shape, a_.dtype) for a_ in arrays], jax.ShapeDtypeStruct((8, 128), F32)),
        in_specs=[HBM_SPEC] * nb,
        out_specs=(SEM_SPEC, SEM_SPEC, *[HBM_SPEC] * nb, pl.BlockSpec(memory_space=pltpu.VMEM)),
        input_output_aliases={i: 2 + i for i in range(nb)},
        compiler_params=pltpu.CompilerParams(has_side_effects=DATAFLOW))(*arrays)
    return (out[0], out[1], list(out[2:2 + nb])), out[-1]


def _split_wait(name, handle, ns, n_copies, plan, after):
    send_sems, recv_sems, bufs = handle
    nb = len(bufs)

    def body(*refs):
        for cp in _remote_copies(plan, refs[:ns], refs[ns:nb], refs[nb], refs[nb + 1], n_copies):
            cp.wait_send()
            cp.wait_recv()

    out = pl.pallas_call(
        body, name=name, out_shape=[pltpu.HBM(b_.shape, b_.dtype) for b_ in bufs],
        in_specs=[HBM_SPEC] * nb + [SEM_SPEC, SEM_SPEC, pl.BlockSpec(memory_space=pl.ANY)],
        out_specs=[HBM_SPEC] * nb, input_output_aliases={i: i for i in range(nb)},
        compiler_params=pltpu.CompilerParams(has_side_effects=DATAFLOW))(*bufs, send_sems, recv_sems, after)
    return list(out[ns:])


def _gather_start(shards, tag):
    n = len(shards)
    lands = [lax.empty((4,) + s.shape, s.dtype) for s in shards]

    def plan(srcs, dsts, x, y, c):
        k = 2 * x + y
        copies = []
        for w_ref, o_ref in zip(srcs, dsts):
            h = w_ref.shape[0] // 2
            rows = pl.ds(c * h, h)
            copies += [(w_ref.at[rows], o_ref.at[k, rows], (_flip(x, dx), _flip(y, dy), c)) for dx, dy in CHIP_FLIPS]
        return copies

    handle, token = _split_start(f"gather_{tag}_start", shards, lands, 3 * n, plan)
    return (handle, plan, n), token


def _gather_finish(started, after, tag):
    handle, plan, n = started
    lands = _split_wait(f"gather_{tag}_wait", handle, n, 3 * n, plan, after)

    def plan_d2d(in_refs, out_refs, x, y, c):
        remote = []
        for o_ref in out_refs:
            h = o_ref.shape[1] // 2
            for dx, dy in CHIP_FLIPS:
                half = o_ref.at[2 * _flip(x, dx) + _flip(y, dy), pl.ds(c * h, h)]
                remote.append((half, half, (x, y, 1 - c)))
        return remote, []

    return _comm(f"gather_{tag}_d2d", lands, [jax.ShapeDtypeStruct(l_.shape, l_.dtype) for l_ in lands], 3 * n, 0,
                 plan_d2d, aliases={i: i for i in range(n)})


def _pair_plan(in_refs, out_refs, x, y, c):
    return [(i_, o_, (x, y, 1 - c)) for i_, o_ in zip(in_refs, out_refs)], []


def _rs_start(grads, tag):
    n = len(grads)
    c = lax.axis_index("c")
    halves = [g.shape[1] // 2 for g in grads]
    mine = [lax.dynamic_slice_in_dim(g, c * h, h, axis=1) for g, h in zip(grads, halves)]
    send_a = [lax.dynamic_slice_in_dim(g, (1 - c) * h, h, axis=1).astype(BF16) for g, h in zip(grads, halves)]
    recv_a = _comm(f"rs_pair_{tag}", send_a, [jax.ShapeDtypeStruct(s.shape, BF16) for s in send_a], n, 0, _pair_plan)
    pair, pair_b = [], []
    for i, (mi, ra) in enumerate(zip(mine, recv_a)):
        four, h, cols = mi.shape
        p32, p16 = _addn(mi.reshape(four * h, cols), [ra.reshape(four * h, cols)], name=f"rs_pair_sum_{tag}_{i}",
                         also_bf16=True)
        pair.append(p32.reshape(four, h, cols))
        pair_b.append(p16.reshape(four, h, cols))

    def plan(srcs, dsts, x, y, c_):
        copies = []
        for i_, o_ in zip(srcs, dsts):
            for j, (dx, dy) in enumerate(CHIP_FLIPS):
                fx, fy = _flip(x, dx), _flip(y, dy)
                copies.append((i_.at[2 * fx + fy], o_.at[j], (fx, fy, c_)))
        return copies

    lands = [lax.empty((3,) + p.shape[1:], BF16) for p in pair_b]
    handle, token = _split_start(f"rs_chips_{tag}_start", pair_b, lands, 3 * n, plan)
    return (handle, plan, n, pair), token


def _rs_finish(started, after, tag):
    handle, plan, n, pair = started
    recv_b = _split_wait(f"rs_chips_{tag}_wait", handle, n, 3 * n, plan, after)
    k = 2 * lax.axis_index("x") + lax.axis_index("y")
    tot = [_addn(lax.dynamic_index_in_dim(p, k, 0, keepdims=False), [rb[0], rb[1], rb[2]],
                 name=f"rs_chip_sum_{tag}_{i}") for i, (p, rb) in enumerate(zip(pair, recv_b))]
    other = _comm(f"rs_halves_{tag}", tot, [jax.ShapeDtypeStruct(t.shape, F32) for t in tot], n, 0, _pair_plan)
    return tot, other


def _gather_all(vec, *, name):
    out = jax.ShapeDtypeStruct((8,) + vec.shape, vec.dtype)

    def plan(in_refs, out_refs, x, y, c):
        me = 4 * x + 2 * y + c
        remote = [(in_refs[0], out_refs[0].at[me], (_flip(x, dx), _flip(y, dy), _flip(c, dc)))
                  for dx in (0, 1) for dy in (0, 1) for dc in (0, 1) if (dx, dy, dc) != (0, 0, 0)]
        return remote, [(in_refs[0], out_refs[0].at[me])]

    return _comm(name, [vec], [out], 7, 1, plan)[0]


def _pack(parts):
    flat = [p.reshape(-1).astype(F32) for p in parts]
    total = sum(f.shape[0] for f in flat)
    n = -(-total // 1024) * 128
    vec = jnp.concatenate(flat + [jnp.zeros((8 * n - total,), F32)]).reshape(8, n)
    offs, o = [], 0
    for f in flat:
        offs.append((o, f.shape[0]))
        o += f.shape[0]
    return vec, offs


def _unpack(vec, offs, shapes):
    flat = vec.reshape(-1)
    return [flat[o:o + n].reshape(s) for (o, n), s in zip(offs, shapes)]


BIG = (("ffn1_w_gate_up", "col"), ("ffn1_w_down", "row"), ("w_in", "col"), ("w_out_a", "row"), ("w_out_ssm", "row"),
       ("w_mix_out", "row"), ("w_q", "row"), ("w_kv", "col"), ("w_o_x", "row"), ("ffn2_w_gate_up", "col"),
       ("ffn2_w_down", "row"))
SMALL = ("ffn1_norm", "mix_norm", "conv_a_w", "ssm_conv_w", "ssm_conv_b", "ssm_dt_bias", "ssm_a_log", "ssm_d",
         "ssm_norm", "xattn_norm", "mem_norm", "ffn2_norm", "final_norm")
WEIGHTS = ("ffn1_norm", "ffn1_w_gate_up", "ffn1_w_down", "mix_norm", "w_in", "conv_a_w", "w_out_a", "ssm_conv_w",
           "ssm_conv_b", "ssm_dt_bias", "ssm_a_log", "ssm_d", "ssm_norm", "w_out_ssm", "w_mix_out", "xattn_norm",
           "mem_norm", "w_q", "w_kv", "w_o_x", "ffn2_norm", "ffn2_w_gate_up", "ffn2_w_down", "final_norm")


GATHER_GROUPS = (("a", ("ffn1_w_gate_up", "ffn1_w_down")), ("b", ("w_in",)),
                 ("c", ("w_out_a", "w_out_ssm", "w_mix_out", "w_q", "w_kv", "w_o_x", "ffn2_w_gate_up", "ffn2_w_down")))


def _full_weight(land, own, kind, k):
    parts = [jnp.where(k == kk, own, land[kk]) for kk in range(4)]
    return jnp.concatenate(parts, axis=0 if kind == "row" else 1)


class _GatheredWeights:
    def __init__(self, shards, started, after, k):
        self.shards, self.started, self.after, self.k = shards, started, after, k
        self.full = {}

    def mark(self, value):
        self.after = value

    def __getitem__(self, name):
        if name not in self.full:
            tag, names = next(grp for grp in GATHER_GROUPS if name in grp[1])
            lands = _gather_finish(self.started[tag], self.after, tag)
            for n, land in zip(names, lands):
                w = _full_weight(land, self.shards[n], dict(BIG)[n], self.k)
                self.full[n] = _pad_w_in(w) if n == "w_in" else w
        return self.full[name]


def _shard_major(dw, kind):
    if isinstance(dw, tuple):
        return jnp.concatenate([_shard_major(p, "col2") for p in dw], axis=0)
    if kind == "col2":
        return jnp.transpose(dw.reshape(dw.shape[0], 2, dw.shape[1] // 2), (1, 0, 2))
    if kind == "row":
        return dw.reshape(4, dw.shape[0] // 4, dw.shape[1])
    return jnp.transpose(dw.reshape(dw.shape[0], 4, dw.shape[1] // 4), (1, 0, 2))


def _pad_rows8(w):
    return jnp.concatenate([w, jnp.zeros((8 - w.shape[0], w.shape[1]), w.dtype)], axis=0)


def _group_lanes(v):
    r = v.shape[0]
    return jnp.pad(v.reshape(r, NG, NH // NG), ((0, 0), (0, 0), (0, 128 - NH // NG))).reshape(r, NG * 128)


def _ungroup_lanes(v):
    r = v.shape[0]
    return v.reshape(r, NG, 128)[:, :, :NH // NG].reshape(r, NH)


def _local_step(wfull, small, x, mem, target, token=0.0, on_grads=None):
    bl, s, _ = x.shape
    T = bl * s
    x2, t2 = x.reshape(T, D), target.reshape(T, D)
    mem2 = mem.reshape(-1, D)
    g = {}
    tok = [token]
    mark = getattr(wfull, "mark", lambda value: None)

    def gain(name):
        return small[name].reshape(1, -1) + tok[0]

    def emit(tag, names):
        if on_grads is not None:
            tok[0] = tok[0] + on_grads(tag, {n: g[n] for n in names})

    def ffn_fwd(h, norm, wgu, wd, tag):
        n = _norm_fwd(h, gain(norm), name=f"{tag}_norm")
        gate, up, a = _gate_up_fwd(n, wfull[wgu], name=f"{tag}_gate_up")
        out = _mm(a, wfull[wd], "nn", tk=DFF, scale=FFN_RES, residual=h, name=f"{tag}_down")
        return out, (n, gate, up, a)

    def ffn_bwd(dh, h, norm, wgu, wd, saved, tag):
        n, gate, up, a = saved
        dgate, dup = _act_bwd(dh, wfull[wd], gate, up, FFN_RES, name=f"{tag}_d_act")
        g[wd] = _mm(a, dh, "tn", tm=1408, scale=FFN_RES, name=f"{tag}_d_w_down")
        g[wgu] = (_mm(n, dgate, "tn", tn=1408, name=f"{tag}_d_w_gate"), _mm(n, dup, "tn", tn=1408, name=f"{tag}_d_w_up"))
        emit(tag, (wgu, wd))
        dn = _mm(dgate, wfull[wgu], "nt", a2=dup, tk=1408, name=f"{tag}_d_norm_out")
        dh_in, g[norm] = _norm_bwd(h, gain(norm), dn, dh, name=f"{tag}_d_norm")
        return dh_in

    h1, ffn1_saved = ffn_fwd(x2, "ffn1_norm", "ffn1_w_gate_up", "ffn1_w_down", "ffn1")
    mark(h1)
    u = _norm_fwd(h1, gain("mix_norm"), name="mix_norm")
    pp = _mm(u, wfull["w_in"], "nn", tn=1152, name="in_proj")
    wa8 = _pad_rows8(small["conv_a_w"])
    ws8 = _pad_rows8(small["ssm_conv_w"])
    conv_b = gain("ssm_conv_b")
    bias128 = jnp.pad(gain("ssm_dt_bias"), ((0, 0), (0, 128 - NH)))
    ya_pre = _conv_a_fwd(pp, wa8, bl, s, name="conv_a")
    xc = _conv_ssm_fwd(pp, ws8, conv_b, bl, s, name="conv_ssm")
    mark(xc)
    dt = _dt_fwd(pp, bias128, name="dt")
    dtg = _group_lanes(dt[:, :NH])
    dtt = dt[:, :NH].T.reshape(NG, NH // NG, T)
    alog = gain("ssm_a_log")
    arow, acol = _group_lanes(alog), alog.reshape(NG, NH // NG, 1)
    dexp = jnp.repeat(gain("ssm_d"), HD, axis=1)
    ng = gain("ssm_norm")
    y, yn, prev = _ssd_fwd(xc, pp, dtg, dtt, arow, acol, dexp, ng, bl, s, name="ssd")
    ya = _mm(ya_pre, wfull["w_out_a"], "nn", tn=1024, name="out_a")
    yb = _mm(yn, wfull["w_out_ssm"], "nn", tn=1024, tk=DI, name="out_ssm")
    merged = _merge_fwd(pp, ya, yb, name="merge")
    h2 = _mm(merged, wfull["w_mix_out"], "nn", tn=1024, residual=h1, name="mix_out")
    un = _norm_fwd(h2, gain("xattn_norm"), name="xattn_norm")
    q = _mm(un, wfull["w_q"], "nn", tn=1024, out_dtype=BF16, name="q_proj")
    mn = _norm_fwd(mem2, gain("mem_norm"), name="mem_norm")
    kv = _mm(mn, wfull["w_kv"], "nn", tn=1024, out_dtype=BF16, name="kv_proj")
    o = _attn_fwd(q, kv, bl, s, name="attn")
    h3 = _mm(o, wfull["w_o_x"], "nn", tn=1024, residual=h2, name="attn_out")
    h4, ffn2_saved = ffn_fwd(h3, "ffn2_norm", "ffn2_w_gate_up", "ffn2_w_down", "ffn2")
    sq_err, dh4, dgf = _final_loss(h4, gain("final_norm"), t2, name="final_loss")
    g["final_norm"] = dgf

    dh3 = ffn_bwd(dh4, h3, "ffn2_norm", "ffn2_w_gate_up", "ffn2_w_down", ffn2_saved, "ffn2")
    do = _mm(dh3, wfull["w_o_x"], "nt", tn=1024, out_dtype=BF16, name="d_attn_o")
    g["w_o_x"] = _mm(o, dh3, "tn",name="d_w_o_x")
    dq, dk, dv = _attn_bwd(q, kv, do, bl, s, name="d_attn")
    dun = _mm(dq, wfull["w_q"], "nt", tn=1024, name="d_xattn_norm_out")
    g["w_q"] = _mm(un, dq, "tn",name="d_w_q")
    dkv = jnp.concatenate([dk, dv], axis=1)
    dmn = _mm(dkv, wfull["w_kv"], "nt", tn=1024, tk=2 * D, name="d_mem_norm_out")
    g["w_kv"] = _mm(mn, dkv, "tn",name="d_w_kv")
    emit("attn", ("w_q", "w_kv", "w_o_x"))
    _, g["mem_norm"] = _norm_bwd(mem2, gain("mem_norm"), dmn, None, name="d_mem_norm")
    dh2, g["xattn_norm"] = _norm_bwd(h2, gain("xattn_norm"), dun, dh3, name="d_xattn_norm")
    dmerged = _mm(dh2, wfull["w_mix_out"], "nt", tn=1024, name="d_merged")
    g["w_mix_out"] = _mm(merged, dh2, "tn",name="d_w_mix_out")
    dya, dyb, dga, dgb = _merge_bwd(pp, ya, yb, dmerged, name="d_merge")
    dya_pre = _mm(dya, wfull["w_out_a"], "nt", tn=1024, name="d_conv_a_out")
    g["w_out_a"] = _mm(ya_pre, dya, "tn",name="d_w_out_a")
    dyn = _mm(dyb, wfull["w_out_ssm"], "nt", tn=DI, name="d_ssd_out")
    g["w_out_ssm"] = _mm(yn, dyb, "tn",name="d_w_out_ssm")
    d_ab, d_ac, d_av, dwa8 = _conv_a_bwd(pp, wa8, dya_pre, bl, s, name="d_conv_a")
    g["conv_a_w"] = dwa8[:3]
    dz, dxs, dbm, dcm, ddtg, g["ssm_norm"], ddexp, dalg = _ssd_bwd(
        dyn, y, xc, pp, dtg, dtt, arow, acol, dexp, ng, prev, bl, s, name="d_ssd")
    g["ssm_d"] = ddexp.reshape(NH, HD).sum(axis=1).reshape(1, NH)
    g["ssm_a_log"] = _ungroup_lanes(dalg)
    conv_parts = [_conv_ssm_bwd(pp, ws8, conv_b, dpart, off, bl, s, name=f"d_conv_ssm_{tag}")
                  for dpart, off, tag in ((dxs, 0, "x"), (dbm, DI, "b"), (dcm, DI + NG * NS, "c"))]
    g["ssm_conv_w"] = jnp.concatenate([p[1] for p in conv_parts], axis=1)[:4]
    g["ssm_conv_b"] = jnp.concatenate([p[2] for p in conv_parts], axis=1)
    ddt = jnp.pad(_ungroup_lanes(ddtg), ((0, 0), (0, 128 - NH)))
    ddt_raw, dbias = _dt_bwd(pp, bias128, ddt, name="d_dt")
    g["ssm_dt_bias"] = dbias[:, :NH]
    dpp = jnp.concatenate([d_ab, d_ac, d_av, dz] + [p[0] for p in conv_parts] + [dga, dgb, ddt_raw], axis=1)
    g["w_in"] = _mm(u, dpp, "tn", tn=1152, name="d_w_in")
    emit("mix", ("w_in", "w_out_a", "w_out_ssm", "w_mix_out"))
    du = _mm(dpp, wfull["w_in"], "nt", tk=3456, name="d_mix_norm_out")
    dh1, g["mix_norm"] = _norm_bwd(h1, gain("mix_norm"), du, dh2, name="d_mix_norm")
    dx = ffn_bwd(dh1, x2, "ffn1_norm", "ffn1_w_gate_up", "ffn1_w_down", ffn1_saved, "ffn1")
    return sq_err, dx, g


def _pad_w_in(w):
    return jnp.concatenate([w[:, :O_GA], w[:, O_GA + NH:], w[:, O_GA:O_GA + NH],
                            jnp.zeros((w.shape[0], NPP - NIN), w.dtype)], axis=1)


def _unpad_w_in(w):
    return jnp.concatenate([w[:, :O_GA], w[:, O_DT:O_DT + NH], w[:, O_GA:O_DT]], axis=1)


def kernel(x, mem, ffn1_norm, ffn1_w_gate_up, ffn1_w_down, mix_norm, w_in, conv_a_w, w_out_a, ssm_conv_w, ssm_conv_b, ssm_dt_bias, ssm_a_log, ssm_d, ssm_norm, w_out_ssm, w_mix_out, xattn_norm, mem_norm, w_q, w_kv, w_o_x, ffn2_norm, ffn2_w_gate_up, ffn2_w_down, final_norm, loss_target, m_ffn1_norm, m_ffn1_w_gate_up, m_ffn1_w_down, m_mix_norm, m_w_in, m_conv_a_w, m_w_out_a, m_ssm_conv_w, m_ssm_conv_b, m_ssm_dt_bias, m_ssm_a_log, m_ssm_d, m_ssm_norm, m_w_out_ssm, m_w_mix_out, m_xattn_norm, m_mem_norm, m_w_q, m_w_kv, m_w_o_x, m_ffn2_norm, m_ffn2_w_gate_up, m_ffn2_w_down, m_final_norm, v_ffn1_norm, v_ffn1_w_gate_up, v_ffn1_w_down, v_mix_norm, v_w_in, v_conv_a_w, v_w_out_a, v_ssm_conv_w, v_ssm_conv_b, v_ssm_dt_bias, v_ssm_a_log, v_ssm_d, v_ssm_norm, v_w_out_ssm, v_w_mix_out, v_xattn_norm, v_mem_norm, v_w_q, v_w_kv, v_w_o_x, v_ffn2_norm, v_ffn2_w_gate_up, v_ffn2_w_down, v_final_norm):
    a = dict(locals())
    xi, yi = lax.axis_index("x"), lax.axis_index("y")
    k = 2 * xi + yi

    shards = {n: a[n][0].astype(BF16) for n, _ in BIG}
    started, token = {}, jnp.zeros((), F32)
    for tag, names in GATHER_GROUPS:
        started[tag], tk = _gather_start([shards[n] for n in names], tag)
        token = token + tk[0, 0]
    wfull = _GatheredWeights(shards, started, tk, k)
    conv_vec, conv_offs = _pack([a["conv_a_w"], a["ssm_conv_w"]])
    conv_all = _gather_all(conv_vec, name="gather_conv_w")
    conv_sh = [_unpack(conv_all[2 * kk], conv_offs, [a["conv_a_w"].shape[1:], a["ssm_conv_w"].shape[1:]])
               for kk in range(4)]
    small = {n: a[n] for n in SMALL}
    small["conv_a_w"] = jnp.concatenate([cs[0] for cs in conv_sh], axis=1)
    small["ssm_conv_w"] = jnp.concatenate([cs[1] for cs in conv_sh], axis=1)

    rs_started = []

    def on_grads(tag, grads):
        names = [n for n, _ in BIG if n in grads]
        shard_major = [_shard_major(_unpad_w_in(grads[n]) if n == "w_in" else grads[n], dict(BIG)[n]) for n in names]
        st, tk = _rs_start(shard_major, tag)
        rs_started.append((tag, names, st))
        return tk[0, 0]

    sq_err, dx, g = _local_step(wfull, small, x, mem, loss_target, token, on_grads)
    loss = lax.psum(0.5 / D * jnp.sum(sq_err), ("x", "y", "c"))

    ci = lax.axis_index("c")
    out = {}
    for tag, names, st in rs_started:
        g_mine, g_other = _rs_finish(st, dx, tag)
        for n, gm, go in zip(names, g_mine, g_other):
            res = _adamw_halves(a[n][0], gm, go, a["m_" + n][0], a["v_" + n][0], ci, name=f"adamw_{n}")
            out[n] = tuple(t.reshape(a[n].shape) for t in res)

    full_shapes = [g[n].shape for n in SMALL]
    gvec, goffs = _pack([g[n] for n in SMALL])
    gsum = _sum_leading(_gather_all(gvec, name="gather_small_grads"), name="sum_small_grads")
    gsmall = dict(zip(SMALL, _unpack(gsum, goffs, full_shapes)))
    for n in ("conv_a_w", "ssm_conv_w"):
        width = a[n].shape[2]
        gsmall[n] = lax.dynamic_slice_in_dim(gsmall[n], k * width, width, axis=1)
    local_shapes = [a[n].shape for n in SMALL]
    packs = [_pack([t[n] for n in SMALL]) for t in
             ({n: a[n] for n in SMALL}, gsmall, {n: a["m_" + n] for n in SMALL}, {n: a["v_" + n] for n in SMALL})]
    offs = packs[0][1]
    res = _adamw(*[p[0] for p in packs], name="adamw_small")
    unp = [_unpack(r, offs, local_shapes) for r in res]
    for i, n in enumerate(SMALL):
        out[n] = (gsmall[n].reshape(a[n].shape), unp[0][i], unp[1][i], unp[2][i])

    grad_x = dx.reshape(x.shape)
    return (loss, grad_x, *[out[n][0] for n in WEIGHTS], *[out[n][1] for n in WEIGHTS],
            *[out[n][2] for n in WEIGHTS], *[out[n][3] for n in WEIGHTS])
```

```python
import functools
import math

import jax
import jax.numpy as jnp
from jax import lax
from jax.experimental import pallas as pl
from jax.experimental.pallas import tpu as pltpu

F32 = jnp.float32
BF16 = jnp.bfloat16
MXU = jnp.bfloat16
HI = lax.Precision.HIGHEST

D = 1024
DFF = 2816
DI = 2048
NH, HD, NG, NS, CH = 32, 64, 4, 128, 128
GW = DI // NG
XH, XD = 4, 256
EPS = 1e-6
NEG = -1e30
O_AB, O_AC, O_AV, O_Z, O_XBC, O_GA, O_GB, O_DT, NPP = 0, 1024, 2048, 3072, 5120, 8192, 9216, 10240, 10368
NIN = 10272
FFN_RES = 0.5
ADAM_LR, ADAM_B1, ADAM_B2, ADAM_EPS, ADAM_WD, ADAM_STEP = 0.001, 0.9, 0.999, 1e-08, 0.01, 10
VMEM_LIMIT = 56 * 1024 * 1024
MESH = pl.DeviceIdType.MESH
CHIP_FLIPS = ((1, 0), (0, 1), (1, 1))


def _cp(*sem):
    return pltpu.CompilerParams(dimension_semantics=sem, vmem_limit_bytes=VMEM_LIMIT)


def _tile(n, pref, align=128):
    if n <= pref:
        return n
    t = (pref // align) * align
    while t >= align:
        if n % t == 0:
            return t
        t -= align
    raise ValueError((n, pref))


def _dot(a, b, dims, prec=None):
    return lax.dot_general(a, b, (dims, ((), ())), preferred_element_type=F32, precision=prec)


def _nn(a, b, prec=None):
    return _dot(a, b, ((1,), (0,)), prec)


def _nt(a, b):
    return _dot(a, b, ((1,), (1,)))


def _tn(a, b):
    return _dot(a, b, ((0,), (0,)))


def _sig(x):
    return jax.nn.sigmoid(x)


def _mm(a, b, mode, *, name, tm=1024, tn=1024, tk=1024, out_dtype=F32, scale=None, residual=None, a2=None):
    if mode == "nn":
        (M, K), (K2, N) = a.shape, b.shape
    elif mode == "nt":
        (M, K), (N, K2) = a.shape, b.shape
        if a2 is not None:
            assert a2.shape == a.shape
            K2 = K2 // 2
    else:
        (K, M), (K2, N) = a.shape, b.shape
    assert K == K2, (name, a.shape, b.shape)
    tm, tn, tk = _tile(M, tm), _tile(N, tn), _tile(K, tk)
    nk = K // tk
    if mode == "nn":
        a_spec = pl.BlockSpec((tm, tk), lambda i, j, k: (i, k))
        b_spec = pl.BlockSpec((tk, tn), lambda i, j, k: (k, j))
        dims = ((1,), (0,))
    elif mode == "nt":
        a_spec = pl.BlockSpec((tm, tk), lambda i, j, k: (i, k))
        b_spec = pl.BlockSpec((tn, tk), lambda i, j, k: (j, k))
        dims = ((1,), (1,))
    else:
        a_spec = pl.BlockSpec((tk, tm), lambda i, j, k: (k, i))
        b_spec = pl.BlockSpec((tk, tn), lambda i, j, k: (k, j))
        dims = ((0,), (0,))
    o_spec = pl.BlockSpec((tm, tn), lambda i, j, k: (i, j))
    has_res = residual is not None

    def finish(acc, r_ref, o_ref):
        if scale is not None:
            acc = acc * scale
        if has_res:
            acc = acc + r_ref[...]
        o_ref[...] = acc.astype(out_dtype)

    dual = a2 is not None
    n_in = 2 + 2 * dual + has_res

    def body(*refs):
        a_ref, b_ref = refs[0], refs[1]
        r_ref = refs[n_in - 1] if has_res else None
        o_ref = refs[n_in]
        part = _dot(a_ref[...].astype(MXU), b_ref[...].astype(MXU), dims)
        if dual:
            part = part + _dot(refs[2][...].astype(MXU), refs[3][...].astype(MXU), dims)
        if nk == 1:
            finish(part, r_ref, o_ref)
            return
        acc_ref = refs[-1]
        k = pl.program_id(2)

        @pl.when(k == 0)
        def _():
            acc_ref[...] = part

        @pl.when(k > 0)
        def _():
            acc_ref[...] += part

        @pl.when(k == nk - 1)
        def _():
            finish(acc_ref[...], r_ref, o_ref)

    ins, in_specs = [a, b], [a_spec, b_spec]
    if dual:
        ins += [a2, b]
        in_specs += [a_spec, pl.BlockSpec((tn, tk), lambda i, j, k: (j, k + nk))]
    if has_res:
        ins.append(residual)
        in_specs.append(o_spec)
    return pl.pallas_call(
        body, grid=(M // tm, N // tn, nk), in_specs=in_specs, out_specs=o_spec,
        out_shape=jax.ShapeDtypeStruct((M, N), out_dtype),
        scratch_shapes=[pltpu.VMEM((tm, tn), F32)] if nk > 1 else [],
        compiler_params=_cp("parallel", "parallel", "arbitrary"), name=name)(*ins)


def _norm_fwd(x, g, *, name):
    T, d = x.shape
    tr = _tile(T, 512, 8)

    def body(x_ref, g_ref, o_ref):
        xv = x_ref[...]
        r = lax.rsqrt(jnp.mean(xv * xv, axis=-1, keepdims=True) + EPS)
        o_ref[...] = (xv * r * g_ref[...]).astype(BF16)

    return pl.pallas_call(
        body, grid=(T // tr,),
        in_specs=[pl.BlockSpec((tr, d), lambda i: (i, 0)), pl.BlockSpec((1, d), lambda i: (0, 0))],
        out_specs=pl.BlockSpec((tr, d), lambda i: (i, 0)),
        out_shape=jax.ShapeDtypeStruct((T, d), BF16), compiler_params=_cp("parallel"), name=name)(x, g)


def _norm_bwd(x, g, dn, dres, *, name):
    T, d = x.shape
    tr = _tile(T, 512, 8)
    has_res = dres is not None

    def body(*refs):
        x_ref, g_ref, dn_ref = refs[:3]
        dr_ref = refs[3] if has_res else None
        dx_ref, dg_ref = refs[-2], refs[-1]

        @pl.when(pl.program_id(0) == 0)
        def _():
            dg_ref[...] = jnp.zeros_like(dg_ref)

        xv = x_ref[...]
        dnv = dn_ref[...].astype(F32)
        r = lax.rsqrt(jnp.mean(xv * xv, axis=-1, keepdims=True) + EPS)
        xh = xv * r
        dg_ref[...] += jnp.sum(dnv * xh, axis=0, keepdims=True)
        dxh = dnv * g_ref[...]
        dx = r * (dxh - xh * jnp.mean(dxh * xh, axis=-1, keepdims=True))
        if has_res:
            dx = dx + dr_ref[...]
        dx_ref[...] = dx

    row = pl.BlockSpec((tr, d), lambda i: (i, 0))
    vec = pl.BlockSpec((1, d), lambda i: (0, 0))
    ins = [x, g, dn] + ([dres] if has_res else [])
    return pl.pallas_call(
        body, grid=(T // tr,), in_specs=[row, vec, row] + ([row] if has_res else []),
        out_specs=[row, vec],
        out_shape=[jax.ShapeDtypeStruct((T, d), F32), jax.ShapeDtypeStruct((1, d), F32)],
        compiler_params=_cp("arbitrary"), name=name)(*ins)


def _final_loss(h, g, target, *, name):
    T, d = h.shape
    tr = _tile(T, 512, 8)

    def body(h_ref, g_ref, t_ref, l_ref, dh_ref, dg_ref):
        @pl.when(pl.program_id(0) == 0)
        def _():
            l_ref[...] = jnp.zeros_like(l_ref)
            dg_ref[...] = jnp.zeros_like(dg_ref)

        xv = h_ref[...]
        r = lax.rsqrt(jnp.mean(xv * xv, axis=-1, keepdims=True) + EPS)
        xh = xv * r
        e = xh * g_ref[...] - t_ref[...]
        l_ref[...] += jnp.sum(e * e, axis=0, keepdims=True)
        dy = e * (1.0 / d)
        dg_ref[...] += jnp.sum(dy * xh, axis=0, keepdims=True)
        dxh = dy * g_ref[...]
        dh_ref[...] = r * (dxh - xh * jnp.mean(dxh * xh, axis=-1, keepdims=True))

    row = pl.BlockSpec((tr, d), lambda i: (i, 0))
    vec = pl.BlockSpec((1, d), lambda i: (0, 0))
    return pl.pallas_call(
        body, grid=(T // tr,), in_specs=[row, vec, row], out_specs=[vec, row, vec],
        out_shape=[jax.ShapeDtypeStruct((1, d), F32), jax.ShapeDtypeStruct((T, d), F32),
                   jax.ShapeDtypeStruct((1, d), F32)],
        compiler_params=_cp("arbitrary"), name=name)(h, g, target)


def _gate_up_fwd(n, wgu, *, name):
    T, d = n.shape
    f = wgu.shape[1] // 2
    tm, tn = _tile(T, 512, 8), _tile(f, 1408)
    nf = f // tn

    def body(n_ref, wg_ref, wu_ref, g_ref, u_ref, a_ref):
        nv = n_ref[...].astype(MXU)
        gv = _nn(nv, wg_ref[...].astype(MXU))
        uv = _nn(nv, wu_ref[...].astype(MXU))
        g_ref[...] = gv.astype(BF16)
        u_ref[...] = uv.astype(BF16)
        a_ref[...] = (gv * _sig(gv) * uv).astype(BF16)

    out = pl.BlockSpec((tm, tn), lambda i, j: (i, j))
    act = jax.ShapeDtypeStruct((T, f), BF16)
    return pl.pallas_call(
        body, grid=(T // tm, nf),
        in_specs=[pl.BlockSpec((tm, d), lambda i, j: (i, 0)), pl.BlockSpec((d, tn), lambda i, j: (0, j)),
                  pl.BlockSpec((d, tn), lambda i, j: (0, j + nf))],
        out_specs=[out, out, out], out_shape=[act, act, act], compiler_params=_cp("parallel", "parallel"),
        name=name)(n, wgu, wgu)


def _act_bwd(dh, wd, gate, up, scale, *, name):
    T, d = dh.shape
    f = wd.shape[0]
    tm, tn = _tile(T, 512, 8), _tile(f, 1408)

    def body(dh_ref, wd_ref, g_ref, u_ref, dg_ref, du_ref):
        da = scale * _nt(dh_ref[...].astype(MXU), wd_ref[...].astype(MXU))
        gv, uv = g_ref[...].astype(F32), u_ref[...].astype(F32)
        s = _sig(gv)
        dg_ref[...] = (da * uv * (s * (1.0 + gv * (1.0 - s)))).astype(BF16)
        du_ref[...] = (da * (gv * s)).astype(BF16)

    tile = pl.BlockSpec((tm, tn), lambda i, j: (i, j))
    act = jax.ShapeDtypeStruct((T, f), BF16)
    return pl.pallas_call(
        body, grid=(T // tm, f // tn),
        in_specs=[pl.BlockSpec((tm, d), lambda i, j: (i, 0)), pl.BlockSpec((tn, d), lambda i, j: (j, 0)), tile, tile],
        out_specs=[tile, tile], out_shape=[act, act], compiler_params=_cp("parallel", "parallel"),
        name=name)(dh, wd, gate, up)


def _shift_down(x, d, ri):
    if d == 0:
        return x
    return jnp.where(ri >= d, pltpu.roll(x, d, 0), 0.0)


def _shift_up(x, d, ri):
    if d == 0:
        return x
    s = x.shape[0]
    return jnp.where(ri < s - d, pltpu.roll(x, s - d, 0), 0.0)


def _conv_taps(x, w_ref, ktaps, ri):
    acc = None
    for k in range(ktaps):
        t = w_ref[k:k + 1, :] * _shift_down(x, ktaps - 1 - k, ri)
        acc = t if acc is None else acc + t
    return acc


def _conv_a_fwd(pp, w8, bl, s, *, name):
    tc = 256
    nb = D // tc

    def body(b_ref, c_ref, v_ref, w_ref, o_ref):
        ri = lax.broadcasted_iota(jnp.int32, (s, tc), 0)
        cv = c_ref[...] * v_ref[...]
        o_ref[...] = (b_ref[...] * _conv_taps(cv, w_ref, 3, ri)).astype(BF16)

    def col(off):
        return pl.BlockSpec((s, tc), lambda b, j: (b, off // tc + j))

    return pl.pallas_call(
        body, grid=(bl, nb),
        in_specs=[col(O_AB), col(O_AC), col(O_AV), pl.BlockSpec((8, tc), lambda b, j: (0, j))],
        out_specs=pl.BlockSpec((s, tc), lambda b, j: (b, j)),
        out_shape=jax.ShapeDtypeStruct((bl * s, D), BF16), compiler_params=_cp("parallel", "parallel"),
        name=name)(pp, pp, pp, w8)


def _conv_a_bwd(pp, w8, dya, bl, s, *, name):
    tc = 256
    nb = D // tc

    def body(b_ref, c_ref, v_ref, w_ref, dy_ref, db_ref, dc_ref, dv_ref, dw_ref):
        @pl.when(pl.program_id(1) == 0)
        def _():
            dw_ref[...] = jnp.zeros_like(dw_ref)

        ri = lax.broadcasted_iota(jnp.int32, (s, tc), 0)
        cvec, vvec, dy = c_ref[...], v_ref[...], dy_ref[...]
        cv = cvec * vvec
        db_ref[...] = (dy * _conv_taps(cv, w_ref, 3, ri)).astype(BF16)
        dconv = dy * b_ref[...]
        dcv = None
        for k in range(3):
            t = w_ref[k:k + 1, :] * _shift_up(dconv, 2 - k, ri)
            dcv = t if dcv is None else dcv + t
            dw_ref[k:k + 1, :] += jnp.sum(dconv * _shift_down(cv, 2 - k, ri), axis=0, keepdims=True)
        dc_ref[...] = (dcv * vvec).astype(BF16)
        dv_ref[...] = (dcv * cvec).astype(BF16)

    def col(off):
        return pl.BlockSpec((s, tc), lambda j, b: (b, off // tc + j))

    plain = pl.BlockSpec((s, tc), lambda j, b: (b, j))
    wspec = pl.BlockSpec((8, tc), lambda j, b: (0, j))
    act = jax.ShapeDtypeStruct((bl * s, D), BF16)
    return pl.pallas_call(
        body, grid=(nb, bl), in_specs=[col(O_AB), col(O_AC), col(O_AV), wspec, plain],
        out_specs=[plain, plain, plain, wspec],
        out_shape=[act, act, act, jax.ShapeDtypeStruct((8, D), F32)],
        compiler_params=_cp("parallel", "arbitrary"), name=name)(pp, pp, pp, w8, dya)


def _conv_ssm_fwd(pp, w8, bias, bl, s, *, name):
    tc = 256
    width = DI + 2 * NG * NS
    nb = width // tc

    def body(x_ref, w_ref, b_ref, o_ref):
        ri = lax.broadcasted_iota(jnp.int32, (s, tc), 0)
        pre = _conv_taps(x_ref[...], w_ref, 4, ri) + b_ref[...]
        o_ref[...] = pre * _sig(pre)

    return pl.pallas_call(
        body, grid=(bl, nb),
        in_specs=[pl.BlockSpec((s, tc), lambda b, j: (b, O_XBC // tc + j)),
                  pl.BlockSpec((8, tc), lambda b, j: (0, j)), pl.BlockSpec((1, tc), lambda b, j: (0, j))],
        out_specs=pl.BlockSpec((s, tc), lambda b, j: (b, j)),
        out_shape=jax.ShapeDtypeStruct((bl * s, width), F32), compiler_params=_cp("parallel", "parallel"),
        name=name)(pp, w8, bias)


def _conv_ssm_bwd(pp, w8, bias, dxc, ch_off, bl, s, *, name):
    n = dxc.shape[1]
    tc = 256
    nb = n // tc
    o0 = ch_off // tc

    def body(x_ref, w_ref, b_ref, d_ref, dx_ref, dw_ref, db_ref):
        @pl.when(pl.program_id(1) == 0)
        def _():
            dw_ref[...] = jnp.zeros_like(dw_ref)
            db_ref[...] = jnp.zeros_like(db_ref)

        ri = lax.broadcasted_iota(jnp.int32, (s, tc), 0)
        xv = x_ref[...]
        pre = _conv_taps(xv, w_ref, 4, ri) + b_ref[...]
        sg = _sig(pre)
        dpre = d_ref[...] * (sg * (1.0 + pre * (1.0 - sg)))
        db_ref[...] += jnp.sum(dpre, axis=0, keepdims=True)
        dx = None
        for k in range(4):
            t = w_ref[k:k + 1, :] * _shift_up(dpre, 3 - k, ri)
            dx = t if dx is None else dx + t
            dw_ref[k:k + 1, :] += jnp.sum(dpre * _shift_down(xv, 3 - k, ri), axis=0, keepdims=True)
        dx_ref[...] = dx.astype(BF16)

    plain = pl.BlockSpec((s, tc), lambda j, b: (b, j))
    return pl.pallas_call(
        body, grid=(nb, bl),
        in_specs=[pl.BlockSpec((s, tc), lambda j, b: (b, O_XBC // tc + o0 + j)),
                  pl.BlockSpec((8, tc), lambda j, b: (0, o0 + j)), pl.BlockSpec((1, tc), lambda j, b: (0, o0 + j)),
                  plain],
        out_specs=[plain, pl.BlockSpec((8, tc), lambda j, b: (0, j)), pl.BlockSpec((1, tc), lambda j, b: (0, j))],
        out_shape=[jax.ShapeDtypeStruct((bl * s, n), BF16), jax.ShapeDtypeStruct((8, n), F32),
                   jax.ShapeDtypeStruct((1, n), F32)],
        compiler_params=_cp("parallel", "arbitrary"), name=name)(pp, w8, bias, dxc)


def _softplus(x):
    return jnp.maximum(x, 0.0) + jnp.log1p(jnp.exp(-jnp.abs(x)))


def _dt_fwd(pp, bias128, *, name):
    T = pp.shape[0]
    tr = _tile(T, 1024, 8)

    def body(x_ref, b_ref, o_ref):
        lane = lax.broadcasted_iota(jnp.int32, (tr, 128), 1)
        o_ref[...] = jnp.where(lane < NH, _softplus(x_ref[...] + b_ref[...]), 0.0)

    return pl.pallas_call(
        body, grid=(T // tr,),
        in_specs=[pl.BlockSpec((tr, 128), lambda i: (i, O_DT // 128)), pl.BlockSpec((1, 128), lambda i: (0, 0))],
        out_specs=pl.BlockSpec((tr, 128), lambda i: (i, 0)),
        out_shape=jax.ShapeDtypeStruct((T, 128), F32), compiler_params=_cp("parallel"), name=name)(pp, bias128)


def _dt_bwd(pp, bias128, ddt, *, name):
    T = pp.shape[0]
    tr = _tile(T, 1024, 8)

    def body(x_ref, b_ref, d_ref, o_ref, db_ref):
        @pl.when(pl.program_id(0) == 0)
        def _():
            db_ref[...] = jnp.zeros_like(db_ref)

        lane = lax.broadcasted_iota(jnp.int32, (tr, 128), 1)
        dr = jnp.where(lane < NH, d_ref[...] * _sig(x_ref[...] + b_ref[...]), 0.0)
        db_ref[...] += jnp.sum(dr, axis=0, keepdims=True)
        o_ref[...] = dr.astype(BF16)

    row = pl.BlockSpec((tr, 128), lambda i: (i, 0))
    vec = pl.BlockSpec((1, 128), lambda i: (0, 0))
    return pl.pallas_call(
        body, grid=(T // tr,),
        in_specs=[pl.BlockSpec((tr, 128), lambda i: (i, O_DT // 128)), vec, row],
        out_specs=[row, vec],
        out_shape=[jax.ShapeDtypeStruct((T, 128), BF16), jax.ShapeDtypeStruct((1, 128), F32)],
        compiler_params=_cp("arbitrary"), name=name)(pp, bias128, ddt)


def _ssd_common(dt, dtt, arow, acol):
    ri = lax.broadcasted_iota(jnp.int32, (CH, CH), 0)
    ci = lax.broadcasted_iota(jnp.int32, (CH, CH), 1)
    tril = ri >= ci
    triu = ri <= ci
    acs_col = _nn(tril.astype(F32), dt * arow, HI)
    acs_row = _nn(dtt * acol, triu.astype(F32), HI)
    return tril, triu, acs_col, acs_row


def _pair_terms(q, dt, acs_col, acs_row, tril, triu, lo, with_t):
    ha, hb = 2 * q, 2 * q + 1
    col_a, col_b = acs_col[:, ha:ha + 1], acs_col[:, hb:hb + 1]
    row_a, row_b = acs_row[ha:ha + 1, :], acs_row[hb:hb + 1, :]
    last_a, last_b = acs_col[CH - 1:CH, ha:ha + 1], acs_col[CH - 1:CH, hb:hb + 1]
    out = dict(
        dtsel=jnp.where(lo, dt[:, ha:ha + 1], dt[:, hb:hb + 1]),
        d_a=jnp.exp(jnp.where(tril, col_a - row_a, NEG)), d_b=jnp.exp(jnp.where(tril, col_b - row_b, NEG)),
        esel=jnp.where(lo, jnp.exp(col_a), jnp.exp(col_b)),
        fsel=jnp.where(lo, jnp.exp(last_a - col_a), jnp.exp(last_b - col_b)),
        g_a=jnp.exp(last_a), g_b=jnp.exp(last_b))
    if with_t:
        out["dt_a"] = jnp.exp(jnp.where(triu, row_a - col_a, NEG))
        out["dt_b"] = jnp.exp(jnp.where(triu, row_b - col_b, NEG))
    return out


def _ssd_fwd(xc, pp, dtg, dtt, arow, acol, dexp, ng, bl, s, *, name):
    nc = s // CH
    T = bl * s

    def body(xs_ref, bm_ref, cm_ref, z_ref, dt_ref, dtt_ref, arow_ref, acol_ref, dexp_ref, ng_ref,
             y_ref, yn_ref, prev_ref, st_ref):
        @pl.when(pl.program_id(2) == 0)
        def _():
            st_ref[...] = jnp.zeros_like(st_ref)

        dt = dt_ref[...]
        tril, triu, acs_col, acs_row = _ssd_common(dt, dtt_ref[...], -jnp.exp(arow_ref[...]), -jnp.exp(acol_ref[...]))
        bm, cm = bm_ref[...].astype(MXU), cm_ref[...].astype(MXU)
        cb = _nt(cm, bm)
        lo = lax.broadcasted_iota(jnp.int32, (CH, 128), 1) < HD
        sub_lo = lax.broadcasted_iota(jnp.int32, (128, NS), 0) < HD
        for q in range(4):
            t = _pair_terms(q, dt, acs_col, acs_row, tril, triu, lo, False)
            x = xs_ref[:, 128 * q:128 * (q + 1)]
            xd = x * t["dtsel"]
            y = (_nn((cb * t["d_a"]).astype(MXU), jnp.where(lo, xd, 0.0).astype(MXU))
                 + _nn((cb * t["d_b"]).astype(MXU), jnp.where(lo, 0.0, xd).astype(MXU)))
            prev = st_ref[q]
            prev_ref[q] = prev
            y = y + t["esel"] * _nt(cm, prev.astype(MXU))
            st_ref[q] = prev * jnp.where(sub_lo, t["g_a"], t["g_b"]) + _tn((xd * t["fsel"]).astype(MXU), bm)
            y_ref[:, 128 * q:128 * (q + 1)] = y + dexp_ref[:, 128 * q:128 * (q + 1)] * x
        zv = z_ref[...]
        yg = y_ref[...] * (zv * _sig(zv))
        r = lax.rsqrt(jnp.mean(yg * yg, axis=-1, keepdims=True) + EPS)
        yn_ref[...] = (yg * r * ng_ref[...]).astype(BF16)

    def row(width, off_blocks):
        return pl.BlockSpec((CH, width), lambda g, b, c: (b * nc + c, off_blocks + g))

    return pl.pallas_call(
        body, grid=(NG, bl, nc),
        in_specs=[row(GW, 0), row(NS, DI // NS), row(NS, DI // NS + NG), row(GW, O_Z // GW), row(128, 0),
                  pl.BlockSpec((None, 8, CH), lambda g, b, c: (g, 0, b * nc + c)),
                  pl.BlockSpec((1, 128), lambda g, b, c: (0, g)),
                  pl.BlockSpec((None, 8, 1), lambda g, b, c: (g, 0, 0)),
                  pl.BlockSpec((1, GW), lambda g, b, c: (0, g)), pl.BlockSpec((1, GW), lambda g, b, c: (0, g))],
        out_specs=[row(GW, 0), row(GW, 0),
                   pl.BlockSpec((None, 4, 128, NS), lambda g, b, c: (b * nc + c, g, 0, 0))],
        out_shape=[jax.ShapeDtypeStruct((T, DI), F32), jax.ShapeDtypeStruct((T, DI), BF16),
                   jax.ShapeDtypeStruct((bl * nc, 16, 128, NS), F32)],
        scratch_shapes=[pltpu.VMEM((4, 128, NS), F32)],
        compiler_params=_cp("parallel", "parallel", "arbitrary"), name=name,
    )(xc, xc, xc, pp, dtg, dtt, arow, acol, dexp, ng)


def _ssd_bwd(dyn, y, xc, pp, dtg, dtt, arow, acol, dexp, ng, prev, bl, s, *, name):
    nc = s // CH
    T = bl * s

    def rsum(v):
        return jnp.sum(v, axis=1, keepdims=True)

    def asum(v):
        return jnp.sum(rsum(v), axis=0, keepdims=True)

    def body(dyn_ref, y_ref, xs_ref, bm_ref, cm_ref, z_ref, dt_ref, dtt_ref, arow_ref, acol_ref, dexp_ref, ng_ref,
             prev_ref, dz_ref, dxs_ref, db_ref, dc_ref, ddt_ref, dng_ref, dd_ref, dal_ref, dst_ref):
        @pl.when((pl.program_id(1) == 0) & (pl.program_id(2) == 0))
        def _():
            dng_ref[...] = jnp.zeros_like(dng_ref)
            dd_ref[...] = jnp.zeros_like(dd_ref)
            dal_ref[...] = jnp.zeros_like(dal_ref)

        @pl.when(pl.program_id(2) == 0)
        def _():
            dst_ref[...] = jnp.zeros_like(dst_ref)

        yv, zv = y_ref[...], z_ref[...]
        sz = _sig(zv)
        silu = zv * sz
        yg = yv * silu
        r = lax.rsqrt(jnp.mean(yg * yg, axis=-1, keepdims=True) + EPS)
        yh = yg * r
        dynv = dyn_ref[...]
        dng_ref[...] += jnp.sum(dynv * yh, axis=0, keepdims=True)
        dyh = dynv * ng_ref[...]
        dyg = r * (dyh - yh * jnp.mean(dyh * yh, axis=-1, keepdims=True))
        dz_ref[...] = (dyg * yv * (sz * (1.0 + zv * (1.0 - sz)))).astype(BF16)
        dxs_ref[...] = dyg * silu
        dd_ref[...] += jnp.sum(dxs_ref[...] * xs_ref[...], axis=0, keepdims=True)

        dt = dt_ref[...]
        arow_v = -jnp.exp(arow_ref[...])
        tril, triu, acs_col, acs_row = _ssd_common(dt, dtt_ref[...], arow_v, -jnp.exp(acol_ref[...]))
        bm, cm = bm_ref[...].astype(MXU), cm_ref[...].astype(MXU)
        cb, cbt = _nt(cm, bm), _nt(bm, cm)
        lane = lax.broadcasted_iota(jnp.int32, (CH, 128), 1)
        rowi = lax.broadcasted_iota(jnp.int32, (CH, 128), 0)
        lo = lane < HD
        sub_lo = lax.broadcasted_iota(jnp.int32, (128, NS), 0) < HD
        dcb = jnp.zeros((CH, CH), F32)
        dcbt = jnp.zeros((CH, CH), F32)
        dc_acc = jnp.zeros((CH, NS), F32)
        db_acc = jnp.zeros((CH, NS), F32)
        dacs = jnp.zeros((CH, 128), F32)
        ddtx = jnp.zeros((CH, 128), F32)
        for q in range(4):
            ha, hb = 2 * q, 2 * q + 1
            t = _pair_terms(q, dt, acs_col, acs_row, tril, triu, lo, True)
            x = xs_ref[:, 128 * q:128 * (q + 1)]
            dy = dxs_ref[:, 128 * q:128 * (q + 1)]
            xd = x * t["dtsel"]
            xd_m, dy_m = xd.astype(MXU), dy.astype(MXU)
            xd_lo, xd_hi = jnp.where(lo, xd, 0.0).astype(MXU), jnp.where(lo, 0.0, xd).astype(MXU)
            dy_lo, dy_hi = jnp.where(lo, dy, 0.0).astype(MXU), jnp.where(lo, 0.0, dy).astype(MXU)
            m_a, m_b = cb * t["d_a"], cb * t["d_b"]
            mt_a, mt_b = cbt * t["dt_a"], cbt * t["dt_b"]
            prev = prev_ref[q]
            dnext = dst_ref[q]
            prev_m, dnext_m = prev.astype(MXU), dnext.astype(MXU)
            bds = _nt(bm, dnext_m)
            dxd = _nn(mt_a.astype(MXU), dy_lo) + _nn(mt_b.astype(MXU), dy_hi) + t["fsel"] * bds
            yoff = t["esel"] * _nt(cm, prev_m)
            dye = dy * t["esel"]
            dye_m = dye.astype(MXU)
            xdf_m = (xd * t["fsel"]).astype(MXU)
            dst_ref[q] = dnext * jnp.where(sub_lo, t["g_a"], t["g_b"]) + _tn(dye_m, cm)
            dm_a, dm_b = _nt(dy_lo, xd_m), _nt(dy_hi, xd_m)
            dmt_a, dmt_b = _nt(xd_lo, dy_m), _nt(xd_hi, dy_m)
            dcb = dcb + dm_a * t["d_a"] + dm_b * t["d_b"]
            dcbt = dcbt + dmt_a * t["dt_a"] + dmt_b * t["dt_b"]
            tyf = dy * yoff - t["fsel"] * xd * bds
            tf = t["fsel"] * xd * bds
            ra = rsum(dm_a * m_a) - rsum(dmt_a * mt_a) + rsum(jnp.where(lo, tyf, 0.0))
            rb = rsum(dm_b * m_b) - rsum(dmt_b * mt_b) + rsum(jnp.where(lo, 0.0, tyf))
            dpp = dnext * prev
            ea = asum(jnp.where(lo, tf, 0.0)) + t["g_a"] * asum(jnp.where(sub_lo, dpp, 0.0))
            eb = asum(jnp.where(lo, 0.0, tf)) + t["g_b"] * asum(jnp.where(sub_lo, 0.0, dpp))
            is_last = rowi == CH - 1
            dacs = (dacs + jnp.where(lane == ha, ra + jnp.where(is_last, ea, 0.0), 0.0)
                    + jnp.where(lane == hb, rb + jnp.where(is_last, eb, 0.0), 0.0))
            tx = dxd * x
            ddtx = (ddtx + jnp.where(lane == ha, rsum(jnp.where(lo, tx, 0.0)), 0.0)
                    + jnp.where(lane == hb, rsum(jnp.where(lo, 0.0, tx)), 0.0))
            dxs_ref[:, 128 * q:128 * (q + 1)] = dxd * t["dtsel"] + dexp_ref[:, 128 * q:128 * (q + 1)] * dy
            dc_acc = dc_acc + _nn(dye_m, prev_m)
            db_acc = db_acc + _nn(xdf_m, dnext_m)
        dc_ref[...] = dc_acc + _nn(dcb.astype(MXU), bm)
        db_ref[...] = db_acc + _nn(dcbt.astype(MXU), cm)
        dla = _nn(triu.astype(F32), dacs, HI)
        ddt_ref[...] = dla * arow_v + ddtx
        dal_ref[...] += jnp.sum(dla * dt, axis=0, keepdims=True) * arow_v

    def row(width, off_blocks):
        return pl.BlockSpec((CH, width), lambda g, b, c: (b * nc + nc - 1 - c, off_blocks + g))

    gvec = pl.BlockSpec((1, GW), lambda g, b, c: (0, g))
    hvec = pl.BlockSpec((1, 128), lambda g, b, c: (0, g))
    return pl.pallas_call(
        body, grid=(NG, bl, nc),
        in_specs=[row(GW, 0), row(GW, 0), row(GW, 0), row(NS, DI // NS), row(NS, DI // NS + NG), row(GW, O_Z // GW),
                  row(128, 0), pl.BlockSpec((None, 8, CH), lambda g, b, c: (g, 0, b * nc + nc - 1 - c)),
                  hvec, pl.BlockSpec((None, 8, 1), lambda g, b, c: (g, 0, 0)), gvec, gvec,
                  pl.BlockSpec((None, 4, 128, NS), lambda g, b, c: (b * nc + nc - 1 - c, g, 0, 0))],
        out_specs=[row(GW, 0), row(GW, 0), row(NS, 0), row(NS, 0), row(128, 0), gvec, gvec, hvec],
        out_shape=[jax.ShapeDtypeStruct((T, DI), BF16), jax.ShapeDtypeStruct((T, DI), F32),
                   jax.ShapeDtypeStruct((T, NG * NS), F32), jax.ShapeDtypeStruct((T, NG * NS), F32),
                   jax.ShapeDtypeStruct((T, NG * 128), F32), jax.ShapeDtypeStruct((1, DI), F32),
                   jax.ShapeDtypeStruct((1, DI), F32), jax.ShapeDtypeStruct((1, NG * 128), F32)],
        scratch_shapes=[pltpu.VMEM((4, 128, NS), F32)],
        compiler_params=_cp("arbitrary", "arbitrary", "arbitrary"), name=name,
    )(dyn, y, xc, xc, xc, pp, dtg, dtt, arow, acol, dexp, ng, prev)


def _merge_fwd(pp, ya, yb, *, name):
    T = ya.shape[0]
    tr = _tile(T, 512, 8)

    def body(ga_ref, gb_ref, ya_ref, yb_ref, o_ref):
        o_ref[...] = (_sig(ga_ref[...]) * ya_ref[...] + _sig(gb_ref[...]) * yb_ref[...]).astype(BF16)

    row = pl.BlockSpec((tr, D), lambda i: (i, 0))
    return pl.pallas_call(
        body, grid=(T // tr,),
        in_specs=[pl.BlockSpec((tr, D), lambda i: (i, O_GA // D)), pl.BlockSpec((tr, D), lambda i: (i, O_GB // D)),
                  row, row],
        out_specs=row, out_shape=jax.ShapeDtypeStruct((T, D), BF16), compiler_params=_cp("parallel"),
        name=name)(pp, pp, ya, yb)


def _merge_bwd(pp, ya, yb, dm, *, name):
    T = ya.shape[0]
    tr = _tile(T, 512, 8)

    def body(ga_ref, gb_ref, ya_ref, yb_ref, dm_ref, dya_ref, dyb_ref, dga_ref, dgb_ref):
        sa, sb, dmv = _sig(ga_ref[...]), _sig(gb_ref[...]), dm_ref[...]
        dya_ref[...] = (dmv * sa).astype(BF16)
        dyb_ref[...] = (dmv * sb).astype(BF16)
        dga_ref[...] = (dmv * ya_ref[...] * (sa * (1.0 - sa))).astype(BF16)
        dgb_ref[...] = (dmv * yb_ref[...] * (sb * (1.0 - sb))).astype(BF16)

    row = pl.BlockSpec((tr, D), lambda i: (i, 0))
    act = jax.ShapeDtypeStruct((T, D), BF16)
    return pl.pallas_call(
        body, grid=(T // tr,),
        in_specs=[pl.BlockSpec((tr, D), lambda i: (i, O_GA // D)), pl.BlockSpec((tr, D), lambda i: (i, O_GB // D)),
                  row, row, row],
        out_specs=[row, row, row, row], out_shape=[act, act, act, act], compiler_params=_cp("parallel"),
        name=name)(pp, pp, ya, yb, dm)


def _softmax_rows(sc):
    e = jnp.exp(sc - jnp.max(sc, axis=-1, keepdims=True))
    return e / jnp.sum(e, axis=-1, keepdims=True)


def _attn_fwd(q, kv, bl, s, *, name):
    m = kv.shape[0] // bl
    tq = _tile(s, 512)
    nq = s // tq
    scale = 1.0 / math.sqrt(XD)

    def body(q_ref, k_ref, v_ref, o_ref):
        p = _softmax_rows(_nt(q_ref[...], k_ref[...]) * scale)
        o_ref[...] = _nn(p.astype(MXU), v_ref[...]).astype(BF16)

    qspec = pl.BlockSpec((tq, XD), lambda b, h, i: (b * nq + i, h))
    return pl.pallas_call(
        body, grid=(bl, XH, nq),
        in_specs=[qspec, pl.BlockSpec((m, XD), lambda b, h, i: (b, h)),
                  pl.BlockSpec((m, XD), lambda b, h, i: (b, XH + h))],
        out_specs=qspec, out_shape=jax.ShapeDtypeStruct((bl * s, D), BF16),
        compiler_params=_cp("parallel", "parallel", "parallel"), name=name)(q, kv, kv)


def _attn_bwd(q, kv, do, bl, s, *, name):
    m = kv.shape[0] // bl
    tq = _tile(s, 512)
    nq = s // tq
    scale = 1.0 / math.sqrt(XD)

    def body(q_ref, k_ref, v_ref, do_ref, dq_ref, dk_ref, dv_ref):
        @pl.when(pl.program_id(2) == 0)
        def _():
            dk_ref[...] = jnp.zeros_like(dk_ref)
            dv_ref[...] = jnp.zeros_like(dv_ref)

        qv, kvv, vv, dov = q_ref[...], k_ref[...], v_ref[...], do_ref[...]
        p = _softmax_rows(_nt(qv, kvv) * scale)
        dp = _nt(dov, vv)
        ds = (p * (dp - jnp.sum(dp * p, axis=-1, keepdims=True)) * scale).astype(MXU)
        dq_ref[...] = _nn(ds, kvv).astype(BF16)
        dk_ref[...] += _tn(ds, qv)
        dv_ref[...] += _tn(p.astype(MXU), dov)

    qspec = pl.BlockSpec((tq, XD), lambda b, h, i: (b * nq + i, h))
    kspec = pl.BlockSpec((m, XD), lambda b, h, i: (b, h))
    return pl.pallas_call(
        body, grid=(bl, XH, nq),
        in_specs=[qspec, kspec, pl.BlockSpec((m, XD), lambda b, h, i: (b, XH + h)), qspec],
        out_specs=[qspec, kspec, kspec],
        out_shape=[jax.ShapeDtypeStruct((bl * s, D), BF16), jax.ShapeDtypeStruct((bl * m, D), F32),
                   jax.ShapeDtypeStruct((bl * m, D), F32)],
        compiler_params=_cp("parallel", "parallel", "arbitrary"), name=name)(q, kv, kv, do)


def _row_tile(r, c, max_elems=512 * 1024, align=16):
    best = None
    for t in range(align, r + 1, align):
        if r % t == 0 and t * c <= max_elems:
            best = t
    return best if best is not None else r


def _addn(a, others, *, name, also_bf16=False):
    r, c = a.shape
    tr = _row_tile(r, c)
    n = len(others)

    def body(*refs):
        acc = refs[0][...].astype(F32)
        for o_ref in refs[1:1 + n]:
            acc = acc + o_ref[...].astype(F32)
        refs[1 + n][...] = acc
        if also_bf16:
            refs[2 + n][...] = acc.astype(BF16)

    spec = pl.BlockSpec((tr, c), lambda i: (i, 0))
    shapes = [jax.ShapeDtypeStruct((r, c), F32)] + ([jax.ShapeDtypeStruct((r, c), BF16)] if also_bf16 else [])
    out = pl.pallas_call(
        body, grid=(r // tr,), in_specs=[spec] * (1 + n), out_specs=[spec] * len(shapes), out_shape=shapes,
        compiler_params=_cp("parallel"), name=name)(a, *others)
    return out if also_bf16 else out[0]


def _sum_leading(a, *, name):
    n, r, c = a.shape

    def body(a_ref, o_ref):
        acc = a_ref[0]
        for i in range(1, n):
            acc = acc + a_ref[i]
        o_ref[...] = acc

    return pl.pallas_call(body, out_shape=jax.ShapeDtypeStruct((r, c), F32), name=name)(a)


def _adamw_math(wv, gv, mv, vv):
    m2 = ADAM_B1 * mv + (1.0 - ADAM_B1) * gv
    v2 = ADAM_B2 * vv + (1.0 - ADAM_B2) * (gv * gv)
    m_hat = m2 / (1.0 - ADAM_B1 ** ADAM_STEP)
    v_hat = v2 / (1.0 - ADAM_B2 ** ADAM_STEP)
    return -ADAM_LR * (m_hat / (jnp.sqrt(v_hat) + ADAM_EPS) + ADAM_WD * wv), m2, v2


def _adamw(w, g, m, v, *, name):
    r, c = w.shape
    tr = _row_tile(r, c, align=8)

    def body(w_ref, g_ref, m_ref, v_ref, d_ref, mo_ref, vo_ref):
        d_ref[...], mo_ref[...], vo_ref[...] = _adamw_math(w_ref[...], g_ref[...], m_ref[...], v_ref[...])

    spec = pl.BlockSpec((tr, c), lambda i: (i, 0))
    shp = jax.ShapeDtypeStruct((r, c), F32)
    return pl.pallas_call(
        body, grid=(r // tr,), in_specs=[spec] * 4, out_specs=[spec] * 3, out_shape=[shp] * 3,
        compiler_params=_cp("parallel"), name=name)(w, g, m, v)


def _adamw_halves(w, g_mine, g_other, m, v, c, *, name):
    r, cols = w.shape
    h = r // 2
    tr = _row_tile(h, cols, align=8)
    nh = h // tr

    def body(c_ref, w_ref, gm_ref, go_ref, m_ref, v_ref, g_ref, d_ref, mo_ref, vo_ref):
        gv = jnp.where(pl.program_id(0) // nh == c_ref[0], gm_ref[...], go_ref[...])
        g_ref[...] = gv
        d_ref[...], mo_ref[...], vo_ref[...] = _adamw_math(w_ref[...], gv, m_ref[...], v_ref[...])

    full = pl.BlockSpec((tr, cols), lambda i, c_: (i, 0))
    half = pl.BlockSpec((tr, cols), lambda i, c_: (i % nh, 0))
    shp = jax.ShapeDtypeStruct((r, cols), F32)
    return pl.pallas_call(
        body,
        grid_spec=pltpu.PrefetchScalarGridSpec(num_scalar_prefetch=1, grid=(2 * nh,),
                                               in_specs=[full, half, half, full, full], out_specs=[full] * 4),
        out_shape=[shp] * 4, compiler_params=_cp("parallel"), name=name,
    )(jnp.reshape(c, (1,)).astype(jnp.int32), w, g_mine, g_other, m, v)


def _flip(i, d):
    return 1 - i if d else i


def _comm(name, ins, out_shapes, n_remote, n_local, plan, aliases=None):
    n_in, n_out = len(ins), len(out_shapes)

    def body(*refs):
        in_refs, out_refs = refs[:n_in], refs[n_in:n_in + n_out]
        send_sems, recv_sems = refs[n_in + n_out], refs[n_in + n_out + 1]
        x, y, c = lax.axis_index("x"), lax.axis_index("y"), lax.axis_index("c")
        remote, local = plan(in_refs, out_refs, x, y, c)
        assert len(remote) == n_remote and len(local) == n_local
        copies = []
        if n_local:
            loc_sems = refs[n_in + n_out + 2]
            copies += [pltpu.make_async_copy(s_, d_, loc_sems.at[i]) for i, (s_, d_) in enumerate(local)]
        copies += [pltpu.make_async_remote_copy(src_ref=s_, dst_ref=d_, send_sem=send_sems.at[i],
                                                recv_sem=recv_sems.at[i], device_id=dev, device_id_type=MESH)
                   for i, (s_, d_, dev) in enumerate(remote)]
        for cp in copies:
            cp.start()
        for cp in copies:
            cp.wait()

    hbm = pl.BlockSpec(memory_space=pl.ANY)
    scratch = [pltpu.SemaphoreType.DMA((n_remote,)), pltpu.SemaphoreType.DMA((n_remote,))]
    if n_local:
        scratch.append(pltpu.SemaphoreType.DMA((n_local,)))
    return pl.pallas_call(
        body, in_specs=[hbm] * n_in, out_specs=[hbm] * n_out, out_shape=out_shapes, scratch_shapes=scratch,
        input_output_aliases=aliases or {}, compiler_params=pltpu.CompilerParams(has_side_effects=True),
        name=name)(*ins)


HBM_SPEC = pl.BlockSpec(memory_space=pltpu.HBM)
SEM_SPEC = pl.BlockSpec(memory_space=pltpu.SEMAPHORE)
DATAFLOW = pltpu.SideEffectType.DATAFLOW_SIDE_EFFECTING


def _remote_copies(plan, srcs, lands, send_sems, recv_sems, n_copies):
    x, y, c = lax.axis_index("x"), lax.axis_index("y"), lax.axis_index("c")
    copies = plan(srcs, lands, x, y, c)
    assert len(copies) == n_copies
    return [pltpu.make_async_remote_copy(src_ref=s_, dst_ref=d_, send_sem=send_sems.at[i], recv_sem=recv_sems.at[i],
                                         device_id=dev, device_id_type=MESH) for i, (s_, d_, dev) in enumerate(copies)]


def _split_start(name, srcs, lands, n_copies, plan, after=None):
    ns, nb = len(srcs), len(srcs) + len(lands)
    n_after = 0 if after is None else 1
    n_in = nb + n_after

    def body(*refs):
        for cp in _remote_copies(plan, refs[:ns], refs[ns:nb], refs[n_in], refs[n_in + 1], n_copies):
            cp.start()
        refs[-1][...] = jnp.zeros_like(refs[-1])

    arrays = [pltpu.with_memory_space_constraint(a_, pltpu.HBM) for a_ in list(srcs) + list(lands)]
    out = pl.pallas_call(
        body, name=name,
        out_shape=(pltpu.SemaphoreType.DMA((n_copies,)), pltpu.SemaphoreType.DMA((n_copies,)),
                   *[pltpu.HBM(a_.shape, a_.dtype) for a_ in arrays], jax.ShapeDtypeStruct((8, 128), F32)),
        in_specs=[HBM_SPEC] * nb + [pl.BlockSpec(memory_space=pl.ANY)] * n_after,
        out_specs=(SEM_SPEC, SEM_SPEC, *[HBM_SPEC] * nb, pl.BlockSpec(memory_space=pltpu.VMEM)),
        input_output_aliases={i: 2 + i for i in range(nb)},
        compiler_params=pltpu.CompilerParams(has_side_effects=DATAFLOW))(*arrays, *([after] * n_after))
    return (out[0], out[1], list(out[2:2 + nb])), out[-1]


def _split_wait(name, handle, ns, n_copies, plan, after):
    send_sems, recv_sems, bufs = handle
    nb = len(bufs)

    def body(*refs):
        for cp in _remote_copies(plan, refs[:ns], refs[ns:nb], refs[nb], refs[nb + 1], n_copies):
            cp.wait_send()
            cp.wait_recv()

    out = pl.pallas_call(
        body, name=name, out_shape=[pltpu.HBM(b_.shape, b_.dtype) for b_ in bufs],
        in_specs=[HBM_SPEC] * nb + [SEM_SPEC, SEM_SPEC, pl.BlockSpec(memory_space=pl.ANY)],
        out_specs=[HBM_SPEC] * nb, input_output_aliases={i: i for i in range(nb)},
        compiler_params=pltpu.CompilerParams(has_side_effects=DATAFLOW))(*bufs, send_sems, recv_sems, after)
    return list(out[ns:])


def _gather_start(shards, tag, after=None):
    n = len(shards)
    lands = [lax.empty((4,) + s.shape, s.dtype) for s in shards]

    def plan(srcs, dsts, x, y, c):
        k = 2 * x + y
        copies = []
        for w_ref, o_ref in zip(srcs, dsts):
            h = w_ref.shape[0] // 2
            rows = pl.ds(c * h, h)
            copies += [(w_ref.at[rows], o_ref.at[k, rows], (_flip(x, dx), _flip(y, dy), c)) for dx, dy in CHIP_FLIPS]
        return copies

    handle, token = _split_start(f"gather_{tag}_start", shards, lands, 3 * n, plan, after)
    return (handle, plan, n), token


def _gather_wait(started, after, tag):
    handle, plan, n = started
    return _split_wait(f"gather_{tag}_wait", handle, n, 3 * n, plan, after)


def _gather_d2d(lands, before, tag):
    n = len(lands)

    def plan_d2d(in_refs, out_refs, x, y, c):
        remote = []
        for o_ref in out_refs:
            h = o_ref.shape[1] // 2
            for dx, dy in CHIP_FLIPS:
                half = o_ref.at[2 * _flip(x, dx) + _flip(y, dy), pl.ds(c * h, h)]
                remote.append((half, half, (x, y, 1 - c)))
        return remote, []

    return _comm(f"gather_{tag}_d2d", list(lands) + list(before),
                 [jax.ShapeDtypeStruct(l_.shape, l_.dtype) for l_ in lands], 3 * n, 0, plan_d2d,
                 aliases={i: i for i in range(n)})


def _pair_plan(in_refs, out_refs, x, y, c):
    return [(i_, o_, (x, y, 1 - c)) for i_, o_ in zip(in_refs, out_refs)], []


def _rs_start(grads, tag):
    n = len(grads)
    c = lax.axis_index("c")
    halves = [g.shape[1] // 2 for g in grads]
    mine = [lax.dynamic_slice_in_dim(g, c * h, h, axis=1) for g, h in zip(grads, halves)]
    send_a = [lax.dynamic_slice_in_dim(g, (1 - c) * h, h, axis=1).astype(BF16) for g, h in zip(grads, halves)]
    recv_a = _comm(f"rs_pair_{tag}", send_a, [jax.ShapeDtypeStruct(s.shape, BF16) for s in send_a], n, 0, _pair_plan)
    pair, pair_b = [], []
    for i, (mi, ra) in enumerate(zip(mine, recv_a)):
        four, h, cols = mi.shape
        p32, p16 = _addn(mi.reshape(four * h, cols), [ra.reshape(four * h, cols)], name=f"rs_pair_sum_{tag}_{i}",
                         also_bf16=True)
        pair.append(p32.reshape(four, h, cols))
        pair_b.append(p16.reshape(four, h, cols))

    def plan(srcs, dsts, x, y, c_):
        copies = []
        for i_, o_ in zip(srcs, dsts):
            for j, (dx, dy) in enumerate(CHIP_FLIPS):
                fx, fy = _flip(x, dx), _flip(y, dy)
                copies.append((i_.at[2 * fx + fy], o_.at[j], (fx, fy, c_)))
        return copies

    lands = [lax.empty((3,) + p.shape[1:], BF16) for p in pair_b]
    handle, token = _split_start(f"rs_chips_{tag}_start", pair_b, lands, 3 * n, plan)
    return (handle, plan, n, pair), token


def _rs_finish(started, after, tag):
    handle, plan, n, pair = started
    recv_b = _split_wait(f"rs_chips_{tag}_wait", handle, n, 3 * n, plan, after)
    k = 2 * lax.axis_index("x") + lax.axis_index("y")
    tot = [_addn(lax.dynamic_index_in_dim(p, k, 0, keepdims=False), [rb[0], rb[1], rb[2]],
                 name=f"rs_chip_sum_{tag}_{i}") for i, (p, rb) in enumerate(zip(pair, recv_b))]
    other = _comm(f"rs_halves_{tag}", tot, [jax.ShapeDtypeStruct(t.shape, F32) for t in tot], n, 0, _pair_plan)
    return tot, other


def _gather_all(vec, *, name):
    out = jax.ShapeDtypeStruct((8,) + vec.shape, vec.dtype)

    def plan(in_refs, out_refs, x, y, c):
        me = 4 * x + 2 * y + c
        remote = [(in_refs[0], out_refs[0].at[me], (_flip(x, dx), _flip(y, dy), _flip(c, dc)))
                  for dx in (0, 1) for dy in (0, 1) for dc in (0, 1) if (dx, dy, dc) != (0, 0, 0)]
        return remote, [(in_refs[0], out_refs[0].at[me])]

    return _comm(name, [vec], [out], 7, 1, plan)[0]


def _pack(parts):
    flat = [p.reshape(-1).astype(F32) for p in parts]
    total = sum(f.shape[0] for f in flat)
    n = -(-total // 1024) * 128
    vec = jnp.concatenate(flat + [jnp.zeros((8 * n - total,), F32)]).reshape(8, n)
    offs, o = [], 0
    for f in flat:
        offs.append((o, f.shape[0]))
        o += f.shape[0]
    return vec, offs


def _unpack(vec, offs, shapes):
    flat = vec.reshape(-1)
    return [flat[o:o + n].reshape(s) for (o, n), s in zip(offs, shapes)]


BIG = (("ffn1_w_gate_up", "col"), ("ffn1_w_down", "row"), ("w_in", "col"), ("w_out_a", "row"), ("w_out_ssm", "row"),
       ("w_mix_out", "row"), ("w_q", "row"), ("w_kv", "col"), ("w_o_x", "row"), ("ffn2_w_gate_up", "col"),
       ("ffn2_w_down", "row"))
SMALL = ("ffn1_norm", "mix_norm", "conv_a_w", "ssm_conv_w", "ssm_conv_b", "ssm_dt_bias", "ssm_a_log", "ssm_d",
         "ssm_norm", "xattn_norm", "mem_norm", "ffn2_norm", "final_norm")
WEIGHTS = ("ffn1_norm", "ffn1_w_gate_up", "ffn1_w_down", "mix_norm", "w_in", "conv_a_w", "w_out_a", "ssm_conv_w",
           "ssm_conv_b", "ssm_dt_bias", "ssm_a_log", "ssm_d", "ssm_norm", "w_out_ssm", "w_mix_out", "xattn_norm",
           "mem_norm", "w_q", "w_kv", "w_o_x", "ffn2_norm", "ffn2_w_gate_up", "ffn2_w_down", "final_norm")


GATHER_GROUPS = (("a", ("ffn1_w_gate_up",)), ("b", ("ffn1_w_down", "w_in")),
                 ("c", ("w_out_a", "w_out_ssm", "w_mix_out", "w_q", "w_kv", "w_o_x", "ffn2_w_gate_up", "ffn2_w_down")))


def _full_weight(land, own, kind, k):
    parts = [jnp.where(k == kk, own, land[kk]) for kk in range(4)]
    return jnp.concatenate(parts, axis=0 if kind == "row" else 1)


class _GatheredWeights:
    def __init__(self, shards, k):
        self.shards, self.k = shards, k
        self.full = {}
        self.n_done = 0
        self.started, self.after = self._start(0, None)

    def _start(self, gi, after):
        tag, names = GATHER_GROUPS[gi]
        return _gather_start([self.shards[n] for n in names], tag, after)

    def mark(self, value):
        self.after = value

    def __getitem__(self, name):
        if name not in self.full:
            tag, names = GATHER_GROUPS[self.n_done]
            assert name in names, (name, tag)
            lands = _gather_wait(self.started, self.after, tag)
            before = []
            if self.n_done + 1 < len(GATHER_GROUPS):
                self.started, token = self._start(self.n_done + 1, lands[0])
                before = [token]
            lands = _gather_d2d(lands, before, tag)
            for n, land in zip(names, lands):
                w = _full_weight(land, self.shards[n], dict(BIG)[n], self.k)
                self.full[n] = _pad_w_in(w) if n == "w_in" else w
            self.n_done += 1
        return self.full[name]


def _shard_major(dw, kind):
    if isinstance(dw, tuple):
        return jnp.concatenate([_shard_major(p, "col2") for p in dw], axis=0)
    if kind == "col2":
        return jnp.transpose(dw.reshape(dw.shape[0], 2, dw.shape[1] // 2), (1, 0, 2))
    if kind == "row":
        return dw.reshape(4, dw.shape[0] // 4, dw.shape[1])
    return jnp.transpose(dw.reshape(dw.shape[0], 4, dw.shape[1] // 4), (1, 0, 2))


def _pad_rows8(w):
    return jnp.concatenate([w, jnp.zeros((8 - w.shape[0], w.shape[1]), w.dtype)], axis=0)


def _group_lanes(v):
    r = v.shape[0]
    return jnp.pad(v.reshape(r, NG, NH // NG), ((0, 0), (0, 0), (0, 128 - NH // NG))).reshape(r, NG * 128)


def _ungroup_lanes(v):
    r = v.shape[0]
    return v.reshape(r, NG, 128)[:, :, :NH // NG].reshape(r, NH)


def _local_step(wfull, small, x, mem, target, token=0.0, on_grads=None):
    bl, s, _ = x.shape
    T = bl * s
    x2, t2 = x.reshape(T, D), target.reshape(T, D)
    mem2 = mem.reshape(-1, D)
    g = {}
    tok = [token]
    mark = getattr(wfull, "mark", lambda value: None)

    def gain(name):
        return small[name].reshape(1, -1) + tok[0]

    def emit(tag, names):
        if on_grads is not None:
            tok[0] = tok[0] + on_grads(tag, {n: g[n] for n in names})

    def ffn_fwd(h, norm, wgu, wd, tag):
        n = _norm_fwd(h, gain(norm), name=f"{tag}_norm")
        gate, up, a = _gate_up_fwd(n, wfull[wgu], name=f"{tag}_gate_up")
        mark(a)
        out = _mm(a, wfull[wd], "nn", tk=DFF, scale=FFN_RES, residual=h, name=f"{tag}_down")
        return out, (n, gate, up, a)

    def ffn_bwd(dh, h, norm, wgu, wd, saved, tag):
        n, gate, up, a = saved
        dgate, dup = _act_bwd(dh, wfull[wd], gate, up, FFN_RES, name=f"{tag}_d_act")
        g[wd] = _mm(a, dh, "tn", tm=1408, scale=FFN_RES, name=f"{tag}_d_w_down")
        g[wgu] = (_mm(n, dgate, "tn", tn=1408, name=f"{tag}_d_w_gate"), _mm(n, dup, "tn", tn=1408, name=f"{tag}_d_w_up"))
        emit(tag, (wgu, wd))
        dn = _mm(dgate, wfull[wgu], "nt", a2=dup, tk=1408, name=f"{tag}_d_norm_out")
        dh_in, g[norm] = _norm_bwd(h, gain(norm), dn, dh, name=f"{tag}_d_norm")
        return dh_in

    h1, ffn1_saved = ffn_fwd(x2, "ffn1_norm", "ffn1_w_gate_up", "ffn1_w_down", "ffn1")
    mark(h1)
    u = _norm_fwd(h1, gain("mix_norm"), name="mix_norm")
    pp = _mm(u, wfull["w_in"], "nn", tn=1152, name="in_proj")
    wa8 = _pad_rows8(small["conv_a_w"])
    ws8 = _pad_rows8(small["ssm_conv_w"])
    conv_b = gain("ssm_conv_b")
    bias128 = jnp.pad(gain("ssm_dt_bias"), ((0, 0), (0, 128 - NH)))
    ya_pre = _conv_a_fwd(pp, wa8, bl, s, name="conv_a")
    xc = _conv_ssm_fwd(pp, ws8, conv_b, bl, s, name="conv_ssm")
    mark(xc)
    dt = _dt_fwd(pp, bias128, name="dt")
    dtg = _group_lanes(dt[:, :NH])
    dtt = dt[:, :NH].T.reshape(NG, NH // NG, T)
    alog = gain("ssm_a_log")
    arow, acol = _group_lanes(alog), alog.reshape(NG, NH // NG, 1)
    dexp = jnp.repeat(gain("ssm_d"), HD, axis=1)
    ng = gain("ssm_norm")
    y, yn, prev = _ssd_fwd(xc, pp, dtg, dtt, arow, acol, dexp, ng, bl, s, name="ssd")
    ya = _mm(ya_pre, wfull["w_out_a"], "nn", tn=1024, name="out_a")
    yb = _mm(yn, wfull["w_out_ssm"], "nn", tn=1024, tk=DI, name="out_ssm")
    merged = _merge_fwd(pp, ya, yb, name="merge")
    h2 = _mm(merged, wfull["w_mix_out"], "nn", tn=1024, residual=h1, name="mix_out")
    un = _norm_fwd(h2, gain("xattn_norm"), name="xattn_norm")
    q = _mm(un, wfull["w_q"], "nn", tn=1024, out_dtype=BF16, name="q_proj")
    mn = _norm_fwd(mem2, gain("mem_norm"), name="mem_norm")
    kv = _mm(mn, wfull["w_kv"], "nn", tn=1024, out_dtype=BF16, name="kv_proj")
    o = _attn_fwd(q, kv, bl, s, name="attn")
    h3 = _mm(o, wfull["w_o_x"], "nn", tn=1024, residual=h2, name="attn_out")
    h4, ffn2_saved = ffn_fwd(h3, "ffn2_norm", "ffn2_w_gate_up", "ffn2_w_down", "ffn2")
    sq_err, dh4, dgf = _final_loss(h4, gain("final_norm"), t2, name="final_loss")
    g["final_norm"] = dgf

    dh3 = ffn_bwd(dh4, h3, "ffn2_norm", "ffn2_w_gate_up", "ffn2_w_down", ffn2_saved, "ffn2")
    do = _mm(dh3, wfull["w_o_x"], "nt", tn=1024, out_dtype=BF16, name="d_attn_o")
    g["w_o_x"] = _mm(o, dh3, "tn",name="d_w_o_x")
    dq, dk, dv = _attn_bwd(q, kv, do, bl, s, name="d_attn")
    dun = _mm(dq, wfull["w_q"], "nt", tn=1024, name="d_xattn_norm_out")
    g["w_q"] = _mm(un, dq, "tn",name="d_w_q")
    dkv = jnp.concatenate([dk, dv], axis=1)
    dmn = _mm(dkv, wfull["w_kv"], "nt", tn=1024, tk=2 * D, name="d_mem_norm_out")
    g["w_kv"] = _mm(mn, dkv, "tn",name="d_w_kv")
    emit("attn", ("w_q", "w_kv", "w_o_x"))
    _, g["mem_norm"] = _norm_bwd(mem2, gain("mem_norm"), dmn, None, name="d_mem_norm")
    dh2, g["xattn_norm"] = _norm_bwd(h2, gain("xattn_norm"), dun, dh3, name="d_xattn_norm")
    dmerged = _mm(dh2, wfull["w_mix_out"], "nt", tn=1024, name="d_merged")
    g["w_mix_out"] = _mm(merged, dh2, "tn",name="d_w_mix_out")
    dya, dyb, dga, dgb = _merge_bwd(pp, ya, yb, dmerged, name="d_merge")
    dya_pre = _mm(dya, wfull["w_out_a"], "nt", tn=1024, name="d_conv_a_out")
    g["w_out_a"] = _mm(ya_pre, dya, "tn",name="d_w_out_a")
    dyn = _mm(dyb, wfull["w_out_ssm"], "nt", tn=DI, name="d_ssd_out")
    g["w_out_ssm"] = _mm(yn, dyb, "tn",name="d_w_out_ssm")
    d_ab, d_ac, d_av, dwa8 = _conv_a_bwd(pp, wa8, dya_pre, bl, s, name="d_conv_a")
    g["conv_a_w"] = dwa8[:3]
    dz, dxs, dbm, dcm, ddtg, g["ssm_norm"], ddexp, dalg = _ssd_bwd(
        dyn, y, xc, pp, dtg, dtt, arow, acol, dexp, ng, prev, bl, s, name="d_ssd")
    g["ssm_d"] = ddexp.reshape(NH, HD).sum(axis=1).reshape(1, NH)
    g["ssm_a_log"] = _ungroup_lanes(dalg)
    conv_parts = [_conv_ssm_bwd(pp, ws8, conv_b, dpart, off, bl, s, name=f"d_conv_ssm_{tag}")
                  for dpart, off, tag in ((dxs, 0, "x"), (dbm, DI, "b"), (dcm, DI + NG * NS, "c"))]
    g["ssm_conv_w"] = jnp.concatenate([p[1] for p in conv_parts], axis=1)[:4]
    g["ssm_conv_b"] = jnp.concatenate([p[2] for p in conv_parts], axis=1)
    ddt = jnp.pad(_ungroup_lanes(ddtg), ((0, 0), (0, 128 - NH)))
    ddt_raw, dbias = _dt_bwd(pp, bias128, ddt, name="d_dt")
    g["ssm_dt_bias"] = dbias[:, :NH]
    dpp = jnp.concatenate([d_ab, d_ac, d_av, dz] + [p[0] for p in conv_parts] + [dga, dgb, ddt_raw], axis=1)
    g["w_in"] = _mm(u, dpp, "tn", tn=1152, name="d_w_in")
    emit("mix", ("w_in", "w_out_a", "w_out_ssm", "w_mix_out"))
    du = _mm(dpp, wfull["w_in"], "nt", tk=3456, name="d_mix_norm_out")
    dh1, g["mix_norm"] = _norm_bwd(h1, gain("mix_norm"), du, dh2, name="d_mix_norm")
    dx = ffn_bwd(dh1, x2, "ffn1_norm", "ffn1_w_gate_up", "ffn1_w_down", ffn1_saved, "ffn1")
    return sq_err, dx, g


def _pad_w_in(w):
    return jnp.concatenate([w[:, :O_GA], w[:, O_GA + NH:], w[:, O_GA:O_GA + NH],
                            jnp.zeros((w.shape[0], NPP - NIN), w.dtype)], axis=1)


def _unpad_w_in(w):
    return jnp.concatenate([w[:, :O_GA], w[:, O_DT:O_DT + NH], w[:, O_GA:O_DT]], axis=1)


def kernel(x, mem, ffn1_norm, ffn1_w_gate_up, ffn1_w_down, mix_norm, w_in, conv_a_w, w_out_a, ssm_conv_w, ssm_conv_b, ssm_dt_bias, ssm_a_log, ssm_d, ssm_norm, w_out_ssm, w_mix_out, xattn_norm, mem_norm, w_q, w_kv, w_o_x, ffn2_norm, ffn2_w_gate_up, ffn2_w_down, final_norm, loss_target, m_ffn1_norm, m_ffn1_w_gate_up, m_ffn1_w_down, m_mix_norm, m_w_in, m_conv_a_w, m_w_out_a, m_ssm_conv_w, m_ssm_conv_b, m_ssm_dt_bias, m_ssm_a_log, m_ssm_d, m_ssm_norm, m_w_out_ssm, m_w_mix_out, m_xattn_norm, m_mem_norm, m_w_q, m_w_kv, m_w_o_x, m_ffn2_norm, m_ffn2_w_gate_up, m_ffn2_w_down, m_final_norm, v_ffn1_norm, v_ffn1_w_gate_up, v_ffn1_w_down, v_mix_norm, v_w_in, v_conv_a_w, v_w_out_a, v_ssm_conv_w, v_ssm_conv_b, v_ssm_dt_bias, v_ssm_a_log, v_ssm_d, v_ssm_norm, v_w_out_ssm, v_w_mix_out, v_xattn_norm, v_mem_norm, v_w_q, v_w_kv, v_w_o_x, v_ffn2_norm, v_ffn2_w_gate_up, v_ffn2_w_down, v_final_norm):
    a = dict(locals())
    xi, yi = lax.axis_index("x"), lax.axis_index("y")
    k = 2 * xi + yi

    wfull = _GatheredWeights({n: a[n][0].astype(BF16) for n, _ in BIG}, k)
    conv_vec, conv_offs = _pack([a["conv_a_w"], a["ssm_conv_w"]])
    conv_all = _gather_all(conv_vec, name="gather_conv_w")
    conv_sh = [_unpack(conv_all[2 * kk], conv_offs, [a["conv_a_w"].shape[1:], a["ssm_conv_w"].shape[1:]])
               for kk in range(4)]
    small = {n: a[n] for n in SMALL}
    small["conv_a_w"] = jnp.concatenate([cs[0] for cs in conv_sh], axis=1)
    small["ssm_conv_w"] = jnp.concatenate([cs[1] for cs in conv_sh], axis=1)

    rs_started = []

    def on_grads(tag, grads):
        names = [n for n, _ in BIG if n in grads]
        shard_major = [_shard_major(_unpad_w_in(grads[n]) if n == "w_in" else grads[n], dict(BIG)[n]) for n in names]
        st, tk = _rs_start(shard_major, tag)
        rs_started.append((tag, names, st))
        return tk[0, 0]

    sq_err, dx, g = _local_step(wfull, small, x, mem, loss_target, wfull.after[0, 0], on_grads)
    loss = lax.psum(0.5 / D * jnp.sum(sq_err), ("x", "y", "c"))

    ci = lax.axis_index("c")
    out = {}
    for tag, names, st in rs_started:
        g_mine, g_other = _rs_finish(st, dx, tag)
        for n, gm, go in zip(names, g_mine, g_other):
            res = _adamw_halves(a[n][0], gm, go, a["m_" + n][0], a["v_" + n][0], ci, name=f"adamw_{n}")
            out[n] = tuple(t.reshape(a[n].shape) for t in res)

    full_shapes = [g[n].shape for n in SMALL]
    gvec, goffs = _pack([g[n] for n in SMALL])
    gsum = _sum_leading(_gather_all(gvec, name="gather_small_grads"), name="sum_small_grads")
    gsmall = dict(zip(SMALL, _unpack(gsum, goffs, full_shapes)))
    for n in ("conv_a_w", "ssm_conv_w"):
        width = a[n].shape[2]
        gsmall[n] = lax.dynamic_slice_in_dim(gsmall[n], k * width, width, axis=1)
    local_shapes = [a[n].shape for n in SMALL]
    packs = [_pack([t[n] for n in SMALL]) for t in
             ({n: a[n] for n in SMALL}, gsmall, {n: a["m_" + n] for n in SMALL}, {n: a["v_" + n] for n in SMALL})]
    offs = packs[0][1]
    res = _adamw(*[p[0] for p in packs], name="adamw_small")
    unp = [_unpack(r, offs, local_shapes) for r in res]
    for i, n in enumerate(SMALL):
        out[n] = (gsmall[n].reshape(a[n].shape), unp[0][i], unp[1][i], unp[2][i])

    grad_x = dx.reshape(x.shape)
    return (loss, grad_x, *[out[n][0] for n in WEIGHTS], *[out[n][1] for n in WEIGHTS],
            *[out[n][2] for n in WEIGHTS], *[out[n][3] for n in WEIGHTS])
```

```python
import functools
import math

import jax
import jax.numpy as jnp
from jax import lax
from jax.experimental import pallas as pl
from jax.experimental.pallas import tpu as pltpu

F32 = jnp.float32
BF16 = jnp.bfloat16
MXU = jnp.bfloat16
HI = lax.Precision.HIGHEST

D = 1024
DFF = 2816
DI = 2048
NH, HD, NG, NS, CH = 32, 64, 4, 128, 128
GW = DI // NG
XH, XD = 4, 256
EPS = 1e-6
NEG = -1e30
O_AB, O_AC, O_AV, O_Z, O_XBC, O_GA, O_GB, O_DT, NPP = 0, 1024, 2048, 3072, 5120, 8192, 9216, 10240, 10368
NIN = 10272
FFN_RES = 0.5
ADAM_LR, ADAM_B1, ADAM_B2, ADAM_EPS, ADAM_WD, ADAM_STEP = 0.001, 0.9, 0.999, 1e-08, 0.01, 10
VMEM_LIMIT = 56 * 1024 * 1024
EPI_COLS = 256
MESH = pl.DeviceIdType.MESH
CHIP_FLIPS = ((1, 0), (0, 1), (1, 1))


def _cp(*sem):
    return pltpu.CompilerParams(dimension_semantics=sem, vmem_limit_bytes=VMEM_LIMIT)


def _tile(n, pref, align=128):
    if n <= pref:
        return n
    t = (pref // align) * align
    while t >= align:
        if n % t == 0:
            return t
        t -= align
    raise ValueError((n, pref))


def _dot(a, b, dims, prec=None):
    return lax.dot_general(a, b, (dims, ((), ())), preferred_element_type=F32, precision=prec)


def _nn(a, b, prec=None):
    return _dot(a, b, ((1,), (0,)), prec)


def _nt(a, b):
    return _dot(a, b, ((1,), (1,)))


def _tn(a, b):
    return _dot(a, b, ((0,), (0,)))


def _sig(x):
    return jax.nn.sigmoid(x)


def _mm(a, b, mode, *, name, tm=1024, tn=1024, tk=1024, out_dtype=F32, scale=None, residual=None, a2=None):
    if mode == "nn":
        (M, K), (K2, N) = a.shape, b.shape
    elif mode == "nt":
        (M, K), (N, K2) = a.shape, b.shape
        if a2 is not None:
            assert a2.shape == a.shape
            K2 = K2 // 2
    else:
        (K, M), (K2, N) = a.shape, b.shape
    assert K == K2, (name, a.shape, b.shape)
    tm, tn, tk = _tile(M, tm), _tile(N, tn), _tile(K, tk)
    nk = K // tk
    if mode == "nn":
        a_spec = pl.BlockSpec((tm, tk), lambda i, j, k: (i, k))
        b_spec = pl.BlockSpec((tk, tn), lambda i, j, k: (k, j))
        dims = ((1,), (0,))
    elif mode == "nt":
        a_spec = pl.BlockSpec((tm, tk), lambda i, j, k: (i, k))
        b_spec = pl.BlockSpec((tn, tk), lambda i, j, k: (j, k))
        dims = ((1,), (1,))
    else:
        a_spec = pl.BlockSpec((tk, tm), lambda i, j, k: (k, i))
        b_spec = pl.BlockSpec((tk, tn), lambda i, j, k: (k, j))
        dims = ((0,), (0,))
    o_spec = pl.BlockSpec((tm, tn), lambda i, j, k: (i, j))
    has_res = residual is not None

    def finish(acc, r_ref, o_ref):
        if scale is not None:
            acc = acc * scale
        if has_res:
            acc = acc + r_ref[...]
        o_ref[...] = acc.astype(out_dtype)

    dual = a2 is not None
    n_in = 2 + 2 * dual + has_res

    def body(*refs):
        a_ref, b_ref = refs[0], refs[1]
        r_ref = refs[n_in - 1] if has_res else None
        o_ref = refs[n_in]
        part = _dot(a_ref[...].astype(MXU), b_ref[...].astype(MXU), dims)
        if dual:
            part = part + _dot(refs[2][...].astype(MXU), refs[3][...].astype(MXU), dims)
        if nk == 1:
            finish(part, r_ref, o_ref)
            return
        acc_ref = refs[-1]
        k = pl.program_id(2)

        @pl.when(k == 0)
        def _():
            acc_ref[...] = part

        @pl.when(k > 0)
        def _():
            acc_ref[...] += part

        @pl.when(k == nk - 1)
        def _():
            finish(acc_ref[...], r_ref, o_ref)

    ins, in_specs = [a, b], [a_spec, b_spec]
    if dual:
        ins += [a2, b]
        in_specs += [a_spec, pl.BlockSpec((tn, tk), lambda i, j, k: (j, k + nk))]
    if has_res:
        ins.append(residual)
        in_specs.append(o_spec)
    return pl.pallas_call(
        body, grid=(M // tm, N // tn, nk), in_specs=in_specs, out_specs=o_spec,
        out_shape=jax.ShapeDtypeStruct((M, N), out_dtype),
        scratch_shapes=[pltpu.VMEM((tm, tn), F32)] if nk > 1 else [],
        compiler_params=_cp("parallel", "parallel", "arbitrary"), name=name)(*ins)


def _norm_fwd(x, g, *, name):
    T, d = x.shape
    tr = _tile(T, 512, 8)

    def body(x_ref, g_ref, o_ref):
        xv = x_ref[...]
        r = lax.rsqrt(jnp.mean(xv * xv, axis=-1, keepdims=True) + EPS)
        o_ref[...] = (xv * r * g_ref[...]).astype(BF16)

    return pl.pallas_call(
        body, grid=(T // tr,),
        in_specs=[pl.BlockSpec((tr, d), lambda i: (i, 0)), pl.BlockSpec((1, d), lambda i: (0, 0))],
        out_specs=pl.BlockSpec((tr, d), lambda i: (i, 0)),
        out_shape=jax.ShapeDtypeStruct((T, d), BF16), compiler_params=_cp("parallel"), name=name)(x, g)


def _norm_bwd(x, g, dn, dres, *, name):
    T, d = x.shape
    tr = _tile(T, 512, 8)
    has_res = dres is not None

    def body(*refs):
        x_ref, g_ref, dn_ref = refs[:3]
        dr_ref = refs[3] if has_res else None
        dx_ref, dg_ref = refs[-2], refs[-1]

        @pl.when(pl.program_id(0) == 0)
        def _():
            dg_ref[...] = jnp.zeros_like(dg_ref)

        xv = x_ref[...]
        dnv = dn_ref[...].astype(F32)
        r = lax.rsqrt(jnp.mean(xv * xv, axis=-1, keepdims=True) + EPS)
        xh = xv * r
        dg_ref[...] += jnp.sum(dnv * xh, axis=0, keepdims=True)
        dxh = dnv * g_ref[...]
        dx = r * (dxh - xh * jnp.mean(dxh * xh, axis=-1, keepdims=True))
        if has_res:
            dx = dx + dr_ref[...]
        dx_ref[...] = dx

    row = pl.BlockSpec((tr, d), lambda i: (i, 0))
    vec = pl.BlockSpec((1, d), lambda i: (0, 0))
    ins = [x, g, dn] + ([dres] if has_res else [])
    return pl.pallas_call(
        body, grid=(T // tr,), in_specs=[row, vec, row] + ([row] if has_res else []),
        out_specs=[row, vec],
        out_shape=[jax.ShapeDtypeStruct((T, d), F32), jax.ShapeDtypeStruct((1, d), F32)],
        compiler_params=_cp("arbitrary"), name=name)(*ins)


def _final_loss(h, g, target, *, name):
    T, d = h.shape
    tr = _tile(T, 512, 8)

    def body(h_ref, g_ref, t_ref, l_ref, dh_ref, dg_ref):
        @pl.when(pl.program_id(0) == 0)
        def _():
            l_ref[...] = jnp.zeros_like(l_ref)
            dg_ref[...] = jnp.zeros_like(dg_ref)

        xv = h_ref[...]
        r = lax.rsqrt(jnp.mean(xv * xv, axis=-1, keepdims=True) + EPS)
        xh = xv * r
        e = xh * g_ref[...] - t_ref[...]
        l_ref[...] += jnp.sum(e * e, axis=0, keepdims=True)
        dy = e * (1.0 / d)
        dg_ref[...] += jnp.sum(dy * xh, axis=0, keepdims=True)
        dxh = dy * g_ref[...]
        dh_ref[...] = r * (dxh - xh * jnp.mean(dxh * xh, axis=-1, keepdims=True))

    row = pl.BlockSpec((tr, d), lambda i: (i, 0))
    vec = pl.BlockSpec((1, d), lambda i: (0, 0))
    return pl.pallas_call(
        body, grid=(T // tr,), in_specs=[row, vec, row], out_specs=[vec, row, vec],
        out_shape=[jax.ShapeDtypeStruct((1, d), F32), jax.ShapeDtypeStruct((T, d), F32),
                   jax.ShapeDtypeStruct((1, d), F32)],
        compiler_params=_cp("arbitrary"), name=name)(h, g, target)


def _gate_up_fwd(n, wgu, *, name):
    T, d = n.shape
    f = wgu.shape[1] // 2
    tm, tn = _tile(T, 512, 8), _tile(f, DFF)
    nf = f // tn

    tc = _tile(tn, EPI_COLS)

    def body(n_ref, wg_ref, wu_ref, g_ref, u_ref, a_ref):
        nv = n_ref[...].astype(MXU)
        for j in range(tn // tc):
            sl = slice(j * tc, (j + 1) * tc)
            gv = _nn(nv, wg_ref[:, sl].astype(MXU))
            uv = _nn(nv, wu_ref[:, sl].astype(MXU))
            g_ref[:, sl] = gv.astype(BF16)
            u_ref[:, sl] = uv.astype(BF16)
            a_ref[:, sl] = (gv * _sig(gv) * uv).astype(BF16)

    out = pl.BlockSpec((tm, tn), lambda i, j: (i, j))
    act = jax.ShapeDtypeStruct((T, f), BF16)
    return pl.pallas_call(
        body, grid=(T // tm, nf),
        in_specs=[pl.BlockSpec((tm, d), lambda i, j: (i, 0)), pl.BlockSpec((d, tn), lambda i, j: (0, j)),
                  pl.BlockSpec((d, tn), lambda i, j: (0, j + nf))],
        out_specs=[out, out, out], out_shape=[act, act, act], compiler_params=_cp("parallel", "parallel"),
        name=name)(n, wgu, wgu)


def _act_bwd(dh, wd, gate, up, scale, *, name):
    T, d = dh.shape
    f = wd.shape[0]
    tm, tn = _tile(T, 512, 8), _tile(f, DFF)

    tc = _tile(tn, EPI_COLS)

    def body(dh_ref, wd_ref, g_ref, u_ref, dg_ref, du_ref):
        dhv = dh_ref[...].astype(MXU)
        for j in range(tn // tc):
            sl = slice(j * tc, (j + 1) * tc)
            da = scale * _nt(dhv, wd_ref[sl, :].astype(MXU))
            gv, uv = g_ref[:, sl].astype(F32), u_ref[:, sl].astype(F32)
            s = _sig(gv)
            dg_ref[:, sl] = (da * uv * (s * (1.0 + gv * (1.0 - s)))).astype(BF16)
            du_ref[:, sl] = (da * (gv * s)).astype(BF16)

    tile = pl.BlockSpec((tm, tn), lambda i, j: (i, j))
    act = jax.ShapeDtypeStruct((T, f), BF16)
    return pl.pallas_call(
        body, grid=(T // tm, f // tn),
        in_specs=[pl.BlockSpec((tm, d), lambda i, j: (i, 0)), pl.BlockSpec((tn, d), lambda i, j: (j, 0)), tile, tile],
        out_specs=[tile, tile], out_shape=[act, act], compiler_params=_cp("parallel", "parallel"),
        name=name)(dh, wd, gate, up)


def _shift_down(x, d, ri):
    if d == 0:
        return x
    return jnp.where(ri >= d, pltpu.roll(x, d, 0), 0.0)


def _shift_up(x, d, ri):
    if d == 0:
        return x
    s = x.shape[0]
    return jnp.where(ri < s - d, pltpu.roll(x, s - d, 0), 0.0)


def _conv_taps(x, w_ref, ktaps, ri):
    acc = None
    for k in range(ktaps):
        t = w_ref[k:k + 1, :] * _shift_down(x, ktaps - 1 - k, ri)
        acc = t if acc is None else acc + t
    return acc


def _conv_a_fwd(pp, w8, bl, s, *, name):
    tc = 256
    nb = D // tc

    def body(b_ref, c_ref, v_ref, w_ref, o_ref):
        ri = lax.broadcasted_iota(jnp.int32, (s, tc), 0)
        cv = c_ref[...] * v_ref[...]
        o_ref[...] = (b_ref[...] * _conv_taps(cv, w_ref, 3, ri)).astype(BF16)

    def col(off):
        return pl.BlockSpec((s, tc), lambda b, j: (b, off // tc + j))

    return pl.pallas_call(
        body, grid=(bl, nb),
        in_specs=[col(O_AB), col(O_AC), col(O_AV), pl.BlockSpec((8, tc), lambda b, j: (0, j))],
        out_specs=pl.BlockSpec((s, tc), lambda b, j: (b, j)),
        out_shape=jax.ShapeDtypeStruct((bl * s, D), BF16), compiler_params=_cp("parallel", "parallel"),
        name=name)(pp, pp, pp, w8)


def _conv_a_bwd(pp, w8, dya, bl, s, *, name):
    tc = 256
    nb = D // tc

    def body(b_ref, c_ref, v_ref, w_ref, dy_ref, db_ref, dc_ref, dv_ref, dw_ref):
        @pl.when(pl.program_id(1) == 0)
        def _():
            dw_ref[...] = jnp.zeros_like(dw_ref)

        ri = lax.broadcasted_iota(jnp.int32, (s, tc), 0)
        cvec, vvec, dy = c_ref[...], v_ref[...], dy_ref[...]
        cv = cvec * vvec
        db_ref[...] = (dy * _conv_taps(cv, w_ref, 3, ri)).astype(BF16)
        dconv = dy * b_ref[...]
        dcv = None
        for k in range(3):
            t = w_ref[k:k + 1, :] * _shift_up(dconv, 2 - k, ri)
            dcv = t if dcv is None else dcv + t
            dw_ref[k:k + 1, :] += jnp.sum(dconv * _shift_down(cv, 2 - k, ri), axis=0, keepdims=True)
        dc_ref[...] = (dcv * vvec).astype(BF16)
        dv_ref[...] = (dcv * cvec).astype(BF16)

    def col(off):
        return pl.BlockSpec((s, tc), lambda j, b: (b, off // tc + j))

    plain = pl.BlockSpec((s, tc), lambda j, b: (b, j))
    wspec = pl.BlockSpec((8, tc), lambda j, b: (0, j))
    act = jax.ShapeDtypeStruct((bl * s, D), BF16)
    return pl.pallas_call(
        body, grid=(nb, bl), in_specs=[col(O_AB), col(O_AC), col(O_AV), wspec, plain],
        out_specs=[plain, plain, plain, wspec],
        out_shape=[act, act, act, jax.ShapeDtypeStruct((8, D), F32)],
        compiler_params=_cp("parallel", "arbitrary"), name=name)(pp, pp, pp, w8, dya)


def _conv_ssm_fwd(pp, w8, bias, bl, s, *, name):
    tc = 256
    width = DI + 2 * NG * NS
    nb = width // tc

    def body(x_ref, w_ref, b_ref, o_ref):
        ri = lax.broadcasted_iota(jnp.int32, (s, tc), 0)
        pre = _conv_taps(x_ref[...], w_ref, 4, ri) + b_ref[...]
        o_ref[...] = pre * _sig(pre)

    return pl.pallas_call(
        body, grid=(bl, nb),
        in_specs=[pl.BlockSpec((s, tc), lambda b, j: (b, O_XBC // tc + j)),
                  pl.BlockSpec((8, tc), lambda b, j: (0, j)), pl.BlockSpec((1, tc), lambda b, j: (0, j))],
        out_specs=pl.BlockSpec((s, tc), lambda b, j: (b, j)),
        out_shape=jax.ShapeDtypeStruct((bl * s, width), F32), compiler_params=_cp("parallel", "parallel"),
        name=name)(pp, w8, bias)


def _conv_ssm_bwd(pp, w8, bias, dxc, ch_off, bl, s, *, name):
    n = dxc.shape[1]
    tc = 256
    nb = n // tc
    o0 = ch_off // tc

    def body(x_ref, w_ref, b_ref, d_ref, dx_ref, dw_ref, db_ref):
        @pl.when(pl.program_id(1) == 0)
        def _():
            dw_ref[...] = jnp.zeros_like(dw_ref)
            db_ref[...] = jnp.zeros_like(db_ref)

        ri = lax.broadcasted_iota(jnp.int32, (s, tc), 0)
        xv = x_ref[...]
        pre = _conv_taps(xv, w_ref, 4, ri) + b_ref[...]
        sg = _sig(pre)
        dpre = d_ref[...] * (sg * (1.0 + pre * (1.0 - sg)))
        db_ref[...] += jnp.sum(dpre, axis=0, keepdims=True)
        dx = None
        for k in range(4):
            t = w_ref[k:k + 1, :] * _shift_up(dpre, 3 - k, ri)
            dx = t if dx is None else dx + t
            dw_ref[k:k + 1, :] += jnp.sum(dpre * _shift_down(xv, 3 - k, ri), axis=0, keepdims=True)
        dx_ref[...] = dx.astype(BF16)

    plain = pl.BlockSpec((s, tc), lambda j, b: (b, j))
    return pl.pallas_call(
        body, grid=(nb, bl),
        in_specs=[pl.BlockSpec((s, tc), lambda j, b: (b, O_XBC // tc + o0 + j)),
                  pl.BlockSpec((8, tc), lambda j, b: (0, o0 + j)), pl.BlockSpec((1, tc), lambda j, b: (0, o0 + j)),
                  plain],
        out_specs=[plain, pl.BlockSpec((8, tc), lambda j, b: (0, j)), pl.BlockSpec((1, tc), lambda j, b: (0, j))],
        out_shape=[jax.ShapeDtypeStruct((bl * s, n), BF16), jax.ShapeDtypeStruct((8, n), F32),
                   jax.ShapeDtypeStruct((1, n), F32)],
        compiler_params=_cp("parallel", "arbitrary"), name=name)(pp, w8, bias, dxc)


def _softplus(x):
    return jnp.maximum(x, 0.0) + jnp.log1p(jnp.exp(-jnp.abs(x)))


def _dt_fwd(pp, bias128, *, name):
    T = pp.shape[0]
    tr = _tile(T, 1024, 8)

    def body(x_ref, b_ref, o_ref):
        lane = lax.broadcasted_iota(jnp.int32, (tr, 128), 1)
        o_ref[...] = jnp.where(lane < NH, _softplus(x_ref[...] + b_ref[...]), 0.0)

    return pl.pallas_call(
        body, grid=(T // tr,),
        in_specs=[pl.BlockSpec((tr, 128), lambda i: (i, O_DT // 128)), pl.BlockSpec((1, 128), lambda i: (0, 0))],
        out_specs=pl.BlockSpec((tr, 128), lambda i: (i, 0)),
        out_shape=jax.ShapeDtypeStruct((T, 128), F32), compiler_params=_cp("parallel"), name=name)(pp, bias128)


def _dt_bwd(pp, bias128, ddt, *, name):
    T = pp.shape[0]
    tr = _tile(T, 1024, 8)

    def body(x_ref, b_ref, d_ref, o_ref, db_ref):
        @pl.when(pl.program_id(0) == 0)
        def _():
            db_ref[...] = jnp.zeros_like(db_ref)

        lane = lax.broadcasted_iota(jnp.int32, (tr, 128), 1)
        dr = jnp.where(lane < NH, d_ref[...] * _sig(x_ref[...] + b_ref[...]), 0.0)
        db_ref[...] += jnp.sum(dr, axis=0, keepdims=True)
        o_ref[...] = dr.astype(BF16)

    row = pl.BlockSpec((tr, 128), lambda i: (i, 0))
    vec = pl.BlockSpec((1, 128), lambda i: (0, 0))
    return pl.pallas_call(
        body, grid=(T // tr,),
        in_specs=[pl.BlockSpec((tr, 128), lambda i: (i, O_DT // 128)), vec, row],
        out_specs=[row, vec],
        out_shape=[jax.ShapeDtypeStruct((T, 128), BF16), jax.ShapeDtypeStruct((1, 128), F32)],
        compiler_params=_cp("arbitrary"), name=name)(pp, bias128, ddt)


def _ssd_common(dt, dtt, arow, acol):
    ri = lax.broadcasted_iota(jnp.int32, (CH, CH), 0)
    ci = lax.broadcasted_iota(jnp.int32, (CH, CH), 1)
    tril = ri >= ci
    triu = ri <= ci
    acs_col = _nn(tril.astype(F32), dt * arow, HI)
    acs_row = _nn(dtt * acol, triu.astype(F32), HI)
    return tril, triu, acs_col, acs_row


def _pair_terms(q, dt, acs_col, acs_row, tril, lo):
    ha, hb = 2 * q, 2 * q + 1
    col_a, col_b = acs_col[:, ha:ha + 1], acs_col[:, hb:hb + 1]
    row_a, row_b = acs_row[ha:ha + 1, :], acs_row[hb:hb + 1, :]
    last_a, last_b = acs_col[CH - 1:CH, ha:ha + 1], acs_col[CH - 1:CH, hb:hb + 1]
    out = dict(
        dtsel=jnp.where(lo, dt[:, ha:ha + 1], dt[:, hb:hb + 1]),
        d_a=jnp.exp(jnp.where(tril, col_a - row_a, NEG)), d_b=jnp.exp(jnp.where(tril, col_b - row_b, NEG)),
        esel=jnp.where(lo, jnp.exp(col_a), jnp.exp(col_b)),
        fsel=jnp.where(lo, jnp.exp(last_a - col_a), jnp.exp(last_b - col_b)),
        g_a=jnp.exp(last_a), g_b=jnp.exp(last_b))
    return out


def _ssd_fwd(xc, pp, dtg, dtt, arow, acol, dexp, ng, bl, s, *, name):
    nc = s // CH
    T = bl * s

    def body(xs_ref, bm_ref, cm_ref, z_ref, dt_ref, dtt_ref, arow_ref, acol_ref, dexp_ref, ng_ref,
             y_ref, yn_ref, prev_ref, st_ref):
        @pl.when(pl.program_id(2) == 0)
        def _():
            st_ref[...] = jnp.zeros_like(st_ref)

        dt = dt_ref[...]
        tril, triu, acs_col, acs_row = _ssd_common(dt, dtt_ref[...], -jnp.exp(arow_ref[...]), -jnp.exp(acol_ref[...]))
        bm, cm = bm_ref[...].astype(MXU), cm_ref[...].astype(MXU)
        cb = _nt(cm, bm)
        lo = lax.broadcasted_iota(jnp.int32, (CH, 128), 1) < HD
        sub_lo = lax.broadcasted_iota(jnp.int32, (128, NS), 0) < HD
        for q in range(4):
            t = _pair_terms(q, dt, acs_col, acs_row, tril, lo)
            x = xs_ref[:, 128 * q:128 * (q + 1)]
            xd = x * t["dtsel"]
            y = (_nn((cb * t["d_a"]).astype(MXU), jnp.where(lo, xd, 0.0).astype(MXU))
                 + _nn((cb * t["d_b"]).astype(MXU), jnp.where(lo, 0.0, xd).astype(MXU)))
            prev = st_ref[q]
            prev_ref[q] = prev
            y = y + t["esel"] * _nt(cm, prev.astype(MXU))
            st_ref[q] = prev * jnp.where(sub_lo, t["g_a"], t["g_b"]) + _tn((xd * t["fsel"]).astype(MXU), bm)
            y_ref[:, 128 * q:128 * (q + 1)] = y + dexp_ref[:, 128 * q:128 * (q + 1)] * x
        zv = z_ref[...]
        yg = y_ref[...] * (zv * _sig(zv))
        r = lax.rsqrt(jnp.mean(yg * yg, axis=-1, keepdims=True) + EPS)
        yn_ref[...] = (yg * r * ng_ref[...]).astype(BF16)

    def row(width, off_blocks):
        return pl.BlockSpec((CH, width), lambda g, b, c: (b * nc + c, off_blocks + g))

    return pl.pallas_call(
        body, grid=(NG, bl, nc),
        in_specs=[row(GW, 0), row(NS, DI // NS), row(NS, DI // NS + NG), row(GW, O_Z // GW), row(128, 0),
                  pl.BlockSpec((None, 8, CH), lambda g, b, c: (g, 0, b * nc + c)),
                  pl.BlockSpec((1, 128), lambda g, b, c: (0, g)),
                  pl.BlockSpec((None, 8, 1), lambda g, b, c: (g, 0, 0)),
                  pl.BlockSpec((1, GW), lambda g, b, c: (0, g)), pl.BlockSpec((1, GW), lambda g, b, c: (0, g))],
        out_specs=[row(GW, 0), row(GW, 0),
                   pl.BlockSpec((None, 4, 128, NS), lambda g, b, c: (b * nc + c, g, 0, 0))],
        out_shape=[jax.ShapeDtypeStruct((T, DI), F32), jax.ShapeDtypeStruct((T, DI), BF16),
                   jax.ShapeDtypeStruct((bl * nc, 16, 128, NS), F32)],
        scratch_shapes=[pltpu.VMEM((4, 128, NS), F32)],
        compiler_params=_cp("parallel", "parallel", "arbitrary"), name=name,
    )(xc, xc, xc, pp, dtg, dtt, arow, acol, dexp, ng)


def _ssd_bwd(dyn, y, xc, pp, dtg, dtt, arow, acol, dexp, ng, prev, bl, s, *, name):
    nc = s // CH
    T = bl * s

    def rsum(v):
        return jnp.sum(v, axis=1, keepdims=True)

    def asum(v):
        return jnp.sum(jnp.sum(v, axis=0, keepdims=True), axis=1, keepdims=True)

    def body(dyn_ref, y_ref, xs_ref, bm_ref, cm_ref, z_ref, dt_ref, dtt_ref, arow_ref, acol_ref, dexp_ref, ng_ref,
             prev_ref, dz_ref, dxs_ref, db_ref, dc_ref, ddt_ref, dng_ref, dd_ref, dal_ref, dst_ref):
        @pl.when((pl.program_id(1) == 0) & (pl.program_id(2) == 0))
        def _():
            dng_ref[...] = jnp.zeros_like(dng_ref)
            dd_ref[...] = jnp.zeros_like(dd_ref)
            dal_ref[...] = jnp.zeros_like(dal_ref)

        @pl.when(pl.program_id(2) == 0)
        def _():
            dst_ref[...] = jnp.zeros_like(dst_ref)

        yv, zv, xsv, dexp_v = y_ref[...], z_ref[...], xs_ref[...], dexp_ref[...]
        sz = _sig(zv)
        silu = zv * sz
        yg = yv * silu
        r = lax.rsqrt(jnp.mean(yg * yg, axis=-1, keepdims=True) + EPS)
        yh = yg * r
        dynv = dyn_ref[...]
        dng_ref[...] += jnp.sum(dynv * yh, axis=0, keepdims=True)
        dyh = dynv * ng_ref[...]
        dyg = r * (dyh - yh * jnp.mean(dyh * yh, axis=-1, keepdims=True))
        dz_ref[...] = (dyg * yv * (sz * (1.0 + zv * (1.0 - sz)))).astype(BF16)
        dy_all = dyg * silu
        dd_ref[...] += jnp.sum(dy_all * xsv, axis=0, keepdims=True)

        dt = dt_ref[...]
        arow_v = -jnp.exp(arow_ref[...])
        tril, triu, acs_col, acs_row = _ssd_common(dt, dtt_ref[...], arow_v, -jnp.exp(acol_ref[...]))
        bm, cm = bm_ref[...].astype(MXU), cm_ref[...].astype(MXU)
        cb = _nt(cm, bm)
        lane = lax.broadcasted_iota(jnp.int32, (CH, 128), 1)
        is_last = lax.broadcasted_iota(jnp.int32, (CH, 128), 0) == CH - 1
        lo = lane < HD
        sub_lo = lax.broadcasted_iota(jnp.int32, (128, NS), 0) < HD
        dcb = jnp.zeros((CH, CH), F32)
        dc_acc = jnp.zeros((CH, NS), F32)
        db_acc = jnp.zeros((CH, NS), F32)
        dacs = jnp.zeros((CH, 128), F32)
        ddtx = jnp.zeros((CH, 128), F32)
        csum = jnp.zeros((8, CH), F32)
        sub8 = lax.broadcasted_iota(jnp.int32, (8, CH), 0)
        for q in range(4):
            ha, hb = 2 * q, 2 * q + 1
            sl = slice(128 * q, 128 * (q + 1))
            t = _pair_terms(q, dt, acs_col, acs_row, tril, lo)
            x, dy = xsv[:, sl], dy_all[:, sl]
            xd = x * t["dtsel"]
            xd_m = xd.astype(MXU)
            dy_lo, dy_hi = jnp.where(lo, dy, 0.0).astype(MXU), jnp.where(lo, 0.0, dy).astype(MXU)
            m_a, m_b = cb * t["d_a"], cb * t["d_b"]
            prev_m = prev_ref[q].astype(MXU)
            dnext = dst_ref[q]
            dnext_m = dnext.astype(MXU)
            bds = _nt(bm, dnext_m)
            dxd = _tn(m_a.astype(MXU), dy_lo) + _tn(m_b.astype(MXU), dy_hi) + t["fsel"] * bds
            dye_m = (dy * t["esel"]).astype(MXU)
            dst_ref[q] = dnext * jnp.where(sub_lo, t["g_a"], t["g_b"]) + _tn(dye_m, cm)
            dm_a, dm_b = _nt(dy_lo, xd_m), _nt(dy_hi, xd_m)
            dcb = dcb + dm_a * t["d_a"] + dm_b * t["d_b"]
            g_a, g_b = dm_a * m_a, dm_b * m_b
            csum = (csum + jnp.where(sub8 == ha, jnp.sum(g_a, axis=0, keepdims=True), 0.0)
                    + jnp.where(sub8 == hb, jnp.sum(g_b, axis=0, keepdims=True), 0.0))
            tf = t["fsel"] * xd * bds
            tyf = dy * (t["esel"] * _nt(cm, prev_m)) - tf
            dpp = dnext * prev_ref[q]
            ea = asum(jnp.where(lo, tf, 0.0)) + t["g_a"] * asum(jnp.where(sub_lo, dpp, 0.0))
            eb = asum(jnp.where(lo, 0.0, tf)) + t["g_b"] * asum(jnp.where(sub_lo, 0.0, dpp))
            ra = rsum(g_a + jnp.where(lo, tyf, 0.0)) + jnp.where(is_last, ea, 0.0)
            rb = rsum(g_b + jnp.where(lo, 0.0, tyf)) + jnp.where(is_last, eb, 0.0)
            dacs = dacs + jnp.where(lane == ha, ra, 0.0) + jnp.where(lane == hb, rb, 0.0)
            tx = dxd * x
            ddtx = (ddtx + jnp.where(lane == ha, rsum(jnp.where(lo, tx, 0.0)), 0.0)
                    + jnp.where(lane == hb, rsum(jnp.where(lo, 0.0, tx)), 0.0))
            dxs_ref[:, sl] = dxd * t["dtsel"] + dexp_v[:, sl] * dy
            dc_acc = dc_acc + _nn(dye_m, prev_m)
            db_acc = db_acc + _nn((xd * t["fsel"]).astype(MXU), dnext_m)
        dcb_m = dcb.astype(MXU)
        dc_ref[...] = dc_acc + _nn(dcb_m, bm)
        db_ref[...] = db_acc + _tn(dcb_m, cm)
        dacs = dacs - jnp.concatenate([csum, jnp.zeros((CH - 8, CH), F32)], axis=0).T
        dla = _nn(triu.astype(F32), dacs, HI)
        ddt_ref[...] = dla * arow_v + ddtx
        dal_ref[...] += jnp.sum(dla * dt, axis=0, keepdims=True) * arow_v

    def row(width, off_blocks):
        return pl.BlockSpec((CH, width), lambda g, b, c: (b * nc + nc - 1 - c, off_blocks + g))

    gvec = pl.BlockSpec((1, GW), lambda g, b, c: (0, g))
    hvec = pl.BlockSpec((1, 128), lambda g, b, c: (0, g))
    return pl.pallas_call(
        body, grid=(NG, bl, nc),
        in_specs=[row(GW, 0), row(GW, 0), row(GW, 0), row(NS, DI // NS), row(NS, DI // NS + NG), row(GW, O_Z // GW),
                  row(128, 0), pl.BlockSpec((None, 8, CH), lambda g, b, c: (g, 0, b * nc + nc - 1 - c)),
                  hvec, pl.BlockSpec((None, 8, 1), lambda g, b, c: (g, 0, 0)), gvec, gvec,
                  pl.BlockSpec((None, 4, 128, NS), lambda g, b, c: (b * nc + nc - 1 - c, g, 0, 0))],
        out_specs=[row(GW, 0), row(GW, 0), row(NS, 0), row(NS, 0), row(128, 0), gvec, gvec, hvec],
        out_shape=[jax.ShapeDtypeStruct((T, DI), BF16), jax.ShapeDtypeStruct((T, DI), F32),
                   jax.ShapeDtypeStruct((T, NG * NS), F32), jax.ShapeDtypeStruct((T, NG * NS), F32),
                   jax.ShapeDtypeStruct((T, NG * 128), F32), jax.ShapeDtypeStruct((1, DI), F32),
                   jax.ShapeDtypeStruct((1, DI), F32), jax.ShapeDtypeStruct((1, NG * 128), F32)],
        scratch_shapes=[pltpu.VMEM((4, 128, NS), F32)],
        compiler_params=_cp("arbitrary", "arbitrary", "arbitrary"), name=name,
    )(dyn, y, xc, xc, xc, pp, dtg, dtt, arow, acol, dexp, ng, prev)


def _merge_fwd(pp, ya, yb, *, name):
    T = ya.shape[0]
    tr = _tile(T, 512, 8)

    def body(ga_ref, gb_ref, ya_ref, yb_ref, o_ref):
        o_ref[...] = (_sig(ga_ref[...]) * ya_ref[...] + _sig(gb_ref[...]) * yb_ref[...]).astype(BF16)

    row = pl.BlockSpec((tr, D), lambda i: (i, 0))
    return pl.pallas_call(
        body, grid=(T // tr,),
        in_specs=[pl.BlockSpec((tr, D), lambda i: (i, O_GA // D)), pl.BlockSpec((tr, D), lambda i: (i, O_GB // D)),
                  row, row],
        out_specs=row, out_shape=jax.ShapeDtypeStruct((T, D), BF16), compiler_params=_cp("parallel"),
        name=name)(pp, pp, ya, yb)


def _merge_bwd(pp, ya, yb, dm, *, name):
    T = ya.shape[0]
    tr = _tile(T, 512, 8)

    def body(ga_ref, gb_ref, ya_ref, yb_ref, dm_ref, dya_ref, dyb_ref, dga_ref, dgb_ref):
        sa, sb, dmv = _sig(ga_ref[...]), _sig(gb_ref[...]), dm_ref[...]
        dya_ref[...] = (dmv * sa).astype(BF16)
        dyb_ref[...] = (dmv * sb).astype(BF16)
        dga_ref[...] = (dmv * ya_ref[...] * (sa * (1.0 - sa))).astype(BF16)
        dgb_ref[...] = (dmv * yb_ref[...] * (sb * (1.0 - sb))).astype(BF16)

    row = pl.BlockSpec((tr, D), lambda i: (i, 0))
    act = jax.ShapeDtypeStruct((T, D), BF16)
    return pl.pallas_call(
        body, grid=(T // tr,),
        in_specs=[pl.BlockSpec((tr, D), lambda i: (i, O_GA // D)), pl.BlockSpec((tr, D), lambda i: (i, O_GB // D)),
                  row, row, row],
        out_specs=[row, row, row, row], out_shape=[act, act, act, act], compiler_params=_cp("parallel"),
        name=name)(pp, pp, ya, yb, dm)


def _softmax_rows(sc):
    e = jnp.exp(sc - jnp.max(sc, axis=-1, keepdims=True))
    return e / jnp.sum(e, axis=-1, keepdims=True)


def _attn_fwd(q, kv, bl, s, *, name):
    m = kv.shape[0] // bl
    tq = _tile(s, 512)
    nq = s // tq
    scale = 1.0 / math.sqrt(XD)

    def body(q_ref, k_ref, v_ref, o_ref):
        p = _softmax_rows(_nt(q_ref[...], k_ref[...]) * scale)
        o_ref[...] = _nn(p.astype(MXU), v_ref[...]).astype(BF16)

    qspec = pl.BlockSpec((tq, XD), lambda b, h, i: (b * nq + i, h))
    return pl.pallas_call(
        body, grid=(bl, XH, nq),
        in_specs=[qspec, pl.BlockSpec((m, XD), lambda b, h, i: (b, h)),
                  pl.BlockSpec((m, XD), lambda b, h, i: (b, XH + h))],
        out_specs=qspec, out_shape=jax.ShapeDtypeStruct((bl * s, D), BF16),
        compiler_params=_cp("parallel", "parallel", "parallel"), name=name)(q, kv, kv)


def _attn_bwd(q, kv, do, bl, s, *, name):
    m = kv.shape[0] // bl
    tq = _tile(s, 512)
    nq = s // tq
    scale = 1.0 / math.sqrt(XD)

    def body(q_ref, k_ref, v_ref, do_ref, dq_ref, dk_ref, dv_ref):
        @pl.when(pl.program_id(2) == 0)
        def _():
            dk_ref[...] = jnp.zeros_like(dk_ref)
            dv_ref[...] = jnp.zeros_like(dv_ref)

        qv, kvv, vv, dov = q_ref[...], k_ref[...], v_ref[...], do_ref[...]
        p = _softmax_rows(_nt(qv, kvv) * scale)
        dp = _nt(dov, vv)
        ds = (p * (dp - jnp.sum(dp * p, axis=-1, keepdims=True)) * scale).astype(MXU)
        dq_ref[...] = _nn(ds, kvv).astype(BF16)
        dk_ref[...] += _tn(ds, qv)
        dv_ref[...] += _tn(p.astype(MXU), dov)

    qspec = pl.BlockSpec((tq, XD), lambda b, h, i: (b * nq + i, h))
    kspec = pl.BlockSpec((m, XD), lambda b, h, i: (b, h))
    return pl.pallas_call(
        body, grid=(bl, XH, nq),
        in_specs=[qspec, kspec, pl.BlockSpec((m, XD), lambda b, h, i: (b, XH + h)), qspec],
        out_specs=[qspec, kspec, kspec],
        out_shape=[jax.ShapeDtypeStruct((bl * s, D), BF16), jax.ShapeDtypeStruct((bl * m, D), F32),
                   jax.ShapeDtypeStruct((bl * m, D), F32)],
        compiler_params=_cp("parallel", "parallel", "arbitrary"), name=name)(q, kv, kv, do)


def _row_tile(r, c, max_elems=512 * 1024, align=16):
    best = None
    for t in range(align, r + 1, align):
        if r % t == 0 and t * c <= max_elems:
            best = t
    return best if best is not None else r


def _addn(a, others, *, name, also_bf16=False):
    r, c = a.shape
    tr = _row_tile(r, c)
    n = len(others)

    def body(*refs):
        acc = refs[0][...].astype(F32)
        for o_ref in refs[1:1 + n]:
            acc = acc + o_ref[...].astype(F32)
        refs[1 + n][...] = acc
        if also_bf16:
            refs[2 + n][...] = acc.astype(BF16)

    spec = pl.BlockSpec((tr, c), lambda i: (i, 0))
    shapes = [jax.ShapeDtypeStruct((r, c), F32)] + ([jax.ShapeDtypeStruct((r, c), BF16)] if also_bf16 else [])
    out = pl.pallas_call(
        body, grid=(r // tr,), in_specs=[spec] * (1 + n), out_specs=[spec] * len(shapes), out_shape=shapes,
        compiler_params=_cp("parallel"), name=name)(a, *others)
    return out if also_bf16 else out[0]


def _sum_leading(a, *, name):
    n, r, c = a.shape

    def body(a_ref, o_ref):
        acc = a_ref[0]
        for i in range(1, n):
            acc = acc + a_ref[i]
        o_ref[...] = acc

    return pl.pallas_call(body, out_shape=jax.ShapeDtypeStruct((r, c), F32), name=name)(a)


def _adamw_math(wv, gv, mv, vv):
    m2 = ADAM_B1 * mv + (1.0 - ADAM_B1) * gv
    v2 = ADAM_B2 * vv + (1.0 - ADAM_B2) * (gv * gv)
    m_hat = m2 / (1.0 - ADAM_B1 ** ADAM_STEP)
    v_hat = v2 / (1.0 - ADAM_B2 ** ADAM_STEP)
    return -ADAM_LR * (m_hat / (jnp.sqrt(v_hat) + ADAM_EPS) + ADAM_WD * wv), m2, v2


def _adamw(w, g, m, v, *, name):
    r, c = w.shape
    tr = _row_tile(r, c, align=8)

    def body(w_ref, g_ref, m_ref, v_ref, d_ref, mo_ref, vo_ref):
        d_ref[...], mo_ref[...], vo_ref[...] = _adamw_math(w_ref[...], g_ref[...], m_ref[...], v_ref[...])

    spec = pl.BlockSpec((tr, c), lambda i: (i, 0))
    shp = jax.ShapeDtypeStruct((r, c), F32)
    return pl.pallas_call(
        body, grid=(r // tr,), in_specs=[spec] * 4, out_specs=[spec] * 3, out_shape=[shp] * 3,
        compiler_params=_cp("parallel"), name=name)(w, g, m, v)


def _adamw_halves(w, g_mine, g_other, m, v, c, *, name):
    r, cols = w.shape
    h = r // 2
    tr = _row_tile(h, cols, align=8)
    nh = h // tr

    def body(c_ref, w_ref, gm_ref, go_ref, m_ref, v_ref, g_ref, d_ref, mo_ref, vo_ref):
        gv = jnp.where(pl.program_id(0) // nh == c_ref[0], gm_ref[...], go_ref[...])
        g_ref[...] = gv
        d_ref[...], mo_ref[...], vo_ref[...] = _adamw_math(w_ref[...], gv, m_ref[...], v_ref[...])

    full = pl.BlockSpec((tr, cols), lambda i, c_: (i, 0))
    half = pl.BlockSpec((tr, cols), lambda i, c_: (i % nh, 0))
    shp = jax.ShapeDtypeStruct((r, cols), F32)
    return pl.pallas_call(
        body,
        grid_spec=pltpu.PrefetchScalarGridSpec(num_scalar_prefetch=1, grid=(2 * nh,),
                                               in_specs=[full, half, half, full, full], out_specs=[full] * 4),
        out_shape=[shp] * 4, compiler_params=_cp("parallel"), name=name,
    )(jnp.reshape(c, (1,)).astype(jnp.int32), w, g_mine, g_other, m, v)


def _flip(i, d):
    return 1 - i if d else i


def _comm(name, ins, out_shapes, n_remote, n_local, plan, aliases=None):
    n_in, n_out = len(ins), len(out_shapes)

    def body(*refs):
        in_refs, out_refs = refs[:n_in], refs[n_in:n_in + n_out]
        send_sems, recv_sems = refs[n_in + n_out], refs[n_in + n_out + 1]
        x, y, c = lax.axis_index("x"), lax.axis_index("y"), lax.axis_index("c")
        remote, local = plan(in_refs, out_refs, x, y, c)
        assert len(remote) == n_remote and len(local) == n_local
        copies = []
        if n_local:
            loc_sems = refs[n_in + n_out + 2]
            copies += [pltpu.make_async_copy(s_, d_, loc_sems.at[i]) for i, (s_, d_) in enumerate(local)]
        copies += [pltpu.make_async_remote_copy(src_ref=s_, dst_ref=d_, send_sem=send_sems.at[i],
                                                recv_sem=recv_sems.at[i], device_id=dev, device_id_type=MESH)
                   for i, (s_, d_, dev) in enumerate(remote)]
        for cp in copies:
            cp.start()
        for cp in copies:
            cp.wait()

    hbm = pl.BlockSpec(memory_space=pl.ANY)
    scratch = [pltpu.SemaphoreType.DMA((n_remote,)), pltpu.SemaphoreType.DMA((n_remote,))]
    if n_local:
        scratch.append(pltpu.SemaphoreType.DMA((n_local,)))
    return pl.pallas_call(
        body, in_specs=[hbm] * n_in, out_specs=[hbm] * n_out, out_shape=out_shapes, scratch_shapes=scratch,
        input_output_aliases=aliases or {}, compiler_params=pltpu.CompilerParams(has_side_effects=True),
        name=name)(*ins)


HBM_SPEC = pl.BlockSpec(memory_space=pltpu.HBM)
SEM_SPEC = pl.BlockSpec(memory_space=pltpu.SEMAPHORE)
DATAFLOW = pltpu.SideEffectType.DATAFLOW_SIDE_EFFECTING


def _remote_copies(plan, srcs, lands, send_sems, recv_sems, n_copies):
    x, y, c = lax.axis_index("x"), lax.axis_index("y"), lax.axis_index("c")
    copies = plan(srcs, lands, x, y, c)
    assert len(copies) == n_copies
    return [pltpu.make_async_remote_copy(src_ref=s_, dst_ref=d_, send_sem=send_sems.at[i], recv_sem=recv_sems.at[i],
                                         device_id=dev, device_id_type=MESH) for i, (s_, d_, dev) in enumerate(copies)]


def _split_start(name, srcs, lands, n_copies, plan, after=None):
    ns, nb = len(srcs), len(srcs) + len(lands)
    n_after = 0 if after is None else 1
    n_in = nb + n_after

    def body(*refs):
        for cp in _remote_copies(plan, refs[:ns], refs[ns:nb], refs[n_in], refs[n_in + 1], n_copies):
            cp.start()
        refs[-1][...] = jnp.zeros_like(refs[-1])

    arrays = [pltpu.with_memory_space_constraint(a_, pltpu.HBM) for a_ in list(srcs) + list(lands)]
    out = pl.pallas_call(
        body, name=name,
        out_shape=(pltpu.SemaphoreType.DMA((n_copies,)), pltpu.SemaphoreType.DMA((n_copies,)),
                   *[pltpu.HBM(a_.shape, a_.dtype) for a_ in arrays], jax.ShapeDtypeStruct((8, 128), F32)),
        in_specs=[HBM_SPEC] * nb + [pl.BlockSpec(memory_space=pl.ANY)] * n_after,
        out_specs=(SEM_SPEC, SEM_SPEC, *[HBM_SPEC] * nb, pl.BlockSpec(memory_space=pltpu.VMEM)),
        input_output_aliases={i: 2 + i for i in range(nb)},
        compiler_params=pltpu.CompilerParams(has_side_effects=DATAFLOW))(*arrays, *([after] * n_after))
    return (out[0], out[1], list(out[2:2 + nb])), out[-1]


def _split_wait(name, handle, ns, n_copies, plan, after):
    send_sems, recv_sems, bufs = handle
    nb = len(bufs)

    def body(*refs):
        for cp in _remote_copies(plan, refs[:ns], refs[ns:nb], refs[nb], refs[nb + 1], n_copies):
            cp.wait_send()
            cp.wait_recv()

    out = pl.pallas_call(
        body, name=name, out_shape=[pltpu.HBM(b_.shape, b_.dtype) for b_ in bufs],
        in_specs=[HBM_SPEC] * nb + [SEM_SPEC, SEM_SPEC, pl.BlockSpec(memory_space=pl.ANY)],
        out_specs=[HBM_SPEC] * nb, input_output_aliases={i: i for i in range(nb)},
        compiler_params=pltpu.CompilerParams(has_side_effects=DATAFLOW))(*bufs, send_sems, recv_sems, after)
    return list(out[ns:])


def _gather_start(shards, tag, after=None):
    n = len(shards)
    lands = [lax.empty((4,) + s.shape, s.dtype) for s in shards]

    def plan(srcs, dsts, x, y, c):
        k = 2 * x + y
        copies = []
        for w_ref, o_ref in zip(srcs, dsts):
            h = w_ref.shape[0] // 2
            rows = pl.ds(c * h, h)
            copies += [(w_ref.at[rows], o_ref.at[k, rows], (_flip(x, dx), _flip(y, dy), c)) for dx, dy in CHIP_FLIPS]
        return copies

    handle, token = _split_start(f"gather_{tag}_start", shards, lands, 3 * n, plan, after)
    return (handle, plan, n), token


def _gather_wait(started, after, tag):
    handle, plan, n = started
    return _split_wait(f"gather_{tag}_wait", handle, n, 3 * n, plan, after)


def _gather_d2d(lands, before, tag):
    n = len(lands)

    def plan_d2d(in_refs, out_refs, x, y, c):
        remote = []
        for o_ref in out_refs:
            h = o_ref.shape[1] // 2
            for dx, dy in CHIP_FLIPS:
                half = o_ref.at[2 * _flip(x, dx) + _flip(y, dy), pl.ds(c * h, h)]
                remote.append((half, half, (x, y, 1 - c)))
        return remote, []

    return _comm(f"gather_{tag}_d2d", list(lands) + list(before),
                 [jax.ShapeDtypeStruct(l_.shape, l_.dtype) for l_ in lands], 3 * n, 0, plan_d2d,
                 aliases={i: i for i in range(n)})


def _pair_plan(in_refs, out_refs, x, y, c):
    return [(i_, o_, (x, y, 1 - c)) for i_, o_ in zip(in_refs, out_refs)], []


def _rs_start(grads, tag):
    n = len(grads)
    c = lax.axis_index("c")
    halves = [g.shape[1] // 2 for g in grads]
    mine = [lax.dynamic_slice_in_dim(g, c * h, h, axis=1) for g, h in zip(grads, halves)]
    send_a = [lax.dynamic_slice_in_dim(g, (1 - c) * h, h, axis=1).astype(BF16) for g, h in zip(grads, halves)]
    recv_a = _comm(f"rs_pair_{tag}", send_a, [jax.ShapeDtypeStruct(s.shape, BF16) for s in send_a], n, 0, _pair_plan)
    pair, pair_b = [], []
    for i, (mi, ra) in enumerate(zip(mine, recv_a)):
        four, h, cols = mi.shape
        p32, p16 = _addn(mi.reshape(four * h, cols), [ra.reshape(four * h, cols)], name=f"rs_pair_sum_{tag}_{i}",
                         also_bf16=True)
        pair.append(p32.reshape(four, h, cols))
        pair_b.append(p16.reshape(four, h, cols))

    def plan(srcs, dsts, x, y, c_):
        copies = []
        for i_, o_ in zip(srcs, dsts):
            for j, (dx, dy) in enumerate(CHIP_FLIPS):
                fx, fy = _flip(x, dx), _flip(y, dy)
                copies.append((i_.at[2 * fx + fy], o_.at[j], (fx, fy, c_)))
        return copies

    lands = [lax.empty((3,) + p.shape[1:], BF16) for p in pair_b]
    handle, token = _split_start(f"rs_chips_{tag}_start", pair_b, lands, 3 * n, plan)
    return (handle, plan, n, pair), token


def _rs_finish(started, after, tag):
    handle, plan, n, pair = started
    recv_b = _split_wait(f"rs_chips_{tag}_wait", handle, n, 3 * n, plan, after)
    k = 2 * lax.axis_index("x") + lax.axis_index("y")
    tot = [_addn(lax.dynamic_index_in_dim(p, k, 0, keepdims=False), [rb[0], rb[1], rb[2]],
                 name=f"rs_chip_sum_{tag}_{i}") for i, (p, rb) in enumerate(zip(pair, recv_b))]
    other = _comm(f"rs_halves_{tag}", tot, [jax.ShapeDtypeStruct(t.shape, F32) for t in tot], n, 0, _pair_plan)
    return tot, other


def _gather_all(vec, *, name):
    out = jax.ShapeDtypeStruct((8,) + vec.shape, vec.dtype)

    def plan(in_refs, out_refs, x, y, c):
        me = 4 * x + 2 * y + c
        remote = [(in_refs[0], out_refs[0].at[me], (_flip(x, dx), _flip(y, dy), _flip(c, dc)))
                  for dx in (0, 1) for dy in (0, 1) for dc in (0, 1) if (dx, dy, dc) != (0, 0, 0)]
        return remote, [(in_refs[0], out_refs[0].at[me])]

    return _comm(name, [vec], [out], 7, 1, plan)[0]


def _pack(parts):
    flat = [p.reshape(-1).astype(F32) for p in parts]
    total = sum(f.shape[0] for f in flat)
    n = -(-total // 1024) * 128
    vec = jnp.concatenate(flat + [jnp.zeros((8 * n - total,), F32)]).reshape(8, n)
    offs, o = [], 0
    for f in flat:
        offs.append((o, f.shape[0]))
        o += f.shape[0]
    return vec, offs


def _unpack(vec, offs, shapes):
    flat = vec.reshape(-1)
    return [flat[o:o + n].reshape(s) for (o, n), s in zip(offs, shapes)]


BIG = (("ffn1_w_gate_up", "col"), ("ffn1_w_down", "row"), ("w_in", "col"), ("w_out_a", "row"), ("w_out_ssm", "row"),
       ("w_mix_out", "row"), ("w_q", "row"), ("w_kv", "col"), ("w_o_x", "row"), ("ffn2_w_gate_up", "col"),
       ("ffn2_w_down", "row"))
SMALL = ("ffn1_norm", "mix_norm", "conv_a_w", "ssm_conv_w", "ssm_conv_b", "ssm_dt_bias", "ssm_a_log", "ssm_d",
         "ssm_norm", "xattn_norm", "mem_norm", "ffn2_norm", "final_norm")
WEIGHTS = ("ffn1_norm", "ffn1_w_gate_up", "ffn1_w_down", "mix_norm", "w_in", "conv_a_w", "w_out_a", "ssm_conv_w",
           "ssm_conv_b", "ssm_dt_bias", "ssm_a_log", "ssm_d", "ssm_norm", "w_out_ssm", "w_mix_out", "xattn_norm",
           "mem_norm", "w_q", "w_kv", "w_o_x", "ffn2_norm", "ffn2_w_gate_up", "ffn2_w_down", "final_norm")


GATHER_GROUPS = (("a", ("ffn1_w_gate_up",)), ("b", ("ffn1_w_down", "w_in")),
                 ("c", ("w_out_a", "w_out_ssm", "w_mix_out", "w_q", "w_kv", "w_o_x", "ffn2_w_gate_up", "ffn2_w_down")))


def _full_weight(land, own, kind, k):
    parts = [jnp.where(k == kk, own, land[kk]) for kk in range(4)]
    return jnp.concatenate(parts, axis=0 if kind == "row" else 1)


class _GatheredWeights:
    def __init__(self, shards, k):
        self.shards, self.k = shards, k
        self.full = {}
        self.n_done = 0
        self.started, self.after = self._start(0, None)

    def _start(self, gi, after):
        tag, names = GATHER_GROUPS[gi]
        return _gather_start([self.shards[n] for n in names], tag, after)

    def mark(self, value):
        self.after = value

    def __getitem__(self, name):
        if name not in self.full:
            tag, names = GATHER_GROUPS[self.n_done]
            assert name in names, (name, tag)
            lands = _gather_wait(self.started, self.after, tag)
            before = []
            if self.n_done + 1 < len(GATHER_GROUPS):
                self.started, token = self._start(self.n_done + 1, lands[0])
                before = [token]
            lands = _gather_d2d(lands, before, tag)
            for n, land in zip(names, lands):
                w = _full_weight(land, self.shards[n], dict(BIG)[n], self.k)
                self.full[n] = _pad_w_in(w) if n == "w_in" else w
            self.n_done += 1
        return self.full[name]


def _shard_major(dw, kind):
    if isinstance(dw, tuple):
        return jnp.concatenate([_shard_major(p, "col2") for p in dw], axis=0)
    if kind == "col2":
        return jnp.transpose(dw.reshape(dw.shape[0], 2, dw.shape[1] // 2), (1, 0, 2))
    if kind == "row":
        return dw.reshape(4, dw.shape[0] // 4, dw.shape[1])
    return jnp.transpose(dw.reshape(dw.shape[0], 4, dw.shape[1] // 4), (1, 0, 2))


def _pad_rows8(w):
    return jnp.concatenate([w, jnp.zeros((8 - w.shape[0], w.shape[1]), w.dtype)], axis=0)


def _group_lanes(v):
    r = v.shape[0]
    return jnp.pad(v.reshape(r, NG, NH // NG), ((0, 0), (0, 0), (0, 128 - NH // NG))).reshape(r, NG * 128)


def _ungroup_lanes(v):
    r = v.shape[0]
    return v.reshape(r, NG, 128)[:, :, :NH // NG].reshape(r, NH)


def _local_step(wfull, small, x, mem, target, token=0.0, on_grads=None):
    bl, s, _ = x.shape
    T = bl * s
    x2, t2 = x.reshape(T, D), target.reshape(T, D)
    mem2 = mem.reshape(-1, D)
    g = {}
    tok = [token]
    mark = getattr(wfull, "mark", lambda value: None)

    def gain(name):
        return small[name].reshape(1, -1) + tok[0]

    def emit(tag, names):
        if on_grads is not None:
            tok[0] = tok[0] + on_grads(tag, {n: g[n] for n in names})

    def ffn_fwd(h, norm, wgu, wd, tag):
        n = _norm_fwd(h, gain(norm), name=f"{tag}_norm")
        gate, up, a = _gate_up_fwd(n, wfull[wgu], name=f"{tag}_gate_up")
        mark(a)
        out = _mm(a, wfull[wd], "nn", tk=DFF, scale=FFN_RES, residual=h, name=f"{tag}_down")
        return out, (n, gate, up, a)

    def ffn_bwd(dh, h, norm, wgu, wd, saved, tag):
        n, gate, up, a = saved
        dgate, dup = _act_bwd(dh, wfull[wd], gate, up, FFN_RES, name=f"{tag}_d_act")
        g[wd] = _mm(a, dh, "tn", tm=1408, scale=FFN_RES, name=f"{tag}_d_w_down")
        g[wgu] = (_mm(n, dgate, "tn", tn=1408, name=f"{tag}_d_w_gate"), _mm(n, dup, "tn", tn=1408, name=f"{tag}_d_w_up"))
        emit(tag, (wgu, wd))
        dn = _mm(dgate, wfull[wgu], "nt", a2=dup, tk=1408, name=f"{tag}_d_norm_out")
        dh_in, g[norm] = _norm_bwd(h, gain(norm), dn, dh, name=f"{tag}_d_norm")
        return dh_in

    h1, ffn1_saved = ffn_fwd(x2, "ffn1_norm", "ffn1_w_gate_up", "ffn1_w_down", "ffn1")
    mark(h1)
    u = _norm_fwd(h1, gain("mix_norm"), name="mix_norm")
    pp = _mm(u, wfull["w_in"], "nn", tn=1152, name="in_proj")
    wa8 = _pad_rows8(small["conv_a_w"])
    ws8 = _pad_rows8(small["ssm_conv_w"])
    conv_b = gain("ssm_conv_b")
    bias128 = jnp.pad(gain("ssm_dt_bias"), ((0, 0), (0, 128 - NH)))
    ya_pre = _conv_a_fwd(pp, wa8, bl, s, name="conv_a")
    xc = _conv_ssm_fwd(pp, ws8, conv_b, bl, s, name="conv_ssm")
    mark(xc)
    dt = _dt_fwd(pp, bias128, name="dt")
    dtg = _group_lanes(dt[:, :NH])
    dtt = dt[:, :NH].T.reshape(NG, NH // NG, T)
    alog = gain("ssm_a_log")
    arow, acol = _group_lanes(alog), alog.reshape(NG, NH // NG, 1)
    dexp = jnp.repeat(gain("ssm_d"), HD, axis=1)
    ng = gain("ssm_norm")
    y, yn, prev = _ssd_fwd(xc, pp, dtg, dtt, arow, acol, dexp, ng, bl, s, name="ssd")
    ya = _mm(ya_pre, wfull["w_out_a"], "nn", tn=1024, name="out_a")
    yb = _mm(yn, wfull["w_out_ssm"], "nn", tn=1024, tk=DI, name="out_ssm")
    merged = _merge_fwd(pp, ya, yb, name="merge")
    h2 = _mm(merged, wfull["w_mix_out"], "nn", tn=1024, residual=h1, name="mix_out")
    un = _norm_fwd(h2, gain("xattn_norm"), name="xattn_norm")
    q = _mm(un, wfull["w_q"], "nn", tn=1024, out_dtype=BF16, name="q_proj")
    mn = _norm_fwd(mem2, gain("mem_norm"), name="mem_norm")
    kv = _mm(mn, wfull["w_kv"], "nn", tn=1024, out_dtype=BF16, name="kv_proj")
    o = _attn_fwd(q, kv, bl, s, name="attn")
    h3 = _mm(o, wfull["w_o_x"], "nn", tn=1024, residual=h2, name="attn_out")
    h4, ffn2_saved = ffn_fwd(h3, "ffn2_norm", "ffn2_w_gate_up", "ffn2_w_down", "ffn2")
    sq_err, dh4, dgf = _final_loss(h4, gain("final_norm"), t2, name="final_loss")
    g["final_norm"] = dgf

    dh3 = ffn_bwd(dh4, h3, "ffn2_norm", "ffn2_w_gate_up", "ffn2_w_down", ffn2_saved, "ffn2")
    do = _mm(dh3, wfull["w_o_x"], "nt", tn=1024, out_dtype=BF16, name="d_attn_o")
    g["w_o_x"] = _mm(o, dh3, "tn",name="d_w_o_x")
    dq, dk, dv = _attn_bwd(q, kv, do, bl, s, name="d_attn")
    dun = _mm(dq, wfull["w_q"], "nt", tn=1024, name="d_xattn_norm_out")
    g["w_q"] = _mm(un, dq, "tn",name="d_w_q")
    dkv = jnp.concatenate([dk, dv], axis=1)
    dmn = _mm(dkv, wfull["w_kv"], "nt", tn=1024, tk=2 * D, name="d_mem_norm_out")
    g["w_kv"] = _mm(mn, dkv, "tn",name="d_w_kv")
    emit("attn", ("w_q", "w_kv", "w_o_x"))
    _, g["mem_norm"] = _norm_bwd(mem2, gain("mem_norm"), dmn, None, name="d_mem_norm")
    dh2, g["xattn_norm"] = _norm_bwd(h2, gain("xattn_norm"), dun, dh3, name="d_xattn_norm")
    dmerged = _mm(dh2, wfull["w_mix_out"], "nt", tn=1024, name="d_merged")
    g["w_mix_out"] = _mm(merged, dh2, "tn",name="d_w_mix_out")
    dya, dyb, dga, dgb = _merge_bwd(pp, ya, yb, dmerged, name="d_merge")
    dya_pre = _mm(dya, wfull["w_out_a"], "nt", tn=1024, name="d_conv_a_out")
    g["w_out_a"] = _mm(ya_pre, dya, "tn",name="d_w_out_a")
    dyn = _mm(dyb, wfull["w_out_ssm"], "nt", tn=DI, name="d_ssd_out")
    g["w_out_ssm"] = _mm(yn, dyb, "tn",name="d_w_out_ssm")
    d_ab, d_ac, d_av, dwa8 = _conv_a_bwd(pp, wa8, dya_pre, bl, s, name="d_conv_a")
    g["conv_a_w"] = dwa8[:3]
    dz, dxs, dbm, dcm, ddtg, g["ssm_norm"], ddexp, dalg = _ssd_bwd(
        dyn, y, xc, pp, dtg, dtt, arow, acol, dexp, ng, prev, bl, s, name="d_ssd")
    g["ssm_d"] = ddexp.reshape(NH, HD).sum(axis=1).reshape(1, NH)
    g["ssm_a_log"] = _ungroup_lanes(dalg)
    conv_parts = [_conv_ssm_bwd(pp, ws8, conv_b, dpart, off, bl, s, name=f"d_conv_ssm_{tag}")
                  for dpart, off, tag in ((dxs, 0, "x"), (dbm, DI, "b"), (dcm, DI + NG * NS, "c"))]
    g["ssm_conv_w"] = jnp.concatenate([p[1] for p in conv_parts], axis=1)[:4]
    g["ssm_conv_b"] = jnp.concatenate([p[2] for p in conv_parts], axis=1)
    ddt = jnp.pad(_ungroup_lanes(ddtg), ((0, 0), (0, 128 - NH)))
    ddt_raw, dbias = _dt_bwd(pp, bias128, ddt, name="d_dt")
    g["ssm_dt_bias"] = dbias[:, :NH]
    dpp = jnp.concatenate([d_ab, d_ac, d_av, dz] + [p[0] for p in conv_parts] + [dga, dgb, ddt_raw], axis=1)
    g["w_in"] = _mm(u, dpp, "tn", tn=1152, name="d_w_in")
    emit("mix", ("w_in", "w_out_a", "w_out_ssm", "w_mix_out"))
    du = _mm(dpp, wfull["w_in"], "nt", tk=3456, name="d_mix_norm_out")
    dh1, g["mix_norm"] = _norm_bwd(h1, gain("mix_norm"), du, dh2, name="d_mix_norm")
    dx = ffn_bwd(dh1, x2, "ffn1_norm", "ffn1_w_gate_up", "ffn1_w_down", ffn1_saved, "ffn1")
    return sq_err, dx, g


def _pad_w_in(w):
    return jnp.concatenate([w[:, :O_GA], w[:, O_GA + NH:], w[:, O_GA:O_GA + NH],
                            jnp.zeros((w.shape[0], NPP - NIN), w.dtype)], axis=1)


def _unpad_w_in(w):
    return jnp.concatenate([w[:, :O_GA], w[:, O_DT:O_DT + NH], w[:, O_GA:O_DT]], axis=1)


def kernel(x, mem, ffn1_norm, ffn1_w_gate_up, ffn1_w_down, mix_norm, w_in, conv_a_w, w_out_a, ssm_conv_w, ssm_conv_b, ssm_dt_bias, ssm_a_log, ssm_d, ssm_norm, w_out_ssm, w_mix_out, xattn_norm, mem_norm, w_q, w_kv, w_o_x, ffn2_norm, ffn2_w_gate_up, ffn2_w_down, final_norm, loss_target, m_ffn1_norm, m_ffn1_w_gate_up, m_ffn1_w_down, m_mix_norm, m_w_in, m_conv_a_w, m_w_out_a, m_ssm_conv_w, m_ssm_conv_b, m_ssm_dt_bias, m_ssm_a_log, m_ssm_d, m_ssm_norm, m_w_out_ssm, m_w_mix_out, m_xattn_norm, m_mem_norm, m_w_q, m_w_kv, m_w_o_x, m_ffn2_norm, m_ffn2_w_gate_up, m_ffn2_w_down, m_final_norm, v_ffn1_norm, v_ffn1_w_gate_up, v_ffn1_w_down, v_mix_norm, v_w_in, v_conv_a_w, v_w_out_a, v_ssm_conv_w, v_ssm_conv_b, v_ssm_dt_bias, v_ssm_a_log, v_ssm_d, v_ssm_norm, v_w_out_ssm, v_w_mix_out, v_xattn_norm, v_mem_norm, v_w_q, v_w_kv, v_w_o_x, v_ffn2_norm, v_ffn2_w_gate_up, v_ffn2_w_down, v_final_norm):
    a = dict(locals())
    xi, yi = lax.axis_index("x"), lax.axis_index("y")
    k = 2 * xi + yi

    wfull = _GatheredWeights({n: a[n][0].astype(BF16) for n, _ in BIG}, k)
    conv_vec, conv_offs = _pack([a["conv_a_w"], a["ssm_conv_w"]])
    conv_all = _gather_all(conv_vec, name="gather_conv_w")
    conv_sh = [_unpack(conv_all[2 * kk], conv_offs, [a["conv_a_w"].shape[1:], a["ssm_conv_w"].shape[1:]])
               for kk in range(4)]
    small = {n: a[n] for n in SMALL}
    small["conv_a_w"] = jnp.concatenate([cs[0] for cs in conv_sh], axis=1)
    small["ssm_conv_w"] = jnp.concatenate([cs[1] for cs in conv_sh], axis=1)

    rs_started = []

    def on_grads(tag, grads):
        names = [n for n, _ in BIG if n in grads]
        shard_major = [_shard_major(_unpad_w_in(grads[n]) if n == "w_in" else grads[n], dict(BIG)[n]) for n in names]
        st, tk = _rs_start(shard_major, tag)
        rs_started.append((tag, names, st))
        return tk[0, 0]

    sq_err, dx, g = _local_step(wfull, small, x, mem, loss_target, wfull.after[0, 0], on_grads)
    loss = lax.psum(0.5 / D * jnp.sum(sq_err), ("x", "y", "c"))

    ci = lax.axis_index("c")
    out = {}
    for tag, names, st in rs_started:
        g_mine, g_other = _rs_finish(st, dx, tag)
        for n, gm, go in zip(names, g_mine, g_other):
            res = _adamw_halves(a[n][0], gm, go, a["m_" + n][0], a["v_" + n][0], ci, name=f"adamw_{n}")
            out[n] = tuple(t.reshape(a[n].shape) for t in res)

    full_shapes = [g[n].shape for n in SMALL]
    gvec, goffs = _pack([g[n] for n in SMALL])
    gsum = _sum_leading(_gather_all(gvec, name="gather_small_grads"), name="sum_small_grads")
    gsmall = dict(zip(SMALL, _unpack(gsum, goffs, full_shapes)))
    for n in ("conv_a_w", "ssm_conv_w"):
        width = a[n].shape[2]
        gsmall[n] = lax.dynamic_slice_in_dim(gsmall[n], k * width, width, axis=1)
    local_shapes = [a[n].shape for n in SMALL]
    packs = [_pack([t[n] for n in SMALL]) for t in
             ({n: a[n] for n in SMALL}, gsmall, {n: a["m_" + n] for n in SMALL}, {n: a["v_" + n] for n in SMALL})]
    offs = packs[0][1]
    res = _adamw(*[p[0] for p in packs], name="adamw_small")
    unp = [_unpack(r, offs, local_shapes) for r in res]
    for i, n in enumerate(SMALL):
        out[n] = (gsmall[n].reshape(a[n].shape), unp[0][i], unp[1][i], unp[2][i])

    grad_x = dx.reshape(x.shape)
    return (loss, grad_x, *[out[n][0] for n in WEIGHTS], *[out[n][1] for n in WEIGHTS],
            *[out[n][2] for n in WEIGHTS], *[out[n][3] for n in WEIGHTS])
```

```python
import functools
import math

import jax
import jax.numpy as jnp
from jax import lax
from jax.experimental import pallas as pl
from jax.experimental.pallas import tpu as pltpu

F32 = jnp.float32
BF16 = jnp.bfloat16
MXU = jnp.bfloat16
HI = lax.Precision.HIGHEST

D = 1024
DFF = 2816
DI = 2048
NH, HD, NG, NS, CH = 32, 64, 4, 128, 128
GW = DI // NG
XH, XD = 4, 256
EPS = 1e-6
NEG = -1e30
CA_TILE = 256
O_CA, O_Z, O_XBC, O_GA, O_GB, O_DT, NPP = 0, 3072, 5120, 8192, 9216, 10240, 10368
NIN = 10272
FFN_RES = 0.5
ADAM_LR, ADAM_B1, ADAM_B2, ADAM_EPS, ADAM_WD, ADAM_STEP = 0.001, 0.9, 0.999, 1e-08, 0.01, 10
VMEM_LIMIT = 56 * 1024 * 1024
EPI_COLS = 256
MESH = pl.DeviceIdType.MESH
CHIP_FLIPS = ((1, 0), (0, 1), (1, 1))


def _cp(*sem):
    return pltpu.CompilerParams(dimension_semantics=sem, vmem_limit_bytes=VMEM_LIMIT)


def _tile(n, pref, align=128):
    if n <= pref:
        return n
    t = (pref // align) * align
    while t >= align:
        if n % t == 0:
            return t
        t -= align
    raise ValueError((n, pref))


def _dot(a, b, dims, prec=None):
    return lax.dot_general(a, b, (dims, ((), ())), preferred_element_type=F32, precision=prec)


def _nn(a, b, prec=None):
    return _dot(a, b, ((1,), (0,)), prec)


def _nt(a, b):
    return _dot(a, b, ((1,), (1,)))


def _tn(a, b):
    return _dot(a, b, ((0,), (0,)))


def _sig(x):
    return jax.nn.sigmoid(x)


def _mm(a, b, mode, *, name, tm=1024, tn=1024, tk=None, out_dtype=F32, scale=None, residual=None, a2=None):
    if tk is None:
        tk = 2048 if mode == "tn" else 1024
    if mode == "nn":
        (M, K), (K2, N) = a.shape, b.shape
    elif mode == "nt":
        (M, K), (N, K2) = a.shape, b.shape
        if a2 is not None:
            assert a2.shape == a.shape
            K2 = K2 // 2
    else:
        (K, M), (K2, N) = a.shape, b.shape
    assert K == K2, (name, a.shape, b.shape)
    tm, tn, tk = _tile(M, tm), _tile(N, tn), _tile(K, tk)
    nk = K // tk
    if mode == "nn":
        a_spec = pl.BlockSpec((tm, tk), lambda i, j, k: (i, k))
        b_spec = pl.BlockSpec((tk, tn), lambda i, j, k: (k, j))
        dims = ((1,), (0,))
    elif mode == "nt":
        a_spec = pl.BlockSpec((tm, tk), lambda i, j, k: (i, k))
        b_spec = pl.BlockSpec((tn, tk), lambda i, j, k: (j, k))
        dims = ((1,), (1,))
    else:
        a_spec = pl.BlockSpec((tk, tm), lambda i, j, k: (k, i))
        b_spec = pl.BlockSpec((tk, tn), lambda i, j, k: (k, j))
        dims = ((0,), (0,))
    o_spec = pl.BlockSpec((tm, tn), lambda i, j, k: (i, j))
    has_res = residual is not None

    def finish(acc, r_ref, o_ref):
        if scale is not None:
            acc = acc * scale
        if has_res:
            acc = acc + r_ref[...]
        o_ref[...] = acc.astype(out_dtype)

    dual = a2 is not None
    n_in = 2 + 2 * dual + has_res

    def body(*refs):
        a_ref, b_ref = refs[0], refs[1]
        r_ref = refs[n_in - 1] if has_res else None
        o_ref = refs[n_in]
        part = _dot(a_ref[...].astype(MXU), b_ref[...].astype(MXU), dims)
        if dual:
            part = part + _dot(refs[2][...].astype(MXU), refs[3][...].astype(MXU), dims)
        if nk == 1:
            finish(part, r_ref, o_ref)
            return
        acc_ref = refs[-1]
        k = pl.program_id(2)

        @pl.when(k == 0)
        def _():
            acc_ref[...] = part

        @pl.when(k > 0)
        def _():
            acc_ref[...] += part

        @pl.when(k == nk - 1)
        def _():
            finish(acc_ref[...], r_ref, o_ref)

    ins, in_specs = [a, b], [a_spec, b_spec]
    if dual:
        ins += [a2, b]
        in_specs += [a_spec, pl.BlockSpec((tn, tk), lambda i, j, k: (j, k + nk))]
    if has_res:
        ins.append(residual)
        in_specs.append(o_spec)
    return pl.pallas_call(
        body, grid=(M // tm, N // tn, nk), in_specs=in_specs, out_specs=o_spec,
        out_shape=jax.ShapeDtypeStruct((M, N), out_dtype),
        scratch_shapes=[pltpu.VMEM((tm, tn), F32)] if nk > 1 else [],
        compiler_params=_cp("parallel", "parallel", "arbitrary"), name=name)(*ins)


def _norm_fwd(x, g, *, name):
    T, d = x.shape
    tr = _tile(T, 512, 8)

    def body(x_ref, g_ref, o_ref):
        xv = x_ref[...]
        r = lax.rsqrt(jnp.mean(xv * xv, axis=-1, keepdims=True) + EPS)
        o_ref[...] = (xv * r * g_ref[...]).astype(BF16)

    return pl.pallas_call(
        body, grid=(T // tr,),
        in_specs=[pl.BlockSpec((tr, d), lambda i: (i, 0)), pl.BlockSpec((1, d), lambda i: (0, 0))],
        out_specs=pl.BlockSpec((tr, d), lambda i: (i, 0)),
        out_shape=jax.ShapeDtypeStruct((T, d), BF16), compiler_params=_cp("parallel"), name=name)(x, g)


def _norm_bwd(x, g, dn, dres, *, name):
    T, d = x.shape
    tr = _tile(T, 512, 8)
    has_res = dres is not None

    def body(*refs):
        x_ref, g_ref, dn_ref = refs[:3]
        dr_ref = refs[3] if has_res else None
        dx_ref, dg_ref = refs[-2], refs[-1]

        @pl.when(pl.program_id(0) == 0)
        def _():
            dg_ref[...] = jnp.zeros_like(dg_ref)

        xv = x_ref[...]
        dnv = dn_ref[...].astype(F32)
        r = lax.rsqrt(jnp.mean(xv * xv, axis=-1, keepdims=True) + EPS)
        xh = xv * r
        dg_ref[...] += jnp.sum(dnv * xh, axis=0, keepdims=True)
        dxh = dnv * g_ref[...]
        dx = r * (dxh - xh * jnp.mean(dxh * xh, axis=-1, keepdims=True))
        if has_res:
            dx = dx + dr_ref[...]
        dx_ref[...] = dx

    row = pl.BlockSpec((tr, d), lambda i: (i, 0))
    vec = pl.BlockSpec((1, d), lambda i: (0, 0))
    ins = [x, g, dn] + ([dres] if has_res else [])
    return pl.pallas_call(
        body, grid=(T // tr,), in_specs=[row, vec, row] + ([row] if has_res else []),
        out_specs=[row, vec],
        out_shape=[jax.ShapeDtypeStruct((T, d), F32), jax.ShapeDtypeStruct((1, d), F32)],
        compiler_params=_cp("arbitrary"), name=name)(*ins)


def _final_loss(h, g, target, *, name):
    T, d = h.shape
    tr = _tile(T, 512, 8)

    def body(h_ref, g_ref, t_ref, l_ref, dh_ref, dg_ref):
        @pl.when(pl.program_id(0) == 0)
        def _():
            l_ref[...] = jnp.zeros_like(l_ref)
            dg_ref[...] = jnp.zeros_like(dg_ref)

        xv = h_ref[...]
        r = lax.rsqrt(jnp.mean(xv * xv, axis=-1, keepdims=True) + EPS)
        xh = xv * r
        e = xh * g_ref[...] - t_ref[...]
        l_ref[...] += jnp.sum(e * e, axis=0, keepdims=True)
        dy = e * (1.0 / d)
        dg_ref[...] += jnp.sum(dy * xh, axis=0, keepdims=True)
        dxh = dy * g_ref[...]
        dh_ref[...] = r * (dxh - xh * jnp.mean(dxh * xh, axis=-1, keepdims=True))

    row = pl.BlockSpec((tr, d), lambda i: (i, 0))
    vec = pl.BlockSpec((1, d), lambda i: (0, 0))
    return pl.pallas_call(
        body, grid=(T // tr,), in_specs=[row, vec, row], out_specs=[vec, row, vec],
        out_shape=[jax.ShapeDtypeStruct((1, d), F32), jax.ShapeDtypeStruct((T, d), F32),
                   jax.ShapeDtypeStruct((1, d), F32)],
        compiler_params=_cp("arbitrary"), name=name)(h, g, target)


def _gate_up_fwd(n, wgu, *, name):
    T, d = n.shape
    f = wgu.shape[1] // 2
    tm, tn = _tile(T, 512, 8), _tile(f, DFF)
    nf = f // tn

    tc = _tile(tn, EPI_COLS)

    def body(n_ref, wg_ref, wu_ref, g_ref, u_ref, a_ref):
        nv = n_ref[...].astype(MXU)
        for j in range(tn // tc):
            sl = slice(j * tc, (j + 1) * tc)
            gv = _nn(nv, wg_ref[:, sl].astype(MXU))
            uv = _nn(nv, wu_ref[:, sl].astype(MXU))
            g_ref[:, sl] = gv.astype(BF16)
            u_ref[:, sl] = uv.astype(BF16)
            a_ref[:, sl] = (gv * _sig(gv) * uv).astype(BF16)

    out = pl.BlockSpec((tm, tn), lambda i, j: (i, j))
    act = jax.ShapeDtypeStruct((T, f), BF16)
    return pl.pallas_call(
        body, grid=(T // tm, nf),
        in_specs=[pl.BlockSpec((tm, d), lambda i, j: (i, 0)), pl.BlockSpec((d, tn), lambda i, j: (0, j)),
                  pl.BlockSpec((d, tn), lambda i, j: (0, j + nf))],
        out_specs=[out, out, out], out_shape=[act, act, act], compiler_params=_cp("parallel", "parallel"),
        name=name)(n, wgu, wgu)


def _act_bwd(dh, wd, gate, up, scale, *, name):
    T, d = dh.shape
    f = wd.shape[0]
    tm, tn = _tile(T, 512, 8), _tile(f, DFF)

    tc = _tile(tn, EPI_COLS)

    def body(dh_ref, wd_ref, g_ref, u_ref, dg_ref, du_ref):
        dhv = dh_ref[...].astype(MXU)
        for j in range(tn // tc):
            sl = slice(j * tc, (j + 1) * tc)
            da = scale * _nt(dhv, wd_ref[sl, :].astype(MXU))
            gv, uv = g_ref[:, sl].astype(F32), u_ref[:, sl].astype(F32)
            s = _sig(gv)
            dg_ref[:, sl] = (da * uv * (s * (1.0 + gv * (1.0 - s)))).astype(BF16)
            du_ref[:, sl] = (da * (gv * s)).astype(BF16)

    tile = pl.BlockSpec((tm, tn), lambda i, j: (i, j))
    act = jax.ShapeDtypeStruct((T, f), BF16)
    return pl.pallas_call(
        body, grid=(T // tm, f // tn),
        in_specs=[pl.BlockSpec((tm, d), lambda i, j: (i, 0)), pl.BlockSpec((tn, d), lambda i, j: (j, 0)), tile, tile],
        out_specs=[tile, tile], out_shape=[act, act], compiler_params=_cp("parallel", "parallel"),
        name=name)(dh, wd, gate, up)


CONV_ROWS = 64
CONV_PAD = 8


def _rows_down(ref, r0, d, cols=slice(None)):
    if r0 - d >= 0:
        return ref[pl.ds(r0 - d, CONV_ROWS), cols]
    assert r0 == 0
    v = ref[pl.ds(0, CONV_ROWS), cols]
    ri = lax.broadcasted_iota(jnp.int32, v.shape, 0)
    return jnp.where(ri >= d, pltpu.roll(v, d, 0), 0.0)


def _fold8(v):
    return jnp.sum(v.reshape(CONV_ROWS // 8, 8, v.shape[1]), axis=0)


def _taps(w_ref, views):
    acc = None
    for k, v in enumerate(views):
        t = w_ref[k:k + 1, :] * v
        acc = t if acc is None else acc + t
    return acc


def _conv_a_fwd(pp, w8, bl, s, *, name):
    tc = CA_TILE
    nb = D // tc
    bcol, ccol, vcol = slice(0, tc), slice(tc, 2 * tc), slice(2 * tc, 3 * tc)

    def body(p_ref, w_ref, o_ref):
        for r0 in range(0, s, CONV_ROWS):
            cv = [_rows_down(p_ref, r0, 2 - k, ccol) * _rows_down(p_ref, r0, 2 - k, vcol) for k in range(3)]
            o_ref[pl.ds(r0, CONV_ROWS), :] = (p_ref[pl.ds(r0, CONV_ROWS), bcol] * _taps(w_ref, cv)).astype(BF16)

    return pl.pallas_call(
        body, grid=(bl, nb),
        in_specs=[pl.BlockSpec((s, 3 * tc), lambda b, j: (b, j)), pl.BlockSpec((8, tc), lambda b, j: (0, j))],
        out_specs=pl.BlockSpec((s, tc), lambda b, j: (b, j)),
        out_shape=jax.ShapeDtypeStruct((bl * s, D), BF16), compiler_params=_cp("parallel", "parallel"),
        name=name)(pp, w8)


def _conv_a_bwd(pp, w8, dya, dpp, bl, s, *, name):
    tc = CA_TILE
    nb = D // tc
    bcol, ccol, vcol = slice(0, tc), slice(tc, 2 * tc), slice(2 * tc, 3 * tc)

    def body(p_ref, w_ref, dy_ref, dpp_in, d_ref, dw_ref, dcp):
        del dpp_in

        @pl.when(pl.program_id(1) == 0)
        def _():
            dw_ref[...] = jnp.zeros_like(dw_ref)

        dcp[pl.ds(s, CONV_PAD), :] = jnp.zeros((CONV_PAD, tc), F32)
        dw_acc = [jnp.zeros((8, tc), F32) for _ in range(3)]
        for r0 in reversed(range(0, s, CONV_ROWS)):
            rows = pl.ds(r0, CONV_ROWS)
            cs = [_rows_down(p_ref, r0, 2 - k, ccol) for k in range(3)]
            vs = [_rows_down(p_ref, r0, 2 - k, vcol) for k in range(3)]
            cv = [c_ * v_ for c_, v_ in zip(cs, vs)]
            dy = dy_ref[rows, :]
            d_ref[rows, bcol] = (dy * _taps(w_ref, cv)).astype(BF16)
            dconv = dy * p_ref[rows, bcol]
            dcp[rows, :] = dconv
            dcv = _taps(w_ref, [dcp[pl.ds(r0 + 2, CONV_ROWS), :], dcp[pl.ds(r0 + 1, CONV_ROWS), :], dconv])
            d_ref[rows, ccol] = (dcv * vs[2]).astype(BF16)
            d_ref[rows, vcol] = (dcv * cs[2]).astype(BF16)
            dw_acc = [acc + _fold8(dconv * cv_) for acc, cv_ in zip(dw_acc, cv)]
        for k in range(3):
            dw_ref[k:k + 1, :] += jnp.sum(dw_acc[k], axis=0, keepdims=True)

    wspec = pl.BlockSpec((8, tc), lambda j, b: (0, j))
    wide = pl.BlockSpec((s, 3 * tc), lambda j, b: (b, j))
    return pl.pallas_call(
        body, grid=(nb, bl),
        in_specs=[wide, wspec, pl.BlockSpec((s, tc), lambda j, b: (b, j)), pl.BlockSpec(memory_space=pl.ANY)],
        out_specs=[wide, wspec], out_shape=[jax.ShapeDtypeStruct(dpp.shape, dpp.dtype), jax.ShapeDtypeStruct((8, D), F32)],
        scratch_shapes=[pltpu.VMEM((s + CONV_PAD, tc), F32)], input_output_aliases={3: 0},
        compiler_params=_cp("parallel", "arbitrary"), name=name)(pp, w8, dya, dpp)


def _conv_ssm_fwd(pp, w8, bias, bl, s, *, name):
    tc = 256
    width = DI + 2 * NG * NS
    nb = width // tc

    def body(x_ref, w_ref, b_ref, o_ref):
        for r0 in range(0, s, CONV_ROWS):
            pre = _taps(w_ref, [_rows_down(x_ref, r0, 3 - k) for k in range(4)]) + b_ref[...]
            o_ref[pl.ds(r0, CONV_ROWS), :] = pre * _sig(pre)

    return pl.pallas_call(
        body, grid=(bl, nb),
        in_specs=[pl.BlockSpec((s, tc), lambda b, j: (b, O_XBC // tc + j)),
                  pl.BlockSpec((8, tc), lambda b, j: (0, j)), pl.BlockSpec((1, tc), lambda b, j: (0, j))],
        out_specs=pl.BlockSpec((s, tc), lambda b, j: (b, j)),
        out_shape=jax.ShapeDtypeStruct((bl * s, width), F32), compiler_params=_cp("parallel", "parallel"),
        name=name)(pp, w8, bias)


def _conv_ssm_bwd(pp, w8, bias, dxc, ch_off, dpp, bl, s, *, name):
    n = dxc.shape[1]
    tc = 256
    nb = n // tc
    o0 = ch_off // tc

    def body(x_ref, w_ref, b_ref, d_ref, dpp_in, dx_ref, dw_ref, db_ref, dp):
        del dpp_in

        @pl.when(pl.program_id(1) == 0)
        def _():
            dw_ref[...] = jnp.zeros_like(dw_ref)
            db_ref[...] = jnp.zeros_like(db_ref)

        dp[pl.ds(s, CONV_PAD), :] = jnp.zeros((CONV_PAD, tc), F32)
        dw_acc = [jnp.zeros((8, tc), F32) for _ in range(4)]
        db_acc = jnp.zeros((8, tc), F32)
        for r0 in reversed(range(0, s, CONV_ROWS)):
            rows = pl.ds(r0, CONV_ROWS)
            xs = [_rows_down(x_ref, r0, 3 - k) for k in range(4)]
            pre = _taps(w_ref, xs) + b_ref[...]
            sg = _sig(pre)
            dpre = d_ref[rows, :] * (sg * (1.0 + pre * (1.0 - sg)))
            dp[rows, :] = dpre
            dx = _taps(w_ref, [dp[pl.ds(r0 + 3 - k, CONV_ROWS), :] for k in range(3)] + [dpre])
            dx_ref[rows, :] = dx.astype(BF16)
            db_acc = db_acc + _fold8(dpre)
            dw_acc = [acc + _fold8(dpre * x_) for acc, x_ in zip(dw_acc, xs)]
        db_ref[...] += jnp.sum(db_acc, axis=0, keepdims=True)
        for k in range(4):
            dw_ref[k:k + 1, :] += jnp.sum(dw_acc[k], axis=0, keepdims=True)

    return pl.pallas_call(
        body, grid=(nb, bl),
        in_specs=[pl.BlockSpec((s, tc), lambda j, b: (b, O_XBC // tc + o0 + j)),
                  pl.BlockSpec((8, tc), lambda j, b: (0, o0 + j)), pl.BlockSpec((1, tc), lambda j, b: (0, o0 + j)),
                  pl.BlockSpec((s, tc), lambda j, b: (b, j)), pl.BlockSpec(memory_space=pl.ANY)],
        out_specs=[pl.BlockSpec((s, tc), lambda j, b: (b, O_XBC // tc + o0 + j)),
                   pl.BlockSpec((8, tc), lambda j, b: (0, j)), pl.BlockSpec((1, tc), lambda j, b: (0, j))],
        out_shape=[jax.ShapeDtypeStruct(dpp.shape, dpp.dtype), jax.ShapeDtypeStruct((8, n), F32),
                   jax.ShapeDtypeStruct((1, n), F32)],
        scratch_shapes=[pltpu.VMEM((s + CONV_PAD, tc), F32)], input_output_aliases={4: 0},
        compiler_params=_cp("parallel", "arbitrary"), name=name)(pp, w8, bias, dxc, dpp)


def _softplus(x):
    return jnp.maximum(x, 0.0) + jnp.log1p(jnp.exp(-jnp.abs(x)))


def _dt_fwd(pp, bias128, *, name):
    T = pp.shape[0]
    tr = _tile(T, 1024, 8)

    def body(x_ref, b_ref, o_ref):
        lane = lax.broadcasted_iota(jnp.int32, (tr, 128), 1)
        o_ref[...] = jnp.where(lane < NH, _softplus(x_ref[...] + b_ref[...]), 0.0)

    return pl.pallas_call(
        body, grid=(T // tr,),
        in_specs=[pl.BlockSpec((tr, 128), lambda i: (i, O_DT // 128)), pl.BlockSpec((1, 128), lambda i: (0, 0))],
        out_specs=pl.BlockSpec((tr, 128), lambda i: (i, 0)),
        out_shape=jax.ShapeDtypeStruct((T, 128), F32), compiler_params=_cp("parallel"), name=name)(pp, bias128)


def _dt_bwd(pp, bias128, ddt, dpp, *, name):
    T = pp.shape[0]
    tr = _tile(T, 1024, 8)

    def body(x_ref, b_ref, d_ref, dpp_in, o_ref, db_ref):
        del dpp_in

        @pl.when(pl.program_id(0) == 0)
        def _():
            db_ref[...] = jnp.zeros_like(db_ref)

        lane = lax.broadcasted_iota(jnp.int32, (tr, 128), 1)
        dr = jnp.where(lane < NH, d_ref[...] * _sig(x_ref[...] + b_ref[...]), 0.0)
        db_ref[...] += jnp.sum(dr, axis=0, keepdims=True)
        o_ref[...] = dr.astype(BF16)

    row = pl.BlockSpec((tr, 128), lambda i: (i, 0))
    col = pl.BlockSpec((tr, 128), lambda i: (i, O_DT // 128))
    vec = pl.BlockSpec((1, 128), lambda i: (0, 0))
    return pl.pallas_call(
        body, grid=(T // tr,),
        in_specs=[col, vec, row, pl.BlockSpec(memory_space=pl.ANY)], out_specs=[col, vec],
        out_shape=[jax.ShapeDtypeStruct(dpp.shape, dpp.dtype), jax.ShapeDtypeStruct((1, 128), F32)],
        input_output_aliases={3: 0}, compiler_params=_cp("arbitrary"), name=name)(pp, bias128, ddt, dpp)


def _ssd_common(dt, dtt, arow, acol):
    ri = lax.broadcasted_iota(jnp.int32, (CH, CH), 0)
    ci = lax.broadcasted_iota(jnp.int32, (CH, CH), 1)
    tril = ri >= ci
    triu = ri <= ci
    acs_col = _nn(tril.astype(F32), dt * arow, HI)
    acs_row = _nn(dtt * acol, triu.astype(F32), HI)
    return tril, triu, acs_col, acs_row


def _pair_terms(q, dt, acs_col, acs_row, tril, lo):
    ha, hb = 2 * q, 2 * q + 1
    col_a, col_b = acs_col[:, ha:ha + 1], acs_col[:, hb:hb + 1]
    row_a, row_b = acs_row[ha:ha + 1, :], acs_row[hb:hb + 1, :]
    last_a, last_b = acs_col[CH - 1:CH, ha:ha + 1], acs_col[CH - 1:CH, hb:hb + 1]
    out = dict(
        dtsel=jnp.where(lo, dt[:, ha:ha + 1], dt[:, hb:hb + 1]),
        d_a=jnp.exp(jnp.where(tril, col_a - row_a, NEG)), d_b=jnp.exp(jnp.where(tril, col_b - row_b, NEG)),
        esel=jnp.where(lo, jnp.exp(col_a), jnp.exp(col_b)),
        fsel=jnp.where(lo, jnp.exp(last_a - col_a), jnp.exp(last_b - col_b)),
        g_a=jnp.exp(last_a), g_b=jnp.exp(last_b))
    return out


def _ssd_fwd(xc, pp, dtg, dtt, arow, acol, dexp, ng, bl, s, *, name):
    nc = s // CH
    T = bl * s

    def body(xs_ref, bm_ref, cm_ref, z_ref, dt_ref, dtt_ref, arow_ref, acol_ref, dexp_ref, ng_ref,
             y_ref, yn_ref, prev_ref, st_ref):
        @pl.when(pl.program_id(2) == 0)
        def _():
            st_ref[...] = jnp.zeros_like(st_ref)

        dt = dt_ref[...]
        tril, triu, acs_col, acs_row = _ssd_common(dt, dtt_ref[...], -jnp.exp(arow_ref[...]), -jnp.exp(acol_ref[...]))
        bm, cm = bm_ref[...].astype(MXU), cm_ref[...].astype(MXU)
        cb = _nt(cm, bm)
        lo = lax.broadcasted_iota(jnp.int32, (CH, 128), 1) < HD
        sub_lo = lax.broadcasted_iota(jnp.int32, (128, NS), 0) < HD
        for q in range(4):
            t = _pair_terms(q, dt, acs_col, acs_row, tril, lo)
            x = xs_ref[:, 128 * q:128 * (q + 1)]
            xd = x * t["dtsel"]
            y = (_nn((cb * t["d_a"]).astype(MXU), jnp.where(lo, xd, 0.0).astype(MXU))
                 + _nn((cb * t["d_b"]).astype(MXU), jnp.where(lo, 0.0, xd).astype(MXU)))
            prev = st_ref[q]
            prev_ref[q] = prev
            y = y + t["esel"] * _nt(cm, prev.astype(MXU))
            st_ref[q] = prev * jnp.where(sub_lo, t["g_a"], t["g_b"]) + _tn((xd * t["fsel"]).astype(MXU), bm)
            y_ref[:, 128 * q:128 * (q + 1)] = y + dexp_ref[:, 128 * q:128 * (q + 1)] * x
        zv = z_ref[...]
        yg = y_ref[...] * (zv * _sig(zv))
        r = lax.rsqrt(jnp.mean(yg * yg, axis=-1, keepdims=True) + EPS)
        yn_ref[...] = (yg * r * ng_ref[...]).astype(BF16)

    def row(width, off_blocks):
        return pl.BlockSpec((CH, width), lambda g, b, c: (b * nc + c, off_blocks + g))

    return pl.pallas_call(
        body, grid=(NG, bl, nc),
        in_specs=[row(GW, 0), row(NS, DI // NS), row(NS, DI // NS + NG), row(GW, O_Z // GW), row(128, 0),
                  pl.BlockSpec((None, 8, CH), lambda g, b, c: (g, 0, b * nc + c)),
                  pl.BlockSpec((1, 128), lambda g, b, c: (0, g)),
                  pl.BlockSpec((None, 8, 1), lambda g, b, c: (g, 0, 0)),
                  pl.BlockSpec((1, GW), lambda g, b, c: (0, g)), pl.BlockSpec((1, GW), lambda g, b, c: (0, g))],
        out_specs=[row(GW, 0), row(GW, 0),
                   pl.BlockSpec((None, 4, 128, NS), lambda g, b, c: (b * nc + c, g, 0, 0))],
        out_shape=[jax.ShapeDtypeStruct((T, DI), F32), jax.ShapeDtypeStruct((T, DI), BF16),
                   jax.ShapeDtypeStruct((bl * nc, 16, 128, NS), F32)],
        scratch_shapes=[pltpu.VMEM((4, 128, NS), F32)],
        compiler_params=_cp("parallel", "parallel", "arbitrary"), name=name,
    )(xc, xc, xc, pp, dtg, dtt, arow, acol, dexp, ng)


def _ssd_bwd(dyn, y, xc, pp, dtg, dtt, arow, acol, dexp, ng, prev, dpp, bl, s, *, name):
    nc = s // CH
    T = bl * s

    def rsum(v):
        return jnp.sum(v, axis=1, keepdims=True)

    def asum(v):
        return jnp.sum(jnp.sum(v, axis=0, keepdims=True), axis=1, keepdims=True)

    def body(dyn_ref, y_ref, xs_ref, bm_ref, cm_ref, z_ref, dt_ref, dtt_ref, arow_ref, acol_ref, dexp_ref, ng_ref,
             prev_ref, dpp_in, dz_ref, dxs_ref, db_ref, dc_ref, ddt_ref, dng_ref, dd_ref, dal_ref, dst_ref):
        del dpp_in

        @pl.when((pl.program_id(1) == 0) & (pl.program_id(2) == 0))
        def _():
            dng_ref[...] = jnp.zeros_like(dng_ref)
            dd_ref[...] = jnp.zeros_like(dd_ref)
            dal_ref[...] = jnp.zeros_like(dal_ref)

        @pl.when(pl.program_id(2) == 0)
        def _():
            dst_ref[...] = jnp.zeros_like(dst_ref)

        yv, zv, xsv, dexp_v = y_ref[...], z_ref[...], xs_ref[...], dexp_ref[...]
        sz = _sig(zv)
        silu = zv * sz
        yg = yv * silu
        r = lax.rsqrt(jnp.mean(yg * yg, axis=-1, keepdims=True) + EPS)
        yh = yg * r
        dynv = dyn_ref[...]
        dng_ref[...] += jnp.sum(dynv * yh, axis=0, keepdims=True)
        dyh = dynv * ng_ref[...]
        dyg = r * (dyh - yh * jnp.mean(dyh * yh, axis=-1, keepdims=True))
        dz_ref[...] = (dyg * yv * (sz * (1.0 + zv * (1.0 - sz)))).astype(BF16)
        dy_all = dyg * silu
        dd_ref[...] += jnp.sum(dy_all * xsv, axis=0, keepdims=True)

        dt = dt_ref[...]
        arow_v = -jnp.exp(arow_ref[...])
        tril, triu, acs_col, acs_row = _ssd_common(dt, dtt_ref[...], arow_v, -jnp.exp(acol_ref[...]))
        bm, cm = bm_ref[...].astype(MXU), cm_ref[...].astype(MXU)
        cb = _nt(cm, bm)
        lane = lax.broadcasted_iota(jnp.int32, (CH, 128), 1)
        is_last = lax.broadcasted_iota(jnp.int32, (CH, 128), 0) == CH - 1
        lo = lane < HD
        sub_lo = lax.broadcasted_iota(jnp.int32, (128, NS), 0) < HD
        dcb = jnp.zeros((CH, CH), F32)
        dc_acc = jnp.zeros((CH, NS), F32)
        db_acc = jnp.zeros((CH, NS), F32)
        dacs = jnp.zeros((CH, 128), F32)
        ddtx = jnp.zeros((CH, 128), F32)
        csum = jnp.zeros((8, CH), F32)
        sub8 = lax.broadcasted_iota(jnp.int32, (8, CH), 0)
        for q in range(4):
            ha, hb = 2 * q, 2 * q + 1
            sl = slice(128 * q, 128 * (q + 1))
            t = _pair_terms(q, dt, acs_col, acs_row, tril, lo)
            x, dy = xsv[:, sl], dy_all[:, sl]
            xd = x * t["dtsel"]
            xd_m = xd.astype(MXU)
            dy_lo, dy_hi = jnp.where(lo, dy, 0.0).astype(MXU), jnp.where(lo, 0.0, dy).astype(MXU)
            m_a, m_b = cb * t["d_a"], cb * t["d_b"]
            prev_m = prev_ref[q].astype(MXU)
            dnext = dst_ref[q]
            dnext_m = dnext.astype(MXU)
            bds = _nt(bm, dnext_m)
            dxd = _tn(m_a.astype(MXU), dy_lo) + _tn(m_b.astype(MXU), dy_hi) + t["fsel"] * bds
            dye_m = (dy * t["esel"]).astype(MXU)
            dst_ref[q] = dnext * jnp.where(sub_lo, t["g_a"], t["g_b"]) + _tn(dye_m, cm)
            dm_a, dm_b = _nt(dy_lo, xd_m), _nt(dy_hi, xd_m)
            dcb = dcb + dm_a * t["d_a"] + dm_b * t["d_b"]
            g_a, g_b = dm_a * m_a, dm_b * m_b
            csum = (csum + jnp.where(sub8 == ha, jnp.sum(g_a, axis=0, keepdims=True), 0.0)
                    + jnp.where(sub8 == hb, jnp.sum(g_b, axis=0, keepdims=True), 0.0))
            tf = t["fsel"] * xd * bds
            tyf = dy * (t["esel"] * _nt(cm, prev_m)) - tf
            dpp = dnext * prev_ref[q]
            ea = asum(jnp.where(lo, tf, 0.0)) + t["g_a"] * asum(jnp.where(sub_lo, dpp, 0.0))
            eb = asum(jnp.where(lo, 0.0, tf)) + t["g_b"] * asum(jnp.where(sub_lo, 0.0, dpp))
            ra = rsum(g_a + jnp.where(lo, tyf, 0.0)) + jnp.where(is_last, ea, 0.0)
            rb = rsum(g_b + jnp.where(lo, 0.0, tyf)) + jnp.where(is_last, eb, 0.0)
            dacs = dacs + jnp.where(lane == ha, ra, 0.0) + jnp.where(lane == hb, rb, 0.0)
            tx = dxd * x
            ddtx = (ddtx + jnp.where(lane == ha, rsum(jnp.where(lo, tx, 0.0)), 0.0)
                    + jnp.where(lane == hb, rsum(jnp.where(lo, 0.0, tx)), 0.0))
            dxs_ref[:, sl] = dxd * t["dtsel"] + dexp_v[:, sl] * dy
            dc_acc = dc_acc + _nn(dye_m, prev_m)
            db_acc = db_acc + _nn((xd * t["fsel"]).astype(MXU), dnext_m)
        dcb_m = dcb.astype(MXU)
        dc_ref[...] = dc_acc + _nn(dcb_m, bm)
        db_ref[...] = db_acc + _tn(dcb_m, cm)
        dacs = dacs - jnp.concatenate([csum, jnp.zeros((CH - 8, CH), F32)], axis=0).T
        dla = _nn(triu.astype(F32), dacs, HI)
        ddt_ref[...] = dla * arow_v + ddtx
        dal_ref[...] += jnp.sum(dla * dt, axis=0, keepdims=True) * arow_v

    def row(width, off_blocks):
        return pl.BlockSpec((CH, width), lambda g, b, c: (b * nc + nc - 1 - c, off_blocks + g))

    gvec = pl.BlockSpec((1, GW), lambda g, b, c: (0, g))
    hvec = pl.BlockSpec((1, 128), lambda g, b, c: (0, g))
    return pl.pallas_call(
        body, grid=(NG, bl, nc),
        in_specs=[row(GW, 0), row(GW, 0), row(GW, 0), row(NS, DI // NS), row(NS, DI // NS + NG), row(GW, O_Z // GW),
                  row(128, 0), pl.BlockSpec((None, 8, CH), lambda g, b, c: (g, 0, b * nc + nc - 1 - c)),
                  hvec, pl.BlockSpec((None, 8, 1), lambda g, b, c: (g, 0, 0)), gvec, gvec,
                  pl.BlockSpec((None, 4, 128, NS), lambda g, b, c: (b * nc + nc - 1 - c, g, 0, 0)),
                  pl.BlockSpec(memory_space=pl.ANY)],
        out_specs=[row(GW, O_Z // GW), row(GW, 0), row(NS, 0), row(NS, 0), row(128, 0), gvec, gvec, hvec],
        input_output_aliases={13: 0},
        out_shape=[jax.ShapeDtypeStruct(dpp.shape, dpp.dtype), jax.ShapeDtypeStruct((T, DI), F32),
                   jax.ShapeDtypeStruct((T, NG * NS), F32), jax.ShapeDtypeStruct((T, NG * NS), F32),
                   jax.ShapeDtypeStruct((T, NG * 128), F32), jax.ShapeDtypeStruct((1, DI), F32),
                   jax.ShapeDtypeStruct((1, DI), F32), jax.ShapeDtypeStruct((1, NG * 128), F32)],
        scratch_shapes=[pltpu.VMEM((4, 128, NS), F32)],
        compiler_params=_cp("arbitrary", "arbitrary", "arbitrary"), name=name,
    )(dyn, y, xc, xc, xc, pp, dtg, dtt, arow, acol, dexp, ng, prev, dpp)


def _merge_fwd(pp, ya, yb, *, name):
    T = ya.shape[0]
    tr = _tile(T, 512, 8)

    def body(ga_ref, gb_ref, ya_ref, yb_ref, o_ref):
        o_ref[...] = (_sig(ga_ref[...]) * ya_ref[...] + _sig(gb_ref[...]) * yb_ref[...]).astype(BF16)

    row = pl.BlockSpec((tr, D), lambda i: (i, 0))
    return pl.pallas_call(
        body, grid=(T // tr,),
        in_specs=[pl.BlockSpec((tr, D), lambda i: (i, O_GA // D)), pl.BlockSpec((tr, D), lambda i: (i, O_GB // D)),
                  row, row],
        out_specs=row, out_shape=jax.ShapeDtypeStruct((T, D), BF16), compiler_params=_cp("parallel"),
        name=name)(pp, pp, ya, yb)


def _merge_bwd(pp, ya, yb, dm, *, name):
    T = ya.shape[0]
    tr = _tile(T, 512, 8)
    assert O_GB == O_GA + D and O_GA % (2 * D) == 0

    def body(g_ref, ya_ref, yb_ref, dm_ref, dya_ref, dyb_ref, dg_ref):
        sa, sb, dmv = _sig(g_ref[:, :D]), _sig(g_ref[:, D:]), dm_ref[...]
        dya_ref[...] = (dmv * sa).astype(BF16)
        dyb_ref[...] = (dmv * sb).astype(BF16)
        dg_ref[:, :D] = (dmv * ya_ref[...] * (sa * (1.0 - sa))).astype(BF16)
        dg_ref[:, D:] = (dmv * yb_ref[...] * (sb * (1.0 - sb))).astype(BF16)

    row = pl.BlockSpec((tr, D), lambda i: (i, 0))
    gates = pl.BlockSpec((tr, 2 * D), lambda i: (i, O_GA // (2 * D)))
    act = jax.ShapeDtypeStruct((T, D), BF16)
    return pl.pallas_call(
        body, grid=(T // tr,), in_specs=[gates, row, row, row], out_specs=[row, row, gates],
        out_shape=[act, act, jax.ShapeDtypeStruct((T, NPP), BF16)], compiler_params=_cp("parallel"),
        name=name)(pp, ya, yb, dm)


def _softmax_rows(sc):
    e = jnp.exp(sc - jnp.max(sc, axis=-1, keepdims=True))
    return e / jnp.sum(e, axis=-1, keepdims=True)


def _attn_fwd(q, kv, bl, s, *, name):
    m = kv.shape[0] // bl
    tq = _tile(s, 512)
    nq = s // tq
    scale = 1.0 / math.sqrt(XD)

    def body(q_ref, k_ref, v_ref, o_ref):
        p = _softmax_rows(_nt(q_ref[...], k_ref[...]) * scale)
        o_ref[...] = _nn(p.astype(MXU), v_ref[...]).astype(BF16)

    qspec = pl.BlockSpec((tq, XD), lambda b, h, i: (b * nq + i, h))
    return pl.pallas_call(
        body, grid=(bl, XH, nq),
        in_specs=[qspec, pl.BlockSpec((m, XD), lambda b, h, i: (b, h)),
                  pl.BlockSpec((m, XD), lambda b, h, i: (b, XH + h))],
        out_specs=qspec, out_shape=jax.ShapeDtypeStruct((bl * s, D), BF16),
        compiler_params=_cp("parallel", "parallel", "parallel"), name=name)(q, kv, kv)


def _attn_bwd(q, kv, do, bl, s, *, name):
    m = kv.shape[0] // bl
    tq = _tile(s, 512)
    nq = s // tq
    scale = 1.0 / math.sqrt(XD)

    def body(q_ref, k_ref, v_ref, do_ref, dq_ref, dk_ref, dv_ref):
        @pl.when(pl.program_id(2) == 0)
        def _():
            dk_ref[...] = jnp.zeros_like(dk_ref)
            dv_ref[...] = jnp.zeros_like(dv_ref)

        qv, kvv, vv, dov = q_ref[...], k_ref[...], v_ref[...], do_ref[...]
        p = _softmax_rows(_nt(qv, kvv) * scale)
        dp = _nt(dov, vv)
        ds = (p * (dp - jnp.sum(dp * p, axis=-1, keepdims=True)) * scale).astype(MXU)
        dq_ref[...] = _nn(ds, kvv).astype(BF16)
        dk_ref[...] += _tn(ds, qv)
        dv_ref[...] += _tn(p.astype(MXU), dov)

    qspec = pl.BlockSpec((tq, XD), lambda b, h, i: (b * nq + i, h))
    kspec = pl.BlockSpec((m, XD), lambda b, h, i: (b, h))
    return pl.pallas_call(
        body, grid=(bl, XH, nq),
        in_specs=[qspec, kspec, pl.BlockSpec((m, XD), lambda b, h, i: (b, XH + h)), qspec],
        out_specs=[qspec, kspec, kspec],
        out_shape=[jax.ShapeDtypeStruct((bl * s, D), BF16), jax.ShapeDtypeStruct((bl * m, D), F32),
                   jax.ShapeDtypeStruct((bl * m, D), F32)],
        compiler_params=_cp("parallel", "parallel", "arbitrary"), name=name)(q, kv, kv, do)


def _row_tile(r, c, max_elems=512 * 1024, align=16):
    best = None
    for t in range(align, r + 1, align):
        if r % t == 0 and t * c <= max_elems:
            best = t
    return best if best is not None else r


def _addn(a, others, *, name, also_bf16=False):
    r, c = a.shape
    tr = _row_tile(r, c)
    n = len(others)

    def body(*refs):
        acc = refs[0][...].astype(F32)
        for o_ref in refs[1:1 + n]:
            acc = acc + o_ref[...].astype(F32)
        refs[1 + n][...] = acc
        if also_bf16:
            refs[2 + n][...] = acc.astype(BF16)

    spec = pl.BlockSpec((tr, c), lambda i: (i, 0))
    shapes = [jax.ShapeDtypeStruct((r, c), F32)] + ([jax.ShapeDtypeStruct((r, c), BF16)] if also_bf16 else [])
    out = pl.pallas_call(
        body, grid=(r // tr,), in_specs=[spec] * (1 + n), out_specs=[spec] * len(shapes), out_shape=shapes,
        compiler_params=_cp("parallel"), name=name)(a, *others)
    return out if also_bf16 else out[0]


def _sum_leading(a, *, name):
    n, r, c = a.shape

    def body(a_ref, o_ref):
        acc = a_ref[0]
        for i in range(1, n):
            acc = acc + a_ref[i]
        o_ref[...] = acc

    return pl.pallas_call(body, out_shape=jax.ShapeDtypeStruct((r, c), F32), name=name)(a)


def _adamw_math(wv, gv, mv, vv):
    m2 = ADAM_B1 * mv + (1.0 - ADAM_B1) * gv
    v2 = ADAM_B2 * vv + (1.0 - ADAM_B2) * (gv * gv)
    m_hat = m2 / (1.0 - ADAM_B1 ** ADAM_STEP)
    v_hat = v2 / (1.0 - ADAM_B2 ** ADAM_STEP)
    return -ADAM_LR * (m_hat / (jnp.sqrt(v_hat) + ADAM_EPS) + ADAM_WD * wv), m2, v2


def _adamw(w, g, m, v, *, name):
    r, c = w.shape
    tr = _row_tile(r, c, align=8)

    def body(w_ref, g_ref, m_ref, v_ref, d_ref, mo_ref, vo_ref):
        d_ref[...], mo_ref[...], vo_ref[...] = _adamw_math(w_ref[...], g_ref[...], m_ref[...], v_ref[...])

    spec = pl.BlockSpec((tr, c), lambda i: (i, 0))
    shp = jax.ShapeDtypeStruct((r, c), F32)
    return pl.pallas_call(
        body, grid=(r // tr,), in_specs=[spec] * 4, out_specs=[spec] * 3, out_shape=[shp] * 3,
        compiler_params=_cp("parallel"), name=name)(w, g, m, v)


def _adamw_halves(w, g_mine, g_other, m, v, c, *, name):
    r, cols = w.shape
    h = r // 2
    tr = _row_tile(h, cols, align=8)
    nh = h // tr

    def body(c_ref, w_ref, gm_ref, go_ref, m_ref, v_ref, g_ref, d_ref, mo_ref, vo_ref):
        gv = jnp.where(pl.program_id(0) // nh == c_ref[0], gm_ref[...], go_ref[...])
        g_ref[...] = gv
        d_ref[...], mo_ref[...], vo_ref[...] = _adamw_math(w_ref[...], gv, m_ref[...], v_ref[...])

    full = pl.BlockSpec((tr, cols), lambda i, c_: (i, 0))
    half = pl.BlockSpec((tr, cols), lambda i, c_: (i % nh, 0))
    shp = jax.ShapeDtypeStruct((r, cols), F32)
    return pl.pallas_call(
        body,
        grid_spec=pltpu.PrefetchScalarGridSpec(num_scalar_prefetch=1, grid=(2 * nh,),
                                               in_specs=[full, half, half, full, full], out_specs=[full] * 4),
        out_shape=[shp] * 4, compiler_params=_cp("parallel"), name=name,
    )(jnp.reshape(c, (1,)).astype(jnp.int32), w, g_mine, g_other, m, v)


def _flip(i, d):
    return 1 - i if d else i


def _comm(name, ins, out_shapes, n_remote, n_local, plan, aliases=None):
    n_in, n_out = len(ins), len(out_shapes)

    def body(*refs):
        in_refs, out_refs = refs[:n_in], refs[n_in:n_in + n_out]
        send_sems, recv_sems = refs[n_in + n_out], refs[n_in + n_out + 1]
        x, y, c = lax.axis_index("x"), lax.axis_index("y"), lax.axis_index("c")
        remote, local = plan(in_refs, out_refs, x, y, c)
        assert len(remote) == n_remote and len(local) == n_local
        copies = []
        if n_local:
            loc_sems = refs[n_in + n_out + 2]
            copies += [pltpu.make_async_copy(s_, d_, loc_sems.at[i]) for i, (s_, d_) in enumerate(local)]
        copies += [pltpu.make_async_remote_copy(src_ref=s_, dst_ref=d_, send_sem=send_sems.at[i],
                                                recv_sem=recv_sems.at[i], device_id=dev, device_id_type=MESH)
                   for i, (s_, d_, dev) in enumerate(remote)]
        for cp in copies:
            cp.start()
        for cp in copies:
            cp.wait()

    hbm = pl.BlockSpec(memory_space=pl.ANY)
    scratch = [pltpu.SemaphoreType.DMA((n_remote,)), pltpu.SemaphoreType.DMA((n_remote,))]
    if n_local:
        scratch.append(pltpu.SemaphoreType.DMA((n_local,)))
    return pl.pallas_call(
        body, in_specs=[hbm] * n_in, out_specs=[hbm] * n_out, out_shape=out_shapes, scratch_shapes=scratch,
        input_output_aliases=aliases or {}, compiler_params=pltpu.CompilerParams(has_side_effects=True),
        name=name)(*ins)


HBM_SPEC = pl.BlockSpec(memory_space=pltpu.HBM)
SEM_SPEC = pl.BlockSpec(memory_space=pltpu.SEMAPHORE)
DATAFLOW = pltpu.SideEffectType.DATAFLOW_SIDE_EFFECTING


def _remote_copies(plan, srcs, lands, send_sems, recv_sems, n_copies):
    x, y, c = lax.axis_index("x"), lax.axis_index("y"), lax.axis_index("c")
    copies = plan(srcs, lands, x, y, c)
    assert len(copies) == n_copies
    return [pltpu.make_async_remote_copy(src_ref=s_, dst_ref=d_, send_sem=send_sems.at[i], recv_sem=recv_sems.at[i],
                                         device_id=dev, device_id_type=MESH) for i, (s_, d_, dev) in enumerate(copies)]


def _split_start(name, srcs, lands, n_copies, plan, after=None):
    ns, nb = len(srcs), len(srcs) + len(lands)
    n_after = 0 if after is None else 1
    n_in = nb + n_after

    def body(*refs):
        for cp in _remote_copies(plan, refs[:ns], refs[ns:nb], refs[n_in], refs[n_in + 1], n_copies):
            cp.start()
        refs[-1][...] = jnp.zeros_like(refs[-1])

    arrays = [pltpu.with_memory_space_constraint(a_, pltpu.HBM) for a_ in list(srcs) + list(lands)]
    out = pl.pallas_call(
        body, name=name,
        out_shape=(pltpu.SemaphoreType.DMA((n_copies,)), pltpu.SemaphoreType.DMA((n_copies,)),
                   *[pltpu.HBM(a_.shape, a_.dtype) for a_ in arrays], jax.ShapeDtypeStruct((8, 128), F32)),
        in_specs=[HBM_SPEC] * nb + [pl.BlockSpec(memory_space=pl.ANY)] * n_after,
        out_specs=(SEM_SPEC, SEM_SPEC, *[HBM_SPEC] * nb, pl.BlockSpec(memory_space=pltpu.VMEM)),
        input_output_aliases={i: 2 + i for i in range(nb)},
        compiler_params=pltpu.CompilerParams(has_side_effects=DATAFLOW))(*arrays, *([after] * n_after))
    return (out[0], out[1], list(out[2:2 + nb])), out[-1]


def _split_wait(name, handle, ns, n_copies, plan, after):
    send_sems, recv_sems, bufs = handle
    nb = len(bufs)

    def body(*refs):
        for cp in _remote_copies(plan, refs[:ns], refs[ns:nb], refs[nb], refs[nb + 1], n_copies):
            cp.wait_send()
            cp.wait_recv()

    out = pl.pallas_call(
        body, name=name, out_shape=[pltpu.HBM(b_.shape, b_.dtype) for b_ in bufs],
        in_specs=[HBM_SPEC] * nb + [SEM_SPEC, SEM_SPEC, pl.BlockSpec(memory_space=pl.ANY)],
        out_specs=[HBM_SPEC] * nb, input_output_aliases={i: i for i in range(nb)},
        compiler_params=pltpu.CompilerParams(has_side_effects=DATAFLOW))(*bufs, send_sems, recv_sems, after)
    return list(out[ns:])


def _gather_start(shards, tag, after=None):
    n = len(shards)
    lands = [lax.empty((4,) + s.shape, s.dtype) for s in shards]

    def plan(srcs, dsts, x, y, c):
        k = 2 * x + y
        copies = []
        for w_ref, o_ref in zip(srcs, dsts):
            h = w_ref.shape[0] // 2
            rows = pl.ds(c * h, h)
            copies += [(w_ref.at[rows], o_ref.at[k, rows], (_flip(x, dx), _flip(y, dy), c)) for dx, dy in CHIP_FLIPS]
        return copies

    handle, token = _split_start(f"gather_{tag}_start", shards, lands, 3 * n, plan, after)
    return (handle, plan, n), token


def _gather_wait(started, after, tag):
    handle, plan, n = started
    return _split_wait(f"gather_{tag}_wait", handle, n, 3 * n, plan, after)


def _gather_d2d(lands, before, tag):
    n = len(lands)

    def plan_d2d(in_refs, out_refs, x, y, c):
        remote = []
        for o_ref in out_refs:
            h = o_ref.shape[1] // 2
            for dx, dy in CHIP_FLIPS:
                half = o_ref.at[2 * _flip(x, dx) + _flip(y, dy), pl.ds(c * h, h)]
                remote.append((half, half, (x, y, 1 - c)))
        return remote, []

    return _comm(f"gather_{tag}_d2d", list(lands) + list(before),
                 [jax.ShapeDtypeStruct(l_.shape, l_.dtype) for l_ in lands], 3 * n, 0, plan_d2d,
                 aliases={i: i for i in range(n)})


def _pair_plan(in_refs, out_refs, x, y, c):
    return [(i_, o_, (x, y, 1 - c)) for i_, o_ in zip(in_refs, out_refs)], []


def _rs_start(grads, tag):
    n = len(grads)
    c = lax.axis_index("c")
    halves = [g.shape[1] // 2 for g in grads]
    mine = [lax.dynamic_slice_in_dim(g, c * h, h, axis=1) for g, h in zip(grads, halves)]
    send_a = [lax.dynamic_slice_in_dim(g, (1 - c) * h, h, axis=1).astype(BF16) for g, h in zip(grads, halves)]
    recv_a = _comm(f"rs_pair_{tag}", send_a, [jax.ShapeDtypeStruct(s.shape, BF16) for s in send_a], n, 0, _pair_plan)
    pair, pair_b = [], []
    for i, (mi, ra) in enumerate(zip(mine, recv_a)):
        four, h, cols = mi.shape
        p32, p16 = _addn(mi.reshape(four * h, cols), [ra.reshape(four * h, cols)], name=f"rs_pair_sum_{tag}_{i}",
                         also_bf16=True)
        pair.append(p32.reshape(four, h, cols))
        pair_b.append(p16.reshape(four, h, cols))

    def plan(srcs, dsts, x, y, c_):
        copies = []
        for i_, o_ in zip(srcs, dsts):
            for j, (dx, dy) in enumerate(CHIP_FLIPS):
                fx, fy = _flip(x, dx), _flip(y, dy)
                copies.append((i_.at[2 * fx + fy], o_.at[j], (fx, fy, c_)))
        return copies

    lands = [lax.empty((3,) + p.shape[1:], BF16) for p in pair_b]
    handle, token = _split_start(f"rs_chips_{tag}_start", pair_b, lands, 3 * n, plan)
    return (handle, plan, n, pair), token


def _rs_finish(started, after, tag):
    handle, plan, n, pair = started
    recv_b = _split_wait(f"rs_chips_{tag}_wait", handle, n, 3 * n, plan, after)
    k = 2 * lax.axis_index("x") + lax.axis_index("y")
    tot = [_addn(lax.dynamic_index_in_dim(p, k, 0, keepdims=False), [rb[0], rb[1], rb[2]],
                 name=f"rs_chip_sum_{tag}_{i}") for i, (p, rb) in enumerate(zip(pair, recv_b))]
    other = _comm(f"rs_halves_{tag}", tot, [jax.ShapeDtypeStruct(t.shape, F32) for t in tot], n, 0, _pair_plan)
    return tot, other


def _gather_all(vec, *, name):
    out = jax.ShapeDtypeStruct((8,) + vec.shape, vec.dtype)

    def plan(in_refs, out_refs, x, y, c):
        me = 4 * x + 2 * y + c
        remote = [(in_refs[0], out_refs[0].at[me], (_flip(x, dx), _flip(y, dy), _flip(c, dc)))
                  for dx in (0, 1) for dy in (0, 1) for dc in (0, 1) if (dx, dy, dc) != (0, 0, 0)]
        return remote, [(in_refs[0], out_refs[0].at[me])]

    return _comm(name, [vec], [out], 7, 1, plan)[0]


def _pack(parts):
    flat = [p.reshape(-1).astype(F32) for p in parts]
    total = sum(f.shape[0] for f in flat)
    n = -(-total // 1024) * 128
    vec = jnp.concatenate(flat + [jnp.zeros((8 * n - total,), F32)]).reshape(8, n)
    offs, o = [], 0
    for f in flat:
        offs.append((o, f.shape[0]))
        o += f.shape[0]
    return vec, offs


def _unpack(vec, offs, shapes):
    flat = vec.reshape(-1)
    return [flat[o:o + n].reshape(s) for (o, n), s in zip(offs, shapes)]


BIG = (("ffn1_w_gate_up", "col"), ("ffn1_w_down", "row"), ("w_in", "col"), ("w_out_a", "row"), ("w_out_ssm", "row"),
       ("w_mix_out", "row"), ("w_q", "row"), ("w_kv", "col"), ("w_o_x", "row"), ("ffn2_w_gate_up", "col"),
       ("ffn2_w_down", "row"))
SMALL = ("ffn1_norm", "mix_norm", "conv_a_w", "ssm_conv_w", "ssm_conv_b", "ssm_dt_bias", "ssm_a_log", "ssm_d",
         "ssm_norm", "xattn_norm", "mem_norm", "ffn2_norm", "final_norm")
WEIGHTS = ("ffn1_norm", "ffn1_w_gate_up", "ffn1_w_down", "mix_norm", "w_in", "conv_a_w", "w_out_a", "ssm_conv_w",
           "ssm_conv_b", "ssm_dt_bias", "ssm_a_log", "ssm_d", "ssm_norm", "w_out_ssm", "w_mix_out", "xattn_norm",
           "mem_norm", "w_q", "w_kv", "w_o_x", "ffn2_norm", "ffn2_w_gate_up", "ffn2_w_down", "final_norm")


GATHER_GROUPS = (("a", ("ffn1_w_gate_up",)), ("b", ("ffn1_w_down", "w_in")),
                 ("c", ("w_out_a", "w_out_ssm", "w_mix_out", "w_q", "w_kv", "w_o_x", "ffn2_w_gate_up", "ffn2_w_down")))


def _full_weight(land, own, kind, k):
    parts = [jnp.where(k == kk, own, land[kk]) for kk in range(4)]
    return jnp.concatenate(parts, axis=0 if kind == "row" else 1)


class _GatheredWeights:
    def __init__(self, shards, k):
        self.shards, self.k = shards, k
        self.full = {}
        self.n_done = 0
        self.started, self.after = self._start(0, None)

    def _start(self, gi, after):
        tag, names = GATHER_GROUPS[gi]
        return _gather_start([self.shards[n] for n in names], tag, after)

    def mark(self, value):
        self.after = value

    def __getitem__(self, name):
        if name not in self.full:
            tag, names = GATHER_GROUPS[self.n_done]
            assert name in names, (name, tag)
            lands = _gather_wait(self.started, self.after, tag)
            before = []
            if self.n_done + 1 < len(GATHER_GROUPS):
                self.started, token = self._start(self.n_done + 1, lands[0])
                before = [token]
            lands = _gather_d2d(lands, before, tag)
            for n, land in zip(names, lands):
                w = _full_weight(land, self.shards[n], dict(BIG)[n], self.k)
                self.full[n] = _pad_w_in(w) if n == "w_in" else w
            self.n_done += 1
        return self.full[name]


def _shard_major(dw, kind):
    if isinstance(dw, tuple):
        return jnp.concatenate([_shard_major(p, "col2") for p in dw], axis=0)
    if kind == "col2":
        return jnp.transpose(dw.reshape(dw.shape[0], 2, dw.shape[1] // 2), (1, 0, 2))
    if kind == "row":
        return dw.reshape(4, dw.shape[0] // 4, dw.shape[1])
    return jnp.transpose(dw.reshape(dw.shape[0], 4, dw.shape[1] // 4), (1, 0, 2))


def _pad_rows8(w):
    return jnp.concatenate([w, jnp.zeros((8 - w.shape[0], w.shape[1]), w.dtype)], axis=0)


def _group_lanes(v):
    r = v.shape[0]
    return jnp.pad(v.reshape(r, NG, NH // NG), ((0, 0), (0, 0), (0, 128 - NH // NG))).reshape(r, NG * 128)


def _ungroup_lanes(v):
    r = v.shape[0]
    return v.reshape(r, NG, 128)[:, :, :NH // NG].reshape(r, NH)


def _local_step(wfull, small, x, mem, target, token=0.0, on_grads=None):
    bl, s, _ = x.shape
    T = bl * s
    x2, t2 = x.reshape(T, D), target.reshape(T, D)
    mem2 = mem.reshape(-1, D)
    g = {}
    tok = [token]
    mark = getattr(wfull, "mark", lambda value: None)

    def gain(name):
        return small[name].reshape(1, -1) + tok[0]

    def emit(tag, names):
        if on_grads is not None:
            tok[0] = tok[0] + on_grads(tag, {n: g[n] for n in names})

    def ffn_fwd(h, norm, wgu, wd, tag):
        n = _norm_fwd(h, gain(norm), name=f"{tag}_norm")
        gate, up, a = _gate_up_fwd(n, wfull[wgu], name=f"{tag}_gate_up")
        mark(a)
        out = _mm(a, wfull[wd], "nn", tk=DFF, scale=FFN_RES, residual=h, name=f"{tag}_down")
        return out, (n, gate, up, a)

    def ffn_bwd(dh, h, norm, wgu, wd, saved, tag):
        n, gate, up, a = saved
        dgate, dup = _act_bwd(dh, wfull[wd], gate, up, FFN_RES, name=f"{tag}_d_act")
        g[wd] = _mm(a, dh, "tn", tm=1408, scale=FFN_RES, name=f"{tag}_d_w_down")
        g[wgu] = (_mm(n, dgate, "tn", tn=1408, name=f"{tag}_d_w_gate"), _mm(n, dup, "tn", tn=1408, name=f"{tag}_d_w_up"))
        emit(tag, (wgu, wd))
        dn = _mm(dgate, wfull[wgu], "nt", a2=dup, tk=1408, name=f"{tag}_d_norm_out")
        dh_in, g[norm] = _norm_bwd(h, gain(norm), dn, dh, name=f"{tag}_d_norm")
        return dh_in

    h1, ffn1_saved = ffn_fwd(x2, "ffn1_norm", "ffn1_w_gate_up", "ffn1_w_down", "ffn1")
    mark(h1)
    u = _norm_fwd(h1, gain("mix_norm"), name="mix_norm")
    pp = _mm(u, wfull["w_in"], "nn", tn=1152, name="in_proj")
    wa8 = _pad_rows8(small["conv_a_w"])
    ws8 = _pad_rows8(small["ssm_conv_w"])
    conv_b = gain("ssm_conv_b")
    bias128 = jnp.pad(gain("ssm_dt_bias"), ((0, 0), (0, 128 - NH)))
    ya_pre = _conv_a_fwd(pp, wa8, bl, s, name="conv_a")
    xc = _conv_ssm_fwd(pp, ws8, conv_b, bl, s, name="conv_ssm")
    mark(xc)
    dt = _dt_fwd(pp, bias128, name="dt")
    dtg = _group_lanes(dt[:, :NH])
    dtt = dt[:, :NH].T.reshape(NG, NH // NG, T)
    alog = gain("ssm_a_log")
    arow, acol = _group_lanes(alog), alog.reshape(NG, NH // NG, 1)
    dexp = jnp.repeat(gain("ssm_d"), HD, axis=1)
    ng = gain("ssm_norm")
    y, yn, prev = _ssd_fwd(xc, pp, dtg, dtt, arow, acol, dexp, ng, bl, s, name="ssd")
    ya = _mm(ya_pre, wfull["w_out_a"], "nn", tn=1024, name="out_a")
    yb = _mm(yn, wfull["w_out_ssm"], "nn", tn=1024, tk=DI, name="out_ssm")
    merged = _merge_fwd(pp, ya, yb, name="merge")
    h2 = _mm(merged, wfull["w_mix_out"], "nn", tn=1024, residual=h1, name="mix_out")
    un = _norm_fwd(h2, gain("xattn_norm"), name="xattn_norm")
    q = _mm(un, wfull["w_q"], "nn", tn=1024, out_dtype=BF16, name="q_proj")
    mn = _norm_fwd(mem2, gain("mem_norm"), name="mem_norm")
    kv = _mm(mn, wfull["w_kv"], "nn", tn=1024, out_dtype=BF16, name="kv_proj")
    o = _attn_fwd(q, kv, bl, s, name="attn")
    h3 = _mm(o, wfull["w_o_x"], "nn", tn=1024, residual=h2, name="attn_out")
    h4, ffn2_saved = ffn_fwd(h3, "ffn2_norm", "ffn2_w_gate_up", "ffn2_w_down", "ffn2")
    sq_err, dh4, dgf = _final_loss(h4, gain("final_norm"), t2, name="final_loss")
    g["final_norm"] = dgf

    dh3 = ffn_bwd(dh4, h3, "ffn2_norm", "ffn2_w_gate_up", "ffn2_w_down", ffn2_saved, "ffn2")
    do = _mm(dh3, wfull["w_o_x"], "nt", tn=1024, out_dtype=BF16, name="d_attn_o")
    g["w_o_x"] = _mm(o, dh3, "tn",name="d_w_o_x")
    dq, dk, dv = _attn_bwd(q, kv, do, bl, s, name="d_attn")
    dun = _mm(dq, wfull["w_q"], "nt", tn=1024, name="d_xattn_norm_out")
    g["w_q"] = _mm(un, dq, "tn",name="d_w_q")
    dkv = jnp.concatenate([dk, dv], axis=1)
    dmn = _mm(dkv, wfull["w_kv"], "nt", tn=1024, tk=2 * D, name="d_mem_norm_out")
    g["w_kv"] = _mm(mn, dkv, "tn",name="d_w_kv")
    emit("attn", ("w_q", "w_kv", "w_o_x"))
    _, g["mem_norm"] = _norm_bwd(mem2, gain("mem_norm"), dmn, None, name="d_mem_norm")
    dh2, g["xattn_norm"] = _norm_bwd(h2, gain("xattn_norm"), dun, dh3, name="d_xattn_norm")
    dmerged = _mm(dh2, wfull["w_mix_out"], "nt", tn=1024, name="d_merged")
    g["w_mix_out"] = _mm(merged, dh2, "tn",name="d_w_mix_out")
    dya, dyb, dpp = _merge_bwd(pp, ya, yb, dmerged, name="d_merge")
    dya_pre = _mm(dya, wfull["w_out_a"], "nt", tn=1024, name="d_conv_a_out")
    g["w_out_a"] = _mm(ya_pre, dya, "tn",name="d_w_out_a")
    dyn = _mm(dyb, wfull["w_out_ssm"], "nt", tn=DI, name="d_ssd_out")
    g["w_out_ssm"] = _mm(yn, dyb, "tn",name="d_w_out_ssm")
    dpp, dwa8 = _conv_a_bwd(pp, wa8, dya_pre, dpp, bl, s, name="d_conv_a")
    g["conv_a_w"] = dwa8[:3]
    dpp, dxs, dbm, dcm, ddtg, g["ssm_norm"], ddexp, dalg = _ssd_bwd(
        dyn, y, xc, pp, dtg, dtt, arow, acol, dexp, ng, prev, dpp, bl, s, name="d_ssd")
    g["ssm_d"] = ddexp.reshape(NH, HD).sum(axis=1).reshape(1, NH)
    g["ssm_a_log"] = _ungroup_lanes(dalg)
    conv_dw, conv_db = [], []
    for dpart, off, tag in ((dxs, 0, "x"), (dbm, DI, "b"), (dcm, DI + NG * NS, "c")):
        dpp, dw_, db_ = _conv_ssm_bwd(pp, ws8, conv_b, dpart, off, dpp, bl, s, name=f"d_conv_ssm_{tag}")
        conv_dw.append(dw_)
        conv_db.append(db_)
    g["ssm_conv_w"] = jnp.concatenate(conv_dw, axis=1)[:4]
    g["ssm_conv_b"] = jnp.concatenate(conv_db, axis=1)
    ddt = jnp.pad(_ungroup_lanes(ddtg), ((0, 0), (0, 128 - NH)))
    dpp, dbias = _dt_bwd(pp, bias128, ddt, dpp, name="d_dt")
    g["ssm_dt_bias"] = dbias[:, :NH]
    g["w_in"] = _mm(u, dpp, "tn", tn=1152, name="d_w_in")
    emit("mix", ("w_in", "w_out_a", "w_out_ssm", "w_mix_out"))
    du = _mm(dpp, wfull["w_in"], "nt", tk=3456, name="d_mix_norm_out")
    dh1, g["mix_norm"] = _norm_bwd(h1, gain("mix_norm"), du, dh2, name="d_mix_norm")
    dx = ffn_bwd(dh1, x2, "ffn1_norm", "ffn1_w_gate_up", "ffn1_w_down", ffn1_saved, "ffn1")
    return sq_err, dx, g


def _pad_w_in(w):
    r = w.shape[0]
    conv_a = jnp.transpose(w[:, :3 * D].reshape(r, 3, D // CA_TILE, CA_TILE), (0, 2, 1, 3)).reshape(r, 3 * D)
    return jnp.concatenate([conv_a, w[:, 3 * D:O_GA], w[:, O_GA + NH:], w[:, O_GA:O_GA + NH],
                            jnp.zeros((r, NPP - NIN), w.dtype)], axis=1)


def _unpad_w_in(w):
    r = w.shape[0]
    conv_a = jnp.transpose(w[:, :3 * D].reshape(r, D // CA_TILE, 3, CA_TILE), (0, 2, 1, 3)).reshape(r, 3 * D)
    return jnp.concatenate([conv_a, w[:, 3 * D:O_GA], w[:, O_DT:O_DT + NH], w[:, O_GA:O_DT]], axis=1)


def kernel(x, mem, ffn1_norm, ffn1_w_gate_up, ffn1_w_down, mix_norm, w_in, conv_a_w, w_out_a, ssm_conv_w, ssm_conv_b, ssm_dt_bias, ssm_a_log, ssm_d, ssm_norm, w_out_ssm, w_mix_out, xattn_norm, mem_norm, w_q, w_kv, w_o_x, ffn2_norm, ffn2_w_gate_up, ffn2_w_down, final_norm, loss_target, m_ffn1_norm, m_ffn1_w_gate_up, m_ffn1_w_down, m_mix_norm, m_w_in, m_conv_a_w, m_w_out_a, m_ssm_conv_w, m_ssm_conv_b, m_ssm_dt_bias, m_ssm_a_log, m_ssm_d, m_ssm_norm, m_w_out_ssm, m_w_mix_out, m_xattn_norm, m_mem_norm, m_w_q, m_w_kv, m_w_o_x, m_ffn2_norm, m_ffn2_w_gate_up, m_ffn2_w_down, m_final_norm, v_ffn1_norm, v_ffn1_w_gate_up, v_ffn1_w_down, v_mix_norm, v_w_in, v_conv_a_w, v_w_out_a, v_ssm_conv_w, v_ssm_conv_b, v_ssm_dt_bias, v_ssm_a_log, v_ssm_d, v_ssm_norm, v_w_out_ssm, v_w_mix_out, v_xattn_norm, v_mem_norm, v_w_q, v_w_kv, v_w_o_x, v_ffn2_norm, v_ffn2_w_gate_up, v_ffn2_w_down, v_final_norm):
    a = dict(locals())
    xi, yi = lax.axis_index("x"), lax.axis_index("y")
    k = 2 * xi + yi

    wfull = _GatheredWeights({n: a[n][0].astype(BF16) for n, _ in BIG}, k)
    conv_vec, conv_offs = _pack([a["conv_a_w"], a["ssm_conv_w"]])
    conv_all = _gather_all(conv_vec, name="gather_conv_w")
    conv_sh = [_unpack(conv_all[2 * kk], conv_offs, [a["conv_a_w"].shape[1:], a["ssm_conv_w"].shape[1:]])
               for kk in range(4)]
    small = {n: a[n] for n in SMALL}
    small["conv_a_w"] = jnp.concatenate([cs[0] for cs in conv_sh], axis=1)
    small["ssm_conv_w"] = jnp.concatenate([cs[1] for cs in conv_sh], axis=1)

    rs_started = []

    def on_grads(tag, grads):
        names = [n for n, _ in BIG if n in grads]
        shard_major = [_shard_major(_unpad_w_in(grads[n]) if n == "w_in" else grads[n], dict(BIG)[n]) for n in names]
        st, tk = _rs_start(shard_major, tag)
        rs_started.append((tag, names, st))
        return tk[0, 0]

    sq_err, dx, g = _local_step(wfull, small, x, mem, loss_target, wfull.after[0, 0], on_grads)
    loss = lax.psum(0.5 / D * jnp.sum(sq_err), ("x", "y", "c"))

    ci = lax.axis_index("c")
    out = {}
    for tag, names, st in rs_started:
        g_mine, g_other = _rs_finish(st, dx, tag)
        for n, gm, go in zip(names, g_mine, g_other):
            res = _adamw_halves(a[n][0], gm, go, a["m_" + n][0], a["v_" + n][0], ci, name=f"adamw_{n}")
            out[n] = tuple(t.reshape(a[n].shape) for t in res)

    full_shapes = [g[n].shape for n in SMALL]
    gvec, goffs = _pack([g[n] for n in SMALL])
    gsum = _sum_leading(_gather_all(gvec, name="gather_small_grads"), name="sum_small_grads")
    gsmall = dict(zip(SMALL, _unpack(gsum, goffs, full_shapes)))
    for n in ("conv_a_w", "ssm_conv_w"):
        width = a[n].shape[2]
        gsmall[n] = lax.dynamic_slice_in_dim(gsmall[n], k * width, width, axis=1)
    local_shapes = [a[n].shape for n in SMALL]
    packs = [_pack([t[n] for n in SMALL]) for t in
             ({n: a[n] for n in SMALL}, gsmall, {n: a["m_" + n] for n in SMALL}, {n: a["v_" + n] for n in SMALL})]
    offs = packs[0][1]
    res = _adamw(*[p[0] for p in packs], name="adamw_small")
    unp = [_unpack(r, offs, local_shapes) for r in res]
    for i, n in enumerate(SMALL):
        out[n] = (gsmall[n].reshape(a[n].shape), unp[0][i], unp[1][i], unp[2][i])

    grad_x = dx.reshape(x.shape)
    return (loss, grad_x, *[out[n][0] for n in WEIGHTS], *[out[n][1] for n in WEIGHTS],
            *[out[n][2] for n in WEIGHTS], *[out[n][3] for n in WEIGHTS])
```

```python
import functools
import math

import jax
import jax.numpy as jnp
from jax import lax
from jax.experimental import pallas as pl
from jax.experimental.pallas import tpu as pltpu

F32 = jnp.float32
BF16 = jnp.bfloat16
MXU = jnp.bfloat16
HI = lax.Precision.HIGHEST

D = 1024
DFF = 2816
DI = 2048
NH, HD, NG, NS, CH = 32, 64, 4, 128, 128
GW = DI // NG
XH, XD = 4, 256
EPS = 1e-6
NEG = -1e30
CA_TILE = 256
O_CA, O_Z, O_XBC, O_GA, O_GB, O_DT, NPP = 0, 3072, 5120, 8192, 9216, 10240, 10368
NIN = 10272
FFN_RES = 0.5
ADAM_LR, ADAM_B1, ADAM_B2, ADAM_EPS, ADAM_WD, ADAM_STEP = 0.001, 0.9, 0.999, 1e-08, 0.01, 10
VMEM_LIMIT = 56 * 1024 * 1024
EPI_COLS = 256
MESH = pl.DeviceIdType.MESH
CHIP_FLIPS = ((1, 0), (0, 1), (1, 1))


def _cp(*sem):
    return pltpu.CompilerParams(dimension_semantics=sem, vmem_limit_bytes=VMEM_LIMIT)


def _tile(n, pref, align=128):
    if n <= pref:
        return n
    t = (pref // align) * align
    while t >= align:
        if n % t == 0:
            return t
        t -= align
    raise ValueError((n, pref))


def _dot(a, b, dims, prec=None):
    return lax.dot_general(a, b, (dims, ((), ())), preferred_element_type=F32, precision=prec)


def _nn(a, b, prec=None):
    return _dot(a, b, ((1,), (0,)), prec)


def _nt(a, b):
    return _dot(a, b, ((1,), (1,)))


def _tn(a, b):
    return _dot(a, b, ((0,), (0,)))


def _sig(x):
    return jax.nn.sigmoid(x)


def _mm(a, b, mode, *, name, tm=1024, tn=1024, tk=None, out_dtype=F32, scale=None, residual=None, a2=None,
        col_shards=0):
    if tk is None:
        tk = 2048 if mode == "tn" else 1024
    if mode == "nn":
        (M, K), (K2, N) = a.shape, b.shape
    elif mode == "nt":
        (M, K), (N, K2) = a.shape, b.shape
        if a2 is not None:
            assert a2.shape == a.shape
            K2 = K2 // 2
    else:
        (K, M), (K2, N) = a.shape, b.shape
    assert K == K2, (name, a.shape, b.shape)
    tm, tn, tk = _tile(M, tm), _tile(N, tn), _tile(K, tk)
    nk = K // tk
    if mode == "nn":
        a_spec = pl.BlockSpec((tm, tk), lambda i, j, k: (i, k))
        b_spec = pl.BlockSpec((tk, tn), lambda i, j, k: (k, j))
        dims = ((1,), (0,))
    elif mode == "nt":
        a_spec = pl.BlockSpec((tm, tk), lambda i, j, k: (i, k))
        b_spec = pl.BlockSpec((tn, tk), lambda i, j, k: (j, k))
        dims = ((1,), (1,))
    else:
        a_spec = pl.BlockSpec((tk, tm), lambda i, j, k: (k, i))
        b_spec = pl.BlockSpec((tk, tn), lambda i, j, k: (k, j))
        dims = ((0,), (0,))
    o_spec = pl.BlockSpec((tm, tn), lambda i, j, k: (i, j))
    out_spec, out_shape = o_spec, jax.ShapeDtypeStruct((M, N), out_dtype)
    if col_shards:
        per = N // col_shards // tn
        assert per * tn * col_shards == N, (name, N, tn, col_shards)
        out_spec = pl.BlockSpec((None, tm, tn), lambda i, j, k: (j // per, i, j % per))
        out_shape = jax.ShapeDtypeStruct((col_shards, M, N // col_shards), out_dtype)
    has_res = residual is not None

    def finish(acc, r_ref, o_ref):
        if scale is not None:
            acc = acc * scale
        if has_res:
            acc = acc + r_ref[...]
        o_ref[...] = acc.astype(out_dtype)

    dual = a2 is not None
    n_in = 2 + 2 * dual + has_res

    def body(*refs):
        a_ref, b_ref = refs[0], refs[1]
        r_ref = refs[n_in - 1] if has_res else None
        o_ref = refs[n_in]
        part = _dot(a_ref[...].astype(MXU), b_ref[...].astype(MXU), dims)
        if dual:
            part = part + _dot(refs[2][...].astype(MXU), refs[3][...].astype(MXU), dims)
        if nk == 1:
            finish(part, r_ref, o_ref)
            return
        acc_ref = refs[-1]
        k = pl.program_id(2)

        @pl.when(k == 0)
        def _():
            acc_ref[...] = part

        @pl.when(k > 0)
        def _():
            acc_ref[...] += part

        @pl.when(k == nk - 1)
        def _():
            finish(acc_ref[...], r_ref, o_ref)

    ins, in_specs = [a, b], [a_spec, b_spec]
    if dual:
        ins += [a2, b]
        in_specs += [a_spec, pl.BlockSpec((tn, tk), lambda i, j, k: (j, k + nk))]
    if has_res:
        ins.append(residual)
        in_specs.append(o_spec)
    return pl.pallas_call(
        body, grid=(M // tm, N // tn, nk), in_specs=in_specs, out_specs=out_spec, out_shape=out_shape,
        scratch_shapes=[pltpu.VMEM((tm, tn), F32)] if nk > 1 else [],
        compiler_params=_cp("parallel", "parallel", "arbitrary"), name=name)(*ins)


def _norm_fwd(x, g, *, name):
    T, d = x.shape
    tr = _tile(T, 512, 8)

    def body(x_ref, g_ref, o_ref):
        xv = x_ref[...]
        r = lax.rsqrt(jnp.mean(xv * xv, axis=-1, keepdims=True) + EPS)
        o_ref[...] = (xv * r * g_ref[...]).astype(BF16)

    return pl.pallas_call(
        body, grid=(T // tr,),
        in_specs=[pl.BlockSpec((tr, d), lambda i: (i, 0)), pl.BlockSpec((1, d), lambda i: (0, 0))],
        out_specs=pl.BlockSpec((tr, d), lambda i: (i, 0)),
        out_shape=jax.ShapeDtypeStruct((T, d), BF16), compiler_params=_cp("parallel"), name=name)(x, g)


def _norm_bwd(x, g, dn, dres, *, name):
    T, d = x.shape
    tr = _tile(T, 512, 8)
    has_res = dres is not None

    def body(*refs):
        x_ref, g_ref, dn_ref = refs[:3]
        dr_ref = refs[3] if has_res else None
        dx_ref, dg_ref = refs[-2], refs[-1]

        @pl.when(pl.program_id(0) == 0)
        def _():
            dg_ref[...] = jnp.zeros_like(dg_ref)

        xv = x_ref[...]
        dnv = dn_ref[...].astype(F32)
        r = lax.rsqrt(jnp.mean(xv * xv, axis=-1, keepdims=True) + EPS)
        xh = xv * r
        dg_ref[...] += jnp.sum(dnv * xh, axis=0, keepdims=True)
        dxh = dnv * g_ref[...]
        dx = r * (dxh - xh * jnp.mean(dxh * xh, axis=-1, keepdims=True))
        if has_res:
            dx = dx + dr_ref[...]
        dx_ref[...] = dx

    row = pl.BlockSpec((tr, d), lambda i: (i, 0))
    vec = pl.BlockSpec((1, d), lambda i: (0, 0))
    ins = [x, g, dn] + ([dres] if has_res else [])
    return pl.pallas_call(
        body, grid=(T // tr,), in_specs=[row, vec, row] + ([row] if has_res else []),
        out_specs=[row, vec],
        out_shape=[jax.ShapeDtypeStruct((T, d), F32), jax.ShapeDtypeStruct((1, d), F32)],
        compiler_params=_cp("arbitrary"), name=name)(*ins)


def _final_loss(h, g, target, *, name):
    T, d = h.shape
    tr = _tile(T, 512, 8)

    def body(h_ref, g_ref, t_ref, l_ref, dh_ref, dg_ref):
        @pl.when(pl.program_id(0) == 0)
        def _():
            l_ref[...] = jnp.zeros_like(l_ref)
            dg_ref[...] = jnp.zeros_like(dg_ref)

        xv = h_ref[...]
        r = lax.rsqrt(jnp.mean(xv * xv, axis=-1, keepdims=True) + EPS)
        xh = xv * r
        e = xh * g_ref[...] - t_ref[...]
        l_ref[...] += jnp.sum(e * e, axis=0, keepdims=True)
        dy = e * (1.0 / d)
        dg_ref[...] += jnp.sum(dy * xh, axis=0, keepdims=True)
        dxh = dy * g_ref[...]
        dh_ref[...] = r * (dxh - xh * jnp.mean(dxh * xh, axis=-1, keepdims=True))

    row = pl.BlockSpec((tr, d), lambda i: (i, 0))
    vec = pl.BlockSpec((1, d), lambda i: (0, 0))
    return pl.pallas_call(
        body, grid=(T // tr,), in_specs=[row, vec, row], out_specs=[vec, row, vec],
        out_shape=[jax.ShapeDtypeStruct((1, d), F32), jax.ShapeDtypeStruct((T, d), F32),
                   jax.ShapeDtypeStruct((1, d), F32)],
        compiler_params=_cp("arbitrary"), name=name)(h, g, target)


def _gate_up_fwd(n, wgu, *, name):
    T, d = n.shape
    f = wgu.shape[1] // 2
    tm, tn = _tile(T, 512, 8), _tile(f, DFF)
    nf = f // tn

    tc = _tile(tn, EPI_COLS)

    def body(n_ref, wg_ref, wu_ref, g_ref, u_ref, a_ref):
        nv = n_ref[...].astype(MXU)
        for j in range(tn // tc):
            sl = slice(j * tc, (j + 1) * tc)
            gv = _nn(nv, wg_ref[:, sl].astype(MXU))
            uv = _nn(nv, wu_ref[:, sl].astype(MXU))
            g_ref[:, sl] = gv.astype(BF16)
            u_ref[:, sl] = uv.astype(BF16)
            a_ref[:, sl] = (gv * _sig(gv) * uv).astype(BF16)

    out = pl.BlockSpec((tm, tn), lambda i, j: (i, j))
    act = jax.ShapeDtypeStruct((T, f), BF16)
    return pl.pallas_call(
        body, grid=(T // tm, nf),
        in_specs=[pl.BlockSpec((tm, d), lambda i, j: (i, 0)), pl.BlockSpec((d, tn), lambda i, j: (0, j)),
                  pl.BlockSpec((d, tn), lambda i, j: (0, j + nf))],
        out_specs=[out, out, out], out_shape=[act, act, act], compiler_params=_cp("parallel", "parallel"),
        name=name)(n, wgu, wgu)


def _act_bwd(dh, wd, gate, up, scale, *, name):
    T, d = dh.shape
    f = wd.shape[0]
    tm, tn = _tile(T, 512, 8), _tile(f, DFF)

    tc = _tile(tn, EPI_COLS)

    def body(dh_ref, wd_ref, g_ref, u_ref, dg_ref, du_ref):
        dhv = dh_ref[...].astype(MXU)
        for j in range(tn // tc):
            sl = slice(j * tc, (j + 1) * tc)
            da = scale * _nt(dhv, wd_ref[sl, :].astype(MXU))
            gv, uv = g_ref[:, sl].astype(F32), u_ref[:, sl].astype(F32)
            s = _sig(gv)
            dg_ref[:, sl] = (da * uv * (s * (1.0 + gv * (1.0 - s)))).astype(BF16)
            du_ref[:, sl] = (da * (gv * s)).astype(BF16)

    tile = pl.BlockSpec((tm, tn), lambda i, j: (i, j))
    act = jax.ShapeDtypeStruct((T, f), BF16)
    return pl.pallas_call(
        body, grid=(T // tm, f // tn),
        in_specs=[pl.BlockSpec((tm, d), lambda i, j: (i, 0)), pl.BlockSpec((tn, d), lambda i, j: (j, 0)), tile, tile],
        out_specs=[tile, tile], out_shape=[act, act], compiler_params=_cp("parallel", "parallel"),
        name=name)(dh, wd, gate, up)


CONV_ROWS = 64
CONV_PAD = 8


def _rows_down(ref, r0, d, cols=slice(None)):
    if r0 - d >= 0:
        return ref[pl.ds(r0 - d, CONV_ROWS), cols]
    assert r0 == 0
    v = ref[pl.ds(0, CONV_ROWS), cols]
    ri = lax.broadcasted_iota(jnp.int32, v.shape, 0)
    return jnp.where(ri >= d, pltpu.roll(v, d, 0), 0.0)


def _fold8(v):
    return jnp.sum(v.reshape(CONV_ROWS // 8, 8, v.shape[1]), axis=0)


def _taps(w_ref, views):
    acc = None
    for k, v in enumerate(views):
        t = w_ref[k:k + 1, :] * v
        acc = t if acc is None else acc + t
    return acc


def _conv_a_fwd(pp, w8, bl, s, *, name):
    tc = CA_TILE
    nb = D // tc
    bcol, ccol, vcol = slice(0, tc), slice(tc, 2 * tc), slice(2 * tc, 3 * tc)

    def body(p_ref, w_ref, o_ref):
        for r0 in range(0, s, CONV_ROWS):
            cv = [_rows_down(p_ref, r0, 2 - k, ccol) * _rows_down(p_ref, r0, 2 - k, vcol) for k in range(3)]
            o_ref[pl.ds(r0, CONV_ROWS), :] = (p_ref[pl.ds(r0, CONV_ROWS), bcol] * _taps(w_ref, cv)).astype(BF16)

    return pl.pallas_call(
        body, grid=(bl, nb),
        in_specs=[pl.BlockSpec((s, 3 * tc), lambda b, j: (b, j)), pl.BlockSpec((8, tc), lambda b, j: (0, j))],
        out_specs=pl.BlockSpec((s, tc), lambda b, j: (b, j)),
        out_shape=jax.ShapeDtypeStruct((bl * s, D), BF16), compiler_params=_cp("parallel", "parallel"),
        name=name)(pp, w8)


def _conv_a_bwd(pp, w8, dya, dpp, bl, s, *, name):
    tc = CA_TILE
    nb = D // tc
    bcol, ccol, vcol = slice(0, tc), slice(tc, 2 * tc), slice(2 * tc, 3 * tc)

    def body(p_ref, w_ref, dy_ref, dpp_in, d_ref, dw_ref, dcp):
        del dpp_in

        @pl.when(pl.program_id(1) == 0)
        def _():
            dw_ref[...] = jnp.zeros_like(dw_ref)

        dcp[pl.ds(s, CONV_PAD), :] = jnp.zeros((CONV_PAD, tc), F32)
        dw_acc = [jnp.zeros((8, tc), F32) for _ in range(3)]
        for r0 in reversed(range(0, s, CONV_ROWS)):
            rows = pl.ds(r0, CONV_ROWS)
            cs = [_rows_down(p_ref, r0, 2 - k, ccol) for k in range(3)]
            vs = [_rows_down(p_ref, r0, 2 - k, vcol) for k in range(3)]
            cv = [c_ * v_ for c_, v_ in zip(cs, vs)]
            dy = dy_ref[rows, :]
            d_ref[rows, bcol] = (dy * _taps(w_ref, cv)).astype(BF16)
            dconv = dy * p_ref[rows, bcol]
            dcp[rows, :] = dconv
            dcv = _taps(w_ref, [dcp[pl.ds(r0 + 2, CONV_ROWS), :], dcp[pl.ds(r0 + 1, CONV_ROWS), :], dconv])
            d_ref[rows, ccol] = (dcv * vs[2]).astype(BF16)
            d_ref[rows, vcol] = (dcv * cs[2]).astype(BF16)
            dw_acc = [acc + _fold8(dconv * cv_) for acc, cv_ in zip(dw_acc, cv)]
        for k in range(3):
            dw_ref[k:k + 1, :] += jnp.sum(dw_acc[k], axis=0, keepdims=True)

    wspec = pl.BlockSpec((8, tc), lambda j, b: (0, j))
    wide = pl.BlockSpec((s, 3 * tc), lambda j, b: (b, j))
    return pl.pallas_call(
        body, grid=(nb, bl),
        in_specs=[wide, wspec, pl.BlockSpec((s, tc), lambda j, b: (b, j)), pl.BlockSpec(memory_space=pl.ANY)],
        out_specs=[wide, wspec], out_shape=[jax.ShapeDtypeStruct(dpp.shape, dpp.dtype), jax.ShapeDtypeStruct((8, D), F32)],
        scratch_shapes=[pltpu.VMEM((s + CONV_PAD, tc), F32)], input_output_aliases={3: 0},
        compiler_params=_cp("parallel", "arbitrary"), name=name)(pp, w8, dya, dpp)


def _conv_ssm_fwd(pp, w8, bias, bl, s, *, name):
    tc = 256
    width = DI + 2 * NG * NS
    nb = width // tc

    def body(x_ref, w_ref, b_ref, o_ref):
        for r0 in range(0, s, CONV_ROWS):
            pre = _taps(w_ref, [_rows_down(x_ref, r0, 3 - k) for k in range(4)]) + b_ref[...]
            o_ref[pl.ds(r0, CONV_ROWS), :] = pre * _sig(pre)

    return pl.pallas_call(
        body, grid=(bl, nb),
        in_specs=[pl.BlockSpec((s, tc), lambda b, j: (b, O_XBC // tc + j)),
                  pl.BlockSpec((8, tc), lambda b, j: (0, j)), pl.BlockSpec((1, tc), lambda b, j: (0, j))],
        out_specs=pl.BlockSpec((s, tc), lambda b, j: (b, j)),
        out_shape=jax.ShapeDtypeStruct((bl * s, width), F32), compiler_params=_cp("parallel", "parallel"),
        name=name)(pp, w8, bias)


def _conv_ssm_bwd(pp, w8, bias, dxc, ch_off, dpp, bl, s, *, name):
    n = dxc.shape[1]
    tc = 256
    nb = n // tc
    o0 = ch_off // tc

    def body(x_ref, w_ref, b_ref, d_ref, dpp_in, dx_ref, dw_ref, db_ref, dp):
        del dpp_in

        @pl.when(pl.program_id(1) == 0)
        def _():
            dw_ref[...] = jnp.zeros_like(dw_ref)
            db_ref[...] = jnp.zeros_like(db_ref)

        dp[pl.ds(s, CONV_PAD), :] = jnp.zeros((CONV_PAD, tc), F32)
        dw_acc = [jnp.zeros((8, tc), F32) for _ in range(4)]
        db_acc = jnp.zeros((8, tc), F32)
        for r0 in reversed(range(0, s, CONV_ROWS)):
            rows = pl.ds(r0, CONV_ROWS)
            xs = [_rows_down(x_ref, r0, 3 - k) for k in range(4)]
            pre = _taps(w_ref, xs) + b_ref[...]
            sg = _sig(pre)
            dpre = d_ref[rows, :] * (sg * (1.0 + pre * (1.0 - sg)))
            dp[rows, :] = dpre
            dx = _taps(w_ref, [dp[pl.ds(r0 + 3 - k, CONV_ROWS), :] for k in range(3)] + [dpre])
            dx_ref[rows, :] = dx.astype(BF16)
            db_acc = db_acc + _fold8(dpre)
            dw_acc = [acc + _fold8(dpre * x_) for acc, x_ in zip(dw_acc, xs)]
        db_ref[...] += jnp.sum(db_acc, axis=0, keepdims=True)
        for k in range(4):
            dw_ref[k:k + 1, :] += jnp.sum(dw_acc[k], axis=0, keepdims=True)

    return pl.pallas_call(
        body, grid=(nb, bl),
        in_specs=[pl.BlockSpec((s, tc), lambda j, b: (b, O_XBC // tc + o0 + j)),
                  pl.BlockSpec((8, tc), lambda j, b: (0, o0 + j)), pl.BlockSpec((1, tc), lambda j, b: (0, o0 + j)),
                  pl.BlockSpec((s, tc), lambda j, b: (b, j)), pl.BlockSpec(memory_space=pl.ANY)],
        out_specs=[pl.BlockSpec((s, tc), lambda j, b: (b, O_XBC // tc + o0 + j)),
                   pl.BlockSpec((8, tc), lambda j, b: (0, j)), pl.BlockSpec((1, tc), lambda j, b: (0, j))],
        out_shape=[jax.ShapeDtypeStruct(dpp.shape, dpp.dtype), jax.ShapeDtypeStruct((8, n), F32),
                   jax.ShapeDtypeStruct((1, n), F32)],
        scratch_shapes=[pltpu.VMEM((s + CONV_PAD, tc), F32)], input_output_aliases={4: 0},
        compiler_params=_cp("parallel", "arbitrary"), name=name)(pp, w8, bias, dxc, dpp)


def _softplus(x):
    return jnp.maximum(x, 0.0) + jnp.log1p(jnp.exp(-jnp.abs(x)))


def _head_group_matrix():
    h = jnp.arange(128)[:, None]
    j = jnp.arange(NG * 128)[None, :]
    per = NH // NG
    return ((h < NH) & (j == (h // per) * 128 + h % per)).astype(F32)


def _dt_fwd(pp, bias128, *, name):
    T = pp.shape[0]
    tr = _tile(T, 1024, 8)
    per = NH // NG

    def body(x_ref, b_ref, p_ref, g_ref, t_ref):
        lane = lax.broadcasted_iota(jnp.int32, (tr, 128), 1)
        dt = jnp.where(lane < NH, _softplus(x_ref[...] + b_ref[...]), 0.0)
        g_ref[...] = _nn(dt, p_ref[...], HI)
        eye = (lax.broadcasted_iota(jnp.int32, (NH, 128), 0) == lax.broadcasted_iota(jnp.int32, (NH, 128), 1))
        t_ref[...] = _dot(eye.astype(F32), dt, ((1,), (1,)), HI).reshape(NG, per, tr)

    return pl.pallas_call(
        body, grid=(T // tr,),
        in_specs=[pl.BlockSpec((tr, 128), lambda i: (i, O_DT // 128)), pl.BlockSpec((1, 128), lambda i: (0, 0)),
                  pl.BlockSpec((128, NG * 128), lambda i: (0, 0))],
        out_specs=[pl.BlockSpec((tr, NG * 128), lambda i: (i, 0)), pl.BlockSpec((NG, per, tr), lambda i: (0, 0, i))],
        out_shape=[jax.ShapeDtypeStruct((T, NG * 128), F32), jax.ShapeDtypeStruct((NG, per, T), F32)],
        compiler_params=_cp("parallel"), name=name)(pp, bias128, _head_group_matrix())


def _dt_bwd(pp, bias128, ddtg, dpp, *, name):
    T = pp.shape[0]
    tr = _tile(T, 1024, 8)

    def body(x_ref, b_ref, d_ref, p_ref, dpp_in, o_ref, db_ref):
        del dpp_in

        @pl.when(pl.program_id(0) == 0)
        def _():
            db_ref[...] = jnp.zeros_like(db_ref)

        lane = lax.broadcasted_iota(jnp.int32, (tr, 128), 1)
        ddt = _dot(d_ref[...], p_ref[...], ((1,), (1,)), HI)
        dr = jnp.where(lane < NH, ddt * _sig(x_ref[...] + b_ref[...]), 0.0)
        db_ref[...] += jnp.sum(dr, axis=0, keepdims=True)
        o_ref[...] = dr.astype(BF16)

    col = pl.BlockSpec((tr, 128), lambda i: (i, O_DT // 128))
    vec = pl.BlockSpec((1, 128), lambda i: (0, 0))
    return pl.pallas_call(
        body, grid=(T // tr,),
        in_specs=[col, vec, pl.BlockSpec((tr, NG * 128), lambda i: (i, 0)), pl.BlockSpec((128, NG * 128), lambda i: (0, 0)),
                  pl.BlockSpec(memory_space=pl.ANY)],
        out_specs=[col, vec],
        out_shape=[jax.ShapeDtypeStruct(dpp.shape, dpp.dtype), jax.ShapeDtypeStruct((1, 128), F32)],
        input_output_aliases={4: 0}, compiler_params=_cp("arbitrary"),
        name=name)(pp, bias128, ddtg, _head_group_matrix(), dpp)


def _ssd_common(dt, dtt, arow, acol):
    ri = lax.broadcasted_iota(jnp.int32, (CH, CH), 0)
    ci = lax.broadcasted_iota(jnp.int32, (CH, CH), 1)
    tril = ri >= ci
    triu = ri <= ci
    acs_col = _nn(tril.astype(F32), dt * arow, HI)
    acs_row = _nn(dtt * acol, triu.astype(F32), HI)
    return tril, triu, acs_col, acs_row


def _pair_terms(q, dt, acs_col, acs_row, tril, lo):
    ha, hb = 2 * q, 2 * q + 1
    col_a, col_b = acs_col[:, ha:ha + 1], acs_col[:, hb:hb + 1]
    row_a, row_b = acs_row[ha:ha + 1, :], acs_row[hb:hb + 1, :]
    last_a, last_b = acs_col[CH - 1:CH, ha:ha + 1], acs_col[CH - 1:CH, hb:hb + 1]
    out = dict(
        dtsel=jnp.where(lo, dt[:, ha:ha + 1], dt[:, hb:hb + 1]),
        d_a=jnp.exp(jnp.where(tril, col_a - row_a, NEG)), d_b=jnp.exp(jnp.where(tril, col_b - row_b, NEG)),
        esel=jnp.where(lo, jnp.exp(col_a), jnp.exp(col_b)),
        fsel=jnp.where(lo, jnp.exp(last_a - col_a), jnp.exp(last_b - col_b)),
        g_a=jnp.exp(last_a), g_b=jnp.exp(last_b))
    return out


def _ssd_fwd(xc, pp, dtg, dtt, arow, acol, dexp, ng, bl, s, *, name):
    nc = s // CH
    T = bl * s

    def body(xs_ref, bm_ref, cm_ref, z_ref, dt_ref, dtt_ref, arow_ref, acol_ref, dexp_ref, ng_ref,
             y_ref, yn_ref, prev_ref, st_ref):
        @pl.when(pl.program_id(2) == 0)
        def _():
            st_ref[...] = jnp.zeros_like(st_ref)

        dt = dt_ref[...]
        tril, triu, acs_col, acs_row = _ssd_common(dt, dtt_ref[...], -jnp.exp(arow_ref[...]), -jnp.exp(acol_ref[...]))
        bm, cm = bm_ref[...].astype(MXU), cm_ref[...].astype(MXU)
        cb = _nt(cm, bm)
        lo = lax.broadcasted_iota(jnp.int32, (CH, 128), 1) < HD
        sub_lo = lax.broadcasted_iota(jnp.int32, (128, NS), 0) < HD
        for q in range(4):
            t = _pair_terms(q, dt, acs_col, acs_row, tril, lo)
            x = xs_ref[:, 128 * q:128 * (q + 1)]
            xd = x * t["dtsel"]
            y = (_nn((cb * t["d_a"]).astype(MXU), jnp.where(lo, xd, 0.0).astype(MXU))
                 + _nn((cb * t["d_b"]).astype(MXU), jnp.where(lo, 0.0, xd).astype(MXU)))
            prev = st_ref[q]
            prev_ref[q] = prev
            y = y + t["esel"] * _nt(cm, prev.astype(MXU))
            st_ref[q] = prev * jnp.where(sub_lo, t["g_a"], t["g_b"]) + _tn((xd * t["fsel"]).astype(MXU), bm)
            y_ref[:, 128 * q:128 * (q + 1)] = y + dexp_ref[:, 128 * q:128 * (q + 1)] * x
        zv = z_ref[...]
        yg = y_ref[...] * (zv * _sig(zv))
        r = lax.rsqrt(jnp.mean(yg * yg, axis=-1, keepdims=True) + EPS)
        yn_ref[...] = (yg * r * ng_ref[...]).astype(BF16)

    def row(width, off_blocks):
        return pl.BlockSpec((CH, width), lambda g, b, c: (b * nc + c, off_blocks + g))

    return pl.pallas_call(
        body, grid=(NG, bl, nc),
        in_specs=[row(GW, 0), row(NS, DI // NS), row(NS, DI // NS + NG), row(GW, O_Z // GW), row(128, 0),
                  pl.BlockSpec((None, 8, CH), lambda g, b, c: (g, 0, b * nc + c)),
                  pl.BlockSpec((1, 128), lambda g, b, c: (0, g)),
                  pl.BlockSpec((None, 8, 1), lambda g, b, c: (g, 0, 0)),
                  pl.BlockSpec((1, GW), lambda g, b, c: (0, g)), pl.BlockSpec((1, GW), lambda g, b, c: (0, g))],
        out_specs=[row(GW, 0), row(GW, 0),
                   pl.BlockSpec((None, 4, 128, NS), lambda g, b, c: (b * nc + c, g, 0, 0))],
        out_shape=[jax.ShapeDtypeStruct((T, DI), F32), jax.ShapeDtypeStruct((T, DI), BF16),
                   jax.ShapeDtypeStruct((bl * nc, 16, 128, NS), F32)],
        scratch_shapes=[pltpu.VMEM((4, 128, NS), F32)],
        compiler_params=_cp("parallel", "parallel", "arbitrary"), name=name,
    )(xc, xc, xc, pp, dtg, dtt, arow, acol, dexp, ng)


def _ssd_bwd(dyn, y, xc, pp, dtg, dtt, arow, acol, dexp, ng, prev, dpp, bl, s, *, name):
    nc = s // CH
    T = bl * s

    def rsum(v):
        return jnp.sum(v, axis=1, keepdims=True)

    def asum(v):
        return jnp.sum(jnp.sum(v, axis=0, keepdims=True), axis=1, keepdims=True)

    def body(dyn_ref, y_ref, xs_ref, bm_ref, cm_ref, z_ref, dt_ref, dtt_ref, arow_ref, acol_ref, dexp_ref, ng_ref,
             prev_ref, dpp_in, dz_ref, dxs_ref, db_ref, dc_ref, ddt_ref, dng_ref, dd_ref, dal_ref, dst_ref):
        del dpp_in

        @pl.when((pl.program_id(1) == 0) & (pl.program_id(2) == 0))
        def _():
            dng_ref[...] = jnp.zeros_like(dng_ref)
            dd_ref[...] = jnp.zeros_like(dd_ref)
            dal_ref[...] = jnp.zeros_like(dal_ref)

        @pl.when(pl.program_id(2) == 0)
        def _():
            dst_ref[...] = jnp.zeros_like(dst_ref)

        yv, zv, xsv, dexp_v = y_ref[...], z_ref[...], xs_ref[...], dexp_ref[...]
        sz = _sig(zv)
        silu = zv * sz
        yg = yv * silu
        r = lax.rsqrt(jnp.mean(yg * yg, axis=-1, keepdims=True) + EPS)
        yh = yg * r
        dynv = dyn_ref[...]
        dng_ref[...] += jnp.sum(dynv * yh, axis=0, keepdims=True)
        dyh = dynv * ng_ref[...]
        dyg = r * (dyh - yh * jnp.mean(dyh * yh, axis=-1, keepdims=True))
        dz_ref[...] = (dyg * yv * (sz * (1.0 + zv * (1.0 - sz)))).astype(BF16)
        dy_all = dyg * silu
        dd_ref[...] += jnp.sum(dy_all * xsv, axis=0, keepdims=True)

        dt = dt_ref[...]
        arow_v = -jnp.exp(arow_ref[...])
        tril, triu, acs_col, acs_row = _ssd_common(dt, dtt_ref[...], arow_v, -jnp.exp(acol_ref[...]))
        bm, cm = bm_ref[...].astype(MXU), cm_ref[...].astype(MXU)
        cb = _nt(cm, bm)
        lane = lax.broadcasted_iota(jnp.int32, (CH, 128), 1)
        is_last = lax.broadcasted_iota(jnp.int32, (CH, 128), 0) == CH - 1
        lo = lane < HD
        sub_lo = lax.broadcasted_iota(jnp.int32, (128, NS), 0) < HD
        dcb = jnp.zeros((CH, CH), F32)
        dc_acc = jnp.zeros((CH, NS), F32)
        db_acc = jnp.zeros((CH, NS), F32)
        dacs = jnp.zeros((CH, 128), F32)
        ddtx = jnp.zeros((CH, 128), F32)
        csum = jnp.zeros((8, CH), F32)
        sub8 = lax.broadcasted_iota(jnp.int32, (8, CH), 0)
        for q in range(4):
            ha, hb = 2 * q, 2 * q + 1
            sl = slice(128 * q, 128 * (q + 1))
            t = _pair_terms(q, dt, acs_col, acs_row, tril, lo)
            x, dy = xsv[:, sl], dy_all[:, sl]
            xd = x * t["dtsel"]
            xd_m = xd.astype(MXU)
            dy_lo, dy_hi = jnp.where(lo, dy, 0.0).astype(MXU), jnp.where(lo, 0.0, dy).astype(MXU)
            m_a, m_b = cb * t["d_a"], cb * t["d_b"]
            prev_m = prev_ref[q].astype(MXU)
            dnext = dst_ref[q]
            dnext_m = dnext.astype(MXU)
            bds = _nt(bm, dnext_m)
            dxd = _tn(m_a.astype(MXU), dy_lo) + _tn(m_b.astype(MXU), dy_hi) + t["fsel"] * bds
            dye_m = (dy * t["esel"]).astype(MXU)
            dst_ref[q] = dnext * jnp.where(sub_lo, t["g_a"], t["g_b"]) + _tn(dye_m, cm)
            dm_a, dm_b = _nt(dy_lo, xd_m), _nt(dy_hi, xd_m)
            dcb = dcb + dm_a * t["d_a"] + dm_b * t["d_b"]
            g_a, g_b = dm_a * m_a, dm_b * m_b
            csum = (csum + jnp.where(sub8 == ha, jnp.sum(g_a, axis=0, keepdims=True), 0.0)
                    + jnp.where(sub8 == hb, jnp.sum(g_b, axis=0, keepdims=True), 0.0))
            tf = t["fsel"] * xd * bds
            tyf = dy * (t["esel"] * _nt(cm, prev_m)) - tf
            dpp = dnext * prev_ref[q]
            ea = asum(jnp.where(lo, tf, 0.0)) + t["g_a"] * asum(jnp.where(sub_lo, dpp, 0.0))
            eb = asum(jnp.where(lo, 0.0, tf)) + t["g_b"] * asum(jnp.where(sub_lo, 0.0, dpp))
            ra = rsum(g_a + jnp.where(lo, tyf, 0.0)) + jnp.where(is_last, ea, 0.0)
            rb = rsum(g_b + jnp.where(lo, 0.0, tyf)) + jnp.where(is_last, eb, 0.0)
            dacs = dacs + jnp.where(lane == ha, ra, 0.0) + jnp.where(lane == hb, rb, 0.0)
            tx = dxd * x
            ddtx = (ddtx + jnp.where(lane == ha, rsum(jnp.where(lo, tx, 0.0)), 0.0)
                    + jnp.where(lane == hb, rsum(jnp.where(lo, 0.0, tx)), 0.0))
            dxs_ref[:, sl] = dxd * t["dtsel"] + dexp_v[:, sl] * dy
            dc_acc = dc_acc + _nn(dye_m, prev_m)
            db_acc = db_acc + _nn((xd * t["fsel"]).astype(MXU), dnext_m)
        dcb_m = dcb.astype(MXU)
        dc_ref[...] = dc_acc + _nn(dcb_m, bm)
        db_ref[...] = db_acc + _tn(dcb_m, cm)
        dacs = dacs - jnp.concatenate([csum, jnp.zeros((CH - 8, CH), F32)], axis=0).T
        dla = _nn(triu.astype(F32), dacs, HI)
        ddt_ref[...] = dla * arow_v + ddtx
        dal_ref[...] += jnp.sum(dla * dt, axis=0, keepdims=True) * arow_v

    def row(width, off_blocks):
        return pl.BlockSpec((CH, width), lambda g, b, c: (b * nc + nc - 1 - c, off_blocks + g))

    gvec = pl.BlockSpec((1, GW), lambda g, b, c: (0, g))
    hvec = pl.BlockSpec((1, 128), lambda g, b, c: (0, g))
    return pl.pallas_call(
        body, grid=(NG, bl, nc),
        in_specs=[row(GW, 0), row(GW, 0), row(GW, 0), row(NS, DI // NS), row(NS, DI // NS + NG), row(GW, O_Z // GW),
                  row(128, 0), pl.BlockSpec((None, 8, CH), lambda g, b, c: (g, 0, b * nc + nc - 1 - c)),
                  hvec, pl.BlockSpec((None, 8, 1), lambda g, b, c: (g, 0, 0)), gvec, gvec,
                  pl.BlockSpec((None, 4, 128, NS), lambda g, b, c: (b * nc + nc - 1 - c, g, 0, 0)),
                  pl.BlockSpec(memory_space=pl.ANY)],
        out_specs=[row(GW, O_Z // GW), row(GW, 0), row(NS, 0), row(NS, 0), row(128, 0), gvec, gvec, hvec],
        input_output_aliases={13: 0},
        out_shape=[jax.ShapeDtypeStruct(dpp.shape, dpp.dtype), jax.ShapeDtypeStruct((T, DI), F32),
                   jax.ShapeDtypeStruct((T, NG * NS), F32), jax.ShapeDtypeStruct((T, NG * NS), F32),
                   jax.ShapeDtypeStruct((T, NG * 128), F32), jax.ShapeDtypeStruct((1, DI), F32),
                   jax.ShapeDtypeStruct((1, DI), F32), jax.ShapeDtypeStruct((1, NG * 128), F32)],
        scratch_shapes=[pltpu.VMEM((4, 128, NS), F32)],
        compiler_params=_cp("arbitrary", "arbitrary", "arbitrary"), name=name,
    )(dyn, y, xc, xc, xc, pp, dtg, dtt, arow, acol, dexp, ng, prev, dpp)


def _merge_fwd(pp, ya, yb, *, name):
    T = ya.shape[0]
    tr = _tile(T, 512, 8)

    def body(ga_ref, gb_ref, ya_ref, yb_ref, o_ref):
        o_ref[...] = (_sig(ga_ref[...]) * ya_ref[...] + _sig(gb_ref[...]) * yb_ref[...]).astype(BF16)

    row = pl.BlockSpec((tr, D), lambda i: (i, 0))
    return pl.pallas_call(
        body, grid=(T // tr,),
        in_specs=[pl.BlockSpec((tr, D), lambda i: (i, O_GA // D)), pl.BlockSpec((tr, D), lambda i: (i, O_GB // D)),
                  row, row],
        out_specs=row, out_shape=jax.ShapeDtypeStruct((T, D), BF16), compiler_params=_cp("parallel"),
        name=name)(pp, pp, ya, yb)


def _merge_bwd(pp, ya, yb, dm, *, name):
    T = ya.shape[0]
    tr = _tile(T, 512, 8)
    assert O_GB == O_GA + D and O_GA % (2 * D) == 0

    def body(g_ref, ya_ref, yb_ref, dm_ref, dya_ref, dyb_ref, dg_ref):
        sa, sb, dmv = _sig(g_ref[:, :D]), _sig(g_ref[:, D:]), dm_ref[...]
        dya_ref[...] = (dmv * sa).astype(BF16)
        dyb_ref[...] = (dmv * sb).astype(BF16)
        dg_ref[:, :D] = (dmv * ya_ref[...] * (sa * (1.0 - sa))).astype(BF16)
        dg_ref[:, D:] = (dmv * yb_ref[...] * (sb * (1.0 - sb))).astype(BF16)

    row = pl.BlockSpec((tr, D), lambda i: (i, 0))
    gates = pl.BlockSpec((tr, 2 * D), lambda i: (i, O_GA // (2 * D)))
    act = jax.ShapeDtypeStruct((T, D), BF16)
    return pl.pallas_call(
        body, grid=(T // tr,), in_specs=[gates, row, row, row], out_specs=[row, row, gates],
        out_shape=[act, act, jax.ShapeDtypeStruct((T, NPP), BF16)], compiler_params=_cp("parallel"),
        name=name)(pp, ya, yb, dm)


def _softmax_rows(sc):
    e = jnp.exp(sc - jnp.max(sc, axis=-1, keepdims=True))
    return e / jnp.sum(e, axis=-1, keepdims=True)


def _attn_fwd(q, kv, bl, s, *, name):
    m = kv.shape[0] // bl
    tq = _tile(s, 512)
    nq = s // tq
    scale = 1.0 / math.sqrt(XD)

    def body(q_ref, k_ref, v_ref, o_ref):
        p = _softmax_rows(_nt(q_ref[...], k_ref[...]) * scale)
        o_ref[...] = _nn(p.astype(MXU), v_ref[...]).astype(BF16)

    qspec = pl.BlockSpec((tq, XD), lambda b, h, i: (b * nq + i, h))
    return pl.pallas_call(
        body, grid=(bl, XH, nq),
        in_specs=[qspec, pl.BlockSpec((m, XD), lambda b, h, i: (b, h)),
                  pl.BlockSpec((m, XD), lambda b, h, i: (b, XH + h))],
        out_specs=qspec, out_shape=jax.ShapeDtypeStruct((bl * s, D), BF16),
        compiler_params=_cp("parallel", "parallel", "parallel"), name=name)(q, kv, kv)


def _attn_bwd(q, kv, do, bl, s, *, name):
    m = kv.shape[0] // bl
    tq = _tile(s, 512)
    nq = s // tq
    scale = 1.0 / math.sqrt(XD)

    def body(q_ref, k_ref, v_ref, do_ref, dq_ref, dk_ref, dv_ref):
        @pl.when(pl.program_id(2) == 0)
        def _():
            dk_ref[...] = jnp.zeros_like(dk_ref)
            dv_ref[...] = jnp.zeros_like(dv_ref)

        qv, kvv, vv, dov = q_ref[...], k_ref[...], v_ref[...], do_ref[...]
        p = _softmax_rows(_nt(qv, kvv) * scale)
        dp = _nt(dov, vv)
        ds = (p * (dp - jnp.sum(dp * p, axis=-1, keepdims=True)) * scale).astype(MXU)
        dq_ref[...] = _nn(ds, kvv).astype(BF16)
        dk_ref[...] += _tn(ds, qv)
        dv_ref[...] += _tn(p.astype(MXU), dov)

    qspec = pl.BlockSpec((tq, XD), lambda b, h, i: (b * nq + i, h))
    kspec = pl.BlockSpec((m, XD), lambda b, h, i: (b, h))
    return pl.pallas_call(
        body, grid=(bl, XH, nq),
        in_specs=[qspec, kspec, pl.BlockSpec((m, XD), lambda b, h, i: (b, XH + h)), qspec],
        out_specs=[qspec, kspec, kspec],
        out_shape=[jax.ShapeDtypeStruct((bl * s, D), BF16), jax.ShapeDtypeStruct((bl * m, D), F32),
                   jax.ShapeDtypeStruct((bl * m, D), F32)],
        compiler_params=_cp("parallel", "parallel", "arbitrary"), name=name)(q, kv, kv, do)


def _row_tile(r, c, max_elems=512 * 1024, align=16):
    best = None
    for t in range(align, r + 1, align):
        if r % t == 0 and t * c <= max_elems:
            best = t
    return best if best is not None else r


def _addn(a, others, *, name, also_bf16=False):
    r, c = a.shape
    tr = _row_tile(r, c)
    n = len(others)

    def body(*refs):
        acc = refs[0][...].astype(F32)
        for o_ref in refs[1:1 + n]:
            acc = acc + o_ref[...].astype(F32)
        refs[1 + n][...] = acc
        if also_bf16:
            refs[2 + n][...] = acc.astype(BF16)

    spec = pl.BlockSpec((tr, c), lambda i: (i, 0))
    shapes = [jax.ShapeDtypeStruct((r, c), F32)] + ([jax.ShapeDtypeStruct((r, c), BF16)] if also_bf16 else [])
    out = pl.pallas_call(
        body, grid=(r // tr,), in_specs=[spec] * (1 + n), out_specs=[spec] * len(shapes), out_shape=shapes,
        compiler_params=_cp("parallel"), name=name)(a, *others)
    return out if also_bf16 else out[0]


def _sum_leading(a, *, name):
    n, r, c = a.shape

    def body(a_ref, o_ref):
        acc = a_ref[0]
        for i in range(1, n):
            acc = acc + a_ref[i]
        o_ref[...] = acc

    return pl.pallas_call(body, out_shape=jax.ShapeDtypeStruct((r, c), F32), name=name)(a)


def _adamw_math(wv, gv, mv, vv):
    m2 = ADAM_B1 * mv + (1.0 - ADAM_B1) * gv
    v2 = ADAM_B2 * vv + (1.0 - ADAM_B2) * (gv * gv)
    m_hat = m2 / (1.0 - ADAM_B1 ** ADAM_STEP)
    v_hat = v2 / (1.0 - ADAM_B2 ** ADAM_STEP)
    return -ADAM_LR * (m_hat / (jnp.sqrt(v_hat) + ADAM_EPS) + ADAM_WD * wv), m2, v2


def _adamw(w, g, m, v, *, name):
    r, c = w.shape
    tr = _row_tile(r, c, align=8)

    def body(w_ref, g_ref, m_ref, v_ref, d_ref, mo_ref, vo_ref):
        d_ref[...], mo_ref[...], vo_ref[...] = _adamw_math(w_ref[...], g_ref[...], m_ref[...], v_ref[...])

    spec = pl.BlockSpec((tr, c), lambda i: (i, 0))
    shp = jax.ShapeDtypeStruct((r, c), F32)
    return pl.pallas_call(
        body, grid=(r // tr,), in_specs=[spec] * 4, out_specs=[spec] * 3, out_shape=[shp] * 3,
        compiler_params=_cp("parallel"), name=name)(w, g, m, v)


def _adamw_halves(w, g_mine, g_other, m, v, c, *, name):
    r, cols = w.shape
    h = r // 2
    tr = _row_tile(h, cols, align=8)
    nh = h // tr

    def body(c_ref, w_ref, gm_ref, go_ref, m_ref, v_ref, g_ref, d_ref, mo_ref, vo_ref):
        gv = jnp.where(pl.program_id(0) // nh == c_ref[0], gm_ref[...], go_ref[...])
        g_ref[...] = gv
        d_ref[...], mo_ref[...], vo_ref[...] = _adamw_math(w_ref[...], gv, m_ref[...], v_ref[...])

    full = pl.BlockSpec((tr, cols), lambda i, c_: (i, 0))
    half = pl.BlockSpec((tr, cols), lambda i, c_: (i % nh, 0))
    shp = jax.ShapeDtypeStruct((r, cols), F32)
    return pl.pallas_call(
        body,
        grid_spec=pltpu.PrefetchScalarGridSpec(num_scalar_prefetch=1, grid=(2 * nh,),
                                               in_specs=[full, half, half, full, full], out_specs=[full] * 4),
        out_shape=[shp] * 4, compiler_params=_cp("parallel"), name=name,
    )(jnp.reshape(c, (1,)).astype(jnp.int32), w, g_mine, g_other, m, v)


def _flip(i, d):
    return 1 - i if d else i


def _comm(name, ins, out_shapes, n_remote, n_local, plan, aliases=None):
    n_in, n_out = len(ins), len(out_shapes)

    def body(*refs):
        in_refs, out_refs = refs[:n_in], refs[n_in:n_in + n_out]
        send_sems, recv_sems = refs[n_in + n_out], refs[n_in + n_out + 1]
        x, y, c = lax.axis_index("x"), lax.axis_index("y"), lax.axis_index("c")
        remote, local = plan(in_refs, out_refs, x, y, c)
        assert len(remote) == n_remote and len(local) == n_local
        copies = []
        if n_local:
            loc_sems = refs[n_in + n_out + 2]
            copies += [pltpu.make_async_copy(s_, d_, loc_sems.at[i]) for i, (s_, d_) in enumerate(local)]
        copies += [pltpu.make_async_remote_copy(src_ref=s_, dst_ref=d_, send_sem=send_sems.at[i],
                                                recv_sem=recv_sems.at[i], device_id=dev, device_id_type=MESH)
                   for i, (s_, d_, dev) in enumerate(remote)]
        for cp in copies:
            cp.start()
        for cp in copies:
            cp.wait()

    hbm = pl.BlockSpec(memory_space=pl.ANY)
    scratch = [pltpu.SemaphoreType.DMA((n_remote,)), pltpu.SemaphoreType.DMA((n_remote,))]
    if n_local:
        scratch.append(pltpu.SemaphoreType.DMA((n_local,)))
    return pl.pallas_call(
        body, in_specs=[hbm] * n_in, out_specs=[hbm] * n_out, out_shape=out_shapes, scratch_shapes=scratch,
        input_output_aliases=aliases or {}, compiler_params=pltpu.CompilerParams(has_side_effects=True),
        name=name)(*ins)


HBM_SPEC = pl.BlockSpec(memory_space=pltpu.HBM)
SEM_SPEC = pl.BlockSpec(memory_space=pltpu.SEMAPHORE)
DATAFLOW = pltpu.SideEffectType.DATAFLOW_SIDE_EFFECTING


def _remote_copies(plan, srcs, lands, send_sems, recv_sems, n_copies):
    x, y, c = lax.axis_index("x"), lax.axis_index("y"), lax.axis_index("c")
    copies = plan(srcs, lands, x, y, c)
    assert len(copies) == n_copies
    return [pltpu.make_async_remote_copy(src_ref=s_, dst_ref=d_, send_sem=send_sems.at[i], recv_sem=recv_sems.at[i],
                                         device_id=dev, device_id_type=MESH) for i, (s_, d_, dev) in enumerate(copies)]


def _split_start(name, srcs, lands, n_copies, plan, after=None):
    ns, nb = len(srcs), len(srcs) + len(lands)
    n_after = 0 if after is None else 1
    n_in = nb + n_after

    def body(*refs):
        for cp in _remote_copies(plan, refs[:ns], refs[ns:nb], refs[n_in], refs[n_in + 1], n_copies):
            cp.start()
        refs[-1][...] = jnp.zeros_like(refs[-1])

    arrays = [pltpu.with_memory_space_constraint(a_, pltpu.HBM) for a_ in list(srcs) + list(lands)]
    out = pl.pallas_call(
        body, name=name,
        out_shape=(pltpu.SemaphoreType.DMA((n_copies,)), pltpu.SemaphoreType.DMA((n_copies,)),
                   *[pltpu.HBM(a_.shape, a_.dtype) for a_ in arrays], jax.ShapeDtypeStruct((8, 128), F32)),
        in_specs=[HBM_SPEC] * nb + [pl.BlockSpec(memory_space=pl.ANY)] * n_after,
        out_specs=(SEM_SPEC, SEM_SPEC, *[HBM_SPEC] * nb, pl.BlockSpec(memory_space=pltpu.VMEM)),
        input_output_aliases={i: 2 + i for i in range(nb)},
        compiler_params=pltpu.CompilerParams(has_side_effects=DATAFLOW))(*arrays, *([after] * n_after))
    return (out[0], out[1], list(out[2:2 + nb])), out[-1]


def _split_wait(name, handle, ns, n_copies, plan, after):
    send_sems, recv_sems, bufs = handle
    nb = len(bufs)

    def body(*refs):
        for cp in _remote_copies(plan, refs[:ns], refs[ns:nb], refs[nb], refs[nb + 1], n_copies):
            cp.wait_send()
            cp.wait_recv()

    out = pl.pallas_call(
        body, name=name, out_shape=[pltpu.HBM(b_.shape, b_.dtype) for b_ in bufs],
        in_specs=[HBM_SPEC] * nb + [SEM_SPEC, SEM_SPEC, pl.BlockSpec(memory_space=pl.ANY)],
        out_specs=[HBM_SPEC] * nb, input_output_aliases={i: i for i in range(nb)},
        compiler_params=pltpu.CompilerParams(has_side_effects=DATAFLOW))(*bufs, send_sems, recv_sems, after)
    return list(out[ns:])


def _gather_start(shards, tag, after=None):
    n = len(shards)
    lands = [lax.empty((4,) + s.shape, s.dtype) for s in shards]

    def plan(srcs, dsts, x, y, c):
        k = 2 * x + y
        copies = []
        for w_ref, o_ref in zip(srcs, dsts):
            h = w_ref.shape[0] // 2
            rows = pl.ds(c * h, h)
            copies += [(w_ref.at[rows], o_ref.at[k, rows], (_flip(x, dx), _flip(y, dy), c)) for dx, dy in CHIP_FLIPS]
        return copies

    handle, token = _split_start(f"gather_{tag}_start", shards, lands, 3 * n, plan, after)
    return (handle, plan, n), token


def _gather_wait(started, after, tag):
    handle, plan, n = started
    return _split_wait(f"gather_{tag}_wait", handle, n, 3 * n, plan, after)


def _gather_d2d(lands, before, tag):
    n = len(lands)

    def plan_d2d(in_refs, out_refs, x, y, c):
        remote = []
        for o_ref in out_refs:
            h = o_ref.shape[1] // 2
            for dx, dy in CHIP_FLIPS:
                half = o_ref.at[2 * _flip(x, dx) + _flip(y, dy), pl.ds(c * h, h)]
                remote.append((half, half, (x, y, 1 - c)))
        return remote, []

    return _comm(f"gather_{tag}_d2d", list(lands) + list(before),
                 [jax.ShapeDtypeStruct(l_.shape, l_.dtype) for l_ in lands], 3 * n, 0, plan_d2d,
                 aliases={i: i for i in range(n)})


def _pair_plan(in_refs, out_refs, x, y, c):
    return [(i_, o_, (x, y, 1 - c)) for i_, o_ in zip(in_refs, out_refs)], []


def _rs_start(grads, tag):
    n = len(grads)
    c = lax.axis_index("c")
    halves = [g.shape[1] // 2 for g in grads]
    mine = [lax.dynamic_slice_in_dim(g, c * h, h, axis=1) for g, h in zip(grads, halves)]
    send_a = [lax.dynamic_slice_in_dim(g, (1 - c) * h, h, axis=1).astype(BF16) for g, h in zip(grads, halves)]
    recv_a = _comm(f"rs_pair_{tag}", send_a, [jax.ShapeDtypeStruct(s.shape, BF16) for s in send_a], n, 0, _pair_plan)
    pair, pair_b = [], []
    for i, (mi, ra) in enumerate(zip(mine, recv_a)):
        four, h, cols = mi.shape
        p32, p16 = _addn(mi.reshape(four * h, cols), [ra.reshape(four * h, cols)], name=f"rs_pair_sum_{tag}_{i}",
                         also_bf16=True)
        pair.append(p32.reshape(four, h, cols))
        pair_b.append(p16.reshape(four, h, cols))

    def plan(srcs, dsts, x, y, c_):
        copies = []
        for i_, o_ in zip(srcs, dsts):
            for j, (dx, dy) in enumerate(CHIP_FLIPS):
                fx, fy = _flip(x, dx), _flip(y, dy)
                copies.append((i_.at[2 * fx + fy], o_.at[j], (fx, fy, c_)))
        return copies

    lands = [lax.empty((3,) + p.shape[1:], BF16) for p in pair_b]
    handle, token = _split_start(f"rs_chips_{tag}_start", pair_b, lands, 3 * n, plan)
    return (handle, plan, n, pair), token


def _rs_finish(started, after, tag):
    handle, plan, n, pair = started
    recv_b = _split_wait(f"rs_chips_{tag}_wait", handle, n, 3 * n, plan, after)
    k = 2 * lax.axis_index("x") + lax.axis_index("y")
    tot = [_addn(lax.dynamic_index_in_dim(p, k, 0, keepdims=False), [rb[0], rb[1], rb[2]],
                 name=f"rs_chip_sum_{tag}_{i}") for i, (p, rb) in enumerate(zip(pair, recv_b))]
    other = _comm(f"rs_halves_{tag}", tot, [jax.ShapeDtypeStruct(t.shape, F32) for t in tot], n, 0, _pair_plan)
    return tot, other


def _gather_all(vec, *, name):
    out = jax.ShapeDtypeStruct((8,) + vec.shape, vec.dtype)

    def plan(in_refs, out_refs, x, y, c):
        me = 4 * x + 2 * y + c
        remote = [(in_refs[0], out_refs[0].at[me], (_flip(x, dx), _flip(y, dy), _flip(c, dc)))
                  for dx in (0, 1) for dy in (0, 1) for dc in (0, 1) if (dx, dy, dc) != (0, 0, 0)]
        return remote, [(in_refs[0], out_refs[0].at[me])]

    return _comm(name, [vec], [out], 7, 1, plan)[0]


def _pack(parts):
    flat = [p.reshape(-1).astype(F32) for p in parts]
    total = sum(f.shape[0] for f in flat)
    n = -(-total // 1024) * 128
    vec = jnp.concatenate(flat + [jnp.zeros((8 * n - total,), F32)]).reshape(8, n)
    offs, o = [], 0
    for f in flat:
        offs.append((o, f.shape[0]))
        o += f.shape[0]
    return vec, offs


def _unpack(vec, offs, shapes):
    flat = vec.reshape(-1)
    return [flat[o:o + n].reshape(s) for (o, n), s in zip(offs, shapes)]


BIG = (("ffn1_w_gate_up", "col"), ("ffn1_w_down", "row"), ("w_in", "col"), ("w_out_a", "row"), ("w_out_ssm", "row"),
       ("w_mix_out", "row"), ("w_q", "row"), ("w_kv", "col"), ("w_o_x", "row"), ("ffn2_w_gate_up", "col"),
       ("ffn2_w_down", "row"))
SMALL = ("ffn1_norm", "mix_norm", "conv_a_w", "ssm_conv_w", "ssm_conv_b", "ssm_dt_bias", "ssm_a_log", "ssm_d",
         "ssm_norm", "xattn_norm", "mem_norm", "ffn2_norm", "final_norm")
WEIGHTS = ("ffn1_norm", "ffn1_w_gate_up", "ffn1_w_down", "mix_norm", "w_in", "conv_a_w", "w_out_a", "ssm_conv_w",
           "ssm_conv_b", "ssm_dt_bias", "ssm_a_log", "ssm_d", "ssm_norm", "w_out_ssm", "w_mix_out", "xattn_norm",
           "mem_norm", "w_q", "w_kv", "w_o_x", "ffn2_norm", "ffn2_w_gate_up", "ffn2_w_down", "final_norm")


GATHER_GROUPS = (("a", ("ffn1_w_gate_up",)), ("b", ("ffn1_w_down", "w_in")),
                 ("c", ("w_out_a", "w_out_ssm", "w_mix_out", "w_q", "w_kv", "w_o_x", "ffn2_w_gate_up", "ffn2_w_down")))


def _place_own(land, own, k, *, name):
    four, r, cols = land.shape
    tr = _row_tile(r, cols)

    def body(k_ref, own_ref, land_in, o_ref):
        del k_ref, land_in
        o_ref[...] = own_ref[...]

    return pl.pallas_call(
        body,
        grid_spec=pltpu.PrefetchScalarGridSpec(
            num_scalar_prefetch=1, grid=(r // tr,),
            in_specs=[pl.BlockSpec((tr, cols), lambda i, k_: (i, 0)), pl.BlockSpec(memory_space=pl.ANY)],
            out_specs=pl.BlockSpec((None, tr, cols), lambda i, k_: (k_[0], i, 0))),
        out_shape=jax.ShapeDtypeStruct(land.shape, land.dtype), input_output_aliases={2: 0},
        compiler_params=_cp("parallel"), name=name)(jnp.reshape(k, (1,)).astype(jnp.int32), own, land)


def _full_weight(land, own, kind, k, *, name):
    land = _place_own(land, own, k, name=name)
    four, r, cols = land.shape
    if kind == "row":
        return land.reshape(four * r, cols)
    return jnp.transpose(land, (1, 0, 2)).reshape(r, four * cols)


class _GatheredWeights:
    def __init__(self, shards, k, after):
        self.shards, self.k = shards, k
        self.full = {}
        self.n_done = 0
        self.started, self.after = self._start(0, after)

    def _start(self, gi, after):
        tag, names = GATHER_GROUPS[gi]
        return _gather_start([self.shards[n] for n in names], tag, after)

    def mark(self, value):
        self.after = value

    def __getitem__(self, name):
        if name not in self.full:
            tag, names = GATHER_GROUPS[self.n_done]
            assert name in names, (name, tag)
            lands = _gather_wait(self.started, self.after, tag)
            before = []
            if self.n_done + 1 < len(GATHER_GROUPS):
                self.started, token = self._start(self.n_done + 1, lands[0])
                before = [token]
            lands = _gather_d2d(lands, before, tag)
            for n, land in zip(names, lands):
                w = _full_weight(land, self.shards[n], dict(BIG)[n], self.k, name=f"own_{n}")
                self.full[n] = _pad_w_in(w) if n == "w_in" else w
            self.n_done += 1
        return self.full[name]


def _shard_major(dw, kind):
    if isinstance(dw, tuple):
        return jnp.concatenate(dw, axis=0)
    if dw.ndim == 3:
        return dw
    if kind == "row":
        return dw.reshape(4, dw.shape[0] // 4, dw.shape[1])
    return jnp.transpose(dw.reshape(dw.shape[0], 4, dw.shape[1] // 4), (1, 0, 2))


def _pad_rows8(w):
    return jnp.concatenate([w, jnp.zeros((8 - w.shape[0], w.shape[1]), w.dtype)], axis=0)


def _group_lanes(v):
    r = v.shape[0]
    return jnp.pad(v.reshape(r, NG, NH // NG), ((0, 0), (0, 0), (0, 128 - NH // NG))).reshape(r, NG * 128)


def _ungroup_lanes(v):
    r = v.shape[0]
    return v.reshape(r, NG, 128)[:, :, :NH // NG].reshape(r, NH)


def _local_step(wfull, small, x, mem, target, token=0.0, on_grads=None):
    bl, s, _ = x.shape
    T = bl * s
    x2, t2 = x.reshape(T, D), target.reshape(T, D)
    mem2 = mem.reshape(-1, D)
    g = {}
    tok = [token]
    mark = getattr(wfull, "mark", lambda value: None)

    def gain(name):
        return small[name].reshape(1, -1) + tok[0]

    def emit(tag, names):
        if on_grads is not None:
            tok[0] = tok[0] + on_grads(tag, {n: g[n] for n in names})

    def ffn_fwd(h, norm, wgu, wd, tag):
        n = _norm_fwd(h, gain(norm), name=f"{tag}_norm")
        gate, up, a = _gate_up_fwd(n, wfull[wgu], name=f"{tag}_gate_up")
        mark(a)
        out = _mm(a, wfull[wd], "nn", tk=DFF, scale=FFN_RES, residual=h, name=f"{tag}_down")
        return out, (n, gate, up, a)

    def ffn_bwd(dh, h, norm, wgu, wd, saved, tag):
        n, gate, up, a = saved
        dgate, dup = _act_bwd(dh, wfull[wd], gate, up, FFN_RES, name=f"{tag}_d_act")
        g[wd] = _mm(a, dh, "tn", tm=1408, scale=FFN_RES, name=f"{tag}_d_w_down")
        g[wgu] = (_mm(n, dgate, "tn", tn=1408, col_shards=2, name=f"{tag}_d_w_gate"),
                  _mm(n, dup, "tn", tn=1408, col_shards=2, name=f"{tag}_d_w_up"))
        emit(tag, (wgu, wd))
        dn = _mm(dgate, wfull[wgu], "nt", a2=dup, tk=1408, name=f"{tag}_d_norm_out")
        dh_in, g[norm] = _norm_bwd(h, gain(norm), dn, dh, name=f"{tag}_d_norm")
        return dh_in

    h1, ffn1_saved = ffn_fwd(x2, "ffn1_norm", "ffn1_w_gate_up", "ffn1_w_down", "ffn1")
    mark(h1)
    u = _norm_fwd(h1, gain("mix_norm"), name="mix_norm")
    pp = _mm(u, wfull["w_in"], "nn", tn=1152, name="in_proj")
    wa8 = _pad_rows8(small["conv_a_w"])
    ws8 = _pad_rows8(small["ssm_conv_w"])
    conv_b = gain("ssm_conv_b")
    bias128 = jnp.pad(gain("ssm_dt_bias"), ((0, 0), (0, 128 - NH)))
    ya_pre = _conv_a_fwd(pp, wa8, bl, s, name="conv_a")
    xc = _conv_ssm_fwd(pp, ws8, conv_b, bl, s, name="conv_ssm")
    mark(xc)
    dtg, dtt = _dt_fwd(pp, bias128, name="dt")
    alog = gain("ssm_a_log")
    arow, acol = _group_lanes(alog), alog.reshape(NG, NH // NG, 1)
    dexp = jnp.repeat(gain("ssm_d"), HD, axis=1)
    ng = gain("ssm_norm")
    y, yn, prev = _ssd_fwd(xc, pp, dtg, dtt, arow, acol, dexp, ng, bl, s, name="ssd")
    ya = _mm(ya_pre, wfull["w_out_a"], "nn", tn=1024, name="out_a")
    yb = _mm(yn, wfull["w_out_ssm"], "nn", tn=1024, tk=DI, name="out_ssm")
    merged = _merge_fwd(pp, ya, yb, name="merge")
    h2 = _mm(merged, wfull["w_mix_out"], "nn", tn=1024, residual=h1, name="mix_out")
    un = _norm_fwd(h2, gain("xattn_norm"), name="xattn_norm")
    q = _mm(un, wfull["w_q"], "nn", tn=1024, out_dtype=BF16, name="q_proj")
    mn = _norm_fwd(mem2, gain("mem_norm"), name="mem_norm")
    kv = _mm(mn, wfull["w_kv"], "nn", tn=1024, out_dtype=BF16, name="kv_proj")
    o = _attn_fwd(q, kv, bl, s, name="attn")
    h3 = _mm(o, wfull["w_o_x"], "nn", tn=1024, residual=h2, name="attn_out")
    h4, ffn2_saved = ffn_fwd(h3, "ffn2_norm", "ffn2_w_gate_up", "ffn2_w_down", "ffn2")
    sq_err, dh4, dgf = _final_loss(h4, gain("final_norm"), t2, name="final_loss")
    g["final_norm"] = dgf

    dh3 = ffn_bwd(dh4, h3, "ffn2_norm", "ffn2_w_gate_up", "ffn2_w_down", ffn2_saved, "ffn2")
    do = _mm(dh3, wfull["w_o_x"], "nt", tn=1024, out_dtype=BF16, name="d_attn_o")
    g["w_o_x"] = _mm(o, dh3, "tn",name="d_w_o_x")
    dq, dk, dv = _attn_bwd(q, kv, do, bl, s, name="d_attn")
    dun = _mm(dq, wfull["w_q"], "nt", tn=1024, name="d_xattn_norm_out")
    g["w_q"] = _mm(un, dq, "tn",name="d_w_q")
    dkv = jnp.concatenate([dk, dv], axis=1)
    dmn = _mm(dkv, wfull["w_kv"], "nt", tn=1024, tk=2 * D, name="d_mem_norm_out")
    g["w_kv"] = _mm(mn, dkv, "tn", tn=512, col_shards=4, name="d_w_kv")
    emit("attn", ("w_q", "w_kv", "w_o_x"))
    _, g["mem_norm"] = _norm_bwd(mem2, gain("mem_norm"), dmn, None, name="d_mem_norm")
    dh2, g["xattn_norm"] = _norm_bwd(h2, gain("xattn_norm"), dun, dh3, name="d_xattn_norm")
    dmerged = _mm(dh2, wfull["w_mix_out"], "nt", tn=1024, name="d_merged")
    g["w_mix_out"] = _mm(merged, dh2, "tn",name="d_w_mix_out")
    dya, dyb, dpp = _merge_bwd(pp, ya, yb, dmerged, name="d_merge")
    dya_pre = _mm(dya, wfull["w_out_a"], "nt", tn=1024, name="d_conv_a_out")
    g["w_out_a"] = _mm(ya_pre, dya, "tn",name="d_w_out_a")
    dyn = _mm(dyb, wfull["w_out_ssm"], "nt", tn=DI, name="d_ssd_out")
    g["w_out_ssm"] = _mm(yn, dyb, "tn",name="d_w_out_ssm")
    dpp, dwa8 = _conv_a_bwd(pp, wa8, dya_pre, dpp, bl, s, name="d_conv_a")
    g["conv_a_w"] = dwa8[:3]
    dpp, dxs, dbm, dcm, ddtg, g["ssm_norm"], ddexp, dalg = _ssd_bwd(
        dyn, y, xc, pp, dtg, dtt, arow, acol, dexp, ng, prev, dpp, bl, s, name="d_ssd")
    g["ssm_d"] = ddexp.reshape(NH, HD).sum(axis=1).reshape(1, NH)
    g["ssm_a_log"] = _ungroup_lanes(dalg)
    conv_dw, conv_db = [], []
    for dpart, off, tag in ((dxs, 0, "x"), (dbm, DI, "b"), (dcm, DI + NG * NS, "c")):
        dpp, dw_, db_ = _conv_ssm_bwd(pp, ws8, conv_b, dpart, off, dpp, bl, s, name=f"d_conv_ssm_{tag}")
        conv_dw.append(dw_)
        conv_db.append(db_)
    g["ssm_conv_w"] = jnp.concatenate(conv_dw, axis=1)[:4]
    g["ssm_conv_b"] = jnp.concatenate(conv_db, axis=1)
    dpp, dbias = _dt_bwd(pp, bias128, ddtg, dpp, name="d_dt")
    g["ssm_dt_bias"] = dbias[:, :NH]
    g["w_in"] = _mm(u, dpp, "tn", tn=1152, name="d_w_in")
    emit("mix", ("w_in", "w_out_a", "w_out_ssm", "w_mix_out"))
    du = _mm(dpp, wfull["w_in"], "nt", tk=3456, name="d_mix_norm_out")
    dh1, g["mix_norm"] = _norm_bwd(h1, gain("mix_norm"), du, dh2, name="d_mix_norm")
    dx = ffn_bwd(dh1, x2, "ffn1_norm", "ffn1_w_gate_up", "ffn1_w_down", ffn1_saved, "ffn1")
    return sq_err, dx, g


def _pad_w_in(w):
    r = w.shape[0]
    conv_a = jnp.transpose(w[:, :3 * D].reshape(r, 3, D // CA_TILE, CA_TILE), (0, 2, 1, 3)).reshape(r, 3 * D)
    return jnp.concatenate([conv_a, w[:, 3 * D:O_GA], w[:, O_GA + NH:], w[:, O_GA:O_GA + NH],
                            jnp.zeros((r, NPP - NIN), w.dtype)], axis=1)


def _unpad_w_in(w):
    r = w.shape[0]
    conv_a = jnp.transpose(w[:, :3 * D].reshape(r, D // CA_TILE, 3, CA_TILE), (0, 2, 1, 3)).reshape(r, 3 * D)
    return jnp.concatenate([conv_a, w[:, 3 * D:O_GA], w[:, O_DT:O_DT + NH], w[:, O_GA:O_DT]], axis=1)


def kernel(x, mem, ffn1_norm, ffn1_w_gate_up, ffn1_w_down, mix_norm, w_in, conv_a_w, w_out_a, ssm_conv_w, ssm_conv_b, ssm_dt_bias, ssm_a_log, ssm_d, ssm_norm, w_out_ssm, w_mix_out, xattn_norm, mem_norm, w_q, w_kv, w_o_x, ffn2_norm, ffn2_w_gate_up, ffn2_w_down, final_norm, loss_target, m_ffn1_norm, m_ffn1_w_gate_up, m_ffn1_w_down, m_mix_norm, m_w_in, m_conv_a_w, m_w_out_a, m_ssm_conv_w, m_ssm_conv_b, m_ssm_dt_bias, m_ssm_a_log, m_ssm_d, m_ssm_norm, m_w_out_ssm, m_w_mix_out, m_xattn_norm, m_mem_norm, m_w_q, m_w_kv, m_w_o_x, m_ffn2_norm, m_ffn2_w_gate_up, m_ffn2_w_down, m_final_norm, v_ffn1_norm, v_ffn1_w_gate_up, v_ffn1_w_down, v_mix_norm, v_w_in, v_conv_a_w, v_w_out_a, v_ssm_conv_w, v_ssm_conv_b, v_ssm_dt_bias, v_ssm_a_log, v_ssm_d, v_ssm_norm, v_w_out_ssm, v_w_mix_out, v_xattn_norm, v_mem_norm, v_w_q, v_w_kv, v_w_o_x, v_ffn2_norm, v_ffn2_w_gate_up, v_ffn2_w_down, v_final_norm):
    a = dict(locals())
    xi, yi = lax.axis_index("x"), lax.axis_index("y")
    k = 2 * xi + yi

    conv_vec, conv_offs = _pack([a["conv_a_w"], a["ssm_conv_w"]])
    conv_all = _gather_all(conv_vec, name="gather_conv_w")
    wfull = _GatheredWeights({n: a[n][0].astype(BF16) for n, _ in BIG}, k, conv_all)
    conv_sh = [_unpack(conv_all[2 * kk], conv_offs, [a["conv_a_w"].shape[1:], a["ssm_conv_w"].shape[1:]])
               for kk in range(4)]
    small = {n: a[n] for n in SMALL}
    small["conv_a_w"] = jnp.concatenate([cs[0] for cs in conv_sh], axis=1)
    small["ssm_conv_w"] = jnp.concatenate([cs[1] for cs in conv_sh], axis=1)

    rs_started = []

    def on_grads(tag, grads):
        names = [n for n, _ in BIG if n in grads]
        shard_major = [_shard_major(_unpad_w_in(grads[n]) if n == "w_in" else grads[n], dict(BIG)[n]) for n in names]
        st, tk = _rs_start(shard_major, tag)
        rs_started.append((tag, names, st))
        return tk[0, 0]

    sq_err, dx, g = _local_step(wfull, small, x, mem, loss_target, wfull.after[0, 0], on_grads)
    loss = lax.psum(0.5 / D * jnp.sum(sq_err), ("x", "y", "c"))

    ci = lax.axis_index("c")
    out = {}
    for tag, names, st in rs_started:
        g_mine, g_other = _rs_finish(st, dx, tag)
        for n, gm, go in zip(names, g_mine, g_other):
            res = _adamw_halves(a[n][0], gm, go, a["m_" + n][0], a["v_" + n][0], ci, name=f"adamw_{n}")
            out[n] = tuple(t.reshape(a[n].shape) for t in res)

    full_shapes = [g[n].shape for n in SMALL]
    gvec, goffs = _pack([g[n] for n in SMALL])
    gsum = _sum_leading(_gather_all(gvec, name="gather_small_grads"), name="sum_small_grads")
    gsmall = dict(zip(SMALL, _unpack(gsum, goffs, full_shapes)))
    for n in ("conv_a_w", "ssm_conv_w"):
        width = a[n].shape[2]
        gsmall[n] = lax.dynamic_slice_in_dim(gsmall[n], k * width, width, axis=1)
    local_shapes = [a[n].shape for n in SMALL]
    packs = [_pack([t[n] for n in SMALL]) for t in
             ({n: a[n] for n in SMALL}, gsmall, {n: a["m_" + n] for n in SMALL}, {n: a["v_" + n] for n in SMALL})]
    offs = packs[0][1]
    res = _adamw(*[p[0] for p in packs], name="adamw_small")
    unp = [_unpack(r, offs, local_shapes) for r in res]
    for i, n in enumerate(SMALL):
        out[n] = (gsmall[n].reshape(a[n].shape), unp[0][i], unp[1][i], unp[2][i])

    grad_x = dx.reshape(x.shape)
    return (loss, grad_x, *[out[n][0] for n in WEIGHTS], *[out[n][1] for n in WEIGHTS],
            *[out[n][2] for n in WEIGHTS], *[out[n][3] for n in WEIGHTS])
```

```python
import functools
import math

import jax
import jax.numpy as jnp
from jax import lax
from jax.experimental import pallas as pl
from jax.experimental.pallas import tpu as pltpu

F32 = jnp.float32
BF16 = jnp.bfloat16
MXU = jnp.bfloat16
HI = lax.Precision.HIGHEST

D = 1024
DFF = 2816
DI = 2048
NH, HD, NG, NS, CH = 32, 64, 4, 128, 128
GW = DI // NG
XH, XD = 4, 256
EPS = 1e-6
NEG = -1e30
CA_TILE = 256
O_CA, O_Z, O_XBC, O_GA, O_GB, O_DT, NPP = 0, 3072, 5120, 8192, 9216, 10240, 10368
NIN = 10272
FFN_RES = 0.5
ADAM_LR, ADAM_B1, ADAM_B2, ADAM_EPS, ADAM_WD, ADAM_STEP = 0.001, 0.9, 0.999, 1e-08, 0.01, 10
VMEM_LIMIT = 56 * 1024 * 1024
EPI_COLS = 256
MESH = pl.DeviceIdType.MESH
CHIP_FLIPS = ((1, 0), (0, 1), (1, 1))


def _cp(*sem):
    return pltpu.CompilerParams(dimension_semantics=sem, vmem_limit_bytes=VMEM_LIMIT)


def _tile(n, pref, align=128):
    if n <= pref:
        return n
    t = (pref // align) * align
    while t >= align:
        if n % t == 0:
            return t
        t -= align
    raise ValueError((n, pref))


def _dot(a, b, dims, prec=None):
    return lax.dot_general(a, b, (dims, ((), ())), preferred_element_type=F32, precision=prec)


def _nn(a, b, prec=None):
    return _dot(a, b, ((1,), (0,)), prec)


def _nt(a, b):
    return _dot(a, b, ((1,), (1,)))


def _tn(a, b):
    return _dot(a, b, ((0,), (0,)))


def _sig(x):
    return jax.nn.sigmoid(x)


def _mm(a, b, mode, *, name, tm=1024, tn=1024, tk=None, out_dtype=F32, scale=None, residual=None, a2=None,
        col_shards=0, norm_gain=None):
    if tk is None:
        tk = 2048 if mode == "tn" else 1024
    if mode == "nn":
        (M, K), (K2, N) = a.shape, b.shape
    elif mode == "nt":
        (M, K), (N, K2) = a.shape, b.shape
        if a2 is not None:
            assert a2.shape == a.shape
            K2 = K2 // 2
    else:
        (K, M), (K2, N) = a.shape, b.shape
    assert K == K2, (name, a.shape, b.shape)
    tm, tn, tk = _tile(M, tm), _tile(N, tn), _tile(K, tk)
    nk = K // tk
    if mode == "nn":
        a_spec = pl.BlockSpec((tm, tk), lambda i, j, k: (i, k))
        b_spec = pl.BlockSpec((tk, tn), lambda i, j, k: (k, j))
        dims = ((1,), (0,))
    elif mode == "nt":
        a_spec = pl.BlockSpec((tm, tk), lambda i, j, k: (i, k))
        b_spec = pl.BlockSpec((tn, tk), lambda i, j, k: (j, k))
        dims = ((1,), (1,))
    else:
        a_spec = pl.BlockSpec((tk, tm), lambda i, j, k: (k, i))
        b_spec = pl.BlockSpec((tk, tn), lambda i, j, k: (k, j))
        dims = ((0,), (0,))
    o_spec = pl.BlockSpec((tm, tn), lambda i, j, k: (i, j))
    out_spec, out_shape = o_spec, jax.ShapeDtypeStruct((M, N), out_dtype)
    if col_shards:
        per = N // col_shards // tn
        assert per * tn * col_shards == N, (name, N, tn, col_shards)
        out_spec = pl.BlockSpec((None, tm, tn), lambda i, j, k: (j // per, i, j % per))
        out_shape = jax.ShapeDtypeStruct((col_shards, M, N // col_shards), out_dtype)
    has_res = residual is not None
    has_norm = norm_gain is not None
    assert not has_norm or (tn == N and not col_shards)
    dual = a2 is not None
    n_in = 2 + 2 * dual + has_res + has_norm

    def body(*refs):
        a_ref, b_ref = refs[0], refs[1]
        o_ref = refs[n_in]

        def finish(acc):
            if scale is not None:
                acc = acc * scale
            if has_res:
                acc = acc + refs[2 + 2 * dual][...]
            o_ref[...] = acc.astype(out_dtype)
            if has_norm:
                rs = lax.rsqrt(jnp.mean(acc * acc, axis=-1, keepdims=True) + EPS)
                refs[n_in + 1][...] = (acc * rs * refs[n_in - 1][...]).astype(BF16)

        part = _dot(a_ref[...].astype(MXU), b_ref[...].astype(MXU), dims)
        if dual:
            part = part + _dot(refs[2][...].astype(MXU), refs[3][...].astype(MXU), dims)
        if nk == 1:
            finish(part)
            return
        acc_ref = refs[-1]
        k = pl.program_id(2)

        @pl.when(k == 0)
        def _():
            acc_ref[...] = part

        @pl.when(k > 0)
        def _():
            acc_ref[...] += part

        @pl.when(k == nk - 1)
        def _():
            finish(acc_ref[...])

    ins, in_specs = [a, b], [a_spec, b_spec]
    if dual:
        ins += [a2, b]
        in_specs += [a_spec, pl.BlockSpec((tn, tk), lambda i, j, k: (j, k + nk))]
    if has_res:
        ins.append(residual)
        in_specs.append(o_spec)
    if has_norm:
        ins.append(norm_gain)
        in_specs.append(pl.BlockSpec((1, tn), lambda i, j, k: (0, j)))
        out_spec, out_shape = [out_spec, o_spec], [out_shape, jax.ShapeDtypeStruct((M, N), BF16)]
    return pl.pallas_call(
        body, grid=(M // tm, N // tn, nk), in_specs=in_specs, out_specs=out_spec, out_shape=out_shape,
        scratch_shapes=[pltpu.VMEM((tm, tn), F32)] if nk > 1 else [],
        compiler_params=_cp("parallel", "parallel", "arbitrary"), name=name)(*ins)


def _norm_fwd(x, g, *, name):
    T, d = x.shape
    tr = _tile(T, 512, 8)

    def body(x_ref, g_ref, o_ref):
        xv = x_ref[...]
        r = lax.rsqrt(jnp.mean(xv * xv, axis=-1, keepdims=True) + EPS)
        o_ref[...] = (xv * r * g_ref[...]).astype(BF16)

    return pl.pallas_call(
        body, grid=(T // tr,),
        in_specs=[pl.BlockSpec((tr, d), lambda i: (i, 0)), pl.BlockSpec((1, d), lambda i: (0, 0))],
        out_specs=pl.BlockSpec((tr, d), lambda i: (i, 0)),
        out_shape=jax.ShapeDtypeStruct((T, d), BF16), compiler_params=_cp("parallel"), name=name)(x, g)


def _norm_bwd(x, g, dn, dres, *, name):
    T, d = x.shape
    tr = _tile(T, 512, 8)
    has_res = dres is not None

    def body(*refs):
        x_ref, g_ref, dn_ref = refs[:3]
        dr_ref = refs[3] if has_res else None
        dx_ref, dg_ref = refs[-2], refs[-1]

        @pl.when(pl.program_id(0) == 0)
        def _():
            dg_ref[...] = jnp.zeros_like(dg_ref)

        xv = x_ref[...]
        dnv = dn_ref[...].astype(F32)
        r = lax.rsqrt(jnp.mean(xv * xv, axis=-1, keepdims=True) + EPS)
        xh = xv * r
        dg_ref[...] += jnp.sum(dnv * xh, axis=0, keepdims=True)
        dxh = dnv * g_ref[...]
        dx = r * (dxh - xh * jnp.mean(dxh * xh, axis=-1, keepdims=True))
        if has_res:
            dx = dx + dr_ref[...]
        dx_ref[...] = dx

    row = pl.BlockSpec((tr, d), lambda i: (i, 0))
    vec = pl.BlockSpec((1, d), lambda i: (0, 0))
    ins = [x, g, dn] + ([dres] if has_res else [])
    return pl.pallas_call(
        body, grid=(T // tr,), in_specs=[row, vec, row] + ([row] if has_res else []),
        out_specs=[row, vec],
        out_shape=[jax.ShapeDtypeStruct((T, d), F32), jax.ShapeDtypeStruct((1, d), F32)],
        compiler_params=_cp("arbitrary"), name=name)(*ins)


def _final_loss(h, g, target, *, name):
    T, d = h.shape
    tr = _tile(T, 512, 8)

    def body(h_ref, g_ref, t_ref, l_ref, dh_ref, dg_ref):
        @pl.when(pl.program_id(0) == 0)
        def _():
            l_ref[...] = jnp.zeros_like(l_ref)
            dg_ref[...] = jnp.zeros_like(dg_ref)

        xv = h_ref[...]
        r = lax.rsqrt(jnp.mean(xv * xv, axis=-1, keepdims=True) + EPS)
        xh = xv * r
        e = xh * g_ref[...] - t_ref[...]
        l_ref[...] += jnp.sum(e * e, axis=0, keepdims=True)
        dy = e * (1.0 / d)
        dg_ref[...] += jnp.sum(dy * xh, axis=0, keepdims=True)
        dxh = dy * g_ref[...]
        dh_ref[...] = r * (dxh - xh * jnp.mean(dxh * xh, axis=-1, keepdims=True))

    row = pl.BlockSpec((tr, d), lambda i: (i, 0))
    vec = pl.BlockSpec((1, d), lambda i: (0, 0))
    return pl.pallas_call(
        body, grid=(T // tr,), in_specs=[row, vec, row], out_specs=[vec, row, vec],
        out_shape=[jax.ShapeDtypeStruct((1, d), F32), jax.ShapeDtypeStruct((T, d), F32),
                   jax.ShapeDtypeStruct((1, d), F32)],
        compiler_params=_cp("arbitrary"), name=name)(h, g, target)


def _gate_up_fwd(n, wgu, *, name):
    T, d = n.shape
    f = wgu.shape[1] // 2
    tm, tn = _tile(T, 512, 8), _tile(f, DFF)
    nf = f // tn

    tc = _tile(tn, EPI_COLS)

    def body(n_ref, wg_ref, wu_ref, g_ref, u_ref, a_ref):
        nv = n_ref[...].astype(MXU)
        for j in range(tn // tc):
            sl = slice(j * tc, (j + 1) * tc)
            gv = _nn(nv, wg_ref[:, sl].astype(MXU))
            uv = _nn(nv, wu_ref[:, sl].astype(MXU))
            g_ref[:, sl] = gv.astype(BF16)
            u_ref[:, sl] = uv.astype(BF16)
            a_ref[:, sl] = (gv * _sig(gv) * uv).astype(BF16)

    out = pl.BlockSpec((tm, tn), lambda i, j: (i, j))
    act = jax.ShapeDtypeStruct((T, f), BF16)
    return pl.pallas_call(
        body, grid=(T // tm, nf),
        in_specs=[pl.BlockSpec((tm, d), lambda i, j: (i, 0)), pl.BlockSpec((d, tn), lambda i, j: (0, j)),
                  pl.BlockSpec((d, tn), lambda i, j: (0, j + nf))],
        out_specs=[out, out, out], out_shape=[act, act, act], compiler_params=_cp("parallel", "parallel"),
        name=name)(n, wgu, wgu)


def _act_bwd(dh, wd, gate, up, scale, *, name):
    T, d = dh.shape
    f = wd.shape[0]
    tm, tn = _tile(T, 512, 8), _tile(f, DFF)

    tc = _tile(tn, EPI_COLS)

    def body(dh_ref, wd_ref, g_ref, u_ref, dg_ref, du_ref):
        dhv = dh_ref[...].astype(MXU)
        for j in range(tn // tc):
            sl = slice(j * tc, (j + 1) * tc)
            da = scale * _nt(dhv, wd_ref[sl, :].astype(MXU))
            gv, uv = g_ref[:, sl].astype(F32), u_ref[:, sl].astype(F32)
            s = _sig(gv)
            dg_ref[:, sl] = (da * uv * (s * (1.0 + gv * (1.0 - s)))).astype(BF16)
            du_ref[:, sl] = (da * (gv * s)).astype(BF16)

    tile = pl.BlockSpec((tm, tn), lambda i, j: (i, j))
    act = jax.ShapeDtypeStruct((T, f), BF16)
    return pl.pallas_call(
        body, grid=(T // tm, f // tn),
        in_specs=[pl.BlockSpec((tm, d), lambda i, j: (i, 0)), pl.BlockSpec((tn, d), lambda i, j: (j, 0)), tile, tile],
        out_specs=[tile, tile], out_shape=[act, act], compiler_params=_cp("parallel", "parallel"),
        name=name)(dh, wd, gate, up)


CONV_ROWS = 64
CONV_PAD = 8


def _rows_down(ref, r0, d, cols=slice(None)):
    if r0 - d >= 0:
        return ref[pl.ds(r0 - d, CONV_ROWS), cols]
    assert r0 == 0
    v = ref[pl.ds(0, CONV_ROWS), cols]
    ri = lax.broadcasted_iota(jnp.int32, v.shape, 0)
    return jnp.where(ri >= d, pltpu.roll(v, d, 0), 0.0)


def _fold8(v):
    return jnp.sum(v.reshape(CONV_ROWS // 8, 8, v.shape[1]), axis=0)


def _taps(w_ref, views):
    acc = None
    for k, v in enumerate(views):
        t = w_ref[k:k + 1, :] * v
        acc = t if acc is None else acc + t
    return acc


def _conv_a_fwd(pp, w8, bl, s, *, name):
    tc = CA_TILE
    nb = D // tc
    bcol, ccol, vcol = slice(0, tc), slice(tc, 2 * tc), slice(2 * tc, 3 * tc)

    def body(p_ref, w_ref, o_ref):
        for r0 in range(0, s, CONV_ROWS):
            cv = [_rows_down(p_ref, r0, 2 - k, ccol) * _rows_down(p_ref, r0, 2 - k, vcol) for k in range(3)]
            o_ref[pl.ds(r0, CONV_ROWS), :] = (p_ref[pl.ds(r0, CONV_ROWS), bcol] * _taps(w_ref, cv)).astype(BF16)

    return pl.pallas_call(
        body, grid=(bl, nb),
        in_specs=[pl.BlockSpec((s, 3 * tc), lambda b, j: (b, j)), pl.BlockSpec((8, tc), lambda b, j: (0, j))],
        out_specs=pl.BlockSpec((s, tc), lambda b, j: (b, j)),
        out_shape=jax.ShapeDtypeStruct((bl * s, D), BF16), compiler_params=_cp("parallel", "parallel"),
        name=name)(pp, w8)


def _conv_a_bwd(pp, w8, dya, dpp, bl, s, *, name):
    tc = CA_TILE
    nb = D // tc
    bcol, ccol, vcol = slice(0, tc), slice(tc, 2 * tc), slice(2 * tc, 3 * tc)

    def body(p_ref, w_ref, dy_ref, dpp_in, d_ref, dw_ref, dcp):
        del dpp_in

        @pl.when(pl.program_id(1) == 0)
        def _():
            dw_ref[...] = jnp.zeros_like(dw_ref)

        dcp[pl.ds(s, CONV_PAD), :] = jnp.zeros((CONV_PAD, tc), F32)
        dw_acc = [jnp.zeros((8, tc), F32) for _ in range(3)]
        for r0 in reversed(range(0, s, CONV_ROWS)):
            rows = pl.ds(r0, CONV_ROWS)
            cs = [_rows_down(p_ref, r0, 2 - k, ccol) for k in range(3)]
            vs = [_rows_down(p_ref, r0, 2 - k, vcol) for k in range(3)]
            cv = [c_ * v_ for c_, v_ in zip(cs, vs)]
            dy = dy_ref[rows, :]
            d_ref[rows, bcol] = (dy * _taps(w_ref, cv)).astype(BF16)
            dconv = dy * p_ref[rows, bcol]
            dcp[rows, :] = dconv
            dcv = _taps(w_ref, [dcp[pl.ds(r0 + 2, CONV_ROWS), :], dcp[pl.ds(r0 + 1, CONV_ROWS), :], dconv])
            d_ref[rows, ccol] = (dcv * vs[2]).astype(BF16)
            d_ref[rows, vcol] = (dcv * cs[2]).astype(BF16)
            dw_acc = [acc + _fold8(dconv * cv_) for acc, cv_ in zip(dw_acc, cv)]
        for k in range(3):
            dw_ref[k:k + 1, :] += jnp.sum(dw_acc[k], axis=0, keepdims=True)

    wspec = pl.BlockSpec((8, tc), lambda j, b: (0, j))
    wide = pl.BlockSpec((s, 3 * tc), lambda j, b: (b, j))
    return pl.pallas_call(
        body, grid=(nb, bl),
        in_specs=[wide, wspec, pl.BlockSpec((s, tc), lambda j, b: (b, j)), pl.BlockSpec(memory_space=pl.ANY)],
        out_specs=[wide, wspec], out_shape=[jax.ShapeDtypeStruct(dpp.shape, dpp.dtype), jax.ShapeDtypeStruct((8, D), F32)],
        scratch_shapes=[pltpu.VMEM((s + CONV_PAD, tc), F32)], input_output_aliases={3: 0},
        compiler_params=_cp("parallel", "arbitrary"), name=name)(pp, w8, dya, dpp)


def _conv_ssm_fwd(pp, w8, bias, bl, s, *, name):
    tc = 256
    width = DI + 2 * NG * NS
    nb = width // tc

    def body(x_ref, w_ref, b_ref, o_ref):
        for r0 in range(0, s, CONV_ROWS):
            pre = _taps(w_ref, [_rows_down(x_ref, r0, 3 - k) for k in range(4)]) + b_ref[...]
            o_ref[pl.ds(r0, CONV_ROWS), :] = pre * _sig(pre)

    return pl.pallas_call(
        body, grid=(bl, nb),
        in_specs=[pl.BlockSpec((s, tc), lambda b, j: (b, O_XBC // tc + j)),
                  pl.BlockSpec((8, tc), lambda b, j: (0, j)), pl.BlockSpec((1, tc), lambda b, j: (0, j))],
        out_specs=pl.BlockSpec((s, tc), lambda b, j: (b, j)),
        out_shape=jax.ShapeDtypeStruct((bl * s, width), F32), compiler_params=_cp("parallel", "parallel"),
        name=name)(pp, w8, bias)


def _conv_ssm_bwd(pp, w8, bias, dxc, ch_off, dpp, bl, s, *, name):
    n = dxc.shape[1]
    tc = 256
    nb = n // tc
    o0 = ch_off // tc

    def body(x_ref, w_ref, b_ref, d_ref, dpp_in, dx_ref, dw_ref, db_ref, dp):
        del dpp_in

        @pl.when(pl.program_id(1) == 0)
        def _():
            dw_ref[...] = jnp.zeros_like(dw_ref)
            db_ref[...] = jnp.zeros_like(db_ref)

        dp[pl.ds(s, CONV_PAD), :] = jnp.zeros((CONV_PAD, tc), F32)
        dw_acc = [jnp.zeros((8, tc), F32) for _ in range(4)]
        db_acc = jnp.zeros((8, tc), F32)
        for r0 in reversed(range(0, s, CONV_ROWS)):
            rows = pl.ds(r0, CONV_ROWS)
            xs = [_rows_down(x_ref, r0, 3 - k) for k in range(4)]
            pre = _taps(w_ref, xs) + b_ref[...]
            sg = _sig(pre)
            dpre = d_ref[rows, :] * (sg * (1.0 + pre * (1.0 - sg)))
            dp[rows, :] = dpre
            dx = _taps(w_ref, [dp[pl.ds(r0 + 3 - k, CONV_ROWS), :] for k in range(3)] + [dpre])
            dx_ref[rows, :] = dx.astype(BF16)
            db_acc = db_acc + _fold8(dpre)
            dw_acc = [acc + _fold8(dpre * x_) for acc, x_ in zip(dw_acc, xs)]
        db_ref[...] += jnp.sum(db_acc, axis=0, keepdims=True)
        for k in range(4):
            dw_ref[k:k + 1, :] += jnp.sum(dw_acc[k], axis=0, keepdims=True)

    return pl.pallas_call(
        body, grid=(nb, bl),
        in_specs=[pl.BlockSpec((s, tc), lambda j, b: (b, O_XBC // tc + o0 + j)),
                  pl.BlockSpec((8, tc), lambda j, b: (0, o0 + j)), pl.BlockSpec((1, tc), lambda j, b: (0, o0 + j)),
                  pl.BlockSpec((s, tc), lambda j, b: (b, j)), pl.BlockSpec(memory_space=pl.ANY)],
        out_specs=[pl.BlockSpec((s, tc), lambda j, b: (b, O_XBC // tc + o0 + j)),
                   pl.BlockSpec((8, tc), lambda j, b: (0, j)), pl.BlockSpec((1, tc), lambda j, b: (0, j))],
        out_shape=[jax.ShapeDtypeStruct(dpp.shape, dpp.dtype), jax.ShapeDtypeStruct((8, n), F32),
                   jax.ShapeDtypeStruct((1, n), F32)],
        scratch_shapes=[pltpu.VMEM((s + CONV_PAD, tc), F32)], input_output_aliases={4: 0},
        compiler_params=_cp("parallel", "arbitrary"), name=name)(pp, w8, bias, dxc, dpp)


def _softplus(x):
    return jnp.maximum(x, 0.0) + jnp.log1p(jnp.exp(-jnp.abs(x)))


def _head_group_matrix():
    h = jnp.arange(128)[:, None]
    j = jnp.arange(NG * 128)[None, :]
    per = NH // NG
    return ((h < NH) & (j == (h // per) * 128 + h % per)).astype(F32)


def _dt_fwd(pp, bias128, alog128, *, name):
    T = pp.shape[0]
    tr = _tile(T, 1024, CH)
    per = NH // NG

    def body(x_ref, b_ref, al_ref, p_ref, g_ref, t_ref, ag_ref, at_ref):
        lane = lax.broadcasted_iota(jnp.int32, (tr, 128), 1)
        dt = jnp.where(lane < NH, _softplus(x_ref[...] + b_ref[...]), 0.0)
        g_ref[...] = _nn(dt, p_ref[...], HI)
        eye = (lax.broadcasted_iota(jnp.int32, (NH, 128), 0)
               == lax.broadcasted_iota(jnp.int32, (NH, 128), 1)).astype(F32)
        t_ref[...] = _dot(eye, dt, ((1,), (1,)), HI).reshape(NG, per, tr)
        la = dt * -jnp.exp(al_ref[...])
        tril = _tril().astype(F32)
        for ci in range(tr // CH):
            rows = slice(ci * CH, (ci + 1) * CH)
            acs = _nn(tril, la[rows], HI)
            ag_ref[rows, :] = _nn(acs, p_ref[...], HI)
            at_ref[:, :, rows] = _dot(eye, acs, ((1,), (1,)), HI).reshape(NG, per, CH)

    grouped = pl.BlockSpec((tr, NG * 128), lambda i: (i, 0))
    transposed = pl.BlockSpec((NG, per, tr), lambda i: (0, 0, i))
    vec = pl.BlockSpec((1, 128), lambda i: (0, 0))
    return pl.pallas_call(
        body, grid=(T // tr,),
        in_specs=[pl.BlockSpec((tr, 128), lambda i: (i, O_DT // 128)), vec, vec,
                  pl.BlockSpec((128, NG * 128), lambda i: (0, 0))],
        out_specs=[grouped, transposed, grouped, transposed],
        out_shape=[jax.ShapeDtypeStruct((T, NG * 128), F32), jax.ShapeDtypeStruct((NG, per, T), F32)] * 2,
        compiler_params=_cp("parallel"), name=name)(pp, bias128, alog128, _head_group_matrix())


def _dt_bwd(pp, bias128, ddtg, dpp, *, name):
    T = pp.shape[0]
    tr = _tile(T, 1024, 8)

    def body(x_ref, b_ref, d_ref, p_ref, dpp_in, o_ref, db_ref):
        del dpp_in

        @pl.when(pl.program_id(0) == 0)
        def _():
            db_ref[...] = jnp.zeros_like(db_ref)

        lane = lax.broadcasted_iota(jnp.int32, (tr, 128), 1)
        ddt = _dot(d_ref[...], p_ref[...], ((1,), (1,)), HI)
        dr = jnp.where(lane < NH, ddt * _sig(x_ref[...] + b_ref[...]), 0.0)
        db_ref[...] += jnp.sum(dr, axis=0, keepdims=True)
        o_ref[...] = dr.astype(BF16)

    col = pl.BlockSpec((tr, 128), lambda i: (i, O_DT // 128))
    vec = pl.BlockSpec((1, 128), lambda i: (0, 0))
    return pl.pallas_call(
        body, grid=(T // tr,),
        in_specs=[col, vec, pl.BlockSpec((tr, NG * 128), lambda i: (i, 0)), pl.BlockSpec((128, NG * 128), lambda i: (0, 0)),
                  pl.BlockSpec(memory_space=pl.ANY)],
        out_specs=[col, vec],
        out_shape=[jax.ShapeDtypeStruct(dpp.shape, dpp.dtype), jax.ShapeDtypeStruct((1, 128), F32)],
        input_output_aliases={4: 0}, compiler_params=_cp("arbitrary"),
        name=name)(pp, bias128, ddtg, _head_group_matrix(), dpp)


def _tril():
    return lax.broadcasted_iota(jnp.int32, (CH, CH), 0) >= lax.broadcasted_iota(jnp.int32, (CH, CH), 1)


def _ssd_common(dt, dtt, arow, acol):
    ri = lax.broadcasted_iota(jnp.int32, (CH, CH), 0)
    ci = lax.broadcasted_iota(jnp.int32, (CH, CH), 1)
    tril = ri >= ci
    triu = ri <= ci
    acs_col = _nn(tril.astype(F32), dt * arow, HI)
    acs_row = _nn(dtt * acol, triu.astype(F32), HI)
    return tril, triu, acs_col, acs_row


def _pair_terms(q, dt, acs_col, acs_row, tril, lo):
    ha, hb = 2 * q, 2 * q + 1
    col_a, col_b = acs_col[:, ha:ha + 1], acs_col[:, hb:hb + 1]
    row_a, row_b = acs_row[ha:ha + 1, :], acs_row[hb:hb + 1, :]
    last_a, last_b = acs_col[CH - 1:CH, ha:ha + 1], acs_col[CH - 1:CH, hb:hb + 1]
    out = dict(
        dtsel=jnp.where(lo, dt[:, ha:ha + 1], dt[:, hb:hb + 1]),
        d_a=jnp.exp(jnp.where(tril, col_a - row_a, NEG)), d_b=jnp.exp(jnp.where(tril, col_b - row_b, NEG)),
        esel=jnp.where(lo, jnp.exp(col_a), jnp.exp(col_b)),
        fsel=jnp.where(lo, jnp.exp(last_a - col_a), jnp.exp(last_b - col_b)),
        g_a=jnp.exp(last_a), g_b=jnp.exp(last_b))
    return out


def _ssd_fwd(xc, pp, dtg, acsg, acst, dexp, ng, bl, s, *, name):
    nc = s // CH
    T = bl * s

    def body(xs_ref, bm_ref, cm_ref, z_ref, dt_ref, acs_col_ref, acs_row_ref, dexp_ref, ng_ref,
             y_ref, yn_ref, prev_ref, st_ref):
        @pl.when(pl.program_id(2) == 0)
        def _():
            st_ref[...] = jnp.zeros_like(st_ref)

        dt, acs_col, acs_row, tril = dt_ref[...], acs_col_ref[...], acs_row_ref[...], _tril()
        bm, cm = bm_ref[...].astype(MXU), cm_ref[...].astype(MXU)
        cb = _nt(cm, bm)
        lo = lax.broadcasted_iota(jnp.int32, (CH, 128), 1) < HD
        sub_lo = lax.broadcasted_iota(jnp.int32, (128, NS), 0) < HD
        for q in range(4):
            t = _pair_terms(q, dt, acs_col, acs_row, tril, lo)
            x = xs_ref[:, 128 * q:128 * (q + 1)]
            xd = x * t["dtsel"]
            y = (_nn((cb * t["d_a"]).astype(MXU), jnp.where(lo, xd, 0.0).astype(MXU))
                 + _nn((cb * t["d_b"]).astype(MXU), jnp.where(lo, 0.0, xd).astype(MXU)))
            prev = st_ref[q]
            prev_ref[q] = prev
            y = y + t["esel"] * _nt(cm, prev.astype(MXU))
            st_ref[q] = prev * jnp.where(sub_lo, t["g_a"], t["g_b"]) + _tn((xd * t["fsel"]).astype(MXU), bm)
            y_ref[:, 128 * q:128 * (q + 1)] = y + dexp_ref[:, 128 * q:128 * (q + 1)] * x
        zv = z_ref[...]
        yg = y_ref[...] * (zv * _sig(zv))
        r = lax.rsqrt(jnp.mean(yg * yg, axis=-1, keepdims=True) + EPS)
        yn_ref[...] = (yg * r * ng_ref[...]).astype(BF16)

    def row(width, off_blocks):
        return pl.BlockSpec((CH, width), lambda g, b, c: (b * nc + c, off_blocks + g))

    return pl.pallas_call(
        body, grid=(NG, bl, nc),
        in_specs=[row(GW, 0), row(NS, DI // NS), row(NS, DI // NS + NG), row(GW, O_Z // GW), row(128, 0), row(128, 0),
                  pl.BlockSpec((None, 8, CH), lambda g, b, c: (g, 0, b * nc + c)),
                  pl.BlockSpec((1, GW), lambda g, b, c: (0, g)), pl.BlockSpec((1, GW), lambda g, b, c: (0, g))],
        out_specs=[row(GW, 0), row(GW, 0),
                   pl.BlockSpec((None, 4, 128, NS), lambda g, b, c: (b * nc + c, g, 0, 0))],
        out_shape=[jax.ShapeDtypeStruct((T, DI), F32), jax.ShapeDtypeStruct((T, DI), BF16),
                   jax.ShapeDtypeStruct((bl * nc, 16, 128, NS), F32)],
        scratch_shapes=[pltpu.VMEM((4, 128, NS), F32)],
        compiler_params=_cp("parallel", "parallel", "arbitrary"), name=name,
    )(xc, xc, xc, pp, dtg, acsg, acst, dexp, ng)


def _ssd_bwd(dyn, y, xc, pp, dtg, dtt, arow, acol, dexp, ng, prev, dpp, bl, s, *, name):
    nc = s // CH
    T = bl * s

    def rsum(v):
        return jnp.sum(v, axis=1, keepdims=True)

    def asum(v):
        return jnp.sum(jnp.sum(v, axis=0, keepdims=True), axis=1, keepdims=True)

    def body(dyn_ref, y_ref, xs_ref, bm_ref, cm_ref, z_ref, dt_ref, dtt_ref, arow_ref, acol_ref, dexp_ref, ng_ref,
             prev_ref, dpp_in, dz_ref, dxs_ref, db_ref, dc_ref, ddt_ref, dng_ref, dd_ref, dal_ref, dst_ref):
        del dpp_in

        @pl.when((pl.program_id(1) == 0) & (pl.program_id(2) == 0))
        def _():
            dng_ref[...] = jnp.zeros_like(dng_ref)
            dd_ref[...] = jnp.zeros_like(dd_ref)
            dal_ref[...] = jnp.zeros_like(dal_ref)

        @pl.when(pl.program_id(2) == 0)
        def _():
            dst_ref[...] = jnp.zeros_like(dst_ref)

        yv, zv, xsv, dexp_v = y_ref[...], z_ref[...], xs_ref[...], dexp_ref[...]
        sz = _sig(zv)
        silu = zv * sz
        yg = yv * silu
        r = lax.rsqrt(jnp.mean(yg * yg, axis=-1, keepdims=True) + EPS)
        yh = yg * r
        dynv = dyn_ref[...]
        dng_ref[...] += jnp.sum(dynv * yh, axis=0, keepdims=True)
        dyh = dynv * ng_ref[...]
        dyg = r * (dyh - yh * jnp.mean(dyh * yh, axis=-1, keepdims=True))
        dz_ref[...] = (dyg * yv * (sz * (1.0 + zv * (1.0 - sz)))).astype(BF16)
        dy_all = dyg * silu
        dd_ref[...] += jnp.sum(dy_all * xsv, axis=0, keepdims=True)

        dt = dt_ref[...]
        arow_v = -jnp.exp(arow_ref[...])
        tril, triu, acs_col, acs_row = _ssd_common(dt, dtt_ref[...], arow_v, -jnp.exp(acol_ref[...]))
        bm, cm = bm_ref[...].astype(MXU), cm_ref[...].astype(MXU)
        cb = _nt(cm, bm)
        lane = lax.broadcasted_iota(jnp.int32, (CH, 128), 1)
        is_last = lax.broadcasted_iota(jnp.int32, (CH, 128), 0) == CH - 1
        lo = lane < HD
        sub_lo = lax.broadcasted_iota(jnp.int32, (128, NS), 0) < HD
        dcb = jnp.zeros((CH, CH), F32)
        dc_acc = jnp.zeros((CH, NS), F32)
        db_acc = jnp.zeros((CH, NS), F32)
        dacs = jnp.zeros((CH, 128), F32)
        ddtx = jnp.zeros((CH, 128), F32)
        csum = jnp.zeros((8, CH), F32)
        sub8 = lax.broadcasted_iota(jnp.int32, (8, CH), 0)
        for q in range(4):
            ha, hb = 2 * q, 2 * q + 1
            sl = slice(128 * q, 128 * (q + 1))
            t = _pair_terms(q, dt, acs_col, acs_row, tril, lo)
            x, dy = xsv[:, sl], dy_all[:, sl]
            xd = x * t["dtsel"]
            xd_m = xd.astype(MXU)
            dy_lo, dy_hi = jnp.where(lo, dy, 0.0).astype(MXU), jnp.where(lo, 0.0, dy).astype(MXU)
            m_a, m_b = cb * t["d_a"], cb * t["d_b"]
            prev_m = prev_ref[q].astype(MXU)
            dnext = dst_ref[q]
            dnext_m = dnext.astype(MXU)
            bds = _nt(bm, dnext_m)
            dxd = _tn(m_a.astype(MXU), dy_lo) + _tn(m_b.astype(MXU), dy_hi) + t["fsel"] * bds
            dye_m = (dy * t["esel"]).astype(MXU)
            dst_ref[q] = dnext * jnp.where(sub_lo, t["g_a"], t["g_b"]) + _tn(dye_m, cm)
            dm_a, dm_b = _nt(dy_lo, xd_m), _nt(dy_hi, xd_m)
            dcb = dcb + dm_a * t["d_a"] + dm_b * t["d_b"]
            g_a, g_b = dm_a * m_a, dm_b * m_b
            csum = (csum + jnp.where(sub8 == ha, jnp.sum(g_a, axis=0, keepdims=True), 0.0)
                    + jnp.where(sub8 == hb, jnp.sum(g_b, axis=0, keepdims=True), 0.0))
            tf = t["fsel"] * xd * bds
            tyf = dy * (t["esel"] * _nt(cm, prev_m)) - tf
            dpp = dnext * prev_ref[q]
            ea = asum(jnp.where(lo, tf, 0.0)) + t["g_a"] * asum(jnp.where(sub_lo, dpp, 0.0))
            eb = asum(jnp.where(lo, 0.0, tf)) + t["g_b"] * asum(jnp.where(sub_lo, 0.0, dpp))
            ra = rsum(g_a + jnp.where(lo, tyf, 0.0)) + jnp.where(is_last, ea, 0.0)
            rb = rsum(g_b + jnp.where(lo, 0.0, tyf)) + jnp.where(is_last, eb, 0.0)
            dacs = dacs + jnp.where(lane == ha, ra, 0.0) + jnp.where(lane == hb, rb, 0.0)
            tx = dxd * x
            ddtx = (ddtx + jnp.where(lane == ha, rsum(jnp.where(lo, tx, 0.0)), 0.0)
                    + jnp.where(lane == hb, rsum(jnp.where(lo, 0.0, tx)), 0.0))
            dxs_ref[:, sl] = dxd * t["dtsel"] + dexp_v[:, sl] * dy
            dc_acc = dc_acc + _nn(dye_m, prev_m)
            db_acc = db_acc + _nn((xd * t["fsel"]).astype(MXU), dnext_m)
        dcb_m = dcb.astype(MXU)
        dc_ref[...] = dc_acc + _nn(dcb_m, bm)
        db_ref[...] = db_acc + _tn(dcb_m, cm)
        dacs = dacs - jnp.concatenate([csum, jnp.zeros((CH - 8, CH), F32)], axis=0).T
        dla = _nn(triu.astype(F32), dacs, HI)
        ddt_ref[...] = dla * arow_v + ddtx
        dal_ref[...] += jnp.sum(dla * dt, axis=0, keepdims=True) * arow_v

    def row(width, off_blocks):
        return pl.BlockSpec((CH, width), lambda g, b, c: (b * nc + nc - 1 - c, off_blocks + g))

    gvec = pl.BlockSpec((1, GW), lambda g, b, c: (0, g))
    hvec = pl.BlockSpec((1, 128), lambda g, b, c: (0, g))
    return pl.pallas_call(
        body, grid=(NG, bl, nc),
        in_specs=[row(GW, 0), row(GW, 0), row(GW, 0), row(NS, DI // NS), row(NS, DI // NS + NG), row(GW, O_Z // GW),
                  row(128, 0), pl.BlockSpec((None, 8, CH), lambda g, b, c: (g, 0, b * nc + nc - 1 - c)),
                  hvec, pl.BlockSpec((None, 8, 1), lambda g, b, c: (g, 0, 0)), gvec, gvec,
                  pl.BlockSpec((None, 4, 128, NS), lambda g, b, c: (b * nc + nc - 1 - c, g, 0, 0)),
                  pl.BlockSpec(memory_space=pl.ANY)],
        out_specs=[row(GW, O_Z // GW), row(GW, 0), row(NS, 0), row(NS, 0), row(128, 0), gvec, gvec, hvec],
        input_output_aliases={13: 0},
        out_shape=[jax.ShapeDtypeStruct(dpp.shape, dpp.dtype), jax.ShapeDtypeStruct((T, DI), F32),
                   jax.ShapeDtypeStruct((T, NG * NS), F32), jax.ShapeDtypeStruct((T, NG * NS), F32),
                   jax.ShapeDtypeStruct((T, NG * 128), F32), jax.ShapeDtypeStruct((1, DI), F32),
                   jax.ShapeDtypeStruct((1, DI), F32), jax.ShapeDtypeStruct((1, NG * 128), F32)],
        scratch_shapes=[pltpu.VMEM((4, 128, NS), F32)],
        compiler_params=_cp("arbitrary", "arbitrary", "arbitrary"), name=name,
    )(dyn, y, xc, xc, xc, pp, dtg, dtt, arow, acol, dexp, ng, prev, dpp)


def _merge_fwd(pp, ya, yb, *, name):
    T = ya.shape[0]
    tr = _tile(T, 512, 8)

    def body(ga_ref, gb_ref, ya_ref, yb_ref, o_ref):
        o_ref[...] = (_sig(ga_ref[...]) * ya_ref[...] + _sig(gb_ref[...]) * yb_ref[...]).astype(BF16)

    row = pl.BlockSpec((tr, D), lambda i: (i, 0))
    return pl.pallas_call(
        body, grid=(T // tr,),
        in_specs=[pl.BlockSpec((tr, D), lambda i: (i, O_GA // D)), pl.BlockSpec((tr, D), lambda i: (i, O_GB // D)),
                  row, row],
        out_specs=row, out_shape=jax.ShapeDtypeStruct((T, D), BF16), compiler_params=_cp("parallel"),
        name=name)(pp, pp, ya, yb)


def _merge_bwd(pp, ya, yb, dm, *, name):
    T = ya.shape[0]
    tr = _tile(T, 512, 8)
    assert O_GB == O_GA + D and O_GA % (2 * D) == 0

    def body(g_ref, ya_ref, yb_ref, dm_ref, dya_ref, dyb_ref, dg_ref):
        sa, sb, dmv = _sig(g_ref[:, :D]), _sig(g_ref[:, D:]), dm_ref[...]
        dya_ref[...] = (dmv * sa).astype(BF16)
        dyb_ref[...] = (dmv * sb).astype(BF16)
        dg_ref[:, :D] = (dmv * ya_ref[...] * (sa * (1.0 - sa))).astype(BF16)
        dg_ref[:, D:] = (dmv * yb_ref[...] * (sb * (1.0 - sb))).astype(BF16)

    row = pl.BlockSpec((tr, D), lambda i: (i, 0))
    gates = pl.BlockSpec((tr, 2 * D), lambda i: (i, O_GA // (2 * D)))
    act = jax.ShapeDtypeStruct((T, D), BF16)
    return pl.pallas_call(
        body, grid=(T // tr,), in_specs=[gates, row, row, row], out_specs=[row, row, gates],
        out_shape=[act, act, jax.ShapeDtypeStruct((T, NPP), BF16)], compiler_params=_cp("parallel"),
        name=name)(pp, ya, yb, dm)


def _softmax_rows(sc):
    e = jnp.exp(sc - jnp.max(sc, axis=-1, keepdims=True))
    return e / jnp.sum(e, axis=-1, keepdims=True)


def _attn_fwd(q, kv, bl, s, *, name):
    m = kv.shape[0] // bl
    tq = _tile(s, 512)
    nq = s // tq
    scale = 1.0 / math.sqrt(XD)

    def body(q_ref, k_ref, v_ref, o_ref):
        p = _softmax_rows(_nt(q_ref[...], k_ref[...]) * scale)
        o_ref[...] = _nn(p.astype(MXU), v_ref[...]).astype(BF16)

    qspec = pl.BlockSpec((tq, XD), lambda b, h, i: (b * nq + i, h))
    return pl.pallas_call(
        body, grid=(bl, XH, nq),
        in_specs=[qspec, pl.BlockSpec((m, XD), lambda b, h, i: (b, h)),
                  pl.BlockSpec((m, XD), lambda b, h, i: (b, XH + h))],
        out_specs=qspec, out_shape=jax.ShapeDtypeStruct((bl * s, D), BF16),
        compiler_params=_cp("parallel", "parallel", "parallel"), name=name)(q, kv, kv)


def _attn_bwd(q, kv, do, bl, s, *, name):
    m = kv.shape[0] // bl
    tq = _tile(s, 512)
    nq = s // tq
    scale = 1.0 / math.sqrt(XD)

    def body(q_ref, k_ref, v_ref, do_ref, dq_ref, dk_ref, dv_ref):
        @pl.when(pl.program_id(2) == 0)
        def _():
            dk_ref[...] = jnp.zeros_like(dk_ref)
            dv_ref[...] = jnp.zeros_like(dv_ref)

        qv, kvv, vv, dov = q_ref[...], k_ref[...], v_ref[...], do_ref[...]
        p = _softmax_rows(_nt(qv, kvv) * scale)
        dp = _nt(dov, vv)
        ds = (p * (dp - jnp.sum(dp * p, axis=-1, keepdims=True)) * scale).astype(MXU)
        dq_ref[...] = _nn(ds, kvv).astype(BF16)
        dk_ref[...] += _tn(ds, qv)
        dv_ref[...] += _tn(p.astype(MXU), dov)

    qspec = pl.BlockSpec((tq, XD), lambda b, h, i: (b * nq + i, h))
    kspec = pl.BlockSpec((m, XD), lambda b, h, i: (b, h))
    return pl.pallas_call(
        body, grid=(bl, XH, nq),
        in_specs=[qspec, kspec, pl.BlockSpec((m, XD), lambda b, h, i: (b, XH + h)), qspec],
        out_specs=[qspec, kspec, kspec],
        out_shape=[jax.ShapeDtypeStruct((bl * s, D), BF16), jax.ShapeDtypeStruct((bl * m, D), F32),
                   jax.ShapeDtypeStruct((bl * m, D), F32)],
        compiler_params=_cp("parallel", "parallel", "arbitrary"), name=name)(q, kv, kv, do)


def _row_tile(r, c, max_elems=512 * 1024, align=16):
    best = None
    for t in range(align, r + 1, align):
        if r % t == 0 and t * c <= max_elems:
            best = t
    return best if best is not None else r


def _addn(a, others, *, name, also_bf16=False):
    r, c = a.shape
    tr = _row_tile(r, c)
    n = len(others)

    def body(*refs):
        acc = refs[0][...].astype(F32)
        for o_ref in refs[1:1 + n]:
            acc = acc + o_ref[...].astype(F32)
        refs[1 + n][...] = acc
        if also_bf16:
            refs[2 + n][...] = acc.astype(BF16)

    spec = pl.BlockSpec((tr, c), lambda i: (i, 0))
    shapes = [jax.ShapeDtypeStruct((r, c), F32)] + ([jax.ShapeDtypeStruct((r, c), BF16)] if also_bf16 else [])
    out = pl.pallas_call(
        body, grid=(r // tr,), in_specs=[spec] * (1 + n), out_specs=[spec] * len(shapes), out_shape=shapes,
        compiler_params=_cp("parallel"), name=name)(a, *others)
    return out if also_bf16 else out[0]


def _sum_leading(a, *, name):
    n, r, c = a.shape

    def body(a_ref, o_ref):
        acc = a_ref[0]
        for i in range(1, n):
            acc = acc + a_ref[i]
        o_ref[...] = acc

    return pl.pallas_call(body, out_shape=jax.ShapeDtypeStruct((r, c), F32), name=name)(a)


def _adamw_math(wv, gv, mv, vv):
    m2 = ADAM_B1 * mv + (1.0 - ADAM_B1) * gv
    v2 = ADAM_B2 * vv + (1.0 - ADAM_B2) * (gv * gv)
    m_hat = m2 / (1.0 - ADAM_B1 ** ADAM_STEP)
    v_hat = v2 / (1.0 - ADAM_B2 ** ADAM_STEP)
    return -ADAM_LR * (m_hat / (jnp.sqrt(v_hat) + ADAM_EPS) + ADAM_WD * wv), m2, v2


def _adamw(w, g, m, v, *, name):
    r, c = w.shape
    tr = _row_tile(r, c, align=8)

    def body(w_ref, g_ref, m_ref, v_ref, d_ref, mo_ref, vo_ref):
        d_ref[...], mo_ref[...], vo_ref[...] = _adamw_math(w_ref[...], g_ref[...], m_ref[...], v_ref[...])

    spec = pl.BlockSpec((tr, c), lambda i: (i, 0))
    shp = jax.ShapeDtypeStruct((r, c), F32)
    return pl.pallas_call(
        body, grid=(r // tr,), in_specs=[spec] * 4, out_specs=[spec] * 3, out_shape=[shp] * 3,
        compiler_params=_cp("parallel"), name=name)(w, g, m, v)


def _adamw_halves(w, g_mine, g_other, m, v, c, *, name):
    r, cols = w.shape
    h = r // 2
    tr = _row_tile(h, cols, align=8)
    nh = h // tr

    def body(c_ref, w_ref, gm_ref, go_ref, m_ref, v_ref, g_ref, d_ref, mo_ref, vo_ref):
        gv = jnp.where(pl.program_id(0) // nh == c_ref[0], gm_ref[...], go_ref[...])
        g_ref[...] = gv
        d_ref[...], mo_ref[...], vo_ref[...] = _adamw_math(w_ref[...], gv, m_ref[...], v_ref[...])

    full = pl.BlockSpec((tr, cols), lambda i, c_: (i, 0))
    half = pl.BlockSpec((tr, cols), lambda i, c_: (i % nh, 0))
    shp = jax.ShapeDtypeStruct((r, cols), F32)
    return pl.pallas_call(
        body,
        grid_spec=pltpu.PrefetchScalarGridSpec(num_scalar_prefetch=1, grid=(2 * nh,),
                                               in_specs=[full, half, half, full, full], out_specs=[full] * 4),
        out_shape=[shp] * 4, compiler_params=_cp("parallel"), name=name,
    )(jnp.reshape(c, (1,)).astype(jnp.int32), w, g_mine, g_other, m, v)


def _flip(i, d):
    return 1 - i if d else i


def _comm(name, ins, out_shapes, n_remote, n_local, plan, aliases=None):
    n_in, n_out = len(ins), len(out_shapes)

    def body(*refs):
        in_refs, out_refs = refs[:n_in], refs[n_in:n_in + n_out]
        send_sems, recv_sems = refs[n_in + n_out], refs[n_in + n_out + 1]
        x, y, c = lax.axis_index("x"), lax.axis_index("y"), lax.axis_index("c")
        remote, local = plan(in_refs, out_refs, x, y, c)
        assert len(remote) == n_remote and len(local) == n_local
        copies = []
        if n_local:
            loc_sems = refs[n_in + n_out + 2]
            copies += [pltpu.make_async_copy(s_, d_, loc_sems.at[i]) for i, (s_, d_) in enumerate(local)]
        copies += [pltpu.make_async_remote_copy(src_ref=s_, dst_ref=d_, send_sem=send_sems.at[i],
                                                recv_sem=recv_sems.at[i], device_id=dev, device_id_type=MESH)
                   for i, (s_, d_, dev) in enumerate(remote)]
        for cp in copies:
            cp.start()
        for cp in copies:
            cp.wait()

    hbm = pl.BlockSpec(memory_space=pl.ANY)
    scratch = [pltpu.SemaphoreType.DMA((n_remote,)), pltpu.SemaphoreType.DMA((n_remote,))]
    if n_local:
        scratch.append(pltpu.SemaphoreType.DMA((n_local,)))
    return pl.pallas_call(
        body, in_specs=[hbm] * n_in, out_specs=[hbm] * n_out, out_shape=out_shapes, scratch_shapes=scratch,
        input_output_aliases=aliases or {}, compiler_params=pltpu.CompilerParams(has_side_effects=True),
        name=name)(*ins)


HBM_SPEC = pl.BlockSpec(memory_space=pltpu.HBM)
SEM_SPEC = pl.BlockSpec(memory_space=pltpu.SEMAPHORE)
DATAFLOW = pltpu.SideEffectType.DATAFLOW_SIDE_EFFECTING


def _remote_copies(plan, srcs, lands, send_sems, recv_sems, n_copies):
    x, y, c = lax.axis_index("x"), lax.axis_index("y"), lax.axis_index("c")
    copies = plan(srcs, lands, x, y, c)
    assert len(copies) == n_copies
    return [pltpu.make_async_remote_copy(src_ref=s_, dst_ref=d_, send_sem=send_sems.at[i], recv_sem=recv_sems.at[i],
                                         device_id=dev, device_id_type=MESH) for i, (s_, d_, dev) in enumerate(copies)]


def _split_start(name, srcs, lands, n_copies, plan, after=None):
    ns, nb = len(srcs), len(srcs) + len(lands)
    n_after = 0 if after is None else 1
    n_in = nb + n_after

    def body(*refs):
        for cp in _remote_copies(plan, refs[:ns], refs[ns:nb], refs[n_in], refs[n_in + 1], n_copies):
            cp.start()
        refs[-1][...] = jnp.zeros_like(refs[-1])

    arrays = [pltpu.with_memory_space_constraint(a_, pltpu.HBM) for a_ in list(srcs) + list(lands)]
    out = pl.pallas_call(
        body, name=name,
        out_shape=(pltpu.SemaphoreType.DMA((n_copies,)), pltpu.SemaphoreType.DMA((n_copies,)),
                   *[pltpu.HBM(a_.shape, a_.dtype) for a_ in arrays], jax.ShapeDtypeStruct((8, 128), F32)),
        in_specs=[HBM_SPEC] * nb + [pl.BlockSpec(memory_space=pl.ANY)] * n_after,
        out_specs=(SEM_SPEC, SEM_SPEC, *[HBM_SPEC] * nb, pl.BlockSpec(memory_space=pltpu.VMEM)),
        input_output_aliases={i: 2 + i for i in range(nb)},
        compiler_params=pltpu.CompilerParams(has_side_effects=DATAFLOW))(*arrays, *([after] * n_after))
    return (out[0], out[1], list(out[2:2 + nb])), out[-1]


def _split_wait(name, handle, ns, n_copies, plan, after):
    send_sems, recv_sems, bufs = handle
    nb = len(bufs)

    def body(*refs):
        for cp in _remote_copies(plan, refs[:ns], refs[ns:nb], refs[nb], refs[nb + 1], n_copies):
            cp.wait_send()
            cp.wait_recv()

    out = pl.pallas_call(
        body, name=name, out_shape=[pltpu.HBM(b_.shape, b_.dtype) for b_ in bufs],
        in_specs=[HBM_SPEC] * nb + [SEM_SPEC, SEM_SPEC, pl.BlockSpec(memory_space=pl.ANY)],
        out_specs=[HBM_SPEC] * nb, input_output_aliases={i: i for i in range(nb)},
        compiler_params=pltpu.CompilerParams(has_side_effects=DATAFLOW))(*bufs, send_sems, recv_sems, after)
    return list(out[ns:])


def _gather_start(shards, tag, after=None):
    n = len(shards)
    lands = [lax.empty((4,) + s.shape, s.dtype) for s in shards]

    def plan(srcs, dsts, x, y, c):
        k = 2 * x + y
        copies = []
        for w_ref, o_ref in zip(srcs, dsts):
            h = w_ref.shape[0] // 2
            rows = pl.ds(c * h, h)
            copies += [(w_ref.at[rows], o_ref.at[k, rows], (_flip(x, dx), _flip(y, dy), c)) for dx, dy in CHIP_FLIPS]
        return copies

    handle, token = _split_start(f"gather_{tag}_start", shards, lands, 3 * n, plan, after)
    return (handle, plan, n), token


def _gather_wait(started, after, tag):
    handle, plan, n = started
    return _split_wait(f"gather_{tag}_wait", handle, n, 3 * n, plan, after)


def _gather_d2d(lands, before, tag):
    n = len(lands)

    def plan_d2d(in_refs, out_refs, x, y, c):
        remote = []
        for o_ref in out_refs:
            h = o_ref.shape[1] // 2
            for dx, dy in CHIP_FLIPS:
                half = o_ref.at[2 * _flip(x, dx) + _flip(y, dy), pl.ds(c * h, h)]
                remote.append((half, half, (x, y, 1 - c)))
        return remote, []

    return _comm(f"gather_{tag}_d2d", list(lands) + list(before),
                 [jax.ShapeDtypeStruct(l_.shape, l_.dtype) for l_ in lands], 3 * n, 0, plan_d2d,
                 aliases={i: i for i in range(n)})


def _pair_plan(in_refs, out_refs, x, y, c):
    return [(i_, o_, (x, y, 1 - c)) for i_, o_ in zip(in_refs, out_refs)], []


def _rs_start(grads, tag):
    n = len(grads)
    c = lax.axis_index("c")
    halves = [g.shape[1] // 2 for g in grads]
    mine = [lax.dynamic_slice_in_dim(g, c * h, h, axis=1) for g, h in zip(grads, halves)]
    send_a = [lax.dynamic_slice_in_dim(g, (1 - c) * h, h, axis=1).astype(BF16) for g, h in zip(grads, halves)]
    recv_a = _comm(f"rs_pair_{tag}", send_a, [jax.ShapeDtypeStruct(s.shape, BF16) for s in send_a], n, 0, _pair_plan)
    pair, pair_b = [], []
    for i, (mi, ra) in enumerate(zip(mine, recv_a)):
        four, h, cols = mi.shape
        p32, p16 = _addn(mi.reshape(four * h, cols), [ra.reshape(four * h, cols)], name=f"rs_pair_sum_{tag}_{i}",
                         also_bf16=True)
        pair.append(p32.reshape(four, h, cols))
        pair_b.append(p16.reshape(four, h, cols))

    def plan(srcs, dsts, x, y, c_):
        copies = []
        for i_, o_ in zip(srcs, dsts):
            for j, (dx, dy) in enumerate(CHIP_FLIPS):
                fx, fy = _flip(x, dx), _flip(y, dy)
                copies.append((i_.at[2 * fx + fy], o_.at[j], (fx, fy, c_)))
        return copies

    lands = [lax.empty((3,) + p.shape[1:], BF16) for p in pair_b]
    handle, token = _split_start(f"rs_chips_{tag}_start", pair_b, lands, 3 * n, plan)
    return (handle, plan, n, pair), token


def _rs_finish(started, after, tag):
    handle, plan, n, pair = started
    recv_b = _split_wait(f"rs_chips_{tag}_wait", handle, n, 3 * n, plan, after)
    k = 2 * lax.axis_index("x") + lax.axis_index("y")
    tot = [_addn(lax.dynamic_index_in_dim(p, k, 0, keepdims=False), [rb[0], rb[1], rb[2]],
                 name=f"rs_chip_sum_{tag}_{i}") for i, (p, rb) in enumerate(zip(pair, recv_b))]
    other = _comm(f"rs_halves_{tag}", tot, [jax.ShapeDtypeStruct(t.shape, F32) for t in tot], n, 0, _pair_plan)
    return tot, other


def _gather_all(vec, *, name):
    out = jax.ShapeDtypeStruct((8,) + vec.shape, vec.dtype)

    def plan(in_refs, out_refs, x, y, c):
        me = 4 * x + 2 * y + c
        remote = [(in_refs[0], out_refs[0].at[me], (_flip(x, dx), _flip(y, dy), _flip(c, dc)))
                  for dx in (0, 1) for dy in (0, 1) for dc in (0, 1) if (dx, dy, dc) != (0, 0, 0)]
        return remote, [(in_refs[0], out_refs[0].at[me])]

    return _comm(name, [vec], [out], 7, 1, plan)[0]


def _pack(parts):
    flat = [p.reshape(-1).astype(F32) for p in parts]
    total = sum(f.shape[0] for f in flat)
    n = -(-total // 1024) * 128
    vec = jnp.concatenate(flat + [jnp.zeros((8 * n - total,), F32)]).reshape(8, n)
    offs, o = [], 0
    for f in flat:
        offs.append((o, f.shape[0]))
        o += f.shape[0]
    return vec, offs


def _unpack(vec, offs, shapes):
    flat = vec.reshape(-1)
    return [flat[o:o + n].reshape(s) for (o, n), s in zip(offs, shapes)]


BIG = (("ffn1_w_gate_up", "col"), ("ffn1_w_down", "row"), ("w_in", "col"), ("w_out_a", "row"), ("w_out_ssm", "row"),
       ("w_mix_out", "row"), ("w_q", "row"), ("w_kv", "col"), ("w_o_x", "row"), ("ffn2_w_gate_up", "col"),
       ("ffn2_w_down", "row"))
SMALL = ("ffn1_norm", "mix_norm", "conv_a_w", "ssm_conv_w", "ssm_conv_b", "ssm_dt_bias", "ssm_a_log", "ssm_d",
         "ssm_norm", "xattn_norm", "mem_norm", "ffn2_norm", "final_norm")
WEIGHTS = ("ffn1_norm", "ffn1_w_gate_up", "ffn1_w_down", "mix_norm", "w_in", "conv_a_w", "w_out_a", "ssm_conv_w",
           "ssm_conv_b", "ssm_dt_bias", "ssm_a_log", "ssm_d", "ssm_norm", "w_out_ssm", "w_mix_out", "xattn_norm",
           "mem_norm", "w_q", "w_kv", "w_o_x", "ffn2_norm", "ffn2_w_gate_up", "ffn2_w_down", "final_norm")


GATHER_GROUPS = (("a", ("ffn1_w_gate_up",)), ("b", ("ffn1_w_down", "w_in")),
                 ("c", ("w_out_a", "w_out_ssm", "w_mix_out", "w_q", "w_kv", "w_o_x", "ffn2_w_gate_up", "ffn2_w_down")))


def _place_own(land, own, k, *, name):
    four, r, cols = land.shape
    tr = _row_tile(r, cols)

    def body(k_ref, own_ref, land_in, o_ref):
        del k_ref, land_in
        o_ref[...] = own_ref[...]

    return pl.pallas_call(
        body,
        grid_spec=pltpu.PrefetchScalarGridSpec(
            num_scalar_prefetch=1, grid=(r // tr,),
            in_specs=[pl.BlockSpec((tr, cols), lambda i, k_: (i, 0)), pl.BlockSpec(memory_space=pl.ANY)],
            out_specs=pl.BlockSpec((None, tr, cols), lambda i, k_: (k_[0], i, 0))),
        out_shape=jax.ShapeDtypeStruct(land.shape, land.dtype), input_output_aliases={2: 0},
        compiler_params=_cp("parallel"), name=name)(jnp.reshape(k, (1,)).astype(jnp.int32), own, land)


def _full_weight(land, own, kind, k, *, name):
    land = _place_own(land, own, k, name=name)
    four, r, cols = land.shape
    if kind == "row":
        return land.reshape(four * r, cols)
    return jnp.transpose(land, (1, 0, 2)).reshape(r, four * cols)


class _GatheredWeights:
    def __init__(self, shards32, k, after):
        first = GATHER_GROUPS[0][1]
        self.shards, self.k = {n: shards32[n].astype(BF16) for n in first}, k
        self.full = {}
        self.n_done = 0
        self.started, token = self._start(0, after)
        self.token = token[0, 0]
        self.shards.update({n: (w + token[0, 0]).astype(BF16) for n, w in shards32.items() if n not in first})
        self.after = jnp.stack([self.shards[n][0, 0] for n in shards32 if n not in first]).astype(F32).reshape(1, -1)

    def _start(self, gi, after):
        tag, names = GATHER_GROUPS[gi]
        return _gather_start([self.shards[n] for n in names], tag, after)

    def mark(self, value):
        self.after = value

    def __getitem__(self, name):
        if name not in self.full:
            tag, names = GATHER_GROUPS[self.n_done]
            assert name in names, (name, tag)
            lands = _gather_wait(self.started, self.after, tag)
            before = []
            if self.n_done + 1 < len(GATHER_GROUPS):
                self.started, token = self._start(self.n_done + 1, lands[0])
                before = [token]
            lands = _gather_d2d(lands, before, tag)
            for n, land in zip(names, lands):
                w = _full_weight(land, self.shards[n], dict(BIG)[n], self.k, name=f"own_{n}")
                self.full[n] = _pad_w_in(w) if n == "w_in" else w
            self.n_done += 1
        return self.full[name]


def _shard_major(dw, kind):
    if isinstance(dw, tuple):
        return jnp.concatenate(dw, axis=0)
    if dw.ndim == 3:
        return dw
    if kind == "row":
        return dw.reshape(4, dw.shape[0] // 4, dw.shape[1])
    return jnp.transpose(dw.reshape(dw.shape[0], 4, dw.shape[1] // 4), (1, 0, 2))


def _pad_rows8(w):
    return jnp.concatenate([w, jnp.zeros((8 - w.shape[0], w.shape[1]), w.dtype)], axis=0)


def _group_lanes(v):
    r = v.shape[0]
    return jnp.pad(v.reshape(r, NG, NH // NG), ((0, 0), (0, 0), (0, 128 - NH // NG))).reshape(r, NG * 128)


def _ungroup_lanes(v):
    r = v.shape[0]
    return v.reshape(r, NG, 128)[:, :, :NH // NG].reshape(r, NH)


def _local_step(wfull, small, x, mem, target, token=0.0, on_grads=None):
    bl, s, _ = x.shape
    T = bl * s
    x2, t2 = x.reshape(T, D), target.reshape(T, D)
    mem2 = mem.reshape(-1, D)
    g = {}
    tok = [token]
    mark = getattr(wfull, "mark", lambda value: None)

    def gain(name):
        return small[name].reshape(1, -1) + tok[0]

    def emit(tag, names):
        if on_grads is not None:
            tok[0] = tok[0] + on_grads(tag, {n: g[n] for n in names})

    def ffn_fwd(h, n, wgu, wd, tag, next_gain=None):
        gate, up, a = _gate_up_fwd(n, wfull[wgu], name=f"{tag}_gate_up")
        mark(a)
        out = _mm(a, wfull[wd], "nn", tk=DFF, scale=FFN_RES, residual=h, norm_gain=next_gain, name=f"{tag}_down")
        return out, (n, gate, up, a)

    def ffn_bwd(dh, h, norm, wgu, wd, saved, tag):
        n, gate, up, a = saved
        dgate, dup = _act_bwd(dh, wfull[wd], gate, up, FFN_RES, name=f"{tag}_d_act")
        g[wd] = _mm(a, dh, "tn", tm=1408, scale=FFN_RES, name=f"{tag}_d_w_down")
        g[wgu] = (_mm(n, dgate, "tn", tn=1408, col_shards=2, name=f"{tag}_d_w_gate"),
                  _mm(n, dup, "tn", tn=1408, col_shards=2, name=f"{tag}_d_w_up"))
        emit(tag, (wgu, wd))
        dn = _mm(dgate, wfull[wgu], "nt", a2=dup, tk=1408, name=f"{tag}_d_norm_out")
        dh_in, g[norm] = _norm_bwd(h, gain(norm), dn, dh, name=f"{tag}_d_norm")
        return dh_in

    n1 = _norm_fwd(x2, gain("ffn1_norm"), name="ffn1_norm")
    (h1, u), ffn1_saved = ffn_fwd(x2, n1, "ffn1_w_gate_up", "ffn1_w_down", "ffn1", gain("mix_norm"))
    mark(h1)
    pp = _mm(u, wfull["w_in"], "nn", tn=1152, name="in_proj")
    wa8 = _pad_rows8(small["conv_a_w"])
    ws8 = _pad_rows8(small["ssm_conv_w"])
    conv_b = gain("ssm_conv_b")
    bias128 = jnp.pad(gain("ssm_dt_bias"), ((0, 0), (0, 128 - NH)))
    ya_pre = _conv_a_fwd(pp, wa8, bl, s, name="conv_a")
    xc = _conv_ssm_fwd(pp, ws8, conv_b, bl, s, name="conv_ssm")
    mark(xc)
    alog = gain("ssm_a_log")
    dtg, dtt, acsg, acst = _dt_fwd(pp, bias128, jnp.pad(alog, ((0, 0), (0, 128 - NH))), name="dt")
    arow, acol = _group_lanes(alog), alog.reshape(NG, NH // NG, 1)
    dexp = jnp.repeat(gain("ssm_d"), HD, axis=1)
    ng = gain("ssm_norm")
    y, yn, prev = _ssd_fwd(xc, pp, dtg, acsg, acst, dexp, ng, bl, s, name="ssd")
    ya = _mm(ya_pre, wfull["w_out_a"], "nn", tn=1024, name="out_a")
    yb = _mm(yn, wfull["w_out_ssm"], "nn", tn=1024, tk=DI, name="out_ssm")
    merged = _merge_fwd(pp, ya, yb, name="merge")
    h2, un = _mm(merged, wfull["w_mix_out"], "nn", residual=h1, norm_gain=gain("xattn_norm"), name="mix_out")
    q = _mm(un, wfull["w_q"], "nn", tn=1024, out_dtype=BF16, name="q_proj")
    mn = _norm_fwd(mem2, gain("mem_norm"), name="mem_norm")
    kv = _mm(mn, wfull["w_kv"], "nn", tn=1024, out_dtype=BF16, name="kv_proj")
    o = _attn_fwd(q, kv, bl, s, name="attn")
    h3, n2 = _mm(o, wfull["w_o_x"], "nn", residual=h2, norm_gain=gain("ffn2_norm"), name="attn_out")
    h4, ffn2_saved = ffn_fwd(h3, n2, "ffn2_w_gate_up", "ffn2_w_down", "ffn2")
    sq_err, dh4, dgf = _final_loss(h4, gain("final_norm"), t2, name="final_loss")
    g["final_norm"] = dgf

    dh3 = ffn_bwd(dh4, h3, "ffn2_norm", "ffn2_w_gate_up", "ffn2_w_down", ffn2_saved, "ffn2")
    do = _mm(dh3, wfull["w_o_x"], "nt", tn=1024, out_dtype=BF16, name="d_attn_o")
    g["w_o_x"] = _mm(o, dh3, "tn",name="d_w_o_x")
    dq, dk, dv = _attn_bwd(q, kv, do, bl, s, name="d_attn")
    dun = _mm(dq, wfull["w_q"], "nt", tn=1024, name="d_xattn_norm_out")
    g["w_q"] = _mm(un, dq, "tn",name="d_w_q")
    dkv = jnp.concatenate([dk, dv], axis=1)
    dmn = _mm(dkv, wfull["w_kv"], "nt", tn=1024, tk=2 * D, name="d_mem_norm_out")
    g["w_kv"] = _mm(mn, dkv, "tn", tn=512, col_shards=4, name="d_w_kv")
    emit("attn", ("w_q", "w_kv", "w_o_x"))
    _, g["mem_norm"] = _norm_bwd(mem2, gain("mem_norm"), dmn, None, name="d_mem_norm")
    dh2, g["xattn_norm"] = _norm_bwd(h2, gain("xattn_norm"), dun, dh3, name="d_xattn_norm")
    dmerged = _mm(dh2, wfull["w_mix_out"], "nt", tn=1024, name="d_merged")
    g["w_mix_out"] = _mm(merged, dh2, "tn",name="d_w_mix_out")
    dya, dyb, dpp = _merge_bwd(pp, ya, yb, dmerged, name="d_merge")
    dya_pre = _mm(dya, wfull["w_out_a"], "nt", tn=1024, name="d_conv_a_out")
    g["w_out_a"] = _mm(ya_pre, dya, "tn",name="d_w_out_a")
    dyn = _mm(dyb, wfull["w_out_ssm"], "nt", tn=DI, name="d_ssd_out")
    g["w_out_ssm"] = _mm(yn, dyb, "tn",name="d_w_out_ssm")
    dpp, dwa8 = _conv_a_bwd(pp, wa8, dya_pre, dpp, bl, s, name="d_conv_a")
    g["conv_a_w"] = dwa8[:3]
    dpp, dxs, dbm, dcm, ddtg, g["ssm_norm"], ddexp, dalg = _ssd_bwd(
        dyn, y, xc, pp, dtg, dtt, arow, acol, dexp, ng, prev, dpp, bl, s, name="d_ssd")
    g["ssm_d"] = ddexp.reshape(NH, HD).sum(axis=1).reshape(1, NH)
    g["ssm_a_log"] = _ungroup_lanes(dalg)
    conv_dw, conv_db = [], []
    for dpart, off, tag in ((dxs, 0, "x"), (dbm, DI, "b"), (dcm, DI + NG * NS, "c")):
        dpp, dw_, db_ = _conv_ssm_bwd(pp, ws8, conv_b, dpart, off, dpp, bl, s, name=f"d_conv_ssm_{tag}")
        conv_dw.append(dw_)
        conv_db.append(db_)
    g["ssm_conv_w"] = jnp.concatenate(conv_dw, axis=1)[:4]
    g["ssm_conv_b"] = jnp.concatenate(conv_db, axis=1)
    dpp, dbias = _dt_bwd(pp, bias128, ddtg, dpp, name="d_dt")
    g["ssm_dt_bias"] = dbias[:, :NH]
    g["w_in"] = _mm(u, dpp, "tn", tn=1152, name="d_w_in")
    emit("mix", ("w_in", "w_out_a", "w_out_ssm", "w_mix_out"))
    du = _mm(dpp, wfull["w_in"], "nt", tk=3456, name="d_mix_norm_out")
    dh1, g["mix_norm"] = _norm_bwd(h1, gain("mix_norm"), du, dh2, name="d_mix_norm")
    dx = ffn_bwd(dh1, x2, "ffn1_norm", "ffn1_w_gate_up", "ffn1_w_down", ffn1_saved, "ffn1")
    return sq_err, dx, g


def _pad_w_in(w):
    r = w.shape[0]
    conv_a = jnp.transpose(w[:, :3 * D].reshape(r, 3, D // CA_TILE, CA_TILE), (0, 2, 1, 3)).reshape(r, 3 * D)
    return jnp.concatenate([conv_a, w[:, 3 * D:O_GA], w[:, O_GA + NH:], w[:, O_GA:O_GA + NH],
                            jnp.zeros((r, NPP - NIN), w.dtype)], axis=1)


def _unpad_w_in(w):
    r = w.shape[0]
    conv_a = jnp.transpose(w[:, :3 * D].reshape(r, D // CA_TILE, 3, CA_TILE), (0, 2, 1, 3)).reshape(r, 3 * D)
    return jnp.concatenate([conv_a, w[:, 3 * D:O_GA], w[:, O_DT:O_DT + NH], w[:, O_GA:O_DT]], axis=1)


def kernel(x, mem, ffn1_norm, ffn1_w_gate_up, ffn1_w_down, mix_norm, w_in, conv_a_w, w_out_a, ssm_conv_w, ssm_conv_b, ssm_dt_bias, ssm_a_log, ssm_d, ssm_norm, w_out_ssm, w_mix_out, xattn_norm, mem_norm, w_q, w_kv, w_o_x, ffn2_norm, ffn2_w_gate_up, ffn2_w_down, final_norm, loss_target, m_ffn1_norm, m_ffn1_w_gate_up, m_ffn1_w_down, m_mix_norm, m_w_in, m_conv_a_w, m_w_out_a, m_ssm_conv_w, m_ssm_conv_b, m_ssm_dt_bias, m_ssm_a_log, m_ssm_d, m_ssm_norm, m_w_out_ssm, m_w_mix_out, m_xattn_norm, m_mem_norm, m_w_q, m_w_kv, m_w_o_x, m_ffn2_norm, m_ffn2_w_gate_up, m_ffn2_w_down, m_final_norm, v_ffn1_norm, v_ffn1_w_gate_up, v_ffn1_w_down, v_mix_norm, v_w_in, v_conv_a_w, v_w_out_a, v_ssm_conv_w, v_ssm_conv_b, v_ssm_dt_bias, v_ssm_a_log, v_ssm_d, v_ssm_norm, v_w_out_ssm, v_w_mix_out, v_xattn_norm, v_mem_norm, v_w_q, v_w_kv, v_w_o_x, v_ffn2_norm, v_ffn2_w_gate_up, v_ffn2_w_down, v_final_norm):
    a = dict(locals())
    xi, yi = lax.axis_index("x"), lax.axis_index("y")
    k = 2 * xi + yi

    conv_vec, conv_offs = _pack([a["conv_a_w"], a["ssm_conv_w"]])
    conv_all = _gather_all(conv_vec, name="gather_conv_w")
    wfull = _GatheredWeights({n: a[n][0] for n, _ in BIG}, k, conv_all)
    conv_sh = [_unpack(conv_all[2 * kk], conv_offs, [a["conv_a_w"].shape[1:], a["ssm_conv_w"].shape[1:]])
               for kk in range(4)]
    small = {n: a[n] for n in SMALL}
    small["conv_a_w"] = jnp.concatenate([cs[0] for cs in conv_sh], axis=1)
    small["ssm_conv_w"] = jnp.concatenate([cs[1] for cs in conv_sh], axis=1)

    rs_started = []

    def on_grads(tag, grads):
        names = [n for n, _ in BIG if n in grads]
        shard_major = [_shard_major(_unpad_w_in(grads[n]) if n == "w_in" else grads[n], dict(BIG)[n]) for n in names]
        st, tk = _rs_start(shard_major, tag)
        rs_started.append((tag, names, st))
        return tk[0, 0]

    sq_err, dx, g = _local_step(wfull, small, x, mem, loss_target, wfull.token, on_grads)
    loss = lax.psum(0.5 / D * jnp.sum(sq_err), ("x", "y", "c"))

    ci = lax.axis_index("c")
    out = {}
    for tag, names, st in rs_started:
        g_mine, g_other = _rs_finish(st, dx, tag)
        for n, gm, go in zip(names, g_mine, g_other):
            res = _adamw_halves(a[n][0], gm, go, a["m_" + n][0], a["v_" + n][0], ci, name=f"adamw_{n}")
            out[n] = tuple(t.reshape(a[n].shape) for t in res)

    full_shapes = [g[n].shape for n in SMALL]
    gvec, goffs = _pack([g[n] for n in SMALL])
    gsum = _sum_leading(_gather_all(gvec, name="gather_small_grads"), name="sum_small_grads")
    gsmall = dict(zip(SMALL, _unpack(gsum, goffs, full_shapes)))
    for n in ("conv_a_w", "ssm_conv_w"):
        width = a[n].shape[2]
        gsmall[n] = lax.dynamic_slice_in_dim(gsmall[n], k * width, width, axis=1)
    local_shapes = [a[n].shape for n in SMALL]
    packs = [_pack([t[n] for n in SMALL]) for t in
             ({n: a[n] for n in SMALL}, gsmall, {n: a["m_" + n] for n in SMALL}, {n: a["v_" + n] for n in SMALL})]
    offs = packs[0][1]
    res = _adamw(*[p[0] for p in packs], name="adamw_small")
    unp = [_unpack(r, offs, local_shapes) for r in res]
    for i, n in enumerate(SMALL):
        out[n] = (gsmall[n].reshape(a[n].shape), unp[0][i], unp[1][i], unp[2][i])

    grad_x = dx.reshape(x.shape)
    return (loss, grad_x, *[out[n][0] for n in WEIGHTS], *[out[n][1] for n in WEIGHTS],
            *[out[n][2] for n in WEIGHTS], *[out[n][3] for n in WEIGHTS])
```

```python
import functools
import math

import jax
import jax.numpy as jnp
from jax import lax
from jax.experimental import pallas as pl
from jax.experimental.pallas import tpu as pltpu

F32 = jnp.float32
BF16 = jnp.bfloat16
MXU = jnp.bfloat16
HI = lax.Precision.HIGHEST

D = 1024
DFF = 2816
DI = 2048
NH, HD, NG, NS, CH = 32, 64, 4, 128, 128
GW = DI // NG
XH, XD = 4, 256
EPS = 1e-6
NEG = -1e30
CA_TILE = 256
O_CA, O_Z, O_XBC, O_GA, O_GB, O_DT, NPP = 0, 3072, 5120, 8192, 9216, 10240, 10368
NIN = 10272
FFN_RES = 0.5
ADAM_LR, ADAM_B1, ADAM_B2, ADAM_EPS, ADAM_WD, ADAM_STEP = 0.001, 0.9, 0.999, 1e-08, 0.01, 10
VMEM_LIMIT = 56 * 1024 * 1024
EPI_COLS = 256
MESH = pl.DeviceIdType.MESH
CHIP_FLIPS = ((1, 0), (0, 1), (1, 1))


def _cp(*sem):
    return pltpu.CompilerParams(dimension_semantics=sem, vmem_limit_bytes=VMEM_LIMIT)


def _tile(n, pref, align=128):
    if n <= pref:
        return n
    t = (pref // align) * align
    while t >= align:
        if n % t == 0:
            return t
        t -= align
    raise ValueError((n, pref))


def _dot(a, b, dims, prec=None):
    return lax.dot_general(a, b, (dims, ((), ())), preferred_element_type=F32, precision=prec)


def _nn(a, b, prec=None):
    return _dot(a, b, ((1,), (0,)), prec)


def _nt(a, b):
    return _dot(a, b, ((1,), (1,)))


def _tn(a, b):
    return _dot(a, b, ((0,), (0,)))


def _sig(x):
    return jax.nn.sigmoid(x)


def _mm(a, b, mode, *, name, tm=1024, tn=1024, tk=None, out_dtype=F32, scale=None, residual=None, a2=None,
        col_shards=0, norm_gain=None):
    if tk is None:
        tk = 2048 if mode == "tn" else 1024
    if mode == "nn":
        (M, K), (K2, N) = a.shape, b.shape
    elif mode == "nt":
        (M, K), (N, K2) = a.shape, b.shape
        if a2 is not None:
            assert a2.shape == a.shape
            K2 = K2 // 2
    else:
        (K, M), (K2, N) = a.shape, b.shape
    assert K == K2, (name, a.shape, b.shape)
    tm, tn, tk = _tile(M, tm), _tile(N, tn), _tile(K, tk)
    nk = K // tk
    if mode == "nn":
        a_spec = pl.BlockSpec((tm, tk), lambda i, j, k: (i, k))
        b_spec = pl.BlockSpec((tk, tn), lambda i, j, k: (k, j))
        dims = ((1,), (0,))
    elif mode == "nt":
        a_spec = pl.BlockSpec((tm, tk), lambda i, j, k: (i, k))
        b_spec = pl.BlockSpec((tn, tk), lambda i, j, k: (j, k))
        dims = ((1,), (1,))
    else:
        a_spec = pl.BlockSpec((tk, tm), lambda i, j, k: (k, i))
        b_spec = pl.BlockSpec((tk, tn), lambda i, j, k: (k, j))
        dims = ((0,), (0,))
    o_spec = pl.BlockSpec((tm, tn), lambda i, j, k: (i, j))
    out_spec, out_shape = o_spec, jax.ShapeDtypeStruct((M, N), out_dtype)
    if col_shards:
        per = N // col_shards // tn
        assert per * tn * col_shards == N, (name, N, tn, col_shards)
        out_spec = pl.BlockSpec((None, tm, tn), lambda i, j, k: (j // per, i, j % per))
        out_shape = jax.ShapeDtypeStruct((col_shards, M, N // col_shards), out_dtype)
    has_res = residual is not None
    has_norm = norm_gain is not None
    assert not has_norm or (tn == N and not col_shards)
    dual = a2 is not None
    n_in = 2 + 2 * dual + has_res + has_norm

    def body(*refs):
        a_ref, b_ref = refs[0], refs[1]
        o_ref = refs[n_in]

        def finish(acc):
            if scale is not None:
                acc = acc * scale
            if has_res:
                acc = acc + refs[2 + 2 * dual][...]
            o_ref[...] = acc.astype(out_dtype)
            if has_norm:
                rs = lax.rsqrt(jnp.mean(acc * acc, axis=-1, keepdims=True) + EPS)
                refs[n_in + 1][...] = (acc * rs * refs[n_in - 1][...]).astype(BF16)

        part = _dot(a_ref[...].astype(MXU), b_ref[...].astype(MXU), dims)
        if dual:
            part = part + _dot(refs[2][...].astype(MXU), refs[3][...].astype(MXU), dims)
        if nk == 1:
            finish(part)
            return
        acc_ref = refs[-1]
        k = pl.program_id(2)

        @pl.when(k == 0)
        def _():
            acc_ref[...] = part

        @pl.when(k > 0)
        def _():
            acc_ref[...] += part

        @pl.when(k == nk - 1)
        def _():
            finish(acc_ref[...])

    ins, in_specs = [a, b], [a_spec, b_spec]
    if dual:
        ins += [a2, b]
        in_specs += [a_spec, pl.BlockSpec((tn, tk), lambda i, j, k: (j, k + nk))]
    if has_res:
        ins.append(residual)
        in_specs.append(o_spec)
    if has_norm:
        ins.append(norm_gain)
        in_specs.append(pl.BlockSpec((1, tn), lambda i, j, k: (0, j)))
        out_spec, out_shape = [out_spec, o_spec], [out_shape, jax.ShapeDtypeStruct((M, N), BF16)]
    return pl.pallas_call(
        body, grid=(M // tm, N // tn, nk), in_specs=in_specs, out_specs=out_spec, out_shape=out_shape,
        scratch_shapes=[pltpu.VMEM((tm, tn), F32)] if nk > 1 else [],
        compiler_params=_cp("parallel", "parallel", "arbitrary"), name=name)(*ins)


def _norm_fwd(x, g, *, name):
    T, d = x.shape
    tr = _tile(T, 512, 8)

    def body(x_ref, g_ref, o_ref):
        xv = x_ref[...]
        r = lax.rsqrt(jnp.mean(xv * xv, axis=-1, keepdims=True) + EPS)
        o_ref[...] = (xv * r * g_ref[...]).astype(BF16)

    return pl.pallas_call(
        body, grid=(T // tr,),
        in_specs=[pl.BlockSpec((tr, d), lambda i: (i, 0)), pl.BlockSpec((1, d), lambda i: (0, 0))],
        out_specs=pl.BlockSpec((tr, d), lambda i: (i, 0)),
        out_shape=jax.ShapeDtypeStruct((T, d), BF16), compiler_params=_cp("parallel"), name=name)(x, g)


def _norm_bwd(x, g, dn, dres, *, name):
    T, d = x.shape
    tr = _tile(T, 512, 8)
    has_res = dres is not None

    def body(*refs):
        x_ref, g_ref, dn_ref = refs[:3]
        dr_ref = refs[3] if has_res else None
        dx_ref, dg_ref = refs[-2], refs[-1]

        @pl.when(pl.program_id(0) == 0)
        def _():
            dg_ref[...] = jnp.zeros_like(dg_ref)

        xv = x_ref[...]
        dnv = dn_ref[...].astype(F32)
        r = lax.rsqrt(jnp.mean(xv * xv, axis=-1, keepdims=True) + EPS)
        xh = xv * r
        dg_ref[...] += jnp.sum(dnv * xh, axis=0, keepdims=True)
        dxh = dnv * g_ref[...]
        dx = r * (dxh - xh * jnp.mean(dxh * xh, axis=-1, keepdims=True))
        if has_res:
            dx = dx + dr_ref[...]
        dx_ref[...] = dx

    row = pl.BlockSpec((tr, d), lambda i: (i, 0))
    vec = pl.BlockSpec((1, d), lambda i: (0, 0))
    ins = [x, g, dn] + ([dres] if has_res else [])
    return pl.pallas_call(
        body, grid=(T // tr,), in_specs=[row, vec, row] + ([row] if has_res else []),
        out_specs=[row, vec],
        out_shape=[jax.ShapeDtypeStruct((T, d), F32), jax.ShapeDtypeStruct((1, d), F32)],
        compiler_params=_cp("arbitrary"), name=name)(*ins)


def _final_loss(h, g, target, *, name):
    T, d = h.shape
    tr = _tile(T, 512, 8)

    def body(h_ref, g_ref, t_ref, l_ref, dh_ref, dg_ref):
        @pl.when(pl.program_id(0) == 0)
        def _():
            l_ref[...] = jnp.zeros_like(l_ref)
            dg_ref[...] = jnp.zeros_like(dg_ref)

        xv = h_ref[...]
        r = lax.rsqrt(jnp.mean(xv * xv, axis=-1, keepdims=True) + EPS)
        xh = xv * r
        e = xh * g_ref[...] - t_ref[...]
        l_ref[...] += jnp.sum(e * e, axis=0, keepdims=True)
        dy = e * (1.0 / d)
        dg_ref[...] += jnp.sum(dy * xh, axis=0, keepdims=True)
        dxh = dy * g_ref[...]
        dh_ref[...] = r * (dxh - xh * jnp.mean(dxh * xh, axis=-1, keepdims=True))

    row = pl.BlockSpec((tr, d), lambda i: (i, 0))
    vec = pl.BlockSpec((1, d), lambda i: (0, 0))
    return pl.pallas_call(
        body, grid=(T // tr,), in_specs=[row, vec, row], out_specs=[vec, row, vec],
        out_shape=[jax.ShapeDtypeStruct((1, d), F32), jax.ShapeDtypeStruct((T, d), F32),
                   jax.ShapeDtypeStruct((1, d), F32)],
        compiler_params=_cp("arbitrary"), name=name)(h, g, target)


def _gate_up_fwd(n, wgu, *, name):
    T, d = n.shape
    f = wgu.shape[1] // 2
    tm, tn = _tile(T, 512, 8), _tile(f, DFF)
    nf = f // tn

    tc = _tile(tn, EPI_COLS)

    def body(n_ref, wg_ref, wu_ref, g_ref, u_ref, a_ref):
        nv = n_ref[...].astype(MXU)
        for j in range(tn // tc):
            sl = slice(j * tc, (j + 1) * tc)
            gv = _nn(nv, wg_ref[:, sl].astype(MXU))
            uv = _nn(nv, wu_ref[:, sl].astype(MXU))
            g_ref[:, sl] = gv.astype(BF16)
            u_ref[:, sl] = uv.astype(BF16)
            a_ref[:, sl] = (gv * _sig(gv) * uv).astype(BF16)

    out = pl.BlockSpec((tm, tn), lambda i, j: (i, j))
    act = jax.ShapeDtypeStruct((T, f), BF16)
    return pl.pallas_call(
        body, grid=(T // tm, nf),
        in_specs=[pl.BlockSpec((tm, d), lambda i, j: (i, 0)), pl.BlockSpec((d, tn), lambda i, j: (0, j)),
                  pl.BlockSpec((d, tn), lambda i, j: (0, j + nf))],
        out_specs=[out, out, out], out_shape=[act, act, act], compiler_params=_cp("parallel", "parallel"),
        name=name)(n, wgu, wgu)


def _act_bwd(dh, wd, gate, up, scale, *, name):
    T, d = dh.shape
    f = wd.shape[0]
    tm, tn = _tile(T, 512, 8), _tile(f, DFF)

    tc = _tile(tn, EPI_COLS)

    def body(dh_ref, wd_ref, g_ref, u_ref, dg_ref, du_ref):
        dhv = dh_ref[...].astype(MXU)
        for j in range(tn // tc):
            sl = slice(j * tc, (j + 1) * tc)
            da = scale * _nt(dhv, wd_ref[sl, :].astype(MXU))
            gv, uv = g_ref[:, sl].astype(F32), u_ref[:, sl].astype(F32)
            s = _sig(gv)
            dg_ref[:, sl] = (da * uv * (s * (1.0 + gv * (1.0 - s)))).astype(BF16)
            du_ref[:, sl] = (da * (gv * s)).astype(BF16)

    tile = pl.BlockSpec((tm, tn), lambda i, j: (i, j))
    act = jax.ShapeDtypeStruct((T, f), BF16)
    return pl.pallas_call(
        body, grid=(T // tm, f // tn),
        in_specs=[pl.BlockSpec((tm, d), lambda i, j: (i, 0)), pl.BlockSpec((tn, d), lambda i, j: (j, 0)), tile, tile],
        out_specs=[tile, tile], out_shape=[act, act], compiler_params=_cp("parallel", "parallel"),
        name=name)(dh, wd, gate, up)


CONV_ROWS = 64
CONV_PAD = 8


def _rows_down(ref, r0, d, cols=slice(None)):
    if r0 - d >= 0:
        return ref[pl.ds(r0 - d, CONV_ROWS), cols]
    assert r0 == 0
    v = ref[pl.ds(0, CONV_ROWS), cols]
    ri = lax.broadcasted_iota(jnp.int32, v.shape, 0)
    return jnp.where(ri >= d, pltpu.roll(v, d, 0), 0.0)


def _fold8(v):
    return jnp.sum(v.reshape(CONV_ROWS // 8, 8, v.shape[1]), axis=0)


def _taps(w_ref, views):
    acc = None
    for k, v in enumerate(views):
        t = w_ref[k:k + 1, :] * v
        acc = t if acc is None else acc + t
    return acc


def _conv_a_fwd(pp, w8, bl, s, *, name):
    tc = CA_TILE
    nb = D // tc
    bcol, ccol, vcol = slice(0, tc), slice(tc, 2 * tc), slice(2 * tc, 3 * tc)

    def body(p_ref, w_ref, o_ref):
        for r0 in range(0, s, CONV_ROWS):
            cv = [_rows_down(p_ref, r0, 2 - k, ccol) * _rows_down(p_ref, r0, 2 - k, vcol) for k in range(3)]
            o_ref[pl.ds(r0, CONV_ROWS), :] = (p_ref[pl.ds(r0, CONV_ROWS), bcol] * _taps(w_ref, cv)).astype(BF16)

    return pl.pallas_call(
        body, grid=(bl, nb),
        in_specs=[pl.BlockSpec((s, 3 * tc), lambda b, j: (b, j)), pl.BlockSpec((8, tc), lambda b, j: (0, j))],
        out_specs=pl.BlockSpec((s, tc), lambda b, j: (b, j)),
        out_shape=jax.ShapeDtypeStruct((bl * s, D), BF16), compiler_params=_cp("parallel", "parallel"),
        name=name)(pp, w8)


def _conv_a_bwd(pp, w8, dya, dpp, bl, s, *, name):
    tc = CA_TILE
    nb = D // tc
    bcol, ccol, vcol = slice(0, tc), slice(tc, 2 * tc), slice(2 * tc, 3 * tc)

    def body(p_ref, w_ref, dy_ref, dpp_in, d_ref, dw_ref, dcp):
        del dpp_in

        @pl.when(pl.program_id(1) == 0)
        def _():
            dw_ref[...] = jnp.zeros_like(dw_ref)

        dcp[pl.ds(s, CONV_PAD), :] = jnp.zeros((CONV_PAD, tc), F32)
        dw_acc = [jnp.zeros((8, tc), F32) for _ in range(3)]
        for r0 in reversed(range(0, s, CONV_ROWS)):
            rows = pl.ds(r0, CONV_ROWS)
            cs = [_rows_down(p_ref, r0, 2 - k, ccol) for k in range(3)]
            vs = [_rows_down(p_ref, r0, 2 - k, vcol) for k in range(3)]
            cv = [c_ * v_ for c_, v_ in zip(cs, vs)]
            dy = dy_ref[rows, :]
            d_ref[rows, bcol] = (dy * _taps(w_ref, cv)).astype(BF16)
            dconv = dy * p_ref[rows, bcol]
            dcp[rows, :] = dconv
            dcv = _taps(w_ref, [dcp[pl.ds(r0 + 2, CONV_ROWS), :], dcp[pl.ds(r0 + 1, CONV_ROWS), :], dconv])
            d_ref[rows, ccol] = (dcv * vs[2]).astype(BF16)
            d_ref[rows, vcol] = (dcv * cs[2]).astype(BF16)
            dw_acc = [acc + _fold8(dconv * cv_) for acc, cv_ in zip(dw_acc, cv)]
        for k in range(3):
            dw_ref[k:k + 1, :] += jnp.sum(dw_acc[k], axis=0, keepdims=True)

    wspec = pl.BlockSpec((8, tc), lambda j, b: (0, j))
    wide = pl.BlockSpec((s, 3 * tc), lambda j, b: (b, j))
    return pl.pallas_call(
        body, grid=(nb, bl),
        in_specs=[wide, wspec, pl.BlockSpec((s, tc), lambda j, b: (b, j)), pl.BlockSpec(memory_space=pl.ANY)],
        out_specs=[wide, wspec], out_shape=[jax.ShapeDtypeStruct(dpp.shape, dpp.dtype), jax.ShapeDtypeStruct((8, D), F32)],
        scratch_shapes=[pltpu.VMEM((s + CONV_PAD, tc), F32)], input_output_aliases={3: 0},
        compiler_params=_cp("parallel", "arbitrary"), name=name)(pp, w8, dya, dpp)


def _conv_ssm_fwd(pp, w8, bias, bl, s, *, name):
    tc = 256
    width = DI + 2 * NG * NS
    nb = width // tc

    def body(x_ref, w_ref, b_ref, o_ref):
        for r0 in range(0, s, CONV_ROWS):
            pre = _taps(w_ref, [_rows_down(x_ref, r0, 3 - k) for k in range(4)]) + b_ref[...]
            o_ref[pl.ds(r0, CONV_ROWS), :] = pre * _sig(pre)

    return pl.pallas_call(
        body, grid=(bl, nb),
        in_specs=[pl.BlockSpec((s, tc), lambda b, j: (b, O_XBC // tc + j)),
                  pl.BlockSpec((8, tc), lambda b, j: (0, j)), pl.BlockSpec((1, tc), lambda b, j: (0, j))],
        out_specs=pl.BlockSpec((s, tc), lambda b, j: (b, j)),
        out_shape=jax.ShapeDtypeStruct((bl * s, width), F32), compiler_params=_cp("parallel", "parallel"),
        name=name)(pp, w8, bias)


def _conv_ssm_bwd(pp, w8, bias, dxc, ch_off, dpp, bl, s, *, name):
    n = dxc.shape[1]
    tc = 256
    nb = n // tc
    o0 = ch_off // tc

    def body(x_ref, w_ref, b_ref, d_ref, dpp_in, dx_ref, dw_ref, db_ref, dp):
        del dpp_in

        @pl.when(pl.program_id(1) == 0)
        def _():
            dw_ref[...] = jnp.zeros_like(dw_ref)
            db_ref[...] = jnp.zeros_like(db_ref)

        dp[pl.ds(s, CONV_PAD), :] = jnp.zeros((CONV_PAD, tc), F32)
        dw_acc = [jnp.zeros((8, tc), F32) for _ in range(4)]
        db_acc = jnp.zeros((8, tc), F32)
        for r0 in reversed(range(0, s, CONV_ROWS)):
            rows = pl.ds(r0, CONV_ROWS)
            xs = [_rows_down(x_ref, r0, 3 - k) for k in range(4)]
            pre = _taps(w_ref, xs) + b_ref[...]
            sg = _sig(pre)
            dpre = d_ref[rows, :] * (sg * (1.0 + pre * (1.0 - sg)))
            dp[rows, :] = dpre
            dx = _taps(w_ref, [dp[pl.ds(r0 + 3 - k, CONV_ROWS), :] for k in range(3)] + [dpre])
            dx_ref[rows, :] = dx.astype(BF16)
            db_acc = db_acc + _fold8(dpre)
            dw_acc = [acc + _fold8(dpre * x_) for acc, x_ in zip(dw_acc, xs)]
        db_ref[...] += jnp.sum(db_acc, axis=0, keepdims=True)
        for k in range(4):
            dw_ref[k:k + 1, :] += jnp.sum(dw_acc[k], axis=0, keepdims=True)

    return pl.pallas_call(
        body, grid=(nb, bl),
        in_specs=[pl.BlockSpec((s, tc), lambda j, b: (b, O_XBC // tc + o0 + j)),
                  pl.BlockSpec((8, tc), lambda j, b: (0, o0 + j)), pl.BlockSpec((1, tc), lambda j, b: (0, o0 + j)),
                  pl.BlockSpec((s, tc), lambda j, b: (b, j)), pl.BlockSpec(memory_space=pl.ANY)],
        out_specs=[pl.BlockSpec((s, tc), lambda j, b: (b, O_XBC // tc + o0 + j)),
                   pl.BlockSpec((8, tc), lambda j, b: (0, j)), pl.BlockSpec((1, tc), lambda j, b: (0, j))],
        out_shape=[jax.ShapeDtypeStruct(dpp.shape, dpp.dtype), jax.ShapeDtypeStruct((8, n), F32),
                   jax.ShapeDtypeStruct((1, n), F32)],
        scratch_shapes=[pltpu.VMEM((s + CONV_PAD, tc), F32)], input_output_aliases={4: 0},
        compiler_params=_cp("parallel", "arbitrary"), name=name)(pp, w8, bias, dxc, dpp)


def _softplus(x):
    return jnp.maximum(x, 0.0) + jnp.log1p(jnp.exp(-jnp.abs(x)))


def _head_group_matrix():
    h = jnp.arange(128)[:, None]
    j = jnp.arange(NG * 128)[None, :]
    per = NH // NG
    return ((h < NH) & (j == (h // per) * 128 + h % per)).astype(F32)


def _dt_fwd(pp, bias128, alog128, *, name):
    T = pp.shape[0]
    tr = _tile(T, 1024, CH)
    per = NH // NG

    def body(x_ref, b_ref, al_ref, p_ref, g_ref, t_ref, ag_ref, at_ref):
        lane = lax.broadcasted_iota(jnp.int32, (tr, 128), 1)
        dt = jnp.where(lane < NH, _softplus(x_ref[...] + b_ref[...]), 0.0)
        g_ref[...] = _nn(dt, p_ref[...], HI)
        eye = (lax.broadcasted_iota(jnp.int32, (NH, 128), 0)
               == lax.broadcasted_iota(jnp.int32, (NH, 128), 1)).astype(F32)
        t_ref[...] = _dot(eye, dt, ((1,), (1,)), HI).reshape(NG, per, tr)
        la = dt * -jnp.exp(al_ref[...])
        tril = _tril().astype(F32)
        for ci in range(tr // CH):
            rows = slice(ci * CH, (ci + 1) * CH)
            acs = _nn(tril, la[rows], HI)
            ag_ref[rows, :] = _nn(acs, p_ref[...], HI)
            at_ref[:, :, rows] = _dot(eye, acs, ((1,), (1,)), HI).reshape(NG, per, CH)

    grouped = pl.BlockSpec((tr, NG * 128), lambda i: (i, 0))
    transposed = pl.BlockSpec((NG, per, tr), lambda i: (0, 0, i))
    vec = pl.BlockSpec((1, 128), lambda i: (0, 0))
    return pl.pallas_call(
        body, grid=(T // tr,),
        in_specs=[pl.BlockSpec((tr, 128), lambda i: (i, O_DT // 128)), vec, vec,
                  pl.BlockSpec((128, NG * 128), lambda i: (0, 0))],
        out_specs=[grouped, transposed, grouped, transposed],
        out_shape=[jax.ShapeDtypeStruct((T, NG * 128), F32), jax.ShapeDtypeStruct((NG, per, T), F32)] * 2,
        compiler_params=_cp("parallel"), name=name)(pp, bias128, alog128, _head_group_matrix())


def _dt_bwd(pp, bias128, ddtg, dpp, *, name):
    T = pp.shape[0]
    tr = _tile(T, 1024, 8)

    def body(x_ref, b_ref, d_ref, p_ref, dpp_in, o_ref, db_ref):
        del dpp_in

        @pl.when(pl.program_id(0) == 0)
        def _():
            db_ref[...] = jnp.zeros_like(db_ref)

        lane = lax.broadcasted_iota(jnp.int32, (tr, 128), 1)
        ddt = _dot(d_ref[...], p_ref[...], ((1,), (1,)), HI)
        dr = jnp.where(lane < NH, ddt * _sig(x_ref[...] + b_ref[...]), 0.0)
        db_ref[...] += jnp.sum(dr, axis=0, keepdims=True)
        o_ref[...] = dr.astype(BF16)

    col = pl.BlockSpec((tr, 128), lambda i: (i, O_DT // 128))
    vec = pl.BlockSpec((1, 128), lambda i: (0, 0))
    return pl.pallas_call(
        body, grid=(T // tr,),
        in_specs=[col, vec, pl.BlockSpec((tr, NG * 128), lambda i: (i, 0)), pl.BlockSpec((128, NG * 128), lambda i: (0, 0)),
                  pl.BlockSpec(memory_space=pl.ANY)],
        out_specs=[col, vec],
        out_shape=[jax.ShapeDtypeStruct(dpp.shape, dpp.dtype), jax.ShapeDtypeStruct((1, 128), F32)],
        input_output_aliases={4: 0}, compiler_params=_cp("arbitrary"),
        name=name)(pp, bias128, ddtg, _head_group_matrix(), dpp)


def _tril():
    return lax.broadcasted_iota(jnp.int32, (CH, CH), 0) >= lax.broadcasted_iota(jnp.int32, (CH, CH), 1)


def _ssd_common(dt, dtt, arow, acol):
    ri = lax.broadcasted_iota(jnp.int32, (CH, CH), 0)
    ci = lax.broadcasted_iota(jnp.int32, (CH, CH), 1)
    tril = ri >= ci
    triu = ri <= ci
    acs_col = _nn(tril.astype(F32), dt * arow, HI)
    acs_row = _nn(dtt * acol, triu.astype(F32), HI)
    return tril, triu, acs_col, acs_row


def _pair_terms(q, dt, acs_col, acs_row, tril, lo):
    ha, hb = 2 * q, 2 * q + 1
    col_a, col_b = acs_col[:, ha:ha + 1], acs_col[:, hb:hb + 1]
    row_a, row_b = acs_row[ha:ha + 1, :], acs_row[hb:hb + 1, :]
    last_a, last_b = acs_col[CH - 1:CH, ha:ha + 1], acs_col[CH - 1:CH, hb:hb + 1]
    out = dict(
        dtsel=jnp.where(lo, dt[:, ha:ha + 1], dt[:, hb:hb + 1]),
        d_a=jnp.exp(jnp.where(tril, col_a - row_a, NEG)), d_b=jnp.exp(jnp.where(tril, col_b - row_b, NEG)),
        esel=jnp.where(lo, jnp.exp(col_a), jnp.exp(col_b)),
        fsel=jnp.where(lo, jnp.exp(last_a - col_a), jnp.exp(last_b - col_b)),
        g_a=jnp.exp(last_a), g_b=jnp.exp(last_b))
    return out


def _ssd_fwd(xc, pp, dtg, acsg, acst, dexp, ng, bl, s, *, name):
    nc = s // CH
    T = bl * s

    def body(xs_ref, bm_ref, cm_ref, z_ref, dt_ref, acs_col_ref, acs_row_ref, dexp_ref, ng_ref,
             y_ref, yn_ref, prev_ref, st_ref):
        @pl.when(pl.program_id(2) == 0)
        def _():
            st_ref[...] = jnp.zeros_like(st_ref)

        dt, acs_col, acs_row, tril = dt_ref[...], acs_col_ref[...], acs_row_ref[...], _tril()
        bm, cm = bm_ref[...].astype(MXU), cm_ref[...].astype(MXU)
        cb = _nt(cm, bm)
        lo = lax.broadcasted_iota(jnp.int32, (CH, 128), 1) < HD
        sub_lo = lax.broadcasted_iota(jnp.int32, (128, NS), 0) < HD
        for q in range(4):
            t = _pair_terms(q, dt, acs_col, acs_row, tril, lo)
            x = xs_ref[:, 128 * q:128 * (q + 1)]
            xd = x * t["dtsel"]
            y = (_nn((cb * t["d_a"]).astype(MXU), jnp.where(lo, xd, 0.0).astype(MXU))
                 + _nn((cb * t["d_b"]).astype(MXU), jnp.where(lo, 0.0, xd).astype(MXU)))
            prev = st_ref[q]
            prev_ref[q] = prev
            y = y + t["esel"] * _nt(cm, prev.astype(MXU))
            st_ref[q] = prev * jnp.where(sub_lo, t["g_a"], t["g_b"]) + _tn((xd * t["fsel"]).astype(MXU), bm)
            y_ref[:, 128 * q:128 * (q + 1)] = y + dexp_ref[:, 128 * q:128 * (q + 1)] * x
        zv = z_ref[...]
        yg = y_ref[...] * (zv * _sig(zv))
        r = lax.rsqrt(jnp.mean(yg * yg, axis=-1, keepdims=True) + EPS)
        yn_ref[...] = (yg * r * ng_ref[...]).astype(BF16)

    def row(width, off_blocks):
        return pl.BlockSpec((CH, width), lambda g, b, c: (b * nc + c, off_blocks + g))

    return pl.pallas_call(
        body, grid=(NG, bl, nc),
        in_specs=[row(GW, 0), row(NS, DI // NS), row(NS, DI // NS + NG), row(GW, O_Z // GW), row(128, 0), row(128, 0),
                  pl.BlockSpec((None, 8, CH), lambda g, b, c: (g, 0, b * nc + c)),
                  pl.BlockSpec((1, GW), lambda g, b, c: (0, g)), pl.BlockSpec((1, GW), lambda g, b, c: (0, g))],
        out_specs=[row(GW, 0), row(GW, 0),
                   pl.BlockSpec((None, 4, 128, NS), lambda g, b, c: (b * nc + c, g, 0, 0))],
        out_shape=[jax.ShapeDtypeStruct((T, DI), F32), jax.ShapeDtypeStruct((T, DI), BF16),
                   jax.ShapeDtypeStruct((bl * nc, 16, 128, NS), F32)],
        scratch_shapes=[pltpu.VMEM((4, 128, NS), F32)],
        compiler_params=_cp("parallel", "parallel", "arbitrary"), name=name,
    )(xc, xc, xc, pp, dtg, acsg, acst, dexp, ng)


def _ssd_bwd(dyn, y, xc, pp, dtg, dtt, arow, acol, dexp, ng, prev, dpp, bl, s, *, name):
    nc = s // CH
    T = bl * s

    def rsum(v):
        return jnp.sum(v, axis=1, keepdims=True)

    def asum(v):
        return jnp.sum(jnp.sum(v, axis=0, keepdims=True), axis=1, keepdims=True)

    def body(dyn_ref, y_ref, xs_ref, bm_ref, cm_ref, z_ref, dt_ref, dtt_ref, arow_ref, acol_ref, dexp_ref, ng_ref,
             prev_ref, dpp_in, dz_ref, dxs_ref, db_ref, dc_ref, ddt_ref, dng_ref, dd_ref, dal_ref, dst_ref):
        del dpp_in

        @pl.when((pl.program_id(1) == 0) & (pl.program_id(2) == 0))
        def _():
            dng_ref[...] = jnp.zeros_like(dng_ref)
            dd_ref[...] = jnp.zeros_like(dd_ref)
            dal_ref[...] = jnp.zeros_like(dal_ref)

        @pl.when(pl.program_id(2) == 0)
        def _():
            dst_ref[...] = jnp.zeros_like(dst_ref)

        yv, zv, xsv, dexp_v = y_ref[...], z_ref[...], xs_ref[...], dexp_ref[...]
        sz = _sig(zv)
        silu = zv * sz
        yg = yv * silu
        r = lax.rsqrt(jnp.mean(yg * yg, axis=-1, keepdims=True) + EPS)
        yh = yg * r
        dynv = dyn_ref[...]
        dng_ref[...] += jnp.sum(dynv * yh, axis=0, keepdims=True)
        dyh = dynv * ng_ref[...]
        dyg = r * (dyh - yh * jnp.mean(dyh * yh, axis=-1, keepdims=True))
        dz_ref[...] = (dyg * yv * (sz * (1.0 + zv * (1.0 - sz)))).astype(BF16)
        dy_all = dyg * silu
        dd_ref[...] += jnp.sum(dy_all * xsv, axis=0, keepdims=True)

        dt = dt_ref[...]
        arow_v = -jnp.exp(arow_ref[...])
        tril, triu, acs_col, acs_row = _ssd_common(dt, dtt_ref[...], arow_v, -jnp.exp(acol_ref[...]))
        bm, cm = bm_ref[...].astype(MXU), cm_ref[...].astype(MXU)
        cb = _nt(cm, bm)
        lane = lax.broadcasted_iota(jnp.int32, (CH, 128), 1)
        is_last = lax.broadcasted_iota(jnp.int32, (CH, 128), 0) == CH - 1
        lo = lane < HD
        sub_lo = lax.broadcasted_iota(jnp.int32, (128, NS), 0) < HD
        dcb = jnp.zeros((CH, CH), F32)
        dc_acc = jnp.zeros((CH, NS), F32)
        db_acc = jnp.zeros((CH, NS), F32)
        dacs = jnp.zeros((CH, 128), F32)
        ddtx = jnp.zeros((CH, 128), F32)
        csum = jnp.zeros((8, CH), F32)
        sub8 = lax.broadcasted_iota(jnp.int32, (8, CH), 0)
        for q in range(4):
            ha, hb = 2 * q, 2 * q + 1
            sl = slice(128 * q, 128 * (q + 1))
            t = _pair_terms(q, dt, acs_col, acs_row, tril, lo)
            x, dy = xsv[:, sl], dy_all[:, sl]
            xd = x * t["dtsel"]
            xd_m = xd.astype(MXU)
            dy_lo, dy_hi = jnp.where(lo, dy, 0.0).astype(MXU), jnp.where(lo, 0.0, dy).astype(MXU)
            m_a, m_b = cb * t["d_a"], cb * t["d_b"]
            prev_m = prev_ref[q].astype(MXU)
            dnext = dst_ref[q]
            dnext_m = dnext.astype(MXU)
            bds = _nt(bm, dnext_m)
            dxd = _tn(m_a.astype(MXU), dy_lo) + _tn(m_b.astype(MXU), dy_hi) + t["fsel"] * bds
            dye_m = (dy * t["esel"]).astype(MXU)
            dst_ref[q] = dnext * jnp.where(sub_lo, t["g_a"], t["g_b"]) + _tn(dye_m, cm)
            dm_a, dm_b = _nt(dy_lo, xd_m), _nt(dy_hi, xd_m)
            dcb = dcb + dm_a * t["d_a"] + dm_b * t["d_b"]
            g_a, g_b = dm_a * m_a, dm_b * m_b
            csum = (csum + jnp.where(sub8 == ha, jnp.sum(g_a, axis=0, keepdims=True), 0.0)
                    + jnp.where(sub8 == hb, jnp.sum(g_b, axis=0, keepdims=True), 0.0))
            tf = t["fsel"] * xd * bds
            tyf = dy * (t["esel"] * _nt(cm, prev_m)) - tf
            dpp = dnext * prev_ref[q]
            ea = asum(jnp.where(lo, tf, 0.0)) + t["g_a"] * asum(jnp.where(sub_lo, dpp, 0.0))
            eb = asum(jnp.where(lo, 0.0, tf)) + t["g_b"] * asum(jnp.where(sub_lo, 0.0, dpp))
            ra = rsum(g_a + jnp.where(lo, tyf, 0.0)) + jnp.where(is_last, ea, 0.0)
            rb = rsum(g_b + jnp.where(lo, 0.0, tyf)) + jnp.where(is_last, eb, 0.0)
            dacs = dacs + jnp.where(lane == ha, ra, 0.0) + jnp.where(lane == hb, rb, 0.0)
            tx = dxd * x
            ddtx = (ddtx + jnp.where(lane == ha, rsum(jnp.where(lo, tx, 0.0)), 0.0)
                    + jnp.where(lane == hb, rsum(jnp.where(lo, 0.0, tx)), 0.0))
            dxs_ref[:, sl] = dxd * t["dtsel"] + dexp_v[:, sl] * dy
            dc_acc = dc_acc + _nn(dye_m, prev_m)
            db_acc = db_acc + _nn((xd * t["fsel"]).astype(MXU), dnext_m)
        dcb_m = dcb.astype(MXU)
        dc_ref[...] = dc_acc + _nn(dcb_m, bm)
        db_ref[...] = db_acc + _tn(dcb_m, cm)
        dacs = dacs - jnp.concatenate([csum, jnp.zeros((CH - 8, CH), F32)], axis=0).T
        dla = _nn(triu.astype(F32), dacs, HI)
        ddt_ref[...] = dla * arow_v + ddtx
        dal_ref[...] += jnp.sum(dla * dt, axis=0, keepdims=True) * arow_v

    def row(width, off_blocks):
        return pl.BlockSpec((CH, width), lambda g, b, c: (b * nc + nc - 1 - c, off_blocks + g))

    gvec = pl.BlockSpec((1, GW), lambda g, b, c: (0, g))
    hvec = pl.BlockSpec((1, 128), lambda g, b, c: (0, g))
    return pl.pallas_call(
        body, grid=(NG, bl, nc),
        in_specs=[row(GW, 0), row(GW, 0), row(GW, 0), row(NS, DI // NS), row(NS, DI // NS + NG), row(GW, O_Z // GW),
                  row(128, 0), pl.BlockSpec((None, 8, CH), lambda g, b, c: (g, 0, b * nc + nc - 1 - c)),
                  hvec, pl.BlockSpec((None, 8, 1), lambda g, b, c: (g, 0, 0)), gvec, gvec,
                  pl.BlockSpec((None, 4, 128, NS), lambda g, b, c: (b * nc + nc - 1 - c, g, 0, 0)),
                  pl.BlockSpec(memory_space=pl.ANY)],
        out_specs=[row(GW, O_Z // GW), row(GW, 0), row(NS, 0), row(NS, 0), row(128, 0), gvec, gvec, hvec],
        input_output_aliases={13: 0},
        out_shape=[jax.ShapeDtypeStruct(dpp.shape, dpp.dtype), jax.ShapeDtypeStruct((T, DI), F32),
                   jax.ShapeDtypeStruct((T, NG * NS), F32), jax.ShapeDtypeStruct((T, NG * NS), F32),
                   jax.ShapeDtypeStruct((T, NG * 128), F32), jax.ShapeDtypeStruct((1, DI), F32),
                   jax.ShapeDtypeStruct((1, DI), F32), jax.ShapeDtypeStruct((1, NG * 128), F32)],
        scratch_shapes=[pltpu.VMEM((4, 128, NS), F32)],
        compiler_params=_cp("arbitrary", "arbitrary", "arbitrary"), name=name,
    )(dyn, y, xc, xc, xc, pp, dtg, dtt, arow, acol, dexp, ng, prev, dpp)


def _merge_fwd(pp, ya, yb, *, name):
    T = ya.shape[0]
    tr = _tile(T, 512, 8)

    def body(ga_ref, gb_ref, ya_ref, yb_ref, o_ref):
        o_ref[...] = (_sig(ga_ref[...]) * ya_ref[...] + _sig(gb_ref[...]) * yb_ref[...]).astype(BF16)

    row = pl.BlockSpec((tr, D), lambda i: (i, 0))
    return pl.pallas_call(
        body, grid=(T // tr,),
        in_specs=[pl.BlockSpec((tr, D), lambda i: (i, O_GA // D)), pl.BlockSpec((tr, D), lambda i: (i, O_GB // D)),
                  row, row],
        out_specs=row, out_shape=jax.ShapeDtypeStruct((T, D), BF16), compiler_params=_cp("parallel"),
        name=name)(pp, pp, ya, yb)


def _merge_bwd(pp, ya, yb, dm, *, name):
    T = ya.shape[0]
    tr = _tile(T, 512, 8)
    assert O_GB == O_GA + D and O_GA % (2 * D) == 0

    def body(g_ref, ya_ref, yb_ref, dm_ref, dya_ref, dyb_ref, dg_ref):
        sa, sb, dmv = _sig(g_ref[:, :D]), _sig(g_ref[:, D:]), dm_ref[...]
        dya_ref[...] = (dmv * sa).astype(BF16)
        dyb_ref[...] = (dmv * sb).astype(BF16)
        dg_ref[:, :D] = (dmv * ya_ref[...] * (sa * (1.0 - sa))).astype(BF16)
        dg_ref[:, D:] = (dmv * yb_ref[...] * (sb * (1.0 - sb))).astype(BF16)

    row = pl.BlockSpec((tr, D), lambda i: (i, 0))
    gates = pl.BlockSpec((tr, 2 * D), lambda i: (i, O_GA // (2 * D)))
    act = jax.ShapeDtypeStruct((T, D), BF16)
    return pl.pallas_call(
        body, grid=(T // tr,), in_specs=[gates, row, row, row], out_specs=[row, row, gates],
        out_shape=[act, act, jax.ShapeDtypeStruct((T, NPP), BF16)], compiler_params=_cp("parallel"),
        name=name)(pp, ya, yb, dm)


def _softmax_rows(sc):
    e = jnp.exp(sc - jnp.max(sc, axis=-1, keepdims=True))
    return e / jnp.sum(e, axis=-1, keepdims=True)


def _attn_fwd(q, kv, bl, s, *, name):
    m = kv.shape[0] // bl
    tq = _tile(s, 512)
    nq = s // tq
    scale = 1.0 / math.sqrt(XD)

    def body(q_ref, k_ref, v_ref, o_ref):
        p = _softmax_rows(_nt(q_ref[...], k_ref[...]) * scale)
        o_ref[...] = _nn(p.astype(MXU), v_ref[...]).astype(BF16)

    qspec = pl.BlockSpec((tq, XD), lambda b, h, i: (b * nq + i, h))
    return pl.pallas_call(
        body, grid=(bl, XH, nq),
        in_specs=[qspec, pl.BlockSpec((m, XD), lambda b, h, i: (b, h)),
                  pl.BlockSpec((m, XD), lambda b, h, i: (b, XH + h))],
        out_specs=qspec, out_shape=jax.ShapeDtypeStruct((bl * s, D), BF16),
        compiler_params=_cp("parallel", "parallel", "parallel"), name=name)(q, kv, kv)


def _attn_bwd(q, kv, do, bl, s, *, name):
    m = kv.shape[0] // bl
    tq = _tile(s, 512)
    nq = s // tq
    scale = 1.0 / math.sqrt(XD)

    def body(q_ref, k_ref, v_ref, do_ref, dq_ref, dk_ref, dv_ref):
        @pl.when(pl.program_id(2) == 0)
        def _():
            dk_ref[...] = jnp.zeros_like(dk_ref)
            dv_ref[...] = jnp.zeros_like(dv_ref)

        qv, kvv, vv, dov = q_ref[...], k_ref[...], v_ref[...], do_ref[...]
        p = _softmax_rows(_nt(qv, kvv) * scale)
        dp = _nt(dov, vv)
        ds = (p * (dp - jnp.sum(dp * p, axis=-1, keepdims=True)) * scale).astype(MXU)
        dq_ref[...] = _nn(ds, kvv).astype(BF16)
        dk_ref[...] += _tn(ds, qv)
        dv_ref[...] += _tn(p.astype(MXU), dov)

    qspec = pl.BlockSpec((tq, XD), lambda b, h, i: (b * nq + i, h))
    kspec = pl.BlockSpec((m, XD), lambda b, h, i: (b, h))
    return pl.pallas_call(
        body, grid=(bl, XH, nq),
        in_specs=[qspec, kspec, pl.BlockSpec((m, XD), lambda b, h, i: (b, XH + h)), qspec],
        out_specs=[qspec, kspec, kspec],
        out_shape=[jax.ShapeDtypeStruct((bl * s, D), BF16), jax.ShapeDtypeStruct((bl * m, D), F32),
                   jax.ShapeDtypeStruct((bl * m, D), F32)],
        compiler_params=_cp("parallel", "parallel", "arbitrary"), name=name)(q, kv, kv, do)


def _row_tile(r, c, max_elems=512 * 1024, align=16):
    best = None
    for t in range(align, r + 1, align):
        if r % t == 0 and t * c <= max_elems:
            best = t
    return best if best is not None else r


def _addn(a, others, *, name, also_bf16=False):
    r, c = a.shape
    tr = _row_tile(r, c)
    n = len(others)

    def body(*refs):
        acc = refs[0][...].astype(F32)
        for o_ref in refs[1:1 + n]:
            acc = acc + o_ref[...].astype(F32)
        refs[1 + n][...] = acc
        if also_bf16:
            refs[2 + n][...] = acc.astype(BF16)

    spec = pl.BlockSpec((tr, c), lambda i: (i, 0))
    shapes = [jax.ShapeDtypeStruct((r, c), F32)] + ([jax.ShapeDtypeStruct((r, c), BF16)] if also_bf16 else [])
    out = pl.pallas_call(
        body, grid=(r // tr,), in_specs=[spec] * (1 + n), out_specs=[spec] * len(shapes), out_shape=shapes,
        compiler_params=_cp("parallel"), name=name)(a, *others)
    return out if also_bf16 else out[0]


def _sum_leading(a, *, name):
    n, r, c = a.shape

    def body(a_ref, o_ref):
        acc = a_ref[0]
        for i in range(1, n):
            acc = acc + a_ref[i]
        o_ref[...] = acc

    return pl.pallas_call(body, out_shape=jax.ShapeDtypeStruct((r, c), F32), name=name)(a)


def _adamw_math(wv, gv, mv, vv):
    m2 = ADAM_B1 * mv + (1.0 - ADAM_B1) * gv
    v2 = ADAM_B2 * vv + (1.0 - ADAM_B2) * (gv * gv)
    m_hat = m2 / (1.0 - ADAM_B1 ** ADAM_STEP)
    v_hat = v2 / (1.0 - ADAM_B2 ** ADAM_STEP)
    return -ADAM_LR * (m_hat / (jnp.sqrt(v_hat) + ADAM_EPS) + ADAM_WD * wv), m2, v2


def _adamw(w, g, m, v, *, name):
    r, c = w.shape
    tr = _row_tile(r, c, align=8)

    def body(w_ref, g_ref, m_ref, v_ref, d_ref, mo_ref, vo_ref):
        d_ref[...], mo_ref[...], vo_ref[...] = _adamw_math(w_ref[...], g_ref[...], m_ref[...], v_ref[...])

    spec = pl.BlockSpec((tr, c), lambda i: (i, 0))
    shp = jax.ShapeDtypeStruct((r, c), F32)
    return pl.pallas_call(
        body, grid=(r // tr,), in_specs=[spec] * 4, out_specs=[spec] * 3, out_shape=[shp] * 3,
        compiler_params=_cp("parallel"), name=name)(w, g, m, v)


def _adamw_halves(w, g_mine, g_other, m, v, c, *, name):
    r, cols = w.shape
    h = r // 2
    tr = _row_tile(h, cols, align=8)
    nh = h // tr

    def body(c_ref, w_ref, gm_ref, go_ref, m_ref, v_ref, g_ref, d_ref, mo_ref, vo_ref):
        gv = jnp.where(pl.program_id(0) // nh == c_ref[0], gm_ref[...], go_ref[...])
        g_ref[...] = gv
        d_ref[...], mo_ref[...], vo_ref[...] = _adamw_math(w_ref[...], gv, m_ref[...], v_ref[...])

    full = pl.BlockSpec((tr, cols), lambda i, c_: (i, 0))
    half = pl.BlockSpec((tr, cols), lambda i, c_: (i % nh, 0))
    shp = jax.ShapeDtypeStruct((r, cols), F32)
    return pl.pallas_call(
        body,
        grid_spec=pltpu.PrefetchScalarGridSpec(num_scalar_prefetch=1, grid=(2 * nh,),
                                               in_specs=[full, half, half, full, full], out_specs=[full] * 4),
        out_shape=[shp] * 4, compiler_params=_cp("parallel"), name=name,
    )(jnp.reshape(c, (1,)).astype(jnp.int32), w, g_mine, g_other, m, v)


def _flip(i, d):
    return 1 - i if d else i


def _comm(name, ins, out_shapes, n_remote, n_local, plan, aliases=None):
    n_in, n_out = len(ins), len(out_shapes)

    def body(*refs):
        in_refs, out_refs = refs[:n_in], refs[n_in:n_in + n_out]
        send_sems, recv_sems = refs[n_in + n_out], refs[n_in + n_out + 1]
        x, y, c = lax.axis_index("x"), lax.axis_index("y"), lax.axis_index("c")
        remote, local = plan(in_refs, out_refs, x, y, c)
        assert len(remote) == n_remote and len(local) == n_local
        copies = []
        if n_local:
            loc_sems = refs[n_in + n_out + 2]
            copies += [pltpu.make_async_copy(s_, d_, loc_sems.at[i]) for i, (s_, d_) in enumerate(local)]
        copies += [pltpu.make_async_remote_copy(src_ref=s_, dst_ref=d_, send_sem=send_sems.at[i],
                                                recv_sem=recv_sems.at[i], device_id=dev, device_id_type=MESH)
                   for i, (s_, d_, dev) in enumerate(remote)]
        for cp in copies:
            cp.start()
        for cp in copies:
            cp.wait()

    hbm = pl.BlockSpec(memory_space=pl.ANY)
    scratch = [pltpu.SemaphoreType.DMA((n_remote,)), pltpu.SemaphoreType.DMA((n_remote,))]
    if n_local:
        scratch.append(pltpu.SemaphoreType.DMA((n_local,)))
    return pl.pallas_call(
        body, in_specs=[hbm] * n_in, out_specs=[hbm] * n_out, out_shape=out_shapes, scratch_shapes=scratch,
        input_output_aliases=aliases or {}, compiler_params=pltpu.CompilerParams(has_side_effects=True),
        name=name)(*ins)


HBM_SPEC = pl.BlockSpec(memory_space=pltpu.HBM)
SEM_SPEC = pl.BlockSpec(memory_space=pltpu.SEMAPHORE)
DATAFLOW = pltpu.SideEffectType.DATAFLOW_SIDE_EFFECTING


def _remote_copies(plan, srcs, lands, send_sems, recv_sems, n_copies):
    x, y, c = lax.axis_index("x"), lax.axis_index("y"), lax.axis_index("c")
    copies = plan(srcs, lands, x, y, c)
    assert len(copies) == n_copies
    return [pltpu.make_async_remote_copy(src_ref=s_, dst_ref=d_, send_sem=send_sems.at[i], recv_sem=recv_sems.at[i],
                                         device_id=dev, device_id_type=MESH) for i, (s_, d_, dev) in enumerate(copies)]


def _split_start(name, srcs, lands, n_copies, plan, after=None):
    ns, nb = len(srcs), len(srcs) + len(lands)
    n_after = 0 if after is None else 1
    n_in = nb + n_after

    def body(*refs):
        for cp in _remote_copies(plan, refs[:ns], refs[ns:nb], refs[n_in], refs[n_in + 1], n_copies):
            cp.start()
        refs[-1][...] = jnp.zeros_like(refs[-1])

    arrays = [pltpu.with_memory_space_constraint(a_, pltpu.HBM) for a_ in list(srcs) + list(lands)]
    out = pl.pallas_call(
        body, name=name,
        out_shape=(pltpu.SemaphoreType.DMA((n_copies,)), pltpu.SemaphoreType.DMA((n_copies,)),
                   *[pltpu.HBM(a_.shape, a_.dtype) for a_ in arrays], jax.ShapeDtypeStruct((8, 128), F32)),
        in_specs=[HBM_SPEC] * nb + [pl.BlockSpec(memory_space=pl.ANY)] * n_after,
        out_specs=(SEM_SPEC, SEM_SPEC, *[HBM_SPEC] * nb, pl.BlockSpec(memory_space=pltpu.VMEM)),
        input_output_aliases={i: 2 + i for i in range(nb)},
        compiler_params=pltpu.CompilerParams(has_side_effects=DATAFLOW))(*arrays, *([after] * n_after))
    return (out[0], out[1], list(out[2:2 + nb])), out[-1]


def _split_wait(name, handle, ns, n_copies, plan, after):
    send_sems, recv_sems, bufs = handle
    nb = len(bufs)

    def body(*refs):
        for cp in _remote_copies(plan, refs[:ns], refs[ns:nb], refs[nb], refs[nb + 1], n_copies):
            cp.wait_send()
            cp.wait_recv()

    out = pl.pallas_call(
        body, name=name, out_shape=[pltpu.HBM(b_.shape, b_.dtype) for b_ in bufs],
        in_specs=[HBM_SPEC] * nb + [SEM_SPEC, SEM_SPEC, pl.BlockSpec(memory_space=pl.ANY)],
        out_specs=[HBM_SPEC] * nb, input_output_aliases={i: i for i in range(nb)},
        compiler_params=pltpu.CompilerParams(has_side_effects=DATAFLOW))(*bufs, send_sems, recv_sems, after)
    return list(out[ns:])


def _gather_start(shards, tag, after=None):
    n = len(shards)
    lands = [lax.empty((4,) + s.shape, s.dtype) for s in shards]

    def plan(srcs, dsts, x, y, c):
        k = 2 * x + y
        copies = []
        for w_ref, o_ref in zip(srcs, dsts):
            h = w_ref.shape[0] // 2
            rows = pl.ds(c * h, h)
            copies += [(w_ref.at[rows], o_ref.at[k, rows], (_flip(x, dx), _flip(y, dy), c)) for dx, dy in CHIP_FLIPS]
        return copies

    handle, token = _split_start(f"gather_{tag}_start", shards, lands, 3 * n, plan, after)
    return (handle, plan, n), token


def _gather_wait(started, after, tag):
    handle, plan, n = started
    return _split_wait(f"gather_{tag}_wait", handle, n, 3 * n, plan, after)


def _gather_d2d(lands, before, tag):
    n = len(lands)

    def plan_d2d(in_refs, out_refs, x, y, c):
        remote = []
        for o_ref in out_refs:
            h = o_ref.shape[1] // 2
            for dx, dy in CHIP_FLIPS:
                half = o_ref.at[2 * _flip(x, dx) + _flip(y, dy), pl.ds(c * h, h)]
                remote.append((half, half, (x, y, 1 - c)))
        return remote, []

    return _comm(f"gather_{tag}_d2d", list(lands) + list(before),
                 [jax.ShapeDtypeStruct(l_.shape, l_.dtype) for l_ in lands], 3 * n, 0, plan_d2d,
                 aliases={i: i for i in range(n)})


def _pair_plan(in_refs, out_refs, x, y, c):
    return [(i_, o_, (x, y, 1 - c)) for i_, o_ in zip(in_refs, out_refs)], []


def _rs_start(grads, tag):
    n = len(grads)
    c = lax.axis_index("c")
    def rows(g, start, h):
        return g.rows(start, h) if isinstance(g, _WInGrad) else lax.dynamic_slice_in_dim(g, start, h, axis=1)

    halves = [g.shape[1] // 2 for g in grads]
    mine = [rows(g, c * h, h) for g, h in zip(grads, halves)]
    send_a = [rows(g, (1 - c) * h, h).astype(BF16) for g, h in zip(grads, halves)]
    recv_a = _comm(f"rs_pair_{tag}", send_a, [jax.ShapeDtypeStruct(s.shape, BF16) for s in send_a], n, 0, _pair_plan)
    pair, pair_b = [], []
    for i, (mi, ra) in enumerate(zip(mine, recv_a)):
        four, h, cols = mi.shape
        p32, p16 = _addn(mi.reshape(four * h, cols), [ra.reshape(four * h, cols)], name=f"rs_pair_sum_{tag}_{i}",
                         also_bf16=True)
        pair.append(p32.reshape(four, h, cols))
        pair_b.append(p16.reshape(four, h, cols))

    def plan(srcs, dsts, x, y, c_):
        copies = []
        for i_, o_ in zip(srcs, dsts):
            for j, (dx, dy) in enumerate(CHIP_FLIPS):
                fx, fy = _flip(x, dx), _flip(y, dy)
                copies.append((i_.at[2 * fx + fy], o_.at[j], (fx, fy, c_)))
        return copies

    lands = [lax.empty((3,) + p.shape[1:], BF16) for p in pair_b]
    handle, token = _split_start(f"rs_chips_{tag}_start", pair_b, lands, 3 * n, plan)
    return (handle, plan, n, pair), token


def _rs_finish(started, after, tag):
    handle, plan, n, pair = started
    recv_b = _split_wait(f"rs_chips_{tag}_wait", handle, n, 3 * n, plan, after)
    k = 2 * lax.axis_index("x") + lax.axis_index("y")
    tot = [_addn(lax.dynamic_index_in_dim(p, k, 0, keepdims=False), [rb[0], rb[1], rb[2]],
                 name=f"rs_chip_sum_{tag}_{i}") for i, (p, rb) in enumerate(zip(pair, recv_b))]
    other = _comm(f"rs_halves_{tag}", tot, [jax.ShapeDtypeStruct(t.shape, F32) for t in tot], n, 0, _pair_plan)
    return tot, other


def _gather_all(vec, *, name, before=()):
    out = jax.ShapeDtypeStruct((8,) + vec.shape, vec.dtype)

    def plan(in_refs, out_refs, x, y, c):
        me = 4 * x + 2 * y + c
        remote = [(in_refs[0], out_refs[0].at[me], (_flip(x, dx), _flip(y, dy), _flip(c, dc)))
                  for dx in (0, 1) for dy in (0, 1) for dc in (0, 1) if (dx, dy, dc) != (0, 0, 0)]
        return remote, [(in_refs[0], out_refs[0].at[me])]

    return _comm(name, [vec] + list(before), [out], 7, 1, plan)[0]


def _pack(parts):
    flat = [p.reshape(-1).astype(F32) for p in parts]
    total = sum(f.shape[0] for f in flat)
    n = -(-total // 1024) * 128
    vec = jnp.concatenate(flat + [jnp.zeros((8 * n - total,), F32)]).reshape(8, n)
    offs, o = [], 0
    for f in flat:
        offs.append((o, f.shape[0]))
        o += f.shape[0]
    return vec, offs


def _unpack(vec, offs, shapes):
    flat = vec.reshape(-1)
    return [flat[o:o + n].reshape(s) for (o, n), s in zip(offs, shapes)]


BIG = (("ffn1_w_gate_up", "col"), ("ffn1_w_down", "row"), ("w_in", "col"), ("w_out_a", "row"), ("w_out_ssm", "row"),
       ("w_mix_out", "row"), ("w_q", "row"), ("w_kv", "col"), ("w_o_x", "row"), ("ffn2_w_gate_up", "col"),
       ("ffn2_w_down", "row"))
SMALL = ("ffn1_norm", "mix_norm", "conv_a_w", "ssm_conv_w", "ssm_conv_b", "ssm_dt_bias", "ssm_a_log", "ssm_d",
         "ssm_norm", "xattn_norm", "mem_norm", "ffn2_norm", "final_norm")
WEIGHTS = ("ffn1_norm", "ffn1_w_gate_up", "ffn1_w_down", "mix_norm", "w_in", "conv_a_w", "w_out_a", "ssm_conv_w",
           "ssm_conv_b", "ssm_dt_bias", "ssm_a_log", "ssm_d", "ssm_norm", "w_out_ssm", "w_mix_out", "xattn_norm",
           "mem_norm", "w_q", "w_kv", "w_o_x", "ffn2_norm", "ffn2_w_gate_up", "ffn2_w_down", "final_norm")


GATHER_GROUPS = (("a", ("ffn1_w_gate_up",)), ("b", ("ffn1_w_down", "w_in")),
                 ("c", ("w_out_a", "w_out_ssm", "w_mix_out", "w_q", "w_kv", "w_o_x", "ffn2_w_gate_up", "ffn2_w_down")))


def _place_own(land, own, k, *, name):
    four, r, cols = land.shape
    tr = _row_tile(r, cols)

    def body(k_ref, own_ref, land_in, o_ref):
        del k_ref, land_in
        o_ref[...] = own_ref[...]

    return pl.pallas_call(
        body,
        grid_spec=pltpu.PrefetchScalarGridSpec(
            num_scalar_prefetch=1, grid=(r // tr,),
            in_specs=[pl.BlockSpec((tr, cols), lambda i, k_: (i, 0)), pl.BlockSpec(memory_space=pl.ANY)],
            out_specs=pl.BlockSpec((None, tr, cols), lambda i, k_: (k_[0], i, 0))),
        out_shape=jax.ShapeDtypeStruct(land.shape, land.dtype), input_output_aliases={2: 0},
        compiler_params=_cp("parallel"), name=name)(jnp.reshape(k, (1,)).astype(jnp.int32), own, land)


def _full_weight(land, own, kind, k, *, name):
    land = _place_own(land, own, k, name=name)
    four, r, cols = land.shape
    if kind == "row":
        return land.reshape(four * r, cols)
    return jnp.transpose(land, (1, 0, 2)).reshape(r, four * cols)


class _GatheredWeights:
    def __init__(self, shards32, k, after):
        first = GATHER_GROUPS[0][1]
        self.shards, self.k = {n: shards32[n].astype(BF16) for n in first}, k
        self.full = {}
        self.n_done = 0
        self.started, token = self._start(0, after)
        self.token = token[0, 0]
        self.shards.update({n: (w + token[0, 0]).astype(BF16) for n, w in shards32.items() if n not in first})
        self.after = jnp.stack([self.shards[n][0, 0] for n in shards32 if n not in first]).astype(F32).reshape(1, -1)

    def _start(self, gi, after):
        tag, names = GATHER_GROUPS[gi]
        return _gather_start([self.shards[n] for n in names], tag, after)

    def mark(self, value):
        self.after = value

    def __getitem__(self, name):
        if name not in self.full:
            tag, names = GATHER_GROUPS[self.n_done]
            assert name in names, (name, tag)
            lands = _gather_wait(self.started, self.after, tag)
            before = []
            if self.n_done + 1 < len(GATHER_GROUPS):
                self.started, token = self._start(self.n_done + 1, lands[0])
                before = [token]
            lands = _gather_d2d(lands, before, tag)
            for n, land in zip(names, lands):
                if n == "w_in":
                    self.full[n] = _pad_w_in_shards(_place_own(land, self.shards[n], self.k, name=f"own_{n}"))
                else:
                    self.full[n] = _full_weight(land, self.shards[n], dict(BIG)[n], self.k, name=f"own_{n}")
            self.n_done += 1
        return self.full[name]


def _shard_major(dw, kind):
    if isinstance(dw, tuple):
        return jnp.concatenate(dw, axis=0)
    if dw.ndim == 3:
        return dw
    if kind == "row":
        return dw.reshape(4, dw.shape[0] // 4, dw.shape[1])
    return jnp.transpose(dw.reshape(dw.shape[0], 4, dw.shape[1] // 4), (1, 0, 2))


def _pad_rows8(w):
    return jnp.concatenate([w, jnp.zeros((8 - w.shape[0], w.shape[1]), w.dtype)], axis=0)


def _group_lanes(v):
    r = v.shape[0]
    return jnp.pad(v.reshape(r, NG, NH // NG), ((0, 0), (0, 0), (0, 128 - NH // NG))).reshape(r, NG * 128)


def _ungroup_lanes(v):
    r = v.shape[0]
    return v.reshape(r, NG, 128)[:, :, :NH // NG].reshape(r, NH)


def _local_step(wfull, small, x, mem, target, token=0.0, on_grads=None):
    bl, s, _ = x.shape
    T = bl * s
    x2, t2 = x.reshape(T, D), target.reshape(T, D)
    mem2 = mem.reshape(-1, D)
    g = {}
    tok = [token]
    mark = getattr(wfull, "mark", lambda value: None)

    def gain(name):
        return small[name].reshape(1, -1) + tok[0]

    def emit(tag, names):
        if on_grads is not None:
            tok[0] = tok[0] + on_grads(tag, {n: g[n] for n in names})

    def ffn_fwd(h, n, wgu, wd, tag, next_gain=None):
        gate, up, a = _gate_up_fwd(n, wfull[wgu], name=f"{tag}_gate_up")
        mark(a)
        out = _mm(a, wfull[wd], "nn", tk=DFF, scale=FFN_RES, residual=h, norm_gain=next_gain, name=f"{tag}_down")
        return out, (n, gate, up, a)

    def ffn_bwd(dh, h, norm, wgu, wd, saved, tag):
        n, gate, up, a = saved
        dgate, dup = _act_bwd(dh, wfull[wd], gate, up, FFN_RES, name=f"{tag}_d_act")
        g[wd] = _mm(a, dh, "tn", tm=1408, scale=FFN_RES, name=f"{tag}_d_w_down")
        g[wgu] = (_mm(n, dgate, "tn", tn=1408, col_shards=2, name=f"{tag}_d_w_gate"),
                  _mm(n, dup, "tn", tn=1408, col_shards=2, name=f"{tag}_d_w_up"))
        emit(tag, (wgu, wd))
        dn = _mm(dgate, wfull[wgu], "nt", a2=dup, tk=1408, name=f"{tag}_d_norm_out")
        dh_in, g[norm] = _norm_bwd(h, gain(norm), dn, dh, name=f"{tag}_d_norm")
        return dh_in

    n1 = _norm_fwd(x2, gain("ffn1_norm"), name="ffn1_norm")
    (h1, u), ffn1_saved = ffn_fwd(x2, n1, "ffn1_w_gate_up", "ffn1_w_down", "ffn1", gain("mix_norm"))
    mark(h1)
    pp = _mm(u, wfull["w_in"], "nn", tn=1152, name="in_proj")
    wa8 = _pad_rows8(small["conv_a_w"])
    ws8 = _pad_rows8(small["ssm_conv_w"])
    conv_b = gain("ssm_conv_b")
    bias128 = jnp.pad(gain("ssm_dt_bias"), ((0, 0), (0, 128 - NH)))
    ya_pre = _conv_a_fwd(pp, wa8, bl, s, name="conv_a")
    xc = _conv_ssm_fwd(pp, ws8, conv_b, bl, s, name="conv_ssm")
    mark(xc)
    alog = gain("ssm_a_log")
    dtg, dtt, acsg, acst = _dt_fwd(pp, bias128, jnp.pad(alog, ((0, 0), (0, 128 - NH))), name="dt")
    arow, acol = _group_lanes(alog), alog.reshape(NG, NH // NG, 1)
    dexp = jnp.repeat(gain("ssm_d"), HD, axis=1)
    ng = gain("ssm_norm")
    y, yn, prev = _ssd_fwd(xc, pp, dtg, acsg, acst, dexp, ng, bl, s, name="ssd")
    ya = _mm(ya_pre, wfull["w_out_a"], "nn", tn=1024, name="out_a")
    yb = _mm(yn, wfull["w_out_ssm"], "nn", tn=1024, tk=DI, name="out_ssm")
    merged = _merge_fwd(pp, ya, yb, name="merge")
    h2, un = _mm(merged, wfull["w_mix_out"], "nn", residual=h1, norm_gain=gain("xattn_norm"), name="mix_out")
    q = _mm(un, wfull["w_q"], "nn", tn=1024, out_dtype=BF16, name="q_proj")
    mn = _norm_fwd(mem2, gain("mem_norm"), name="mem_norm")
    kv = _mm(mn, wfull["w_kv"], "nn", tn=1024, out_dtype=BF16, name="kv_proj")
    o = _attn_fwd(q, kv, bl, s, name="attn")
    h3, n2 = _mm(o, wfull["w_o_x"], "nn", residual=h2, norm_gain=gain("ffn2_norm"), name="attn_out")
    h4, ffn2_saved = ffn_fwd(h3, n2, "ffn2_w_gate_up", "ffn2_w_down", "ffn2")
    sq_err, dh4, dgf = _final_loss(h4, gain("final_norm"), t2, name="final_loss")
    g["final_norm"] = dgf

    dh3 = ffn_bwd(dh4, h3, "ffn2_norm", "ffn2_w_gate_up", "ffn2_w_down", ffn2_saved, "ffn2")
    do = _mm(dh3, wfull["w_o_x"], "nt", tn=1024, out_dtype=BF16, name="d_attn_o")
    g["w_o_x"] = _mm(o, dh3, "tn",name="d_w_o_x")
    dq, dk, dv = _attn_bwd(q, kv, do, bl, s, name="d_attn")
    dun = _mm(dq, wfull["w_q"], "nt", tn=1024, name="d_xattn_norm_out")
    g["w_q"] = _mm(un, dq, "tn",name="d_w_q")
    dkv = jnp.concatenate([dk, dv], axis=1)
    dmn = _mm(dkv, wfull["w_kv"], "nt", tn=1024, tk=2 * D, name="d_mem_norm_out")
    g["w_kv"] = _mm(mn, dkv, "tn", tn=512, col_shards=4, name="d_w_kv")
    emit("attn", ("w_q", "w_kv", "w_o_x"))
    _, g["mem_norm"] = _norm_bwd(mem2, gain("mem_norm"), dmn, None, name="d_mem_norm")
    dh2, g["xattn_norm"] = _norm_bwd(h2, gain("xattn_norm"), dun, dh3, name="d_xattn_norm")
    dmerged = _mm(dh2, wfull["w_mix_out"], "nt", tn=1024, name="d_merged")
    g["w_mix_out"] = _mm(merged, dh2, "tn",name="d_w_mix_out")
    dya, dyb, dpp = _merge_bwd(pp, ya, yb, dmerged, name="d_merge")
    dya_pre = _mm(dya, wfull["w_out_a"], "nt", tn=1024, name="d_conv_a_out")
    g["w_out_a"] = _mm(ya_pre, dya, "tn",name="d_w_out_a")
    dyn = _mm(dyb, wfull["w_out_ssm"], "nt", tn=DI, name="d_ssd_out")
    g["w_out_ssm"] = _mm(yn, dyb, "tn",name="d_w_out_ssm")
    dpp, dwa8 = _conv_a_bwd(pp, wa8, dya_pre, dpp, bl, s, name="d_conv_a")
    g["conv_a_w"] = dwa8[:3]
    dpp, dxs, dbm, dcm, ddtg, g["ssm_norm"], ddexp, dalg = _ssd_bwd(
        dyn, y, xc, pp, dtg, dtt, arow, acol, dexp, ng, prev, dpp, bl, s, name="d_ssd")
    g["ssm_d"] = ddexp.reshape(NH, HD).sum(axis=1).reshape(1, NH)
    g["ssm_a_log"] = _ungroup_lanes(dalg)
    conv_dw, conv_db = [], []
    for dpart, off, tag in ((dxs, 0, "x"), (dbm, DI, "b"), (dcm, DI + NG * NS, "c")):
        dpp, dw_, db_ = _conv_ssm_bwd(pp, ws8, conv_b, dpart, off, dpp, bl, s, name=f"d_conv_ssm_{tag}")
        conv_dw.append(dw_)
        conv_db.append(db_)
    g["ssm_conv_w"] = jnp.concatenate(conv_dw, axis=1)[:4]
    g["ssm_conv_b"] = jnp.concatenate(conv_db, axis=1)
    dpp, dbias = _dt_bwd(pp, bias128, ddtg, dpp, name="d_dt")
    g["ssm_dt_bias"] = dbias[:, :NH]
    g["w_in"] = _mm(u, dpp, "tn", tn=1152, name="d_w_in")
    emit("mix", ("w_in", "w_out_a", "w_out_ssm", "w_mix_out"))
    du = _mm(dpp, wfull["w_in"], "nt", tk=3456, name="d_mix_norm_out")
    dh1, g["mix_norm"] = _norm_bwd(h1, gain("mix_norm"), du, dh2, name="d_mix_norm")
    dx = ffn_bwd(dh1, x2, "ffn1_norm", "ffn1_w_gate_up", "ffn1_w_down", ffn1_saved, "ffn1")
    return sq_err, dx, g


W_IN_SHARD = NIN // 4


def _w_in_segments():
    segs, p = [], 0
    for t in range(D // CA_TILE):
        for which in range(3):
            segs.append((D * which + CA_TILE * t, p, CA_TILE))
            p += CA_TILE
    for s, n in ((3 * D, O_GA - 3 * D), (O_GA + NH, 2 * D), (O_GA, NH)):
        segs.append((s, p, n))
        p += n
    assert p == NIN and segs[-1][1] == O_DT and segs[-2][1] == O_GA
    return segs


def _pad_w_in_shards(land):
    pieces = []
    for s, _, n in _w_in_segments():
        while n > 0:
            kk, off = divmod(s, W_IN_SHARD)
            take = min(n, W_IN_SHARD - off)
            pieces.append(land[kk][:, off:off + take])
            s, n = s + take, n - take
    return jnp.concatenate(pieces + [jnp.zeros((land.shape[1], NPP - NIN), land.dtype)], axis=1)


class _WInGrad:
    def __init__(self, dwp):
        self.dwp = dwp
        self.shape = (4, dwp.shape[0], W_IN_SHARD)

    def rows(self, start, n):
        part = lax.dynamic_slice_in_dim(self.dwp, start, n, axis=0)
        shards = []
        for kk in range(4):
            n0, n1 = W_IN_SHARD * kk, W_IN_SHARD * (kk + 1)
            cuts = sorted((max(s, n0), p + max(s, n0) - s, min(s + m, n1) - max(s, n0))
                          for s, p, m in _w_in_segments() if min(s + m, n1) > max(s, n0))
            shards.append(jnp.concatenate([part[:, p:p + m] for _, p, m in cuts], axis=1))
        return jnp.stack(shards)


def _pad_w_in(w):
    return _pad_w_in_shards(jnp.stack(jnp.split(w, 4, axis=1)))


def _unpad_w_in(w):
    return jnp.concatenate(list(_WInGrad(w).rows(0, w.shape[0])), axis=1)


def kernel(x, mem, ffn1_norm, ffn1_w_gate_up, ffn1_w_down, mix_norm, w_in, conv_a_w, w_out_a, ssm_conv_w, ssm_conv_b, ssm_dt_bias, ssm_a_log, ssm_d, ssm_norm, w_out_ssm, w_mix_out, xattn_norm, mem_norm, w_q, w_kv, w_o_x, ffn2_norm, ffn2_w_gate_up, ffn2_w_down, final_norm, loss_target, m_ffn1_norm, m_ffn1_w_gate_up, m_ffn1_w_down, m_mix_norm, m_w_in, m_conv_a_w, m_w_out_a, m_ssm_conv_w, m_ssm_conv_b, m_ssm_dt_bias, m_ssm_a_log, m_ssm_d, m_ssm_norm, m_w_out_ssm, m_w_mix_out, m_xattn_norm, m_mem_norm, m_w_q, m_w_kv, m_w_o_x, m_ffn2_norm, m_ffn2_w_gate_up, m_ffn2_w_down, m_final_norm, v_ffn1_norm, v_ffn1_w_gate_up, v_ffn1_w_down, v_mix_norm, v_w_in, v_conv_a_w, v_w_out_a, v_ssm_conv_w, v_ssm_conv_b, v_ssm_dt_bias, v_ssm_a_log, v_ssm_d, v_ssm_norm, v_w_out_ssm, v_w_mix_out, v_xattn_norm, v_mem_norm, v_w_q, v_w_kv, v_w_o_x, v_ffn2_norm, v_ffn2_w_gate_up, v_ffn2_w_down, v_final_norm):
    a = dict(locals())
    xi, yi = lax.axis_index("x"), lax.axis_index("y")
    k = 2 * xi + yi

    conv_vec, conv_offs = _pack([a["conv_a_w"], a["ssm_conv_w"]])
    conv_all = _gather_all(conv_vec, name="gather_conv_w")
    wfull = _GatheredWeights({n: a[n][0] for n, _ in BIG}, k, conv_all)
    conv_sh = [_unpack(conv_all[2 * kk], conv_offs, [a["conv_a_w"].shape[1:], a["ssm_conv_w"].shape[1:]])
               for kk in range(4)]
    small = {n: a[n] for n in SMALL}
    small["conv_a_w"] = jnp.concatenate([cs[0] for cs in conv_sh], axis=1)
    small["ssm_conv_w"] = jnp.concatenate([cs[1] for cs in conv_sh], axis=1)

    rs_started = []

    def on_grads(tag, grads):
        names = [n for n, _ in BIG if n in grads]
        shard_major = [_WInGrad(grads[n]) if n == "w_in" else _shard_major(grads[n], dict(BIG)[n]) for n in names]
        st, tk = _rs_start(shard_major, tag)
        rs_started.append((tag, names, st))
        return tk[0, 0]

    sq_err, dx, g = _local_step(wfull, small, x, mem, loss_target, wfull.token, on_grads)
    loss = lax.psum(0.5 / D * jnp.sum(sq_err), ("x", "y", "c"))

    ci = lax.axis_index("c")
    out = {}

    def finish(tag, names, st, after):
        g_mine, g_other = _rs_finish(st, after, tag)
        for n, gm, go in zip(names, g_mine, g_other):
            res = _adamw_halves(a[n][0], gm, go, a["m_" + n][0], a["v_" + n][0], ci, name=f"adamw_{n}")
            out[n] = tuple(t.reshape(a[n].shape) for t in res)
        return res[1]

    done = dx
    for grp in rs_started[:-1]:
        done = finish(*grp, dx)

    full_shapes = [g[n].shape for n in SMALL]
    gvec, goffs = _pack([g[n] for n in SMALL])
    gsum = _sum_leading(_gather_all(gvec, name="gather_small_grads", before=[done]), name="sum_small_grads")
    finish(*rs_started[-1], gsum)
    gsmall = dict(zip(SMALL, _unpack(gsum, goffs, full_shapes)))
    for n in ("conv_a_w", "ssm_conv_w"):
        width = a[n].shape[2]
        gsmall[n] = lax.dynamic_slice_in_dim(gsmall[n], k * width, width, axis=1)
    local_shapes = [a[n].shape for n in SMALL]
    packs = [_pack([t[n] for n in SMALL]) for t in
             ({n: a[n] for n in SMALL}, gsmall, {n: a["m_" + n] for n in SMALL}, {n: a["v_" + n] for n in SMALL})]
    offs = packs[0][1]
    res = _adamw(*[p[0] for p in packs], name="adamw_small")
    unp = [_unpack(r, offs, local_shapes) for r in res]
    for i, n in enumerate(SMALL):
        out[n] = (gsmall[n].reshape(a[n].shape), unp[0][i], unp[1][i], unp[2][i])

    grad_x = dx.reshape(x.shape)
    return (loss, grad_x, *[out[n][0] for n in WEIGHTS], *[out[n][1] for n in WEIGHTS],
            *[out[n][2] for n in WEIGHTS], *[out[n][3] for n in WEIGHTS])
```

```python
import functools
import math

import jax
import jax.numpy as jnp
from jax import lax
from jax.experimental import pallas as pl
from jax.experimental.pallas import tpu as pltpu

F32 = jnp.float32
BF16 = jnp.bfloat16
MXU = jnp.bfloat16
HI = lax.Precision.HIGHEST

D = 1024
DFF = 2816
DI = 2048
NH, HD, NG, NS, CH = 32, 64, 4, 128, 128
GW = DI // NG
XH, XD = 4, 256
EPS = 1e-6
NEG = -1e30
CA_TILE = 256
O_CA, O_Z, O_XBC, O_GA, O_GB, O_DT, NPP = 0, 3072, 5120, 8192, 9216, 10240, 10368
NIN = 10272
FFN_RES = 0.5
ADAM_LR, ADAM_B1, ADAM_B2, ADAM_EPS, ADAM_WD, ADAM_STEP = 0.001, 0.9, 0.999, 1e-08, 0.01, 10
VMEM_LIMIT = 56 * 1024 * 1024
EPI_COLS = 256
SSD_EX = 2
MESH = pl.DeviceIdType.MESH
CHIP_FLIPS = ((1, 0), (0, 1), (1, 1))


def _cp(*sem):
    return pltpu.CompilerParams(dimension_semantics=sem, vmem_limit_bytes=VMEM_LIMIT)


def _tile(n, pref, align=128):
    if n <= pref:
        return n
    t = (pref // align) * align
    while t >= align:
        if n % t == 0:
            return t
        t -= align
    raise ValueError((n, pref))


def _dot(a, b, dims, prec=None):
    return lax.dot_general(a, b, (dims, ((), ())), preferred_element_type=F32, precision=prec)


def _nn(a, b, prec=None):
    return _dot(a, b, ((1,), (0,)), prec)


def _nt(a, b):
    return _dot(a, b, ((1,), (1,)))


def _tn(a, b):
    return _dot(a, b, ((0,), (0,)))


def _sig(x):
    return jax.nn.sigmoid(x)


def _mm(a, b, mode, *, name, tm=1024, tn=1024, tk=None, out_dtype=F32, scale=None, residual=None, a2=None,
        col_shards=0, norm_gain=None):
    if tk is None:
        tk = 2048 if mode == "tn" else 1024
    if mode == "nn":
        (M, K), (K2, N) = a.shape, b.shape
    elif mode == "nt":
        (M, K), (N, K2) = a.shape, b.shape
        if a2 is not None:
            assert a2.shape == a.shape
            K2 = K2 // 2
    else:
        (K, M), (K2, N) = a.shape, b.shape
    assert K == K2, (name, a.shape, b.shape)
    tm, tn, tk = _tile(M, tm), _tile(N, tn), _tile(K, tk)
    nk = K // tk
    if mode == "nn":
        a_spec = pl.BlockSpec((tm, tk), lambda i, j, k: (i, k))
        b_spec = pl.BlockSpec((tk, tn), lambda i, j, k: (k, j))
        dims = ((1,), (0,))
    elif mode == "nt":
        a_spec = pl.BlockSpec((tm, tk), lambda i, j, k: (i, k))
        b_spec = pl.BlockSpec((tn, tk), lambda i, j, k: (j, k))
        dims = ((1,), (1,))
    else:
        a_spec = pl.BlockSpec((tk, tm), lambda i, j, k: (k, i))
        b_spec = pl.BlockSpec((tk, tn), lambda i, j, k: (k, j))
        dims = ((0,), (0,))
    o_spec = pl.BlockSpec((tm, tn), lambda i, j, k: (i, j))
    out_spec, out_shape = o_spec, jax.ShapeDtypeStruct((M, N), out_dtype)
    if col_shards:
        per = N // col_shards // tn
        assert per * tn * col_shards == N, (name, N, tn, col_shards)
        out_spec = pl.BlockSpec((None, tm, tn), lambda i, j, k: (j // per, i, j % per))
        out_shape = jax.ShapeDtypeStruct((col_shards, M, N // col_shards), out_dtype)
    has_res = residual is not None
    has_norm = norm_gain is not None
    assert not has_norm or (tn == N and not col_shards)
    dual = a2 is not None
    n_in = 2 + 2 * dual + has_res + has_norm

    def body(*refs):
        a_ref, b_ref = refs[0], refs[1]
        o_ref = refs[n_in]

        def finish(acc):
            if scale is not None:
                acc = acc * scale
            if has_res:
                acc = acc + refs[2 + 2 * dual][...]
            o_ref[...] = acc.astype(out_dtype)
            if has_norm:
                rs = lax.rsqrt(jnp.mean(acc * acc, axis=-1, keepdims=True) + EPS)
                refs[n_in + 1][...] = (acc * rs * refs[n_in - 1][...]).astype(BF16)

        part = _dot(a_ref[...].astype(MXU), b_ref[...].astype(MXU), dims)
        if dual:
            part = part + _dot(refs[2][...].astype(MXU), refs[3][...].astype(MXU), dims)
        if nk == 1:
            finish(part)
            return
        acc_ref = refs[-1]
        k = pl.program_id(2)

        @pl.when(k == 0)
        def _():
            acc_ref[...] = part

        @pl.when(k > 0)
        def _():
            acc_ref[...] += part

        @pl.when(k == nk - 1)
        def _():
            finish(acc_ref[...])

    ins, in_specs = [a, b], [a_spec, b_spec]
    if dual:
        ins += [a2, b]
        in_specs += [a_spec, pl.BlockSpec((tn, tk), lambda i, j, k: (j, k + nk))]
    if has_res:
        ins.append(residual)
        in_specs.append(o_spec)
    if has_norm:
        ins.append(norm_gain)
        in_specs.append(pl.BlockSpec((1, tn), lambda i, j, k: (0, j)))
        out_spec, out_shape = [out_spec, o_spec], [out_shape, jax.ShapeDtypeStruct((M, N), BF16)]
    return pl.pallas_call(
        body, grid=(M // tm, N // tn, nk), in_specs=in_specs, out_specs=out_spec, out_shape=out_shape,
        scratch_shapes=[pltpu.VMEM((tm, tn), F32)] if nk > 1 else [],
        compiler_params=_cp("parallel", "parallel", "arbitrary"), name=name)(*ins)


def _norm_fwd(x, g, *, name):
    T, d = x.shape
    tr = _tile(T, 512, 8)

    def body(x_ref, g_ref, o_ref):
        xv = x_ref[...]
        r = lax.rsqrt(jnp.mean(xv * xv, axis=-1, keepdims=True) + EPS)
        o_ref[...] = (xv * r * g_ref[...]).astype(BF16)

    return pl.pallas_call(
        body, grid=(T // tr,),
        in_specs=[pl.BlockSpec((tr, d), lambda i: (i, 0)), pl.BlockSpec((1, d), lambda i: (0, 0))],
        out_specs=pl.BlockSpec((tr, d), lambda i: (i, 0)),
        out_shape=jax.ShapeDtypeStruct((T, d), BF16), compiler_params=_cp("parallel"), name=name)(x, g)


def _norm_bwd(x, g, dn, dres, *, name):
    T, d = x.shape
    tr = _tile(T, 512, 8)
    has_res = dres is not None

    def body(*refs):
        x_ref, g_ref, dn_ref = refs[:3]
        dr_ref = refs[3] if has_res else None
        dx_ref, dg_ref = refs[-2], refs[-1]

        @pl.when(pl.program_id(0) == 0)
        def _():
            dg_ref[...] = jnp.zeros_like(dg_ref)

        xv = x_ref[...]
        dnv = dn_ref[...].astype(F32)
        r = lax.rsqrt(jnp.mean(xv * xv, axis=-1, keepdims=True) + EPS)
        xh = xv * r
        dg_ref[...] += jnp.sum(dnv * xh, axis=0, keepdims=True)
        dxh = dnv * g_ref[...]
        dx = r * (dxh - xh * jnp.mean(dxh * xh, axis=-1, keepdims=True))
        if has_res:
            dx = dx + dr_ref[...]
        dx_ref[...] = dx

    row = pl.BlockSpec((tr, d), lambda i: (i, 0))
    vec = pl.BlockSpec((1, d), lambda i: (0, 0))
    ins = [x, g, dn] + ([dres] if has_res else [])
    return pl.pallas_call(
        body, grid=(T // tr,), in_specs=[row, vec, row] + ([row] if has_res else []),
        out_specs=[row, vec],
        out_shape=[jax.ShapeDtypeStruct((T, d), F32), jax.ShapeDtypeStruct((1, d), F32)],
        compiler_params=_cp("arbitrary"), name=name)(*ins)


def _final_loss(h, g, target, *, name):
    T, d = h.shape
    tr = _tile(T, 512, 8)

    def body(h_ref, g_ref, t_ref, l_ref, dh_ref, dg_ref):
        @pl.when(pl.program_id(0) == 0)
        def _():
            l_ref[...] = jnp.zeros_like(l_ref)
            dg_ref[...] = jnp.zeros_like(dg_ref)

        xv = h_ref[...]
        r = lax.rsqrt(jnp.mean(xv * xv, axis=-1, keepdims=True) + EPS)
        xh = xv * r
        e = xh * g_ref[...] - t_ref[...]
        l_ref[...] += jnp.sum(e * e, axis=0, keepdims=True)
        dy = e * (1.0 / d)
        dg_ref[...] += jnp.sum(dy * xh, axis=0, keepdims=True)
        dxh = dy * g_ref[...]
        dh_ref[...] = r * (dxh - xh * jnp.mean(dxh * xh, axis=-1, keepdims=True))

    row = pl.BlockSpec((tr, d), lambda i: (i, 0))
    vec = pl.BlockSpec((1, d), lambda i: (0, 0))
    return pl.pallas_call(
        body, grid=(T // tr,), in_specs=[row, vec, row], out_specs=[vec, row, vec],
        out_shape=[jax.ShapeDtypeStruct((1, d), F32), jax.ShapeDtypeStruct((T, d), F32),
                   jax.ShapeDtypeStruct((1, d), F32)],
        compiler_params=_cp("arbitrary"), name=name)(h, g, target)


def _gate_up_fwd(n, wgu, *, name):
    T, d = n.shape
    f = wgu.shape[1] // 2
    tm, tn = _tile(T, 512, 8), _tile(f, DFF)
    nf = f // tn

    tc = _tile(tn, EPI_COLS)

    def body(n_ref, wg_ref, wu_ref, g_ref, u_ref, a_ref):
        nv = n_ref[...].astype(MXU)
        for j in range(tn // tc):
            sl = slice(j * tc, (j + 1) * tc)
            gv = _nn(nv, wg_ref[:, sl].astype(MXU))
            uv = _nn(nv, wu_ref[:, sl].astype(MXU))
            g_ref[:, sl] = gv.astype(BF16)
            u_ref[:, sl] = uv.astype(BF16)
            a_ref[:, sl] = (gv * _sig(gv) * uv).astype(BF16)

    out = pl.BlockSpec((tm, tn), lambda i, j: (i, j))
    act = jax.ShapeDtypeStruct((T, f), BF16)
    return pl.pallas_call(
        body, grid=(T // tm, nf),
        in_specs=[pl.BlockSpec((tm, d), lambda i, j: (i, 0)), pl.BlockSpec((d, tn), lambda i, j: (0, j)),
                  pl.BlockSpec((d, tn), lambda i, j: (0, j + nf))],
        out_specs=[out, out, out], out_shape=[act, act, act], compiler_params=_cp("parallel", "parallel"),
        name=name)(n, wgu, wgu)


def _act_bwd(dh, wd, gate, up, scale, *, name):
    T, d = dh.shape
    f = wd.shape[0]
    tm, tn = _tile(T, 512, 8), _tile(f, DFF)

    tc = _tile(tn, EPI_COLS)

    def body(dh_ref, wd_ref, g_ref, u_ref, dg_ref, du_ref):
        dhv = dh_ref[...].astype(MXU)
        for j in range(tn // tc):
            sl = slice(j * tc, (j + 1) * tc)
            da = scale * _nt(dhv, wd_ref[sl, :].astype(MXU))
            gv, uv = g_ref[:, sl].astype(F32), u_ref[:, sl].astype(F32)
            s = _sig(gv)
            dg_ref[:, sl] = (da * uv * (s * (1.0 + gv * (1.0 - s)))).astype(BF16)
            du_ref[:, sl] = (da * (gv * s)).astype(BF16)

    tile = pl.BlockSpec((tm, tn), lambda i, j: (i, j))
    act = jax.ShapeDtypeStruct((T, f), BF16)
    return pl.pallas_call(
        body, grid=(T // tm, f // tn),
        in_specs=[pl.BlockSpec((tm, d), lambda i, j: (i, 0)), pl.BlockSpec((tn, d), lambda i, j: (j, 0)), tile, tile],
        out_specs=[tile, tile], out_shape=[act, act], compiler_params=_cp("parallel", "parallel"),
        name=name)(dh, wd, gate, up)


CONV_ROWS = 64
CONV_PAD = 8


def _rows_down(ref, r0, d, cols=slice(None)):
    if r0 - d >= 0:
        return ref[pl.ds(r0 - d, CONV_ROWS), cols]
    assert r0 == 0
    v = ref[pl.ds(0, CONV_ROWS), cols]
    ri = lax.broadcasted_iota(jnp.int32, v.shape, 0)
    return jnp.where(ri >= d, pltpu.roll(v, d, 0), 0.0)


def _fold8(v):
    return jnp.sum(v.reshape(CONV_ROWS // 8, 8, v.shape[1]), axis=0)


def _taps(w_ref, views):
    acc = None
    for k, v in enumerate(views):
        t = w_ref[k:k + 1, :] * v
        acc = t if acc is None else acc + t
    return acc


def _conv_a_fwd(pp, w8, bl, s, *, name):
    tc = CA_TILE
    nb = D // tc
    bcol, ccol, vcol = slice(0, tc), slice(tc, 2 * tc), slice(2 * tc, 3 * tc)

    def body(p_ref, w_ref, o_ref):
        for r0 in range(0, s, CONV_ROWS):
            cv = [_rows_down(p_ref, r0, 2 - k, ccol) * _rows_down(p_ref, r0, 2 - k, vcol) for k in range(3)]
            o_ref[pl.ds(r0, CONV_ROWS), :] = (p_ref[pl.ds(r0, CONV_ROWS), bcol] * _taps(w_ref, cv)).astype(BF16)

    return pl.pallas_call(
        body, grid=(bl, nb),
        in_specs=[pl.BlockSpec((s, 3 * tc), lambda b, j: (b, j)), pl.BlockSpec((8, tc), lambda b, j: (0, j))],
        out_specs=pl.BlockSpec((s, tc), lambda b, j: (b, j)),
        out_shape=jax.ShapeDtypeStruct((bl * s, D), BF16), compiler_params=_cp("parallel", "parallel"),
        name=name)(pp, w8)


def _conv_a_bwd(pp, w8, dya, dpp, bl, s, *, name):
    tc = CA_TILE
    nb = D // tc
    bcol, ccol, vcol = slice(0, tc), slice(tc, 2 * tc), slice(2 * tc, 3 * tc)

    def body(p_ref, w_ref, dy_ref, dpp_in, d_ref, dw_ref, dcp):
        del dpp_in

        @pl.when(pl.program_id(1) == 0)
        def _():
            dw_ref[...] = jnp.zeros_like(dw_ref)

        dcp[pl.ds(s, CONV_PAD), :] = jnp.zeros((CONV_PAD, tc), F32)
        dw_acc = [jnp.zeros((8, tc), F32) for _ in range(3)]
        for r0 in reversed(range(0, s, CONV_ROWS)):
            rows = pl.ds(r0, CONV_ROWS)
            cs = [_rows_down(p_ref, r0, 2 - k, ccol) for k in range(3)]
            vs = [_rows_down(p_ref, r0, 2 - k, vcol) for k in range(3)]
            cv = [c_ * v_ for c_, v_ in zip(cs, vs)]
            dy = dy_ref[rows, :]
            d_ref[rows, bcol] = (dy * _taps(w_ref, cv)).astype(BF16)
            dconv = dy * p_ref[rows, bcol]
            dcp[rows, :] = dconv
            dcv = _taps(w_ref, [dcp[pl.ds(r0 + 2, CONV_ROWS), :], dcp[pl.ds(r0 + 1, CONV_ROWS), :], dconv])
            d_ref[rows, ccol] = (dcv * vs[2]).astype(BF16)
            d_ref[rows, vcol] = (dcv * cs[2]).astype(BF16)
            dw_acc = [acc + _fold8(dconv * cv_) for acc, cv_ in zip(dw_acc, cv)]
        for k in range(3):
            dw_ref[k:k + 1, :] += jnp.sum(dw_acc[k], axis=0, keepdims=True)

    wspec = pl.BlockSpec((8, tc), lambda j, b: (0, j))
    wide = pl.BlockSpec((s, 3 * tc), lambda j, b: (b, j))
    return pl.pallas_call(
        body, grid=(nb, bl),
        in_specs=[wide, wspec, pl.BlockSpec((s, tc), lambda j, b: (b, j)), pl.BlockSpec(memory_space=pl.ANY)],
        out_specs=[wide, wspec], out_shape=[jax.ShapeDtypeStruct(dpp.shape, dpp.dtype), jax.ShapeDtypeStruct((8, D), F32)],
        scratch_shapes=[pltpu.VMEM((s + CONV_PAD, tc), F32)], input_output_aliases={3: 0},
        compiler_params=_cp("parallel", "arbitrary"), name=name)(pp, w8, dya, dpp)


def _conv_ssm_fwd(pp, w8, bias, bl, s, *, name):
    tc = 256
    width = DI + 2 * NG * NS
    nb = width // tc

    def body(x_ref, w_ref, b_ref, o_ref):
        for r0 in range(0, s, CONV_ROWS):
            pre = _taps(w_ref, [_rows_down(x_ref, r0, 3 - k) for k in range(4)]) + b_ref[...]
            o_ref[pl.ds(r0, CONV_ROWS), :] = pre * _sig(pre)

    return pl.pallas_call(
        body, grid=(bl, nb),
        in_specs=[pl.BlockSpec((s, tc), lambda b, j: (b, O_XBC // tc + j)),
                  pl.BlockSpec((8, tc), lambda b, j: (0, j)), pl.BlockSpec((1, tc), lambda b, j: (0, j))],
        out_specs=pl.BlockSpec((s, tc), lambda b, j: (b, j)),
        out_shape=jax.ShapeDtypeStruct((bl * s, width), F32), compiler_params=_cp("parallel", "parallel"),
        name=name)(pp, w8, bias)


def _conv_ssm_bwd(pp, w8, bias, dxc, ch_off, dpp, bl, s, *, name):
    n = dxc.shape[1]
    tc = 256
    nb = n // tc
    o0 = ch_off // tc

    def body(x_ref, w_ref, b_ref, d_ref, dpp_in, dx_ref, dw_ref, db_ref, dp):
        del dpp_in

        @pl.when(pl.program_id(1) == 0)
        def _():
            dw_ref[...] = jnp.zeros_like(dw_ref)
            db_ref[...] = jnp.zeros_like(db_ref)

        dp[pl.ds(s, CONV_PAD), :] = jnp.zeros((CONV_PAD, tc), F32)
        dw_acc = [jnp.zeros((8, tc), F32) for _ in range(4)]
        db_acc = jnp.zeros((8, tc), F32)
        for r0 in reversed(range(0, s, CONV_ROWS)):
            rows = pl.ds(r0, CONV_ROWS)
            xs = [_rows_down(x_ref, r0, 3 - k) for k in range(4)]
            pre = _taps(w_ref, xs) + b_ref[...]
            sg = _sig(pre)
            dpre = d_ref[rows, :] * (sg * (1.0 + pre * (1.0 - sg)))
            dp[rows, :] = dpre
            dx = _taps(w_ref, [dp[pl.ds(r0 + 3 - k, CONV_ROWS), :] for k in range(3)] + [dpre])
            dx_ref[rows, :] = dx.astype(BF16)
            db_acc = db_acc + _fold8(dpre)
            dw_acc = [acc + _fold8(dpre * x_) for acc, x_ in zip(dw_acc, xs)]
        db_ref[...] += jnp.sum(db_acc, axis=0, keepdims=True)
        for k in range(4):
            dw_ref[k:k + 1, :] += jnp.sum(dw_acc[k], axis=0, keepdims=True)

    return pl.pallas_call(
        body, grid=(nb, bl),
        in_specs=[pl.BlockSpec((s, tc), lambda j, b: (b, O_XBC // tc + o0 + j)),
                  pl.BlockSpec((8, tc), lambda j, b: (0, o0 + j)), pl.BlockSpec((1, tc), lambda j, b: (0, o0 + j)),
                  pl.BlockSpec((s, tc), lambda j, b: (b, j)), pl.BlockSpec(memory_space=pl.ANY)],
        out_specs=[pl.BlockSpec((s, tc), lambda j, b: (b, O_XBC // tc + o0 + j)),
                   pl.BlockSpec((8, tc), lambda j, b: (0, j)), pl.BlockSpec((1, tc), lambda j, b: (0, j))],
        out_shape=[jax.ShapeDtypeStruct(dpp.shape, dpp.dtype), jax.ShapeDtypeStruct((8, n), F32),
                   jax.ShapeDtypeStruct((1, n), F32)],
        scratch_shapes=[pltpu.VMEM((s + CONV_PAD, tc), F32)], input_output_aliases={4: 0},
        compiler_params=_cp("parallel", "arbitrary"), name=name)(pp, w8, bias, dxc, dpp)


def _softplus(x):
    return jnp.maximum(x, 0.0) + jnp.log1p(jnp.exp(-jnp.abs(x)))


def _head_group_matrix():
    h = jnp.arange(128)[:, None]
    j = jnp.arange(NG * 128)[None, :]
    per = NH // NG
    return ((h < NH) & (j == (h // per) * 128 + h % per)).astype(F32)


def _dt_fwd(pp, bias128, *, name):
    T = pp.shape[0]
    tr = _tile(T, 1024, 8)
    per = NH // NG

    def body(x_ref, b_ref, p_ref, g_ref, t_ref):
        lane = lax.broadcasted_iota(jnp.int32, (tr, 128), 1)
        dt = jnp.where(lane < NH, _softplus(x_ref[...] + b_ref[...]), 0.0)
        g_ref[...] = _nn(dt, p_ref[...], HI)
        eye = (lax.broadcasted_iota(jnp.int32, (NH, 128), 0)
               == lax.broadcasted_iota(jnp.int32, (NH, 128), 1)).astype(F32)
        t_ref[...] = _dot(eye, dt, ((1,), (1,)), HI).reshape(NG, per, tr)

    vec = pl.BlockSpec((1, 128), lambda i: (0, 0))
    return pl.pallas_call(
        body, grid=(T // tr,),
        in_specs=[pl.BlockSpec((tr, 128), lambda i: (i, O_DT // 128)), vec,
                  pl.BlockSpec((128, NG * 128), lambda i: (0, 0))],
        out_specs=[pl.BlockSpec((tr, NG * 128), lambda i: (i, 0)), pl.BlockSpec((NG, per, tr), lambda i: (0, 0, i))],
        out_shape=[jax.ShapeDtypeStruct((T, NG * 128), F32), jax.ShapeDtypeStruct((NG, per, T), F32)],
        compiler_params=_cp("parallel"), name=name)(pp, bias128, _head_group_matrix())


def _dt_bwd(pp, bias128, ddtg, dpp, *, name):
    T = pp.shape[0]
    tr = _tile(T, 1024, 8)

    def body(x_ref, b_ref, d_ref, p_ref, dpp_in, o_ref, db_ref):
        del dpp_in

        @pl.when(pl.program_id(0) == 0)
        def _():
            db_ref[...] = jnp.zeros_like(db_ref)

        lane = lax.broadcasted_iota(jnp.int32, (tr, 128), 1)
        ddt = _dot(d_ref[...], p_ref[...], ((1,), (1,)), HI)
        dr = jnp.where(lane < NH, ddt * _sig(x_ref[...] + b_ref[...]), 0.0)
        db_ref[...] += jnp.sum(dr, axis=0, keepdims=True)
        o_ref[...] = dr.astype(BF16)

    col = pl.BlockSpec((tr, 128), lambda i: (i, O_DT // 128))
    vec = pl.BlockSpec((1, 128), lambda i: (0, 0))
    return pl.pallas_call(
        body, grid=(T // tr,),
        in_specs=[col, vec, pl.BlockSpec((tr, NG * 128), lambda i: (i, 0)), pl.BlockSpec((128, NG * 128), lambda i: (0, 0)),
                  pl.BlockSpec(memory_space=pl.ANY)],
        out_specs=[col, vec],
        out_shape=[jax.ShapeDtypeStruct(dpp.shape, dpp.dtype), jax.ShapeDtypeStruct((1, 128), F32)],
        input_output_aliases={4: 0}, compiler_params=_cp("arbitrary"),
        name=name)(pp, bias128, ddtg, _head_group_matrix(), dpp)


def _tril():
    return lax.broadcasted_iota(jnp.int32, (CH, CH), 0) >= lax.broadcasted_iota(jnp.int32, (CH, CH), 1)


def _ssd_common(dt, dtt, arow, acol):
    ri = lax.broadcasted_iota(jnp.int32, (CH, CH), 0)
    ci = lax.broadcasted_iota(jnp.int32, (CH, CH), 1)
    tril = ri >= ci
    triu = ri <= ci
    acs_col = _nn(tril.astype(F32), dt * arow, HI)
    acs_row = _nn(dtt * acol, triu.astype(F32), HI)
    return tril, triu, acs_col, acs_row


def _pair_terms(q, dt, acs_col, acs_row, tril, lo):
    ha, hb = 2 * q, 2 * q + 1
    col_a, col_b = acs_col[:, ha:ha + 1], acs_col[:, hb:hb + 1]
    row_a, row_b = acs_row[ha:ha + 1, :], acs_row[hb:hb + 1, :]
    last_a, last_b = acs_col[CH - 1:CH, ha:ha + 1], acs_col[CH - 1:CH, hb:hb + 1]
    out = dict(
        dtsel=jnp.where(lo, dt[:, ha:ha + 1], dt[:, hb:hb + 1]),
        d_a=jnp.exp(jnp.where(tril, col_a - row_a, NEG)), d_b=jnp.exp(jnp.where(tril, col_b - row_b, NEG)),
        esel=jnp.where(lo, jnp.exp(col_a), jnp.exp(col_b)),
        fsel=jnp.where(lo, jnp.exp(last_a - col_a), jnp.exp(last_b - col_b)),
        g_a=jnp.exp(last_a), g_b=jnp.exp(last_b))
    return out


def _ssd_fwd(xc, pp, dtg, dtt, arow, acol, dexp, ng, bl, s, *, name):
    nc = s // CH
    T = bl * s

    ex = SSD_EX if bl % SSD_EX == 0 else 1

    def body(*refs):
        arow_ref, acol_ref, dexp_ref, ng_ref = refs[6 * ex:6 * ex + 4]
        st_ref = refs[-1]

        @pl.when(pl.program_id(2) == 0)
        def _():
            st_ref[...] = jnp.zeros_like(st_ref)

        y_ref, yn_ref, prev_ref = refs[6 * ex + 4:6 * ex + 7]
        for e in range(ex):
            one(*refs[6 * e:6 * e + 6], arow_ref, acol_ref, dexp_ref, ng_ref,
                y_ref.at[e], yn_ref.at[e], prev_ref.at[e], st_ref.at[e])

    def one(xs_ref, bm_ref, cm_ref, z_ref, dt_ref, dtt_ref, arow_ref, acol_ref, dexp_ref, ng_ref,
            y_ref, yn_ref, prev_ref, st_ref):
        dt = dt_ref[...]
        tril, _, acs_col, acs_row = _ssd_common(dt, dtt_ref[...], -jnp.exp(arow_ref[...]), -jnp.exp(acol_ref[...]))
        bm, cm = bm_ref[...].astype(MXU), cm_ref[...].astype(MXU)
        cb = _nt(cm, bm)
        lo = lax.broadcasted_iota(jnp.int32, (CH, 128), 1) < HD
        sub_lo = lax.broadcasted_iota(jnp.int32, (128, NS), 0) < HD
        for q in range(4):
            t = _pair_terms(q, dt, acs_col, acs_row, tril, lo)
            x = xs_ref[:, 128 * q:128 * (q + 1)]
            xd = x * t["dtsel"]
            y = (_nn((cb * t["d_a"]).astype(MXU), jnp.where(lo, xd, 0.0).astype(MXU))
                 + _nn((cb * t["d_b"]).astype(MXU), jnp.where(lo, 0.0, xd).astype(MXU)))
            prev = st_ref[q]
            prev_ref[q] = prev
            y = y + t["esel"] * _nt(cm, prev.astype(MXU))
            st_ref[q] = prev * jnp.where(sub_lo, t["g_a"], t["g_b"]) + _tn((xd * t["fsel"]).astype(MXU), bm)
            y_ref[:, 128 * q:128 * (q + 1)] = y + dexp_ref[:, 128 * q:128 * (q + 1)] * x
        zv = z_ref[...]
        yg = y_ref[...] * (zv * _sig(zv))
        r = lax.rsqrt(jnp.mean(yg * yg, axis=-1, keepdims=True) + EPS)
        yn_ref[...] = (yg * r * ng_ref[...]).astype(BF16)

    def row(e, width, off_blocks):
        return pl.BlockSpec((CH, width), lambda g, b, c: ((b * ex + e) * nc + c, off_blocks + g))

    per_ex_in = [[row(e, GW, 0), row(e, NS, DI // NS), row(e, NS, DI // NS + NG), row(e, GW, O_Z // GW), row(e, 128, 0),
                  pl.BlockSpec((None, 8, CH), lambda g, b, c, e=e: (g, 0, (b * ex + e) * nc + c))] for e in range(ex)]
    by_example = pl.BlockSpec((ex, CH, GW), lambda g, b, c: (b, c, g))
    y, yn, prev = pl.pallas_call(
        body, grid=(NG, bl // ex, nc),
        in_specs=sum(per_ex_in, []) + [pl.BlockSpec((1, 128), lambda g, b, c: (0, g)),
                                       pl.BlockSpec((None, 8, 1), lambda g, b, c: (g, 0, 0)),
                                       pl.BlockSpec((1, GW), lambda g, b, c: (0, g)),
                                       pl.BlockSpec((1, GW), lambda g, b, c: (0, g))],
        out_specs=[by_example, by_example,
                   pl.BlockSpec((ex, None, 4, 128, NS), lambda g, b, c: (b, c, g, 0, 0))],
        out_shape=[jax.ShapeDtypeStruct((bl, s, DI), F32), jax.ShapeDtypeStruct((bl, s, DI), BF16),
                   jax.ShapeDtypeStruct((bl, nc, 16, 128, NS), F32)],
        scratch_shapes=[pltpu.VMEM((ex, 4, 128, NS), F32)],
        compiler_params=_cp("parallel", "parallel", "arbitrary"), name=name,
    )(*([xc, xc, xc, pp, dtg, dtt] * ex), arow, acol, dexp, ng)
    return y.reshape(T, DI), yn.reshape(T, DI), prev.reshape(bl * nc, 16, 128, NS)


def _ssd_bwd(dyn, y, xc, pp, dtg, dtt, arow, acol, dexp, ng, prev, dpp, bl, s, *, name):
    nc = s // CH
    T = bl * s

    def rsum(v):
        return jnp.sum(v, axis=1, keepdims=True)

    def asum(v):
        return jnp.sum(jnp.sum(v, axis=0, keepdims=True), axis=1, keepdims=True)

    def body(dyn_ref, y_ref, xs_ref, bm_ref, cm_ref, z_ref, dt_ref, dtt_ref, arow_ref, acol_ref, dexp_ref, ng_ref,
             prev_ref, dpp_in, dz_ref, dxs_ref, db_ref, dc_ref, ddt_ref, dng_ref, dd_ref, dal_ref, dst_ref):
        del dpp_in

        @pl.when((pl.program_id(1) == 0) & (pl.program_id(2) == 0))
        def _():
            dng_ref[...] = jnp.zeros_like(dng_ref)
            dd_ref[...] = jnp.zeros_like(dd_ref)
            dal_ref[...] = jnp.zeros_like(dal_ref)

        @pl.when(pl.program_id(2) == 0)
        def _():
            dst_ref[...] = jnp.zeros_like(dst_ref)

        yv, zv, xsv, dexp_v = y_ref[...], z_ref[...], xs_ref[...], dexp_ref[...]
        sz = _sig(zv)
        silu = zv * sz
        yg = yv * silu
        r = lax.rsqrt(jnp.mean(yg * yg, axis=-1, keepdims=True) + EPS)
        yh = yg * r
        dynv = dyn_ref[...]
        dng_ref[...] += jnp.sum(dynv * yh, axis=0, keepdims=True)
        dyh = dynv * ng_ref[...]
        dyg = r * (dyh - yh * jnp.mean(dyh * yh, axis=-1, keepdims=True))
        dz_ref[...] = (dyg * yv * (sz * (1.0 + zv * (1.0 - sz)))).astype(BF16)
        dy_all = dyg * silu
        dd_ref[...] += jnp.sum(dy_all * xsv, axis=0, keepdims=True)

        dt = dt_ref[...]
        arow_v = -jnp.exp(arow_ref[...])
        tril, triu, acs_col, acs_row = _ssd_common(dt, dtt_ref[...], arow_v, -jnp.exp(acol_ref[...]))
        bm, cm = bm_ref[...].astype(MXU), cm_ref[...].astype(MXU)
        cb = _nt(cm, bm)
        lane = lax.broadcasted_iota(jnp.int32, (CH, 128), 1)
        is_last = lax.broadcasted_iota(jnp.int32, (CH, 128), 0) == CH - 1
        lo = lane < HD
        sub_lo = lax.broadcasted_iota(jnp.int32, (128, NS), 0) < HD
        dcb = jnp.zeros((CH, CH), F32)
        dc_acc = jnp.zeros((CH, NS), F32)
        db_acc = jnp.zeros((CH, NS), F32)
        dacs = jnp.zeros((CH, 128), F32)
        ddtx = jnp.zeros((CH, 128), F32)
        csum = jnp.zeros((8, CH), F32)
        sub8 = lax.broadcasted_iota(jnp.int32, (8, CH), 0)
        for q in range(4):
            ha, hb = 2 * q, 2 * q + 1
            sl = slice(128 * q, 128 * (q + 1))
            t = _pair_terms(q, dt, acs_col, acs_row, tril, lo)
            x, dy = xsv[:, sl], dy_all[:, sl]
            xd = x * t["dtsel"]
            xd_m = xd.astype(MXU)
            dy_lo, dy_hi = jnp.where(lo, dy, 0.0).astype(MXU), jnp.where(lo, 0.0, dy).astype(MXU)
            m_a, m_b = cb * t["d_a"], cb * t["d_b"]
            prev_m = prev_ref[q].astype(MXU)
            dnext = dst_ref[q]
            dnext_m = dnext.astype(MXU)
            bds = _nt(bm, dnext_m)
            dxd = _tn(m_a.astype(MXU), dy_lo) + _tn(m_b.astype(MXU), dy_hi) + t["fsel"] * bds
            dye_m = (dy * t["esel"]).astype(MXU)
            dst_ref[q] = dnext * jnp.where(sub_lo, t["g_a"], t["g_b"]) + _tn(dye_m, cm)
            dm_a, dm_b = _nt(dy_lo, xd_m), _nt(dy_hi, xd_m)
            dcb = dcb + dm_a * t["d_a"] + dm_b * t["d_b"]
            g_a, g_b = dm_a * m_a, dm_b * m_b
            csum = (csum + jnp.where(sub8 == ha, jnp.sum(g_a, axis=0, keepdims=True), 0.0)
                    + jnp.where(sub8 == hb, jnp.sum(g_b, axis=0, keepdims=True), 0.0))
            tf = t["fsel"] * xd * bds
            tyf = dy * (t["esel"] * _nt(cm, prev_m)) - tf
            dpp = dnext * prev_ref[q]
            ea = asum(jnp.where(lo, tf, 0.0)) + t["g_a"] * asum(jnp.where(sub_lo, dpp, 0.0))
            eb = asum(jnp.where(lo, 0.0, tf)) + t["g_b"] * asum(jnp.where(sub_lo, 0.0, dpp))
            ra = rsum(g_a + jnp.where(lo, tyf, 0.0)) + jnp.where(is_last, ea, 0.0)
            rb = rsum(g_b + jnp.where(lo, 0.0, tyf)) + jnp.where(is_last, eb, 0.0)
            dacs = dacs + jnp.where(lane == ha, ra, 0.0) + jnp.where(lane == hb, rb, 0.0)
            tx = dxd * x
            ddtx = (ddtx + jnp.where(lane == ha, rsum(jnp.where(lo, tx, 0.0)), 0.0)
                    + jnp.where(lane == hb, rsum(jnp.where(lo, 0.0, tx)), 0.0))
            dxs_ref[:, sl] = dxd * t["dtsel"] + dexp_v[:, sl] * dy
            dc_acc = dc_acc + _nn(dye_m, prev_m)
            db_acc = db_acc + _nn((xd * t["fsel"]).astype(MXU), dnext_m)
        dcb_m = dcb.astype(MXU)
        dc_ref[...] = dc_acc + _nn(dcb_m, bm)
        db_ref[...] = db_acc + _tn(dcb_m, cm)
        dacs = dacs - jnp.concatenate([csum, jnp.zeros((CH - 8, CH), F32)], axis=0).T
        dla = _nn(triu.astype(F32), dacs, HI)
        ddt_ref[...] = dla * arow_v + ddtx
        dal_ref[...] += jnp.sum(dla * dt, axis=0, keepdims=True) * arow_v

    def row(width, off_blocks):
        return pl.BlockSpec((CH, width), lambda g, b, c: (b * nc + nc - 1 - c, off_blocks + g))

    gvec = pl.BlockSpec((1, GW), lambda g, b, c: (0, g))
    hvec = pl.BlockSpec((1, 128), lambda g, b, c: (0, g))
    return pl.pallas_call(
        body, grid=(NG, bl, nc),
        in_specs=[row(GW, 0), row(GW, 0), row(GW, 0), row(NS, DI // NS), row(NS, DI // NS + NG), row(GW, O_Z // GW),
                  row(128, 0), pl.BlockSpec((None, 8, CH), lambda g, b, c: (g, 0, b * nc + nc - 1 - c)),
                  hvec, pl.BlockSpec((None, 8, 1), lambda g, b, c: (g, 0, 0)), gvec, gvec,
                  pl.BlockSpec((None, 4, 128, NS), lambda g, b, c: (b * nc + nc - 1 - c, g, 0, 0)),
                  pl.BlockSpec(memory_space=pl.ANY)],
        out_specs=[row(GW, O_Z // GW), row(GW, 0), row(NS, 0), row(NS, 0), row(128, 0), gvec, gvec, hvec],
        input_output_aliases={13: 0},
        out_shape=[jax.ShapeDtypeStruct(dpp.shape, dpp.dtype), jax.ShapeDtypeStruct((T, DI), F32),
                   jax.ShapeDtypeStruct((T, NG * NS), F32), jax.ShapeDtypeStruct((T, NG * NS), F32),
                   jax.ShapeDtypeStruct((T, NG * 128), F32), jax.ShapeDtypeStruct((1, DI), F32),
                   jax.ShapeDtypeStruct((1, DI), F32), jax.ShapeDtypeStruct((1, NG * 128), F32)],
        scratch_shapes=[pltpu.VMEM((4, 128, NS), F32)],
        compiler_params=_cp("arbitrary", "arbitrary", "arbitrary"), name=name,
    )(dyn, y, xc, xc, xc, pp, dtg, dtt, arow, acol, dexp, ng, prev, dpp)


def _merge_fwd(pp, ya, yb, *, name):
    T = ya.shape[0]
    tr = _tile(T, 512, 8)

    def body(ga_ref, gb_ref, ya_ref, yb_ref, o_ref):
        o_ref[...] = (_sig(ga_ref[...]) * ya_ref[...] + _sig(gb_ref[...]) * yb_ref[...]).astype(BF16)

    row = pl.BlockSpec((tr, D), lambda i: (i, 0))
    return pl.pallas_call(
        body, grid=(T // tr,),
        in_specs=[pl.BlockSpec((tr, D), lambda i: (i, O_GA // D)), pl.BlockSpec((tr, D), lambda i: (i, O_GB // D)),
                  row, row],
        out_specs=row, out_shape=jax.ShapeDtypeStruct((T, D), BF16), compiler_params=_cp("parallel"),
        name=name)(pp, pp, ya, yb)


def _merge_bwd(pp, ya, yb, dm, *, name):
    T = ya.shape[0]
    tr = _tile(T, 512, 8)
    assert O_GB == O_GA + D and O_GA % (2 * D) == 0

    def body(g_ref, ya_ref, yb_ref, dm_ref, dya_ref, dyb_ref, dg_ref):
        sa, sb, dmv = _sig(g_ref[:, :D]), _sig(g_ref[:, D:]), dm_ref[...]
        dya_ref[...] = (dmv * sa).astype(BF16)
        dyb_ref[...] = (dmv * sb).astype(BF16)
        dg_ref[:, :D] = (dmv * ya_ref[...] * (sa * (1.0 - sa))).astype(BF16)
        dg_ref[:, D:] = (dmv * yb_ref[...] * (sb * (1.0 - sb))).astype(BF16)

    row = pl.BlockSpec((tr, D), lambda i: (i, 0))
    gates = pl.BlockSpec((tr, 2 * D), lambda i: (i, O_GA // (2 * D)))
    act = jax.ShapeDtypeStruct((T, D), BF16)
    return pl.pallas_call(
        body, grid=(T // tr,), in_specs=[gates, row, row, row], out_specs=[row, row, gates],
        out_shape=[act, act, jax.ShapeDtypeStruct((T, NPP), BF16)], compiler_params=_cp("parallel"),
        name=name)(pp, ya, yb, dm)


def _softmax_rows(sc):
    e = jnp.exp(sc - jnp.max(sc, axis=-1, keepdims=True))
    return e / jnp.sum(e, axis=-1, keepdims=True)


def _attn_fwd(q, kv, bl, s, *, name):
    m = kv.shape[0] // bl
    tq = _tile(s, 512)
    nq = s // tq
    scale = 1.0 / math.sqrt(XD)

    def body(q_ref, k_ref, v_ref, o_ref):
        p = _softmax_rows(_nt(q_ref[...], k_ref[...]) * scale)
        o_ref[...] = _nn(p.astype(MXU), v_ref[...]).astype(BF16)

    qspec = pl.BlockSpec((tq, XD), lambda b, h, i: (b * nq + i, h))
    return pl.pallas_call(
        body, grid=(bl, XH, nq),
        in_specs=[qspec, pl.BlockSpec((m, XD), lambda b, h, i: (b, h)),
                  pl.BlockSpec((m, XD), lambda b, h, i: (b, XH + h))],
        out_specs=qspec, out_shape=jax.ShapeDtypeStruct((bl * s, D), BF16),
        compiler_params=_cp("parallel", "parallel", "parallel"), name=name)(q, kv, kv)


def _attn_bwd(q, kv, do, bl, s, *, name):
    m = kv.shape[0] // bl
    tq = _tile(s, 512)
    nq = s // tq
    scale = 1.0 / math.sqrt(XD)

    def body(q_ref, k_ref, v_ref, do_ref, dq_ref, dk_ref, dv_ref):
        @pl.when(pl.program_id(2) == 0)
        def _():
            dk_ref[...] = jnp.zeros_like(dk_ref)
            dv_ref[...] = jnp.zeros_like(dv_ref)

        qv, kvv, vv, dov = q_ref[...], k_ref[...], v_ref[...], do_ref[...]
        p = _softmax_rows(_nt(qv, kvv) * scale)
        dp = _nt(dov, vv)
        ds = (p * (dp - jnp.sum(dp * p, axis=-1, keepdims=True)) * scale).astype(MXU)
        dq_ref[...] = _nn(ds, kvv).astype(BF16)
        dk_ref[...] += _tn(ds, qv)
        dv_ref[...] += _tn(p.astype(MXU), dov)

    qspec = pl.BlockSpec((tq, XD), lambda b, h, i: (b * nq + i, h))
    kspec = pl.BlockSpec((m, XD), lambda b, h, i: (b, h))
    return pl.pallas_call(
        body, grid=(bl, XH, nq),
        in_specs=[qspec, kspec, pl.BlockSpec((m, XD), lambda b, h, i: (b, XH + h)), qspec],
        out_specs=[qspec, kspec, kspec],
        out_shape=[jax.ShapeDtypeStruct((bl * s, D), BF16), jax.ShapeDtypeStruct((bl * m, D), F32),
                   jax.ShapeDtypeStruct((bl * m, D), F32)],
        compiler_params=_cp("parallel", "parallel", "arbitrary"), name=name)(q, kv, kv, do)


def _row_tile(r, c, max_elems=512 * 1024, align=16):
    best = None
    for t in range(align, r + 1, align):
        if r % t == 0 and t * c <= max_elems:
            best = t
    return best if best is not None else r


def _addn(a, others, *, name, also_bf16=False):
    r, c = a.shape
    tr = _row_tile(r, c)
    n = len(others)

    def body(*refs):
        acc = refs[0][...].astype(F32)
        for o_ref in refs[1:1 + n]:
            acc = acc + o_ref[...].astype(F32)
        refs[1 + n][...] = acc
        if also_bf16:
            refs[2 + n][...] = acc.astype(BF16)

    spec = pl.BlockSpec((tr, c), lambda i: (i, 0))
    shapes = [jax.ShapeDtypeStruct((r, c), F32)] + ([jax.ShapeDtypeStruct((r, c), BF16)] if also_bf16 else [])
    out = pl.pallas_call(
        body, grid=(r // tr,), in_specs=[spec] * (1 + n), out_specs=[spec] * len(shapes), out_shape=shapes,
        compiler_params=_cp("parallel"), name=name)(a, *others)
    return out if also_bf16 else out[0]


def _sum_leading(a, *, name):
    n, r, c = a.shape

    def body(a_ref, o_ref):
        acc = a_ref[0]
        for i in range(1, n):
            acc = acc + a_ref[i]
        o_ref[...] = acc

    return pl.pallas_call(body, out_shape=jax.ShapeDtypeStruct((r, c), F32), name=name)(a)


def _adamw_math(wv, gv, mv, vv):
    m2 = ADAM_B1 * mv + (1.0 - ADAM_B1) * gv
    v2 = ADAM_B2 * vv + (1.0 - ADAM_B2) * (gv * gv)
    m_hat = m2 / (1.0 - ADAM_B1 ** ADAM_STEP)
    v_hat = v2 / (1.0 - ADAM_B2 ** ADAM_STEP)
    return -ADAM_LR * (m_hat / (jnp.sqrt(v_hat) + ADAM_EPS) + ADAM_WD * wv), m2, v2


def _adamw(w, g, m, v, *, name):
    r, c = w.shape
    tr = _row_tile(r, c, align=8)

    def body(w_ref, g_ref, m_ref, v_ref, d_ref, mo_ref, vo_ref):
        d_ref[...], mo_ref[...], vo_ref[...] = _adamw_math(w_ref[...], g_ref[...], m_ref[...], v_ref[...])

    spec = pl.BlockSpec((tr, c), lambda i: (i, 0))
    shp = jax.ShapeDtypeStruct((r, c), F32)
    return pl.pallas_call(
        body, grid=(r // tr,), in_specs=[spec] * 4, out_specs=[spec] * 3, out_shape=[shp] * 3,
        compiler_params=_cp("parallel"), name=name)(w, g, m, v)


def _adamw_halves(w, g_mine, g_other, m, v, c, *, name):
    r, cols = w.shape
    h = r // 2
    tr = _row_tile(h, cols, align=8)
    nh = h // tr

    def body(c_ref, w_ref, gm_ref, go_ref, m_ref, v_ref, g_ref, d_ref, mo_ref, vo_ref):
        gv = jnp.where(pl.program_id(0) // nh == c_ref[0], gm_ref[...], go_ref[...])
        g_ref[...] = gv
        d_ref[...], mo_ref[...], vo_ref[...] = _adamw_math(w_ref[...], gv, m_ref[...], v_ref[...])

    full = pl.BlockSpec((tr, cols), lambda i, c_: (i, 0))
    half = pl.BlockSpec((tr, cols), lambda i, c_: (i % nh, 0))
    shp = jax.ShapeDtypeStruct((r, cols), F32)
    return pl.pallas_call(
        body,
        grid_spec=pltpu.PrefetchScalarGridSpec(num_scalar_prefetch=1, grid=(2 * nh,),
                                               in_specs=[full, half, half, full, full], out_specs=[full] * 4),
        out_shape=[shp] * 4, compiler_params=_cp("parallel"), name=name,
    )(jnp.reshape(c, (1,)).astype(jnp.int32), w, g_mine, g_other, m, v)


def _flip(i, d):
    return 1 - i if d else i


def _comm(name, ins, out_shapes, n_remote, n_local, plan, aliases=None):
    n_in, n_out = len(ins), len(out_shapes)

    def body(*refs):
        in_refs, out_refs = refs[:n_in], refs[n_in:n_in + n_out]
        send_sems, recv_sems = refs[n_in + n_out], refs[n_in + n_out + 1]
        x, y, c = lax.axis_index("x"), lax.axis_index("y"), lax.axis_index("c")
        remote, local = plan(in_refs, out_refs, x, y, c)
        assert len(remote) == n_remote and len(local) == n_local
        copies = []
        if n_local:
            loc_sems = refs[n_in + n_out + 2]
            copies += [pltpu.make_async_copy(s_, d_, loc_sems.at[i]) for i, (s_, d_) in enumerate(local)]
        copies += [pltpu.make_async_remote_copy(src_ref=s_, dst_ref=d_, send_sem=send_sems.at[i],
                                                recv_sem=recv_sems.at[i], device_id=dev, device_id_type=MESH)
                   for i, (s_, d_, dev) in enumerate(remote)]
        for cp in copies:
            cp.start()
        for cp in copies:
            cp.wait()

    hbm = pl.BlockSpec(memory_space=pl.ANY)
    scratch = [pltpu.SemaphoreType.DMA((n_remote,)), pltpu.SemaphoreType.DMA((n_remote,))]
    if n_local:
        scratch.append(pltpu.SemaphoreType.DMA((n_local,)))
    return pl.pallas_call(
        body, in_specs=[hbm] * n_in, out_specs=[hbm] * n_out, out_shape=out_shapes, scratch_shapes=scratch,
        input_output_aliases=aliases or {}, compiler_params=pltpu.CompilerParams(has_side_effects=True),
        name=name)(*ins)


HBM_SPEC = pl.BlockSpec(memory_space=pltpu.HBM)
SEM_SPEC = pl.BlockSpec(memory_space=pltpu.SEMAPHORE)
DATAFLOW = pltpu.SideEffectType.DATAFLOW_SIDE_EFFECTING


def _remote_copies(plan, srcs, lands, send_sems, recv_sems, n_copies):
    x, y, c = lax.axis_index("x"), lax.axis_index("y"), lax.axis_index("c")
    copies = plan(srcs, lands, x, y, c)
    assert len(copies) == n_copies
    return [pltpu.make_async_remote_copy(src_ref=s_, dst_ref=d_, send_sem=send_sems.at[i], recv_sem=recv_sems.at[i],
                                         device_id=dev, device_id_type=MESH) for i, (s_, d_, dev) in enumerate(copies)]


def _split_start(name, srcs, lands, n_copies, plan, after=None):
    ns, nb = len(srcs), len(srcs) + len(lands)
    n_after = 0 if after is None else 1
    n_in = nb + n_after

    def body(*refs):
        for cp in _remote_copies(plan, refs[:ns], refs[ns:nb], refs[n_in], refs[n_in + 1], n_copies):
            cp.start()
        refs[-1][...] = jnp.zeros_like(refs[-1])

    arrays = [pltpu.with_memory_space_constraint(a_, pltpu.HBM) for a_ in list(srcs) + list(lands)]
    out = pl.pallas_call(
        body, name=name,
        out_shape=(pltpu.SemaphoreType.DMA((n_copies,)), pltpu.SemaphoreType.DMA((n_copies,)),
                   *[pltpu.HBM(a_.shape, a_.dtype) for a_ in arrays], jax.ShapeDtypeStruct((8, 128), F32)),
        in_specs=[HBM_SPEC] * nb + [pl.BlockSpec(memory_space=pl.ANY)] * n_after,
        out_specs=(SEM_SPEC, SEM_SPEC, *[HBM_SPEC] * nb, pl.BlockSpec(memory_space=pltpu.VMEM)),
        input_output_aliases={i: 2 + i for i in range(nb)},
        compiler_params=pltpu.CompilerParams(has_side_effects=DATAFLOW))(*arrays, *([after] * n_after))
    return (out[0], out[1], list(out[2:2 + nb])), out[-1]


def _split_wait(name, handle, ns, n_copies, plan, after):
    send_sems, recv_sems, bufs = handle
    nb = len(bufs)

    def body(*refs):
        for cp in _remote_copies(plan, refs[:ns], refs[ns:nb], refs[nb], refs[nb + 1], n_copies):
            cp.wait_send()
            cp.wait_recv()

    out = pl.pallas_call(
        body, name=name, out_shape=[pltpu.HBM(b_.shape, b_.dtype) for b_ in bufs],
        in_specs=[HBM_SPEC] * nb + [SEM_SPEC, SEM_SPEC, pl.BlockSpec(memory_space=pl.ANY)],
        out_specs=[HBM_SPEC] * nb, input_output_aliases={i: i for i in range(nb)},
        compiler_params=pltpu.CompilerParams(has_side_effects=DATAFLOW))(*bufs, send_sems, recv_sems, after)
    return list(out[ns:])


def _gather_start(shards, tag, after=None):
    n = len(shards)
    lands = [lax.empty((4,) + s.shape, s.dtype) for s in shards]

    def plan(srcs, dsts, x, y, c):
        k = 2 * x + y
        copies = []
        for w_ref, o_ref in zip(srcs, dsts):
            h = w_ref.shape[0] // 2
            rows = pl.ds(c * h, h)
            copies += [(w_ref.at[rows], o_ref.at[k, rows], (_flip(x, dx), _flip(y, dy), c)) for dx, dy in CHIP_FLIPS]
        return copies

    handle, token = _split_start(f"gather_{tag}_start", shards, lands, 3 * n, plan, after)
    return (handle, plan, n), token


def _gather_wait(started, after, tag):
    handle, plan, n = started
    return _split_wait(f"gather_{tag}_wait", handle, n, 3 * n, plan, after)


def _gather_d2d(lands, before, tag):
    n = len(lands)

    def plan_d2d(in_refs, out_refs, x, y, c):
        remote = []
        for o_ref in out_refs:
            h = o_ref.shape[1] // 2
            for dx, dy in CHIP_FLIPS:
                half = o_ref.at[2 * _flip(x, dx) + _flip(y, dy), pl.ds(c * h, h)]
                remote.append((half, half, (x, y, 1 - c)))
        return remote, []

    return _comm(f"gather_{tag}_d2d", list(lands) + list(before),
                 [jax.ShapeDtypeStruct(l_.shape, l_.dtype) for l_ in lands], 3 * n, 0, plan_d2d,
                 aliases={i: i for i in range(n)})


def _pair_plan(in_refs, out_refs, x, y, c):
    return [(i_, o_, (x, y, 1 - c)) for i_, o_ in zip(in_refs, out_refs)], []


def _rs_start(grads, tag):
    n = len(grads)
    c = lax.axis_index("c")
    def rows(g, start, h):
        return g.rows(start, h) if isinstance(g, _WInGrad) else lax.dynamic_slice_in_dim(g, start, h, axis=1)

    halves = [g.shape[1] // 2 for g in grads]
    mine = [rows(g, c * h, h) for g, h in zip(grads, halves)]
    send_a = [rows(g, (1 - c) * h, h).astype(BF16) for g, h in zip(grads, halves)]
    recv_a = _comm(f"rs_pair_{tag}", send_a, [jax.ShapeDtypeStruct(s.shape, BF16) for s in send_a], n, 0, _pair_plan)
    pair, pair_b = [], []
    for i, (mi, ra) in enumerate(zip(mine, recv_a)):
        four, h, cols = mi.shape
        p32, p16 = _addn(mi.reshape(four * h, cols), [ra.reshape(four * h, cols)], name=f"rs_pair_sum_{tag}_{i}",
                         also_bf16=True)
        pair.append(p32.reshape(four, h, cols))
        pair_b.append(p16.reshape(four, h, cols))

    def plan(srcs, dsts, x, y, c_):
        copies = []
        for i_, o_ in zip(srcs, dsts):
            for j, (dx, dy) in enumerate(CHIP_FLIPS):
                fx, fy = _flip(x, dx), _flip(y, dy)
                copies.append((i_.at[2 * fx + fy], o_.at[j], (fx, fy, c_)))
        return copies

    lands = [lax.empty((3,) + p.shape[1:], BF16) for p in pair_b]
    handle, token = _split_start(f"rs_chips_{tag}_start", pair_b, lands, 3 * n, plan)
    return (handle, plan, n, pair), token


def _rs_finish(started, after, tag):
    handle, plan, n, pair = started
    recv_b = _split_wait(f"rs_chips_{tag}_wait", handle, n, 3 * n, plan, after)
    k = 2 * lax.axis_index("x") + lax.axis_index("y")
    tot = [_addn(lax.dynamic_index_in_dim(p, k, 0, keepdims=False), [rb[0], rb[1], rb[2]],
                 name=f"rs_chip_sum_{tag}_{i}") for i, (p, rb) in enumerate(zip(pair, recv_b))]
    other = _comm(f"rs_halves_{tag}", tot, [jax.ShapeDtypeStruct(t.shape, F32) for t in tot], n, 0, _pair_plan)
    return tot, other


def _gather_all(vec, *, name, before=()):
    out = jax.ShapeDtypeStruct((8,) + vec.shape, vec.dtype)

    def plan(in_refs, out_refs, x, y, c):
        me = 4 * x + 2 * y + c
        remote = [(in_refs[0], out_refs[0].at[me], (_flip(x, dx), _flip(y, dy), _flip(c, dc)))
                  for dx in (0, 1) for dy in (0, 1) for dc in (0, 1) if (dx, dy, dc) != (0, 0, 0)]
        return remote, [(in_refs[0], out_refs[0].at[me])]

    return _comm(name, [vec] + list(before), [out], 7, 1, plan)[0]


def _pack(parts):
    flat = [p.reshape(-1).astype(F32) for p in parts]
    total = sum(f.shape[0] for f in flat)
    n = -(-total // 1024) * 128
    vec = jnp.concatenate(flat + [jnp.zeros((8 * n - total,), F32)]).reshape(8, n)
    offs, o = [], 0
    for f in flat:
        offs.append((o, f.shape[0]))
        o += f.shape[0]
    return vec, offs


def _unpack(vec, offs, shapes):
    flat = vec.reshape(-1)
    return [flat[o:o + n].reshape(s) for (o, n), s in zip(offs, shapes)]


BIG = (("ffn1_w_gate_up", "col"), ("ffn1_w_down", "row"), ("w_in", "col"), ("w_out_a", "row"), ("w_out_ssm", "row"),
       ("w_mix_out", "row"), ("w_q", "row"), ("w_kv", "col"), ("w_o_x", "row"), ("ffn2_w_gate_up", "col"),
       ("ffn2_w_down", "row"))
SMALL = ("ffn1_norm", "mix_norm", "conv_a_w", "ssm_conv_w", "ssm_conv_b", "ssm_dt_bias", "ssm_a_log", "ssm_d",
         "ssm_norm", "xattn_norm", "mem_norm", "ffn2_norm", "final_norm")
WEIGHTS = ("ffn1_norm", "ffn1_w_gate_up", "ffn1_w_down", "mix_norm", "w_in", "conv_a_w", "w_out_a", "ssm_conv_w",
           "ssm_conv_b", "ssm_dt_bias", "ssm_a_log", "ssm_d", "ssm_norm", "w_out_ssm", "w_mix_out", "xattn_norm",
           "mem_norm", "w_q", "w_kv", "w_o_x", "ffn2_norm", "ffn2_w_gate_up", "ffn2_w_down", "final_norm")


GATHER_GROUPS = (("a", ("ffn1_w_gate_up",)), ("b", ("ffn1_w_down", "w_in")),
                 ("c", ("w_out_a", "w_out_ssm", "w_mix_out", "w_q", "w_kv", "w_o_x", "ffn2_w_gate_up", "ffn2_w_down")))


def _place_own(land, own, k, *, name):
    four, r, cols = land.shape
    tr = _row_tile(r, cols)

    def body(k_ref, own_ref, land_in, o_ref):
        del k_ref, land_in
        o_ref[...] = own_ref[...]

    return pl.pallas_call(
        body,
        grid_spec=pltpu.PrefetchScalarGridSpec(
            num_scalar_prefetch=1, grid=(r // tr,),
            in_specs=[pl.BlockSpec((tr, cols), lambda i, k_: (i, 0)), pl.BlockSpec(memory_space=pl.ANY)],
            out_specs=pl.BlockSpec((None, tr, cols), lambda i, k_: (k_[0], i, 0))),
        out_shape=jax.ShapeDtypeStruct(land.shape, land.dtype), input_output_aliases={2: 0},
        compiler_params=_cp("parallel"), name=name)(jnp.reshape(k, (1,)).astype(jnp.int32), own, land)


def _full_weight(land, own, kind, k, *, name):
    land = _place_own(land, own, k, name=name)
    four, r, cols = land.shape
    if kind == "row":
        return land.reshape(four * r, cols)
    return jnp.transpose(land, (1, 0, 2)).reshape(r, four * cols)


class _GatheredWeights:
    def __init__(self, shards32, k, after):
        first = GATHER_GROUPS[0][1]
        self.shards, self.k = {n: shards32[n].astype(BF16) for n in first}, k
        self.full = {}
        self.n_done = 0
        self.started, token = self._start(0, after)
        self.token = token[0, 0]
        self.shards.update({n: (w + token[0, 0]).astype(BF16) for n, w in shards32.items() if n not in first})
        self.after = jnp.stack([self.shards[n][0, 0] for n in shards32 if n not in first]).astype(F32).reshape(1, -1)

    def _start(self, gi, after):
        tag, names = GATHER_GROUPS[gi]
        return _gather_start([self.shards[n] for n in names], tag, after)

    def mark(self, value):
        self.after = value

    def __getitem__(self, name):
        if name not in self.full:
            tag, names = GATHER_GROUPS[self.n_done]
            assert name in names, (name, tag)
            lands = _gather_wait(self.started, self.after, tag)
            before = []
            if self.n_done + 1 < len(GATHER_GROUPS):
                self.started, token = self._start(self.n_done + 1, lands[0])
                before = [token]
            lands = _gather_d2d(lands, before, tag)
            for n, land in zip(names, lands):
                if n == "w_in":
                    self.full[n] = _pad_w_in_shards(_place_own(land, self.shards[n], self.k, name=f"own_{n}"))
                else:
                    self.full[n] = _full_weight(land, self.shards[n], dict(BIG)[n], self.k, name=f"own_{n}")
            self.n_done += 1
        return self.full[name]


def _shard_major(dw, kind):
    if isinstance(dw, tuple):
        return jnp.concatenate(dw, axis=0)
    if dw.ndim == 3:
        return dw
    if kind == "row":
        return dw.reshape(4, dw.shape[0] // 4, dw.shape[1])
    return jnp.transpose(dw.reshape(dw.shape[0], 4, dw.shape[1] // 4), (1, 0, 2))


def _pad_rows8(w):
    return jnp.concatenate([w, jnp.zeros((8 - w.shape[0], w.shape[1]), w.dtype)], axis=0)


def _group_lanes(v):
    r = v.shape[0]
    return jnp.pad(v.reshape(r, NG, NH // NG), ((0, 0), (0, 0), (0, 128 - NH // NG))).reshape(r, NG * 128)


def _ungroup_lanes(v):
    r = v.shape[0]
    return v.reshape(r, NG, 128)[:, :, :NH // NG].reshape(r, NH)


def _local_step(wfull, small, x, mem, target, token=0.0, on_grads=None):
    bl, s, _ = x.shape
    T = bl * s
    x2, t2 = x.reshape(T, D), target.reshape(T, D)
    mem2 = mem.reshape(-1, D)
    g = {}
    tok = [token]
    mark = getattr(wfull, "mark", lambda value: None)

    def gain(name):
        return small[name].reshape(1, -1) + tok[0]

    def emit(tag, names):
        if on_grads is not None:
            tok[0] = tok[0] + on_grads(tag, {n: g[n] for n in names})

    def ffn_fwd(h, n, wgu, wd, tag, next_gain=None):
        gate, up, a = _gate_up_fwd(n, wfull[wgu], name=f"{tag}_gate_up")
        mark(a)
        out = _mm(a, wfull[wd], "nn", tk=DFF, scale=FFN_RES, residual=h, norm_gain=next_gain, name=f"{tag}_down")
        return out, (n, gate, up, a)

    def ffn_bwd(dh, h, norm, wgu, wd, saved, tag):
        n, gate, up, a = saved
        dgate, dup = _act_bwd(dh, wfull[wd], gate, up, FFN_RES, name=f"{tag}_d_act")
        g[wd] = _mm(a, dh, "tn", tm=1408, scale=FFN_RES, name=f"{tag}_d_w_down")
        g[wgu] = (_mm(n, dgate, "tn", tn=1408, col_shards=2, name=f"{tag}_d_w_gate"),
                  _mm(n, dup, "tn", tn=1408, col_shards=2, name=f"{tag}_d_w_up"))
        emit(tag, (wgu, wd))
        dn = _mm(dgate, wfull[wgu], "nt", a2=dup, tm=512, tk=DFF, name=f"{tag}_d_norm_out")
        dh_in, g[norm] = _norm_bwd(h, gain(norm), dn, dh, name=f"{tag}_d_norm")
        return dh_in

    n1 = _norm_fwd(x2, gain("ffn1_norm"), name="ffn1_norm")
    (h1, u), ffn1_saved = ffn_fwd(x2, n1, "ffn1_w_gate_up", "ffn1_w_down", "ffn1", gain("mix_norm"))
    mark(h1)
    pp = _mm(u, wfull["w_in"], "nn", tn=1152, name="in_proj")
    wa8 = _pad_rows8(small["conv_a_w"])
    ws8 = _pad_rows8(small["ssm_conv_w"])
    conv_b = gain("ssm_conv_b")
    bias128 = jnp.pad(gain("ssm_dt_bias"), ((0, 0), (0, 128 - NH)))
    ya_pre = _conv_a_fwd(pp, wa8, bl, s, name="conv_a")
    xc = _conv_ssm_fwd(pp, ws8, conv_b, bl, s, name="conv_ssm")
    mark(xc)
    alog = gain("ssm_a_log")
    dtg, dtt = _dt_fwd(pp, bias128, name="dt")
    arow, acol = _group_lanes(alog), alog.reshape(NG, NH // NG, 1)
    dexp = jnp.repeat(gain("ssm_d"), HD, axis=1)
    ng = gain("ssm_norm")
    y, yn, prev = _ssd_fwd(xc, pp, dtg, dtt, arow, acol, dexp, ng, bl, s, name="ssd")
    ya = _mm(ya_pre, wfull["w_out_a"], "nn", tn=1024, name="out_a")
    yb = _mm(yn, wfull["w_out_ssm"], "nn", tn=1024, tk=DI, name="out_ssm")
    merged = _merge_fwd(pp, ya, yb, name="merge")
    h2, un = _mm(merged, wfull["w_mix_out"], "nn", residual=h1, norm_gain=gain("xattn_norm"), name="mix_out")
    q = _mm(un, wfull["w_q"], "nn", tn=1024, out_dtype=BF16, name="q_proj")
    mn = _norm_fwd(mem2, gain("mem_norm"), name="mem_norm")
    kv = _mm(mn, wfull["w_kv"], "nn", tn=1024, out_dtype=BF16, name="kv_proj")
    o = _attn_fwd(q, kv, bl, s, name="attn")
    h3, n2 = _mm(o, wfull["w_o_x"], "nn", residual=h2, norm_gain=gain("ffn2_norm"), name="attn_out")
    h4, ffn2_saved = ffn_fwd(h3, n2, "ffn2_w_gate_up", "ffn2_w_down", "ffn2")
    sq_err, dh4, dgf = _final_loss(h4, gain("final_norm"), t2, name="final_loss")
    g["final_norm"] = dgf

    dh3 = ffn_bwd(dh4, h3, "ffn2_norm", "ffn2_w_gate_up", "ffn2_w_down", ffn2_saved, "ffn2")
    do = _mm(dh3, wfull["w_o_x"], "nt", tn=1024, out_dtype=BF16, name="d_attn_o")
    g["w_o_x"] = _mm(o, dh3, "tn",name="d_w_o_x")
    dq, dk, dv = _attn_bwd(q, kv, do, bl, s, name="d_attn")
    dun = _mm(dq, wfull["w_q"], "nt", tn=1024, name="d_xattn_norm_out")
    g["w_q"] = _mm(un, dq, "tn",name="d_w_q")
    dkv = jnp.concatenate([dk, dv], axis=1)
    dmn = _mm(dkv, wfull["w_kv"], "nt", tn=1024, tk=2 * D, name="d_mem_norm_out")
    g["w_kv"] = _mm(mn, dkv, "tn", tn=512, col_shards=4, name="d_w_kv")
    emit("attn", ("w_q", "w_kv", "w_o_x"))
    _, g["mem_norm"] = _norm_bwd(mem2, gain("mem_norm"), dmn, None, name="d_mem_norm")
    dh2, g["xattn_norm"] = _norm_bwd(h2, gain("xattn_norm"), dun, dh3, name="d_xattn_norm")
    dmerged = _mm(dh2, wfull["w_mix_out"], "nt", tn=1024, name="d_merged")
    g["w_mix_out"] = _mm(merged, dh2, "tn",name="d_w_mix_out")
    dya, dyb, dpp = _merge_bwd(pp, ya, yb, dmerged, name="d_merge")
    dya_pre = _mm(dya, wfull["w_out_a"], "nt", tn=1024, name="d_conv_a_out")
    g["w_out_a"] = _mm(ya_pre, dya, "tn",name="d_w_out_a")
    dyn = _mm(dyb, wfull["w_out_ssm"], "nt", tn=DI, name="d_ssd_out")
    g["w_out_ssm"] = _mm(yn, dyb, "tn",name="d_w_out_ssm")
    dpp, dwa8 = _conv_a_bwd(pp, wa8, dya_pre, dpp, bl, s, name="d_conv_a")
    g["conv_a_w"] = dwa8[:3]
    dpp, dxs, dbm, dcm, ddtg, g["ssm_norm"], ddexp, dalg = _ssd_bwd(
        dyn, y, xc, pp, dtg, dtt, arow, acol, dexp, ng, prev, dpp, bl, s, name="d_ssd")
    g["ssm_d"] = ddexp.reshape(NH, HD).sum(axis=1).reshape(1, NH)
    g["ssm_a_log"] = _ungroup_lanes(dalg)
    conv_dw, conv_db = [], []
    for dpart, off, tag in ((dxs, 0, "x"), (dbm, DI, "b"), (dcm, DI + NG * NS, "c")):
        dpp, dw_, db_ = _conv_ssm_bwd(pp, ws8, conv_b, dpart, off, dpp, bl, s, name=f"d_conv_ssm_{tag}")
        conv_dw.append(dw_)
        conv_db.append(db_)
    g["ssm_conv_w"] = jnp.concatenate(conv_dw, axis=1)[:4]
    g["ssm_conv_b"] = jnp.concatenate(conv_db, axis=1)
    dpp, dbias = _dt_bwd(pp, bias128, ddtg, dpp, name="d_dt")
    g["ssm_dt_bias"] = dbias[:, :NH]
    g["w_in"] = _mm(u, dpp, "tn", tn=1152, name="d_w_in")
    emit("mix", ("w_in", "w_out_a", "w_out_ssm", "w_mix_out"))
    du = _mm(dpp, wfull["w_in"], "nt", tk=3456, name="d_mix_norm_out")
    dh1, g["mix_norm"] = _norm_bwd(h1, gain("mix_norm"), du, dh2, name="d_mix_norm")
    dx = ffn_bwd(dh1, x2, "ffn1_norm", "ffn1_w_gate_up", "ffn1_w_down", ffn1_saved, "ffn1")
    return sq_err, dx, g


W_IN_SHARD = NIN // 4


def _w_in_segments():
    segs, p = [], 0
    for t in range(D // CA_TILE):
        for which in range(3):
            segs.append((D * which + CA_TILE * t, p, CA_TILE))
            p += CA_TILE
    for s, n in ((3 * D, O_GA - 3 * D), (O_GA + NH, 2 * D), (O_GA, NH)):
        segs.append((s, p, n))
        p += n
    assert p == NIN and segs[-1][1] == O_DT and segs[-2][1] == O_GA
    return segs


def _pad_w_in_shards(land):
    pieces = []
    for s, _, n in _w_in_segments():
        while n > 0:
            kk, off = divmod(s, W_IN_SHARD)
            take = min(n, W_IN_SHARD - off)
            pieces.append(land[kk][:, off:off + take])
            s, n = s + take, n - take
    return jnp.concatenate(pieces + [jnp.zeros((land.shape[1], NPP - NIN), land.dtype)], axis=1)


class _WInGrad:
    def __init__(self, dwp):
        self.dwp = dwp
        self.shape = (4, dwp.shape[0], W_IN_SHARD)

    def rows(self, start, n):
        part = lax.dynamic_slice_in_dim(self.dwp, start, n, axis=0)
        shards = []
        for kk in range(4):
            n0, n1 = W_IN_SHARD * kk, W_IN_SHARD * (kk + 1)
            cuts = sorted((max(s, n0), p + max(s, n0) - s, min(s + m, n1) - max(s, n0))
                          for s, p, m in _w_in_segments() if min(s + m, n1) > max(s, n0))
            shards.append(jnp.concatenate([part[:, p:p + m] for _, p, m in cuts], axis=1))
        return jnp.stack(shards)


def _pad_w_in(w):
    return _pad_w_in_shards(jnp.stack(jnp.split(w, 4, axis=1)))


def _unpad_w_in(w):
    return jnp.concatenate(list(_WInGrad(w).rows(0, w.shape[0])), axis=1)


def kernel(x, mem, ffn1_norm, ffn1_w_gate_up, ffn1_w_down, mix_norm, w_in, conv_a_w, w_out_a, ssm_conv_w, ssm_conv_b, ssm_dt_bias, ssm_a_log, ssm_d, ssm_norm, w_out_ssm, w_mix_out, xattn_norm, mem_norm, w_q, w_kv, w_o_x, ffn2_norm, ffn2_w_gate_up, ffn2_w_down, final_norm, loss_target, m_ffn1_norm, m_ffn1_w_gate_up, m_ffn1_w_down, m_mix_norm, m_w_in, m_conv_a_w, m_w_out_a, m_ssm_conv_w, m_ssm_conv_b, m_ssm_dt_bias, m_ssm_a_log, m_ssm_d, m_ssm_norm, m_w_out_ssm, m_w_mix_out, m_xattn_norm, m_mem_norm, m_w_q, m_w_kv, m_w_o_x, m_ffn2_norm, m_ffn2_w_gate_up, m_ffn2_w_down, m_final_norm, v_ffn1_norm, v_ffn1_w_gate_up, v_ffn1_w_down, v_mix_norm, v_w_in, v_conv_a_w, v_w_out_a, v_ssm_conv_w, v_ssm_conv_b, v_ssm_dt_bias, v_ssm_a_log, v_ssm_d, v_ssm_norm, v_w_out_ssm, v_w_mix_out, v_xattn_norm, v_mem_norm, v_w_q, v_w_kv, v_w_o_x, v_ffn2_norm, v_ffn2_w_gate_up, v_ffn2_w_down, v_final_norm):
    a = dict(locals())
    xi, yi = lax.axis_index("x"), lax.axis_index("y")
    k = 2 * xi + yi

    conv_vec, conv_offs = _pack([a["conv_a_w"], a["ssm_conv_w"]])
    conv_all = _gather_all(conv_vec, name="gather_conv_w")
    wfull = _GatheredWeights({n: a[n][0] for n, _ in BIG}, k, conv_all)
    conv_sh = [_unpack(conv_all[2 * kk], conv_offs, [a["conv_a_w"].shape[1:], a["ssm_conv_w"].shape[1:]])
               for kk in range(4)]
    small = {n: a[n] for n in SMALL}
    small["conv_a_w"] = jnp.concatenate([cs[0] for cs in conv_sh], axis=1)
    small["ssm_conv_w"] = jnp.concatenate([cs[1] for cs in conv_sh], axis=1)

    rs_started = []

    def on_grads(tag, grads):
        names = [n for n, _ in BIG if n in grads]
        shard_major = [_WInGrad(grads[n]) if n == "w_in" else _shard_major(grads[n], dict(BIG)[n]) for n in names]
        st, tk = _rs_start(shard_major, tag)
        rs_started.append((tag, names, st))
        return tk[0, 0]

    sq_err, dx, g = _local_step(wfull, small, x, mem, loss_target, wfull.token, on_grads)
    loss = lax.psum(0.5 / D * jnp.sum(sq_err), ("x", "y", "c"))

    ci = lax.axis_index("c")
    out = {}

    def finish(tag, names, st, after):
        g_mine, g_other = _rs_finish(st, after, tag)
        for n, gm, go in zip(names, g_mine, g_other):
            res = _adamw_halves(a[n][0], gm, go, a["m_" + n][0], a["v_" + n][0], ci, name=f"adamw_{n}")
            out[n] = tuple(t.reshape(a[n].shape) for t in res)
        return res[1]

    done = dx
    for grp in rs_started[:-1]:
        done = finish(*grp, dx)

    full_shapes = [g[n].shape for n in SMALL]
    gvec, goffs = _pack([g[n] for n in SMALL])
    gsum = _sum_leading(_gather_all(gvec, name="gather_small_grads", before=[done]), name="sum_small_grads")
    finish(*rs_started[-1], gsum)
    gsmall = dict(zip(SMALL, _unpack(gsum, goffs, full_shapes)))
    for n in ("conv_a_w", "ssm_conv_w"):
        width = a[n].shape[2]
        gsmall[n] = lax.dynamic_slice_in_dim(gsmall[n], k * width, width, axis=1)
    local_shapes = [a[n].shape for n in SMALL]
    packs = [_pack([t[n] for n in SMALL]) for t in
             ({n: a[n] for n in SMALL}, gsmall, {n: a["m_" + n] for n in SMALL}, {n: a["v_" + n] for n in SMALL})]
    offs = packs[0][1]
    res = _adamw(*[p[0] for p in packs], name="adamw_small")
    unp = [_unpack(r, offs, local_shapes) for r in res]
    for i, n in enumerate(SMALL):
        out[n] = (gsmall[n].reshape(a[n].shape), unp[0][i], unp[1][i], unp[2][i])

    grad_x = dx.reshape(x.shape)
    return (loss, grad_x, *[out[n][0] for n in WEIGHTS], *[out[n][1] for n in WEIGHTS],
            *[out[n][2] for n in WEIGHTS], *[out[n][3] for n in WEIGHTS])
```

```python
import functools
import math

import jax
import jax.numpy as jnp
from jax import lax
from jax.experimental import pallas as pl
from jax.experimental.pallas import tpu as pltpu

F32 = jnp.float32
BF16 = jnp.bfloat16
MXU = jnp.bfloat16
HI = lax.Precision.HIGHEST

D = 1024
DFF = 2816
DI = 2048
NH, HD, NG, NS, CH = 32, 64, 4, 128, 128
GW = DI // NG
XH, XD = 4, 256
EPS = 1e-6
NEG = -1e30
CA_TILE = 256
O_CA, O_Z, O_XBC, O_GA, O_GB, O_DT, NPP = 0, 3072, 5120, 8192, 9216, 10240, 10368
NIN = 10272
FFN_RES = 0.5
ADAM_LR, ADAM_B1, ADAM_B2, ADAM_EPS, ADAM_WD, ADAM_STEP = 0.001, 0.9, 0.999, 1e-08, 0.01, 10
VMEM_LIMIT = 56 * 1024 * 1024
EPI_COLS = 256
SSD_EX = 2
MESH = pl.DeviceIdType.MESH
CHIP_FLIPS = ((1, 0), (0, 1), (1, 1))


def _cp(*sem):
    return pltpu.CompilerParams(dimension_semantics=sem, vmem_limit_bytes=VMEM_LIMIT)


def _tile(n, pref, align=128):
    if n <= pref:
        return n
    t = (pref // align) * align
    while t >= align:
        if n % t == 0:
            return t
        t -= align
    raise ValueError((n, pref))


def _dot(a, b, dims, prec=None):
    return lax.dot_general(a, b, (dims, ((), ())), preferred_element_type=F32, precision=prec)


def _nn(a, b, prec=None):
    return _dot(a, b, ((1,), (0,)), prec)


def _nt(a, b):
    return _dot(a, b, ((1,), (1,)))


def _tn(a, b):
    return _dot(a, b, ((0,), (0,)))


def _sig(x):
    return jax.nn.sigmoid(x)


def _mm(a, b, mode, *, name, tm=1024, tn=1024, tk=None, out_dtype=F32, scale=None, residual=None, a2=None,
        col_shards=0, norm_gain=None):
    if tk is None:
        tk = 2048 if mode == "tn" else 1024
    if mode == "nn":
        (M, K), (K2, N) = a.shape, b.shape
    elif mode == "nt":
        (M, K), (N, K2) = a.shape, b.shape
        if a2 is not None:
            assert a2.shape == a.shape
            K2 = K2 // 2
    else:
        (K, M), (K2, N) = a.shape, b.shape
    assert K == K2, (name, a.shape, b.shape)
    tm, tn, tk = _tile(M, tm), _tile(N, tn), _tile(K, tk)
    nk = K // tk
    if mode == "nn":
        a_spec = pl.BlockSpec((tm, tk), lambda i, j, k: (i, k))
        b_spec = pl.BlockSpec((tk, tn), lambda i, j, k: (k, j))
        dims = ((1,), (0,))
    elif mode == "nt":
        a_spec = pl.BlockSpec((tm, tk), lambda i, j, k: (i, k))
        b_spec = pl.BlockSpec((tn, tk), lambda i, j, k: (j, k))
        dims = ((1,), (1,))
    else:
        a_spec = pl.BlockSpec((tk, tm), lambda i, j, k: (k, i))
        b_spec = pl.BlockSpec((tk, tn), lambda i, j, k: (k, j))
        dims = ((0,), (0,))
    o_spec = pl.BlockSpec((tm, tn), lambda i, j, k: (i, j))
    out_spec, out_shape = o_spec, jax.ShapeDtypeStruct((M, N), out_dtype)
    if col_shards:
        per = N // col_shards // tn
        assert per * tn * col_shards == N, (name, N, tn, col_shards)
        out_spec = pl.BlockSpec((None, tm, tn), lambda i, j, k: (j // per, i, j % per))
        out_shape = jax.ShapeDtypeStruct((col_shards, M, N // col_shards), out_dtype)
    has_res = residual is not None
    has_norm = norm_gain is not None
    assert not has_norm or (tn == N and not col_shards)
    dual = a2 is not None
    n_in = 2 + 2 * dual + has_res + has_norm

    def body(*refs):
        a_ref, b_ref = refs[0], refs[1]
        o_ref = refs[n_in]

        def finish(acc):
            if scale is not None:
                acc = acc * scale
            if has_res:
                acc = acc + refs[2 + 2 * dual][...]
            o_ref[...] = acc.astype(out_dtype)
            if has_norm:
                rs = lax.rsqrt(jnp.mean(acc * acc, axis=-1, keepdims=True) + EPS)
                refs[n_in + 1][...] = (acc * rs * refs[n_in - 1][...]).astype(BF16)

        part = _dot(a_ref[...].astype(MXU), b_ref[...].astype(MXU), dims)
        if dual:
            part = part + _dot(refs[2][...].astype(MXU), refs[3][...].astype(MXU), dims)
        if nk == 1:
            finish(part)
            return
        acc_ref = refs[-1]
        k = pl.program_id(2)

        @pl.when(k == 0)
        def _():
            acc_ref[...] = part

        @pl.when(k > 0)
        def _():
            acc_ref[...] += part

        @pl.when(k == nk - 1)
        def _():
            finish(acc_ref[...])

    ins, in_specs = [a, b], [a_spec, b_spec]
    if dual:
        ins += [a2, b]
        in_specs += [a_spec, pl.BlockSpec((tn, tk), lambda i, j, k: (j, k + nk))]
    if has_res:
        ins.append(residual)
        in_specs.append(o_spec)
    if has_norm:
        ins.append(norm_gain)
        in_specs.append(pl.BlockSpec((1, tn), lambda i, j, k: (0, j)))
        out_spec, out_shape = [out_spec, o_spec], [out_shape, jax.ShapeDtypeStruct((M, N), BF16)]
    return pl.pallas_call(
        body, grid=(M // tm, N // tn, nk), in_specs=in_specs, out_specs=out_spec, out_shape=out_shape,
        scratch_shapes=[pltpu.VMEM((tm, tn), F32)] if nk > 1 else [],
        compiler_params=_cp("parallel", "parallel", "arbitrary"), name=name)(*ins)


def _norm_fwd(x, g, *, name):
    T, d = x.shape
    tr = _tile(T, 512, 8)

    def body(x_ref, g_ref, o_ref):
        xv = x_ref[...]
        r = lax.rsqrt(jnp.mean(xv * xv, axis=-1, keepdims=True) + EPS)
        o_ref[...] = (xv * r * g_ref[...]).astype(BF16)

    return pl.pallas_call(
        body, grid=(T // tr,),
        in_specs=[pl.BlockSpec((tr, d), lambda i: (i, 0)), pl.BlockSpec((1, d), lambda i: (0, 0))],
        out_specs=pl.BlockSpec((tr, d), lambda i: (i, 0)),
        out_shape=jax.ShapeDtypeStruct((T, d), BF16), compiler_params=_cp("parallel"), name=name)(x, g)


def _norm_bwd(x, g, dn, dres, *, name):
    T, d = x.shape
    tr = _tile(T, 512, 8)
    has_res = dres is not None

    def body(*refs):
        x_ref, g_ref, dn_ref = refs[:3]
        dr_ref = refs[3] if has_res else None
        dx_ref, dg_ref = refs[-2], refs[-1]

        @pl.when(pl.program_id(0) == 0)
        def _():
            dg_ref[...] = jnp.zeros_like(dg_ref)

        xv = x_ref[...]
        dnv = dn_ref[...].astype(F32)
        r = lax.rsqrt(jnp.mean(xv * xv, axis=-1, keepdims=True) + EPS)
        xh = xv * r
        dg_ref[...] += jnp.sum(dnv * xh, axis=0, keepdims=True)
        dxh = dnv * g_ref[...]
        dx = r * (dxh - xh * jnp.mean(dxh * xh, axis=-1, keepdims=True))
        if has_res:
            dx = dx + dr_ref[...]
        dx_ref[...] = dx

    row = pl.BlockSpec((tr, d), lambda i: (i, 0))
    vec = pl.BlockSpec((1, d), lambda i: (0, 0))
    ins = [x, g, dn] + ([dres] if has_res else [])
    return pl.pallas_call(
        body, grid=(T // tr,), in_specs=[row, vec, row] + ([row] if has_res else []),
        out_specs=[row, vec],
        out_shape=[jax.ShapeDtypeStruct((T, d), F32), jax.ShapeDtypeStruct((1, d), F32)],
        compiler_params=_cp("arbitrary"), name=name)(*ins)


def _final_loss(h, g, target, *, name):
    T, d = h.shape
    tr = _tile(T, 512, 8)

    def body(h_ref, g_ref, t_ref, l_ref, dh_ref, dg_ref):
        @pl.when(pl.program_id(0) == 0)
        def _():
            l_ref[...] = jnp.zeros_like(l_ref)
            dg_ref[...] = jnp.zeros_like(dg_ref)

        xv = h_ref[...]
        r = lax.rsqrt(jnp.mean(xv * xv, axis=-1, keepdims=True) + EPS)
        xh = xv * r
        e = xh * g_ref[...] - t_ref[...]
        l_ref[...] += jnp.sum(e * e, axis=0, keepdims=True)
        dy = e * (1.0 / d)
        dg_ref[...] += jnp.sum(dy * xh, axis=0, keepdims=True)
        dxh = dy * g_ref[...]
        dh_ref[...] = r * (dxh - xh * jnp.mean(dxh * xh, axis=-1, keepdims=True))

    row = pl.BlockSpec((tr, d), lambda i: (i, 0))
    vec = pl.BlockSpec((1, d), lambda i: (0, 0))
    return pl.pallas_call(
        body, grid=(T // tr,), in_specs=[row, vec, row], out_specs=[vec, row, vec],
        out_shape=[jax.ShapeDtypeStruct((1, d), F32), jax.ShapeDtypeStruct((T, d), F32),
                   jax.ShapeDtypeStruct((1, d), F32)],
        compiler_params=_cp("arbitrary"), name=name)(h, g, target)


def _gate_up_fwd(n, wgu, *, name):
    T, d = n.shape
    f = wgu.shape[1] // 2
    tm, tn = _tile(T, 512, 8), _tile(f, DFF)
    nf = f // tn

    tc = _tile(tn, EPI_COLS)

    def body(n_ref, wg_ref, wu_ref, g_ref, u_ref, a_ref):
        nv = n_ref[...].astype(MXU)
        for j in range(tn // tc):
            sl = slice(j * tc, (j + 1) * tc)
            gv = _nn(nv, wg_ref[:, sl].astype(MXU))
            uv = _nn(nv, wu_ref[:, sl].astype(MXU))
            g_ref[:, sl] = gv.astype(BF16)
            u_ref[:, sl] = uv.astype(BF16)
            a_ref[:, sl] = (gv * _sig(gv) * uv).astype(BF16)

    out = pl.BlockSpec((tm, tn), lambda i, j: (i, j))
    act = jax.ShapeDtypeStruct((T, f), BF16)
    return pl.pallas_call(
        body, grid=(T // tm, nf),
        in_specs=[pl.BlockSpec((tm, d), lambda i, j: (i, 0)), pl.BlockSpec((d, tn), lambda i, j: (0, j)),
                  pl.BlockSpec((d, tn), lambda i, j: (0, j + nf))],
        out_specs=[out, out, out], out_shape=[act, act, act], compiler_params=_cp("parallel", "parallel"),
        name=name)(n, wgu, wgu)


def _act_bwd(dh, wd, gate, up, scale, *, name):
    T, d = dh.shape
    f = wd.shape[0]
    tm, tn = _tile(T, 512, 8), _tile(f, DFF)

    tc = _tile(tn, EPI_COLS)

    def body(dh_ref, wd_ref, g_ref, u_ref, dg_ref, du_ref):
        dhv = dh_ref[...].astype(MXU)
        for j in range(tn // tc):
            sl = slice(j * tc, (j + 1) * tc)
            da = scale * _nt(dhv, wd_ref[sl, :].astype(MXU))
            gv, uv = g_ref[:, sl].astype(F32), u_ref[:, sl].astype(F32)
            s = _sig(gv)
            dg_ref[:, sl] = (da * uv * (s * (1.0 + gv * (1.0 - s)))).astype(BF16)
            du_ref[:, sl] = (da * (gv * s)).astype(BF16)

    tile = pl.BlockSpec((tm, tn), lambda i, j: (i, j))
    act = jax.ShapeDtypeStruct((T, f), BF16)
    return pl.pallas_call(
        body, grid=(T // tm, f // tn),
        in_specs=[pl.BlockSpec((tm, d), lambda i, j: (i, 0)), pl.BlockSpec((tn, d), lambda i, j: (j, 0)), tile, tile],
        out_specs=[tile, tile], out_shape=[act, act], compiler_params=_cp("parallel", "parallel"),
        name=name)(dh, wd, gate, up)


CONV_ROWS = 64
CONV_PAD = 8


def _rows_down(ref, r0, d, cols=slice(None)):
    if r0 - d >= 0:
        return ref[pl.ds(r0 - d, CONV_ROWS), cols]
    assert r0 == 0
    v = ref[pl.ds(0, CONV_ROWS), cols]
    ri = lax.broadcasted_iota(jnp.int32, v.shape, 0)
    return jnp.where(ri >= d, pltpu.roll(v, d, 0), 0.0)


def _fold8(v):
    return jnp.sum(v.reshape(CONV_ROWS // 8, 8, v.shape[1]), axis=0)


def _taps(w_ref, views):
    acc = None
    for k, v in enumerate(views):
        t = w_ref[k:k + 1, :] * v
        acc = t if acc is None else acc + t
    return acc


def _conv_a_fwd(pp, w8, bl, s, *, name):
    tc = CA_TILE
    nb = D // tc
    bcol, ccol, vcol = slice(0, tc), slice(tc, 2 * tc), slice(2 * tc, 3 * tc)

    def body(p_ref, w_ref, o_ref):
        for r0 in range(0, s, CONV_ROWS):
            cv = [_rows_down(p_ref, r0, 2 - k, ccol) * _rows_down(p_ref, r0, 2 - k, vcol) for k in range(3)]
            o_ref[pl.ds(r0, CONV_ROWS), :] = (p_ref[pl.ds(r0, CONV_ROWS), bcol] * _taps(w_ref, cv)).astype(BF16)

    return pl.pallas_call(
        body, grid=(bl, nb),
        in_specs=[pl.BlockSpec((s, 3 * tc), lambda b, j: (b, j)), pl.BlockSpec((8, tc), lambda b, j: (0, j))],
        out_specs=pl.BlockSpec((s, tc), lambda b, j: (b, j)),
        out_shape=jax.ShapeDtypeStruct((bl * s, D), BF16), compiler_params=_cp("parallel", "parallel"),
        name=name)(pp, w8)


def _conv_a_bwd(pp, w8, dya, dpp, bl, s, *, name):
    tc = CA_TILE
    nb = D // tc
    bcol, ccol, vcol = slice(0, tc), slice(tc, 2 * tc), slice(2 * tc, 3 * tc)

    def body(p_ref, w_ref, dy_ref, dpp_in, d_ref, dw_ref, dcp):
        del dpp_in

        @pl.when(pl.program_id(1) == 0)
        def _():
            dw_ref[...] = jnp.zeros_like(dw_ref)

        dcp[pl.ds(s, CONV_PAD), :] = jnp.zeros((CONV_PAD, tc), F32)
        dw_acc = [jnp.zeros((8, tc), F32) for _ in range(3)]
        for r0 in reversed(range(0, s, CONV_ROWS)):
            rows = pl.ds(r0, CONV_ROWS)
            cs = [_rows_down(p_ref, r0, 2 - k, ccol) for k in range(3)]
            vs = [_rows_down(p_ref, r0, 2 - k, vcol) for k in range(3)]
            cv = [c_ * v_ for c_, v_ in zip(cs, vs)]
            dy = dy_ref[rows, :]
            d_ref[rows, bcol] = (dy * _taps(w_ref, cv)).astype(BF16)
            dconv = dy * p_ref[rows, bcol]
            dcp[rows, :] = dconv
            dcv = _taps(w_ref, [dcp[pl.ds(r0 + 2, CONV_ROWS), :], dcp[pl.ds(r0 + 1, CONV_ROWS), :], dconv])
            d_ref[rows, ccol] = (dcv * vs[2]).astype(BF16)
            d_ref[rows, vcol] = (dcv * cs[2]).astype(BF16)
            dw_acc = [acc + _fold8(dconv * cv_) for acc, cv_ in zip(dw_acc, cv)]
        for k in range(3):
            dw_ref[k:k + 1, :] += jnp.sum(dw_acc[k], axis=0, keepdims=True)

    wspec = pl.BlockSpec((8, tc), lambda j, b: (0, j))
    wide = pl.BlockSpec((s, 3 * tc), lambda j, b: (b, j))
    return pl.pallas_call(
        body, grid=(nb, bl),
        in_specs=[wide, wspec, pl.BlockSpec((s, tc), lambda j, b: (b, j)), pl.BlockSpec(memory_space=pl.ANY)],
        out_specs=[wide, wspec], out_shape=[jax.ShapeDtypeStruct(dpp.shape, dpp.dtype), jax.ShapeDtypeStruct((8, D), F32)],
        scratch_shapes=[pltpu.VMEM((s + CONV_PAD, tc), F32)], input_output_aliases={3: 0},
        compiler_params=_cp("parallel", "arbitrary"), name=name)(pp, w8, dya, dpp)


def _conv_ssm_fwd(pp, w8, bias, bl, s, *, name):
    tc = 256
    width = DI + 2 * NG * NS
    nb = width // tc

    def body(x_ref, w_ref, b_ref, o_ref):
        for r0 in range(0, s, CONV_ROWS):
            pre = _taps(w_ref, [_rows_down(x_ref, r0, 3 - k) for k in range(4)]) + b_ref[...]
            o_ref[pl.ds(r0, CONV_ROWS), :] = pre * _sig(pre)

    return pl.pallas_call(
        body, grid=(bl, nb),
        in_specs=[pl.BlockSpec((s, tc), lambda b, j: (b, O_XBC // tc + j)),
                  pl.BlockSpec((8, tc), lambda b, j: (0, j)), pl.BlockSpec((1, tc), lambda b, j: (0, j))],
        out_specs=pl.BlockSpec((s, tc), lambda b, j: (b, j)),
        out_shape=jax.ShapeDtypeStruct((bl * s, width), F32), compiler_params=_cp("parallel", "parallel"),
        name=name)(pp, w8, bias)


def _conv_ssm_bwd(pp, w8, bias, dxc, ch_off, dpp, bl, s, *, name):
    n = dxc.shape[1]
    tc = 256
    nb = n // tc
    o0 = ch_off // tc

    def body(x_ref, w_ref, b_ref, d_ref, dpp_in, dx_ref, dw_ref, db_ref, dp):
        del dpp_in

        @pl.when(pl.program_id(1) == 0)
        def _():
            dw_ref[...] = jnp.zeros_like(dw_ref)
            db_ref[...] = jnp.zeros_like(db_ref)

        dp[pl.ds(s, CONV_PAD), :] = jnp.zeros((CONV_PAD, tc), F32)
        dw_acc = [jnp.zeros((8, tc), F32) for _ in range(4)]
        db_acc = jnp.zeros((8, tc), F32)
        for r0 in reversed(range(0, s, CONV_ROWS)):
            rows = pl.ds(r0, CONV_ROWS)
            xs = [_rows_down(x_ref, r0, 3 - k) for k in range(4)]
            pre = _taps(w_ref, xs) + b_ref[...]
            sg = _sig(pre)
            dpre = d_ref[rows, :] * (sg * (1.0 + pre * (1.0 - sg)))
            dp[rows, :] = dpre
            dx = _taps(w_ref, [dp[pl.ds(r0 + 3 - k, CONV_ROWS), :] for k in range(3)] + [dpre])
            dx_ref[rows, :] = dx.astype(BF16)
            db_acc = db_acc + _fold8(dpre)
            dw_acc = [acc + _fold8(dpre * x_) for acc, x_ in zip(dw_acc, xs)]
        db_ref[...] += jnp.sum(db_acc, axis=0, keepdims=True)
        for k in range(4):
            dw_ref[k:k + 1, :] += jnp.sum(dw_acc[k], axis=0, keepdims=True)

    return pl.pallas_call(
        body, grid=(nb, bl),
        in_specs=[pl.BlockSpec((s, tc), lambda j, b: (b, O_XBC // tc + o0 + j)),
                  pl.BlockSpec((8, tc), lambda j, b: (0, o0 + j)), pl.BlockSpec((1, tc), lambda j, b: (0, o0 + j)),
                  pl.BlockSpec((s, tc), lambda j, b: (b, j)), pl.BlockSpec(memory_space=pl.ANY)],
        out_specs=[pl.BlockSpec((s, tc), lambda j, b: (b, O_XBC // tc + o0 + j)),
                   pl.BlockSpec((8, tc), lambda j, b: (0, j)), pl.BlockSpec((1, tc), lambda j, b: (0, j))],
        out_shape=[jax.ShapeDtypeStruct(dpp.shape, dpp.dtype), jax.ShapeDtypeStruct((8, n), F32),
                   jax.ShapeDtypeStruct((1, n), F32)],
        scratch_shapes=[pltpu.VMEM((s + CONV_PAD, tc), F32)], input_output_aliases={4: 0},
        compiler_params=_cp("parallel", "arbitrary"), name=name)(pp, w8, bias, dxc, dpp)


def _softplus(x):
    return jnp.maximum(x, 0.0) + jnp.log1p(jnp.exp(-jnp.abs(x)))


def _head_group_matrix():
    h = jnp.arange(128)[:, None]
    j = jnp.arange(NG * 128)[None, :]
    per = NH // NG
    return ((h < NH) & (j == (h // per) * 128 + h % per)).astype(F32)


def _dt_fwd(pp, bias128, *, name):
    T = pp.shape[0]
    tr = _tile(T, 1024, 8)
    per = NH // NG

    def body(x_ref, b_ref, p_ref, g_ref, t_ref):
        lane = lax.broadcasted_iota(jnp.int32, (tr, 128), 1)
        dt = jnp.where(lane < NH, _softplus(x_ref[...] + b_ref[...]), 0.0)
        g_ref[...] = _nn(dt, p_ref[...], HI)
        eye = (lax.broadcasted_iota(jnp.int32, (NH, 128), 0)
               == lax.broadcasted_iota(jnp.int32, (NH, 128), 1)).astype(F32)
        t_ref[...] = _dot(eye, dt, ((1,), (1,)), HI).reshape(NG, per, tr)

    vec = pl.BlockSpec((1, 128), lambda i: (0, 0))
    return pl.pallas_call(
        body, grid=(T // tr,),
        in_specs=[pl.BlockSpec((tr, 128), lambda i: (i, O_DT // 128)), vec,
                  pl.BlockSpec((128, NG * 128), lambda i: (0, 0))],
        out_specs=[pl.BlockSpec((tr, NG * 128), lambda i: (i, 0)), pl.BlockSpec((NG, per, tr), lambda i: (0, 0, i))],
        out_shape=[jax.ShapeDtypeStruct((T, NG * 128), F32), jax.ShapeDtypeStruct((NG, per, T), F32)],
        compiler_params=_cp("parallel"), name=name)(pp, bias128, _head_group_matrix())


def _dt_bwd(pp, bias128, ddtg, dpp, *, name):
    T = pp.shape[0]
    tr = _tile(T, 1024, 8)

    def body(x_ref, b_ref, d_ref, p_ref, dpp_in, o_ref, db_ref):
        del dpp_in

        @pl.when(pl.program_id(0) == 0)
        def _():
            db_ref[...] = jnp.zeros_like(db_ref)

        lane = lax.broadcasted_iota(jnp.int32, (tr, 128), 1)
        ddt = _dot(d_ref[...], p_ref[...], ((1,), (1,)), HI)
        dr = jnp.where(lane < NH, ddt * _sig(x_ref[...] + b_ref[...]), 0.0)
        db_ref[...] += jnp.sum(dr, axis=0, keepdims=True)
        o_ref[...] = dr.astype(BF16)

    col = pl.BlockSpec((tr, 128), lambda i: (i, O_DT // 128))
    vec = pl.BlockSpec((1, 128), lambda i: (0, 0))
    return pl.pallas_call(
        body, grid=(T // tr,),
        in_specs=[col, vec, pl.BlockSpec((tr, NG * 128), lambda i: (i, 0)), pl.BlockSpec((128, NG * 128), lambda i: (0, 0)),
                  pl.BlockSpec(memory_space=pl.ANY)],
        out_specs=[col, vec],
        out_shape=[jax.ShapeDtypeStruct(dpp.shape, dpp.dtype), jax.ShapeDtypeStruct((1, 128), F32)],
        input_output_aliases={4: 0}, compiler_params=_cp("arbitrary"),
        name=name)(pp, bias128, ddtg, _head_group_matrix(), dpp)


def _tril():
    return lax.broadcasted_iota(jnp.int32, (CH, CH), 0) >= lax.broadcasted_iota(jnp.int32, (CH, CH), 1)


def _ssd_common(dt, dtt, arow, acol):
    ri = lax.broadcasted_iota(jnp.int32, (CH, CH), 0)
    ci = lax.broadcasted_iota(jnp.int32, (CH, CH), 1)
    tril = ri >= ci
    triu = ri <= ci
    acs_col = _nn(tril.astype(F32), dt * arow, HI)
    acs_row = _nn(dtt * acol, triu.astype(F32), HI)
    return tril, triu, acs_col, acs_row


def _pair_terms(q, dt, acs_col, acs_row, tril, lo):
    ha, hb = 2 * q, 2 * q + 1
    col_a, col_b = acs_col[:, ha:ha + 1], acs_col[:, hb:hb + 1]
    row_a, row_b = acs_row[ha:ha + 1, :], acs_row[hb:hb + 1, :]
    last_a, last_b = acs_col[CH - 1:CH, ha:ha + 1], acs_col[CH - 1:CH, hb:hb + 1]
    out = dict(
        dtsel=jnp.where(lo, dt[:, ha:ha + 1], dt[:, hb:hb + 1]),
        d_a=jnp.exp(jnp.where(tril, col_a - row_a, NEG)), d_b=jnp.exp(jnp.where(tril, col_b - row_b, NEG)),
        esel=jnp.where(lo, jnp.exp(col_a), jnp.exp(col_b)),
        fsel=jnp.where(lo, jnp.exp(last_a - col_a), jnp.exp(last_b - col_b)),
        g_a=jnp.exp(last_a), g_b=jnp.exp(last_b))
    return out


def _ssd_fwd(xc, pp, dtg, dtt, arow, acol, dexp, ng, bl, s, *, name):
    nc = s // CH
    T = bl * s

    ex = SSD_EX if bl % SSD_EX == 0 else 1

    def body(*refs):
        arow_ref, acol_ref, dexp_ref, ng_ref = refs[6 * ex:6 * ex + 4]
        st_ref = refs[-1]

        @pl.when(pl.program_id(2) == 0)
        def _():
            st_ref[...] = jnp.zeros_like(st_ref)

        y_ref, yn_ref, prev_ref = refs[6 * ex + 4:6 * ex + 7]
        for e in range(ex):
            one(*refs[6 * e:6 * e + 6], arow_ref, acol_ref, dexp_ref, ng_ref,
                y_ref.at[e], yn_ref.at[e], prev_ref.at[e], st_ref.at[e])

    def one(xs_ref, bm_ref, cm_ref, z_ref, dt_ref, dtt_ref, arow_ref, acol_ref, dexp_ref, ng_ref,
            y_ref, yn_ref, prev_ref, st_ref):
        dt = dt_ref[...]
        tril, _, acs_col, acs_row = _ssd_common(dt, dtt_ref[...], -jnp.exp(arow_ref[...]), -jnp.exp(acol_ref[...]))
        bm, cm = bm_ref[...].astype(MXU), cm_ref[...].astype(MXU)
        cb = _nt(cm, bm)
        lo = lax.broadcasted_iota(jnp.int32, (CH, 128), 1) < HD
        sub_lo = lax.broadcasted_iota(jnp.int32, (128, NS), 0) < HD
        for q in range(4):
            t = _pair_terms(q, dt, acs_col, acs_row, tril, lo)
            x = xs_ref[:, 128 * q:128 * (q + 1)]
            xd = x * t["dtsel"]
            y = (_nn((cb * t["d_a"]).astype(MXU), jnp.where(lo, xd, 0.0).astype(MXU))
                 + _nn((cb * t["d_b"]).astype(MXU), jnp.where(lo, 0.0, xd).astype(MXU)))
            prev = st_ref[q]
            prev_ref[q] = prev
            y = y + t["esel"] * _nt(cm, prev.astype(MXU))
            st_ref[q] = prev * jnp.where(sub_lo, t["g_a"], t["g_b"]) + _tn((xd * t["fsel"]).astype(MXU), bm)
            y_ref[:, 128 * q:128 * (q + 1)] = y + dexp_ref[:, 128 * q:128 * (q + 1)] * x
        zv = z_ref[...]
        yg = y_ref[...] * (zv * _sig(zv))
        r = lax.rsqrt(jnp.mean(yg * yg, axis=-1, keepdims=True) + EPS)
        yn_ref[...] = (yg * r * ng_ref[...]).astype(BF16)

    def row(e, width, off_blocks):
        return pl.BlockSpec((CH, width), lambda g, b, c: ((b * ex + e) * nc + c, off_blocks + g))

    per_ex_in = [[row(e, GW, 0), row(e, NS, DI // NS), row(e, NS, DI // NS + NG), row(e, GW, O_Z // GW), row(e, 128, 0),
                  pl.BlockSpec((None, 8, CH), lambda g, b, c, e=e: (g, 0, (b * ex + e) * nc + c))] for e in range(ex)]
    by_example = pl.BlockSpec((ex, CH, GW), lambda g, b, c: (b, c, g))
    y, yn, prev = pl.pallas_call(
        body, grid=(NG, bl // ex, nc),
        in_specs=sum(per_ex_in, []) + [pl.BlockSpec((1, 128), lambda g, b, c: (0, g)),
                                       pl.BlockSpec((None, 8, 1), lambda g, b, c: (g, 0, 0)),
                                       pl.BlockSpec((1, GW), lambda g, b, c: (0, g)),
                                       pl.BlockSpec((1, GW), lambda g, b, c: (0, g))],
        out_specs=[by_example, by_example,
                   pl.BlockSpec((ex, None, 4, 128, NS), lambda g, b, c: (b, c, g, 0, 0))],
        out_shape=[jax.ShapeDtypeStruct((bl, s, DI), F32), jax.ShapeDtypeStruct((bl, s, DI), BF16),
                   jax.ShapeDtypeStruct((bl, nc, 16, 128, NS), F32)],
        scratch_shapes=[pltpu.VMEM((ex, 4, 128, NS), F32)],
        compiler_params=_cp("parallel", "parallel", "arbitrary"), name=name,
    )(*([xc, xc, xc, pp, dtg, dtt] * ex), arow, acol, dexp, ng)
    return y.reshape(T, DI), yn.reshape(T, DI), prev.reshape(bl * nc, 16, 128, NS)


def _ssd_bwd(dyn, y, xc, pp, dtg, dtt, arow, acol, dexp, ng, prev, dpp, bl, s, *, name):
    nc = s // CH
    T = bl * s

    def rsum(v):
        return jnp.sum(v, axis=1, keepdims=True)

    def asum(v):
        return jnp.sum(jnp.sum(v, axis=0, keepdims=True), axis=1, keepdims=True)

    def body(dyn_ref, y_ref, xs_ref, bm_ref, cm_ref, z_ref, dt_ref, dtt_ref, arow_ref, acol_ref, dexp_ref, ng_ref,
             prev_ref, dpp_in, dz_ref, dxs_ref, db_ref, dc_ref, ddt_ref, dng_ref, dd_ref, dal_ref, dst_ref):
        del dpp_in

        @pl.when((pl.program_id(1) == 0) & (pl.program_id(2) == 0))
        def _():
            dng_ref[...] = jnp.zeros_like(dng_ref)
            dd_ref[...] = jnp.zeros_like(dd_ref)
            dal_ref[...] = jnp.zeros_like(dal_ref)

        @pl.when(pl.program_id(2) == 0)
        def _():
            dst_ref[...] = jnp.zeros_like(dst_ref)

        yv, zv, xsv, dexp_v = y_ref[...], z_ref[...], xs_ref[...], dexp_ref[...]
        sz = _sig(zv)
        silu = zv * sz
        yg = yv * silu
        r = lax.rsqrt(jnp.mean(yg * yg, axis=-1, keepdims=True) + EPS)
        yh = yg * r
        dynv = dyn_ref[...]
        dng_ref[...] += jnp.sum(dynv * yh, axis=0, keepdims=True)
        dyh = dynv * ng_ref[...]
        dyg = r * (dyh - yh * jnp.mean(dyh * yh, axis=-1, keepdims=True))
        dz_ref[...] = (dyg * yv * (sz * (1.0 + zv * (1.0 - sz)))).astype(BF16)
        dy_all = dyg * silu
        dd_ref[...] += jnp.sum(dy_all * xsv, axis=0, keepdims=True)

        dt = dt_ref[...]
        arow_v = -jnp.exp(arow_ref[...])
        tril, triu, acs_col, acs_row = _ssd_common(dt, dtt_ref[...], arow_v, -jnp.exp(acol_ref[...]))
        bm, cm = bm_ref[...].astype(MXU), cm_ref[...].astype(MXU)
        cb = _nt(cm, bm)
        lane = lax.broadcasted_iota(jnp.int32, (CH, 128), 1)
        is_last = lax.broadcasted_iota(jnp.int32, (CH, 128), 0) == CH - 1
        lo = lane < HD
        sub_lo = lax.broadcasted_iota(jnp.int32, (128, NS), 0) < HD
        dcb = jnp.zeros((CH, CH), F32)
        dc_acc = jnp.zeros((CH, NS), F32)
        db_acc = jnp.zeros((CH, NS), F32)
        dacs = jnp.zeros((CH, 128), F32)
        ddtx = jnp.zeros((CH, 128), F32)
        csum = jnp.zeros((8, CH), F32)
        sub8 = lax.broadcasted_iota(jnp.int32, (8, CH), 0)
        for q in range(4):
            ha, hb = 2 * q, 2 * q + 1
            sl = slice(128 * q, 128 * (q + 1))
            t = _pair_terms(q, dt, acs_col, acs_row, tril, lo)
            x, dy = xsv[:, sl], dy_all[:, sl]
            xd = x * t["dtsel"]
            xd_m = xd.astype(MXU)
            dy_lo, dy_hi = jnp.where(lo, dy, 0.0).astype(MXU), jnp.where(lo, 0.0, dy).astype(MXU)
            m_a, m_b = cb * t["d_a"], cb * t["d_b"]
            prev_m = prev_ref[q].astype(MXU)
            dnext = dst_ref[q]
            dnext_m = dnext.astype(MXU)
            bds = _nt(bm, dnext_m)
            dxd = _tn(m_a.astype(MXU), dy_lo) + _tn(m_b.astype(MXU), dy_hi) + t["fsel"] * bds
            dye_m = (dy * t["esel"]).astype(MXU)
            dst_ref[q] = dnext * jnp.where(sub_lo, t["g_a"], t["g_b"]) + _tn(dye_m, cm)
            dm_a, dm_b = _nt(dy_lo, xd_m), _nt(dy_hi, xd_m)
            dcb = dcb + dm_a * t["d_a"] + dm_b * t["d_b"]
            g_a, g_b = dm_a * m_a, dm_b * m_b
            csum = (csum + jnp.where(sub8 == ha, jnp.sum(g_a, axis=0, keepdims=True), 0.0)
                    + jnp.where(sub8 == hb, jnp.sum(g_b, axis=0, keepdims=True), 0.0))
            tf = t["fsel"] * xd * bds
            tyf = dy * (t["esel"] * _nt(cm, prev_m)) - tf
            dpp = dnext * prev_ref[q]
            ea = asum(jnp.where(lo, tf, 0.0)) + t["g_a"] * asum(jnp.where(sub_lo, dpp, 0.0))
            eb = asum(jnp.where(lo, 0.0, tf)) + t["g_b"] * asum(jnp.where(sub_lo, 0.0, dpp))
            ra = rsum(g_a + jnp.where(lo, tyf, 0.0)) + jnp.where(is_last, ea, 0.0)
            rb = rsum(g_b + jnp.where(lo, 0.0, tyf)) + jnp.where(is_last, eb, 0.0)
            dacs = dacs + jnp.where(lane == ha, ra, 0.0) + jnp.where(lane == hb, rb, 0.0)
            tx = dxd * x
            ddtx = (ddtx + jnp.where(lane == ha, rsum(jnp.where(lo, tx, 0.0)), 0.0)
                    + jnp.where(lane == hb, rsum(jnp.where(lo, 0.0, tx)), 0.0))
            dxs_ref[:, sl] = dxd * t["dtsel"] + dexp_v[:, sl] * dy
            dc_acc = dc_acc + _nn(dye_m, prev_m)
            db_acc = db_acc + _nn((xd * t["fsel"]).astype(MXU), dnext_m)
        dcb_m = dcb.astype(MXU)
        dc_ref[...] = dc_acc + _nn(dcb_m, bm)
        db_ref[...] = db_acc + _tn(dcb_m, cm)
        dacs = dacs - jnp.concatenate([csum, jnp.zeros((CH - 8, CH), F32)], axis=0).T
        dla = _nn(triu.astype(F32), dacs, HI)
        ddt_ref[...] = dla * arow_v + ddtx
        dal_ref[...] += jnp.sum(dla * dt, axis=0, keepdims=True) * arow_v

    def row(width, off_blocks):
        return pl.BlockSpec((CH, width), lambda g, b, c: (b * nc + nc - 1 - c, off_blocks + g))

    gvec = pl.BlockSpec((1, GW), lambda g, b, c: (0, g))
    hvec = pl.BlockSpec((1, 128), lambda g, b, c: (0, g))
    return pl.pallas_call(
        body, grid=(NG, bl, nc),
        in_specs=[row(GW, 0), row(GW, 0), row(GW, 0), row(NS, DI // NS), row(NS, DI // NS + NG), row(GW, O_Z // GW),
                  row(128, 0), pl.BlockSpec((None, 8, CH), lambda g, b, c: (g, 0, b * nc + nc - 1 - c)),
                  hvec, pl.BlockSpec((None, 8, 1), lambda g, b, c: (g, 0, 0)), gvec, gvec,
                  pl.BlockSpec((None, 4, 128, NS), lambda g, b, c: (b * nc + nc - 1 - c, g, 0, 0)),
                  pl.BlockSpec(memory_space=pl.ANY)],
        out_specs=[row(GW, O_Z // GW), row(GW, 0), row(NS, 0), row(NS, 0), row(128, 0), gvec, gvec, hvec],
        input_output_aliases={13: 0},
        out_shape=[jax.ShapeDtypeStruct(dpp.shape, dpp.dtype), jax.ShapeDtypeStruct((T, DI), F32),
                   jax.ShapeDtypeStruct((T, NG * NS), F32), jax.ShapeDtypeStruct((T, NG * NS), F32),
                   jax.ShapeDtypeStruct((T, NG * 128), F32), jax.ShapeDtypeStruct((1, DI), F32),
                   jax.ShapeDtypeStruct((1, DI), F32), jax.ShapeDtypeStruct((1, NG * 128), F32)],
        scratch_shapes=[pltpu.VMEM((4, 128, NS), F32)],
        compiler_params=_cp("arbitrary", "arbitrary", "arbitrary"), name=name,
    )(dyn, y, xc, xc, xc, pp, dtg, dtt, arow, acol, dexp, ng, prev, dpp)


def _merge_fwd(pp, ya, yb, *, name):
    T = ya.shape[0]
    tr = _tile(T, 512, 8)

    def body(ga_ref, gb_ref, ya_ref, yb_ref, o_ref):
        o_ref[...] = (_sig(ga_ref[...]) * ya_ref[...] + _sig(gb_ref[...]) * yb_ref[...]).astype(BF16)

    row = pl.BlockSpec((tr, D), lambda i: (i, 0))
    return pl.pallas_call(
        body, grid=(T // tr,),
        in_specs=[pl.BlockSpec((tr, D), lambda i: (i, O_GA // D)), pl.BlockSpec((tr, D), lambda i: (i, O_GB // D)),
                  row, row],
        out_specs=row, out_shape=jax.ShapeDtypeStruct((T, D), BF16), compiler_params=_cp("parallel"),
        name=name)(pp, pp, ya, yb)


def _merge_bwd(pp, ya, yb, dm, *, name):
    T = ya.shape[0]
    tr = _tile(T, 512, 8)
    assert O_GB == O_GA + D and O_GA % (2 * D) == 0

    def body(g_ref, ya_ref, yb_ref, dm_ref, dya_ref, dyb_ref, dg_ref):
        sa, sb, dmv = _sig(g_ref[:, :D]), _sig(g_ref[:, D:]), dm_ref[...]
        dya_ref[...] = (dmv * sa).astype(BF16)
        dyb_ref[...] = (dmv * sb).astype(BF16)
        dg_ref[:, :D] = (dmv * ya_ref[...] * (sa * (1.0 - sa))).astype(BF16)
        dg_ref[:, D:] = (dmv * yb_ref[...] * (sb * (1.0 - sb))).astype(BF16)

    row = pl.BlockSpec((tr, D), lambda i: (i, 0))
    gates = pl.BlockSpec((tr, 2 * D), lambda i: (i, O_GA // (2 * D)))
    act = jax.ShapeDtypeStruct((T, D), BF16)
    return pl.pallas_call(
        body, grid=(T // tr,), in_specs=[gates, row, row, row], out_specs=[row, row, gates],
        out_shape=[act, act, jax.ShapeDtypeStruct((T, NPP), BF16)], compiler_params=_cp("parallel"),
        name=name)(pp, ya, yb, dm)


def _softmax_rows(sc):
    e = jnp.exp(sc - jnp.max(sc, axis=-1, keepdims=True))
    return e / jnp.sum(e, axis=-1, keepdims=True)


def _attn_fwd(q, kv, bl, s, *, name):
    m = kv.shape[0] // bl
    tq = _tile(s, 512)
    nq = s // tq
    scale = 1.0 / math.sqrt(XD)

    def body(q_ref, k_ref, v_ref, o_ref):
        p = _softmax_rows(_nt(q_ref[...], k_ref[...]) * scale)
        o_ref[...] = _nn(p.astype(MXU), v_ref[...]).astype(BF16)

    qspec = pl.BlockSpec((tq, XD), lambda b, h, i: (b * nq + i, h))
    return pl.pallas_call(
        body, grid=(bl, XH, nq),
        in_specs=[qspec, pl.BlockSpec((m, XD), lambda b, h, i: (b, h)),
                  pl.BlockSpec((m, XD), lambda b, h, i: (b, XH + h))],
        out_specs=qspec, out_shape=jax.ShapeDtypeStruct((bl * s, D), BF16),
        compiler_params=_cp("parallel", "parallel", "parallel"), name=name)(q, kv, kv)


def _attn_bwd(q, kv, do, bl, s, *, name):
    m = kv.shape[0] // bl
    tq = _tile(s, 512)
    nq = s // tq
    scale = 1.0 / math.sqrt(XD)

    def body(q_ref, k_ref, v_ref, do_ref, dq_ref, dk_ref, dv_ref):
        @pl.when(pl.program_id(2) == 0)
        def _():
            dk_ref[...] = jnp.zeros_like(dk_ref)
            dv_ref[...] = jnp.zeros_like(dv_ref)

        qv, kvv, vv, dov = q_ref[...], k_ref[...], v_ref[...], do_ref[...]
        p = _softmax_rows(_nt(qv, kvv) * scale)
        dp = _nt(dov, vv)
        ds = (p * (dp - jnp.sum(dp * p, axis=-1, keepdims=True)) * scale).astype(MXU)
        dq_ref[...] = _nn(ds, kvv).astype(BF16)
        dk_ref[...] += _tn(ds, qv)
        dv_ref[...] += _tn(p.astype(MXU), dov)

    qspec = pl.BlockSpec((tq, XD), lambda b, h, i: (b * nq + i, h))
    kspec = pl.BlockSpec((m, XD), lambda b, h, i: (b, h))
    return pl.pallas_call(
        body, grid=(bl, XH, nq),
        in_specs=[qspec, kspec, pl.BlockSpec((m, XD), lambda b, h, i: (b, XH + h)), qspec],
        out_specs=[qspec, kspec, kspec],
        out_shape=[jax.ShapeDtypeStruct((bl * s, D), BF16), jax.ShapeDtypeStruct((bl * m, D), F32),
                   jax.ShapeDtypeStruct((bl * m, D), F32)],
        compiler_params=_cp("parallel", "parallel", "arbitrary"), name=name)(q, kv, kv, do)


def _row_tile(r, c, max_elems=512 * 1024, align=16):
    best = None
    for t in range(align, r + 1, align):
        if r % t == 0 and t * c <= max_elems:
            best = t
    return best if best is not None else r


def _addn(a, others, *, name, also_bf16=False):
    r, c = a.shape
    tr = _row_tile(r, c)
    n = len(others)

    def body(*refs):
        acc = refs[0][...].astype(F32)
        for o_ref in refs[1:1 + n]:
            acc = acc + o_ref[...].astype(F32)
        refs[1 + n][...] = acc
        if also_bf16:
            refs[2 + n][...] = acc.astype(BF16)

    spec = pl.BlockSpec((tr, c), lambda i: (i, 0))
    shapes = [jax.ShapeDtypeStruct((r, c), F32)] + ([jax.ShapeDtypeStruct((r, c), BF16)] if also_bf16 else [])
    out = pl.pallas_call(
        body, grid=(r // tr,), in_specs=[spec] * (1 + n), out_specs=[spec] * len(shapes), out_shape=shapes,
        compiler_params=_cp("parallel"), name=name)(a, *others)
    return out if also_bf16 else out[0]


def _sum_leading(a, *, name):
    n, r, c = a.shape

    def body(a_ref, o_ref):
        acc = a_ref[0]
        for i in range(1, n):
            acc = acc + a_ref[i]
        o_ref[...] = acc

    return pl.pallas_call(body, out_shape=jax.ShapeDtypeStruct((r, c), F32), name=name)(a)


def _adamw_math(wv, gv, mv, vv):
    m2 = ADAM_B1 * mv + (1.0 - ADAM_B1) * gv
    v2 = ADAM_B2 * vv + (1.0 - ADAM_B2) * (gv * gv)
    m_hat = m2 / (1.0 - ADAM_B1 ** ADAM_STEP)
    v_hat = v2 / (1.0 - ADAM_B2 ** ADAM_STEP)
    return -ADAM_LR * (m_hat / (jnp.sqrt(v_hat) + ADAM_EPS) + ADAM_WD * wv), m2, v2


def _adamw(w, g, m, v, *, name):
    r, c = w.shape
    tr = _row_tile(r, c, align=8)

    def body(w_ref, g_ref, m_ref, v_ref, d_ref, mo_ref, vo_ref):
        d_ref[...], mo_ref[...], vo_ref[...] = _adamw_math(w_ref[...], g_ref[...], m_ref[...], v_ref[...])

    spec = pl.BlockSpec((tr, c), lambda i: (i, 0))
    shp = jax.ShapeDtypeStruct((r, c), F32)
    return pl.pallas_call(
        body, grid=(r // tr,), in_specs=[spec] * 4, out_specs=[spec] * 3, out_shape=[shp] * 3,
        compiler_params=_cp("parallel"), name=name)(w, g, m, v)


def _adamw_halves(w, g_mine, g_other, m, v, c, *, name):
    r, cols = w.shape
    h = r // 2
    tr = _row_tile(h, cols, align=8)
    nh = h // tr

    def body(c_ref, w_ref, gm_ref, go_ref, m_ref, v_ref, g_ref, d_ref, mo_ref, vo_ref):
        gv = jnp.where(pl.program_id(0) // nh == c_ref[0], gm_ref[...], go_ref[...])
        g_ref[...] = gv
        d_ref[...], mo_ref[...], vo_ref[...] = _adamw_math(w_ref[...], gv, m_ref[...], v_ref[...])

    full = pl.BlockSpec((tr, cols), lambda i, c_: (i, 0))
    half = pl.BlockSpec((tr, cols), lambda i, c_: (i % nh, 0))
    shp = jax.ShapeDtypeStruct((r, cols), F32)
    return pl.pallas_call(
        body,
        grid_spec=pltpu.PrefetchScalarGridSpec(num_scalar_prefetch=1, grid=(2 * nh,),
                                               in_specs=[full, half, half, full, full], out_specs=[full] * 4),
        out_shape=[shp] * 4, compiler_params=_cp("parallel"), name=name,
    )(jnp.reshape(c, (1,)).astype(jnp.int32), w, g_mine, g_other, m, v)


def _flip(i, d):
    return 1 - i if d else i


def _comm(name, ins, out_shapes, n_remote, n_local, plan, aliases=None):
    n_in, n_out = len(ins), len(out_shapes)

    def body(*refs):
        in_refs, out_refs = refs[:n_in], refs[n_in:n_in + n_out]
        send_sems, recv_sems = refs[n_in + n_out], refs[n_in + n_out + 1]
        x, y, c = lax.axis_index("x"), lax.axis_index("y"), lax.axis_index("c")
        remote, local = plan(in_refs, out_refs, x, y, c)
        assert len(remote) == n_remote and len(local) == n_local
        copies = []
        if n_local:
            loc_sems = refs[n_in + n_out + 2]
            copies += [pltpu.make_async_copy(s_, d_, loc_sems.at[i]) for i, (s_, d_) in enumerate(local)]
        copies += [pltpu.make_async_remote_copy(src_ref=s_, dst_ref=d_, send_sem=send_sems.at[i],
                                                recv_sem=recv_sems.at[i], device_id=dev, device_id_type=MESH)
                   for i, (s_, d_, dev) in enumerate(remote)]
        for cp in copies:
            cp.start()
        for cp in copies:
            cp.wait()

    hbm = pl.BlockSpec(memory_space=pl.ANY)
    scratch = [pltpu.SemaphoreType.DMA((n_remote,)), pltpu.SemaphoreType.DMA((n_remote,))]
    if n_local:
        scratch.append(pltpu.SemaphoreType.DMA((n_local,)))
    return pl.pallas_call(
        body, in_specs=[hbm] * n_in, out_specs=[hbm] * n_out, out_shape=out_shapes, scratch_shapes=scratch,
        input_output_aliases=aliases or {}, compiler_params=pltpu.CompilerParams(has_side_effects=True),
        name=name)(*ins)


HBM_SPEC = pl.BlockSpec(memory_space=pltpu.HBM)
SEM_SPEC = pl.BlockSpec(memory_space=pltpu.SEMAPHORE)
DATAFLOW = pltpu.SideEffectType.DATAFLOW_SIDE_EFFECTING


def _remote_copies(plan, srcs, lands, send_sems, recv_sems, n_copies):
    x, y, c = lax.axis_index("x"), lax.axis_index("y"), lax.axis_index("c")
    copies = plan(srcs, lands, x, y, c)
    assert len(copies) == n_copies
    return [pltpu.make_async_remote_copy(src_ref=s_, dst_ref=d_, send_sem=send_sems.at[i], recv_sem=recv_sems.at[i],
                                         device_id=dev, device_id_type=MESH) for i, (s_, d_, dev) in enumerate(copies)]


def _split_start(name, srcs, lands, n_copies, plan, after=None):
    ns, nb = len(srcs), len(srcs) + len(lands)
    n_after = 0 if after is None else 1
    n_in = nb + n_after

    def body(*refs):
        for cp in _remote_copies(plan, refs[:ns], refs[ns:nb], refs[n_in], refs[n_in + 1], n_copies):
            cp.start()
        refs[-1][...] = jnp.zeros_like(refs[-1])

    arrays = [pltpu.with_memory_space_constraint(a_, pltpu.HBM) for a_ in list(srcs) + list(lands)]
    out = pl.pallas_call(
        body, name=name,
        out_shape=(pltpu.SemaphoreType.DMA((n_copies,)), pltpu.SemaphoreType.DMA((n_copies,)),
                   *[pltpu.HBM(a_.shape, a_.dtype) for a_ in arrays], jax.ShapeDtypeStruct((8, 128), F32)),
        in_specs=[HBM_SPEC] * nb + [pl.BlockSpec(memory_space=pl.ANY)] * n_after,
        out_specs=(SEM_SPEC, SEM_SPEC, *[HBM_SPEC] * nb, pl.BlockSpec(memory_space=pltpu.VMEM)),
        input_output_aliases={i: 2 + i for i in range(nb)},
        compiler_params=pltpu.CompilerParams(has_side_effects=DATAFLOW))(*arrays, *([after] * n_after))
    return (out[0], out[1], list(out[2:2 + nb])), out[-1]


def _split_wait(name, handle, ns, n_copies, plan, after):
    send_sems, recv_sems, bufs = handle
    nb = len(bufs)

    def body(*refs):
        for cp in _remote_copies(plan, refs[:ns], refs[ns:nb], refs[nb], refs[nb + 1], n_copies):
            cp.wait_send()
            cp.wait_recv()

    out = pl.pallas_call(
        body, name=name, out_shape=[pltpu.HBM(b_.shape, b_.dtype) for b_ in bufs],
        in_specs=[HBM_SPEC] * nb + [SEM_SPEC, SEM_SPEC, pl.BlockSpec(memory_space=pl.ANY)],
        out_specs=[HBM_SPEC] * nb, input_output_aliases={i: i for i in range(nb)},
        compiler_params=pltpu.CompilerParams(has_side_effects=DATAFLOW))(*bufs, send_sems, recv_sems, after)
    return list(out[ns:])


def _gather_start(shards, tag, after=None):
    n = len(shards)
    lands = [lax.empty((4,) + s.shape, s.dtype) for s in shards]

    def plan(srcs, dsts, x, y, c):
        k = 2 * x + y
        copies = []
        for w_ref, o_ref in zip(srcs, dsts):
            h = w_ref.shape[0] // 2
            rows = pl.ds(c * h, h)
            copies += [(w_ref.at[rows], o_ref.at[k, rows], (_flip(x, dx), _flip(y, dy), c)) for dx, dy in CHIP_FLIPS]
        return copies

    handle, token = _split_start(f"gather_{tag}_start", shards, lands, 3 * n, plan, after)
    return (handle, plan, n), token


def _gather_wait(started, after, tag):
    handle, plan, n = started
    return _split_wait(f"gather_{tag}_wait", handle, n, 3 * n, plan, after)


def _gather_d2d(lands, before, tag):
    n = len(lands)

    def plan_d2d(in_refs, out_refs, x, y, c):
        remote = []
        for o_ref in out_refs:
            h = o_ref.shape[1] // 2
            for dx, dy in CHIP_FLIPS:
                half = o_ref.at[2 * _flip(x, dx) + _flip(y, dy), pl.ds(c * h, h)]
                remote.append((half, half, (x, y, 1 - c)))
        return remote, []

    return _comm(f"gather_{tag}_d2d", list(lands) + list(before),
                 [jax.ShapeDtypeStruct(l_.shape, l_.dtype) for l_ in lands], 3 * n, 0, plan_d2d,
                 aliases={i: i for i in range(n)})


def _pair_plan(in_refs, out_refs, x, y, c):
    return [(i_, o_, (x, y, 1 - c)) for i_, o_ in zip(in_refs, out_refs)], []


def _rs_pair_start(grads, tag):
    n = len(grads)
    c = lax.axis_index("c")

    def rows(g, start, h):
        return g.rows(start, h) if isinstance(g, _WInGrad) else lax.dynamic_slice_in_dim(g, start, h, axis=1)

    halves = [g.shape[1] // 2 for g in grads]
    mine = [rows(g, c * h, h) for g, h in zip(grads, halves)]
    send_a = [rows(g, (1 - c) * h, h).astype(BF16) for g, h in zip(grads, halves)]

    def plan(srcs, dsts, x, y, c_):
        return [(i_, o_, (x, y, 1 - c_)) for i_, o_ in zip(srcs, dsts)]

    handle, token = _split_start(f"rs_pair_{tag}_start", send_a, [lax.empty(s_.shape, BF16) for s_ in send_a], n, plan)
    return (handle, plan, n, mine), token


def _rs_chips_start(pair_started, after, tag):
    handle, pair_plan, n, mine = pair_started
    recv_a = _split_wait(f"rs_pair_{tag}_wait", handle, n, n, pair_plan, after)
    pair, pair_b = [], []
    for i, (mi, ra) in enumerate(zip(mine, recv_a)):
        four, h, cols = mi.shape
        p32, p16 = _addn(mi.reshape(four * h, cols), [ra.reshape(four * h, cols)], name=f"rs_pair_sum_{tag}_{i}",
                         also_bf16=True)
        pair.append(p32.reshape(four, h, cols))
        pair_b.append(p16.reshape(four, h, cols))

    def plan(srcs, dsts, x, y, c_):
        copies = []
        for i_, o_ in zip(srcs, dsts):
            for j, (dx, dy) in enumerate(CHIP_FLIPS):
                fx, fy = _flip(x, dx), _flip(y, dy)
                copies.append((i_.at[2 * fx + fy], o_.at[j], (fx, fy, c_)))
        return copies

    lands = [lax.empty((3,) + p.shape[1:], BF16) for p in pair_b]
    handle, token = _split_start(f"rs_chips_{tag}_start", pair_b, lands, 3 * n, plan)
    return (handle, plan, n, pair), token


def _rs_finish(started, after, tag):
    handle, plan, n, pair = started
    recv_b = _split_wait(f"rs_chips_{tag}_wait", handle, n, 3 * n, plan, after)
    k = 2 * lax.axis_index("x") + lax.axis_index("y")
    tot = [_addn(lax.dynamic_index_in_dim(p, k, 0, keepdims=False), [rb[0], rb[1], rb[2]],
                 name=f"rs_chip_sum_{tag}_{i}") for i, (p, rb) in enumerate(zip(pair, recv_b))]
    other = _comm(f"rs_halves_{tag}", tot, [jax.ShapeDtypeStruct(t.shape, F32) for t in tot], n, 0, _pair_plan)
    return tot, other


def _gather_all(vec, *, name, before=()):
    out = jax.ShapeDtypeStruct((8,) + vec.shape, vec.dtype)

    def plan(in_refs, out_refs, x, y, c):
        me = 4 * x + 2 * y + c
        remote = [(in_refs[0], out_refs[0].at[me], (_flip(x, dx), _flip(y, dy), _flip(c, dc)))
                  for dx in (0, 1) for dy in (0, 1) for dc in (0, 1) if (dx, dy, dc) != (0, 0, 0)]
        return remote, [(in_refs[0], out_refs[0].at[me])]

    return _comm(name, [vec] + list(before), [out], 7, 1, plan)[0]


def _pack(parts):
    flat = [p.reshape(-1).astype(F32) for p in parts]
    total = sum(f.shape[0] for f in flat)
    n = -(-total // 1024) * 128
    vec = jnp.concatenate(flat + [jnp.zeros((8 * n - total,), F32)]).reshape(8, n)
    offs, o = [], 0
    for f in flat:
        offs.append((o, f.shape[0]))
        o += f.shape[0]
    return vec, offs


def _unpack(vec, offs, shapes):
    flat = vec.reshape(-1)
    return [flat[o:o + n].reshape(s) for (o, n), s in zip(offs, shapes)]


BIG = (("ffn1_w_gate_up", "col"), ("ffn1_w_down", "row"), ("w_in", "col"), ("w_out_a", "row"), ("w_out_ssm", "row"),
       ("w_mix_out", "row"), ("w_q", "row"), ("w_kv", "col"), ("w_o_x", "row"), ("ffn2_w_gate_up", "col"),
       ("ffn2_w_down", "row"))
SMALL = ("ffn1_norm", "mix_norm", "conv_a_w", "ssm_conv_w", "ssm_conv_b", "ssm_dt_bias", "ssm_a_log", "ssm_d",
         "ssm_norm", "xattn_norm", "mem_norm", "ffn2_norm", "final_norm")
WEIGHTS = ("ffn1_norm", "ffn1_w_gate_up", "ffn1_w_down", "mix_norm", "w_in", "conv_a_w", "w_out_a", "ssm_conv_w",
           "ssm_conv_b", "ssm_dt_bias", "ssm_a_log", "ssm_d", "ssm_norm", "w_out_ssm", "w_mix_out", "xattn_norm",
           "mem_norm", "w_q", "w_kv", "w_o_x", "ffn2_norm", "ffn2_w_gate_up", "ffn2_w_down", "final_norm")


LAST_GRAD_GROUP = "ffn1"
GATHER_GROUPS = (("a", ("ffn1_w_gate_up",)), ("b", ("ffn1_w_down", "w_in")),
                 ("c", ("w_out_a", "w_out_ssm", "w_mix_out", "w_q", "w_kv", "w_o_x", "ffn2_w_gate_up", "ffn2_w_down")))


def _place_own(land, own, k, *, name):
    four, r, cols = land.shape
    tr = _row_tile(r, cols)

    def body(k_ref, own_ref, land_in, o_ref):
        del k_ref, land_in
        o_ref[...] = own_ref[...]

    return pl.pallas_call(
        body,
        grid_spec=pltpu.PrefetchScalarGridSpec(
            num_scalar_prefetch=1, grid=(r // tr,),
            in_specs=[pl.BlockSpec((tr, cols), lambda i, k_: (i, 0)), pl.BlockSpec(memory_space=pl.ANY)],
            out_specs=pl.BlockSpec((None, tr, cols), lambda i, k_: (k_[0], i, 0))),
        out_shape=jax.ShapeDtypeStruct(land.shape, land.dtype), input_output_aliases={2: 0},
        compiler_params=_cp("parallel"), name=name)(jnp.reshape(k, (1,)).astype(jnp.int32), own, land)


def _full_weight(land, own, kind, k, *, name):
    land = _place_own(land, own, k, name=name)
    four, r, cols = land.shape
    if kind == "row":
        return land.reshape(four * r, cols)
    return jnp.transpose(land, (1, 0, 2)).reshape(r, four * cols)


class _GatheredWeights:
    def __init__(self, shards32, k, after):
        first = GATHER_GROUPS[0][1]
        self.shards, self.k = {n: shards32[n].astype(BF16) for n in first}, k
        self.full = {}
        self.n_done = 0
        self.started, token = self._start(0, after)
        self.token = token[0, 0]
        self.shards.update({n: (w + token[0, 0]).astype(BF16) for n, w in shards32.items() if n not in first})
        self.after = jnp.stack([self.shards[n][0, 0] for n in shards32 if n not in first]).astype(F32).reshape(1, -1)

    def _start(self, gi, after):
        tag, names = GATHER_GROUPS[gi]
        return _gather_start([self.shards[n] for n in names], tag, after)

    def mark(self, value):
        self.after = value

    def __getitem__(self, name):
        if name not in self.full:
            tag, names = GATHER_GROUPS[self.n_done]
            assert name in names, (name, tag)
            lands = _gather_wait(self.started, self.after, tag)
            before = []
            if self.n_done + 1 < len(GATHER_GROUPS):
                self.started, token = self._start(self.n_done + 1, lands[0])
                before = [token]
            lands = _gather_d2d(lands, before, tag)
            for n, land in zip(names, lands):
                if n == "w_in":
                    self.full[n] = _pad_w_in_shards(_place_own(land, self.shards[n], self.k, name=f"own_{n}"))
                else:
                    self.full[n] = _full_weight(land, self.shards[n], dict(BIG)[n], self.k, name=f"own_{n}")
            self.n_done += 1
        return self.full[name]


def _shard_major(dw, kind):
    if isinstance(dw, tuple):
        return jnp.concatenate(dw, axis=0)
    if dw.ndim == 3:
        return dw
    if kind == "row":
        return dw.reshape(4, dw.shape[0] // 4, dw.shape[1])
    return jnp.transpose(dw.reshape(dw.shape[0], 4, dw.shape[1] // 4), (1, 0, 2))


def _pad_rows8(w):
    return jnp.concatenate([w, jnp.zeros((8 - w.shape[0], w.shape[1]), w.dtype)], axis=0)


def _group_lanes(v):
    r = v.shape[0]
    return jnp.pad(v.reshape(r, NG, NH // NG), ((0, 0), (0, 0), (0, 128 - NH // NG))).reshape(r, NG * 128)


def _ungroup_lanes(v):
    r = v.shape[0]
    return v.reshape(r, NG, 128)[:, :, :NH // NG].reshape(r, NH)


def _local_step(wfull, small, x, mem, target, token=0.0, on_grads=None):
    bl, s, _ = x.shape
    T = bl * s
    x2, t2 = x.reshape(T, D), target.reshape(T, D)
    mem2 = mem.reshape(-1, D)
    g = {}
    tok = [token]
    mark = getattr(wfull, "mark", lambda value: None)

    def gain(name):
        return small[name].reshape(1, -1) + tok[0]

    def emit(tag, names):
        if on_grads is not None:
            tok[0] = tok[0] + on_grads(tag, {n: g[n] for n in names})

    def ffn_fwd(h, n, wgu, wd, tag, next_gain=None):
        gate, up, a = _gate_up_fwd(n, wfull[wgu], name=f"{tag}_gate_up")
        mark(a)
        out = _mm(a, wfull[wd], "nn", tk=DFF, scale=FFN_RES, residual=h, norm_gain=next_gain, name=f"{tag}_down")
        return out, (n, gate, up, a)

    def ffn_bwd(dh, h, norm, wgu, wd, saved, tag):
        n, gate, up, a = saved
        dgate, dup = _act_bwd(dh, wfull[wd], gate, up, FFN_RES, name=f"{tag}_d_act")
        g[wd] = _mm(a, dh, "tn", tm=1408, scale=FFN_RES, name=f"{tag}_d_w_down")
        g[wgu] = (_mm(n, dgate, "tn", tn=1408, col_shards=2, name=f"{tag}_d_w_gate"),
                  _mm(n, dup, "tn", tn=1408, col_shards=2, name=f"{tag}_d_w_up"))
        emit(tag, (wgu, wd))
        dn = _mm(dgate, wfull[wgu], "nt", a2=dup, tm=512, tk=DFF, name=f"{tag}_d_norm_out")
        dh_in, g[norm] = _norm_bwd(h, gain(norm), dn, dh, name=f"{tag}_d_norm")
        return dh_in

    n1 = _norm_fwd(x2, gain("ffn1_norm"), name="ffn1_norm")
    (h1, u), ffn1_saved = ffn_fwd(x2, n1, "ffn1_w_gate_up", "ffn1_w_down", "ffn1", gain("mix_norm"))
    mark(h1)
    pp = _mm(u, wfull["w_in"], "nn", tn=1152, name="in_proj")
    wa8 = _pad_rows8(small["conv_a_w"])
    ws8 = _pad_rows8(small["ssm_conv_w"])
    conv_b = gain("ssm_conv_b")
    bias128 = jnp.pad(gain("ssm_dt_bias"), ((0, 0), (0, 128 - NH)))
    ya_pre = _conv_a_fwd(pp, wa8, bl, s, name="conv_a")
    xc = _conv_ssm_fwd(pp, ws8, conv_b, bl, s, name="conv_ssm")
    mark(xc)
    alog = gain("ssm_a_log")
    dtg, dtt = _dt_fwd(pp, bias128, name="dt")
    arow, acol = _group_lanes(alog), alog.reshape(NG, NH // NG, 1)
    dexp = jnp.repeat(gain("ssm_d"), HD, axis=1)
    ng = gain("ssm_norm")
    y, yn, prev = _ssd_fwd(xc, pp, dtg, dtt, arow, acol, dexp, ng, bl, s, name="ssd")
    ya = _mm(ya_pre, wfull["w_out_a"], "nn", tn=1024, name="out_a")
    yb = _mm(yn, wfull["w_out_ssm"], "nn", tn=1024, tk=DI, name="out_ssm")
    merged = _merge_fwd(pp, ya, yb, name="merge")
    h2, un = _mm(merged, wfull["w_mix_out"], "nn", residual=h1, norm_gain=gain("xattn_norm"), name="mix_out")
    q = _mm(un, wfull["w_q"], "nn", tn=1024, out_dtype=BF16, name="q_proj")
    mn = _norm_fwd(mem2, gain("mem_norm"), name="mem_norm")
    kv = _mm(mn, wfull["w_kv"], "nn", tn=1024, out_dtype=BF16, name="kv_proj")
    o = _attn_fwd(q, kv, bl, s, name="attn")
    h3, n2 = _mm(o, wfull["w_o_x"], "nn", residual=h2, norm_gain=gain("ffn2_norm"), name="attn_out")
    h4, ffn2_saved = ffn_fwd(h3, n2, "ffn2_w_gate_up", "ffn2_w_down", "ffn2")
    sq_err, dh4, dgf = _final_loss(h4, gain("final_norm"), t2, name="final_loss")
    g["final_norm"] = dgf

    dh3 = ffn_bwd(dh4, h3, "ffn2_norm", "ffn2_w_gate_up", "ffn2_w_down", ffn2_saved, "ffn2")
    do = _mm(dh3, wfull["w_o_x"], "nt", tn=1024, out_dtype=BF16, name="d_attn_o")
    g["w_o_x"] = _mm(o, dh3, "tn",name="d_w_o_x")
    dq, dk, dv = _attn_bwd(q, kv, do, bl, s, name="d_attn")
    dun = _mm(dq, wfull["w_q"], "nt", tn=1024, name="d_xattn_norm_out")
    g["w_q"] = _mm(un, dq, "tn",name="d_w_q")
    dkv = jnp.concatenate([dk, dv], axis=1)
    dmn = _mm(dkv, wfull["w_kv"], "nt", tn=1024, tk=2 * D, name="d_mem_norm_out")
    g["w_kv"] = _mm(mn, dkv, "tn", tn=512, col_shards=4, name="d_w_kv")
    emit("attn", ("w_q", "w_kv", "w_o_x"))
    _, g["mem_norm"] = _norm_bwd(mem2, gain("mem_norm"), dmn, None, name="d_mem_norm")
    dh2, g["xattn_norm"] = _norm_bwd(h2, gain("xattn_norm"), dun, dh3, name="d_xattn_norm")
    dmerged = _mm(dh2, wfull["w_mix_out"], "nt", tn=1024, name="d_merged")
    g["w_mix_out"] = _mm(merged, dh2, "tn",name="d_w_mix_out")
    dya, dyb, dpp = _merge_bwd(pp, ya, yb, dmerged, name="d_merge")
    dya_pre = _mm(dya, wfull["w_out_a"], "nt", tn=1024, name="d_conv_a_out")
    g["w_out_a"] = _mm(ya_pre, dya, "tn",name="d_w_out_a")
    dyn = _mm(dyb, wfull["w_out_ssm"], "nt", tn=DI, name="d_ssd_out")
    g["w_out_ssm"] = _mm(yn, dyb, "tn",name="d_w_out_ssm")
    dpp, dwa8 = _conv_a_bwd(pp, wa8, dya_pre, dpp, bl, s, name="d_conv_a")
    g["conv_a_w"] = dwa8[:3]
    dpp, dxs, dbm, dcm, ddtg, g["ssm_norm"], ddexp, dalg = _ssd_bwd(
        dyn, y, xc, pp, dtg, dtt, arow, acol, dexp, ng, prev, dpp, bl, s, name="d_ssd")
    g["ssm_d"] = ddexp.reshape(NH, HD).sum(axis=1).reshape(1, NH)
    g["ssm_a_log"] = _ungroup_lanes(dalg)
    conv_dw, conv_db = [], []
    for dpart, off, tag in ((dxs, 0, "x"), (dbm, DI, "b"), (dcm, DI + NG * NS, "c")):
        dpp, dw_, db_ = _conv_ssm_bwd(pp, ws8, conv_b, dpart, off, dpp, bl, s, name=f"d_conv_ssm_{tag}")
        conv_dw.append(dw_)
        conv_db.append(db_)
    g["ssm_conv_w"] = jnp.concatenate(conv_dw, axis=1)[:4]
    g["ssm_conv_b"] = jnp.concatenate(conv_db, axis=1)
    dpp, dbias = _dt_bwd(pp, bias128, ddtg, dpp, name="d_dt")
    g["ssm_dt_bias"] = dbias[:, :NH]
    g["w_in"] = _mm(u, dpp, "tn", tn=1152, name="d_w_in")
    emit("mix", ("w_in", "w_out_a", "w_out_ssm", "w_mix_out"))
    du = _mm(dpp, wfull["w_in"], "nt", tk=3456, name="d_mix_norm_out")
    dh1, g["mix_norm"] = _norm_bwd(h1, gain("mix_norm"), du, dh2, name="d_mix_norm")
    dx = ffn_bwd(dh1, x2, "ffn1_norm", "ffn1_w_gate_up", "ffn1_w_down", ffn1_saved, "ffn1")
    return sq_err, dx, g


W_IN_SHARD = NIN // 4


def _w_in_segments():
    segs, p = [], 0
    for t in range(D // CA_TILE):
        for which in range(3):
            segs.append((D * which + CA_TILE * t, p, CA_TILE))
            p += CA_TILE
    for s, n in ((3 * D, O_GA - 3 * D), (O_GA + NH, 2 * D), (O_GA, NH)):
        segs.append((s, p, n))
        p += n
    assert p == NIN and segs[-1][1] == O_DT and segs[-2][1] == O_GA
    return segs


def _pad_w_in_shards(land):
    pieces = []
    for s, _, n in _w_in_segments():
        while n > 0:
            kk, off = divmod(s, W_IN_SHARD)
            take = min(n, W_IN_SHARD - off)
            pieces.append(land[kk][:, off:off + take])
            s, n = s + take, n - take
    return jnp.concatenate(pieces + [jnp.zeros((land.shape[1], NPP - NIN), land.dtype)], axis=1)


class _WInGrad:
    def __init__(self, dwp):
        self.dwp = dwp
        self.shape = (4, dwp.shape[0], W_IN_SHARD)

    def rows(self, start, n):
        part = lax.dynamic_slice_in_dim(self.dwp, start, n, axis=0)
        shards = []
        for kk in range(4):
            n0, n1 = W_IN_SHARD * kk, W_IN_SHARD * (kk + 1)
            cuts = sorted((max(s, n0), p + max(s, n0) - s, min(s + m, n1) - max(s, n0))
                          for s, p, m in _w_in_segments() if min(s + m, n1) > max(s, n0))
            shards.append(jnp.concatenate([part[:, p:p + m] for _, p, m in cuts], axis=1))
        return jnp.stack(shards)


def _pad_w_in(w):
    return _pad_w_in_shards(jnp.stack(jnp.split(w, 4, axis=1)))


def _unpad_w_in(w):
    return jnp.concatenate(list(_WInGrad(w).rows(0, w.shape[0])), axis=1)


def kernel(x, mem, ffn1_norm, ffn1_w_gate_up, ffn1_w_down, mix_norm, w_in, conv_a_w, w_out_a, ssm_conv_w, ssm_conv_b, ssm_dt_bias, ssm_a_log, ssm_d, ssm_norm, w_out_ssm, w_mix_out, xattn_norm, mem_norm, w_q, w_kv, w_o_x, ffn2_norm, ffn2_w_gate_up, ffn2_w_down, final_norm, loss_target, m_ffn1_norm, m_ffn1_w_gate_up, m_ffn1_w_down, m_mix_norm, m_w_in, m_conv_a_w, m_w_out_a, m_ssm_conv_w, m_ssm_conv_b, m_ssm_dt_bias, m_ssm_a_log, m_ssm_d, m_ssm_norm, m_w_out_ssm, m_w_mix_out, m_xattn_norm, m_mem_norm, m_w_q, m_w_kv, m_w_o_x, m_ffn2_norm, m_ffn2_w_gate_up, m_ffn2_w_down, m_final_norm, v_ffn1_norm, v_ffn1_w_gate_up, v_ffn1_w_down, v_mix_norm, v_w_in, v_conv_a_w, v_w_out_a, v_ssm_conv_w, v_ssm_conv_b, v_ssm_dt_bias, v_ssm_a_log, v_ssm_d, v_ssm_norm, v_w_out_ssm, v_w_mix_out, v_xattn_norm, v_mem_norm, v_w_q, v_w_kv, v_w_o_x, v_ffn2_norm, v_ffn2_w_gate_up, v_ffn2_w_down, v_final_norm):
    a = dict(locals())
    xi, yi = lax.axis_index("x"), lax.axis_index("y")
    k = 2 * xi + yi

    conv_vec, conv_offs = _pack([a["conv_a_w"], a["ssm_conv_w"]])
    conv_all = _gather_all(conv_vec, name="gather_conv_w")
    wfull = _GatheredWeights({n: a[n][0] for n, _ in BIG}, k, conv_all)
    conv_sh = [_unpack(conv_all[2 * kk], conv_offs, [a["conv_a_w"].shape[1:], a["ssm_conv_w"].shape[1:]])
               for kk in range(4)]
    small = {n: a[n] for n in SMALL}
    small["conv_a_w"] = jnp.concatenate([cs[0] for cs in conv_sh], axis=1)
    small["ssm_conv_w"] = jnp.concatenate([cs[1] for cs in conv_sh], axis=1)

    rs_started, pending = [], []

    def chips_start(after):
        tag, names, st = pending.pop()
        st, tk = _rs_chips_start(st, after, tag)
        rs_started.append((tag, names, st))
        return tk[0, 0]

    def on_grads(tag, grads):
        names = [n for n, _ in BIG if n in grads]
        first = grads[names[0]]
        token = chips_start(first[0] if isinstance(first, tuple) else first) if pending else 0.0
        shard_major = [_WInGrad(grads[n]) if n == "w_in" else _shard_major(grads[n], dict(BIG)[n]) for n in names]
        st, tk = _rs_pair_start(shard_major, tag)
        pending.append((tag, names, st))
        if tag == LAST_GRAD_GROUP:
            return token + tk[0, 0] + chips_start(tk)
        return token + tk[0, 0]

    sq_err, dx, g = _local_step(wfull, small, x, mem, loss_target, wfull.token, on_grads)
    loss = lax.psum(0.5 / D * jnp.sum(sq_err), ("x", "y", "c"))

    ci = lax.axis_index("c")
    out = {}

    def finish(tag, names, st, after):
        g_mine, g_other = _rs_finish(st, after, tag)
        for n, gm, go in zip(names, g_mine, g_other):
            res = _adamw_halves(a[n][0], gm, go, a["m_" + n][0], a["v_" + n][0], ci, name=f"adamw_{n}")
            out[n] = tuple(t.reshape(a[n].shape) for t in res)
        return res[1]

    done = dx
    for grp in rs_started[:-1]:
        done = finish(*grp, dx)

    full_shapes = [g[n].shape for n in SMALL]
    gvec, goffs = _pack([g[n] for n in SMALL])
    gsum = _sum_leading(_gather_all(gvec, name="gather_small_grads", before=[done]), name="sum_small_grads")
    finish(*rs_started[-1], gsum)
    gsmall = dict(zip(SMALL, _unpack(gsum, goffs, full_shapes)))
    for n in ("conv_a_w", "ssm_conv_w"):
        width = a[n].shape[2]
        gsmall[n] = lax.dynamic_slice_in_dim(gsmall[n], k * width, width, axis=1)
    local_shapes = [a[n].shape for n in SMALL]
    packs = [_pack([t[n] for n in SMALL]) for t in
             ({n: a[n] for n in SMALL}, gsmall, {n: a["m_" + n] for n in SMALL}, {n: a["v_" + n] for n in SMALL})]
    offs = packs[0][1]
    res = _adamw(*[p[0] for p in packs], name="adamw_small")
    unp = [_unpack(r, offs, local_shapes) for r in res]
    for i, n in enumerate(SMALL):
        out[n] = (gsmall[n].reshape(a[n].shape), unp[0][i], unp[1][i], unp[2][i])

    grad_x = dx.reshape(x.shape)
    return (loss, grad_x, *[out[n][0] for n in WEIGHTS], *[out[n][1] for n in WEIGHTS],
            *[out[n][2] for n in WEIGHTS], *[out[n][3] for n in WEIGHTS])
```

```python
import functools
import math

import jax
import jax.numpy as jnp
from jax import lax
from jax.experimental import pallas as pl
from jax.experimental.pallas import tpu as pltpu

F32 = jnp.float32
BF16 = jnp.bfloat16
MXU = jnp.bfloat16
HI = lax.Precision.HIGHEST

D = 1024
DFF = 2816
DI = 2048
NH, HD, NG, NS, CH = 32, 64, 4, 128, 128
GW = DI // NG
XH, XD = 4, 256
EPS = 1e-6
NEG = -1e30
CA_TILE = 256
O_CA, O_Z, O_XBC, O_GA, O_GB, O_DT, NPP = 0, 3072, 5120, 8192, 9216, 10240, 10368
NIN = 10272
FFN_RES = 0.5
ADAM_LR, ADAM_B1, ADAM_B2, ADAM_EPS, ADAM_WD, ADAM_STEP = 0.001, 0.9, 0.999, 1e-08, 0.01, 10
VMEM_LIMIT = 56 * 1024 * 1024
EPI_COLS = 256
SSD_EX = 2
MESH = pl.DeviceIdType.MESH
CHIP_FLIPS = ((1, 0), (0, 1), (1, 1))


def _cp(*sem):
    return pltpu.CompilerParams(dimension_semantics=sem, vmem_limit_bytes=VMEM_LIMIT)


def _tile(n, pref, align=128):
    if n <= pref:
        return n
    t = (pref // align) * align
    while t >= align:
        if n % t == 0:
            return t
        t -= align
    raise ValueError((n, pref))


def _dot(a, b, dims, prec=None):
    return lax.dot_general(a, b, (dims, ((), ())), preferred_element_type=F32, precision=prec)


def _nn(a, b, prec=None):
    return _dot(a, b, ((1,), (0,)), prec)


def _nt(a, b):
    return _dot(a, b, ((1,), (1,)))


def _tn(a, b):
    return _dot(a, b, ((0,), (0,)))


def _sig(x):
    return jax.nn.sigmoid(x)


def _mm(a, b, mode, *, name, tm=1024, tn=1024, tk=None, out_dtype=F32, scale=None, residual=None, a2=None,
        col_shards=0, norm_gain=None):
    if tk is None:
        tk = 2048 if mode == "tn" else 1024
    if mode == "nn":
        (M, K), (K2, N) = a.shape, b.shape
    elif mode == "nt":
        (M, K), (N, K2) = a.shape, b.shape
        if a2 is not None:
            assert a2.shape == a.shape
            K2 = K2 // 2
    else:
        (K, M), (K2, N) = a.shape, b.shape
    assert K == K2, (name, a.shape, b.shape)
    tm, tn, tk = _tile(M, tm), _tile(N, tn), _tile(K, tk)
    nk = K // tk
    if mode == "nn":
        a_spec = pl.BlockSpec((tm, tk), lambda i, j, k: (i, k))
        b_spec = pl.BlockSpec((tk, tn), lambda i, j, k: (k, j))
        dims = ((1,), (0,))
    elif mode == "nt":
        a_spec = pl.BlockSpec((tm, tk), lambda i, j, k: (i, k))
        b_spec = pl.BlockSpec((tn, tk), lambda i, j, k: (j, k))
        dims = ((1,), (1,))
    else:
        a_spec = pl.BlockSpec((tk, tm), lambda i, j, k: (k, i))
        b_spec = pl.BlockSpec((tk, tn), lambda i, j, k: (k, j))
        dims = ((0,), (0,))
    o_spec = pl.BlockSpec((tm, tn), lambda i, j, k: (i, j))
    out_spec, out_shape = o_spec, jax.ShapeDtypeStruct((M, N), out_dtype)
    if col_shards:
        per = N // col_shards // tn
        assert per * tn * col_shards == N, (name, N, tn, col_shards)
        out_spec = pl.BlockSpec((None, tm, tn), lambda i, j, k: (j // per, i, j % per))
        out_shape = jax.ShapeDtypeStruct((col_shards, M, N // col_shards), out_dtype)
    has_res = residual is not None
    has_norm = norm_gain is not None
    assert not has_norm or (tn == N and not col_shards)
    dual = a2 is not None
    n_in = 2 + 2 * dual + has_res + has_norm

    def body(*refs):
        a_ref, b_ref = refs[0], refs[1]
        o_ref = refs[n_in]

        def finish(acc):
            if scale is not None:
                acc = acc * scale
            if has_res:
                acc = acc + refs[2 + 2 * dual][...]
            o_ref[...] = acc.astype(out_dtype)
            if has_norm:
                rs = lax.rsqrt(jnp.mean(acc * acc, axis=-1, keepdims=True) + EPS)
                refs[n_in + 1][...] = (acc * rs * refs[n_in - 1][...]).astype(BF16)

        part = _dot(a_ref[...].astype(MXU), b_ref[...].astype(MXU), dims)
        if dual:
            part = part + _dot(refs[2][...].astype(MXU), refs[3][...].astype(MXU), dims)
        if nk == 1:
            finish(part)
            return
        acc_ref = refs[-1]
        k = pl.program_id(2)

        @pl.when(k == 0)
        def _():
            acc_ref[...] = part

        @pl.when(k > 0)
        def _():
            acc_ref[...] += part

        @pl.when(k == nk - 1)
        def _():
            finish(acc_ref[...])

    ins, in_specs = [a, b], [a_spec, b_spec]
    if dual:
        ins += [a2, b]
        in_specs += [a_spec, pl.BlockSpec((tn, tk), lambda i, j, k: (j, k + nk))]
    if has_res:
        ins.append(residual)
        in_specs.append(o_spec)
    if has_norm:
        ins.append(norm_gain)
        in_specs.append(pl.BlockSpec((1, tn), lambda i, j, k: (0, j)))
        out_spec, out_shape = [out_spec, o_spec], [out_shape, jax.ShapeDtypeStruct((M, N), BF16)]
    return pl.pallas_call(
        body, grid=(M // tm, N // tn, nk), in_specs=in_specs, out_specs=out_spec, out_shape=out_shape,
        scratch_shapes=[pltpu.VMEM((tm, tn), F32)] if nk > 1 else [],
        compiler_params=_cp("parallel", "parallel", "arbitrary"), name=name)(*ins)


def _norm_fwd(x, g, *, name):
    T, d = x.shape
    tr = _tile(T, 512, 8)

    def body(x_ref, g_ref, o_ref):
        xv = x_ref[...]
        r = lax.rsqrt(jnp.mean(xv * xv, axis=-1, keepdims=True) + EPS)
        o_ref[...] = (xv * r * g_ref[...]).astype(BF16)

    return pl.pallas_call(
        body, grid=(T // tr,),
        in_specs=[pl.BlockSpec((tr, d), lambda i: (i, 0)), pl.BlockSpec((1, d), lambda i: (0, 0))],
        out_specs=pl.BlockSpec((tr, d), lambda i: (i, 0)),
        out_shape=jax.ShapeDtypeStruct((T, d), BF16), compiler_params=_cp("parallel"), name=name)(x, g)


def _norm_bwd(x, g, dn, dres, *, name):
    T, d = x.shape
    tr = _tile(T, 512, 8)
    has_res = dres is not None

    def body(*refs):
        x_ref, g_ref, dn_ref = refs[:3]
        dr_ref = refs[3] if has_res else None
        dx_ref, dg_ref = refs[-2], refs[-1]

        @pl.when(pl.program_id(0) == 0)
        def _():
            dg_ref[...] = jnp.zeros_like(dg_ref)

        xv = x_ref[...]
        dnv = dn_ref[...].astype(F32)
        r = lax.rsqrt(jnp.mean(xv * xv, axis=-1, keepdims=True) + EPS)
        xh = xv * r
        dg_ref[...] += jnp.sum(dnv * xh, axis=0, keepdims=True)
        dxh = dnv * g_ref[...]
        dx = r * (dxh - xh * jnp.mean(dxh * xh, axis=-1, keepdims=True))
        if has_res:
            dx = dx + dr_ref[...]
        dx_ref[...] = dx

    row = pl.BlockSpec((tr, d), lambda i: (i, 0))
    vec = pl.BlockSpec((1, d), lambda i: (0, 0))
    ins = [x, g, dn] + ([dres] if has_res else [])
    return pl.pallas_call(
        body, grid=(T // tr,), in_specs=[row, vec, row] + ([row] if has_res else []),
        out_specs=[row, vec],
        out_shape=[jax.ShapeDtypeStruct((T, d), F32), jax.ShapeDtypeStruct((1, d), F32)],
        compiler_params=_cp("arbitrary"), name=name)(*ins)


def _final_loss(h, g, target, *, name):
    T, d = h.shape
    tr = _tile(T, 512, 8)

    def body(h_ref, g_ref, t_ref, l_ref, dh_ref, dg_ref):
        @pl.when(pl.program_id(0) == 0)
        def _():
            l_ref[...] = jnp.zeros_like(l_ref)
            dg_ref[...] = jnp.zeros_like(dg_ref)

        xv = h_ref[...]
        r = lax.rsqrt(jnp.mean(xv * xv, axis=-1, keepdims=True) + EPS)
        xh = xv * r
        e = xh * g_ref[...] - t_ref[...]
        l_ref[...] += jnp.sum(e * e, axis=0, keepdims=True)
        dy = e * (1.0 / d)
        dg_ref[...] += jnp.sum(dy * xh, axis=0, keepdims=True)
        dxh = dy * g_ref[...]
        dh_ref[...] = r * (dxh - xh * jnp.mean(dxh * xh, axis=-1, keepdims=True))

    row = pl.BlockSpec((tr, d), lambda i: (i, 0))
    vec = pl.BlockSpec((1, d), lambda i: (0, 0))
    return pl.pallas_call(
        body, grid=(T // tr,), in_specs=[row, vec, row], out_specs=[vec, row, vec],
        out_shape=[jax.ShapeDtypeStruct((1, d), F32), jax.ShapeDtypeStruct((T, d), F32),
                   jax.ShapeDtypeStruct((1, d), F32)],
        compiler_params=_cp("arbitrary"), name=name)(h, g, target)


def _gate_up_fwd(n, wgu, *, name):
    T, d = n.shape
    f = wgu.shape[1] // 2
    tm, tn = _tile(T, 512, 8), _tile(f, DFF)
    nf = f // tn

    tc = _tile(tn, EPI_COLS)

    def body(n_ref, wg_ref, wu_ref, g_ref, u_ref, a_ref):
        nv = n_ref[...].astype(MXU)
        for j in range(tn // tc):
            sl = slice(j * tc, (j + 1) * tc)
            gv = _nn(nv, wg_ref[:, sl].astype(MXU))
            uv = _nn(nv, wu_ref[:, sl].astype(MXU))
            g_ref[:, sl] = gv.astype(BF16)
            u_ref[:, sl] = uv.astype(BF16)
            a_ref[:, sl] = (gv * _sig(gv) * uv).astype(BF16)

    out = pl.BlockSpec((tm, tn), lambda i, j: (i, j))
    act = jax.ShapeDtypeStruct((T, f), BF16)
    return pl.pallas_call(
        body, grid=(T // tm, nf),
        in_specs=[pl.BlockSpec((tm, d), lambda i, j: (i, 0)), pl.BlockSpec((d, tn), lambda i, j: (0, j)),
                  pl.BlockSpec((d, tn), lambda i, j: (0, j + nf))],
        out_specs=[out, out, out], out_shape=[act, act, act], compiler_params=_cp("parallel", "parallel"),
        name=name)(n, wgu, wgu)


def _act_bwd(dh, wd, gate, up, scale, *, name):
    T, d = dh.shape
    f = wd.shape[0]
    tm, tn = _tile(T, 512, 8), _tile(f, DFF)

    tc = _tile(tn, EPI_COLS)

    def body(dh_ref, wd_ref, g_ref, u_ref, dg_ref, du_ref):
        dhv = dh_ref[...].astype(MXU)
        for j in range(tn // tc):
            sl = slice(j * tc, (j + 1) * tc)
            da = scale * _nt(dhv, wd_ref[sl, :].astype(MXU))
            gv, uv = g_ref[:, sl].astype(F32), u_ref[:, sl].astype(F32)
            s = _sig(gv)
            dg_ref[:, sl] = (da * uv * (s * (1.0 + gv * (1.0 - s)))).astype(BF16)
            du_ref[:, sl] = (da * (gv * s)).astype(BF16)

    tile = pl.BlockSpec((tm, tn), lambda i, j: (i, j))
    act = jax.ShapeDtypeStruct((T, f), BF16)
    return pl.pallas_call(
        body, grid=(T // tm, f // tn),
        in_specs=[pl.BlockSpec((tm, d), lambda i, j: (i, 0)), pl.BlockSpec((tn, d), lambda i, j: (j, 0)), tile, tile],
        out_specs=[tile, tile], out_shape=[act, act], compiler_params=_cp("parallel", "parallel"),
        name=name)(dh, wd, gate, up)


CONV_ROWS = 64
CONV_PAD = 8


def _rows_down(ref, r0, d, cols=slice(None)):
    if r0 - d >= 0:
        return ref[pl.ds(r0 - d, CONV_ROWS), cols]
    assert r0 == 0
    v = ref[pl.ds(0, CONV_ROWS), cols]
    ri = lax.broadcasted_iota(jnp.int32, v.shape, 0)
    return jnp.where(ri >= d, pltpu.roll(v, d, 0), 0.0)


def _fold8(v):
    return jnp.sum(v.reshape(CONV_ROWS // 8, 8, v.shape[1]), axis=0)


def _taps(w_ref, views):
    acc = None
    for k, v in enumerate(views):
        t = w_ref[k:k + 1, :] * v
        acc = t if acc is None else acc + t
    return acc


def _conv_a_fwd(pp, w8, bl, s, *, name):
    tc = CA_TILE
    nb = D // tc
    bcol, ccol, vcol = slice(0, tc), slice(tc, 2 * tc), slice(2 * tc, 3 * tc)

    def body(p_ref, w_ref, o_ref):
        for r0 in range(0, s, CONV_ROWS):
            cv = [_rows_down(p_ref, r0, 2 - k, ccol) * _rows_down(p_ref, r0, 2 - k, vcol) for k in range(3)]
            o_ref[pl.ds(r0, CONV_ROWS), :] = (p_ref[pl.ds(r0, CONV_ROWS), bcol] * _taps(w_ref, cv)).astype(BF16)

    return pl.pallas_call(
        body, grid=(bl, nb),
        in_specs=[pl.BlockSpec((s, 3 * tc), lambda b, j: (b, j)), pl.BlockSpec((8, tc), lambda b, j: (0, j))],
        out_specs=pl.BlockSpec((s, tc), lambda b, j: (b, j)),
        out_shape=jax.ShapeDtypeStruct((bl * s, D), BF16), compiler_params=_cp("parallel", "parallel"),
        name=name)(pp, w8)


def _conv_a_bwd(pp, w8, dya, dpp, bl, s, *, name):
    tc = CA_TILE
    nb = D // tc
    bcol, ccol, vcol = slice(0, tc), slice(tc, 2 * tc), slice(2 * tc, 3 * tc)

    def body(p_ref, w_ref, dy_ref, dpp_in, d_ref, dw_ref, dcp):
        del dpp_in

        @pl.when(pl.program_id(1) == 0)
        def _():
            dw_ref[...] = jnp.zeros_like(dw_ref)

        dcp[pl.ds(s, CONV_PAD), :] = jnp.zeros((CONV_PAD, tc), F32)
        dw_acc = [jnp.zeros((8, tc), F32) for _ in range(3)]
        for r0 in reversed(range(0, s, CONV_ROWS)):
            rows = pl.ds(r0, CONV_ROWS)
            cs = [_rows_down(p_ref, r0, 2 - k, ccol) for k in range(3)]
            vs = [_rows_down(p_ref, r0, 2 - k, vcol) for k in range(3)]
            cv = [c_ * v_ for c_, v_ in zip(cs, vs)]
            dy = dy_ref[rows, :]
            d_ref[rows, bcol] = (dy * _taps(w_ref, cv)).astype(BF16)
            dconv = dy * p_ref[rows, bcol]
            dcp[rows, :] = dconv
            dcv = _taps(w_ref, [dcp[pl.ds(r0 + 2, CONV_ROWS), :], dcp[pl.ds(r0 + 1, CONV_ROWS), :], dconv])
            d_ref[rows, ccol] = (dcv * vs[2]).astype(BF16)
            d_ref[rows, vcol] = (dcv * cs[2]).astype(BF16)
            dw_acc = [acc + _fold8(dconv * cv_) for acc, cv_ in zip(dw_acc, cv)]
        for k in range(3):
            dw_ref[k:k + 1, :] += jnp.sum(dw_acc[k], axis=0, keepdims=True)

    wspec = pl.BlockSpec((8, tc), lambda j, b: (0, j))
    wide = pl.BlockSpec((s, 3 * tc), lambda j, b: (b, j))
    return pl.pallas_call(
        body, grid=(nb, bl),
        in_specs=[wide, wspec, pl.BlockSpec((s, tc), lambda j, b: (b, j)), pl.BlockSpec(memory_space=pl.ANY)],
        out_specs=[wide, wspec], out_shape=[jax.ShapeDtypeStruct(dpp.shape, dpp.dtype), jax.ShapeDtypeStruct((8, D), F32)],
        scratch_shapes=[pltpu.VMEM((s + CONV_PAD, tc), F32)], input_output_aliases={3: 0},
        compiler_params=_cp("parallel", "arbitrary"), name=name)(pp, w8, dya, dpp)


def _conv_ssm_fwd(pp, w8, bias, bl, s, *, name):
    tc = 256
    width = DI + 2 * NG * NS
    nb = width // tc

    def body(x_ref, w_ref, b_ref, o_ref):
        for r0 in range(0, s, CONV_ROWS):
            pre = _taps(w_ref, [_rows_down(x_ref, r0, 3 - k) for k in range(4)]) + b_ref[...]
            o_ref[pl.ds(r0, CONV_ROWS), :] = pre * _sig(pre)

    return pl.pallas_call(
        body, grid=(bl, nb),
        in_specs=[pl.BlockSpec((s, tc), lambda b, j: (b, O_XBC // tc + j)),
                  pl.BlockSpec((8, tc), lambda b, j: (0, j)), pl.BlockSpec((1, tc), lambda b, j: (0, j))],
        out_specs=pl.BlockSpec((s, tc), lambda b, j: (b, j)),
        out_shape=jax.ShapeDtypeStruct((bl * s, width), F32), compiler_params=_cp("parallel", "parallel"),
        name=name)(pp, w8, bias)


def _conv_ssm_bwd(pp, w8, bias, dxc, ch_off, dpp, bl, s, *, name):
    n = dxc.shape[1]
    tc = 256
    nb = n // tc
    o0 = ch_off // tc

    def body(x_ref, w_ref, b_ref, d_ref, dpp_in, dx_ref, dw_ref, db_ref, dp):
        del dpp_in

        @pl.when(pl.program_id(1) == 0)
        def _():
            dw_ref[...] = jnp.zeros_like(dw_ref)
            db_ref[...] = jnp.zeros_like(db_ref)

        dp[pl.ds(s, CONV_PAD), :] = jnp.zeros((CONV_PAD, tc), F32)
        dw_acc = [jnp.zeros((8, tc), F32) for _ in range(4)]
        db_acc = jnp.zeros((8, tc), F32)
        for r0 in reversed(range(0, s, CONV_ROWS)):
            rows = pl.ds(r0, CONV_ROWS)
            xs = [_rows_down(x_ref, r0, 3 - k) for k in range(4)]
            pre = _taps(w_ref, xs) + b_ref[...]
            sg = _sig(pre)
            dpre = d_ref[rows, :] * (sg * (1.0 + pre * (1.0 - sg)))
            dp[rows, :] = dpre
            dx = _taps(w_ref, [dp[pl.ds(r0 + 3 - k, CONV_ROWS), :] for k in range(3)] + [dpre])
            dx_ref[rows, :] = dx.astype(BF16)
            db_acc = db_acc + _fold8(dpre)
            dw_acc = [acc + _fold8(dpre * x_) for acc, x_ in zip(dw_acc, xs)]
        db_ref[...] += jnp.sum(db_acc, axis=0, keepdims=True)
        for k in range(4):
            dw_ref[k:k + 1, :] += jnp.sum(dw_acc[k], axis=0, keepdims=True)

    return pl.pallas_call(
        body, grid=(nb, bl),
        in_specs=[pl.BlockSpec((s, tc), lambda j, b: (b, O_XBC // tc + o0 + j)),
                  pl.BlockSpec((8, tc), lambda j, b: (0, o0 + j)), pl.BlockSpec((1, tc), lambda j, b: (0, o0 + j)),
                  pl.BlockSpec((s, tc), lambda j, b: (b, j)), pl.BlockSpec(memory_space=pl.ANY)],
        out_specs=[pl.BlockSpec((s, tc), lambda j, b: (b, O_XBC // tc + o0 + j)),
                   pl.BlockSpec((8, tc), lambda j, b: (0, j)), pl.BlockSpec((1, tc), lambda j, b: (0, j))],
        out_shape=[jax.ShapeDtypeStruct(dpp.shape, dpp.dtype), jax.ShapeDtypeStruct((8, n), F32),
                   jax.ShapeDtypeStruct((1, n), F32)],
        scratch_shapes=[pltpu.VMEM((s + CONV_PAD, tc), F32)], input_output_aliases={4: 0},
        compiler_params=_cp("parallel", "arbitrary"), name=name)(pp, w8, bias, dxc, dpp)


def _softplus(x):
    return jnp.maximum(x, 0.0) + jnp.log1p(jnp.exp(-jnp.abs(x)))


def _head_group_matrix():
    h = jnp.arange(128)[:, None]
    j = jnp.arange(NG * 128)[None, :]
    per = NH // NG
    return ((h < NH) & (j == (h // per) * 128 + h % per)).astype(F32)


def _dt_fwd(pp, bias128, *, name):
    T = pp.shape[0]
    tr = _tile(T, 1024, 8)
    per = NH // NG

    def body(x_ref, b_ref, p_ref, g_ref, t_ref):
        lane = lax.broadcasted_iota(jnp.int32, (tr, 128), 1)
        dt = jnp.where(lane < NH, _softplus(x_ref[...] + b_ref[...]), 0.0)
        g_ref[...] = _nn(dt, p_ref[...], HI)
        eye = (lax.broadcasted_iota(jnp.int32, (NH, 128), 0)
               == lax.broadcasted_iota(jnp.int32, (NH, 128), 1)).astype(F32)
        t_ref[...] = _dot(eye, dt, ((1,), (1,)), HI).reshape(NG, per, tr)

    vec = pl.BlockSpec((1, 128), lambda i: (0, 0))
    return pl.pallas_call(
        body, grid=(T // tr,),
        in_specs=[pl.BlockSpec((tr, 128), lambda i: (i, O_DT // 128)), vec,
                  pl.BlockSpec((128, NG * 128), lambda i: (0, 0))],
        out_specs=[pl.BlockSpec((tr, NG * 128), lambda i: (i, 0)), pl.BlockSpec((NG, per, tr), lambda i: (0, 0, i))],
        out_shape=[jax.ShapeDtypeStruct((T, NG * 128), F32), jax.ShapeDtypeStruct((NG, per, T), F32)],
        compiler_params=_cp("parallel"), name=name)(pp, bias128, _head_group_matrix())


def _dt_bwd(pp, bias128, ddtg, dpp, *, name):
    T = pp.shape[0]
    tr = _tile(T, 1024, 8)

    def body(x_ref, b_ref, d_ref, p_ref, dpp_in, o_ref, db_ref):
        del dpp_in

        @pl.when(pl.program_id(0) == 0)
        def _():
            db_ref[...] = jnp.zeros_like(db_ref)

        lane = lax.broadcasted_iota(jnp.int32, (tr, 128), 1)
        ddt = _dot(d_ref[...], p_ref[...], ((1,), (1,)), HI)
        dr = jnp.where(lane < NH, ddt * _sig(x_ref[...] + b_ref[...]), 0.0)
        db_ref[...] += jnp.sum(dr, axis=0, keepdims=True)
        o_ref[...] = dr.astype(BF16)

    col = pl.BlockSpec((tr, 128), lambda i: (i, O_DT // 128))
    vec = pl.BlockSpec((1, 128), lambda i: (0, 0))
    return pl.pallas_call(
        body, grid=(T // tr,),
        in_specs=[col, vec, pl.BlockSpec((tr, NG * 128), lambda i: (i, 0)), pl.BlockSpec((128, NG * 128), lambda i: (0, 0)),
                  pl.BlockSpec(memory_space=pl.ANY)],
        out_specs=[col, vec],
        out_shape=[jax.ShapeDtypeStruct(dpp.shape, dpp.dtype), jax.ShapeDtypeStruct((1, 128), F32)],
        input_output_aliases={4: 0}, compiler_params=_cp("arbitrary"),
        name=name)(pp, bias128, ddtg, _head_group_matrix(), dpp)


def _tril():
    return lax.broadcasted_iota(jnp.int32, (CH, CH), 0) >= lax.broadcasted_iota(jnp.int32, (CH, CH), 1)


def _ssd_common(dt, dtt, arow, acol):
    ri = lax.broadcasted_iota(jnp.int32, (CH, CH), 0)
    ci = lax.broadcasted_iota(jnp.int32, (CH, CH), 1)
    tril = ri >= ci
    triu = ri <= ci
    acs_col = _nn(tril.astype(F32), dt * arow, HI)
    acs_row = _nn(dtt * acol, triu.astype(F32), HI)
    return tril, triu, acs_col, acs_row


def _pair_terms(q, dt, acs_col, acs_row, tril, lo):
    ha, hb = 2 * q, 2 * q + 1
    col_a, col_b = acs_col[:, ha:ha + 1], acs_col[:, hb:hb + 1]
    row_a, row_b = acs_row[ha:ha + 1, :], acs_row[hb:hb + 1, :]
    last_a, last_b = acs_col[CH - 1:CH, ha:ha + 1], acs_col[CH - 1:CH, hb:hb + 1]
    out = dict(
        dtsel=jnp.where(lo, dt[:, ha:ha + 1], dt[:, hb:hb + 1]),
        d_a=jnp.exp(jnp.where(tril, col_a - row_a, NEG)), d_b=jnp.exp(jnp.where(tril, col_b - row_b, NEG)),
        esel=jnp.where(lo, jnp.exp(col_a), jnp.exp(col_b)),
        fsel=jnp.where(lo, jnp.exp(last_a - col_a), jnp.exp(last_b - col_b)),
        g_a=jnp.exp(last_a), g_b=jnp.exp(last_b))
    return out


def _ssd_fwd(xc, pp, dtg, dtt, arow, acol, dexp, ng, bl, s, *, name):
    nc = s // CH
    T = bl * s

    ex = SSD_EX if bl % SSD_EX == 0 else 1

    def body(*refs):
        arow_ref, acol_ref, dexp_ref, ng_ref = refs[6 * ex:6 * ex + 4]
        st_ref = refs[-1]

        @pl.when(pl.program_id(2) == 0)
        def _():
            st_ref[...] = jnp.zeros_like(st_ref)

        y_ref, yn_ref, prev_ref = refs[6 * ex + 4:6 * ex + 7]
        for e in range(ex):
            one(*refs[6 * e:6 * e + 6], arow_ref, acol_ref, dexp_ref, ng_ref,
                y_ref.at[e], yn_ref.at[e], prev_ref.at[e], st_ref.at[e])

    def one(xs_ref, bm_ref, cm_ref, z_ref, dt_ref, dtt_ref, arow_ref, acol_ref, dexp_ref, ng_ref,
            y_ref, yn_ref, prev_ref, st_ref):
        dt = dt_ref[...]
        tril, _, acs_col, acs_row = _ssd_common(dt, dtt_ref[...], -jnp.exp(arow_ref[...]), -jnp.exp(acol_ref[...]))
        bm, cm = bm_ref[...].astype(MXU), cm_ref[...].astype(MXU)
        cb = _nt(cm, bm)
        lo = lax.broadcasted_iota(jnp.int32, (CH, 128), 1) < HD
        sub_lo = lax.broadcasted_iota(jnp.int32, (128, NS), 0) < HD
        for q in range(4):
            t = _pair_terms(q, dt, acs_col, acs_row, tril, lo)
            x = xs_ref[:, 128 * q:128 * (q + 1)]
            xd = x * t["dtsel"]
            y = (_nn((cb * t["d_a"]).astype(MXU), jnp.where(lo, xd, 0.0).astype(MXU))
                 + _nn((cb * t["d_b"]).astype(MXU), jnp.where(lo, 0.0, xd).astype(MXU)))
            prev = st_ref[q]
            prev_ref[q] = prev
            y = y + t["esel"] * _nt(cm, prev.astype(MXU))
            st_ref[q] = prev * jnp.where(sub_lo, t["g_a"], t["g_b"]) + _tn((xd * t["fsel"]).astype(MXU), bm)
            y_ref[:, 128 * q:128 * (q + 1)] = y + dexp_ref[:, 128 * q:128 * (q + 1)] * x
        zv = z_ref[...]
        yg = y_ref[...] * (zv * _sig(zv))
        r = lax.rsqrt(jnp.mean(yg * yg, axis=-1, keepdims=True) + EPS)
        yn_ref[...] = (yg * r * ng_ref[...]).astype(BF16)

    def row(e, width, off_blocks):
        return pl.BlockSpec((CH, width), lambda g, b, c: ((b * ex + e) * nc + c, off_blocks + g))

    per_ex_in = [[row(e, GW, 0), row(e, NS, DI // NS), row(e, NS, DI // NS + NG), row(e, GW, O_Z // GW), row(e, 128, 0),
                  pl.BlockSpec((None, 8, CH), lambda g, b, c, e=e: (g, 0, (b * ex + e) * nc + c))] for e in range(ex)]
    by_example = pl.BlockSpec((ex, CH, GW), lambda g, b, c: (b, c, g))
    y, yn, prev = pl.pallas_call(
        body, grid=(NG, bl // ex, nc),
        in_specs=sum(per_ex_in, []) + [pl.BlockSpec((1, 128), lambda g, b, c: (0, g)),
                                       pl.BlockSpec((None, 8, 1), lambda g, b, c: (g, 0, 0)),
                                       pl.BlockSpec((1, GW), lambda g, b, c: (0, g)),
                                       pl.BlockSpec((1, GW), lambda g, b, c: (0, g))],
        out_specs=[by_example, by_example,
                   pl.BlockSpec((ex, None, 4, 128, NS), lambda g, b, c: (b, c, g, 0, 0))],
        out_shape=[jax.ShapeDtypeStruct((bl, s, DI), F32), jax.ShapeDtypeStruct((bl, s, DI), BF16),
                   jax.ShapeDtypeStruct((bl, nc, 16, 128, NS), F32)],
        scratch_shapes=[pltpu.VMEM((ex, 4, 128, NS), F32)],
        compiler_params=_cp("parallel", "parallel", "arbitrary"), name=name,
    )(*([xc, xc, xc, pp, dtg, dtt] * ex), arow, acol, dexp, ng)
    return y.reshape(T, DI), yn.reshape(T, DI), prev.reshape(bl * nc, 16, 128, NS)


def _ssd_bwd(dyn, y, xc, pp, dtg, dtt, arow, acol, dexp, ng, prev, dpp, bl, s, *, name):
    nc = s // CH
    T = bl * s

    def rsum(v):
        return jnp.sum(v, axis=1, keepdims=True)

    def asum(v):
        return jnp.sum(jnp.sum(v, axis=0, keepdims=True), axis=1, keepdims=True)

    def body(dyn_ref, y_ref, xs_ref, bm_ref, cm_ref, z_ref, dt_ref, dtt_ref, arow_ref, acol_ref, dexp_ref, ng_ref,
             prev_ref, dpp_in, dz_ref, dxs_ref, db_ref, dc_ref, ddt_ref, dng_ref, dd_ref, dal_ref, dst_ref):
        del dpp_in

        @pl.when((pl.program_id(1) == 0) & (pl.program_id(2) == 0))
        def _():
            dng_ref[...] = jnp.zeros_like(dng_ref)
            dd_ref[...] = jnp.zeros_like(dd_ref)
            dal_ref[...] = jnp.zeros_like(dal_ref)

        @pl.when(pl.program_id(2) == 0)
        def _():
            dst_ref[...] = jnp.zeros_like(dst_ref)

        yv, zv, xsv, dexp_v = y_ref[...], z_ref[...], xs_ref[...], dexp_ref[...]
        sz = _sig(zv)
        silu = zv * sz
        yg = yv * silu
        r = lax.rsqrt(jnp.mean(yg * yg, axis=-1, keepdims=True) + EPS)
        yh = yg * r
        dynv = dyn_ref[...]
        dng_ref[...] += jnp.sum(dynv * yh, axis=0, keepdims=True)
        dyh = dynv * ng_ref[...]
        dyg = r * (dyh - yh * jnp.mean(dyh * yh, axis=-1, keepdims=True))
        dz_ref[...] = (dyg * yv * (sz * (1.0 + zv * (1.0 - sz)))).astype(BF16)
        dy_all = dyg * silu
        dd_ref[...] += jnp.sum(dy_all * xsv, axis=0, keepdims=True)

        dt = dt_ref[...]
        arow_v = -jnp.exp(arow_ref[...])
        tril, triu, acs_col, acs_row = _ssd_common(dt, dtt_ref[...], arow_v, -jnp.exp(acol_ref[...]))
        bm, cm = bm_ref[...].astype(MXU), cm_ref[...].astype(MXU)
        cb = _nt(cm, bm)
        lane = lax.broadcasted_iota(jnp.int32, (CH, 128), 1)
        is_last = lax.broadcasted_iota(jnp.int32, (CH, 128), 0) == CH - 1
        lo = lane < HD
        sub_lo = lax.broadcasted_iota(jnp.int32, (128, NS), 0) < HD
        dcb = jnp.zeros((CH, CH), F32)
        dc_acc = jnp.zeros((CH, NS), F32)
        db_acc = jnp.zeros((CH, NS), F32)
        dacs = jnp.zeros((CH, 128), F32)
        ddtx = jnp.zeros((CH, 128), F32)
        csum = jnp.zeros((8, CH), F32)
        sub8 = lax.broadcasted_iota(jnp.int32, (8, CH), 0)
        for q in range(4):
            ha, hb = 2 * q, 2 * q + 1
            sl = slice(128 * q, 128 * (q + 1))
            t = _pair_terms(q, dt, acs_col, acs_row, tril, lo)
            x, dy = xsv[:, sl], dy_all[:, sl]
            xd = x * t["dtsel"]
            xd_m = xd.astype(MXU)
            dy_lo, dy_hi = jnp.where(lo, dy, 0.0).astype(MXU), jnp.where(lo, 0.0, dy).astype(MXU)
            m_a, m_b = cb * t["d_a"], cb * t["d_b"]
            prev_m = prev_ref[q].astype(MXU)
            dnext = dst_ref[q]
            dnext_m = dnext.astype(MXU)
            bds = _nt(bm, dnext_m)
            dxd = _tn(m_a.astype(MXU), dy_lo) + _tn(m_b.astype(MXU), dy_hi) + t["fsel"] * bds
            dye_m = (dy * t["esel"]).astype(MXU)
            dst_ref[q] = dnext * jnp.where(sub_lo, t["g_a"], t["g_b"]) + _tn(dye_m, cm)
            dm_a, dm_b = _nt(dy_lo, xd_m), _nt(dy_hi, xd_m)
            dcb = dcb + dm_a * t["d_a"] + dm_b * t["d_b"]
            g_a, g_b = dm_a * m_a, dm_b * m_b
            csum = (csum + jnp.where(sub8 == ha, jnp.sum(g_a, axis=0, keepdims=True), 0.0)
                    + jnp.where(sub8 == hb, jnp.sum(g_b, axis=0, keepdims=True), 0.0))
            tf = t["fsel"] * xd * bds
            tyf = dy * (t["esel"] * _nt(cm, prev_m)) - tf
            dpp = dnext * prev_ref[q]
            ea = asum(jnp.where(lo, tf, 0.0)) + t["g_a"] * asum(jnp.where(sub_lo, dpp, 0.0))
            eb = asum(jnp.where(lo, 0.0, tf)) + t["g_b"] * asum(jnp.where(sub_lo, 0.0, dpp))
            ra = rsum(g_a + jnp.where(lo, tyf, 0.0)) + jnp.where(is_last, ea, 0.0)
            rb = rsum(g_b + jnp.where(lo, 0.0, tyf)) + jnp.where(is_last, eb, 0.0)
            dacs = dacs + jnp.where(lane == ha, ra, 0.0) + jnp.where(lane == hb, rb, 0.0)
            tx = dxd * x
            ddtx = (ddtx + jnp.where(lane == ha, rsum(jnp.where(lo, tx, 0.0)), 0.0)
                    + jnp.where(lane == hb, rsum(jnp.where(lo, 0.0, tx)), 0.0))
            dxs_ref[:, sl] = dxd * t["dtsel"] + dexp_v[:, sl] * dy
            dc_acc = dc_acc + _nn(dye_m, prev_m)
            db_acc = db_acc + _nn((xd * t["fsel"]).astype(MXU), dnext_m)
        dcb_m = dcb.astype(MXU)
        dc_ref[...] = dc_acc + _nn(dcb_m, bm)
        db_ref[...] = db_acc + _tn(dcb_m, cm)
        dacs = dacs - jnp.concatenate([csum, jnp.zeros((CH - 8, CH), F32)], axis=0).T
        dla = _nn(triu.astype(F32), dacs, HI)
        ddt_ref[...] = dla * arow_v + ddtx
        dal_ref[...] += jnp.sum(dla * dt, axis=0, keepdims=True) * arow_v

    def row(width, off_blocks):
        return pl.BlockSpec((CH, width), lambda g, b, c: (b * nc + nc - 1 - c, off_blocks + g))

    gvec = pl.BlockSpec((1, GW), lambda g, b, c: (0, g))
    hvec = pl.BlockSpec((1, 128), lambda g, b, c: (0, g))
    return pl.pallas_call(
        body, grid=(NG, bl, nc),
        in_specs=[row(GW, 0), row(GW, 0), row(GW, 0), row(NS, DI // NS), row(NS, DI // NS + NG), row(GW, O_Z // GW),
                  row(128, 0), pl.BlockSpec((None, 8, CH), lambda g, b, c: (g, 0, b * nc + nc - 1 - c)),
                  hvec, pl.BlockSpec((None, 8, 1), lambda g, b, c: (g, 0, 0)), gvec, gvec,
                  pl.BlockSpec((None, 4, 128, NS), lambda g, b, c: (b * nc + nc - 1 - c, g, 0, 0)),
                  pl.BlockSpec(memory_space=pl.ANY)],
        out_specs=[row(GW, O_Z // GW), row(GW, 0), row(NS, 0), row(NS, 0), row(128, 0), gvec, gvec, hvec],
        input_output_aliases={13: 0},
        out_shape=[jax.ShapeDtypeStruct(dpp.shape, dpp.dtype), jax.ShapeDtypeStruct((T, DI), F32),
                   jax.ShapeDtypeStruct((T, NG * NS), F32), jax.ShapeDtypeStruct((T, NG * NS), F32),
                   jax.ShapeDtypeStruct((T, NG * 128), F32), jax.ShapeDtypeStruct((1, DI), F32),
                   jax.ShapeDtypeStruct((1, DI), F32), jax.ShapeDtypeStruct((1, NG * 128), F32)],
        scratch_shapes=[pltpu.VMEM((4, 128, NS), F32)],
        compiler_params=_cp("arbitrary", "arbitrary", "arbitrary"), name=name,
    )(dyn, y, xc, xc, xc, pp, dtg, dtt, arow, acol, dexp, ng, prev, dpp)


def _merge_fwd(pp, ya, yb, *, name):
    T = ya.shape[0]
    tr = _tile(T, 512, 8)

    def body(ga_ref, gb_ref, ya_ref, yb_ref, o_ref):
        o_ref[...] = (_sig(ga_ref[...]) * ya_ref[...].astype(F32)
                      + _sig(gb_ref[...]) * yb_ref[...].astype(F32)).astype(BF16)

    row = pl.BlockSpec((tr, D), lambda i: (i, 0))
    return pl.pallas_call(
        body, grid=(T // tr,),
        in_specs=[pl.BlockSpec((tr, D), lambda i: (i, O_GA // D)), pl.BlockSpec((tr, D), lambda i: (i, O_GB // D)),
                  row, row],
        out_specs=row, out_shape=jax.ShapeDtypeStruct((T, D), BF16), compiler_params=_cp("parallel"),
        name=name)(pp, pp, ya, yb)


def _merge_bwd(pp, ya, yb, dm, *, name):
    T = ya.shape[0]
    tr = _tile(T, 512, 8)
    assert O_GB == O_GA + D and O_GA % (2 * D) == 0

    def body(g_ref, ya_ref, yb_ref, dm_ref, dya_ref, dyb_ref, dg_ref):
        sa, sb, dmv = _sig(g_ref[:, :D]), _sig(g_ref[:, D:]), dm_ref[...]
        dya_ref[...] = (dmv * sa).astype(BF16)
        dyb_ref[...] = (dmv * sb).astype(BF16)
        dg_ref[:, :D] = (dmv * ya_ref[...].astype(F32) * (sa * (1.0 - sa))).astype(BF16)
        dg_ref[:, D:] = (dmv * yb_ref[...].astype(F32) * (sb * (1.0 - sb))).astype(BF16)

    row = pl.BlockSpec((tr, D), lambda i: (i, 0))
    gates = pl.BlockSpec((tr, 2 * D), lambda i: (i, O_GA // (2 * D)))
    act = jax.ShapeDtypeStruct((T, D), BF16)
    return pl.pallas_call(
        body, grid=(T // tr,), in_specs=[gates, row, row, row], out_specs=[row, row, gates],
        out_shape=[act, act, jax.ShapeDtypeStruct((T, NPP), BF16)], compiler_params=_cp("parallel"),
        name=name)(pp, ya, yb, dm)


def _softmax_rows(sc):
    e = jnp.exp(sc - jnp.max(sc, axis=-1, keepdims=True))
    return e / jnp.sum(e, axis=-1, keepdims=True)


def _attn_fwd(q, kv, bl, s, *, name):
    m = kv.shape[0] // bl
    tq = _tile(s, 512)
    nq = s // tq
    scale = 1.0 / math.sqrt(XD)

    def body(q_ref, k_ref, v_ref, o_ref):
        p = _softmax_rows(_nt(q_ref[...], k_ref[...]) * scale)
        o_ref[...] = _nn(p.astype(MXU), v_ref[...]).astype(BF16)

    qspec = pl.BlockSpec((tq, XD), lambda b, h, i: (b * nq + i, h))
    return pl.pallas_call(
        body, grid=(bl, XH, nq),
        in_specs=[qspec, pl.BlockSpec((m, XD), lambda b, h, i: (b, h)),
                  pl.BlockSpec((m, XD), lambda b, h, i: (b, XH + h))],
        out_specs=qspec, out_shape=jax.ShapeDtypeStruct((bl * s, D), BF16),
        compiler_params=_cp("parallel", "parallel", "parallel"), name=name)(q, kv, kv)


def _attn_bwd(q, kv, do, bl, s, *, name):
    m = kv.shape[0] // bl
    tq = _tile(s, 512)
    nq = s // tq
    scale = 1.0 / math.sqrt(XD)

    def body(q_ref, k_ref, v_ref, do_ref, dq_ref, dk_ref, dv_ref):
        @pl.when(pl.program_id(2) == 0)
        def _():
            dk_ref[...] = jnp.zeros_like(dk_ref)
            dv_ref[...] = jnp.zeros_like(dv_ref)

        qv, kvv, vv, dov = q_ref[...], k_ref[...], v_ref[...], do_ref[...]
        p = _softmax_rows(_nt(qv, kvv) * scale)
        dp = _nt(dov, vv)
        ds = (p * (dp - jnp.sum(dp * p, axis=-1, keepdims=True)) * scale).astype(MXU)
        dq_ref[...] = _nn(ds, kvv).astype(BF16)
        dk_ref[...] += _tn(ds, qv)
        dv_ref[...] += _tn(p.astype(MXU), dov)

    qspec = pl.BlockSpec((tq, XD), lambda b, h, i: (b * nq + i, h))
    kspec = pl.BlockSpec((m, XD), lambda b, h, i: (b, h))
    return pl.pallas_call(
        body, grid=(bl, XH, nq),
        in_specs=[qspec, kspec, pl.BlockSpec((m, XD), lambda b, h, i: (b, XH + h)), qspec],
        out_specs=[qspec, kspec, kspec],
        out_shape=[jax.ShapeDtypeStruct((bl * s, D), BF16), jax.ShapeDtypeStruct((bl * m, D), F32),
                   jax.ShapeDtypeStruct((bl * m, D), F32)],
        compiler_params=_cp("parallel", "parallel", "arbitrary"), name=name)(q, kv, kv, do)


def _row_tile(r, c, max_elems=512 * 1024, align=16):
    best = None
    for t in range(align, r + 1, align):
        if r % t == 0 and t * c <= max_elems:
            best = t
    return best if best is not None else r


def _addn(a, others, *, name, also_bf16=False):
    r, c = a.shape
    tr = _row_tile(r, c)
    n = len(others)

    def body(*refs):
        acc = refs[0][...].astype(F32)
        for o_ref in refs[1:1 + n]:
            acc = acc + o_ref[...].astype(F32)
        refs[1 + n][...] = acc
        if also_bf16:
            refs[2 + n][...] = acc.astype(BF16)

    spec = pl.BlockSpec((tr, c), lambda i: (i, 0))
    shapes = [jax.ShapeDtypeStruct((r, c), F32)] + ([jax.ShapeDtypeStruct((r, c), BF16)] if also_bf16 else [])
    out = pl.pallas_call(
        body, grid=(r // tr,), in_specs=[spec] * (1 + n), out_specs=[spec] * len(shapes), out_shape=shapes,
        compiler_params=_cp("parallel"), name=name)(a, *others)
    return out if also_bf16 else out[0]


def _sum_leading(a, *, name):
    n, r, c = a.shape

    def body(a_ref, o_ref):
        acc = a_ref[0]
        for i in range(1, n):
            acc = acc + a_ref[i]
        o_ref[...] = acc

    return pl.pallas_call(body, out_shape=jax.ShapeDtypeStruct((r, c), F32), name=name)(a)


def _adamw_math(wv, gv, mv, vv):
    m2 = ADAM_B1 * mv + (1.0 - ADAM_B1) * gv
    v2 = ADAM_B2 * vv + (1.0 - ADAM_B2) * (gv * gv)
    m_hat = m2 / (1.0 - ADAM_B1 ** ADAM_STEP)
    v_hat = v2 / (1.0 - ADAM_B2 ** ADAM_STEP)
    return -ADAM_LR * (m_hat / (jnp.sqrt(v_hat) + ADAM_EPS) + ADAM_WD * wv), m2, v2


def _adamw(w, g, m, v, *, name):
    r, c = w.shape
    tr = _row_tile(r, c, align=8)

    def body(w_ref, g_ref, m_ref, v_ref, d_ref, mo_ref, vo_ref):
        d_ref[...], mo_ref[...], vo_ref[...] = _adamw_math(w_ref[...], g_ref[...], m_ref[...], v_ref[...])

    spec = pl.BlockSpec((tr, c), lambda i: (i, 0))
    shp = jax.ShapeDtypeStruct((r, c), F32)
    return pl.pallas_call(
        body, grid=(r // tr,), in_specs=[spec] * 4, out_specs=[spec] * 3, out_shape=[shp] * 3,
        compiler_params=_cp("parallel"), name=name)(w, g, m, v)


def _adamw_halves(w, g_mine, g_other, m, v, c, *, name):
    r, cols = w.shape
    h = r // 2
    tr = _row_tile(h, cols, align=8)
    nh = h // tr

    def body(c_ref, w_ref, gm_ref, go_ref, m_ref, v_ref, g_ref, d_ref, mo_ref, vo_ref):
        gv = jnp.where(pl.program_id(0) // nh == c_ref[0], gm_ref[...], go_ref[...])
        g_ref[...] = gv
        d_ref[...], mo_ref[...], vo_ref[...] = _adamw_math(w_ref[...], gv, m_ref[...], v_ref[...])

    full = pl.BlockSpec((tr, cols), lambda i, c_: (i, 0))
    half = pl.BlockSpec((tr, cols), lambda i, c_: (i % nh, 0))
    shp = jax.ShapeDtypeStruct((r, cols), F32)
    return pl.pallas_call(
        body,
        grid_spec=pltpu.PrefetchScalarGridSpec(num_scalar_prefetch=1, grid=(2 * nh,),
                                               in_specs=[full, half, half, full, full], out_specs=[full] * 4),
        out_shape=[shp] * 4, compiler_params=_cp("parallel"), name=name,
    )(jnp.reshape(c, (1,)).astype(jnp.int32), w, g_mine, g_other, m, v)


def _flip(i, d):
    return 1 - i if d else i


def _comm(name, ins, out_shapes, n_remote, n_local, plan, aliases=None):
    n_in, n_out = len(ins), len(out_shapes)

    def body(*refs):
        in_refs, out_refs = refs[:n_in], refs[n_in:n_in + n_out]
        send_sems, recv_sems = refs[n_in + n_out], refs[n_in + n_out + 1]
        x, y, c = lax.axis_index("x"), lax.axis_index("y"), lax.axis_index("c")
        remote, local = plan(in_refs, out_refs, x, y, c)
        assert len(remote) == n_remote and len(local) == n_local
        copies = []
        if n_local:
            loc_sems = refs[n_in + n_out + 2]
            copies += [pltpu.make_async_copy(s_, d_, loc_sems.at[i]) for i, (s_, d_) in enumerate(local)]
        copies += [pltpu.make_async_remote_copy(src_ref=s_, dst_ref=d_, send_sem=send_sems.at[i],
                                                recv_sem=recv_sems.at[i], device_id=dev, device_id_type=MESH)
                   for i, (s_, d_, dev) in enumerate(remote)]
        for cp in copies:
            cp.start()
        for cp in copies:
            cp.wait()

    hbm = pl.BlockSpec(memory_space=pl.ANY)
    scratch = [pltpu.SemaphoreType.DMA((n_remote,)), pltpu.SemaphoreType.DMA((n_remote,))]
    if n_local:
        scratch.append(pltpu.SemaphoreType.DMA((n_local,)))
    return pl.pallas_call(
        body, in_specs=[hbm] * n_in, out_specs=[hbm] * n_out, out_shape=out_shapes, scratch_shapes=scratch,
        input_output_aliases=aliases or {}, compiler_params=pltpu.CompilerParams(has_side_effects=True),
        name=name)(*ins)


HBM_SPEC = pl.BlockSpec(memory_space=pltpu.HBM)
SEM_SPEC = pl.BlockSpec(memory_space=pltpu.SEMAPHORE)
DATAFLOW = pltpu.SideEffectType.DATAFLOW_SIDE_EFFECTING


def _remote_copies(plan, srcs, lands, send_sems, recv_sems, n_copies):
    x, y, c = lax.axis_index("x"), lax.axis_index("y"), lax.axis_index("c")
    copies = plan(srcs, lands, x, y, c)
    assert len(copies) == n_copies
    return [pltpu.make_async_remote_copy(src_ref=s_, dst_ref=d_, send_sem=send_sems.at[i], recv_sem=recv_sems.at[i],
                                         device_id=dev, device_id_type=MESH) for i, (s_, d_, dev) in enumerate(copies)]


def _split_start(name, srcs, lands, n_copies, plan, after=None):
    ns, nb = len(srcs), len(srcs) + len(lands)
    n_after = 0 if after is None else 1
    n_in = nb + n_after

    def body(*refs):
        for cp in _remote_copies(plan, refs[:ns], refs[ns:nb], refs[n_in], refs[n_in + 1], n_copies):
            cp.start()
        refs[-1][...] = jnp.zeros_like(refs[-1])

    arrays = [pltpu.with_memory_space_constraint(a_, pltpu.HBM) for a_ in list(srcs) + list(lands)]
    out = pl.pallas_call(
        body, name=name,
        out_shape=(pltpu.SemaphoreType.DMA((n_copies,)), pltpu.SemaphoreType.DMA((n_copies,)),
                   *[pltpu.HBM(a_.shape, a_.dtype) for a_ in arrays], jax.ShapeDtypeStruct((8, 128), F32)),
        in_specs=[HBM_SPEC] * nb + [pl.BlockSpec(memory_space=pl.ANY)] * n_after,
        out_specs=(SEM_SPEC, SEM_SPEC, *[HBM_SPEC] * nb, pl.BlockSpec(memory_space=pltpu.VMEM)),
        input_output_aliases={i: 2 + i for i in range(nb)},
        compiler_params=pltpu.CompilerParams(has_side_effects=DATAFLOW))(*arrays, *([after] * n_after))
    return (out[0], out[1], list(out[2:2 + nb])), out[-1]


def _split_wait(name, handle, ns, n_copies, plan, after):
    send_sems, recv_sems, bufs = handle
    nb = len(bufs)

    def body(*refs):
        for cp in _remote_copies(plan, refs[:ns], refs[ns:nb], refs[nb], refs[nb + 1], n_copies):
            cp.wait_send()
            cp.wait_recv()

    out = pl.pallas_call(
        body, name=name, out_shape=[pltpu.HBM(b_.shape, b_.dtype) for b_ in bufs],
        in_specs=[HBM_SPEC] * nb + [SEM_SPEC, SEM_SPEC, pl.BlockSpec(memory_space=pl.ANY)],
        out_specs=[HBM_SPEC] * nb, input_output_aliases={i: i for i in range(nb)},
        compiler_params=pltpu.CompilerParams(has_side_effects=DATAFLOW))(*bufs, send_sems, recv_sems, after)
    return list(out[ns:])


def _gather_start(shards, tag, after=None):
    n = len(shards)
    lands = [lax.empty((4,) + s.shape, s.dtype) for s in shards]

    def plan(srcs, dsts, x, y, c):
        k = 2 * x + y
        copies = []
        for w_ref, o_ref in zip(srcs, dsts):
            h = w_ref.shape[0] // 2
            rows = pl.ds(c * h, h)
            copies += [(w_ref.at[rows], o_ref.at[k, rows], (_flip(x, dx), _flip(y, dy), c)) for dx, dy in CHIP_FLIPS]
        return copies

    handle, token = _split_start(f"gather_{tag}_start", shards, lands, 3 * n, plan, after)
    return (handle, plan, n), token


def _gather_wait(started, after, tag):
    handle, plan, n = started
    return _split_wait(f"gather_{tag}_wait", handle, n, 3 * n, plan, after)


def _gather_d2d(lands, before, tag):
    n = len(lands)

    def plan_d2d(in_refs, out_refs, x, y, c):
        remote = []
        for o_ref in out_refs:
            h = o_ref.shape[1] // 2
            for dx, dy in CHIP_FLIPS:
                half = o_ref.at[2 * _flip(x, dx) + _flip(y, dy), pl.ds(c * h, h)]
                remote.append((half, half, (x, y, 1 - c)))
        return remote, []

    return _comm(f"gather_{tag}_d2d", list(lands) + list(before),
                 [jax.ShapeDtypeStruct(l_.shape, l_.dtype) for l_ in lands], 3 * n, 0, plan_d2d,
                 aliases={i: i for i in range(n)})


def _pair_plan(in_refs, out_refs, x, y, c):
    return [(i_, o_, (x, y, 1 - c)) for i_, o_ in zip(in_refs, out_refs)], []


def _rs_start(grads, tag):
    n = len(grads)
    c = lax.axis_index("c")
    def rows(g, start, h):
        return g.rows(start, h) if isinstance(g, _WInGrad) else lax.dynamic_slice_in_dim(g, start, h, axis=1)

    halves = [g.shape[1] // 2 for g in grads]
    mine = [rows(g, c * h, h) for g, h in zip(grads, halves)]
    send_a = [rows(g, (1 - c) * h, h).astype(BF16) for g, h in zip(grads, halves)]
    recv_a = _comm(f"rs_pair_{tag}", send_a, [jax.ShapeDtypeStruct(s.shape, BF16) for s in send_a], n, 0, _pair_plan)
    pair, pair_b = [], []
    for i, (mi, ra) in enumerate(zip(mine, recv_a)):
        four, h, cols = mi.shape
        p32, p16 = _addn(mi.reshape(four * h, cols), [ra.reshape(four * h, cols)], name=f"rs_pair_sum_{tag}_{i}",
                         also_bf16=True)
        pair.append(p32.reshape(four, h, cols))
        pair_b.append(p16.reshape(four, h, cols))

    def plan(srcs, dsts, x, y, c_):
        copies = []
        for i_, o_ in zip(srcs, dsts):
            for j, (dx, dy) in enumerate(CHIP_FLIPS):
                fx, fy = _flip(x, dx), _flip(y, dy)
                copies.append((i_.at[2 * fx + fy], o_.at[j], (fx, fy, c_)))
        return copies

    lands = [lax.empty((3,) + p.shape[1:], BF16) for p in pair_b]
    handle, token = _split_start(f"rs_chips_{tag}_start", pair_b, lands, 3 * n, plan)
    return (handle, plan, n, pair), token


def _rs_finish(started, after, tag):
    handle, plan, n, pair = started
    recv_b = _split_wait(f"rs_chips_{tag}_wait", handle, n, 3 * n, plan, after)
    k = 2 * lax.axis_index("x") + lax.axis_index("y")
    tot = [_addn(lax.dynamic_index_in_dim(p, k, 0, keepdims=False), [rb[0], rb[1], rb[2]],
                 name=f"rs_chip_sum_{tag}_{i}") for i, (p, rb) in enumerate(zip(pair, recv_b))]
    other = _comm(f"rs_halves_{tag}", tot, [jax.ShapeDtypeStruct(t.shape, F32) for t in tot], n, 0, _pair_plan)
    return tot, other


def _gather_all(vec, *, name, before=()):
    out = jax.ShapeDtypeStruct((8,) + vec.shape, vec.dtype)

    def plan(in_refs, out_refs, x, y, c):
        me = 4 * x + 2 * y + c
        remote = [(in_refs[0], out_refs[0].at[me], (_flip(x, dx), _flip(y, dy), _flip(c, dc)))
                  for dx in (0, 1) for dy in (0, 1) for dc in (0, 1) if (dx, dy, dc) != (0, 0, 0)]
        return remote, [(in_refs[0], out_refs[0].at[me])]

    return _comm(name, [vec] + list(before), [out], 7, 1, plan)[0]


def _pack(parts):
    flat = [p.reshape(-1).astype(F32) for p in parts]
    total = sum(f.shape[0] for f in flat)
    n = -(-total // 1024) * 128
    vec = jnp.concatenate(flat + [jnp.zeros((8 * n - total,), F32)]).reshape(8, n)
    offs, o = [], 0
    for f in flat:
        offs.append((o, f.shape[0]))
        o += f.shape[0]
    return vec, offs


def _unpack(vec, offs, shapes):
    flat = vec.reshape(-1)
    return [flat[o:o + n].reshape(s) for (o, n), s in zip(offs, shapes)]


BIG = (("ffn1_w_gate_up", "col"), ("ffn1_w_down", "row"), ("w_in", "col"), ("w_out_a", "row"), ("w_out_ssm", "row"),
       ("w_mix_out", "row"), ("w_q", "row"), ("w_kv", "col"), ("w_o_x", "row"), ("ffn2_w_gate_up", "col"),
       ("ffn2_w_down", "row"))
SMALL = ("ffn1_norm", "mix_norm", "conv_a_w", "ssm_conv_w", "ssm_conv_b", "ssm_dt_bias", "ssm_a_log", "ssm_d",
         "ssm_norm", "xattn_norm", "mem_norm", "ffn2_norm", "final_norm")
WEIGHTS = ("ffn1_norm", "ffn1_w_gate_up", "ffn1_w_down", "mix_norm", "w_in", "conv_a_w", "w_out_a", "ssm_conv_w",
           "ssm_conv_b", "ssm_dt_bias", "ssm_a_log", "ssm_d", "ssm_norm", "w_out_ssm", "w_mix_out", "xattn_norm",
           "mem_norm", "w_q", "w_kv", "w_o_x", "ffn2_norm", "ffn2_w_gate_up", "ffn2_w_down", "final_norm")


GATHER_GROUPS = (("a", ("ffn1_w_gate_up",)), ("b", ("ffn1_w_down", "w_in")),
                 ("c", ("w_out_a", "w_out_ssm", "w_mix_out", "w_q", "w_kv", "w_o_x", "ffn2_w_gate_up", "ffn2_w_down")))


def _place_own(land, own, k, *, name):
    four, r, cols = land.shape
    tr = _row_tile(r, cols)

    def body(k_ref, own_ref, land_in, o_ref):
        del k_ref, land_in
        o_ref[...] = own_ref[...]

    return pl.pallas_call(
        body,
        grid_spec=pltpu.PrefetchScalarGridSpec(
            num_scalar_prefetch=1, grid=(r // tr,),
            in_specs=[pl.BlockSpec((tr, cols), lambda i, k_: (i, 0)), pl.BlockSpec(memory_space=pl.ANY)],
            out_specs=pl.BlockSpec((None, tr, cols), lambda i, k_: (k_[0], i, 0))),
        out_shape=jax.ShapeDtypeStruct(land.shape, land.dtype), input_output_aliases={2: 0},
        compiler_params=_cp("parallel"), name=name)(jnp.reshape(k, (1,)).astype(jnp.int32), own, land)


def _full_weight(land, own, kind, k, *, name):
    land = _place_own(land, own, k, name=name)
    four, r, cols = land.shape
    if kind == "row":
        return land.reshape(four * r, cols)
    return jnp.transpose(land, (1, 0, 2)).reshape(r, four * cols)


class _GatheredWeights:
    def __init__(self, shards32, k, after):
        first = GATHER_GROUPS[0][1]
        self.shards, self.k = {n: shards32[n].astype(BF16) for n in first}, k
        self.full = {}
        self.n_done = 0
        self.started, token = self._start(0, after)
        self.token = token[0, 0]
        self.shards.update({n: (w + token[0, 0]).astype(BF16) for n, w in shards32.items() if n not in first})
        self.after = jnp.stack([self.shards[n][0, 0] for n in shards32 if n not in first]).astype(F32).reshape(1, -1)

    def _start(self, gi, after):
        tag, names = GATHER_GROUPS[gi]
        return _gather_start([self.shards[n] for n in names], tag, after)

    def mark(self, value):
        self.after = value

    def __getitem__(self, name):
        if name not in self.full:
            tag, names = GATHER_GROUPS[self.n_done]
            assert name in names, (name, tag)
            lands = _gather_wait(self.started, self.after, tag)
            before = []
            if self.n_done + 1 < len(GATHER_GROUPS):
                self.started, token = self._start(self.n_done + 1, lands[0])
                before = [token]
            lands = _gather_d2d(lands, before, tag)
            for n, land in zip(names, lands):
                if n == "w_in":
                    self.full[n] = _pad_w_in_shards(_place_own(land, self.shards[n], self.k, name=f"own_{n}"))
                else:
                    self.full[n] = _full_weight(land, self.shards[n], dict(BIG)[n], self.k, name=f"own_{n}")
            self.n_done += 1
        return self.full[name]


def _shard_major(dw, kind):
    if isinstance(dw, tuple):
        return jnp.concatenate(dw, axis=0)
    if dw.ndim == 3:
        return dw
    if kind == "row":
        return dw.reshape(4, dw.shape[0] // 4, dw.shape[1])
    return jnp.transpose(dw.reshape(dw.shape[0], 4, dw.shape[1] // 4), (1, 0, 2))


def _pad_rows8(w):
    return jnp.concatenate([w, jnp.zeros((8 - w.shape[0], w.shape[1]), w.dtype)], axis=0)


def _group_lanes(v):
    r = v.shape[0]
    return jnp.pad(v.reshape(r, NG, NH // NG), ((0, 0), (0, 0), (0, 128 - NH // NG))).reshape(r, NG * 128)


def _ungroup_lanes(v):
    r = v.shape[0]
    return v.reshape(r, NG, 128)[:, :, :NH // NG].reshape(r, NH)


def _local_step(wfull, small, x, mem, target, token=0.0, on_grads=None):
    bl, s, _ = x.shape
    T = bl * s
    x2, t2 = x.reshape(T, D), target.reshape(T, D)
    mem2 = mem.reshape(-1, D)
    g = {}
    tok = [token]
    mark = getattr(wfull, "mark", lambda value: None)

    def gain(name):
        return small[name].reshape(1, -1) + tok[0]

    def emit(tag, names):
        if on_grads is not None:
            tok[0] = tok[0] + on_grads(tag, {n: g[n] for n in names})

    def ffn_fwd(h, n, wgu, wd, tag, next_gain=None):
        gate, up, a = _gate_up_fwd(n, wfull[wgu], name=f"{tag}_gate_up")
        mark(a)
        out = _mm(a, wfull[wd], "nn", tk=DFF, scale=FFN_RES, residual=h, norm_gain=next_gain, name=f"{tag}_down")
        return out, (n, gate, up, a)

    def ffn_bwd(dh, h, norm, wgu, wd, saved, tag):
        n, gate, up, a = saved
        dgate, dup = _act_bwd(dh, wfull[wd], gate, up, FFN_RES, name=f"{tag}_d_act")
        g[wd] = _mm(a, dh, "tn", tm=1408, scale=FFN_RES, name=f"{tag}_d_w_down")
        g[wgu] = (_mm(n, dgate, "tn", tn=1408, col_shards=2, name=f"{tag}_d_w_gate"),
                  _mm(n, dup, "tn", tn=1408, col_shards=2, name=f"{tag}_d_w_up"))
        emit(tag, (wgu, wd))
        dn = _mm(dgate, wfull[wgu], "nt", a2=dup, tm=512, tk=DFF, name=f"{tag}_d_norm_out")
        dh_in, g[norm] = _norm_bwd(h, gain(norm), dn, dh, name=f"{tag}_d_norm")
        return dh_in

    n1 = _norm_fwd(x2, gain("ffn1_norm"), name="ffn1_norm")
    (h1, u), ffn1_saved = ffn_fwd(x2, n1, "ffn1_w_gate_up", "ffn1_w_down", "ffn1", gain("mix_norm"))
    mark(h1)
    pp = _mm(u, wfull["w_in"], "nn", tm=2048, tn=1152, name="in_proj")
    wa8 = _pad_rows8(small["conv_a_w"])
    ws8 = _pad_rows8(small["ssm_conv_w"])
    conv_b = gain("ssm_conv_b")
    bias128 = jnp.pad(gain("ssm_dt_bias"), ((0, 0), (0, 128 - NH)))
    ya_pre = _conv_a_fwd(pp, wa8, bl, s, name="conv_a")
    xc = _conv_ssm_fwd(pp, ws8, conv_b, bl, s, name="conv_ssm")
    mark(xc)
    alog = gain("ssm_a_log")
    dtg, dtt = _dt_fwd(pp, bias128, name="dt")
    arow, acol = _group_lanes(alog), alog.reshape(NG, NH // NG, 1)
    dexp = jnp.repeat(gain("ssm_d"), HD, axis=1)
    ng = gain("ssm_norm")
    y, yn, prev = _ssd_fwd(xc, pp, dtg, dtt, arow, acol, dexp, ng, bl, s, name="ssd")
    ya = _mm(ya_pre, wfull["w_out_a"], "nn", tn=1024, out_dtype=BF16, name="out_a")
    yb = _mm(yn, wfull["w_out_ssm"], "nn", tn=1024, tk=DI, out_dtype=BF16, name="out_ssm")
    merged = _merge_fwd(pp, ya, yb, name="merge")
    h2, un = _mm(merged, wfull["w_mix_out"], "nn", residual=h1, norm_gain=gain("xattn_norm"), name="mix_out")
    q = _mm(un, wfull["w_q"], "nn", tn=1024, out_dtype=BF16, name="q_proj")
    mn = _norm_fwd(mem2, gain("mem_norm"), name="mem_norm")
    kv = _mm(mn, wfull["w_kv"], "nn", tn=1024, out_dtype=BF16, name="kv_proj")
    o = _attn_fwd(q, kv, bl, s, name="attn")
    h3, n2 = _mm(o, wfull["w_o_x"], "nn", residual=h2, norm_gain=gain("ffn2_norm"), name="attn_out")
    h4, ffn2_saved = ffn_fwd(h3, n2, "ffn2_w_gate_up", "ffn2_w_down", "ffn2")
    sq_err, dh4, dgf = _final_loss(h4, gain("final_norm"), t2, name="final_loss")
    g["final_norm"] = dgf

    dh3 = ffn_bwd(dh4, h3, "ffn2_norm", "ffn2_w_gate_up", "ffn2_w_down", ffn2_saved, "ffn2")
    do = _mm(dh3, wfull["w_o_x"], "nt", tn=1024, out_dtype=BF16, name="d_attn_o")
    g["w_o_x"] = _mm(o, dh3, "tn",name="d_w_o_x")
    dq, dk, dv = _attn_bwd(q, kv, do, bl, s, name="d_attn")
    dun = _mm(dq, wfull["w_q"], "nt", tn=1024, name="d_xattn_norm_out")
    g["w_q"] = _mm(un, dq, "tn",name="d_w_q")
    dkv = jnp.concatenate([dk, dv], axis=1)
    dmn = _mm(dkv, wfull["w_kv"], "nt", tn=1024, tk=2 * D, name="d_mem_norm_out")
    g["w_kv"] = _mm(mn, dkv, "tn", tn=512, col_shards=4, name="d_w_kv")
    emit("attn", ("w_q", "w_kv", "w_o_x"))
    _, g["mem_norm"] = _norm_bwd(mem2, gain("mem_norm"), dmn, None, name="d_mem_norm")
    dh2, g["xattn_norm"] = _norm_bwd(h2, gain("xattn_norm"), dun, dh3, name="d_xattn_norm")
    dmerged = _mm(dh2, wfull["w_mix_out"], "nt", tn=1024, name="d_merged")
    g["w_mix_out"] = _mm(merged, dh2, "tn",name="d_w_mix_out")
    dya, dyb, dpp = _merge_bwd(pp, ya, yb, dmerged, name="d_merge")
    dya_pre = _mm(dya, wfull["w_out_a"], "nt", tn=1024, name="d_conv_a_out")
    g["w_out_a"] = _mm(ya_pre, dya, "tn",name="d_w_out_a")
    dyn = _mm(dyb, wfull["w_out_ssm"], "nt", tn=DI, name="d_ssd_out")
    g["w_out_ssm"] = _mm(yn, dyb, "tn",name="d_w_out_ssm")
    dpp, dwa8 = _conv_a_bwd(pp, wa8, dya_pre, dpp, bl, s, name="d_conv_a")
    g["conv_a_w"] = dwa8[:3]
    dpp, dxs, dbm, dcm, ddtg, g["ssm_norm"], ddexp, dalg = _ssd_bwd(
        dyn, y, xc, pp, dtg, dtt, arow, acol, dexp, ng, prev, dpp, bl, s, name="d_ssd")
    g["ssm_d"] = ddexp.reshape(NH, HD).sum(axis=1).reshape(1, NH)
    g["ssm_a_log"] = _ungroup_lanes(dalg)
    conv_dw, conv_db = [], []
    for dpart, off, tag in ((dxs, 0, "x"), (dbm, DI, "b"), (dcm, DI + NG * NS, "c")):
        dpp, dw_, db_ = _conv_ssm_bwd(pp, ws8, conv_b, dpart, off, dpp, bl, s, name=f"d_conv_ssm_{tag}")
        conv_dw.append(dw_)
        conv_db.append(db_)
    g["ssm_conv_w"] = jnp.concatenate(conv_dw, axis=1)[:4]
    g["ssm_conv_b"] = jnp.concatenate(conv_db, axis=1)
    dpp, dbias = _dt_bwd(pp, bias128, ddtg, dpp, name="d_dt")
    g["ssm_dt_bias"] = dbias[:, :NH]
    g["w_in"] = _mm(u, dpp, "tn", tn=1152, name="d_w_in")
    emit("mix", ("w_in", "w_out_a", "w_out_ssm", "w_mix_out"))
    du = _mm(dpp, wfull["w_in"], "nt", tk=3456, name="d_mix_norm_out")
    dh1, g["mix_norm"] = _norm_bwd(h1, gain("mix_norm"), du, dh2, name="d_mix_norm")
    dx = ffn_bwd(dh1, x2, "ffn1_norm", "ffn1_w_gate_up", "ffn1_w_down", ffn1_saved, "ffn1")
    return sq_err, dx, g


W_IN_SHARD = NIN // 4


def _w_in_segments():
    segs, p = [], 0
    for t in range(D // CA_TILE):
        for which in range(3):
            segs.append((D * which + CA_TILE * t, p, CA_TILE))
            p += CA_TILE
    for s, n in ((3 * D, O_GA - 3 * D), (O_GA + NH, 2 * D), (O_GA, NH)):
        segs.append((s, p, n))
        p += n
    assert p == NIN and segs[-1][1] == O_DT and segs[-2][1] == O_GA
    return segs


def _pad_w_in_shards(land):
    pieces = []
    for s, _, n in _w_in_segments():
        while n > 0:
            kk, off = divmod(s, W_IN_SHARD)
            take = min(n, W_IN_SHARD - off)
            pieces.append(land[kk][:, off:off + take])
            s, n = s + take, n - take
    return jnp.concatenate(pieces + [jnp.zeros((land.shape[1], NPP - NIN), land.dtype)], axis=1)


class _WInGrad:
    def __init__(self, dwp):
        self.dwp = dwp
        self.shape = (4, dwp.shape[0], W_IN_SHARD)

    def rows(self, start, n):
        part = lax.dynamic_slice_in_dim(self.dwp, start, n, axis=0)
        shards = []
        for kk in range(4):
            n0, n1 = W_IN_SHARD * kk, W_IN_SHARD * (kk + 1)
            cuts = sorted((max(s, n0), p + max(s, n0) - s, min(s + m, n1) - max(s, n0))
                          for s, p, m in _w_in_segments() if min(s + m, n1) > max(s, n0))
            shards.append(jnp.concatenate([part[:, p:p + m] for _, p, m in cuts], axis=1))
        return jnp.stack(shards)


def _pad_w_in(w):
    return _pad_w_in_shards(jnp.stack(jnp.split(w, 4, axis=1)))


def _unpad_w_in(w):
    return jnp.concatenate(list(_WInGrad(w).rows(0, w.shape[0])), axis=1)


def kernel(x, mem, ffn1_norm, ffn1_w_gate_up, ffn1_w_down, mix_norm, w_in, conv_a_w, w_out_a, ssm_conv_w, ssm_conv_b, ssm_dt_bias, ssm_a_log, ssm_d, ssm_norm, w_out_ssm, w_mix_out, xattn_norm, mem_norm, w_q, w_kv, w_o_x, ffn2_norm, ffn2_w_gate_up, ffn2_w_down, final_norm, loss_target, m_ffn1_norm, m_ffn1_w_gate_up, m_ffn1_w_down, m_mix_norm, m_w_in, m_conv_a_w, m_w_out_a, m_ssm_conv_w, m_ssm_conv_b, m_ssm_dt_bias, m_ssm_a_log, m_ssm_d, m_ssm_norm, m_w_out_ssm, m_w_mix_out, m_xattn_norm, m_mem_norm, m_w_q, m_w_kv, m_w_o_x, m_ffn2_norm, m_ffn2_w_gate_up, m_ffn2_w_down, m_final_norm, v_ffn1_norm, v_ffn1_w_gate_up, v_ffn1_w_down, v_mix_norm, v_w_in, v_conv_a_w, v_w_out_a, v_ssm_conv_w, v_ssm_conv_b, v_ssm_dt_bias, v_ssm_a_log, v_ssm_d, v_ssm_norm, v_w_out_ssm, v_w_mix_out, v_xattn_norm, v_mem_norm, v_w_q, v_w_kv, v_w_o_x, v_ffn2_norm, v_ffn2_w_gate_up, v_ffn2_w_down, v_final_norm):
    a = dict(locals())
    xi, yi = lax.axis_index("x"), lax.axis_index("y")
    k = 2 * xi + yi

    conv_vec, conv_offs = _pack([a["conv_a_w"], a["ssm_conv_w"]])
    conv_all = _gather_all(conv_vec, name="gather_conv_w")
    wfull = _GatheredWeights({n: a[n][0] for n, _ in BIG}, k, conv_all)
    conv_sh = [_unpack(conv_all[2 * kk], conv_offs, [a["conv_a_w"].shape[1:], a["ssm_conv_w"].shape[1:]])
               for kk in range(4)]
    small = {n: a[n] for n in SMALL}
    small["conv_a_w"] = jnp.concatenate([cs[0] for cs in conv_sh], axis=1)
    small["ssm_conv_w"] = jnp.concatenate([cs[1] for cs in conv_sh], axis=1)

    rs_started = []

    def on_grads(tag, grads):
        names = [n for n, _ in BIG if n in grads]
        shard_major = [_WInGrad(grads[n]) if n == "w_in" else _shard_major(grads[n], dict(BIG)[n]) for n in names]
        st, tk = _rs_start(shard_major, tag)
        rs_started.append((tag, names, st))
        return tk[0, 0]

    sq_err, dx, g = _local_step(wfull, small, x, mem, loss_target, wfull.token, on_grads)
    loss = lax.psum(0.5 / D * jnp.sum(sq_err), ("x", "y", "c"))

    ci = lax.axis_index("c")
    out = {}

    def finish(tag, names, st, after):
        g_mine, g_other = _rs_finish(st, after, tag)
        for n, gm, go in zip(names, g_mine, g_other):
            res = _adamw_halves(a[n][0], gm, go, a["m_" + n][0], a["v_" + n][0], ci, name=f"adamw_{n}")
            out[n] = tuple(t.reshape(a[n].shape) for t in res)
        return res[1]

    done = dx
    for grp in rs_started[:-1]:
        done = finish(*grp, dx)

    full_shapes = [g[n].shape for n in SMALL]
    gvec, goffs = _pack([g[n] for n in SMALL])
    gsum = _sum_leading(_gather_all(gvec, name="gather_small_grads", before=[done]), name="sum_small_grads")
    finish(*rs_started[-1], gsum)
    gsmall = dict(zip(SMALL, _unpack(gsum, goffs, full_shapes)))
    for n in ("conv_a_w", "ssm_conv_w"):
        width = a[n].shape[2]
        gsmall[n] = lax.dynamic_slice_in_dim(gsmall[n], k * width, width, axis=1)
    local_shapes = [a[n].shape for n in SMALL]
    packs = [_pack([t[n] for n in SMALL]) for t in
             ({n: a[n] for n in SMALL}, gsmall, {n: a["m_" + n] for n in SMALL}, {n: a["v_" + n] for n in SMALL})]
    offs = packs[0][1]
    res = _adamw(*[p[0] for p in packs], name="adamw_small")
    unp = [_unpack(r, offs, local_shapes) for r in res]
    for i, n in enumerate(SMALL):
        out[n] = (gsmall[n].reshape(a[n].shape), unp[0][i], unp[1][i], unp[2][i])

    grad_x = dx.reshape(x.shape)
    return (loss, grad_x, *[out[n][0] for n in WEIGHTS], *[out[n][1] for n in WEIGHTS],
            *[out[n][2] for n in WEIGHTS], *[out[n][3] for n in WEIGHTS])
```

```python
import functools
import math

import jax
import jax.numpy as jnp
from jax import lax
from jax.experimental import pallas as pl
from jax.experimental.pallas import tpu as pltpu

F32 = jnp.float32
BF16 = jnp.bfloat16
MXU = jnp.bfloat16
HI = lax.Precision.HIGHEST

D = 1024
DFF = 2816
DI = 2048
NH, HD, NG, NS, CH = 32, 64, 4, 128, 128
GW = DI // NG
XH, XD = 4, 256
EPS = 1e-6
NEG = -1e30
CA_TILE = 256
O_CA, O_Z, O_XBC, O_GA, O_GB, O_DT, NPP = 0, 3072, 5120, 8192, 9216, 10240, 10368
NIN = 10272
FFN_RES = 0.5
ADAM_LR, ADAM_B1, ADAM_B2, ADAM_EPS, ADAM_WD, ADAM_STEP = 0.001, 0.9, 0.999, 1e-08, 0.01, 10
VMEM_LIMIT = 56 * 1024 * 1024
EPI_COLS = 256
SSD_EX = 4
MESH = pl.DeviceIdType.MESH
CHIP_FLIPS = ((1, 0), (0, 1), (1, 1))


def _cp(*sem):
    return pltpu.CompilerParams(dimension_semantics=sem, vmem_limit_bytes=VMEM_LIMIT)


def _tile(n, pref, align=128):
    if n <= pref:
        return n
    t = (pref // align) * align
    while t >= align:
        if n % t == 0:
            return t
        t -= align
    raise ValueError((n, pref))


def _dot(a, b, dims, prec=None):
    return lax.dot_general(a, b, (dims, ((), ())), preferred_element_type=F32, precision=prec)


def _nn(a, b, prec=None):
    return _dot(a, b, ((1,), (0,)), prec)


def _nt(a, b):
    return _dot(a, b, ((1,), (1,)))


def _tn(a, b):
    return _dot(a, b, ((0,), (0,)))


def _sig(x):
    return jax.nn.sigmoid(x)


def _mm(a, b, mode, *, name, tm=1024, tn=1024, tk=None, out_dtype=F32, scale=None, residual=None, a2=None,
        col_shards=0, norm_gain=None):
    if tk is None:
        tk = 2048 if mode == "tn" else 1024
    if mode == "nn":
        (M, K), (K2, N) = a.shape, b.shape
    elif mode == "nt":
        (M, K), (N, K2) = a.shape, b.shape
        if a2 is not None:
            assert a2.shape == a.shape
            K2 = K2 // 2
    else:
        (K, M), (K2, N) = a.shape, b.shape
    assert K == K2, (name, a.shape, b.shape)
    tm, tn, tk = _tile(M, tm), _tile(N, tn), _tile(K, tk)
    nk = K // tk
    if mode == "nn":
        a_spec = pl.BlockSpec((tm, tk), lambda i, j, k: (i, k))
        b_spec = pl.BlockSpec((tk, tn), lambda i, j, k: (k, j))
        dims = ((1,), (0,))
    elif mode == "nt":
        a_spec = pl.BlockSpec((tm, tk), lambda i, j, k: (i, k))
        b_spec = pl.BlockSpec((tn, tk), lambda i, j, k: (j, k))
        dims = ((1,), (1,))
    else:
        a_spec = pl.BlockSpec((tk, tm), lambda i, j, k: (k, i))
        b_spec = pl.BlockSpec((tk, tn), lambda i, j, k: (k, j))
        dims = ((0,), (0,))
    o_spec = pl.BlockSpec((tm, tn), lambda i, j, k: (i, j))
    out_spec, out_shape = o_spec, jax.ShapeDtypeStruct((M, N), out_dtype)
    if col_shards:
        per = N // col_shards // tn
        assert per * tn * col_shards == N, (name, N, tn, col_shards)
        out_spec = pl.BlockSpec((None, tm, tn), lambda i, j, k: (j // per, i, j % per))
        out_shape = jax.ShapeDtypeStruct((col_shards, M, N // col_shards), out_dtype)
    has_res = residual is not None
    has_norm = norm_gain is not None
    assert not has_norm or (tn == N and not col_shards)
    dual = a2 is not None
    n_in = 2 + 2 * dual + has_res + has_norm

    def body(*refs):
        a_ref, b_ref = refs[0], refs[1]
        o_ref = refs[n_in]

        def finish(acc):
            if scale is not None:
                acc = acc * scale
            if has_res:
                acc = acc + refs[2 + 2 * dual][...]
            o_ref[...] = acc.astype(out_dtype)
            if has_norm:
                rs = lax.rsqrt(jnp.mean(acc * acc, axis=-1, keepdims=True) + EPS)
                refs[n_in + 1][...] = (acc * rs * refs[n_in - 1][...]).astype(BF16)

        part = _dot(a_ref[...].astype(MXU), b_ref[...].astype(MXU), dims)
        if dual:
            part = part + _dot(refs[2][...].astype(MXU), refs[3][...].astype(MXU), dims)
        if nk == 1:
            finish(part)
            return
        acc_ref = refs[-1]
        k = pl.program_id(2)

        @pl.when(k == 0)
        def _():
            acc_ref[...] = part

        @pl.when(k > 0)
        def _():
            acc_ref[...] += part

        @pl.when(k == nk - 1)
        def _():
            finish(acc_ref[...])

    ins, in_specs = [a, b], [a_spec, b_spec]
    if dual:
        ins += [a2, b]
        in_specs += [a_spec, pl.BlockSpec((tn, tk), lambda i, j, k: (j, k + nk))]
    if has_res:
        ins.append(residual)
        in_specs.append(o_spec)
    if has_norm:
        ins.append(norm_gain)
        in_specs.append(pl.BlockSpec((1, tn), lambda i, j, k: (0, j)))
        out_spec, out_shape = [out_spec, o_spec], [out_shape, jax.ShapeDtypeStruct((M, N), BF16)]
    return pl.pallas_call(
        body, grid=(M // tm, N // tn, nk), in_specs=in_specs, out_specs=out_spec, out_shape=out_shape,
        scratch_shapes=[pltpu.VMEM((tm, tn), F32)] if nk > 1 else [],
        compiler_params=_cp("parallel", "parallel", "arbitrary"), name=name)(*ins)


def _norm_fwd(x, g, *, name):
    T, d = x.shape
    tr = _tile(T, 512, 8)

    def body(x_ref, g_ref, o_ref):
        xv = x_ref[...]
        r = lax.rsqrt(jnp.mean(xv * xv, axis=-1, keepdims=True) + EPS)
        o_ref[...] = (xv * r * g_ref[...]).astype(BF16)

    return pl.pallas_call(
        body, grid=(T // tr,),
        in_specs=[pl.BlockSpec((tr, d), lambda i: (i, 0)), pl.BlockSpec((1, d), lambda i: (0, 0))],
        out_specs=pl.BlockSpec((tr, d), lambda i: (i, 0)),
        out_shape=jax.ShapeDtypeStruct((T, d), BF16), compiler_params=_cp("parallel"), name=name)(x, g)


def _norm_bwd(x, g, dn, dres, *, name):
    T, d = x.shape
    tr = _tile(T, 512, 8)
    has_res = dres is not None

    def body(*refs):
        x_ref, g_ref, dn_ref = refs[:3]
        dr_ref = refs[3] if has_res else None
        dx_ref, dg_ref = refs[-2], refs[-1]

        @pl.when(pl.program_id(0) == 0)
        def _():
            dg_ref[...] = jnp.zeros_like(dg_ref)

        xv = x_ref[...]
        dnv = dn_ref[...].astype(F32)
        r = lax.rsqrt(jnp.mean(xv * xv, axis=-1, keepdims=True) + EPS)
        xh = xv * r
        dg_ref[...] += jnp.sum(dnv * xh, axis=0, keepdims=True)
        dxh = dnv * g_ref[...]
        dx = r * (dxh - xh * jnp.mean(dxh * xh, axis=-1, keepdims=True))
        if has_res:
            dx = dx + dr_ref[...]
        dx_ref[...] = dx

    row = pl.BlockSpec((tr, d), lambda i: (i, 0))
    vec = pl.BlockSpec((1, d), lambda i: (0, 0))
    ins = [x, g, dn] + ([dres] if has_res else [])
    return pl.pallas_call(
        body, grid=(T // tr,), in_specs=[row, vec, row] + ([row] if has_res else []),
        out_specs=[row, vec],
        out_shape=[jax.ShapeDtypeStruct((T, d), F32), jax.ShapeDtypeStruct((1, d), F32)],
        compiler_params=_cp("arbitrary"), name=name)(*ins)


def _final_loss(h, g, target, *, name):
    T, d = h.shape
    tr = _tile(T, 512, 8)

    def body(h_ref, g_ref, t_ref, l_ref, dh_ref, dg_ref):
        @pl.when(pl.program_id(0) == 0)
        def _():
            l_ref[...] = jnp.zeros_like(l_ref)
            dg_ref[...] = jnp.zeros_like(dg_ref)

        xv = h_ref[...]
        r = lax.rsqrt(jnp.mean(xv * xv, axis=-1, keepdims=True) + EPS)
        xh = xv * r
        e = xh * g_ref[...] - t_ref[...]
        l_ref[...] += jnp.sum(e * e, axis=0, keepdims=True)
        dy = e * (1.0 / d)
        dg_ref[...] += jnp.sum(dy * xh, axis=0, keepdims=True)
        dxh = dy * g_ref[...]
        dh_ref[...] = r * (dxh - xh * jnp.mean(dxh * xh, axis=-1, keepdims=True))

    row = pl.BlockSpec((tr, d), lambda i: (i, 0))
    vec = pl.BlockSpec((1, d), lambda i: (0, 0))
    return pl.pallas_call(
        body, grid=(T // tr,), in_specs=[row, vec, row], out_specs=[vec, row, vec],
        out_shape=[jax.ShapeDtypeStruct((1, d), F32), jax.ShapeDtypeStruct((T, d), F32),
                   jax.ShapeDtypeStruct((1, d), F32)],
        compiler_params=_cp("arbitrary"), name=name)(h, g, target)


def _gate_up_fwd(n, wgu, *, name):
    T, d = n.shape
    f = wgu.shape[1] // 2
    tm, tn = _tile(T, 512, 8), _tile(f, DFF)
    nf = f // tn

    tc = _tile(tn, EPI_COLS)

    def body(n_ref, wg_ref, wu_ref, g_ref, u_ref, a_ref):
        nv = n_ref[...].astype(MXU)
        for j in range(tn // tc):
            sl = slice(j * tc, (j + 1) * tc)
            gv = _nn(nv, wg_ref[:, sl].astype(MXU))
            uv = _nn(nv, wu_ref[:, sl].astype(MXU))
            g_ref[:, sl] = gv.astype(BF16)
            u_ref[:, sl] = uv.astype(BF16)
            a_ref[:, sl] = (gv * _sig(gv) * uv).astype(BF16)

    out = pl.BlockSpec((tm, tn), lambda i, j: (i, j))
    act = jax.ShapeDtypeStruct((T, f), BF16)
    return pl.pallas_call(
        body, grid=(T // tm, nf),
        in_specs=[pl.BlockSpec((tm, d), lambda i, j: (i, 0)), pl.BlockSpec((d, tn), lambda i, j: (0, j)),
                  pl.BlockSpec((d, tn), lambda i, j: (0, j + nf))],
        out_specs=[out, out, out], out_shape=[act, act, act], compiler_params=_cp("parallel", "parallel"),
        name=name)(n, wgu, wgu)


def _act_bwd(dh, wd, gate, up, scale, *, name):
    T, d = dh.shape
    f = wd.shape[0]
    tm, tn = _tile(T, 512, 8), _tile(f, DFF)

    tc = _tile(tn, EPI_COLS)

    def body(dh_ref, wd_ref, g_ref, u_ref, dg_ref, du_ref):
        dhv = dh_ref[...].astype(MXU)
        for j in range(tn // tc):
            sl = slice(j * tc, (j + 1) * tc)
            da = scale * _nt(dhv, wd_ref[sl, :].astype(MXU))
            gv, uv = g_ref[:, sl].astype(F32), u_ref[:, sl].astype(F32)
            s = _sig(gv)
            dg_ref[:, sl] = (da * uv * (s * (1.0 + gv * (1.0 - s)))).astype(BF16)
            du_ref[:, sl] = (da * (gv * s)).astype(BF16)

    tile = pl.BlockSpec((tm, tn), lambda i, j: (i, j))
    act = jax.ShapeDtypeStruct((T, f), BF16)
    return pl.pallas_call(
        body, grid=(T // tm, f // tn),
        in_specs=[pl.BlockSpec((tm, d), lambda i, j: (i, 0)), pl.BlockSpec((tn, d), lambda i, j: (j, 0)), tile, tile],
        out_specs=[tile, tile], out_shape=[act, act], compiler_params=_cp("parallel", "parallel"),
        name=name)(dh, wd, gate, up)


CONV_ROWS = 64
CONV_PAD = 8


def _rows_down(ref, r0, d, cols=slice(None)):
    if r0 - d >= 0:
        return ref[pl.ds(r0 - d, CONV_ROWS), cols]
    assert r0 == 0
    v = ref[pl.ds(0, CONV_ROWS), cols]
    ri = lax.broadcasted_iota(jnp.int32, v.shape, 0)
    return jnp.where(ri >= d, pltpu.roll(v, d, 0), 0.0)


def _fold8(v):
    return jnp.sum(v.reshape(CONV_ROWS // 8, 8, v.shape[1]), axis=0)


def _taps(w_ref, views):
    acc = None
    for k, v in enumerate(views):
        t = w_ref[k:k + 1, :] * v
        acc = t if acc is None else acc + t
    return acc


def _conv_a_fwd(pp, w8, bl, s, *, name):
    tc = CA_TILE
    nb = D // tc
    bcol, ccol, vcol = slice(0, tc), slice(tc, 2 * tc), slice(2 * tc, 3 * tc)

    def body(p_ref, w_ref, o_ref):
        for r0 in range(0, s, CONV_ROWS):
            cv = [_rows_down(p_ref, r0, 2 - k, ccol) * _rows_down(p_ref, r0, 2 - k, vcol) for k in range(3)]
            o_ref[pl.ds(r0, CONV_ROWS), :] = (p_ref[pl.ds(r0, CONV_ROWS), bcol] * _taps(w_ref, cv)).astype(BF16)

    return pl.pallas_call(
        body, grid=(bl, nb),
        in_specs=[pl.BlockSpec((s, 3 * tc), lambda b, j: (b, j)), pl.BlockSpec((8, tc), lambda b, j: (0, j))],
        out_specs=pl.BlockSpec((s, tc), lambda b, j: (b, j)),
        out_shape=jax.ShapeDtypeStruct((bl * s, D), BF16), compiler_params=_cp("parallel", "parallel"),
        name=name)(pp, w8)


def _conv_a_bwd(pp, w8, dya, dpp, bl, s, *, name):
    tc = CA_TILE
    nb = D // tc
    bcol, ccol, vcol = slice(0, tc), slice(tc, 2 * tc), slice(2 * tc, 3 * tc)

    def body(p_ref, w_ref, dy_ref, dpp_in, d_ref, dw_ref, dcp):
        del dpp_in

        @pl.when(pl.program_id(1) == 0)
        def _():
            dw_ref[...] = jnp.zeros_like(dw_ref)

        dcp[pl.ds(s, CONV_PAD), :] = jnp.zeros((CONV_PAD, tc), F32)
        dw_acc = [jnp.zeros((8, tc), F32) for _ in range(3)]
        for r0 in reversed(range(0, s, CONV_ROWS)):
            rows = pl.ds(r0, CONV_ROWS)
            cs = [_rows_down(p_ref, r0, 2 - k, ccol) for k in range(3)]
            vs = [_rows_down(p_ref, r0, 2 - k, vcol) for k in range(3)]
            cv = [c_ * v_ for c_, v_ in zip(cs, vs)]
            dy = dy_ref[rows, :]
            d_ref[rows, bcol] = (dy * _taps(w_ref, cv)).astype(BF16)
            dconv = dy * p_ref[rows, bcol]
            dcp[rows, :] = dconv
            dcv = _taps(w_ref, [dcp[pl.ds(r0 + 2, CONV_ROWS), :], dcp[pl.ds(r0 + 1, CONV_ROWS), :], dconv])
            d_ref[rows, ccol] = (dcv * vs[2]).astype(BF16)
            d_ref[rows, vcol] = (dcv * cs[2]).astype(BF16)
            dw_acc = [acc + _fold8(dconv * cv_) for acc, cv_ in zip(dw_acc, cv)]
        for k in range(3):
            dw_ref[k:k + 1, :] += jnp.sum(dw_acc[k], axis=0, keepdims=True)

    wspec = pl.BlockSpec((8, tc), lambda j, b: (0, j))
    wide = pl.BlockSpec((s, 3 * tc), lambda j, b: (b, j))
    return pl.pallas_call(
        body, grid=(nb, bl),
        in_specs=[wide, wspec, pl.BlockSpec((s, tc), lambda j, b: (b, j)), pl.BlockSpec(memory_space=pl.ANY)],
        out_specs=[wide, wspec], out_shape=[jax.ShapeDtypeStruct(dpp.shape, dpp.dtype), jax.ShapeDtypeStruct((8, D), F32)],
        scratch_shapes=[pltpu.VMEM((s + CONV_PAD, tc), F32)], input_output_aliases={3: 0},
        compiler_params=_cp("parallel", "arbitrary"), name=name)(pp, w8, dya, dpp)


def _conv_ssm_fwd(pp, w8, bias, bl, s, *, name):
    tc = 256
    width = DI + 2 * NG * NS
    nb = width // tc

    def body(x_ref, w_ref, b_ref, o_ref):
        for r0 in range(0, s, CONV_ROWS):
            pre = _taps(w_ref, [_rows_down(x_ref, r0, 3 - k) for k in range(4)]) + b_ref[...]
            o_ref[pl.ds(r0, CONV_ROWS), :] = pre * _sig(pre)

    return pl.pallas_call(
        body, grid=(bl, nb),
        in_specs=[pl.BlockSpec((s, tc), lambda b, j: (b, O_XBC // tc + j)),
                  pl.BlockSpec((8, tc), lambda b, j: (0, j)), pl.BlockSpec((1, tc), lambda b, j: (0, j))],
        out_specs=pl.BlockSpec((s, tc), lambda b, j: (b, j)),
        out_shape=jax.ShapeDtypeStruct((bl * s, width), F32), compiler_params=_cp("parallel", "parallel"),
        name=name)(pp, w8, bias)


def _conv_ssm_bwd(pp, w8, bias, dxc, ch_off, dpp, bl, s, *, name):
    n = dxc.shape[1]
    tc = 256
    nb = n // tc
    o0 = ch_off // tc

    def body(x_ref, w_ref, b_ref, d_ref, dpp_in, dx_ref, dw_ref, db_ref, dp):
        del dpp_in

        @pl.when(pl.program_id(1) == 0)
        def _():
            dw_ref[...] = jnp.zeros_like(dw_ref)
            db_ref[...] = jnp.zeros_like(db_ref)

        dp[pl.ds(s, CONV_PAD), :] = jnp.zeros((CONV_PAD, tc), F32)
        dw_acc = [jnp.zeros((8, tc), F32) for _ in range(4)]
        db_acc = jnp.zeros((8, tc), F32)
        for r0 in reversed(range(0, s, CONV_ROWS)):
            rows = pl.ds(r0, CONV_ROWS)
            xs = [_rows_down(x_ref, r0, 3 - k) for k in range(4)]
            pre = _taps(w_ref, xs) + b_ref[...]
            sg = _sig(pre)
            dpre = d_ref[rows, :] * (sg * (1.0 + pre * (1.0 - sg)))
            dp[rows, :] = dpre
            dx = _taps(w_ref, [dp[pl.ds(r0 + 3 - k, CONV_ROWS), :] for k in range(3)] + [dpre])
            dx_ref[rows, :] = dx.astype(BF16)
            db_acc = db_acc + _fold8(dpre)
            dw_acc = [acc + _fold8(dpre * x_) for acc, x_ in zip(dw_acc, xs)]
        db_ref[...] += jnp.sum(db_acc, axis=0, keepdims=True)
        for k in range(4):
            dw_ref[k:k + 1, :] += jnp.sum(dw_acc[k], axis=0, keepdims=True)

    return pl.pallas_call(
        body, grid=(nb, bl),
        in_specs=[pl.BlockSpec((s, tc), lambda j, b: (b, O_XBC // tc + o0 + j)),
                  pl.BlockSpec((8, tc), lambda j, b: (0, o0 + j)), pl.BlockSpec((1, tc), lambda j, b: (0, o0 + j)),
                  pl.BlockSpec((s, tc), lambda j, b: (b, j)), pl.BlockSpec(memory_space=pl.ANY)],
        out_specs=[pl.BlockSpec((s, tc), lambda j, b: (b, O_XBC // tc + o0 + j)),
                   pl.BlockSpec((8, tc), lambda j, b: (0, j)), pl.BlockSpec((1, tc), lambda j, b: (0, j))],
        out_shape=[jax.ShapeDtypeStruct(dpp.shape, dpp.dtype), jax.ShapeDtypeStruct((8, n), F32),
                   jax.ShapeDtypeStruct((1, n), F32)],
        scratch_shapes=[pltpu.VMEM((s + CONV_PAD, tc), F32)], input_output_aliases={4: 0},
        compiler_params=_cp("parallel", "arbitrary"), name=name)(pp, w8, bias, dxc, dpp)


def _softplus(x):
    return jnp.maximum(x, 0.0) + jnp.log1p(jnp.exp(-jnp.abs(x)))


def _head_group_matrix():
    h = jnp.arange(128)[:, None]
    j = jnp.arange(NG * 128)[None, :]
    per = NH // NG
    return ((h < NH) & (j == (h // per) * 128 + h % per)).astype(F32)


def _dt_fwd(pp, bias128, *, name):
    T = pp.shape[0]
    tr = _tile(T, 1024, 8)
    per = NH // NG

    def body(x_ref, b_ref, p_ref, g_ref, t_ref):
        lane = lax.broadcasted_iota(jnp.int32, (tr, 128), 1)
        dt = jnp.where(lane < NH, _softplus(x_ref[...] + b_ref[...]), 0.0)
        g_ref[...] = _nn(dt, p_ref[...], HI)
        eye = (lax.broadcasted_iota(jnp.int32, (NH, 128), 0)
               == lax.broadcasted_iota(jnp.int32, (NH, 128), 1)).astype(F32)
        t_ref[...] = _dot(eye, dt, ((1,), (1,)), HI).reshape(NG, per, tr)

    vec = pl.BlockSpec((1, 128), lambda i: (0, 0))
    return pl.pallas_call(
        body, grid=(T // tr,),
        in_specs=[pl.BlockSpec((tr, 128), lambda i: (i, O_DT // 128)), vec,
                  pl.BlockSpec((128, NG * 128), lambda i: (0, 0))],
        out_specs=[pl.BlockSpec((tr, NG * 128), lambda i: (i, 0)), pl.BlockSpec((NG, per, tr), lambda i: (0, 0, i))],
        out_shape=[jax.ShapeDtypeStruct((T, NG * 128), F32), jax.ShapeDtypeStruct((NG, per, T), F32)],
        compiler_params=_cp("parallel"), name=name)(pp, bias128, _head_group_matrix())


def _dt_bwd(pp, bias128, ddtg, dpp, *, name):
    T = pp.shape[0]
    tr = _tile(T, 1024, 8)

    def body(x_ref, b_ref, d_ref, p_ref, dpp_in, o_ref, db_ref):
        del dpp_in

        @pl.when(pl.program_id(0) == 0)
        def _():
            db_ref[...] = jnp.zeros_like(db_ref)

        lane = lax.broadcasted_iota(jnp.int32, (tr, 128), 1)
        ddt = _dot(d_ref[...], p_ref[...], ((1,), (1,)), HI)
        dr = jnp.where(lane < NH, ddt * _sig(x_ref[...] + b_ref[...]), 0.0)
        db_ref[...] += jnp.sum(dr, axis=0, keepdims=True)
        o_ref[...] = dr.astype(BF16)

    col = pl.BlockSpec((tr, 128), lambda i: (i, O_DT // 128))
    vec = pl.BlockSpec((1, 128), lambda i: (0, 0))
    return pl.pallas_call(
        body, grid=(T // tr,),
        in_specs=[col, vec, pl.BlockSpec((tr, NG * 128), lambda i: (i, 0)), pl.BlockSpec((128, NG * 128), lambda i: (0, 0)),
                  pl.BlockSpec(memory_space=pl.ANY)],
        out_specs=[col, vec],
        out_shape=[jax.ShapeDtypeStruct(dpp.shape, dpp.dtype), jax.ShapeDtypeStruct((1, 128), F32)],
        input_output_aliases={4: 0}, compiler_params=_cp("arbitrary"),
        name=name)(pp, bias128, ddtg, _head_group_matrix(), dpp)


def _tril():
    return lax.broadcasted_iota(jnp.int32, (CH, CH), 0) >= lax.broadcasted_iota(jnp.int32, (CH, CH), 1)


def _ssd_common(dt, dtt, arow, acol):
    ri = lax.broadcasted_iota(jnp.int32, (CH, CH), 0)
    ci = lax.broadcasted_iota(jnp.int32, (CH, CH), 1)
    tril = ri >= ci
    triu = ri <= ci
    acs_col = _nn(tril.astype(F32), dt * arow, HI)
    acs_row = _nn(dtt * acol, triu.astype(F32), HI)
    return tril, triu, acs_col, acs_row


def _pair_terms(q, dt, acs_col, acs_row, tril, lo):
    ha, hb = 2 * q, 2 * q + 1
    col_a, col_b = acs_col[:, ha:ha + 1], acs_col[:, hb:hb + 1]
    row_a, row_b = acs_row[ha:ha + 1, :], acs_row[hb:hb + 1, :]
    last_a, last_b = acs_col[CH - 1:CH, ha:ha + 1], acs_col[CH - 1:CH, hb:hb + 1]
    out = dict(
        dtsel=jnp.where(lo, dt[:, ha:ha + 1], dt[:, hb:hb + 1]),
        d_a=jnp.exp(jnp.where(tril, col_a - row_a, NEG)), d_b=jnp.exp(jnp.where(tril, col_b - row_b, NEG)),
        esel=jnp.where(lo, jnp.exp(col_a), jnp.exp(col_b)),
        fsel=jnp.where(lo, jnp.exp(last_a - col_a), jnp.exp(last_b - col_b)),
        g_a=jnp.exp(last_a), g_b=jnp.exp(last_b))
    return out


def _ssd_fwd(xc, pp, dtg, dtt, arow, acol, dexp, ng, bl, s, *, name):
    nc = s // CH
    T = bl * s

    ex = SSD_EX if bl % SSD_EX == 0 else 1

    def body(*refs):
        arow_ref, acol_ref, dexp_ref, ng_ref = refs[6 * ex:6 * ex + 4]
        st_ref = refs[-1]

        @pl.when(pl.program_id(2) == 0)
        def _():
            st_ref[...] = jnp.zeros_like(st_ref)

        y_ref, yn_ref, prev_ref = refs[6 * ex + 4:6 * ex + 7]
        for e in range(ex):
            one(*refs[6 * e:6 * e + 6], arow_ref, acol_ref, dexp_ref, ng_ref,
                y_ref.at[e], yn_ref.at[e], prev_ref.at[e], st_ref.at[e])

    def one(xs_ref, bm_ref, cm_ref, z_ref, dt_ref, dtt_ref, arow_ref, acol_ref, dexp_ref, ng_ref,
            y_ref, yn_ref, prev_ref, st_ref):
        dt = dt_ref[...]
        tril, _, acs_col, acs_row = _ssd_common(dt, dtt_ref[...], -jnp.exp(arow_ref[...]), -jnp.exp(acol_ref[...]))
        bm, cm = bm_ref[...].astype(MXU), cm_ref[...].astype(MXU)
        cb = _nt(cm, bm)
        lo = lax.broadcasted_iota(jnp.int32, (CH, 128), 1) < HD
        sub_lo = lax.broadcasted_iota(jnp.int32, (128, NS), 0) < HD
        for q in range(4):
            t = _pair_terms(q, dt, acs_col, acs_row, tril, lo)
            x = xs_ref[:, 128 * q:128 * (q + 1)]
            xd = x * t["dtsel"]
            y = (_nn((cb * t["d_a"]).astype(MXU), jnp.where(lo, xd, 0.0).astype(MXU))
                 + _nn((cb * t["d_b"]).astype(MXU), jnp.where(lo, 0.0, xd).astype(MXU)))
            prev = st_ref[q]
            prev_ref[q] = prev
            y = y + t["esel"] * _nt(cm, prev.astype(MXU))
            st_ref[q] = prev * jnp.where(sub_lo, t["g_a"], t["g_b"]) + _tn((xd * t["fsel"]).astype(MXU), bm)
            y_ref[:, 128 * q:128 * (q + 1)] = y + dexp_ref[:, 128 * q:128 * (q + 1)] * x
        zv = z_ref[...]
        yg = y_ref[...] * (zv * _sig(zv))
        r = lax.rsqrt(jnp.mean(yg * yg, axis=-1, keepdims=True) + EPS)
        yn_ref[...] = (yg * r * ng_ref[...]).astype(BF16)

    def row(e, width, off_blocks):
        return pl.BlockSpec((CH, width), lambda g, b, c: ((b * ex + e) * nc + c, off_blocks + g))

    per_ex_in = [[row(e, GW, 0), row(e, NS, DI // NS), row(e, NS, DI // NS + NG), row(e, GW, O_Z // GW), row(e, 128, 0),
                  pl.BlockSpec((None, 8, CH), lambda g, b, c, e=e: (g, 0, (b * ex + e) * nc + c))] for e in range(ex)]
    by_example = pl.BlockSpec((ex, CH, GW), lambda g, b, c: (b, c, g))
    y, yn, prev = pl.pallas_call(
        body, grid=(NG, bl // ex, nc),
        in_specs=sum(per_ex_in, []) + [pl.BlockSpec((1, 128), lambda g, b, c: (0, g)),
                                       pl.BlockSpec((None, 8, 1), lambda g, b, c: (g, 0, 0)),
                                       pl.BlockSpec((1, GW), lambda g, b, c: (0, g)),
                                       pl.BlockSpec((1, GW), lambda g, b, c: (0, g))],
        out_specs=[by_example, by_example,
                   pl.BlockSpec((ex, None, 4, 128, NS), lambda g, b, c: (b, c, g, 0, 0))],
        out_shape=[jax.ShapeDtypeStruct((bl, s, DI), F32), jax.ShapeDtypeStruct((bl, s, DI), BF16),
                   jax.ShapeDtypeStruct((bl, nc, 16, 128, NS), F32)],
        scratch_shapes=[pltpu.VMEM((ex, 4, 128, NS), F32)],
        compiler_params=_cp("parallel", "parallel", "arbitrary"), name=name,
    )(*([xc, xc, xc, pp, dtg, dtt] * ex), arow, acol, dexp, ng)
    return y.reshape(T, DI), yn.reshape(T, DI), prev.reshape(bl * nc, 16, 128, NS)


def _ssd_bwd(dyn, y, xc, pp, dtg, dtt, arow, acol, dexp, ng, prev, dpp, bl, s, *, name):
    nc = s // CH
    T = bl * s

    def rsum(v):
        return jnp.sum(v, axis=1, keepdims=True)

    def asum(v):
        return jnp.sum(jnp.sum(v, axis=0, keepdims=True), axis=1, keepdims=True)

    ex = 1

    def body(*refs):
        shared = refs[9 * ex:9 * ex + 4]
        dz_ref, dxs_ref, db_ref, dc_ref, ddt_ref, dng_ref, dd_ref, dal_ref = refs[9 * ex + 5:9 * ex + 13]
        dst_ref = refs[-1]

        @pl.when((pl.program_id(1) == 0) & (pl.program_id(2) == 0))
        def _():
            dng_ref[...] = jnp.zeros_like(dng_ref)
            dd_ref[...] = jnp.zeros_like(dd_ref)
            dal_ref[...] = jnp.zeros_like(dal_ref)

        @pl.when(pl.program_id(2) == 0)
        def _():
            dst_ref[...] = jnp.zeros_like(dst_ref)

        for e in range(ex):
            one(*refs[9 * e:9 * e + 8], *shared, refs[9 * e + 8], dz_ref.at[e], dxs_ref.at[e], db_ref.at[e],
                dc_ref.at[e], ddt_ref.at[e], dng_ref, dd_ref, dal_ref, dst_ref.at[e])

    def one(dyn_ref, y_ref, xs_ref, bm_ref, cm_ref, z_ref, dt_ref, dtt_ref, arow_ref, acol_ref, dexp_ref, ng_ref,
            prev_ref, dz_ref, dxs_ref, db_ref, dc_ref, ddt_ref, dng_ref, dd_ref, dal_ref, dst_ref):
        yv, zv, xsv, dexp_v = y_ref[...], z_ref[...], xs_ref[...], dexp_ref[...]
        sz = _sig(zv)
        silu = zv * sz
        yg = yv * silu
        r = lax.rsqrt(jnp.mean(yg * yg, axis=-1, keepdims=True) + EPS)
        yh = yg * r
        dynv = dyn_ref[...]
        dng_ref[...] += jnp.sum(dynv * yh, axis=0, keepdims=True)
        dyh = dynv * ng_ref[...]
        dyg = r * (dyh - yh * jnp.mean(dyh * yh, axis=-1, keepdims=True))
        dz_ref[...] = (dyg * yv * (sz * (1.0 + zv * (1.0 - sz)))).astype(BF16)
        dy_all = dyg * silu
        dd_ref[...] += jnp.sum(dy_all * xsv, axis=0, keepdims=True)

        dt = dt_ref[...]
        arow_v = -jnp.exp(arow_ref[...])
        tril, triu, acs_col, acs_row = _ssd_common(dt, dtt_ref[...], arow_v, -jnp.exp(acol_ref[...]))
        bm, cm = bm_ref[...].astype(MXU), cm_ref[...].astype(MXU)
        cb = _nt(cm, bm)
        lane = lax.broadcasted_iota(jnp.int32, (CH, 128), 1)
        is_last = lax.broadcasted_iota(jnp.int32, (CH, 128), 0) == CH - 1
        lo = lane < HD
        sub_lo = lax.broadcasted_iota(jnp.int32, (128, NS), 0) < HD
        dcb = jnp.zeros((CH, CH), F32)
        dc_acc = jnp.zeros((CH, NS), F32)
        db_acc = jnp.zeros((CH, NS), F32)
        dacs = jnp.zeros((CH, 128), F32)
        ddtx = jnp.zeros((CH, 128), F32)
        csum = jnp.zeros((8, CH), F32)
        sub8 = lax.broadcasted_iota(jnp.int32, (8, CH), 0)
        for q in range(4):
            ha, hb = 2 * q, 2 * q + 1
            sl = slice(128 * q, 128 * (q + 1))
            t = _pair_terms(q, dt, acs_col, acs_row, tril, lo)
            x, dy = xsv[:, sl], dy_all[:, sl]
            xd = x * t["dtsel"]
            xd_m = xd.astype(MXU)
            dy_lo, dy_hi = jnp.where(lo, dy, 0.0).astype(MXU), jnp.where(lo, 0.0, dy).astype(MXU)
            m_a, m_b = cb * t["d_a"], cb * t["d_b"]
            prev_m = prev_ref[q].astype(MXU)
            dnext = dst_ref[q]
            dnext_m = dnext.astype(MXU)
            bds = _nt(bm, dnext_m)
            dxd = _tn(m_a.astype(MXU), dy_lo) + _tn(m_b.astype(MXU), dy_hi) + t["fsel"] * bds
            dye_m = (dy * t["esel"]).astype(MXU)
            dst_ref[q] = dnext * jnp.where(sub_lo, t["g_a"], t["g_b"]) + _tn(dye_m, cm)
            dm_a, dm_b = _nt(dy_lo, xd_m), _nt(dy_hi, xd_m)
            dcb = dcb + dm_a * t["d_a"] + dm_b * t["d_b"]
            g_a, g_b = dm_a * m_a, dm_b * m_b
            csum = (csum + jnp.where(sub8 == ha, jnp.sum(g_a, axis=0, keepdims=True), 0.0)
                    + jnp.where(sub8 == hb, jnp.sum(g_b, axis=0, keepdims=True), 0.0))
            tf = t["fsel"] * xd * bds
            tyf = dy * (t["esel"] * _nt(cm, prev_m)) - tf
            dpp = dnext * prev_ref[q]
            ea = asum(jnp.where(lo, tf, 0.0)) + t["g_a"] * asum(jnp.where(sub_lo, dpp, 0.0))
            eb = asum(jnp.where(lo, 0.0, tf)) + t["g_b"] * asum(jnp.where(sub_lo, 0.0, dpp))
            ra = rsum(g_a + jnp.where(lo, tyf, 0.0)) + jnp.where(is_last, ea, 0.0)
            rb = rsum(g_b + jnp.where(lo, 0.0, tyf)) + jnp.where(is_last, eb, 0.0)
            dacs = dacs + jnp.where(lane == ha, ra, 0.0) + jnp.where(lane == hb, rb, 0.0)
            tx = dxd * x
            ddtx = (ddtx + jnp.where(lane == ha, rsum(jnp.where(lo, tx, 0.0)), 0.0)
                    + jnp.where(lane == hb, rsum(jnp.where(lo, 0.0, tx)), 0.0))
            dxs_ref[:, sl] = dxd * t["dtsel"] + dexp_v[:, sl] * dy
            dc_acc = dc_acc + _nn(dye_m, prev_m)
            db_acc = db_acc + _nn((xd * t["fsel"]).astype(MXU), dnext_m)
        dcb_m = dcb.astype(MXU)
        dc_ref[...] = dc_acc + _nn(dcb_m, bm)
        db_ref[...] = db_acc + _tn(dcb_m, cm)
        dacs = dacs - jnp.concatenate([csum, jnp.zeros((CH - 8, CH), F32)], axis=0).T
        dla = _nn(triu.astype(F32), dacs, HI)
        ddt_ref[...] = dla * arow_v + ddtx
        dal_ref[...] += jnp.sum(dla * dt, axis=0, keepdims=True) * arow_v

    def row(e, width, off_blocks):
        return pl.BlockSpec((CH, width), lambda g, b, c: ((b * ex + e) * nc + nc - 1 - c, off_blocks + g))

    per_ex_in = [[row(e, GW, 0), row(e, GW, 0), row(e, GW, 0), row(e, NS, DI // NS), row(e, NS, DI // NS + NG),
                  row(e, GW, O_Z // GW), row(e, 128, 0),
                  pl.BlockSpec((None, 8, CH), lambda g, b, c, e=e: (g, 0, (b * ex + e) * nc + nc - 1 - c)),
                  pl.BlockSpec((None, 4, 128, NS), lambda g, b, c, e=e: ((b * ex + e) * nc + nc - 1 - c, g, 0, 0))]
                 for e in range(ex)]

    def by_example(width, off_blocks):
        return pl.BlockSpec((ex, CH, width), lambda g, b, c: (b, nc - 1 - c, off_blocks + g))

    gvec = pl.BlockSpec((1, GW), lambda g, b, c: (0, g))
    hvec = pl.BlockSpec((1, 128), lambda g, b, c: (0, g))
    out = pl.pallas_call(
        body, grid=(NG, bl // ex, nc),
        in_specs=sum(per_ex_in, []) + [hvec, pl.BlockSpec((None, 8, 1), lambda g, b, c: (g, 0, 0)), gvec, gvec,
                                       pl.BlockSpec(memory_space=pl.ANY)],
        out_specs=[by_example(GW, O_Z // GW), by_example(GW, 0), by_example(NS, 0), by_example(NS, 0),
                   by_example(128, 0), gvec, gvec, hvec],
        input_output_aliases={9 * ex + 4: 0},
        out_shape=[jax.ShapeDtypeStruct((bl, s, dpp.shape[1]), dpp.dtype), jax.ShapeDtypeStruct((bl, s, DI), F32),
                   jax.ShapeDtypeStruct((bl, s, NG * NS), F32), jax.ShapeDtypeStruct((bl, s, NG * NS), F32),
                   jax.ShapeDtypeStruct((bl, s, NG * 128), F32), jax.ShapeDtypeStruct((1, DI), F32),
                   jax.ShapeDtypeStruct((1, DI), F32), jax.ShapeDtypeStruct((1, NG * 128), F32)],
        scratch_shapes=[pltpu.VMEM((ex, 4, 128, NS), F32)],
        compiler_params=_cp("arbitrary", "arbitrary", "arbitrary"), name=name,
    )(*([dyn, y, xc, xc, xc, pp, dtg, dtt, prev] * ex), arow, acol, dexp, ng, dpp.reshape(bl, s, dpp.shape[1]))
    return (out[0].reshape(T, -1), out[1].reshape(T, DI), out[2].reshape(T, NG * NS), out[3].reshape(T, NG * NS),
            out[4].reshape(T, NG * 128), out[5], out[6], out[7])


def _merge_fwd(pp, ya, yb, *, name):
    T = ya.shape[0]
    tr = _tile(T, 512, 8)

    def body(ga_ref, gb_ref, ya_ref, yb_ref, o_ref):
        o_ref[...] = (_sig(ga_ref[...]) * ya_ref[...].astype(F32)
                      + _sig(gb_ref[...]) * yb_ref[...].astype(F32)).astype(BF16)

    row = pl.BlockSpec((tr, D), lambda i: (i, 0))
    return pl.pallas_call(
        body, grid=(T // tr,),
        in_specs=[pl.BlockSpec((tr, D), lambda i: (i, O_GA // D)), pl.BlockSpec((tr, D), lambda i: (i, O_GB // D)),
                  row, row],
        out_specs=row, out_shape=jax.ShapeDtypeStruct((T, D), BF16), compiler_params=_cp("parallel"),
        name=name)(pp, pp, ya, yb)


def _merge_bwd(pp, ya, yb, dm, *, name):
    T = ya.shape[0]
    tr = _tile(T, 512, 8)
    assert O_GB == O_GA + D and O_GA % (2 * D) == 0

    def body(g_ref, ya_ref, yb_ref, dm_ref, dya_ref, dyb_ref, dg_ref):
        sa, sb, dmv = _sig(g_ref[:, :D]), _sig(g_ref[:, D:]), dm_ref[...]
        dya_ref[...] = (dmv * sa).astype(BF16)
        dyb_ref[...] = (dmv * sb).astype(BF16)
        dg_ref[:, :D] = (dmv * ya_ref[...].astype(F32) * (sa * (1.0 - sa))).astype(BF16)
        dg_ref[:, D:] = (dmv * yb_ref[...].astype(F32) * (sb * (1.0 - sb))).astype(BF16)

    row = pl.BlockSpec((tr, D), lambda i: (i, 0))
    gates = pl.BlockSpec((tr, 2 * D), lambda i: (i, O_GA // (2 * D)))
    act = jax.ShapeDtypeStruct((T, D), BF16)
    return pl.pallas_call(
        body, grid=(T // tr,), in_specs=[gates, row, row, row], out_specs=[row, row, gates],
        out_shape=[act, act, jax.ShapeDtypeStruct((T, NPP), BF16)], compiler_params=_cp("parallel"),
        name=name)(pp, ya, yb, dm)


def _softmax_rows(sc):
    e = jnp.exp(sc - jnp.max(sc, axis=-1, keepdims=True))
    return e / jnp.sum(e, axis=-1, keepdims=True)


def _attn_fwd(q, kv, bl, s, *, name):
    m = kv.shape[0] // bl
    tq = _tile(s, 1024)
    nq = s // tq
    scale = 1.0 / math.sqrt(XD)

    def body(q_ref, k_ref, v_ref, o_ref):
        p = _softmax_rows(_nt(q_ref[...], k_ref[...]) * scale)
        o_ref[...] = _nn(p.astype(MXU), v_ref[...]).astype(BF16)

    qspec = pl.BlockSpec((tq, XD), lambda b, h, i: (b * nq + i, h))
    return pl.pallas_call(
        body, grid=(bl, XH, nq),
        in_specs=[qspec, pl.BlockSpec((m, XD), lambda b, h, i: (b, h)),
                  pl.BlockSpec((m, XD), lambda b, h, i: (b, XH + h))],
        out_specs=qspec, out_shape=jax.ShapeDtypeStruct((bl * s, D), BF16),
        compiler_params=_cp("parallel", "parallel", "parallel"), name=name)(q, kv, kv)


def _attn_bwd(q, kv, do, bl, s, *, name):
    m = kv.shape[0] // bl
    tq = _tile(s, 1024)
    nq = s // tq
    scale = 1.0 / math.sqrt(XD)

    def body(q_ref, k_ref, v_ref, do_ref, dq_ref, dk_ref, dv_ref):
        @pl.when(pl.program_id(2) == 0)
        def _():
            dk_ref[...] = jnp.zeros_like(dk_ref)
            dv_ref[...] = jnp.zeros_like(dv_ref)

        qv, kvv, vv, dov = q_ref[...], k_ref[...], v_ref[...], do_ref[...]
        p = _softmax_rows(_nt(qv, kvv) * scale)
        dp = _nt(dov, vv)
        ds = (p * (dp - jnp.sum(dp * p, axis=-1, keepdims=True)) * scale).astype(MXU)
        dq_ref[...] = _nn(ds, kvv).astype(BF16)
        dk_ref[...] += _tn(ds, qv)
        dv_ref[...] += _tn(p.astype(MXU), dov)

    qspec = pl.BlockSpec((tq, XD), lambda b, h, i: (b * nq + i, h))
    kspec = pl.BlockSpec((m, XD), lambda b, h, i: (b, h))
    return pl.pallas_call(
        body, grid=(bl, XH, nq),
        in_specs=[qspec, kspec, pl.BlockSpec((m, XD), lambda b, h, i: (b, XH + h)), qspec],
        out_specs=[qspec, kspec, kspec],
        out_shape=[jax.ShapeDtypeStruct((bl * s, D), BF16), jax.ShapeDtypeStruct((bl * m, D), F32),
                   jax.ShapeDtypeStruct((bl * m, D), F32)],
        compiler_params=_cp("parallel", "parallel", "arbitrary"), name=name)(q, kv, kv, do)


def _row_tile(r, c, max_elems=512 * 1024, align=16):
    best = None
    for t in range(align, r + 1, align):
        if r % t == 0 and t * c <= max_elems:
            best = t
    return best if best is not None else r


def _addn(a, others, *, name, also_bf16=False):
    r, c = a.shape
    tr = _row_tile(r, c)
    n = len(others)

    def body(*refs):
        acc = refs[0][...].astype(F32)
        for o_ref in refs[1:1 + n]:
            acc = acc + o_ref[...].astype(F32)
        refs[1 + n][...] = acc
        if also_bf16:
            refs[2 + n][...] = acc.astype(BF16)

    spec = pl.BlockSpec((tr, c), lambda i: (i, 0))
    shapes = [jax.ShapeDtypeStruct((r, c), F32)] + ([jax.ShapeDtypeStruct((r, c), BF16)] if also_bf16 else [])
    out = pl.pallas_call(
        body, grid=(r // tr,), in_specs=[spec] * (1 + n), out_specs=[spec] * len(shapes), out_shape=shapes,
        compiler_params=_cp("parallel"), name=name)(a, *others)
    return out if also_bf16 else out[0]


def _sum_leading(a, *, name):
    n, r, c = a.shape

    def body(a_ref, o_ref):
        acc = a_ref[0]
        for i in range(1, n):
            acc = acc + a_ref[i]
        o_ref[...] = acc

    return pl.pallas_call(body, out_shape=jax.ShapeDtypeStruct((r, c), F32), name=name)(a)


def _adamw_math(wv, gv, mv, vv):
    m2 = ADAM_B1 * mv + (1.0 - ADAM_B1) * gv
    v2 = ADAM_B2 * vv + (1.0 - ADAM_B2) * (gv * gv)
    m_hat = m2 / (1.0 - ADAM_B1 ** ADAM_STEP)
    v_hat = v2 / (1.0 - ADAM_B2 ** ADAM_STEP)
    return -ADAM_LR * (m_hat / (jnp.sqrt(v_hat) + ADAM_EPS) + ADAM_WD * wv), m2, v2


def _adamw(w, g, m, v, *, name):
    r, c = w.shape
    tr = _row_tile(r, c, align=8)

    def body(w_ref, g_ref, m_ref, v_ref, d_ref, mo_ref, vo_ref):
        d_ref[...], mo_ref[...], vo_ref[...] = _adamw_math(w_ref[...], g_ref[...], m_ref[...], v_ref[...])

    spec = pl.BlockSpec((tr, c), lambda i: (i, 0))
    shp = jax.ShapeDtypeStruct((r, c), F32)
    return pl.pallas_call(
        body, grid=(r // tr,), in_specs=[spec] * 4, out_specs=[spec] * 3, out_shape=[shp] * 3,
        compiler_params=_cp("parallel"), name=name)(w, g, m, v)


def _adamw_halves(w, g_mine, g_other, m, v, c, *, name):
    r, cols = w.shape
    h = r // 2
    tr = _row_tile(h, cols, align=8)
    nh = h // tr

    def body(c_ref, w_ref, gm_ref, go_ref, m_ref, v_ref, g_ref, d_ref, mo_ref, vo_ref):
        gv = jnp.where(pl.program_id(0) // nh == c_ref[0], gm_ref[...], go_ref[...])
        g_ref[...] = gv
        d_ref[...], mo_ref[...], vo_ref[...] = _adamw_math(w_ref[...], gv, m_ref[...], v_ref[...])

    full = pl.BlockSpec((tr, cols), lambda i, c_: (i, 0))
    half = pl.BlockSpec((tr, cols), lambda i, c_: (i % nh, 0))
    shp = jax.ShapeDtypeStruct((r, cols), F32)
    return pl.pallas_call(
        body,
        grid_spec=pltpu.PrefetchScalarGridSpec(num_scalar_prefetch=1, grid=(2 * nh,),
                                               in_specs=[full, half, half, full, full], out_specs=[full] * 4),
        out_shape=[shp] * 4, compiler_params=_cp("parallel"), name=name,
    )(jnp.reshape(c, (1,)).astype(jnp.int32), w, g_mine, g_other, m, v)


def _flip(i, d):
    return 1 - i if d else i


def _comm(name, ins, out_shapes, n_remote, n_local, plan, aliases=None):
    n_in, n_out = len(ins), len(out_shapes)

    def body(*refs):
        in_refs, out_refs = refs[:n_in], refs[n_in:n_in + n_out]
        send_sems, recv_sems = refs[n_in + n_out], refs[n_in + n_out + 1]
        x, y, c = lax.axis_index("x"), lax.axis_index("y"), lax.axis_index("c")
        remote, local = plan(in_refs, out_refs, x, y, c)
        assert len(remote) == n_remote and len(local) == n_local
        copies = []
        if n_local:
            loc_sems = refs[n_in + n_out + 2]
            copies += [pltpu.make_async_copy(s_, d_, loc_sems.at[i]) for i, (s_, d_) in enumerate(local)]
        copies += [pltpu.make_async_remote_copy(src_ref=s_, dst_ref=d_, send_sem=send_sems.at[i],
                                                recv_sem=recv_sems.at[i], device_id=dev, device_id_type=MESH)
                   for i, (s_, d_, dev) in enumerate(remote)]
        for cp in copies:
            cp.start()
        for cp in copies:
            cp.wait()

    hbm = pl.BlockSpec(memory_space=pl.ANY)
    scratch = [pltpu.SemaphoreType.DMA((n_remote,)), pltpu.SemaphoreType.DMA((n_remote,))]
    if n_local:
        scratch.append(pltpu.SemaphoreType.DMA((n_local,)))
    return pl.pallas_call(
        body, in_specs=[hbm] * n_in, out_specs=[hbm] * n_out, out_shape=out_shapes, scratch_shapes=scratch,
        input_output_aliases=aliases or {}, compiler_params=pltpu.CompilerParams(has_side_effects=True),
        name=name)(*ins)


HBM_SPEC = pl.BlockSpec(memory_space=pltpu.HBM)
SEM_SPEC = pl.BlockSpec(memory_space=pltpu.SEMAPHORE)
DATAFLOW = pltpu.SideEffectType.DATAFLOW_SIDE_EFFECTING


def _remote_copies(plan, srcs, lands, send_sems, recv_sems, n_copies):
    x, y, c = lax.axis_index("x"), lax.axis_index("y"), lax.axis_index("c")
    copies = plan(srcs, lands, x, y, c)
    assert len(copies) == n_copies
    return [pltpu.make_async_remote_copy(src_ref=s_, dst_ref=d_, send_sem=send_sems.at[i], recv_sem=recv_sems.at[i],
                                         device_id=dev, device_id_type=MESH) for i, (s_, d_, dev) in enumerate(copies)]


def _split_start(name, srcs, lands, n_copies, plan, after=None):
    ns, nb = len(srcs), len(srcs) + len(lands)
    n_after = 0 if after is None else 1
    n_in = nb + n_after

    def body(*refs):
        for cp in _remote_copies(plan, refs[:ns], refs[ns:nb], refs[n_in], refs[n_in + 1], n_copies):
            cp.start()
        refs[-1][...] = jnp.zeros_like(refs[-1])

    arrays = [pltpu.with_memory_space_constraint(a_, pltpu.HBM) for a_ in list(srcs) + list(lands)]
    out = pl.pallas_call(
        body, name=name,
        out_shape=(pltpu.SemaphoreType.DMA((n_copies,)), pltpu.SemaphoreType.DMA((n_copies,)),
                   *[pltpu.HBM(a_.shape, a_.dtype) for a_ in arrays], jax.ShapeDtypeStruct((8, 128), F32)),
        in_specs=[HBM_SPEC] * nb + [pl.BlockSpec(memory_space=pl.ANY)] * n_after,
        out_specs=(SEM_SPEC, SEM_SPEC, *[HBM_SPEC] * nb, pl.BlockSpec(memory_space=pltpu.VMEM)),
        input_output_aliases={i: 2 + i for i in range(nb)},
        compiler_params=pltpu.CompilerParams(has_side_effects=DATAFLOW))(*arrays, *([after] * n_after))
    return (out[0], out[1], list(out[2:2 + nb])), out[-1]


def _split_wait(name, handle, ns, n_copies, plan, after):
    send_sems, recv_sems, bufs = handle
    nb = len(bufs)

    def body(*refs):
        for cp in _remote_copies(plan, refs[:ns], refs[ns:nb], refs[nb], refs[nb + 1], n_copies):
            cp.wait_send()
            cp.wait_recv()

    out = pl.pallas_call(
        body, name=name, out_shape=[pltpu.HBM(b_.shape, b_.dtype) for b_ in bufs],
        in_specs=[HBM_SPEC] * nb + [SEM_SPEC, SEM_SPEC, pl.BlockSpec(memory_space=pl.ANY)],
        out_specs=[HBM_SPEC] * nb, input_output_aliases={i: i for i in range(nb)},
        compiler_params=pltpu.CompilerParams(has_side_effects=DATAFLOW))(*bufs, send_sems, recv_sems, after)
    return list(out[ns:])


def _gather_start(shards, tag, after=None):
    n = len(shards)
    lands = [lax.empty((4,) + s.shape, s.dtype) for s in shards]

    def plan(srcs, dsts, x, y, c):
        k = 2 * x + y
        copies = []
        for w_ref, o_ref in zip(srcs, dsts):
            h = w_ref.shape[0] // 2
            rows = pl.ds(c * h, h)
            copies += [(w_ref.at[rows], o_ref.at[k, rows], (_flip(x, dx), _flip(y, dy), c)) for dx, dy in CHIP_FLIPS]
        return copies

    handle, token = _split_start(f"gather_{tag}_start", shards, lands, 3 * n, plan, after)
    return (handle, plan, n), token


def _gather_wait(started, after, tag):
    handle, plan, n = started
    return _split_wait(f"gather_{tag}_wait", handle, n, 3 * n, plan, after)


def _gather_d2d(lands, before, tag):
    n = len(lands)

    def plan_d2d(in_refs, out_refs, x, y, c):
        remote = []
        for o_ref in out_refs:
            h = o_ref.shape[1] // 2
            for dx, dy in CHIP_FLIPS:
                half = o_ref.at[2 * _flip(x, dx) + _flip(y, dy), pl.ds(c * h, h)]
                remote.append((half, half, (x, y, 1 - c)))
        return remote, []

    return _comm(f"gather_{tag}_d2d", list(lands) + list(before),
                 [jax.ShapeDtypeStruct(l_.shape, l_.dtype) for l_ in lands], 3 * n, 0, plan_d2d,
                 aliases={i: i for i in range(n)})


def _pair_plan(in_refs, out_refs, x, y, c):
    return [(i_, o_, (x, y, 1 - c)) for i_, o_ in zip(in_refs, out_refs)], []


def _rs_start(grads, tag):
    n = len(grads)
    c = lax.axis_index("c")
    def rows(g, start, h):
        return g.rows(start, h) if isinstance(g, _WInGrad) else lax.dynamic_slice_in_dim(g, start, h, axis=1)

    halves = [g.shape[1] // 2 for g in grads]
    mine = [rows(g, c * h, h) for g, h in zip(grads, halves)]
    send_a = [rows(g, (1 - c) * h, h).astype(BF16) for g, h in zip(grads, halves)]
    recv_a = _comm(f"rs_pair_{tag}", send_a, [jax.ShapeDtypeStruct(s.shape, BF16) for s in send_a], n, 0, _pair_plan)
    pair, pair_b = [], []
    for i, (mi, ra) in enumerate(zip(mine, recv_a)):
        four, h, cols = mi.shape
        p32, p16 = _addn(mi.reshape(four * h, cols), [ra.reshape(four * h, cols)], name=f"rs_pair_sum_{tag}_{i}",
                         also_bf16=True)
        pair.append(p32.reshape(four, h, cols))
        pair_b.append(p16.reshape(four, h, cols))

    def plan(srcs, dsts, x, y, c_):
        copies = []
        for i_, o_ in zip(srcs, dsts):
            for j, (dx, dy) in enumerate(CHIP_FLIPS):
                fx, fy = _flip(x, dx), _flip(y, dy)
                copies.append((i_.at[2 * fx + fy], o_.at[j], (fx, fy, c_)))
        return copies

    lands = [lax.empty((3,) + p.shape[1:], BF16) for p in pair_b]
    handle, token = _split_start(f"rs_chips_{tag}_start", pair_b, lands, 3 * n, plan)
    return (handle, plan, n, pair), token


def _rs_finish(started, after, tag):
    handle, plan, n, pair = started
    recv_b = _split_wait(f"rs_chips_{tag}_wait", handle, n, 3 * n, plan, after)
    k = 2 * lax.axis_index("x") + lax.axis_index("y")
    tot = [_addn(lax.dynamic_index_in_dim(p, k, 0, keepdims=False), [rb[0], rb[1], rb[2]],
                 name=f"rs_chip_sum_{tag}_{i}") for i, (p, rb) in enumerate(zip(pair, recv_b))]
    other = _comm(f"rs_halves_{tag}", tot, [jax.ShapeDtypeStruct(t.shape, F32) for t in tot], n, 0, _pair_plan)
    return tot, other


def _gather_all(vec, *, name, before=()):
    out = jax.ShapeDtypeStruct((8,) + vec.shape, vec.dtype)

    def plan(in_refs, out_refs, x, y, c):
        me = 4 * x + 2 * y + c
        remote = [(in_refs[0], out_refs[0].at[me], (_flip(x, dx), _flip(y, dy), _flip(c, dc)))
                  for dx in (0, 1) for dy in (0, 1) for dc in (0, 1) if (dx, dy, dc) != (0, 0, 0)]
        return remote, [(in_refs[0], out_refs[0].at[me])]

    return _comm(name, [vec] + list(before), [out], 7, 1, plan)[0]


def _pack(parts):
    flat = [p.reshape(-1).astype(F32) for p in parts]
    total = sum(f.shape[0] for f in flat)
    n = -(-total // 1024) * 128
    vec = jnp.concatenate(flat + [jnp.zeros((8 * n - total,), F32)]).reshape(8, n)
    offs, o = [], 0
    for f in flat:
        offs.append((o, f.shape[0]))
        o += f.shape[0]
    return vec, offs


def _unpack(vec, offs, shapes):
    flat = vec.reshape(-1)
    return [flat[o:o + n].reshape(s) for (o, n), s in zip(offs, shapes)]


BIG = (("ffn1_w_gate_up", "col"), ("ffn1_w_down", "row"), ("w_in", "col"), ("w_out_a", "row"), ("w_out_ssm", "row"),
       ("w_mix_out", "row"), ("w_q", "row"), ("w_kv", "col"), ("w_o_x", "row"), ("ffn2_w_gate_up", "col"),
       ("ffn2_w_down", "row"))
SMALL = ("ffn1_norm", "mix_norm", "conv_a_w", "ssm_conv_w", "ssm_conv_b", "ssm_dt_bias", "ssm_a_log", "ssm_d",
         "ssm_norm", "xattn_norm", "mem_norm", "ffn2_norm", "final_norm")
WEIGHTS = ("ffn1_norm", "ffn1_w_gate_up", "ffn1_w_down", "mix_norm", "w_in", "conv_a_w", "w_out_a", "ssm_conv_w",
           "ssm_conv_b", "ssm_dt_bias", "ssm_a_log", "ssm_d", "ssm_norm", "w_out_ssm", "w_mix_out", "xattn_norm",
           "mem_norm", "w_q", "w_kv", "w_o_x", "ffn2_norm", "ffn2_w_gate_up", "ffn2_w_down", "final_norm")


GATHER_GROUPS = (("a", ("ffn1_w_gate_up",)), ("b", ("ffn1_w_down", "w_in")),
                 ("c", ("w_out_a", "w_out_ssm", "w_mix_out", "w_q", "w_kv", "w_o_x", "ffn2_w_gate_up", "ffn2_w_down")))


def _place_own(land, own, k, *, name):
    four, r, cols = land.shape
    tr = _row_tile(r, cols)

    def body(k_ref, own_ref, land_in, o_ref):
        del k_ref, land_in
        o_ref[...] = own_ref[...]

    return pl.pallas_call(
        body,
        grid_spec=pltpu.PrefetchScalarGridSpec(
            num_scalar_prefetch=1, grid=(r // tr,),
            in_specs=[pl.BlockSpec((tr, cols), lambda i, k_: (i, 0)), pl.BlockSpec(memory_space=pl.ANY)],
            out_specs=pl.BlockSpec((None, tr, cols), lambda i, k_: (k_[0], i, 0))),
        out_shape=jax.ShapeDtypeStruct(land.shape, land.dtype), input_output_aliases={2: 0},
        compiler_params=_cp("parallel"), name=name)(jnp.reshape(k, (1,)).astype(jnp.int32), own, land)


def _full_weight(land, own, kind, k, *, name):
    land = _place_own(land, own, k, name=name)
    four, r, cols = land.shape
    if kind == "row":
        return land.reshape(four * r, cols)
    return jnp.transpose(land, (1, 0, 2)).reshape(r, four * cols)


class _GatheredWeights:
    def __init__(self, shards32, k, after):
        first = GATHER_GROUPS[0][1]
        self.shards, self.k = {n: shards32[n].astype(BF16) for n in first}, k
        self.full = {}
        self.n_done = 0
        self.started, token = self._start(0, after)
        self.token = token[0, 0]
        self.shards.update({n: (w + token[0, 0]).astype(BF16) for n, w in shards32.items() if n not in first})
        self.after = jnp.stack([self.shards[n][0, 0] for n in shards32 if n not in first]).astype(F32).reshape(1, -1)

    def _start(self, gi, after):
        tag, names = GATHER_GROUPS[gi]
        return _gather_start([self.shards[n] for n in names], tag, after)

    def mark(self, value):
        self.after = value

    def __getitem__(self, name):
        if name not in self.full:
            tag, names = GATHER_GROUPS[self.n_done]
            assert name in names, (name, tag)
            lands = _gather_wait(self.started, self.after, tag)
            before = []
            if self.n_done + 1 < len(GATHER_GROUPS):
                self.started, token = self._start(self.n_done + 1, lands[0])
                before = [token]
            lands = _gather_d2d(lands, before, tag)
            for n, land in zip(names, lands):
                if n == "w_in":
                    self.full[n] = _pad_w_in_shards(_place_own(land, self.shards[n], self.k, name=f"own_{n}"))
                else:
                    self.full[n] = _full_weight(land, self.shards[n], dict(BIG)[n], self.k, name=f"own_{n}")
            self.n_done += 1
        return self.full[name]


def _shard_major(dw, kind):
    if isinstance(dw, tuple):
        return jnp.concatenate(dw, axis=0)
    if dw.ndim == 3:
        return dw
    if kind == "row":
        return dw.reshape(4, dw.shape[0] // 4, dw.shape[1])
    return jnp.transpose(dw.reshape(dw.shape[0], 4, dw.shape[1] // 4), (1, 0, 2))


def _pad_rows8(w):
    return jnp.concatenate([w, jnp.zeros((8 - w.shape[0], w.shape[1]), w.dtype)], axis=0)


def _group_lanes(v):
    r = v.shape[0]
    return jnp.pad(v.reshape(r, NG, NH // NG), ((0, 0), (0, 0), (0, 128 - NH // NG))).reshape(r, NG * 128)


def _ungroup_lanes(v):
    r = v.shape[0]
    return v.reshape(r, NG, 128)[:, :, :NH // NG].reshape(r, NH)


def _local_step(wfull, small, x, mem, target, token=0.0, on_grads=None):
    bl, s, _ = x.shape
    T = bl * s
    x2, t2 = x.reshape(T, D), target.reshape(T, D)
    mem2 = mem.reshape(-1, D)
    g = {}
    tok = [token]
    mark = getattr(wfull, "mark", lambda value: None)

    def gain(name):
        return small[name].reshape(1, -1) + tok[0]

    def emit(tag, names):
        if on_grads is not None:
            tok[0] = tok[0] + on_grads(tag, {n: g[n] for n in names})

    def ffn_fwd(h, n, wgu, wd, tag, next_gain=None):
        gate, up, a = _gate_up_fwd(n, wfull[wgu], name=f"{tag}_gate_up")
        mark(a)
        out = _mm(a, wfull[wd], "nn", tk=DFF, scale=FFN_RES, residual=h, norm_gain=next_gain, name=f"{tag}_down")
        return out, (n, gate, up, a)

    def ffn_bwd(dh, h, norm, wgu, wd, saved, tag):
        n, gate, up, a = saved
        dgate, dup = _act_bwd(dh, wfull[wd], gate, up, FFN_RES, name=f"{tag}_d_act")
        g[wd] = _mm(a, dh, "tn", tm=1408, scale=FFN_RES, name=f"{tag}_d_w_down")
        g[wgu] = (_mm(n, dgate, "tn", tn=1408, col_shards=2, name=f"{tag}_d_w_gate"),
                  _mm(n, dup, "tn", tn=1408, col_shards=2, name=f"{tag}_d_w_up"))
        emit(tag, (wgu, wd))
        dn = _mm(dgate, wfull[wgu], "nt", a2=dup, tm=512, tk=DFF, name=f"{tag}_d_norm_out")
        dh_in, g[norm] = _norm_bwd(h, gain(norm), dn, dh, name=f"{tag}_d_norm")
        return dh_in

    n1 = _norm_fwd(x2, gain("ffn1_norm"), name="ffn1_norm")
    (h1, u), ffn1_saved = ffn_fwd(x2, n1, "ffn1_w_gate_up", "ffn1_w_down", "ffn1", gain("mix_norm"))
    mark(h1)
    pp = _mm(u, wfull["w_in"], "nn", tm=2048, tn=1152, name="in_proj")
    wa8 = _pad_rows8(small["conv_a_w"])
    ws8 = _pad_rows8(small["ssm_conv_w"])
    conv_b = gain("ssm_conv_b")
    bias128 = jnp.pad(gain("ssm_dt_bias"), ((0, 0), (0, 128 - NH)))
    ya_pre = _conv_a_fwd(pp, wa8, bl, s, name="conv_a")
    xc = _conv_ssm_fwd(pp, ws8, conv_b, bl, s, name="conv_ssm")
    mark(xc)
    alog = gain("ssm_a_log")
    dtg, dtt = _dt_fwd(pp, bias128, name="dt")
    arow, acol = _group_lanes(alog), alog.reshape(NG, NH // NG, 1)
    dexp = jnp.repeat(gain("ssm_d"), HD, axis=1)
    ng = gain("ssm_norm")
    y, yn, prev = _ssd_fwd(xc, pp, dtg, dtt, arow, acol, dexp, ng, bl, s, name="ssd")
    ya = _mm(ya_pre, wfull["w_out_a"], "nn", tn=1024, out_dtype=BF16, name="out_a")
    yb = _mm(yn, wfull["w_out_ssm"], "nn", tn=1024, tk=DI, out_dtype=BF16, name="out_ssm")
    merged = _merge_fwd(pp, ya, yb, name="merge")
    h2, un = _mm(merged, wfull["w_mix_out"], "nn", residual=h1, norm_gain=gain("xattn_norm"), name="mix_out")
    q = _mm(un, wfull["w_q"], "nn", tn=1024, out_dtype=BF16, name="q_proj")
    mn = _norm_fwd(mem2, gain("mem_norm"), name="mem_norm")
    kv = _mm(mn, wfull["w_kv"], "nn", tn=1024, out_dtype=BF16, name="kv_proj")
    o = _attn_fwd(q, kv, bl, s, name="attn")
    h3, n2 = _mm(o, wfull["w_o_x"], "nn", residual=h2, norm_gain=gain("ffn2_norm"), name="attn_out")
    h4, ffn2_saved = ffn_fwd(h3, n2, "ffn2_w_gate_up", "ffn2_w_down", "ffn2")
    sq_err, dh4, dgf = _final_loss(h4, gain("final_norm"), t2, name="final_loss")
    g["final_norm"] = dgf

    dh3 = ffn_bwd(dh4, h3, "ffn2_norm", "ffn2_w_gate_up", "ffn2_w_down", ffn2_saved, "ffn2")
    do = _mm(dh3, wfull["w_o_x"], "nt", tn=1024, out_dtype=BF16, name="d_attn_o")
    g["w_o_x"] = _mm(o, dh3, "tn",name="d_w_o_x")
    dq, dk, dv = _attn_bwd(q, kv, do, bl, s, name="d_attn")
    dun = _mm(dq, wfull["w_q"], "nt", tn=1024, name="d_xattn_norm_out")
    g["w_q"] = _mm(un, dq, "tn",name="d_w_q")
    dkv = jnp.concatenate([dk, dv], axis=1)
    dmn = _mm(dkv, wfull["w_kv"], "nt", tn=1024, tk=2 * D, name="d_mem_norm_out")
    g["w_kv"] = _mm(mn, dkv, "tn", tn=512, col_shards=4, name="d_w_kv")
    emit("attn", ("w_q", "w_kv", "w_o_x"))
    _, g["mem_norm"] = _norm_bwd(mem2, gain("mem_norm"), dmn, None, name="d_mem_norm")
    dh2, g["xattn_norm"] = _norm_bwd(h2, gain("xattn_norm"), dun, dh3, name="d_xattn_norm")
    dmerged = _mm(dh2, wfull["w_mix_out"], "nt", tn=1024, name="d_merged")
    g["w_mix_out"] = _mm(merged, dh2, "tn",name="d_w_mix_out")
    dya, dyb, dpp = _merge_bwd(pp, ya, yb, dmerged, name="d_merge")
    dya_pre = _mm(dya, wfull["w_out_a"], "nt", tn=1024, name="d_conv_a_out")
    g["w_out_a"] = _mm(ya_pre, dya, "tn",name="d_w_out_a")
    dyn = _mm(dyb, wfull["w_out_ssm"], "nt", tn=DI, name="d_ssd_out")
    g["w_out_ssm"] = _mm(yn, dyb, "tn",name="d_w_out_ssm")
    dpp, dwa8 = _conv_a_bwd(pp, wa8, dya_pre, dpp, bl, s, name="d_conv_a")
    g["conv_a_w"] = dwa8[:3]
    dpp, dxs, dbm, dcm, ddtg, g["ssm_norm"], ddexp, dalg = _ssd_bwd(
        dyn, y, xc, pp, dtg, dtt, arow, acol, dexp, ng, prev, dpp, bl, s, name="d_ssd")
    g["ssm_d"] = ddexp.reshape(NH, HD).sum(axis=1).reshape(1, NH)
    g["ssm_a_log"] = _ungroup_lanes(dalg)
    conv_dw, conv_db = [], []
    for dpart, off, tag in ((dxs, 0, "x"), (dbm, DI, "b"), (dcm, DI + NG * NS, "c")):
        dpp, dw_, db_ = _conv_ssm_bwd(pp, ws8, conv_b, dpart, off, dpp, bl, s, name=f"d_conv_ssm_{tag}")
        conv_dw.append(dw_)
        conv_db.append(db_)
    g["ssm_conv_w"] = jnp.concatenate(conv_dw, axis=1)[:4]
    g["ssm_conv_b"] = jnp.concatenate(conv_db, axis=1)
    dpp, dbias = _dt_bwd(pp, bias128, ddtg, dpp, name="d_dt")
    g["ssm_dt_bias"] = dbias[:, :NH]
    g["w_in"] = _mm(u, dpp, "tn", tn=1152, name="d_w_in")
    emit("mix", ("w_in", "w_out_a", "w_out_ssm", "w_mix_out"))
    du = _mm(dpp, wfull["w_in"], "nt", tk=3456, name="d_mix_norm_out")
    dh1, g["mix_norm"] = _norm_bwd(h1, gain("mix_norm"), du, dh2, name="d_mix_norm")
    dx = ffn_bwd(dh1, x2, "ffn1_norm", "ffn1_w_gate_up", "ffn1_w_down", ffn1_saved, "ffn1")
    return sq_err, dx, g


W_IN_SHARD = NIN // 4


def _w_in_segments():
    segs, p = [], 0
    for t in range(D // CA_TILE):
        for which in range(3):
            segs.append((D * which + CA_TILE * t, p, CA_TILE))
            p += CA_TILE
    for s, n in ((3 * D, O_GA - 3 * D), (O_GA + NH, 2 * D), (O_GA, NH)):
        segs.append((s, p, n))
        p += n
    assert p == NIN and segs[-1][1] == O_DT and segs[-2][1] == O_GA
    return segs


def _pad_w_in_shards(land):
    pieces = []
    for s, _, n in _w_in_segments():
        while n > 0:
            kk, off = divmod(s, W_IN_SHARD)
            take = min(n, W_IN_SHARD - off)
            pieces.append(land[kk][:, off:off + take])
            s, n = s + take, n - take
    return jnp.concatenate(pieces + [jnp.zeros((land.shape[1], NPP - NIN), land.dtype)], axis=1)


class _WInGrad:
    def __init__(self, dwp):
        self.dwp = dwp
        self.shape = (4, dwp.shape[0], W_IN_SHARD)

    def rows(self, start, n):
        part = lax.dynamic_slice_in_dim(self.dwp, start, n, axis=0)
        shards = []
        for kk in range(4):
            n0, n1 = W_IN_SHARD * kk, W_IN_SHARD * (kk + 1)
            cuts = sorted((max(s, n0), p + max(s, n0) - s, min(s + m, n1) - max(s, n0))
                          for s, p, m in _w_in_segments() if min(s + m, n1) > max(s, n0))
            shards.append(jnp.concatenate([part[:, p:p + m] for _, p, m in cuts], axis=1))
        return jnp.stack(shards)


def _pad_w_in(w):
    return _pad_w_in_shards(jnp.stack(jnp.split(w, 4, axis=1)))


def _unpad_w_in(w):
    return jnp.concatenate(list(_WInGrad(w).rows(0, w.shape[0])), axis=1)


def kernel(x, mem, ffn1_norm, ffn1_w_gate_up, ffn1_w_down, mix_norm, w_in, conv_a_w, w_out_a, ssm_conv_w, ssm_conv_b, ssm_dt_bias, ssm_a_log, ssm_d, ssm_norm, w_out_ssm, w_mix_out, xattn_norm, mem_norm, w_q, w_kv, w_o_x, ffn2_norm, ffn2_w_gate_up, ffn2_w_down, final_norm, loss_target, m_ffn1_norm, m_ffn1_w_gate_up, m_ffn1_w_down, m_mix_norm, m_w_in, m_conv_a_w, m_w_out_a, m_ssm_conv_w, m_ssm_conv_b, m_ssm_dt_bias, m_ssm_a_log, m_ssm_d, m_ssm_norm, m_w_out_ssm, m_w_mix_out, m_xattn_norm, m_mem_norm, m_w_q, m_w_kv, m_w_o_x, m_ffn2_norm, m_ffn2_w_gate_up, m_ffn2_w_down, m_final_norm, v_ffn1_norm, v_ffn1_w_gate_up, v_ffn1_w_down, v_mix_norm, v_w_in, v_conv_a_w, v_w_out_a, v_ssm_conv_w, v_ssm_conv_b, v_ssm_dt_bias, v_ssm_a_log, v_ssm_d, v_ssm_norm, v_w_out_ssm, v_w_mix_out, v_xattn_norm, v_mem_norm, v_w_q, v_w_kv, v_w_o_x, v_ffn2_norm, v_ffn2_w_gate_up, v_ffn2_w_down, v_final_norm):
    a = dict(locals())
    xi, yi = lax.axis_index("x"), lax.axis_index("y")
    k = 2 * xi + yi

    conv_vec, conv_offs = _pack([a["conv_a_w"], a["ssm_conv_w"]])
    conv_all = _gather_all(conv_vec, name="gather_conv_w")
    wfull = _GatheredWeights({n: a[n][0] for n, _ in BIG}, k, conv_all)
    conv_sh = [_unpack(conv_all[2 * kk], conv_offs, [a["conv_a_w"].shape[1:], a["ssm_conv_w"].shape[1:]])
               for kk in range(4)]
    small = {n: a[n] for n in SMALL}
    small["conv_a_w"] = jnp.concatenate([cs[0] for cs in conv_sh], axis=1)
    small["ssm_conv_w"] = jnp.concatenate([cs[1] for cs in conv_sh], axis=1)

    rs_started = []

    def on_grads(tag, grads):
        names = [n for n, _ in BIG if n in grads]
        shard_major = [_WInGrad(grads[n]) if n == "w_in" else _shard_major(grads[n], dict(BIG)[n]) for n in names]
        st, tk = _rs_start(shard_major, tag)
        rs_started.append((tag, names, st))
        return tk[0, 0]

    sq_err, dx, g = _local_step(wfull, small, x, mem, loss_target, wfull.token, on_grads)
    loss = lax.psum(0.5 / D * jnp.sum(sq_err), ("x", "y", "c"))

    ci = lax.axis_index("c")
    out = {}

    def finish(tag, names, st, after):
        g_mine, g_other = _rs_finish(st, after, tag)
        for n, gm, go in zip(names, g_mine, g_other):
            res = _adamw_halves(a[n][0], gm, go, a["m_" + n][0], a["v_" + n][0], ci, name=f"adamw_{n}")
            out[n] = tuple(t.reshape(a[n].shape) for t in res)
        return res[1]

    done = dx
    for grp in rs_started[:-1]:
        done = finish(*grp, dx)

    full_shapes = [g[n].shape for n in SMALL]
    gvec, goffs = _pack([g[n] for n in SMALL])
    gsum = _sum_leading(_gather_all(gvec, name="gather_small_grads", before=[done]), name="sum_small_grads")
    finish(*rs_started[-1], gsum)
    gsmall = dict(zip(SMALL, _unpack(gsum, goffs, full_shapes)))
    for n in ("conv_a_w", "ssm_conv_w"):
        width = a[n].shape[2]
        gsmall[n] = lax.dynamic_slice_in_dim(gsmall[n], k * width, width, axis=1)
    local_shapes = [a[n].shape for n in SMALL]
    packs = [_pack([t[n] for n in SMALL]) for t in
             ({n: a[n] for n in SMALL}, gsmall, {n: a["m_" + n] for n in SMALL}, {n: a["v_" + n] for n in SMALL})]
    offs = packs[0][1]
    res = _adamw(*[p[0] for p in packs], name="adamw_small")
    unp = [_unpack(r, offs, local_shapes) for r in res]
    for i, n in enumerate(SMALL):
        out[n] = (gsmall[n].reshape(a[n].shape), unp[0][i], unp[1][i], unp[2][i])

    grad_x = dx.reshape(x.shape)
    return (loss, grad_x, *[out[n][0] for n in WEIGHTS], *[out[n][1] for n in WEIGHTS],
            *[out[n][2] for n in WEIGHTS], *[out[n][3] for n in WEIGHTS])
```

```python
import functools
import math

import jax
import jax.numpy as jnp
from jax import lax
from jax.experimental import pallas as pl
from jax.experimental.pallas import tpu as pltpu

F32 = jnp.float32
BF16 = jnp.bfloat16
MXU = jnp.bfloat16
HI = lax.Precision.HIGHEST

D = 1024
DFF = 2816
DI = 2048
NH, HD, NG, NS, CH = 32, 64, 4, 128, 128
GW = DI // NG
XH, XD = 4, 256
EPS = 1e-6
NEG = -1e30
CA_TILE = 256
O_CA, O_Z, O_XBC, O_GA, O_GB, O_DT, NPP = 0, 3072, 5120, 8192, 9216, 10240, 10368
NIN = 10272
FFN_RES = 0.5
ADAM_LR, ADAM_B1, ADAM_B2, ADAM_EPS, ADAM_WD, ADAM_STEP = 0.001, 0.9, 0.999, 1e-08, 0.01, 10
VMEM_LIMIT = 56 * 1024 * 1024
EPI_COLS = 256
SSD_EX = 4
MESH = pl.DeviceIdType.MESH
CHIP_FLIPS = ((1, 0), (0, 1), (1, 1))


def _cp(*sem):
    return pltpu.CompilerParams(dimension_semantics=sem, vmem_limit_bytes=VMEM_LIMIT)


def _tile(n, pref, align=128):
    if n <= pref:
        return n
    t = (pref // align) * align
    while t >= align:
        if n % t == 0:
            return t
        t -= align
    raise ValueError((n, pref))


def _dot(a, b, dims, prec=None):
    return lax.dot_general(a, b, (dims, ((), ())), preferred_element_type=F32, precision=prec)


def _nn(a, b, prec=None):
    return _dot(a, b, ((1,), (0,)), prec)


def _nt(a, b):
    return _dot(a, b, ((1,), (1,)))


def _tn(a, b):
    return _dot(a, b, ((0,), (0,)))


def _sig(x):
    return jax.nn.sigmoid(x)


def _mm(a, b, mode, *, name, tm=1024, tn=1024, tk=None, out_dtype=F32, scale=None, residual=None, col_shards=0,
        norm_gain=None):
    if tk is None:
        tk = 2048 if mode == "tn" else 1024
    if mode == "nn":
        (M, K), (K2, N) = a.shape, b.shape
    elif mode == "nt":
        (M, K), (N, K2) = a.shape, b.shape
    else:
        (K, M), (K2, N) = a.shape, b.shape
    assert K == K2, (name, a.shape, b.shape)
    tm, tn, tk = _tile(M, tm), _tile(N, tn), _tile(K, tk)
    nk = K // tk
    if mode == "nn":
        a_spec = pl.BlockSpec((tm, tk), lambda i, j, k: (i, k))
        b_spec = pl.BlockSpec((tk, tn), lambda i, j, k: (k, j))
        dims = ((1,), (0,))
    elif mode == "nt":
        a_spec = pl.BlockSpec((tm, tk), lambda i, j, k: (i, k))
        b_spec = pl.BlockSpec((tn, tk), lambda i, j, k: (j, k))
        dims = ((1,), (1,))
    else:
        a_spec = pl.BlockSpec((tk, tm), lambda i, j, k: (k, i))
        b_spec = pl.BlockSpec((tk, tn), lambda i, j, k: (k, j))
        dims = ((0,), (0,))
    o_spec = pl.BlockSpec((tm, tn), lambda i, j, k: (i, j))
    out_spec, out_shape = o_spec, jax.ShapeDtypeStruct((M, N), out_dtype)
    if col_shards:
        per = N // col_shards // tn
        assert per * tn * col_shards == N, (name, N, tn, col_shards)
        out_spec = pl.BlockSpec((None, tm, tn), lambda i, j, k: (j // per, i, j % per))
        out_shape = jax.ShapeDtypeStruct((col_shards, M, N // col_shards), out_dtype)
    has_res = residual is not None
    has_norm = norm_gain is not None
    assert not has_norm or (tn == N and not col_shards)
    n_in = 2 + has_res + has_norm

    def body(*refs):
        a_ref, b_ref = refs[0], refs[1]
        o_ref = refs[n_in]

        def finish(acc):
            if scale is not None:
                acc = acc * scale
            if has_res:
                acc = acc + refs[2][...]
            o_ref[...] = acc.astype(out_dtype)
            if has_norm:
                rs = lax.rsqrt(jnp.mean(acc * acc, axis=-1, keepdims=True) + EPS)
                refs[n_in + 1][...] = (acc * rs * refs[n_in - 1][...]).astype(BF16)

        part = _dot(a_ref[...].astype(MXU), b_ref[...].astype(MXU), dims)
        if nk == 1:
            finish(part)
            return
        acc_ref = refs[-1]
        k = pl.program_id(2)

        @pl.when(k == 0)
        def _():
            acc_ref[...] = part

        @pl.when(k > 0)
        def _():
            acc_ref[...] += part

        @pl.when(k == nk - 1)
        def _():
            finish(acc_ref[...])

    ins, in_specs = [a, b], [a_spec, b_spec]
    if has_res:
        ins.append(residual)
        in_specs.append(o_spec)
    if has_norm:
        ins.append(norm_gain)
        in_specs.append(pl.BlockSpec((1, tn), lambda i, j, k: (0, j)))
        out_spec, out_shape = [out_spec, o_spec], [out_shape, jax.ShapeDtypeStruct((M, N), BF16)]
    return pl.pallas_call(
        body, grid=(M // tm, N // tn, nk), in_specs=in_specs, out_specs=out_spec, out_shape=out_shape,
        scratch_shapes=[pltpu.VMEM((tm, tn), F32)] if nk > 1 else [],
        compiler_params=_cp("parallel", "parallel", "arbitrary"), name=name)(*ins)


def _norm_fwd(x, g, *, name):
    T, d = x.shape
    tr = _tile(T, 512, 8)

    def body(x_ref, g_ref, o_ref):
        xv = x_ref[...]
        r = lax.rsqrt(jnp.mean(xv * xv, axis=-1, keepdims=True) + EPS)
        o_ref[...] = (xv * r * g_ref[...]).astype(BF16)

    return pl.pallas_call(
        body, grid=(T // tr,),
        in_specs=[pl.BlockSpec((tr, d), lambda i: (i, 0)), pl.BlockSpec((1, d), lambda i: (0, 0))],
        out_specs=pl.BlockSpec((tr, d), lambda i: (i, 0)),
        out_shape=jax.ShapeDtypeStruct((T, d), BF16), compiler_params=_cp("parallel"), name=name)(x, g)


def _norm_bwd(x, g, dn, dres, *, name):
    T, d = x.shape
    tr = _tile(T, 512, 8)
    has_res = dres is not None

    def body(*refs):
        x_ref, g_ref, dn_ref = refs[:3]
        dr_ref = refs[3] if has_res else None
        dx_ref, dg_ref = refs[-2], refs[-1]

        @pl.when(pl.program_id(0) == 0)
        def _():
            dg_ref[...] = jnp.zeros_like(dg_ref)

        xv = x_ref[...]
        dnv = dn_ref[...].astype(F32)
        r = lax.rsqrt(jnp.mean(xv * xv, axis=-1, keepdims=True) + EPS)
        xh = xv * r
        dg_ref[...] += jnp.sum(dnv * xh, axis=0, keepdims=True)
        dxh = dnv * g_ref[...]
        dx = r * (dxh - xh * jnp.mean(dxh * xh, axis=-1, keepdims=True))
        if has_res:
            dx = dx + dr_ref[...]
        dx_ref[...] = dx

    row = pl.BlockSpec((tr, d), lambda i: (i, 0))
    vec = pl.BlockSpec((1, d), lambda i: (0, 0))
    ins = [x, g, dn] + ([dres] if has_res else [])
    return pl.pallas_call(
        body, grid=(T // tr,), in_specs=[row, vec, row] + ([row] if has_res else []),
        out_specs=[row, vec],
        out_shape=[jax.ShapeDtypeStruct((T, d), F32), jax.ShapeDtypeStruct((1, d), F32)],
        compiler_params=_cp("arbitrary"), name=name)(*ins)


def _final_loss(h, g, target, *, name):
    T, d = h.shape
    tr = _tile(T, 512, 8)

    def body(h_ref, g_ref, t_ref, l_ref, dh_ref, dg_ref):
        @pl.when(pl.program_id(0) == 0)
        def _():
            l_ref[...] = jnp.zeros_like(l_ref)
            dg_ref[...] = jnp.zeros_like(dg_ref)

        xv = h_ref[...]
        r = lax.rsqrt(jnp.mean(xv * xv, axis=-1, keepdims=True) + EPS)
        xh = xv * r
        e = xh * g_ref[...] - t_ref[...]
        l_ref[...] += jnp.sum(e * e, axis=0, keepdims=True)
        dy = e * (1.0 / d)
        dg_ref[...] += jnp.sum(dy * xh, axis=0, keepdims=True)
        dxh = dy * g_ref[...]
        dh_ref[...] = r * (dxh - xh * jnp.mean(dxh * xh, axis=-1, keepdims=True))

    row = pl.BlockSpec((tr, d), lambda i: (i, 0))
    vec = pl.BlockSpec((1, d), lambda i: (0, 0))
    return pl.pallas_call(
        body, grid=(T // tr,), in_specs=[row, vec, row], out_specs=[vec, row, vec],
        out_shape=[jax.ShapeDtypeStruct((1, d), F32), jax.ShapeDtypeStruct((T, d), F32),
                   jax.ShapeDtypeStruct((1, d), F32)],
        compiler_params=_cp("arbitrary"), name=name)(h, g, target)


def _shard_chunks(width):
    return [(o, min(EPI_COLS, width - o)) for o in range(0, width, EPI_COLS)]


def _gate_up_fwd(n, wsh, *, name):
    T, d = n.shape
    ws = wsh.shape[2]
    f = 2 * ws
    tm = _tile(T, 512, 8)

    def body(n_ref, wg_ref, wu_ref, g_ref, u_ref, a_ref):
        nv = n_ref[...].astype(MXU)
        for sh in range(2):
            for off, size in _shard_chunks(ws):
                gv = _nn(nv, wg_ref[sh, :, off:off + size].astype(MXU))
                uv = _nn(nv, wu_ref[sh, :, off:off + size].astype(MXU))
                sl = slice(sh * ws + off, sh * ws + off + size)
                g_ref[:, sl] = gv.astype(BF16)
                u_ref[:, sl] = uv.astype(BF16)
                a_ref[:, sl] = (gv * _sig(gv) * uv).astype(BF16)

    out = pl.BlockSpec((tm, f), lambda i: (i, 0))
    act = jax.ShapeDtypeStruct((T, f), BF16)
    return pl.pallas_call(
        body, grid=(T // tm,),
        in_specs=[pl.BlockSpec((tm, d), lambda i: (i, 0)), pl.BlockSpec((2, d, ws), lambda i: (0, 0, 0)),
                  pl.BlockSpec((2, d, ws), lambda i: (1, 0, 0))],
        out_specs=[out, out, out], out_shape=[act, act, act], compiler_params=_cp("parallel"),
        name=name)(n, wsh, wsh)


def _gate_up_bwd_input(dgate, dup, wsh, *, name):
    T, f = dgate.shape
    four, d, ws = wsh.shape
    tm = _tile(T, 512, 8)

    def body(dg_ref, du_ref, w_ref, o_ref):
        acc = None
        for sh in range(four):
            src = dg_ref if sh < 2 else du_ref
            part = _nt(src[:, (sh % 2) * ws:(sh % 2 + 1) * ws].astype(MXU), w_ref[sh].astype(MXU))
            acc = part if acc is None else acc + part
        o_ref[...] = acc

    act = pl.BlockSpec((tm, f), lambda i: (i, 0))
    return pl.pallas_call(
        body, grid=(T // tm,),
        in_specs=[act, act, pl.BlockSpec((four, d, ws), lambda i: (0, 0, 0))],
        out_specs=pl.BlockSpec((tm, d), lambda i: (i, 0)), out_shape=jax.ShapeDtypeStruct((T, d), F32),
        compiler_params=_cp("parallel"), name=name)(dgate, dup, wsh)


def _act_bwd(dh, wd, gate, up, scale, *, name):
    T, d = dh.shape
    f = wd.shape[0]
    tm, tn = _tile(T, 512, 8), _tile(f, DFF)

    tc = _tile(tn, EPI_COLS)

    def body(dh_ref, wd_ref, g_ref, u_ref, dg_ref, du_ref):
        dhv = dh_ref[...].astype(MXU)
        for j in range(tn // tc):
            sl = slice(j * tc, (j + 1) * tc)
            da = scale * _nt(dhv, wd_ref[sl, :].astype(MXU))
            gv, uv = g_ref[:, sl].astype(F32), u_ref[:, sl].astype(F32)
            s = _sig(gv)
            dg_ref[:, sl] = (da * uv * (s * (1.0 + gv * (1.0 - s)))).astype(BF16)
            du_ref[:, sl] = (da * (gv * s)).astype(BF16)

    tile = pl.BlockSpec((tm, tn), lambda i, j: (i, j))
    act = jax.ShapeDtypeStruct((T, f), BF16)
    return pl.pallas_call(
        body, grid=(T // tm, f // tn),
        in_specs=[pl.BlockSpec((tm, d), lambda i, j: (i, 0)), pl.BlockSpec((tn, d), lambda i, j: (j, 0)), tile, tile],
        out_specs=[tile, tile], out_shape=[act, act], compiler_params=_cp("parallel", "parallel"),
        name=name)(dh, wd, gate, up)


CONV_ROWS = 64
CONV_PAD = 8


def _rows_down(ref, r0, d, cols=slice(None)):
    if r0 - d >= 0:
        return ref[pl.ds(r0 - d, CONV_ROWS), cols]
    assert r0 == 0
    v = ref[pl.ds(0, CONV_ROWS), cols]
    ri = lax.broadcasted_iota(jnp.int32, v.shape, 0)
    return jnp.where(ri >= d, pltpu.roll(v, d, 0), 0.0)


def _fold8(v):
    return jnp.sum(v.reshape(CONV_ROWS // 8, 8, v.shape[1]), axis=0)


def _taps(w_ref, views):
    acc = None
    for k, v in enumerate(views):
        t = w_ref[k:k + 1, :] * v
        acc = t if acc is None else acc + t
    return acc


def _conv_a_fwd(pp, w8, bl, s, *, name):
    tc = CA_TILE
    nb = D // tc
    bcol, ccol, vcol = slice(0, tc), slice(tc, 2 * tc), slice(2 * tc, 3 * tc)

    def body(p_ref, w_ref, o_ref):
        for r0 in range(0, s, CONV_ROWS):
            cv = [_rows_down(p_ref, r0, 2 - k, ccol) * _rows_down(p_ref, r0, 2 - k, vcol) for k in range(3)]
            o_ref[pl.ds(r0, CONV_ROWS), :] = (p_ref[pl.ds(r0, CONV_ROWS), bcol] * _taps(w_ref, cv)).astype(BF16)

    return pl.pallas_call(
        body, grid=(bl, nb),
        in_specs=[pl.BlockSpec((s, 3 * tc), lambda b, j: (b, j)), pl.BlockSpec((8, tc), lambda b, j: (0, j))],
        out_specs=pl.BlockSpec((s, tc), lambda b, j: (b, j)),
        out_shape=jax.ShapeDtypeStruct((bl * s, D), BF16), compiler_params=_cp("parallel", "parallel"),
        name=name)(pp, w8)


def _conv_a_bwd(pp, w8, dya, dpp, bl, s, *, name):
    tc = CA_TILE
    nb = D // tc
    bcol, ccol, vcol = slice(0, tc), slice(tc, 2 * tc), slice(2 * tc, 3 * tc)

    def body(p_ref, w_ref, dy_ref, dpp_in, d_ref, dw_ref, dcp):
        del dpp_in

        @pl.when(pl.program_id(1) == 0)
        def _():
            dw_ref[...] = jnp.zeros_like(dw_ref)

        dcp[pl.ds(s, CONV_PAD), :] = jnp.zeros((CONV_PAD, tc), F32)
        dw_acc = [jnp.zeros((8, tc), F32) for _ in range(3)]
        for r0 in reversed(range(0, s, CONV_ROWS)):
            rows = pl.ds(r0, CONV_ROWS)
            cs = [_rows_down(p_ref, r0, 2 - k, ccol) for k in range(3)]
            vs = [_rows_down(p_ref, r0, 2 - k, vcol) for k in range(3)]
            cv = [c_ * v_ for c_, v_ in zip(cs, vs)]
            dy = dy_ref[rows, :]
            d_ref[rows, bcol] = (dy * _taps(w_ref, cv)).astype(BF16)
            dconv = dy * p_ref[rows, bcol]
            dcp[rows, :] = dconv
            dcv = _taps(w_ref, [dcp[pl.ds(r0 + 2, CONV_ROWS), :], dcp[pl.ds(r0 + 1, CONV_ROWS), :], dconv])
            d_ref[rows, ccol] = (dcv * vs[2]).astype(BF16)
            d_ref[rows, vcol] = (dcv * cs[2]).astype(BF16)
            dw_acc = [acc + _fold8(dconv * cv_) for acc, cv_ in zip(dw_acc, cv)]
        for k in range(3):
            dw_ref[k:k + 1, :] += jnp.sum(dw_acc[k], axis=0, keepdims=True)

    wspec = pl.BlockSpec((8, tc), lambda j, b: (0, j))
    wide = pl.BlockSpec((s, 3 * tc), lambda j, b: (b, j))
    return pl.pallas_call(
        body, grid=(nb, bl),
        in_specs=[wide, wspec, pl.BlockSpec((s, tc), lambda j, b: (b, j)), pl.BlockSpec(memory_space=pl.ANY)],
        out_specs=[wide, wspec], out_shape=[jax.ShapeDtypeStruct(dpp.shape, dpp.dtype), jax.ShapeDtypeStruct((8, D), F32)],
        scratch_shapes=[pltpu.VMEM((s + CONV_PAD, tc), F32)], input_output_aliases={3: 0},
        compiler_params=_cp("parallel", "arbitrary"), name=name)(pp, w8, dya, dpp)


def _conv_ssm_fwd(pp, w8, bias, bl, s, *, name):
    tc = 256
    width = DI + 2 * NG * NS
    nb = width // tc

    def body(x_ref, w_ref, b_ref, o_ref):
        for r0 in range(0, s, CONV_ROWS):
            pre = _taps(w_ref, [_rows_down(x_ref, r0, 3 - k) for k in range(4)]) + b_ref[...]
            o_ref[pl.ds(r0, CONV_ROWS), :] = pre * _sig(pre)

    return pl.pallas_call(
        body, grid=(bl, nb),
        in_specs=[pl.BlockSpec((s, tc), lambda b, j: (b, O_XBC // tc + j)),
                  pl.BlockSpec((8, tc), lambda b, j: (0, j)), pl.BlockSpec((1, tc), lambda b, j: (0, j))],
        out_specs=pl.BlockSpec((s, tc), lambda b, j: (b, j)),
        out_shape=jax.ShapeDtypeStruct((bl * s, width), F32), compiler_params=_cp("parallel", "parallel"),
        name=name)(pp, w8, bias)


def _conv_ssm_bwd(pp, w8, bias, dxc, ch_off, dpp, bl, s, *, name):
    n = dxc.shape[1]
    tc = 256
    nb = n // tc
    o0 = ch_off // tc

    def body(x_ref, w_ref, b_ref, d_ref, dpp_in, dx_ref, dw_ref, db_ref, dp):
        del dpp_in

        @pl.when(pl.program_id(1) == 0)
        def _():
            dw_ref[...] = jnp.zeros_like(dw_ref)
            db_ref[...] = jnp.zeros_like(db_ref)

        dp[pl.ds(s, CONV_PAD), :] = jnp.zeros((CONV_PAD, tc), F32)
        dw_acc = [jnp.zeros((8, tc), F32) for _ in range(4)]
        db_acc = jnp.zeros((8, tc), F32)
        for r0 in reversed(range(0, s, CONV_ROWS)):
            rows = pl.ds(r0, CONV_ROWS)
            xs = [_rows_down(x_ref, r0, 3 - k) for k in range(4)]
            pre = _taps(w_ref, xs) + b_ref[...]
            sg = _sig(pre)
            dpre = d_ref[rows, :] * (sg * (1.0 + pre * (1.0 - sg)))
            dp[rows, :] = dpre
            dx = _taps(w_ref, [dp[pl.ds(r0 + 3 - k, CONV_ROWS), :] for k in range(3)] + [dpre])
            dx_ref[rows, :] = dx.astype(BF16)
            db_acc = db_acc + _fold8(dpre)
            dw_acc = [acc + _fold8(dpre * x_) for acc, x_ in zip(dw_acc, xs)]
        db_ref[...] += jnp.sum(db_acc, axis=0, keepdims=True)
        for k in range(4):
            dw_ref[k:k + 1, :] += jnp.sum(dw_acc[k], axis=0, keepdims=True)

    return pl.pallas_call(
        body, grid=(nb, bl),
        in_specs=[pl.BlockSpec((s, tc), lambda j, b: (b, O_XBC // tc + o0 + j)),
                  pl.BlockSpec((8, tc), lambda j, b: (0, o0 + j)), pl.BlockSpec((1, tc), lambda j, b: (0, o0 + j)),
                  pl.BlockSpec((s, tc), lambda j, b: (b, j)), pl.BlockSpec(memory_space=pl.ANY)],
        out_specs=[pl.BlockSpec((s, tc), lambda j, b: (b, O_XBC // tc + o0 + j)),
                   pl.BlockSpec((8, tc), lambda j, b: (0, j)), pl.BlockSpec((1, tc), lambda j, b: (0, j))],
        out_shape=[jax.ShapeDtypeStruct(dpp.shape, dpp.dtype), jax.ShapeDtypeStruct((8, n), F32),
                   jax.ShapeDtypeStruct((1, n), F32)],
        scratch_shapes=[pltpu.VMEM((s + CONV_PAD, tc), F32)], input_output_aliases={4: 0},
        compiler_params=_cp("parallel", "arbitrary"), name=name)(pp, w8, bias, dxc, dpp)


def _softplus(x):
    return jnp.maximum(x, 0.0) + jnp.log1p(jnp.exp(-jnp.abs(x)))


def _head_group_matrix():
    h = jnp.arange(128)[:, None]
    j = jnp.arange(NG * 128)[None, :]
    per = NH // NG
    return ((h < NH) & (j == (h // per) * 128 + h % per)).astype(F32)


def _dt_fwd(pp, bias128, *, name):
    T = pp.shape[0]
    tr = _tile(T, 1024, 8)
    per = NH // NG

    def body(x_ref, b_ref, p_ref, g_ref, t_ref):
        lane = lax.broadcasted_iota(jnp.int32, (tr, 128), 1)
        dt = jnp.where(lane < NH, _softplus(x_ref[...] + b_ref[...]), 0.0)
        g_ref[...] = _nn(dt, p_ref[...], HI)
        eye = (lax.broadcasted_iota(jnp.int32, (NH, 128), 0)
               == lax.broadcasted_iota(jnp.int32, (NH, 128), 1)).astype(F32)
        t_ref[...] = _dot(eye, dt, ((1,), (1,)), HI).reshape(NG, per, tr)

    vec = pl.BlockSpec((1, 128), lambda i: (0, 0))
    return pl.pallas_call(
        body, grid=(T // tr,),
        in_specs=[pl.BlockSpec((tr, 128), lambda i: (i, O_DT // 128)), vec,
                  pl.BlockSpec((128, NG * 128), lambda i: (0, 0))],
        out_specs=[pl.BlockSpec((tr, NG * 128), lambda i: (i, 0)), pl.BlockSpec((NG, per, tr), lambda i: (0, 0, i))],
        out_shape=[jax.ShapeDtypeStruct((T, NG * 128), F32), jax.ShapeDtypeStruct((NG, per, T), F32)],
        compiler_params=_cp("parallel"), name=name)(pp, bias128, _head_group_matrix())


def _dt_bwd(pp, bias128, ddtg, dpp, *, name):
    T = pp.shape[0]
    tr = _tile(T, 1024, 8)

    def body(x_ref, b_ref, d_ref, p_ref, dpp_in, o_ref, db_ref):
        del dpp_in

        @pl.when(pl.program_id(0) == 0)
        def _():
            db_ref[...] = jnp.zeros_like(db_ref)

        lane = lax.broadcasted_iota(jnp.int32, (tr, 128), 1)
        ddt = _dot(d_ref[...], p_ref[...], ((1,), (1,)), HI)
        dr = jnp.where(lane < NH, ddt * _sig(x_ref[...] + b_ref[...]), 0.0)
        db_ref[...] += jnp.sum(dr, axis=0, keepdims=True)
        o_ref[...] = dr.astype(BF16)

    col = pl.BlockSpec((tr, 128), lambda i: (i, O_DT // 128))
    vec = pl.BlockSpec((1, 128), lambda i: (0, 0))
    return pl.pallas_call(
        body, grid=(T // tr,),
        in_specs=[col, vec, pl.BlockSpec((tr, NG * 128), lambda i: (i, 0)), pl.BlockSpec((128, NG * 128), lambda i: (0, 0)),
                  pl.BlockSpec(memory_space=pl.ANY)],
        out_specs=[col, vec],
        out_shape=[jax.ShapeDtypeStruct(dpp.shape, dpp.dtype), jax.ShapeDtypeStruct((1, 128), F32)],
        input_output_aliases={4: 0}, compiler_params=_cp("arbitrary"),
        name=name)(pp, bias128, ddtg, _head_group_matrix(), dpp)


def _tril():
    return lax.broadcasted_iota(jnp.int32, (CH, CH), 0) >= lax.broadcasted_iota(jnp.int32, (CH, CH), 1)


def _ssd_common(dt, dtt, arow, acol):
    ri = lax.broadcasted_iota(jnp.int32, (CH, CH), 0)
    ci = lax.broadcasted_iota(jnp.int32, (CH, CH), 1)
    tril = ri >= ci
    triu = ri <= ci
    acs_col = _nn(tril.astype(F32), dt * arow, HI)
    acs_row = _nn(dtt * acol, triu.astype(F32), HI)
    return tril, triu, acs_col, acs_row


def _pair_terms(q, dt, acs_col, acs_row, tril, lo):
    ha, hb = 2 * q, 2 * q + 1
    col_a, col_b = acs_col[:, ha:ha + 1], acs_col[:, hb:hb + 1]
    row_a, row_b = acs_row[ha:ha + 1, :], acs_row[hb:hb + 1, :]
    last_a, last_b = acs_col[CH - 1:CH, ha:ha + 1], acs_col[CH - 1:CH, hb:hb + 1]
    out = dict(
        dtsel=jnp.where(lo, dt[:, ha:ha + 1], dt[:, hb:hb + 1]),
        d_a=jnp.exp(jnp.where(tril, col_a - row_a, NEG)), d_b=jnp.exp(jnp.where(tril, col_b - row_b, NEG)),
        esel=jnp.where(lo, jnp.exp(col_a), jnp.exp(col_b)),
        fsel=jnp.where(lo, jnp.exp(last_a - col_a), jnp.exp(last_b - col_b)),
        g_a=jnp.exp(last_a), g_b=jnp.exp(last_b))
    return out


def _ssd_fwd(xc, pp, dtg, dtt, arow, acol, dexp, ng, bl, s, *, name):
    nc = s // CH
    T = bl * s

    ex = SSD_EX if bl % SSD_EX == 0 else 1

    def body(*refs):
        arow_ref, acol_ref, dexp_ref, ng_ref = refs[6 * ex:6 * ex + 4]
        st_ref = refs[-1]

        @pl.when(pl.program_id(2) == 0)
        def _():
            st_ref[...] = jnp.zeros_like(st_ref)

        y_ref, yn_ref, prev_ref = refs[6 * ex + 4:6 * ex + 7]
        for e in range(ex):
            one(*refs[6 * e:6 * e + 6], arow_ref, acol_ref, dexp_ref, ng_ref,
                y_ref.at[e], yn_ref.at[e], prev_ref.at[e], st_ref.at[e])

    def one(xs_ref, bm_ref, cm_ref, z_ref, dt_ref, dtt_ref, arow_ref, acol_ref, dexp_ref, ng_ref,
            y_ref, yn_ref, prev_ref, st_ref):
        dt = dt_ref[...]
        tril, _, acs_col, acs_row = _ssd_common(dt, dtt_ref[...], -jnp.exp(arow_ref[...]), -jnp.exp(acol_ref[...]))
        bm, cm = bm_ref[...].astype(MXU), cm_ref[...].astype(MXU)
        cb = _nt(cm, bm)
        lo = lax.broadcasted_iota(jnp.int32, (CH, 128), 1) < HD
        sub_lo = lax.broadcasted_iota(jnp.int32, (128, NS), 0) < HD
        for q in range(4):
            t = _pair_terms(q, dt, acs_col, acs_row, tril, lo)
            x = xs_ref[:, 128 * q:128 * (q + 1)]
            xd = x * t["dtsel"]
            y = (_nn((cb * t["d_a"]).astype(MXU), jnp.where(lo, xd, 0.0).astype(MXU))
                 + _nn((cb * t["d_b"]).astype(MXU), jnp.where(lo, 0.0, xd).astype(MXU)))
            prev = st_ref[q]
            prev_ref[q] = prev
            y = y + t["esel"] * _nt(cm, prev.astype(MXU))
            st_ref[q] = prev * jnp.where(sub_lo, t["g_a"], t["g_b"]) + _tn((xd * t["fsel"]).astype(MXU), bm)
            y_ref[:, 128 * q:128 * (q + 1)] = y + dexp_ref[:, 128 * q:128 * (q + 1)] * x
        zv = z_ref[...]
        yg = y_ref[...] * (zv * _sig(zv))
        r = lax.rsqrt(jnp.mean(yg * yg, axis=-1, keepdims=True) + EPS)
        yn_ref[...] = (yg * r * ng_ref[...]).astype(BF16)

    def row(e, width, off_blocks):
        return pl.BlockSpec((CH, width), lambda g, b, c: ((b * ex + e) * nc + c, off_blocks + g))

    per_ex_in = [[row(e, GW, 0), row(e, NS, DI // NS), row(e, NS, DI // NS + NG), row(e, GW, O_Z // GW), row(e, 128, 0),
                  pl.BlockSpec((None, 8, CH), lambda g, b, c, e=e: (g, 0, (b * ex + e) * nc + c))] for e in range(ex)]
    by_example = pl.BlockSpec((ex, CH, GW), lambda g, b, c: (b, c, g))
    y, yn, prev = pl.pallas_call(
        body, grid=(NG, bl // ex, nc),
        in_specs=sum(per_ex_in, []) + [pl.BlockSpec((1, 128), lambda g, b, c: (0, g)),
                                       pl.BlockSpec((None, 8, 1), lambda g, b, c: (g, 0, 0)),
                                       pl.BlockSpec((1, GW), lambda g, b, c: (0, g)),
                                       pl.BlockSpec((1, GW), lambda g, b, c: (0, g))],
        out_specs=[by_example, by_example,
                   pl.BlockSpec((ex, None, 4, 128, NS), lambda g, b, c: (b, c, g, 0, 0))],
        out_shape=[jax.ShapeDtypeStruct((bl, s, DI), F32), jax.ShapeDtypeStruct((bl, s, DI), BF16),
                   jax.ShapeDtypeStruct((bl, nc, 16, 128, NS), F32)],
        scratch_shapes=[pltpu.VMEM((ex, 4, 128, NS), F32)],
        compiler_params=_cp("parallel", "parallel", "arbitrary"), name=name,
    )(*([xc, xc, xc, pp, dtg, dtt] * ex), arow, acol, dexp, ng)
    return y.reshape(T, DI), yn.reshape(T, DI), prev.reshape(bl * nc, 16, 128, NS)


def _ssd_bwd(dyn, y, xc, pp, dtg, dtt, arow, acol, dexp, ng, prev, dpp, bl, s, *, name):
    nc = s // CH
    T = bl * s

    def rsum(v):
        return jnp.sum(v, axis=1, keepdims=True)

    def asum(v):
        return jnp.sum(jnp.sum(v, axis=0, keepdims=True), axis=1, keepdims=True)

    ex = 1

    def body(*refs):
        shared = refs[9 * ex:9 * ex + 4]
        dz_ref, dxs_ref, db_ref, dc_ref, ddt_ref, dng_ref, dd_ref, dal_ref = refs[9 * ex + 5:9 * ex + 13]
        dst_ref = refs[-1]

        @pl.when((pl.program_id(1) == 0) & (pl.program_id(2) == 0))
        def _():
            dng_ref[...] = jnp.zeros_like(dng_ref)
            dd_ref[...] = jnp.zeros_like(dd_ref)
            dal_ref[...] = jnp.zeros_like(dal_ref)

        @pl.when(pl.program_id(2) == 0)
        def _():
            dst_ref[...] = jnp.zeros_like(dst_ref)

        for e in range(ex):
            one(*refs[9 * e:9 * e + 8], *shared, refs[9 * e + 8], dz_ref.at[e], dxs_ref.at[e], db_ref.at[e],
                dc_ref.at[e], ddt_ref.at[e], dng_ref, dd_ref, dal_ref, dst_ref.at[e])

    def one(dyn_ref, y_ref, xs_ref, bm_ref, cm_ref, z_ref, dt_ref, dtt_ref, arow_ref, acol_ref, dexp_ref, ng_ref,
            prev_ref, dz_ref, dxs_ref, db_ref, dc_ref, ddt_ref, dng_ref, dd_ref, dal_ref, dst_ref):
        yv, zv, xsv, dexp_v = y_ref[...], z_ref[...], xs_ref[...], dexp_ref[...]
        sz = _sig(zv)
        silu = zv * sz
        yg = yv * silu
        r = lax.rsqrt(jnp.mean(yg * yg, axis=-1, keepdims=True) + EPS)
        yh = yg * r
        dynv = dyn_ref[...]
        dng_ref[...] += jnp.sum(dynv * yh, axis=0, keepdims=True)
        dyh = dynv * ng_ref[...]
        dyg = r * (dyh - yh * jnp.mean(dyh * yh, axis=-1, keepdims=True))
        dz_ref[...] = (dyg * yv * (sz * (1.0 + zv * (1.0 - sz)))).astype(BF16)
        dy_all = dyg * silu
        dd_ref[...] += jnp.sum(dy_all * xsv, axis=0, keepdims=True)

        dt = dt_ref[...]
        arow_v = -jnp.exp(arow_ref[...])
        tril, triu, acs_col, acs_row = _ssd_common(dt, dtt_ref[...], arow_v, -jnp.exp(acol_ref[...]))
        bm, cm = bm_ref[...].astype(MXU), cm_ref[...].astype(MXU)
        cb = _nt(cm, bm)
        lane = lax.broadcasted_iota(jnp.int32, (CH, 128), 1)
        is_last = lax.broadcasted_iota(jnp.int32, (CH, 128), 0) == CH - 1
        lo = lane < HD
        sub_lo = lax.broadcasted_iota(jnp.int32, (128, NS), 0) < HD
        dcb = jnp.zeros((CH, CH), F32)
        dc_acc = jnp.zeros((CH, NS), F32)
        db_acc = jnp.zeros((CH, NS), F32)
        dacs = jnp.zeros((CH, 128), F32)
        ddtx = jnp.zeros((CH, 128), F32)
        csum = jnp.zeros((8, CH), F32)
        sub8 = lax.broadcasted_iota(jnp.int32, (8, CH), 0)
        for q in range(4):
            ha, hb = 2 * q, 2 * q + 1
            sl = slice(128 * q, 128 * (q + 1))
            t = _pair_terms(q, dt, acs_col, acs_row, tril, lo)
            x, dy = xsv[:, sl], dy_all[:, sl]
            xd = x * t["dtsel"]
            xd_m = xd.astype(MXU)
            dy_lo, dy_hi = jnp.where(lo, dy, 0.0).astype(MXU), jnp.where(lo, 0.0, dy).astype(MXU)
            m_a, m_b = cb * t["d_a"], cb * t["d_b"]
            prev_m = prev_ref[q].astype(MXU)
            dnext = dst_ref[q]
            dnext_m = dnext.astype(MXU)
            bds = _nt(bm, dnext_m)
            dxd = _tn(m_a.astype(MXU), dy_lo) + _tn(m_b.astype(MXU), dy_hi) + t["fsel"] * bds
            dye_m = (dy * t["esel"]).astype(MXU)
            dst_ref[q] = dnext * jnp.where(sub_lo, t["g_a"], t["g_b"]) + _tn(dye_m, cm)
            dm_a, dm_b = _nt(dy_lo, xd_m), _nt(dy_hi, xd_m)
            dcb = dcb + dm_a * t["d_a"] + dm_b * t["d_b"]
            g_a, g_b = dm_a * m_a, dm_b * m_b
            csum = (csum + jnp.where(sub8 == ha, jnp.sum(g_a, axis=0, keepdims=True), 0.0)
                    + jnp.where(sub8 == hb, jnp.sum(g_b, axis=0, keepdims=True), 0.0))
            tf = t["fsel"] * xd * bds
            tyf = dy * (t["esel"] * _nt(cm, prev_m)) - tf
            dpp = dnext * prev_ref[q]
            ea = asum(jnp.where(lo, tf, 0.0)) + t["g_a"] * asum(jnp.where(sub_lo, dpp, 0.0))
            eb = asum(jnp.where(lo, 0.0, tf)) + t["g_b"] * asum(jnp.where(sub_lo, 0.0, dpp))
            ra = rsum(g_a + jnp.where(lo, tyf, 0.0)) + jnp.where(is_last, ea, 0.0)
            rb = rsum(g_b + jnp.where(lo, 0.0, tyf)) + jnp.where(is_last, eb, 0.0)
            dacs = dacs + jnp.where(lane == ha, ra, 0.0) + jnp.where(lane == hb, rb, 0.0)
            tx = dxd * x
            ddtx = (ddtx + jnp.where(lane == ha, rsum(jnp.where(lo, tx, 0.0)), 0.0)
                    + jnp.where(lane == hb, rsum(jnp.where(lo, 0.0, tx)), 0.0))
            dxs_ref[:, sl] = dxd * t["dtsel"] + dexp_v[:, sl] * dy
            dc_acc = dc_acc + _nn(dye_m, prev_m)
            db_acc = db_acc + _nn((xd * t["fsel"]).astype(MXU), dnext_m)
        dcb_m = dcb.astype(MXU)
        dc_ref[...] = dc_acc + _nn(dcb_m, bm)
        db_ref[...] = db_acc + _tn(dcb_m, cm)
        dacs = dacs - jnp.concatenate([csum, jnp.zeros((CH - 8, CH), F32)], axis=0).T
        dla = _nn(triu.astype(F32), dacs, HI)
        ddt_ref[...] = dla * arow_v + ddtx
        dal_ref[...] += jnp.sum(dla * dt, axis=0, keepdims=True) * arow_v

    def row(e, width, off_blocks):
        return pl.BlockSpec((CH, width), lambda g, b, c: ((b * ex + e) * nc + nc - 1 - c, off_blocks + g))

    per_ex_in = [[row(e, GW, 0), row(e, GW, 0), row(e, GW, 0), row(e, NS, DI // NS), row(e, NS, DI // NS + NG),
                  row(e, GW, O_Z // GW), row(e, 128, 0),
                  pl.BlockSpec((None, 8, CH), lambda g, b, c, e=e: (g, 0, (b * ex + e) * nc + nc - 1 - c)),
                  pl.BlockSpec((None, 4, 128, NS), lambda g, b, c, e=e: ((b * ex + e) * nc + nc - 1 - c, g, 0, 0))]
                 for e in range(ex)]

    def by_example(width, off_blocks):
        return pl.BlockSpec((ex, CH, width), lambda g, b, c: (b, nc - 1 - c, off_blocks + g))

    gvec = pl.BlockSpec((1, GW), lambda g, b, c: (0, g))
    hvec = pl.BlockSpec((1, 128), lambda g, b, c: (0, g))
    out = pl.pallas_call(
        body, grid=(NG, bl // ex, nc),
        in_specs=sum(per_ex_in, []) + [hvec, pl.BlockSpec((None, 8, 1), lambda g, b, c: (g, 0, 0)), gvec, gvec,
                                       pl.BlockSpec(memory_space=pl.ANY)],
        out_specs=[by_example(GW, O_Z // GW), by_example(GW, 0), by_example(NS, 0), by_example(NS, 0),
                   by_example(128, 0), gvec, gvec, hvec],
        input_output_aliases={9 * ex + 4: 0},
        out_shape=[jax.ShapeDtypeStruct((bl, s, dpp.shape[1]), dpp.dtype), jax.ShapeDtypeStruct((bl, s, DI), F32),
                   jax.ShapeDtypeStruct((bl, s, NG * NS), F32), jax.ShapeDtypeStruct((bl, s, NG * NS), F32),
                   jax.ShapeDtypeStruct((bl, s, NG * 128), F32), jax.ShapeDtypeStruct((1, DI), F32),
                   jax.ShapeDtypeStruct((1, DI), F32), jax.ShapeDtypeStruct((1, NG * 128), F32)],
        scratch_shapes=[pltpu.VMEM((ex, 4, 128, NS), F32)],
        compiler_params=_cp("arbitrary", "arbitrary", "arbitrary"), name=name,
    )(*([dyn, y, xc, xc, xc, pp, dtg, dtt, prev] * ex), arow, acol, dexp, ng, dpp.reshape(bl, s, dpp.shape[1]))
    return (out[0].reshape(T, -1), out[1].reshape(T, DI), out[2].reshape(T, NG * NS), out[3].reshape(T, NG * NS),
            out[4].reshape(T, NG * 128), out[5], out[6], out[7])


def _merge_fwd(pp, ya, yb, *, name):
    T = ya.shape[0]
    tr = _tile(T, 512, 8)

    def body(ga_ref, gb_ref, ya_ref, yb_ref, o_ref):
        o_ref[...] = (_sig(ga_ref[...]) * ya_ref[...].astype(F32)
                      + _sig(gb_ref[...]) * yb_ref[...].astype(F32)).astype(BF16)

    row = pl.BlockSpec((tr, D), lambda i: (i, 0))
    return pl.pallas_call(
        body, grid=(T // tr,),
        in_specs=[pl.BlockSpec((tr, D), lambda i: (i, O_GA // D)), pl.BlockSpec((tr, D), lambda i: (i, O_GB // D)),
                  row, row],
        out_specs=row, out_shape=jax.ShapeDtypeStruct((T, D), BF16), compiler_params=_cp("parallel"),
        name=name)(pp, pp, ya, yb)


def _merge_bwd(pp, ya, yb, dm, *, name):
    T = ya.shape[0]
    tr = _tile(T, 512, 8)
    assert O_GB == O_GA + D and O_GA % (2 * D) == 0

    def body(g_ref, ya_ref, yb_ref, dm_ref, dya_ref, dyb_ref, dg_ref):
        sa, sb, dmv = _sig(g_ref[:, :D]), _sig(g_ref[:, D:]), dm_ref[...]
        dya_ref[...] = (dmv * sa).astype(BF16)
        dyb_ref[...] = (dmv * sb).astype(BF16)
        dg_ref[:, :D] = (dmv * ya_ref[...].astype(F32) * (sa * (1.0 - sa))).astype(BF16)
        dg_ref[:, D:] = (dmv * yb_ref[...].astype(F32) * (sb * (1.0 - sb))).astype(BF16)

    row = pl.BlockSpec((tr, D), lambda i: (i, 0))
    gates = pl.BlockSpec((tr, 2 * D), lambda i: (i, O_GA // (2 * D)))
    act = jax.ShapeDtypeStruct((T, D), BF16)
    return pl.pallas_call(
        body, grid=(T // tr,), in_specs=[gates, row, row, row], out_specs=[row, row, gates],
        out_shape=[act, act, jax.ShapeDtypeStruct((T, NPP), BF16)], compiler_params=_cp("parallel"),
        name=name)(pp, ya, yb, dm)


def _softmax_rows(sc):
    e = jnp.exp(sc - jnp.max(sc, axis=-1, keepdims=True))
    return e / jnp.sum(e, axis=-1, keepdims=True)


def _attn_fwd(q, kv, bl, s, *, name):
    m = kv.shape[0] // bl
    tq = _tile(s, 1024)
    nq = s // tq
    scale = 1.0 / math.sqrt(XD)

    def body(q_ref, k_ref, v_ref, o_ref):
        p = _softmax_rows(_nt(q_ref[...], k_ref[...]) * scale)
        o_ref[...] = _nn(p.astype(MXU), v_ref[...]).astype(BF16)

    qspec = pl.BlockSpec((tq, XD), lambda b, h, i: (b * nq + i, h))
    return pl.pallas_call(
        body, grid=(bl, XH, nq),
        in_specs=[qspec, pl.BlockSpec((m, XD), lambda b, h, i: (b, h)),
                  pl.BlockSpec((m, XD), lambda b, h, i: (b, XH + h))],
        out_specs=qspec, out_shape=jax.ShapeDtypeStruct((bl * s, D), BF16),
        compiler_params=_cp("parallel", "parallel", "parallel"), name=name)(q, kv, kv)


def _attn_bwd(q, kv, do, bl, s, *, name):
    m = kv.shape[0] // bl
    tq = _tile(s, 1024)
    nq = s // tq
    scale = 1.0 / math.sqrt(XD)

    def body(q_ref, k_ref, v_ref, do_ref, dq_ref, dk_ref, dv_ref):
        @pl.when(pl.program_id(2) == 0)
        def _():
            dk_ref[...] = jnp.zeros_like(dk_ref)
            dv_ref[...] = jnp.zeros_like(dv_ref)

        qv, kvv, vv, dov = q_ref[...], k_ref[...], v_ref[...], do_ref[...]
        p = _softmax_rows(_nt(qv, kvv) * scale)
        dp = _nt(dov, vv)
        ds = (p * (dp - jnp.sum(dp * p, axis=-1, keepdims=True)) * scale).astype(MXU)
        dq_ref[...] = _nn(ds, kvv).astype(BF16)
        dk_ref[...] += _tn(ds, qv)
        dv_ref[...] += _tn(p.astype(MXU), dov)

    qspec = pl.BlockSpec((tq, XD), lambda b, h, i: (b * nq + i, h))
    kspec = pl.BlockSpec((m, XD), lambda b, h, i: (b, h))
    return pl.pallas_call(
        body, grid=(bl, XH, nq),
        in_specs=[qspec, kspec, pl.BlockSpec((m, XD), lambda b, h, i: (b, XH + h)), qspec],
        out_specs=[qspec, kspec, kspec],
        out_shape=[jax.ShapeDtypeStruct((bl * s, D), BF16), jax.ShapeDtypeStruct((bl * m, D), F32),
                   jax.ShapeDtypeStruct((bl * m, D), F32)],
        compiler_params=_cp("parallel", "parallel", "arbitrary"), name=name)(q, kv, kv, do)


def _row_tile(r, c, max_elems=512 * 1024, align=16):
    best = None
    for t in range(align, r + 1, align):
        if r % t == 0 and t * c <= max_elems:
            best = t
    return best if best is not None else r


def _addn(a, others, *, name, also_bf16=False):
    r, c = a.shape
    tr = _row_tile(r, c)
    n = len(others)

    def body(*refs):
        acc = refs[0][...].astype(F32)
        for o_ref in refs[1:1 + n]:
            acc = acc + o_ref[...].astype(F32)
        refs[1 + n][...] = acc
        if also_bf16:
            refs[2 + n][...] = acc.astype(BF16)

    spec = pl.BlockSpec((tr, c), lambda i: (i, 0))
    shapes = [jax.ShapeDtypeStruct((r, c), F32)] + ([jax.ShapeDtypeStruct((r, c), BF16)] if also_bf16 else [])
    out = pl.pallas_call(
        body, grid=(r // tr,), in_specs=[spec] * (1 + n), out_specs=[spec] * len(shapes), out_shape=shapes,
        compiler_params=_cp("parallel"), name=name)(a, *others)
    return out if also_bf16 else out[0]


def _sum_leading(a, *, name):
    n, r, c = a.shape

    def body(a_ref, o_ref):
        acc = a_ref[0]
        for i in range(1, n):
            acc = acc + a_ref[i]
        o_ref[...] = acc

    return pl.pallas_call(body, out_shape=jax.ShapeDtypeStruct((r, c), F32), name=name)(a)


def _adamw_math(wv, gv, mv, vv):
    m2 = ADAM_B1 * mv + (1.0 - ADAM_B1) * gv
    v2 = ADAM_B2 * vv + (1.0 - ADAM_B2) * (gv * gv)
    m_hat = m2 / (1.0 - ADAM_B1 ** ADAM_STEP)
    v_hat = v2 / (1.0 - ADAM_B2 ** ADAM_STEP)
    return -ADAM_LR * (m_hat / (jnp.sqrt(v_hat) + ADAM_EPS) + ADAM_WD * wv), m2, v2


def _adamw(w, g, m, v, *, name):
    r, c = w.shape
    tr = _row_tile(r, c, align=8)

    def body(w_ref, g_ref, m_ref, v_ref, d_ref, mo_ref, vo_ref):
        d_ref[...], mo_ref[...], vo_ref[...] = _adamw_math(w_ref[...], g_ref[...], m_ref[...], v_ref[...])

    spec = pl.BlockSpec((tr, c), lambda i: (i, 0))
    shp = jax.ShapeDtypeStruct((r, c), F32)
    return pl.pallas_call(
        body, grid=(r // tr,), in_specs=[spec] * 4, out_specs=[spec] * 3, out_shape=[shp] * 3,
        compiler_params=_cp("parallel"), name=name)(w, g, m, v)


def _adamw_halves(w, g_mine, g_other, m, v, c, *, name):
    _, r, cols = w.shape
    h = r // 2
    tr = _row_tile(h, cols, align=8)
    nh = h // tr

    def body(c_ref, w_ref, gm_ref, go_ref, m_ref, v_ref, g_ref, d_ref, mo_ref, vo_ref):
        gv = jnp.where(pl.program_id(0) // nh == c_ref[0], gm_ref[...], go_ref[...])
        g_ref[...] = gv
        d_ref[...], mo_ref[...], vo_ref[...] = _adamw_math(w_ref[...], gv, m_ref[...], v_ref[...])

    full = pl.BlockSpec((None, tr, cols), lambda i, c_: (0, i, 0))
    half = pl.BlockSpec((tr, cols), lambda i, c_: (i % nh, 0))
    shp = jax.ShapeDtypeStruct((1, r, cols), F32)
    return pl.pallas_call(
        body,
        grid_spec=pltpu.PrefetchScalarGridSpec(num_scalar_prefetch=1, grid=(2 * nh,),
                                               in_specs=[full, half, half, full, full], out_specs=[full] * 4),
        out_shape=[shp] * 4, compiler_params=_cp("parallel"), name=name,
    )(jnp.reshape(c, (1,)).astype(jnp.int32), w, g_mine, g_other, m, v)


def _flip(i, d):
    return 1 - i if d else i


def _comm(name, ins, out_shapes, n_remote, n_local, plan, aliases=None):
    n_in, n_out = len(ins), len(out_shapes)

    def body(*refs):
        in_refs, out_refs = refs[:n_in], refs[n_in:n_in + n_out]
        send_sems, recv_sems = refs[n_in + n_out], refs[n_in + n_out + 1]
        x, y, c = lax.axis_index("x"), lax.axis_index("y"), lax.axis_index("c")
        remote, local = plan(in_refs, out_refs, x, y, c)
        assert len(remote) == n_remote and len(local) == n_local
        copies = []
        if n_local:
            loc_sems = refs[n_in + n_out + 2]
            copies += [pltpu.make_async_copy(s_, d_, loc_sems.at[i]) for i, (s_, d_) in enumerate(local)]
        copies += [pltpu.make_async_remote_copy(src_ref=s_, dst_ref=d_, send_sem=send_sems.at[i],
                                                recv_sem=recv_sems.at[i], device_id=dev, device_id_type=MESH)
                   for i, (s_, d_, dev) in enumerate(remote)]
        for cp in copies:
            cp.start()
        for cp in copies:
            cp.wait()

    hbm = pl.BlockSpec(memory_space=pl.ANY)
    scratch = [pltpu.SemaphoreType.DMA((n_remote,)), pltpu.SemaphoreType.DMA((n_remote,))]
    if n_local:
        scratch.append(pltpu.SemaphoreType.DMA((n_local,)))
    return pl.pallas_call(
        body, in_specs=[hbm] * n_in, out_specs=[hbm] * n_out, out_shape=out_shapes, scratch_shapes=scratch,
        input_output_aliases=aliases or {}, compiler_params=pltpu.CompilerParams(has_side_effects=True),
        name=name)(*ins)


HBM_SPEC = pl.BlockSpec(memory_space=pltpu.HBM)
SEM_SPEC = pl.BlockSpec(memory_space=pltpu.SEMAPHORE)
DATAFLOW = pltpu.SideEffectType.DATAFLOW_SIDE_EFFECTING


def _remote_copies(plan, srcs, lands, send_sems, recv_sems, n_copies):
    x, y, c = lax.axis_index("x"), lax.axis_index("y"), lax.axis_index("c")
    copies = plan(srcs, lands, x, y, c)
    assert len(copies) == n_copies
    return [pltpu.make_async_remote_copy(src_ref=s_, dst_ref=d_, send_sem=send_sems.at[i], recv_sem=recv_sems.at[i],
                                         device_id=dev, device_id_type=MESH) for i, (s_, d_, dev) in enumerate(copies)]


def _split_start(name, srcs, lands, n_copies, plan, after=None):
    ns, nb = len(srcs), len(srcs) + len(lands)
    n_after = 0 if after is None else 1
    n_in = nb + n_after

    def body(*refs):
        for cp in _remote_copies(plan, refs[:ns], refs[ns:nb], refs[n_in], refs[n_in + 1], n_copies):
            cp.start()
        refs[-1][...] = jnp.zeros_like(refs[-1])

    arrays = [pltpu.with_memory_space_constraint(a_, pltpu.HBM) for a_ in list(srcs) + list(lands)]
    out = pl.pallas_call(
        body, name=name,
        out_shape=(pltpu.SemaphoreType.DMA((n_copies,)), pltpu.SemaphoreType.DMA((n_copies,)),
                   *[pltpu.HBM(a_.shape, a_.dtype) for a_ in arrays], jax.ShapeDtypeStruct((8, 128), F32)),
        in_specs=[HBM_SPEC] * nb + [pl.BlockSpec(memory_space=pl.ANY)] * n_after,
        out_specs=(SEM_SPEC, SEM_SPEC, *[HBM_SPEC] * nb, pl.BlockSpec(memory_space=pltpu.VMEM)),
        input_output_aliases={i: 2 + i for i in range(nb)},
        compiler_params=pltpu.CompilerParams(has_side_effects=DATAFLOW))(*arrays, *([after] * n_after))
    return (out[0], out[1], list(out[2:2 + nb])), out[-1]


def _split_wait(name, handle, ns, n_copies, plan, after):
    send_sems, recv_sems, bufs = handle
    nb = len(bufs)

    def body(*refs):
        for cp in _remote_copies(plan, refs[:ns], refs[ns:nb], refs[nb], refs[nb + 1], n_copies):
            cp.wait_send()
            cp.wait_recv()

    out = pl.pallas_call(
        body, name=name, out_shape=[pltpu.HBM(b_.shape, b_.dtype) for b_ in bufs],
        in_specs=[HBM_SPEC] * nb + [SEM_SPEC, SEM_SPEC, pl.BlockSpec(memory_space=pl.ANY)],
        out_specs=[HBM_SPEC] * nb, input_output_aliases={i: i for i in range(nb)},
        compiler_params=pltpu.CompilerParams(has_side_effects=DATAFLOW))(*bufs, send_sems, recv_sems, after)
    return list(out[ns:])


def _gather_start(shards, tag, after=None):
    n = len(shards)
    lands = [lax.empty((4,) + s.shape, s.dtype) for s in shards]

    def plan(srcs, dsts, x, y, c):
        k = 2 * x + y
        copies = []
        for w_ref, o_ref in zip(srcs, dsts):
            h = w_ref.shape[0] // 2
            rows = pl.ds(c * h, h)
            copies += [(w_ref.at[rows], o_ref.at[k, rows], (_flip(x, dx), _flip(y, dy), c)) for dx, dy in CHIP_FLIPS]
        return copies

    handle, token = _split_start(f"gather_{tag}_start", shards, lands, 3 * n, plan, after)
    return (handle, plan, n), token


def _gather_wait(started, after, tag):
    handle, plan, n = started
    return _split_wait(f"gather_{tag}_wait", handle, n, 3 * n, plan, after)


def _gather_d2d(lands, before, tag):
    n = len(lands)

    def plan_d2d(in_refs, out_refs, x, y, c):
        remote = []
        for o_ref in out_refs:
            h = o_ref.shape[1] // 2
            for dx, dy in CHIP_FLIPS:
                half = o_ref.at[2 * _flip(x, dx) + _flip(y, dy), pl.ds(c * h, h)]
                remote.append((half, half, (x, y, 1 - c)))
        return remote, []

    return _comm(f"gather_{tag}_d2d", list(lands) + list(before),
                 [jax.ShapeDtypeStruct(l_.shape, l_.dtype) for l_ in lands], 3 * n, 0, plan_d2d,
                 aliases={i: i for i in range(n)})


def _pair_plan(in_refs, out_refs, x, y, c):
    return [(i_, o_, (x, y, 1 - c)) for i_, o_ in zip(in_refs, out_refs)], []


def _rs_start(grads, tag):
    n = len(grads)
    c = lax.axis_index("c")
    def rows(g, start, h):
        return g.rows(start, h) if isinstance(g, _WInGrad) else lax.dynamic_slice_in_dim(g, start, h, axis=1)

    halves = [g.shape[1] // 2 for g in grads]
    mine = [rows(g, c * h, h) for g, h in zip(grads, halves)]
    send_a = [rows(g, (1 - c) * h, h).astype(BF16) for g, h in zip(grads, halves)]
    recv_a = _comm(f"rs_pair_{tag}", send_a, [jax.ShapeDtypeStruct(s.shape, BF16) for s in send_a], n, 0, _pair_plan)
    pair, pair_b = [], []
    for i, (mi, ra) in enumerate(zip(mine, recv_a)):
        four, h, cols = mi.shape
        p32, p16 = _addn(mi.reshape(four * h, cols), [ra.reshape(four * h, cols)], name=f"rs_pair_sum_{tag}_{i}",
                         also_bf16=True)
        pair.append(p32.reshape(four, h, cols))
        pair_b.append(p16.reshape(four, h, cols))

    def plan(srcs, dsts, x, y, c_):
        copies = []
        for i_, o_ in zip(srcs, dsts):
            for j, (dx, dy) in enumerate(CHIP_FLIPS):
                fx, fy = _flip(x, dx), _flip(y, dy)
                copies.append((i_.at[2 * fx + fy], o_.at[j], (fx, fy, c_)))
        return copies

    lands = [lax.empty((3,) + p.shape[1:], BF16) for p in pair_b]
    handle, token = _split_start(f"rs_chips_{tag}_start", pair_b, lands, 3 * n, plan)
    return (handle, plan, n, pair), token


def _rs_finish(started, after, tag):
    handle, plan, n, pair = started
    recv_b = _split_wait(f"rs_chips_{tag}_wait", handle, n, 3 * n, plan, after)
    k = 2 * lax.axis_index("x") + lax.axis_index("y")
    tot = [_addn(lax.dynamic_index_in_dim(p, k, 0, keepdims=False), [rb[0], rb[1], rb[2]],
                 name=f"rs_chip_sum_{tag}_{i}") for i, (p, rb) in enumerate(zip(pair, recv_b))]
    other = _comm(f"rs_halves_{tag}", tot, [jax.ShapeDtypeStruct(t.shape, F32) for t in tot], n, 0, _pair_plan)
    return tot, other


def _gather_all(vec, *, name, before=()):
    out = jax.ShapeDtypeStruct((8,) + vec.shape, vec.dtype)

    def plan(in_refs, out_refs, x, y, c):
        me = 4 * x + 2 * y + c
        remote = [(in_refs[0], out_refs[0].at[me], (_flip(x, dx), _flip(y, dy), _flip(c, dc)))
                  for dx in (0, 1) for dy in (0, 1) for dc in (0, 1) if (dx, dy, dc) != (0, 0, 0)]
        return remote, [(in_refs[0], out_refs[0].at[me])]

    return _comm(name, [vec] + list(before), [out], 7, 1, plan)[0]


def _pack(parts):
    flat = [p.reshape(-1).astype(F32) for p in parts]
    total = sum(f.shape[0] for f in flat)
    n = -(-total // 1024) * 128
    vec = jnp.concatenate(flat + [jnp.zeros((8 * n - total,), F32)]).reshape(8, n)
    offs, o = [], 0
    for f in flat:
        offs.append((o, f.shape[0]))
        o += f.shape[0]
    return vec, offs


def _unpack(vec, offs, shapes):
    flat = vec.reshape(-1)
    return [flat[o:o + n].reshape(s) for (o, n), s in zip(offs, shapes)]


BIG = (("ffn1_w_gate_up", "col"), ("ffn1_w_down", "row"), ("w_in", "col"), ("w_out_a", "row"), ("w_out_ssm", "row"),
       ("w_mix_out", "row"), ("w_q", "row"), ("w_kv", "col"), ("w_o_x", "row"), ("ffn2_w_gate_up", "col"),
       ("ffn2_w_down", "row"))
SMALL = ("ffn1_norm", "mix_norm", "conv_a_w", "ssm_conv_w", "ssm_conv_b", "ssm_dt_bias", "ssm_a_log", "ssm_d",
         "ssm_norm", "xattn_norm", "mem_norm", "ffn2_norm", "final_norm")
WEIGHTS = ("ffn1_norm", "ffn1_w_gate_up", "ffn1_w_down", "mix_norm", "w_in", "conv_a_w", "w_out_a", "ssm_conv_w",
           "ssm_conv_b", "ssm_dt_bias", "ssm_a_log", "ssm_d", "ssm_norm", "w_out_ssm", "w_mix_out", "xattn_norm",
           "mem_norm", "w_q", "w_kv", "w_o_x", "ffn2_norm", "ffn2_w_gate_up", "ffn2_w_down", "final_norm")


GATHER_GROUPS = (("a", ("ffn1_w_gate_up",)), ("b", ("ffn1_w_down", "w_in")),
                 ("c", ("w_out_a", "w_out_ssm", "w_mix_out", "w_q", "w_kv", "w_o_x", "ffn2_w_gate_up", "ffn2_w_down")))


def _place_own(land, own, k, *, name):
    four, r, cols = land.shape
    tr = _row_tile(r, cols)

    def body(k_ref, own_ref, land_in, o_ref):
        del k_ref, land_in
        o_ref[...] = own_ref[...]

    return pl.pallas_call(
        body,
        grid_spec=pltpu.PrefetchScalarGridSpec(
            num_scalar_prefetch=1, grid=(r // tr,),
            in_specs=[pl.BlockSpec((tr, cols), lambda i, k_: (i, 0)), pl.BlockSpec(memory_space=pl.ANY)],
            out_specs=pl.BlockSpec((None, tr, cols), lambda i, k_: (k_[0], i, 0))),
        out_shape=jax.ShapeDtypeStruct(land.shape, land.dtype), input_output_aliases={2: 0},
        compiler_params=_cp("parallel"), name=name)(jnp.reshape(k, (1,)).astype(jnp.int32), own, land)


def _full_weight(land, own, kind, k, *, name):
    land = _place_own(land, own, k, name=name)
    four, r, cols = land.shape
    if kind == "row":
        return land.reshape(four * r, cols)
    return jnp.transpose(land, (1, 0, 2)).reshape(r, four * cols)


class _GatheredWeights:
    def __init__(self, shards32, k, after):
        first = GATHER_GROUPS[0][1]
        self.shards, self.k = {n: shards32[n].astype(BF16)[0] for n in first}, k
        self.full = {}
        self.n_done = 0
        self.started, token = self._start(0, after)
        self.token = token[0, 0]
        self.shards.update({n: (w + token[0, 0]).astype(BF16)[0] for n, w in shards32.items() if n not in first})
        self.after = jnp.stack([self.shards[n][0, 0] for n in shards32 if n not in first]).astype(F32).reshape(1, -1)

    def _start(self, gi, after):
        tag, names = GATHER_GROUPS[gi]
        return _gather_start([self.shards[n] for n in names], tag, after)

    def mark(self, value):
        self.after = value

    def __getitem__(self, name):
        if name not in self.full:
            tag, names = GATHER_GROUPS[self.n_done]
            assert name in names, (name, tag)
            lands = _gather_wait(self.started, self.after, tag)
            before = []
            if self.n_done + 1 < len(GATHER_GROUPS):
                self.started, token = self._start(self.n_done + 1, lands[0])
                before = [token]
            lands = _gather_d2d(lands, before, tag)
            for n, land in zip(names, lands):
                if n == "w_in":
                    self.full[n] = _pad_w_in_shards(_place_own(land, self.shards[n], self.k, name=f"own_{n}"))
                elif n.endswith("w_gate_up"):
                    self.full[n] = _place_own(land, self.shards[n], self.k, name=f"own_{n}")
                else:
                    self.full[n] = _full_weight(land, self.shards[n], dict(BIG)[n], self.k, name=f"own_{n}")
            self.n_done += 1
        return self.full[name]


def _shard_major(dw, kind):
    if isinstance(dw, tuple):
        return jnp.concatenate(dw, axis=0)
    if dw.ndim == 3:
        return dw
    if kind == "row":
        return dw.reshape(4, dw.shape[0] // 4, dw.shape[1])
    return jnp.transpose(dw.reshape(dw.shape[0], 4, dw.shape[1] // 4), (1, 0, 2))


def _pad_rows8(w):
    return jnp.concatenate([w, jnp.zeros((8 - w.shape[0], w.shape[1]), w.dtype)], axis=0)


def _group_lanes(v):
    r = v.shape[0]
    return jnp.pad(v.reshape(r, NG, NH // NG), ((0, 0), (0, 0), (0, 128 - NH // NG))).reshape(r, NG * 128)


def _ungroup_lanes(v):
    r = v.shape[0]
    return v.reshape(r, NG, 128)[:, :, :NH // NG].reshape(r, NH)


def _local_step(wfull, small, x, mem, target, token=0.0, on_grads=None):
    bl, s, _ = x.shape
    T = bl * s
    x2, t2 = x.reshape(T, D), target.reshape(T, D)
    mem2 = mem.reshape(-1, D)
    g = {}
    tok = [token]
    mark = getattr(wfull, "mark", lambda value: None)

    def gain(name):
        return small[name].reshape(1, -1) + tok[0]

    def emit(tag, names):
        if on_grads is not None:
            tok[0] = tok[0] + on_grads(tag, {n: g[n] for n in names})

    def ffn_fwd(h, n, wgu, wd, tag, next_gain=None):
        gate, up, a = _gate_up_fwd(n, wfull[wgu], name=f"{tag}_gate_up")
        mark(a)
        out = _mm(a, wfull[wd], "nn", tk=DFF, scale=FFN_RES, residual=h, norm_gain=next_gain, name=f"{tag}_down")
        return out, (n, gate, up, a)

    def ffn_bwd(dh, h, norm, wgu, wd, saved, tag):
        n, gate, up, a = saved
        dgate, dup = _act_bwd(dh, wfull[wd], gate, up, FFN_RES, name=f"{tag}_d_act")
        g[wd] = _mm(a, dh, "tn", tm=1408, scale=FFN_RES, name=f"{tag}_d_w_down")
        g[wgu] = (_mm(n, dgate, "tn", tn=1408, col_shards=2, name=f"{tag}_d_w_gate"),
                  _mm(n, dup, "tn", tn=1408, col_shards=2, name=f"{tag}_d_w_up"))
        emit(tag, (wgu, wd))
        dn = _gate_up_bwd_input(dgate, dup, wfull[wgu], name=f"{tag}_d_norm_out")
        dh_in, g[norm] = _norm_bwd(h, gain(norm), dn, dh, name=f"{tag}_d_norm")
        return dh_in

    n1 = _norm_fwd(x2, gain("ffn1_norm"), name="ffn1_norm")
    (h1, u), ffn1_saved = ffn_fwd(x2, n1, "ffn1_w_gate_up", "ffn1_w_down", "ffn1", gain("mix_norm"))
    mark(h1)
    pp = _mm(u, wfull["w_in"], "nn", tm=2048, tn=1152, name="in_proj")
    wa8 = _pad_rows8(small["conv_a_w"])
    ws8 = _pad_rows8(small["ssm_conv_w"])
    conv_b = gain("ssm_conv_b")
    bias128 = jnp.pad(gain("ssm_dt_bias"), ((0, 0), (0, 128 - NH)))
    ya_pre = _conv_a_fwd(pp, wa8, bl, s, name="conv_a")
    xc = _conv_ssm_fwd(pp, ws8, conv_b, bl, s, name="conv_ssm")
    mark(xc)
    alog = gain("ssm_a_log")
    dtg, dtt = _dt_fwd(pp, bias128, name="dt")
    arow, acol = _group_lanes(alog), alog.reshape(NG, NH // NG, 1)
    dexp = jnp.repeat(gain("ssm_d"), HD, axis=1)
    ng = gain("ssm_norm")
    y, yn, prev = _ssd_fwd(xc, pp, dtg, dtt, arow, acol, dexp, ng, bl, s, name="ssd")
    ya = _mm(ya_pre, wfull["w_out_a"], "nn", tn=1024, out_dtype=BF16, name="out_a")
    yb = _mm(yn, wfull["w_out_ssm"], "nn", tn=1024, tk=DI, out_dtype=BF16, name="out_ssm")
    merged = _merge_fwd(pp, ya, yb, name="merge")
    h2, un = _mm(merged, wfull["w_mix_out"], "nn", residual=h1, norm_gain=gain("xattn_norm"), name="mix_out")
    q = _mm(un, wfull["w_q"], "nn", tn=1024, out_dtype=BF16, name="q_proj")
    mn = _norm_fwd(mem2, gain("mem_norm"), name="mem_norm")
    kv = _mm(mn, wfull["w_kv"], "nn", tn=1024, out_dtype=BF16, name="kv_proj")
    o = _attn_fwd(q, kv, bl, s, name="attn")
    h3, n2 = _mm(o, wfull["w_o_x"], "nn", residual=h2, norm_gain=gain("ffn2_norm"), name="attn_out")
    h4, ffn2_saved = ffn_fwd(h3, n2, "ffn2_w_gate_up", "ffn2_w_down", "ffn2")
    sq_err, dh4, dgf = _final_loss(h4, gain("final_norm"), t2, name="final_loss")
    g["final_norm"] = dgf

    dh3 = ffn_bwd(dh4, h3, "ffn2_norm", "ffn2_w_gate_up", "ffn2_w_down", ffn2_saved, "ffn2")
    do = _mm(dh3, wfull["w_o_x"], "nt", tn=1024, out_dtype=BF16, name="d_attn_o")
    g["w_o_x"] = _mm(o, dh3, "tn",name="d_w_o_x")
    dq, dk, dv = _attn_bwd(q, kv, do, bl, s, name="d_attn")
    dun = _mm(dq, wfull["w_q"], "nt", tn=1024, name="d_xattn_norm_out")
    g["w_q"] = _mm(un, dq, "tn",name="d_w_q")
    dkv = jnp.concatenate([dk, dv], axis=1)
    dmn = _mm(dkv, wfull["w_kv"], "nt", tn=1024, tk=2 * D, name="d_mem_norm_out")
    g["w_kv"] = _mm(mn, dkv, "tn", tn=512, col_shards=4, name="d_w_kv")
    emit("attn", ("w_q", "w_kv", "w_o_x"))
    _, g["mem_norm"] = _norm_bwd(mem2, gain("mem_norm"), dmn, None, name="d_mem_norm")
    dh2, g["xattn_norm"] = _norm_bwd(h2, gain("xattn_norm"), dun, dh3, name="d_xattn_norm")
    dmerged = _mm(dh2, wfull["w_mix_out"], "nt", tn=1024, name="d_merged")
    g["w_mix_out"] = _mm(merged, dh2, "tn",name="d_w_mix_out")
    dya, dyb, dpp = _merge_bwd(pp, ya, yb, dmerged, name="d_merge")
    dya_pre = _mm(dya, wfull["w_out_a"], "nt", tn=1024, name="d_conv_a_out")
    g["w_out_a"] = _mm(ya_pre, dya, "tn",name="d_w_out_a")
    dyn = _mm(dyb, wfull["w_out_ssm"], "nt", tn=DI, name="d_ssd_out")
    g["w_out_ssm"] = _mm(yn, dyb, "tn",name="d_w_out_ssm")
    dpp, dwa8 = _conv_a_bwd(pp, wa8, dya_pre, dpp, bl, s, name="d_conv_a")
    g["conv_a_w"] = dwa8[:3]
    dpp, dxs, dbm, dcm, ddtg, g["ssm_norm"], ddexp, dalg = _ssd_bwd(
        dyn, y, xc, pp, dtg, dtt, arow, acol, dexp, ng, prev, dpp, bl, s, name="d_ssd")
    g["ssm_d"] = ddexp.reshape(NH, HD).sum(axis=1).reshape(1, NH)
    g["ssm_a_log"] = _ungroup_lanes(dalg)
    conv_dw, conv_db = [], []
    for dpart, off, tag in ((dxs, 0, "x"), (dbm, DI, "b"), (dcm, DI + NG * NS, "c")):
        dpp, dw_, db_ = _conv_ssm_bwd(pp, ws8, conv_b, dpart, off, dpp, bl, s, name=f"d_conv_ssm_{tag}")
        conv_dw.append(dw_)
        conv_db.append(db_)
    g["ssm_conv_w"] = jnp.concatenate(conv_dw, axis=1)[:4]
    g["ssm_conv_b"] = jnp.concatenate(conv_db, axis=1)
    dpp, dbias = _dt_bwd(pp, bias128, ddtg, dpp, name="d_dt")
    g["ssm_dt_bias"] = dbias[:, :NH]
    g["w_in"] = _mm(u, dpp, "tn", tn=1152, name="d_w_in")
    emit("mix", ("w_in", "w_out_a", "w_out_ssm", "w_mix_out"))
    du = _mm(dpp, wfull["w_in"], "nt", tk=3456, name="d_mix_norm_out")
    dh1, g["mix_norm"] = _norm_bwd(h1, gain("mix_norm"), du, dh2, name="d_mix_norm")
    dx = ffn_bwd(dh1, x2, "ffn1_norm", "ffn1_w_gate_up", "ffn1_w_down", ffn1_saved, "ffn1")
    return sq_err, dx, g


W_IN_SHARD = NIN // 4


def _w_in_segments():
    segs, p = [], 0
    for t in range(D // CA_TILE):
        for which in range(3):
            segs.append((D * which + CA_TILE * t, p, CA_TILE))
            p += CA_TILE
    for s, n in ((3 * D, O_GA - 3 * D), (O_GA + NH, 2 * D), (O_GA, NH)):
        segs.append((s, p, n))
        p += n
    assert p == NIN and segs[-1][1] == O_DT and segs[-2][1] == O_GA
    return segs


def _pad_w_in_shards(land):
    pieces = []
    for s, _, n in _w_in_segments():
        while n > 0:
            kk, off = divmod(s, W_IN_SHARD)
            take = min(n, W_IN_SHARD - off)
            pieces.append(land[kk][:, off:off + take])
            s, n = s + take, n - take
    return jnp.concatenate(pieces + [jnp.zeros((land.shape[1], NPP - NIN), land.dtype)], axis=1)


class _WInGrad:
    def __init__(self, dwp):
        self.dwp = dwp
        self.shape = (4, dwp.shape[0], W_IN_SHARD)

    def rows(self, start, n):
        part = lax.dynamic_slice_in_dim(self.dwp, start, n, axis=0)
        shards = []
        for kk in range(4):
            n0, n1 = W_IN_SHARD * kk, W_IN_SHARD * (kk + 1)
            cuts = sorted((max(s, n0), p + max(s, n0) - s, min(s + m, n1) - max(s, n0))
                          for s, p, m in _w_in_segments() if min(s + m, n1) > max(s, n0))
            shards.append(jnp.concatenate([part[:, p:p + m] for _, p, m in cuts], axis=1))
        return jnp.stack(shards)


def _pad_w_in(w):
    return _pad_w_in_shards(jnp.stack(jnp.split(w, 4, axis=1)))


def _unpad_w_in(w):
    return jnp.concatenate(list(_WInGrad(w).rows(0, w.shape[0])), axis=1)


def kernel(x, mem, ffn1_norm, ffn1_w_gate_up, ffn1_w_down, mix_norm, w_in, conv_a_w, w_out_a, ssm_conv_w, ssm_conv_b, ssm_dt_bias, ssm_a_log, ssm_d, ssm_norm, w_out_ssm, w_mix_out, xattn_norm, mem_norm, w_q, w_kv, w_o_x, ffn2_norm, ffn2_w_gate_up, ffn2_w_down, final_norm, loss_target, m_ffn1_norm, m_ffn1_w_gate_up, m_ffn1_w_down, m_mix_norm, m_w_in, m_conv_a_w, m_w_out_a, m_ssm_conv_w, m_ssm_conv_b, m_ssm_dt_bias, m_ssm_a_log, m_ssm_d, m_ssm_norm, m_w_out_ssm, m_w_mix_out, m_xattn_norm, m_mem_norm, m_w_q, m_w_kv, m_w_o_x, m_ffn2_norm, m_ffn2_w_gate_up, m_ffn2_w_down, m_final_norm, v_ffn1_norm, v_ffn1_w_gate_up, v_ffn1_w_down, v_mix_norm, v_w_in, v_conv_a_w, v_w_out_a, v_ssm_conv_w, v_ssm_conv_b, v_ssm_dt_bias, v_ssm_a_log, v_ssm_d, v_ssm_norm, v_w_out_ssm, v_w_mix_out, v_xattn_norm, v_mem_norm, v_w_q, v_w_kv, v_w_o_x, v_ffn2_norm, v_ffn2_w_gate_up, v_ffn2_w_down, v_final_norm):
    a = dict(locals())
    xi, yi = lax.axis_index("x"), lax.axis_index("y")
    k = 2 * xi + yi

    conv_vec, conv_offs = _pack([a["conv_a_w"], a["ssm_conv_w"]])
    conv_all = _gather_all(conv_vec, name="gather_conv_w")
    wfull = _GatheredWeights({n: a[n] for n, _ in BIG}, k, conv_all)
    conv_sh = [_unpack(conv_all[2 * kk], conv_offs, [a["conv_a_w"].shape[1:], a["ssm_conv_w"].shape[1:]])
               for kk in range(4)]
    small = {n: a[n] for n in SMALL}
    small["conv_a_w"] = jnp.concatenate([cs[0] for cs in conv_sh], axis=1)
    small["ssm_conv_w"] = jnp.concatenate([cs[1] for cs in conv_sh], axis=1)

    rs_started = []

    def on_grads(tag, grads):
        names = [n for n, _ in BIG if n in grads]
        shard_major = [_WInGrad(grads[n]) if n == "w_in" else _shard_major(grads[n], dict(BIG)[n]) for n in names]
        st, tk = _rs_start(shard_major, tag)
        rs_started.append((tag, names, st))
        return tk[0, 0]

    sq_err, dx, g = _local_step(wfull, small, x, mem, loss_target, wfull.token, on_grads)
    loss = lax.psum(0.5 / D * jnp.sum(sq_err), ("x", "y", "c"))

    ci = lax.axis_index("c")
    out = {}

    def finish(tag, names, st, after):
        g_mine, g_other = _rs_finish(st, after, tag)
        for n, gm, go in zip(names, g_mine, g_other):
            res = _adamw_halves(a[n], gm, go, a["m_" + n], a["v_" + n], ci, name=f"adamw_{n}")
            out[n] = tuple(t.reshape(a[n].shape) for t in res)
        return res[1]

    done = dx
    for grp in rs_started[:-1]:
        done = finish(*grp, dx)

    full_shapes = [g[n].shape for n in SMALL]
    gvec, goffs = _pack([g[n] for n in SMALL])
    gsum = _sum_leading(_gather_all(gvec, name="gather_small_grads", before=[done]), name="sum_small_grads")
    finish(*rs_started[-1], gsum)
    gsmall = dict(zip(SMALL, _unpack(gsum, goffs, full_shapes)))
    for n in ("conv_a_w", "ssm_conv_w"):
        width = a[n].shape[2]
        gsmall[n] = lax.dynamic_slice_in_dim(gsmall[n], k * width, width, axis=1)
    local_shapes = [a[n].shape for n in SMALL]
    packs = [_pack([t[n] for n in SMALL]) for t in
             ({n: a[n] for n in SMALL}, gsmall, {n: a["m_" + n] for n in SMALL}, {n: a["v_" + n] for n in SMALL})]
    offs = packs[0][1]
    res = _adamw(*[p[0] for p in packs], name="adamw_small")
    unp = [_unpack(r, offs, local_shapes) for r in res]
    for i, n in enumerate(SMALL):
        out[n] = (gsmall[n].reshape(a[n].shape), unp[0][i], unp[1][i], unp[2][i])

    grad_x = dx.reshape(x.shape)
    return (loss, grad_x, *[out[n][0] for n in WEIGHTS], *[out[n][1] for n in WEIGHTS],
            *[out[n][2] for n in WEIGHTS], *[out[n][3] for n in WEIGHTS])
```

```python
import functools
import math

import jax
import jax.numpy as jnp
from jax import lax
from jax.experimental import pallas as pl
from jax.experimental.pallas import tpu as pltpu

F32 = jnp.float32
BF16 = jnp.bfloat16
MXU = jnp.bfloat16
HI = lax.Precision.HIGHEST

D = 1024
DFF = 2816
DI = 2048
NH, HD, NG, NS, CH = 32, 64, 4, 128, 128
GW = DI // NG
XH, XD = 4, 256
EPS = 1e-6
NEG = -1e30
CA_TILE = 256
O_CA, O_Z, O_XBC, O_GA, O_GB, O_DT, NPP = 0, 3072, 5120, 8192, 9216, 10240, 10368
NIN = 10272
FFN_RES = 0.5
ADAM_LR, ADAM_B1, ADAM_B2, ADAM_EPS, ADAM_WD, ADAM_STEP = 0.001, 0.9, 0.999, 1e-08, 0.01, 10
VMEM_LIMIT = 56 * 1024 * 1024
EPI_COLS = 256
SSD_EX = 4
MESH = pl.DeviceIdType.MESH
CHIP_FLIPS = ((1, 0), (0, 1), (1, 1))


def _cp(*sem):
    return pltpu.CompilerParams(dimension_semantics=sem, vmem_limit_bytes=VMEM_LIMIT)


def _tile(n, pref, align=128):
    if n <= pref:
        return n
    t = (pref // align) * align
    while t >= align:
        if n % t == 0:
            return t
        t -= align
    raise ValueError((n, pref))


def _dot(a, b, dims, prec=None):
    return lax.dot_general(a, b, (dims, ((), ())), preferred_element_type=F32, precision=prec)


def _nn(a, b, prec=None):
    return _dot(a, b, ((1,), (0,)), prec)


def _nt(a, b):
    return _dot(a, b, ((1,), (1,)))


def _tn(a, b):
    return _dot(a, b, ((0,), (0,)))


def _sig(x):
    return jax.nn.sigmoid(x)


def _mm(a, b, mode, *, name, tm=1024, tn=1024, tk=None, out_dtype=F32, scale=None, residual=None, col_shards=0,
        norm_gain=None):
    if tk is None:
        tk = 2048 if mode == "tn" else 1024
    if mode == "nn":
        (M, K), (K2, N) = a.shape, b.shape
    elif mode == "nt":
        (M, K), (N, K2) = a.shape, b.shape
    else:
        (K, M), (K2, N) = a.shape, b.shape
    assert K == K2, (name, a.shape, b.shape)
    tm, tn, tk = _tile(M, tm), _tile(N, tn), _tile(K, tk)
    nk = K // tk
    if mode == "nn":
        a_spec = pl.BlockSpec((tm, tk), lambda i, j, k: (i, k))
        b_spec = pl.BlockSpec((tk, tn), lambda i, j, k: (k, j))
        dims = ((1,), (0,))
    elif mode == "nt":
        a_spec = pl.BlockSpec((tm, tk), lambda i, j, k: (i, k))
        b_spec = pl.BlockSpec((tn, tk), lambda i, j, k: (j, k))
        dims = ((1,), (1,))
    else:
        a_spec = pl.BlockSpec((tk, tm), lambda i, j, k: (k, i))
        b_spec = pl.BlockSpec((tk, tn), lambda i, j, k: (k, j))
        dims = ((0,), (0,))
    o_spec = pl.BlockSpec((tm, tn), lambda i, j, k: (i, j))
    out_spec, out_shape = o_spec, jax.ShapeDtypeStruct((M, N), out_dtype)
    if col_shards:
        per = N // col_shards // tn
        assert per * tn * col_shards == N, (name, N, tn, col_shards)
        out_spec = pl.BlockSpec((None, tm, tn), lambda i, j, k: (j // per, i, j % per))
        out_shape = jax.ShapeDtypeStruct((col_shards, M, N // col_shards), out_dtype)
    has_res = residual is not None
    has_norm = norm_gain is not None
    assert not has_norm or (tn == N and not col_shards)
    n_in = 2 + has_res + has_norm

    def body(*refs):
        a_ref, b_ref = refs[0], refs[1]
        o_ref = refs[n_in]

        def finish(acc):
            if scale is not None:
                acc = acc * scale
            if has_res:
                acc = acc + refs[2][...]
            o_ref[...] = acc.astype(out_dtype)
            if has_norm:
                rs = lax.rsqrt(jnp.mean(acc * acc, axis=-1, keepdims=True) + EPS)
                refs[n_in + 1][...] = (acc * rs * refs[n_in - 1][...]).astype(BF16)

        part = _dot(a_ref[...].astype(MXU), b_ref[...].astype(MXU), dims)
        if nk == 1:
            finish(part)
            return
        acc_ref = refs[-1]
        k = pl.program_id(2)

        @pl.when(k == 0)
        def _():
            acc_ref[...] = part

        @pl.when(k > 0)
        def _():
            acc_ref[...] += part

        @pl.when(k == nk - 1)
        def _():
            finish(acc_ref[...])

    ins, in_specs = [a, b], [a_spec, b_spec]
    if has_res:
        ins.append(residual)
        in_specs.append(o_spec)
    if has_norm:
        ins.append(norm_gain)
        in_specs.append(pl.BlockSpec((1, tn), lambda i, j, k: (0, j)))
        out_spec, out_shape = [out_spec, o_spec], [out_shape, jax.ShapeDtypeStruct((M, N), BF16)]
    return pl.pallas_call(
        body, grid=(M // tm, N // tn, nk), in_specs=in_specs, out_specs=out_spec, out_shape=out_shape,
        scratch_shapes=[pltpu.VMEM((tm, tn), F32)] if nk > 1 else [],
        compiler_params=_cp("parallel", "parallel", "arbitrary"), name=name)(*ins)


def _norm_fwd(x, g, *, name):
    T, d = x.shape
    tr = _tile(T, 512, 8)

    def body(x_ref, g_ref, o_ref):
        xv = x_ref[...]
        r = lax.rsqrt(jnp.mean(xv * xv, axis=-1, keepdims=True) + EPS)
        o_ref[...] = (xv * r * g_ref[...]).astype(BF16)

    return pl.pallas_call(
        body, grid=(T // tr,),
        in_specs=[pl.BlockSpec((tr, d), lambda i: (i, 0)), pl.BlockSpec((1, d), lambda i: (0, 0))],
        out_specs=pl.BlockSpec((tr, d), lambda i: (i, 0)),
        out_shape=jax.ShapeDtypeStruct((T, d), BF16), compiler_params=_cp("parallel"), name=name)(x, g)


def _norm_bwd(x, g, dn, dres, *, name):
    T, d = x.shape
    tr = _tile(T, 512, 8)
    has_res = dres is not None

    def body(*refs):
        x_ref, g_ref, dn_ref = refs[:3]
        dr_ref = refs[3] if has_res else None
        dx_ref, dg_ref = refs[-2], refs[-1]

        @pl.when(pl.program_id(0) == 0)
        def _():
            dg_ref[...] = jnp.zeros_like(dg_ref)

        xv = x_ref[...]
        dnv = dn_ref[...].astype(F32)
        r = lax.rsqrt(jnp.mean(xv * xv, axis=-1, keepdims=True) + EPS)
        xh = xv * r
        dg_ref[...] += jnp.sum(dnv * xh, axis=0, keepdims=True)
        dxh = dnv * g_ref[...]
        dx = r * (dxh - xh * jnp.mean(dxh * xh, axis=-1, keepdims=True))
        if has_res:
            dx = dx + dr_ref[...]
        dx_ref[...] = dx

    row = pl.BlockSpec((tr, d), lambda i: (i, 0))
    vec = pl.BlockSpec((1, d), lambda i: (0, 0))
    ins = [x, g, dn] + ([dres] if has_res else [])
    return pl.pallas_call(
        body, grid=(T // tr,), in_specs=[row, vec, row] + ([row] if has_res else []),
        out_specs=[row, vec],
        out_shape=[jax.ShapeDtypeStruct((T, d), F32), jax.ShapeDtypeStruct((1, d), F32)],
        compiler_params=_cp("arbitrary"), name=name)(*ins)


def _final_loss(h, g, target, *, name):
    T, d = h.shape
    tr = _tile(T, 512, 8)

    def body(h_ref, g_ref, t_ref, l_ref, dh_ref, dg_ref):
        @pl.when(pl.program_id(0) == 0)
        def _():
            l_ref[...] = jnp.zeros_like(l_ref)
            dg_ref[...] = jnp.zeros_like(dg_ref)

        xv = h_ref[...]
        r = lax.rsqrt(jnp.mean(xv * xv, axis=-1, keepdims=True) + EPS)
        xh = xv * r
        e = xh * g_ref[...] - t_ref[...]
        l_ref[...] += jnp.sum(e * e, axis=0, keepdims=True)
        dy = e * (1.0 / d)
        dg_ref[...] += jnp.sum(dy * xh, axis=0, keepdims=True)
        dxh = dy * g_ref[...]
        dh_ref[...] = r * (dxh - xh * jnp.mean(dxh * xh, axis=-1, keepdims=True))

    row = pl.BlockSpec((tr, d), lambda i: (i, 0))
    vec = pl.BlockSpec((1, d), lambda i: (0, 0))
    return pl.pallas_call(
        body, grid=(T // tr,), in_specs=[row, vec, row], out_specs=[vec, row, vec],
        out_shape=[jax.ShapeDtypeStruct((1, d), F32), jax.ShapeDtypeStruct((T, d), F32),
                   jax.ShapeDtypeStruct((1, d), F32)],
        compiler_params=_cp("arbitrary"), name=name)(h, g, target)


def _shard_chunks(width, size=256):
    starts = list(range(0, width, size))
    if width - starts[-1] < EPI_COLS and len(starts) > 1:
        starts.pop()
    return [(o, (starts[i + 1] if i + 1 < len(starts) else width) - o) for i, o in enumerate(starts)]


def _gate_up_fwd(n, wsh, *, name):
    T, d = n.shape
    ws = wsh.shape[2]
    f = 2 * ws
    tm = _tile(T, 512, 8)

    def body(n_ref, wg_ref, wu_ref, g_ref, u_ref, a_ref):
        nv = n_ref[...].astype(MXU)
        for sh in range(2):
            for off, size in _shard_chunks(ws):
                gv = _nn(nv, wg_ref[sh, :, off:off + size].astype(MXU))
                uv = _nn(nv, wu_ref[sh, :, off:off + size].astype(MXU))
                sl = slice(sh * ws + off, sh * ws + off + size)
                g_ref[:, sl] = gv.astype(BF16)
                u_ref[:, sl] = uv.astype(BF16)
                a_ref[:, sl] = (gv * _sig(gv) * uv).astype(BF16)

    out = pl.BlockSpec((tm, f), lambda i: (i, 0))
    act = jax.ShapeDtypeStruct((T, f), BF16)
    return pl.pallas_call(
        body, grid=(T // tm,),
        in_specs=[pl.BlockSpec((tm, d), lambda i: (i, 0)), pl.BlockSpec((2, d, ws), lambda i: (0, 0, 0)),
                  pl.BlockSpec((2, d, ws), lambda i: (1, 0, 0))],
        out_specs=[out, out, out], out_shape=[act, act, act], compiler_params=_cp("parallel"),
        name=name)(n, wsh, wsh)


def _gate_up_bwd_input(dgate, dup, wsh, x, g, dres, *, name):
    T, f = dgate.shape
    four, d, ws = wsh.shape
    tm = _tile(T, 256, 8)

    def body(dg_ref, du_ref, w_ref, x_ref, g_ref, dr_ref, dx_ref, dgain_ref):
        @pl.when(pl.program_id(0) == 0)
        def _():
            dgain_ref[...] = jnp.zeros_like(dgain_ref)

        dn = None
        for sh in range(four):
            src = dg_ref if sh < 2 else du_ref
            part = _nt(src[:, (sh % 2) * ws:(sh % 2 + 1) * ws].astype(MXU), w_ref[sh].astype(MXU))
            dn = part if dn is None else dn + part
        xv = x_ref[...]
        r = lax.rsqrt(jnp.mean(xv * xv, axis=-1, keepdims=True) + EPS)
        xh = xv * r
        dgain_ref[...] += jnp.sum(dn * xh, axis=0, keepdims=True)
        dxh = dn * g_ref[...]
        dx_ref[...] = r * (dxh - xh * jnp.mean(dxh * xh, axis=-1, keepdims=True)) + dr_ref[...]

    act = pl.BlockSpec((tm, f), lambda i: (i, 0))
    row = pl.BlockSpec((tm, d), lambda i: (i, 0))
    vec = pl.BlockSpec((1, d), lambda i: (0, 0))
    return pl.pallas_call(
        body, grid=(T // tm,),
        in_specs=[act, act, pl.BlockSpec((four, d, ws), lambda i: (0, 0, 0)), row, vec, row],
        out_specs=[row, vec],
        out_shape=[jax.ShapeDtypeStruct((T, d), F32), jax.ShapeDtypeStruct((1, d), F32)],
        compiler_params=_cp("arbitrary"), name=name)(dgate, dup, wsh, x, g, dres)


def _act_bwd(dh, wd, gate, up, scale, *, name):
    T, d = dh.shape
    f = wd.shape[0]
    tm, tn = _tile(T, 512, 8), _tile(f, DFF)

    tc = _tile(tn, EPI_COLS)

    def body(dh_ref, wd_ref, g_ref, u_ref, dg_ref, du_ref):
        dhv = dh_ref[...].astype(MXU)
        for j in range(tn // tc):
            sl = slice(j * tc, (j + 1) * tc)
            da = scale * _nt(dhv, wd_ref[sl, :].astype(MXU))
            gv, uv = g_ref[:, sl].astype(F32), u_ref[:, sl].astype(F32)
            s = _sig(gv)
            dg_ref[:, sl] = (da * uv * (s * (1.0 + gv * (1.0 - s)))).astype(BF16)
            du_ref[:, sl] = (da * (gv * s)).astype(BF16)

    tile = pl.BlockSpec((tm, tn), lambda i, j: (i, j))
    act = jax.ShapeDtypeStruct((T, f), BF16)
    return pl.pallas_call(
        body, grid=(T // tm, f // tn),
        in_specs=[pl.BlockSpec((tm, d), lambda i, j: (i, 0)), pl.BlockSpec((tn, d), lambda i, j: (j, 0)), tile, tile],
        out_specs=[tile, tile], out_shape=[act, act], compiler_params=_cp("parallel", "parallel"),
        name=name)(dh, wd, gate, up)


CONV_ROWS = 64
CONV_PAD = 8


def _rows_down(ref, r0, d, cols=slice(None)):
    if r0 - d >= 0:
        return ref[pl.ds(r0 - d, CONV_ROWS), cols]
    assert r0 == 0
    v = ref[pl.ds(0, CONV_ROWS), cols]
    ri = lax.broadcasted_iota(jnp.int32, v.shape, 0)
    return jnp.where(ri >= d, pltpu.roll(v, d, 0), 0.0)


def _fold8(v):
    return jnp.sum(v.reshape(CONV_ROWS // 8, 8, v.shape[1]), axis=0)


def _taps(w_ref, views):
    acc = None
    for k, v in enumerate(views):
        t = w_ref[k:k + 1, :] * v
        acc = t if acc is None else acc + t
    return acc


def _conv_a_fwd(pp, w8, bl, s, *, name):
    tc = CA_TILE
    nb = D // tc
    bcol, ccol, vcol = slice(0, tc), slice(tc, 2 * tc), slice(2 * tc, 3 * tc)

    def body(p_ref, w_ref, o_ref):
        for r0 in range(0, s, CONV_ROWS):
            cv = [_rows_down(p_ref, r0, 2 - k, ccol) * _rows_down(p_ref, r0, 2 - k, vcol) for k in range(3)]
            o_ref[pl.ds(r0, CONV_ROWS), :] = (p_ref[pl.ds(r0, CONV_ROWS), bcol] * _taps(w_ref, cv)).astype(BF16)

    return pl.pallas_call(
        body, grid=(bl, nb),
        in_specs=[pl.BlockSpec((s, 3 * tc), lambda b, j: (b, j)), pl.BlockSpec((8, tc), lambda b, j: (0, j))],
        out_specs=pl.BlockSpec((s, tc), lambda b, j: (b, j)),
        out_shape=jax.ShapeDtypeStruct((bl * s, D), BF16), compiler_params=_cp("parallel", "parallel"),
        name=name)(pp, w8)


def _conv_a_bwd(pp, w8, dya, dpp, bl, s, *, name):
    tc = CA_TILE
    nb = D // tc
    bcol, ccol, vcol = slice(0, tc), slice(tc, 2 * tc), slice(2 * tc, 3 * tc)

    def body(p_ref, w_ref, dy_ref, dpp_in, d_ref, dw_ref, dcp):
        del dpp_in

        @pl.when(pl.program_id(1) == 0)
        def _():
            dw_ref[...] = jnp.zeros_like(dw_ref)

        dcp[pl.ds(s, CONV_PAD), :] = jnp.zeros((CONV_PAD, tc), F32)
        dw_acc = [jnp.zeros((8, tc), F32) for _ in range(3)]
        for r0 in reversed(range(0, s, CONV_ROWS)):
            rows = pl.ds(r0, CONV_ROWS)
            cs = [_rows_down(p_ref, r0, 2 - k, ccol) for k in range(3)]
            vs = [_rows_down(p_ref, r0, 2 - k, vcol) for k in range(3)]
            cv = [c_ * v_ for c_, v_ in zip(cs, vs)]
            dy = dy_ref[rows, :]
            d_ref[rows, bcol] = (dy * _taps(w_ref, cv)).astype(BF16)
            dconv = dy * p_ref[rows, bcol]
            dcp[rows, :] = dconv
            dcv = _taps(w_ref, [dcp[pl.ds(r0 + 2, CONV_ROWS), :], dcp[pl.ds(r0 + 1, CONV_ROWS), :], dconv])
            d_ref[rows, ccol] = (dcv * vs[2]).astype(BF16)
            d_ref[rows, vcol] = (dcv * cs[2]).astype(BF16)
            dw_acc = [acc + _fold8(dconv * cv_) for acc, cv_ in zip(dw_acc, cv)]
        for k in range(3):
            dw_ref[k:k + 1, :] += jnp.sum(dw_acc[k], axis=0, keepdims=True)

    wspec = pl.BlockSpec((8, tc), lambda j, b: (0, j))
    wide = pl.BlockSpec((s, 3 * tc), lambda j, b: (b, j))
    return pl.pallas_call(
        body, grid=(nb, bl),
        in_specs=[wide, wspec, pl.BlockSpec((s, tc), lambda j, b: (b, j)), pl.BlockSpec(memory_space=pl.ANY)],
        out_specs=[wide, wspec], out_shape=[jax.ShapeDtypeStruct(dpp.shape, dpp.dtype), jax.ShapeDtypeStruct((8, D), F32)],
        scratch_shapes=[pltpu.VMEM((s + CONV_PAD, tc), F32)], input_output_aliases={3: 0},
        compiler_params=_cp("parallel", "arbitrary"), name=name)(pp, w8, dya, dpp)


def _conv_ssm_fwd(pp, w8, bias, bl, s, *, name):
    tc = 256
    width = DI + 2 * NG * NS
    nb = width // tc

    def body(x_ref, w_ref, b_ref, o_ref):
        for r0 in range(0, s, CONV_ROWS):
            pre = _taps(w_ref, [_rows_down(x_ref, r0, 3 - k) for k in range(4)]) + b_ref[...]
            o_ref[pl.ds(r0, CONV_ROWS), :] = pre * _sig(pre)

    return pl.pallas_call(
        body, grid=(bl, nb),
        in_specs=[pl.BlockSpec((s, tc), lambda b, j: (b, O_XBC // tc + j)),
                  pl.BlockSpec((8, tc), lambda b, j: (0, j)), pl.BlockSpec((1, tc), lambda b, j: (0, j))],
        out_specs=pl.BlockSpec((s, tc), lambda b, j: (b, j)),
        out_shape=jax.ShapeDtypeStruct((bl * s, width), F32), compiler_params=_cp("parallel", "parallel"),
        name=name)(pp, w8, bias)


def _conv_ssm_bwd(pp, w8, bias, dxc, ch_off, dpp, bl, s, *, name):
    n = dxc.shape[1]
    tc = 256
    nb = n // tc
    o0 = ch_off // tc

    def body(x_ref, w_ref, b_ref, d_ref, dpp_in, dx_ref, dw_ref, db_ref, dp):
        del dpp_in

        @pl.when(pl.program_id(1) == 0)
        def _():
            dw_ref[...] = jnp.zeros_like(dw_ref)
            db_ref[...] = jnp.zeros_like(db_ref)

        dp[pl.ds(s, CONV_PAD), :] = jnp.zeros((CONV_PAD, tc), F32)
        dw_acc = [jnp.zeros((8, tc), F32) for _ in range(4)]
        db_acc = jnp.zeros((8, tc), F32)
        for r0 in reversed(range(0, s, CONV_ROWS)):
            rows = pl.ds(r0, CONV_ROWS)
            xs = [_rows_down(x_ref, r0, 3 - k) for k in range(4)]
            pre = _taps(w_ref, xs) + b_ref[...]
            sg = _sig(pre)
            dpre = d_ref[rows, :] * (sg * (1.0 + pre * (1.0 - sg)))
            dp[rows, :] = dpre
            dx = _taps(w_ref, [dp[pl.ds(r0 + 3 - k, CONV_ROWS), :] for k in range(3)] + [dpre])
            dx_ref[rows, :] = dx.astype(BF16)
            db_acc = db_acc + _fold8(dpre)
            dw_acc = [acc + _fold8(dpre * x_) for acc, x_ in zip(dw_acc, xs)]
        db_ref[...] += jnp.sum(db_acc, axis=0, keepdims=True)
        for k in range(4):
            dw_ref[k:k + 1, :] += jnp.sum(dw_acc[k], axis=0, keepdims=True)

    return pl.pallas_call(
        body, grid=(nb, bl),
        in_specs=[pl.BlockSpec((s, tc), lambda j, b: (b, O_XBC // tc + o0 + j)),
                  pl.BlockSpec((8, tc), lambda j, b: (0, o0 + j)), pl.BlockSpec((1, tc), lambda j, b: (0, o0 + j)),
                  pl.BlockSpec((s, tc), lambda j, b: (b, j)), pl.BlockSpec(memory_space=pl.ANY)],
        out_specs=[pl.BlockSpec((s, tc), lambda j, b: (b, O_XBC // tc + o0 + j)),
                   pl.BlockSpec((8, tc), lambda j, b: (0, j)), pl.BlockSpec((1, tc), lambda j, b: (0, j))],
        out_shape=[jax.ShapeDtypeStruct(dpp.shape, dpp.dtype), jax.ShapeDtypeStruct((8, n), F32),
                   jax.ShapeDtypeStruct((1, n), F32)],
        scratch_shapes=[pltpu.VMEM((s + CONV_PAD, tc), F32)], input_output_aliases={4: 0},
        compiler_params=_cp("parallel", "arbitrary"), name=name)(pp, w8, bias, dxc, dpp)


def _softplus(x):
    return jnp.maximum(x, 0.0) + jnp.log1p(jnp.exp(-jnp.abs(x)))


def _head_group_matrix():
    h = jnp.arange(128)[:, None]
    j = jnp.arange(NG * 128)[None, :]
    per = NH // NG
    return ((h < NH) & (j == (h // per) * 128 + h % per)).astype(F32)


def _dt_fwd(pp, bias128, *, name):
    T = pp.shape[0]
    tr = _tile(T, 1024, 8)
    per = NH // NG

    def body(x_ref, b_ref, p_ref, g_ref, t_ref):
        lane = lax.broadcasted_iota(jnp.int32, (tr, 128), 1)
        dt = jnp.where(lane < NH, _softplus(x_ref[...] + b_ref[...]), 0.0)
        g_ref[...] = _nn(dt, p_ref[...], HI)
        eye = (lax.broadcasted_iota(jnp.int32, (NH, 128), 0)
               == lax.broadcasted_iota(jnp.int32, (NH, 128), 1)).astype(F32)
        t_ref[...] = _dot(eye, dt, ((1,), (1,)), HI).reshape(NG, per, tr)

    vec = pl.BlockSpec((1, 128), lambda i: (0, 0))
    return pl.pallas_call(
        body, grid=(T // tr,),
        in_specs=[pl.BlockSpec((tr, 128), lambda i: (i, O_DT // 128)), vec,
                  pl.BlockSpec((128, NG * 128), lambda i: (0, 0))],
        out_specs=[pl.BlockSpec((tr, NG * 128), lambda i: (i, 0)), pl.BlockSpec((NG, per, tr), lambda i: (0, 0, i))],
        out_shape=[jax.ShapeDtypeStruct((T, NG * 128), F32), jax.ShapeDtypeStruct((NG, per, T), F32)],
        compiler_params=_cp("parallel"), name=name)(pp, bias128, _head_group_matrix())


def _dt_bwd(pp, bias128, ddtg, dpp, *, name):
    T = pp.shape[0]
    tr = _tile(T, 1024, 8)

    def body(x_ref, b_ref, d_ref, p_ref, dpp_in, o_ref, db_ref):
        del dpp_in

        @pl.when(pl.program_id(0) == 0)
        def _():
            db_ref[...] = jnp.zeros_like(db_ref)

        lane = lax.broadcasted_iota(jnp.int32, (tr, 128), 1)
        ddt = _dot(d_ref[...], p_ref[...], ((1,), (1,)), HI)
        dr = jnp.where(lane < NH, ddt * _sig(x_ref[...] + b_ref[...]), 0.0)
        db_ref[...] += jnp.sum(dr, axis=0, keepdims=True)
        o_ref[...] = dr.astype(BF16)

    col = pl.BlockSpec((tr, 128), lambda i: (i, O_DT // 128))
    vec = pl.BlockSpec((1, 128), lambda i: (0, 0))
    return pl.pallas_call(
        body, grid=(T // tr,),
        in_specs=[col, vec, pl.BlockSpec((tr, NG * 128), lambda i: (i, 0)), pl.BlockSpec((128, NG * 128), lambda i: (0, 0)),
                  pl.BlockSpec(memory_space=pl.ANY)],
        out_specs=[col, vec],
        out_shape=[jax.ShapeDtypeStruct(dpp.shape, dpp.dtype), jax.ShapeDtypeStruct((1, 128), F32)],
        input_output_aliases={4: 0}, compiler_params=_cp("arbitrary"),
        name=name)(pp, bias128, ddtg, _head_group_matrix(), dpp)


def _tril():
    return lax.broadcasted_iota(jnp.int32, (CH, CH), 0) >= lax.broadcasted_iota(jnp.int32, (CH, CH), 1)


def _ssd_common(dt, dtt, arow, acol):
    ri = lax.broadcasted_iota(jnp.int32, (CH, CH), 0)
    ci = lax.broadcasted_iota(jnp.int32, (CH, CH), 1)
    tril = ri >= ci
    triu = ri <= ci
    acs_col = _nn(tril.astype(F32), dt * arow, HI)
    acs_row = _nn(dtt * acol, triu.astype(F32), HI)
    return tril, triu, acs_col, acs_row


def _pair_terms(q, dt, acs_col, acs_row, tril, lo):
    ha, hb = 2 * q, 2 * q + 1
    col_a, col_b = acs_col[:, ha:ha + 1], acs_col[:, hb:hb + 1]
    row_a, row_b = acs_row[ha:ha + 1, :], acs_row[hb:hb + 1, :]
    last_a, last_b = acs_col[CH - 1:CH, ha:ha + 1], acs_col[CH - 1:CH, hb:hb + 1]
    out = dict(
        dtsel=jnp.where(lo, dt[:, ha:ha + 1], dt[:, hb:hb + 1]),
        d_a=jnp.exp(jnp.where(tril, col_a - row_a, NEG)), d_b=jnp.exp(jnp.where(tril, col_b - row_b, NEG)),
        esel=jnp.where(lo, jnp.exp(col_a), jnp.exp(col_b)),
        fsel=jnp.where(lo, jnp.exp(last_a - col_a), jnp.exp(last_b - col_b)),
        g_a=jnp.exp(last_a), g_b=jnp.exp(last_b))
    return out


def _ssd_fwd(xc, pp, dtg, dtt, arow, acol, dexp, ng, bl, s, *, name):
    nc = s // CH
    T = bl * s

    ex = SSD_EX if bl % SSD_EX == 0 else 1

    def body(*refs):
        arow_ref, acol_ref, dexp_ref, ng_ref = refs[6 * ex:6 * ex + 4]
        st_ref = refs[-1]

        @pl.when(pl.program_id(2) == 0)
        def _():
            st_ref[...] = jnp.zeros_like(st_ref)

        y_ref, yn_ref, prev_ref = refs[6 * ex + 4:6 * ex + 7]
        for e in range(ex):
            one(*refs[6 * e:6 * e + 6], arow_ref, acol_ref, dexp_ref, ng_ref,
                y_ref.at[e], yn_ref.at[e], prev_ref.at[e], st_ref.at[e])

    def one(xs_ref, bm_ref, cm_ref, z_ref, dt_ref, dtt_ref, arow_ref, acol_ref, dexp_ref, ng_ref,
            y_ref, yn_ref, prev_ref, st_ref):
        dt = dt_ref[...]
        tril, _, acs_col, acs_row = _ssd_common(dt, dtt_ref[...], -jnp.exp(arow_ref[...]), -jnp.exp(acol_ref[...]))
        bm, cm = bm_ref[...].astype(MXU), cm_ref[...].astype(MXU)
        cb = _nt(cm, bm)
        lo = lax.broadcasted_iota(jnp.int32, (CH, 128), 1) < HD
        sub_lo = lax.broadcasted_iota(jnp.int32, (128, NS), 0) < HD
        for q in range(4):
            t = _pair_terms(q, dt, acs_col, acs_row, tril, lo)
            x = xs_ref[:, 128 * q:128 * (q + 1)]
            xd = x * t["dtsel"]
            y = (_nn((cb * t["d_a"]).astype(MXU), jnp.where(lo, xd, 0.0).astype(MXU))
                 + _nn((cb * t["d_b"]).astype(MXU), jnp.where(lo, 0.0, xd).astype(MXU)))
            prev = st_ref[q]
            prev_ref[q] = prev
            y = y + t["esel"] * _nt(cm, prev.astype(MXU))
            st_ref[q] = prev * jnp.where(sub_lo, t["g_a"], t["g_b"]) + _tn((xd * t["fsel"]).astype(MXU), bm)
            y_ref[:, 128 * q:128 * (q + 1)] = y + dexp_ref[:, 128 * q:128 * (q + 1)] * x
        zv = z_ref[...]
        yg = y_ref[...] * (zv * _sig(zv))
        r = lax.rsqrt(jnp.mean(yg * yg, axis=-1, keepdims=True) + EPS)
        yn_ref[...] = (yg * r * ng_ref[...]).astype(BF16)

    def row(e, width, off_blocks):
        return pl.BlockSpec((CH, width), lambda g, b, c: ((b * ex + e) * nc + c, off_blocks + g))

    per_ex_in = [[row(e, GW, 0), row(e, NS, DI // NS), row(e, NS, DI // NS + NG), row(e, GW, O_Z // GW), row(e, 128, 0),
                  pl.BlockSpec((None, 8, CH), lambda g, b, c, e=e: (g, 0, (b * ex + e) * nc + c))] for e in range(ex)]
    by_example = pl.BlockSpec((ex, CH, GW), lambda g, b, c: (b, c, g))
    y, yn, prev = pl.pallas_call(
        body, grid=(NG, bl // ex, nc),
        in_specs=sum(per_ex_in, []) + [pl.BlockSpec((1, 128), lambda g, b, c: (0, g)),
                                       pl.BlockSpec((None, 8, 1), lambda g, b, c: (g, 0, 0)),
                                       pl.BlockSpec((1, GW), lambda g, b, c: (0, g)),
                                       pl.BlockSpec((1, GW), lambda g, b, c: (0, g))],
        out_specs=[by_example, by_example,
                   pl.BlockSpec((ex, None, 4, 128, NS), lambda g, b, c: (b, c, g, 0, 0))],
        out_shape=[jax.ShapeDtypeStruct((bl, s, DI), F32), jax.ShapeDtypeStruct((bl, s, DI), BF16),
                   jax.ShapeDtypeStruct((bl, nc, 16, 128, NS), F32)],
        scratch_shapes=[pltpu.VMEM((ex, 4, 128, NS), F32)],
        compiler_params=_cp("parallel", "parallel", "arbitrary"), name=name,
    )(*([xc, xc, xc, pp, dtg, dtt] * ex), arow, acol, dexp, ng)
    return y.reshape(T, DI), yn.reshape(T, DI), prev.reshape(bl * nc, 16, 128, NS)


def _ssd_bwd(dyn, y, xc, pp, dtg, dtt, arow, acol, dexp, ng, prev, dpp, bl, s, *, name):
    nc = s // CH
    T = bl * s

    def rsum(v):
        return jnp.sum(v, axis=1, keepdims=True)

    def asum(v):
        return jnp.sum(jnp.sum(v, axis=0, keepdims=True), axis=1, keepdims=True)

    ex = 1

    def body(*refs):
        shared = refs[9 * ex:9 * ex + 4]
        dz_ref, dxs_ref, db_ref, dc_ref, ddt_ref, dng_ref, dd_ref, dal_ref = refs[9 * ex + 5:9 * ex + 13]
        dst_ref = refs[-1]

        @pl.when((pl.program_id(1) == 0) & (pl.program_id(2) == 0))
        def _():
            dng_ref[...] = jnp.zeros_like(dng_ref)
            dd_ref[...] = jnp.zeros_like(dd_ref)
            dal_ref[...] = jnp.zeros_like(dal_ref)

        @pl.when(pl.program_id(2) == 0)
        def _():
            dst_ref[...] = jnp.zeros_like(dst_ref)

        for e in range(ex):
            one(*refs[9 * e:9 * e + 8], *shared, refs[9 * e + 8], dz_ref.at[e], dxs_ref.at[e], db_ref.at[e],
                dc_ref.at[e], ddt_ref.at[e], dng_ref, dd_ref, dal_ref, dst_ref.at[e])

    def one(dyn_ref, y_ref, xs_ref, bm_ref, cm_ref, z_ref, dt_ref, dtt_ref, arow_ref, acol_ref, dexp_ref, ng_ref,
            prev_ref, dz_ref, dxs_ref, db_ref, dc_ref, ddt_ref, dng_ref, dd_ref, dal_ref, dst_ref):
        yv, zv, xsv, dexp_v = y_ref[...], z_ref[...], xs_ref[...], dexp_ref[...]
        sz = _sig(zv)
        silu = zv * sz
        yg = yv * silu
        r = lax.rsqrt(jnp.mean(yg * yg, axis=-1, keepdims=True) + EPS)
        yh = yg * r
        dynv = dyn_ref[...]
        dng_ref[...] += jnp.sum(dynv * yh, axis=0, keepdims=True)
        dyh = dynv * ng_ref[...]
        dyg = r * (dyh - yh * jnp.mean(dyh * yh, axis=-1, keepdims=True))
        dz_ref[...] = (dyg * yv * (sz * (1.0 + zv * (1.0 - sz)))).astype(BF16)
        dy_all = dyg * silu
        dd_ref[...] += jnp.sum(dy_all * xsv, axis=0, keepdims=True)

        dt = dt_ref[...]
        arow_v = -jnp.exp(arow_ref[...])
        tril, triu, acs_col, acs_row = _ssd_common(dt, dtt_ref[...], arow_v, -jnp.exp(acol_ref[...]))
        bm, cm = bm_ref[...].astype(MXU), cm_ref[...].astype(MXU)
        cb = _nt(cm, bm)
        lane = lax.broadcasted_iota(jnp.int32, (CH, 128), 1)
        is_last = lax.broadcasted_iota(jnp.int32, (CH, 128), 0) == CH - 1
        lo = lane < HD
        sub_lo = lax.broadcasted_iota(jnp.int32, (128, NS), 0) < HD
        dcb = jnp.zeros((CH, CH), F32)
        dc_acc = jnp.zeros((CH, NS), F32)
        db_acc = jnp.zeros((CH, NS), F32)
        dacs = jnp.zeros((CH, 128), F32)
        ddtx = jnp.zeros((CH, 128), F32)
        csum = jnp.zeros((8, CH), F32)
        sub8 = lax.broadcasted_iota(jnp.int32, (8, CH), 0)
        for q in range(4):
            ha, hb = 2 * q, 2 * q + 1
            sl = slice(128 * q, 128 * (q + 1))
            t = _pair_terms(q, dt, acs_col, acs_row, tril, lo)
            x, dy = xsv[:, sl], dy_all[:, sl]
            xd = x * t["dtsel"]
            xd_m = xd.astype(MXU)
            dy_lo, dy_hi = jnp.where(lo, dy, 0.0).astype(MXU), jnp.where(lo, 0.0, dy).astype(MXU)
            m_a, m_b = cb * t["d_a"], cb * t["d_b"]
            prev_m = prev_ref[q].astype(MXU)
            dnext = dst_ref[q]
            dnext_m = dnext.astype(MXU)
            bds = _nt(bm, dnext_m)
            dxd = _tn(m_a.astype(MXU), dy_lo) + _tn(m_b.astype(MXU), dy_hi) + t["fsel"] * bds
            dye_m = (dy * t["esel"]).astype(MXU)
            dst_ref[q] = dnext * jnp.where(sub_lo, t["g_a"], t["g_b"]) + _tn(dye_m, cm)
            dm_a, dm_b = _nt(dy_lo, xd_m), _nt(dy_hi, xd_m)
            dcb = dcb + dm_a * t["d_a"] + dm_b * t["d_b"]
            g_a, g_b = dm_a * m_a, dm_b * m_b
            csum = (csum + jnp.where(sub8 == ha, jnp.sum(g_a, axis=0, keepdims=True), 0.0)
                    + jnp.where(sub8 == hb, jnp.sum(g_b, axis=0, keepdims=True), 0.0))
            tf = t["fsel"] * xd * bds
            tyf = dy * (t["esel"] * _nt(cm, prev_m)) - tf
            dpp = dnext * prev_ref[q]
            ea = asum(jnp.where(lo, tf, 0.0)) + t["g_a"] * asum(jnp.where(sub_lo, dpp, 0.0))
            eb = asum(jnp.where(lo, 0.0, tf)) + t["g_b"] * asum(jnp.where(sub_lo, 0.0, dpp))
            ra = rsum(g_a + jnp.where(lo, tyf, 0.0)) + jnp.where(is_last, ea, 0.0)
            rb = rsum(g_b + jnp.where(lo, 0.0, tyf)) + jnp.where(is_last, eb, 0.0)
            dacs = dacs + jnp.where(lane == ha, ra, 0.0) + jnp.where(lane == hb, rb, 0.0)
            tx = dxd * x
            ddtx = (ddtx + jnp.where(lane == ha, rsum(jnp.where(lo, tx, 0.0)), 0.0)
                    + jnp.where(lane == hb, rsum(jnp.where(lo, 0.0, tx)), 0.0))
            dxs_ref[:, sl] = dxd * t["dtsel"] + dexp_v[:, sl] * dy
            dc_acc = dc_acc + _nn(dye_m, prev_m)
            db_acc = db_acc + _nn((xd * t["fsel"]).astype(MXU), dnext_m)
        dcb_m = dcb.astype(MXU)
        dc_ref[...] = dc_acc + _nn(dcb_m, bm)
        db_ref[...] = db_acc + _tn(dcb_m, cm)
        dacs = dacs - jnp.concatenate([csum, jnp.zeros((CH - 8, CH), F32)], axis=0).T
        dla = _nn(triu.astype(F32), dacs, HI)
        ddt_ref[...] = dla * arow_v + ddtx
        dal_ref[...] += jnp.sum(dla * dt, axis=0, keepdims=True) * arow_v

    def row(e, width, off_blocks):
        return pl.BlockSpec((CH, width), lambda g, b, c: ((b * ex + e) * nc + nc - 1 - c, off_blocks + g))

    per_ex_in = [[row(e, GW, 0), row(e, GW, 0), row(e, GW, 0), row(e, NS, DI // NS), row(e, NS, DI // NS + NG),
                  row(e, GW, O_Z // GW), row(e, 128, 0),
                  pl.BlockSpec((None, 8, CH), lambda g, b, c, e=e: (g, 0, (b * ex + e) * nc + nc - 1 - c)),
                  pl.BlockSpec((None, 4, 128, NS), lambda g, b, c, e=e: ((b * ex + e) * nc + nc - 1 - c, g, 0, 0))]
                 for e in range(ex)]

    def by_example(width, off_blocks):
        return pl.BlockSpec((ex, CH, width), lambda g, b, c: (b, nc - 1 - c, off_blocks + g))

    gvec = pl.BlockSpec((1, GW), lambda g, b, c: (0, g))
    hvec = pl.BlockSpec((1, 128), lambda g, b, c: (0, g))
    out = pl.pallas_call(
        body, grid=(NG, bl // ex, nc),
        in_specs=sum(per_ex_in, []) + [hvec, pl.BlockSpec((None, 8, 1), lambda g, b, c: (g, 0, 0)), gvec, gvec,
                                       pl.BlockSpec(memory_space=pl.ANY)],
        out_specs=[by_example(GW, O_Z // GW), by_example(GW, 0), by_example(NS, 0), by_example(NS, 0),
                   by_example(128, 0), gvec, gvec, hvec],
        input_output_aliases={9 * ex + 4: 0},
        out_shape=[jax.ShapeDtypeStruct((bl, s, dpp.shape[1]), dpp.dtype), jax.ShapeDtypeStruct((bl, s, DI), F32),
                   jax.ShapeDtypeStruct((bl, s, NG * NS), F32), jax.ShapeDtypeStruct((bl, s, NG * NS), F32),
                   jax.ShapeDtypeStruct((bl, s, NG * 128), F32), jax.ShapeDtypeStruct((1, DI), F32),
                   jax.ShapeDtypeStruct((1, DI), F32), jax.ShapeDtypeStruct((1, NG * 128), F32)],
        scratch_shapes=[pltpu.VMEM((ex, 4, 128, NS), F32)],
        compiler_params=_cp("arbitrary", "arbitrary", "arbitrary"), name=name,
    )(*([dyn, y, xc, xc, xc, pp, dtg, dtt, prev] * ex), arow, acol, dexp, ng, dpp.reshape(bl, s, dpp.shape[1]))
    return (out[0].reshape(T, -1), out[1].reshape(T, DI), out[2].reshape(T, NG * NS), out[3].reshape(T, NG * NS),
            out[4].reshape(T, NG * 128), out[5], out[6], out[7])


def _merge_fwd(pp, ya, yb, *, name):
    T = ya.shape[0]
    tr = _tile(T, 512, 8)

    def body(ga_ref, gb_ref, ya_ref, yb_ref, o_ref):
        o_ref[...] = (_sig(ga_ref[...]) * ya_ref[...].astype(F32)
                      + _sig(gb_ref[...]) * yb_ref[...].astype(F32)).astype(BF16)

    row = pl.BlockSpec((tr, D), lambda i: (i, 0))
    return pl.pallas_call(
        body, grid=(T // tr,),
        in_specs=[pl.BlockSpec((tr, D), lambda i: (i, O_GA // D)), pl.BlockSpec((tr, D), lambda i: (i, O_GB // D)),
                  row, row],
        out_specs=row, out_shape=jax.ShapeDtypeStruct((T, D), BF16), compiler_params=_cp("parallel"),
        name=name)(pp, pp, ya, yb)


def _merge_bwd(pp, ya, yb, dm, *, name):
    T = ya.shape[0]
    tr = _tile(T, 512, 8)
    assert O_GB == O_GA + D and O_GA % (2 * D) == 0

    def body(g_ref, ya_ref, yb_ref, dm_ref, dya_ref, dyb_ref, dg_ref):
        sa, sb, dmv = _sig(g_ref[:, :D]), _sig(g_ref[:, D:]), dm_ref[...]
        dya_ref[...] = (dmv * sa).astype(BF16)
        dyb_ref[...] = (dmv * sb).astype(BF16)
        dg_ref[:, :D] = (dmv * ya_ref[...].astype(F32) * (sa * (1.0 - sa))).astype(BF16)
        dg_ref[:, D:] = (dmv * yb_ref[...].astype(F32) * (sb * (1.0 - sb))).astype(BF16)

    row = pl.BlockSpec((tr, D), lambda i: (i, 0))
    gates = pl.BlockSpec((tr, 2 * D), lambda i: (i, O_GA // (2 * D)))
    act = jax.ShapeDtypeStruct((T, D), BF16)
    return pl.pallas_call(
        body, grid=(T // tr,), in_specs=[gates, row, row, row], out_specs=[row, row, gates],
        out_shape=[act, act, jax.ShapeDtypeStruct((T, NPP), BF16)], compiler_params=_cp("parallel"),
        name=name)(pp, ya, yb, dm)


def _softmax_rows(sc):
    e = jnp.exp(sc - jnp.max(sc, axis=-1, keepdims=True))
    return e / jnp.sum(e, axis=-1, keepdims=True)


def _attn_fwd(q, kv, bl, s, *, name):
    m = kv.shape[0] // bl
    tq = _tile(s, 1024)
    nq = s // tq
    scale = 1.0 / math.sqrt(XD)

    def body(q_ref, k_ref, v_ref, o_ref):
        p = _softmax_rows(_nt(q_ref[...], k_ref[...]) * scale)
        o_ref[...] = _nn(p.astype(MXU), v_ref[...]).astype(BF16)

    qspec = pl.BlockSpec((tq, XD), lambda b, h, i: (b * nq + i, h))
    return pl.pallas_call(
        body, grid=(bl, XH, nq),
        in_specs=[qspec, pl.BlockSpec((m, XD), lambda b, h, i: (b, h)),
                  pl.BlockSpec((m, XD), lambda b, h, i: (b, XH + h))],
        out_specs=qspec, out_shape=jax.ShapeDtypeStruct((bl * s, D), BF16),
        compiler_params=_cp("parallel", "parallel", "parallel"), name=name)(q, kv, kv)


def _attn_bwd(q, kv, do, bl, s, *, name):
    m = kv.shape[0] // bl
    tq = _tile(s, 1024)
    nq = s // tq
    scale = 1.0 / math.sqrt(XD)

    def body(q_ref, k_ref, v_ref, do_ref, dq_ref, dk_ref, dv_ref):
        @pl.when(pl.program_id(2) == 0)
        def _():
            dk_ref[...] = jnp.zeros_like(dk_ref)
            dv_ref[...] = jnp.zeros_like(dv_ref)

        qv, kvv, vv, dov = q_ref[...], k_ref[...], v_ref[...], do_ref[...]
        p = _softmax_rows(_nt(qv, kvv) * scale)
        dp = _nt(dov, vv)
        ds = (p * (dp - jnp.sum(dp * p, axis=-1, keepdims=True)) * scale).astype(MXU)
        dq_ref[...] = _nn(ds, kvv).astype(BF16)
        dk_ref[...] += _tn(ds, qv)
        dv_ref[...] += _tn(p.astype(MXU), dov)

    qspec = pl.BlockSpec((tq, XD), lambda b, h, i: (b * nq + i, h))
    kspec = pl.BlockSpec((m, XD), lambda b, h, i: (b, h))
    return pl.pallas_call(
        body, grid=(bl, XH, nq),
        in_specs=[qspec, kspec, pl.BlockSpec((m, XD), lambda b, h, i: (b, XH + h)), qspec],
        out_specs=[qspec, kspec, kspec],
        out_shape=[jax.ShapeDtypeStruct((bl * s, D), BF16), jax.ShapeDtypeStruct((bl * m, D), F32),
                   jax.ShapeDtypeStruct((bl * m, D), F32)],
        compiler_params=_cp("parallel", "parallel", "arbitrary"), name=name)(q, kv, kv, do)


def _row_tile(r, c, max_elems=512 * 1024, align=16):
    best = None
    for t in range(align, r + 1, align):
        if r % t == 0 and t * c <= max_elems:
            best = t
    return best if best is not None else r


def _addn(a, others, *, name, also_bf16=False):
    r, c = a.shape
    tr = _row_tile(r, c)
    n = len(others)

    def body(*refs):
        acc = refs[0][...].astype(F32)
        for o_ref in refs[1:1 + n]:
            acc = acc + o_ref[...].astype(F32)
        refs[1 + n][...] = acc
        if also_bf16:
            refs[2 + n][...] = acc.astype(BF16)

    spec = pl.BlockSpec((tr, c), lambda i: (i, 0))
    shapes = [jax.ShapeDtypeStruct((r, c), F32)] + ([jax.ShapeDtypeStruct((r, c), BF16)] if also_bf16 else [])
    out = pl.pallas_call(
        body, grid=(r // tr,), in_specs=[spec] * (1 + n), out_specs=[spec] * len(shapes), out_shape=shapes,
        compiler_params=_cp("parallel"), name=name)(a, *others)
    return out if also_bf16 else out[0]


def _sum_leading(a, *, name):
    n, r, c = a.shape

    def body(a_ref, o_ref):
        acc = a_ref[0]
        for i in range(1, n):
            acc = acc + a_ref[i]
        o_ref[...] = acc

    return pl.pallas_call(body, out_shape=jax.ShapeDtypeStruct((r, c), F32), name=name)(a)


def _adamw_math(wv, gv, mv, vv):
    m2 = ADAM_B1 * mv + (1.0 - ADAM_B1) * gv
    v2 = ADAM_B2 * vv + (1.0 - ADAM_B2) * (gv * gv)
    m_hat = m2 / (1.0 - ADAM_B1 ** ADAM_STEP)
    v_hat = v2 / (1.0 - ADAM_B2 ** ADAM_STEP)
    return -ADAM_LR * (m_hat / (jnp.sqrt(v_hat) + ADAM_EPS) + ADAM_WD * wv), m2, v2


def _adamw(w, g, m, v, *, name):
    r, c = w.shape
    tr = _row_tile(r, c, align=8)

    def body(w_ref, g_ref, m_ref, v_ref, d_ref, mo_ref, vo_ref):
        d_ref[...], mo_ref[...], vo_ref[...] = _adamw_math(w_ref[...], g_ref[...], m_ref[...], v_ref[...])

    spec = pl.BlockSpec((tr, c), lambda i: (i, 0))
    shp = jax.ShapeDtypeStruct((r, c), F32)
    return pl.pallas_call(
        body, grid=(r // tr,), in_specs=[spec] * 4, out_specs=[spec] * 3, out_shape=[shp] * 3,
        compiler_params=_cp("parallel"), name=name)(w, g, m, v)


def _adamw_halves(w, g_mine, g_other, m, v, c, *, name):
    _, r, cols = w.shape
    h = r // 2
    tr = _row_tile(h, cols, align=8)
    nh = h // tr

    def body(c_ref, w_ref, gm_ref, go_ref, m_ref, v_ref, g_ref, d_ref, mo_ref, vo_ref):
        gv = jnp.where(pl.program_id(0) // nh == c_ref[0], gm_ref[...], go_ref[...])
        g_ref[...] = gv
        d_ref[...], mo_ref[...], vo_ref[...] = _adamw_math(w_ref[...], gv, m_ref[...], v_ref[...])

    full = pl.BlockSpec((None, tr, cols), lambda i, c_: (0, i, 0))
    half = pl.BlockSpec((tr, cols), lambda i, c_: (i % nh, 0))
    shp = jax.ShapeDtypeStruct((1, r, cols), F32)
    return pl.pallas_call(
        body,
        grid_spec=pltpu.PrefetchScalarGridSpec(num_scalar_prefetch=1, grid=(2 * nh,),
                                               in_specs=[full, half, half, full, full], out_specs=[full] * 4),
        out_shape=[shp] * 4, compiler_params=_cp("parallel"), name=name,
    )(jnp.reshape(c, (1,)).astype(jnp.int32), w, g_mine, g_other, m, v)


def _flip(i, d):
    return 1 - i if d else i


def _comm(name, ins, out_shapes, n_remote, n_local, plan, aliases=None):
    n_in, n_out = len(ins), len(out_shapes)

    def body(*refs):
        in_refs, out_refs = refs[:n_in], refs[n_in:n_in + n_out]
        send_sems, recv_sems = refs[n_in + n_out], refs[n_in + n_out + 1]
        x, y, c = lax.axis_index("x"), lax.axis_index("y"), lax.axis_index("c")
        remote, local = plan(in_refs, out_refs, x, y, c)
        assert len(remote) == n_remote and len(local) == n_local
        copies = []
        if n_local:
            loc_sems = refs[n_in + n_out + 2]
            copies += [pltpu.make_async_copy(s_, d_, loc_sems.at[i]) for i, (s_, d_) in enumerate(local)]
        copies += [pltpu.make_async_remote_copy(src_ref=s_, dst_ref=d_, send_sem=send_sems.at[i],
                                                recv_sem=recv_sems.at[i], device_id=dev, device_id_type=MESH)
                   for i, (s_, d_, dev) in enumerate(remote)]
        for cp in copies:
            cp.start()
        for cp in copies:
            cp.wait()

    hbm = pl.BlockSpec(memory_space=pl.ANY)
    scratch = [pltpu.SemaphoreType.DMA((n_remote,)), pltpu.SemaphoreType.DMA((n_remote,))]
    if n_local:
        scratch.append(pltpu.SemaphoreType.DMA((n_local,)))
    return pl.pallas_call(
        body, in_specs=[hbm] * n_in, out_specs=[hbm] * n_out, out_shape=out_shapes, scratch_shapes=scratch,
        input_output_aliases=aliases or {}, compiler_params=pltpu.CompilerParams(has_side_effects=True),
        name=name)(*ins)


HBM_SPEC = pl.BlockSpec(memory_space=pltpu.HBM)
SEM_SPEC = pl.BlockSpec(memory_space=pltpu.SEMAPHORE)
DATAFLOW = pltpu.SideEffectType.DATAFLOW_SIDE_EFFECTING


def _remote_copies(plan, srcs, lands, send_sems, recv_sems, n_copies):
    x, y, c = lax.axis_index("x"), lax.axis_index("y"), lax.axis_index("c")
    copies = plan(srcs, lands, x, y, c)
    assert len(copies) == n_copies
    return [pltpu.make_async_remote_copy(src_ref=s_, dst_ref=d_, send_sem=send_sems.at[i], recv_sem=recv_sems.at[i],
                                         device_id=dev, device_id_type=MESH) for i, (s_, d_, dev) in enumerate(copies)]


def _split_start(name, srcs, lands, n_copies, plan, after=None):
    ns, nb = len(srcs), len(srcs) + len(lands)
    n_after = 0 if after is None else 1
    n_in = nb + n_after

    def body(*refs):
        for cp in _remote_copies(plan, refs[:ns], refs[ns:nb], refs[n_in], refs[n_in + 1], n_copies):
            cp.start()
        refs[-1][...] = jnp.zeros_like(refs[-1])

    arrays = [pltpu.with_memory_space_constraint(a_, pltpu.HBM) for a_ in list(srcs) + list(lands)]
    out = pl.pallas_call(
        body, name=name,
        out_shape=(pltpu.SemaphoreType.DMA((n_copies,)), pltpu.SemaphoreType.DMA((n_copies,)),
                   *[pltpu.HBM(a_.shape, a_.dtype) for a_ in arrays], jax.ShapeDtypeStruct((8, 128), F32)),
        in_specs=[HBM_SPEC] * nb + [pl.BlockSpec(memory_space=pl.ANY)] * n_after,
        out_specs=(SEM_SPEC, SEM_SPEC, *[HBM_SPEC] * nb, pl.BlockSpec(memory_space=pltpu.VMEM)),
        input_output_aliases={i: 2 + i for i in range(nb)},
        compiler_params=pltpu.CompilerParams(has_side_effects=DATAFLOW))(*arrays, *([after] * n_after))
    return (out[0], out[1], list(out[2:2 + nb])), out[-1]


def _split_wait(name, handle, ns, n_copies, plan, after):
    send_sems, recv_sems, bufs = handle
    nb = len(bufs)

    def body(*refs):
        for cp in _remote_copies(plan, refs[:ns], refs[ns:nb], refs[nb], refs[nb + 1], n_copies):
            cp.wait_send()
            cp.wait_recv()

    out = pl.pallas_call(
        body, name=name, out_shape=[pltpu.HBM(b_.shape, b_.dtype) for b_ in bufs],
        in_specs=[HBM_SPEC] * nb + [SEM_SPEC, SEM_SPEC, pl.BlockSpec(memory_space=pl.ANY)],
        out_specs=[HBM_SPEC] * nb, input_output_aliases={i: i for i in range(nb)},
        compiler_params=pltpu.CompilerParams(has_side_effects=DATAFLOW))(*bufs, send_sems, recv_sems, after)
    return list(out[ns:])


def _gather_start(shards, tag, after=None):
    n = len(shards)
    lands = [lax.empty((4,) + s.shape, s.dtype) for s in shards]

    def plan(srcs, dsts, x, y, c):
        k = 2 * x + y
        copies = []
        for w_ref, o_ref in zip(srcs, dsts):
            h = w_ref.shape[0] // 2
            rows = pl.ds(c * h, h)
            copies += [(w_ref.at[rows], o_ref.at[k, rows], (_flip(x, dx), _flip(y, dy), c)) for dx, dy in CHIP_FLIPS]
        return copies

    handle, token = _split_start(f"gather_{tag}_start", shards, lands, 3 * n, plan, after)
    return (handle, plan, n), token


def _gather_wait(started, after, tag):
    handle, plan, n = started
    return _split_wait(f"gather_{tag}_wait", handle, n, 3 * n, plan, after)


def _gather_d2d(lands, before, tag):
    n = len(lands)

    def plan_d2d(in_refs, out_refs, x, y, c):
        remote = []
        for o_ref in out_refs:
            h = o_ref.shape[1] // 2
            for dx, dy in CHIP_FLIPS:
                half = o_ref.at[2 * _flip(x, dx) + _flip(y, dy), pl.ds(c * h, h)]
                remote.append((half, half, (x, y, 1 - c)))
        return remote, []

    return _comm(f"gather_{tag}_d2d", list(lands) + list(before),
                 [jax.ShapeDtypeStruct(l_.shape, l_.dtype) for l_ in lands], 3 * n, 0, plan_d2d,
                 aliases={i: i for i in range(n)})


def _pair_plan(in_refs, out_refs, x, y, c):
    return [(i_, o_, (x, y, 1 - c)) for i_, o_ in zip(in_refs, out_refs)], []


def _rs_start(grads, tag):
    n = len(grads)
    c = lax.axis_index("c")
    def rows(g, start, h):
        return g.rows(start, h) if isinstance(g, _WInGrad) else lax.dynamic_slice_in_dim(g, start, h, axis=1)

    halves = [g.shape[1] // 2 for g in grads]
    mine = [rows(g, c * h, h) for g, h in zip(grads, halves)]
    send_a = [rows(g, (1 - c) * h, h).astype(BF16) for g, h in zip(grads, halves)]
    recv_a = _comm(f"rs_pair_{tag}", send_a, [jax.ShapeDtypeStruct(s.shape, BF16) for s in send_a], n, 0, _pair_plan)
    pair, pair_b = [], []
    for i, (mi, ra) in enumerate(zip(mine, recv_a)):
        four, h, cols = mi.shape
        p32, p16 = _addn(mi.reshape(four * h, cols), [ra.reshape(four * h, cols)], name=f"rs_pair_sum_{tag}_{i}",
                         also_bf16=True)
        pair.append(p32.reshape(four, h, cols))
        pair_b.append(p16.reshape(four, h, cols))

    def plan(srcs, dsts, x, y, c_):
        copies = []
        for i_, o_ in zip(srcs, dsts):
            for j, (dx, dy) in enumerate(CHIP_FLIPS):
                fx, fy = _flip(x, dx), _flip(y, dy)
                copies.append((i_.at[2 * fx + fy], o_.at[j], (fx, fy, c_)))
        return copies

    lands = [lax.empty((3,) + p.shape[1:], BF16) for p in pair_b]
    handle, token = _split_start(f"rs_chips_{tag}_start", pair_b, lands, 3 * n, plan)
    return (handle, plan, n, pair), token


def _rs_finish(started, after, tag):
    handle, plan, n, pair = started
    recv_b = _split_wait(f"rs_chips_{tag}_wait", handle, n, 3 * n, plan, after)
    k = 2 * lax.axis_index("x") + lax.axis_index("y")
    tot = [_addn(lax.dynamic_index_in_dim(p, k, 0, keepdims=False), [rb[0], rb[1], rb[2]],
                 name=f"rs_chip_sum_{tag}_{i}") for i, (p, rb) in enumerate(zip(pair, recv_b))]
    other = _comm(f"rs_halves_{tag}", tot, [jax.ShapeDtypeStruct(t.shape, F32) for t in tot], n, 0, _pair_plan)
    return tot, other


def _gather_all(vec, *, name, before=()):
    out = jax.ShapeDtypeStruct((8,) + vec.shape, vec.dtype)

    def plan(in_refs, out_refs, x, y, c):
        me = 4 * x + 2 * y + c
        remote = [(in_refs[0], out_refs[0].at[me], (_flip(x, dx), _flip(y, dy), _flip(c, dc)))
                  for dx in (0, 1) for dy in (0, 1) for dc in (0, 1) if (dx, dy, dc) != (0, 0, 0)]
        return remote, [(in_refs[0], out_refs[0].at[me])]

    return _comm(name, [vec] + list(before), [out], 7, 1, plan)[0]


def _pack(parts):
    flat = [p.reshape(-1).astype(F32) for p in parts]
    total = sum(f.shape[0] for f in flat)
    n = -(-total // 1024) * 128
    vec = jnp.concatenate(flat + [jnp.zeros((8 * n - total,), F32)]).reshape(8, n)
    offs, o = [], 0
    for f in flat:
        offs.append((o, f.shape[0]))
        o += f.shape[0]
    return vec, offs


def _unpack(vec, offs, shapes):
    flat = vec.reshape(-1)
    return [flat[o:o + n].reshape(s) for (o, n), s in zip(offs, shapes)]


BIG = (("ffn1_w_gate_up", "col"), ("ffn1_w_down", "row"), ("w_in", "col"), ("w_out_a", "row"), ("w_out_ssm", "row"),
       ("w_mix_out", "row"), ("w_q", "row"), ("w_kv", "col"), ("w_o_x", "row"), ("ffn2_w_gate_up", "col"),
       ("ffn2_w_down", "row"))
SMALL = ("ffn1_norm", "mix_norm", "conv_a_w", "ssm_conv_w", "ssm_conv_b", "ssm_dt_bias", "ssm_a_log", "ssm_d",
         "ssm_norm", "xattn_norm", "mem_norm", "ffn2_norm", "final_norm")
WEIGHTS = ("ffn1_norm", "ffn1_w_gate_up", "ffn1_w_down", "mix_norm", "w_in", "conv_a_w", "w_out_a", "ssm_conv_w",
           "ssm_conv_b", "ssm_dt_bias", "ssm_a_log", "ssm_d", "ssm_norm", "w_out_ssm", "w_mix_out", "xattn_norm",
           "mem_norm", "w_q", "w_kv", "w_o_x", "ffn2_norm", "ffn2_w_gate_up", "ffn2_w_down", "final_norm")


GATHER_GROUPS = (("a", ("ffn1_w_gate_up",)), ("b", ("ffn1_w_down", "w_in")),
                 ("c", ("w_out_a", "w_out_ssm", "w_mix_out", "w_q", "w_kv", "w_o_x", "ffn2_w_gate_up", "ffn2_w_down")))


def _place_own(land, own, k, *, name):
    four, r, cols = land.shape
    tr = _row_tile(r, cols)

    def body(k_ref, own_ref, land_in, o_ref):
        del k_ref, land_in
        o_ref[...] = own_ref[...]

    return pl.pallas_call(
        body,
        grid_spec=pltpu.PrefetchScalarGridSpec(
            num_scalar_prefetch=1, grid=(r // tr,),
            in_specs=[pl.BlockSpec((tr, cols), lambda i, k_: (i, 0)), pl.BlockSpec(memory_space=pl.ANY)],
            out_specs=pl.BlockSpec((None, tr, cols), lambda i, k_: (k_[0], i, 0))),
        out_shape=jax.ShapeDtypeStruct(land.shape, land.dtype), input_output_aliases={2: 0},
        compiler_params=_cp("parallel"), name=name)(jnp.reshape(k, (1,)).astype(jnp.int32), own, land)


def _full_weight(land, own, kind, k, *, name):
    land = _place_own(land, own, k, name=name)
    four, r, cols = land.shape
    if kind == "row":
        return land.reshape(four * r, cols)
    return jnp.transpose(land, (1, 0, 2)).reshape(r, four * cols)


class _GatheredWeights:
    def __init__(self, shards32, k, after):
        first = GATHER_GROUPS[0][1]
        self.shards, self.k = {n: shards32[n].astype(BF16)[0] for n in first}, k
        self.full = {}
        self.n_done = 0
        self.started, token = self._start(0, after)
        self.token = token[0, 0]
        self.shards.update({n: (w + token[0, 0]).astype(BF16)[0] for n, w in shards32.items() if n not in first})
        self.after = jnp.stack([self.shards[n][0, 0] for n in shards32 if n not in first]).astype(F32).reshape(1, -1)

    def _start(self, gi, after):
        tag, names = GATHER_GROUPS[gi]
        return _gather_start([self.shards[n] for n in names], tag, after)

    def mark(self, value):
        self.after = value

    def __getitem__(self, name):
        if name not in self.full:
            tag, names = GATHER_GROUPS[self.n_done]
            assert name in names, (name, tag)
            lands = _gather_wait(self.started, self.after, tag)
            before = []
            if self.n_done + 1 < len(GATHER_GROUPS):
                self.started, token = self._start(self.n_done + 1, lands[0])
                before = [token]
            lands = _gather_d2d(lands, before, tag)
            for n, land in zip(names, lands):
                if n == "w_in":
                    self.full[n] = _pad_w_in_shards(_place_own(land, self.shards[n], self.k, name=f"own_{n}"))
                elif n.endswith("w_gate_up"):
                    self.full[n] = _place_own(land, self.shards[n], self.k, name=f"own_{n}")
                else:
                    self.full[n] = _full_weight(land, self.shards[n], dict(BIG)[n], self.k, name=f"own_{n}")
            self.n_done += 1
        return self.full[name]


def _shard_major(dw, kind):
    if isinstance(dw, tuple):
        return jnp.concatenate(dw, axis=0)
    if dw.ndim == 3:
        return dw
    if kind == "row":
        return dw.reshape(4, dw.shape[0] // 4, dw.shape[1])
    return jnp.transpose(dw.reshape(dw.shape[0], 4, dw.shape[1] // 4), (1, 0, 2))


def _pad_rows8(w):
    return jnp.concatenate([w, jnp.zeros((8 - w.shape[0], w.shape[1]), w.dtype)], axis=0)


def _group_lanes(v):
    r = v.shape[0]
    return jnp.pad(v.reshape(r, NG, NH // NG), ((0, 0), (0, 0), (0, 128 - NH // NG))).reshape(r, NG * 128)


def _ungroup_lanes(v):
    r = v.shape[0]
    return v.reshape(r, NG, 128)[:, :, :NH // NG].reshape(r, NH)


def _local_step(wfull, small, x, mem, target, token=0.0, on_grads=None):
    bl, s, _ = x.shape
    T = bl * s
    x2, t2 = x.reshape(T, D), target.reshape(T, D)
    mem2 = mem.reshape(-1, D)
    g = {}
    tok = [token]
    mark = getattr(wfull, "mark", lambda value: None)

    def gain(name):
        return small[name].reshape(1, -1) + tok[0]

    def emit(tag, names):
        if on_grads is not None:
            tok[0] = tok[0] + on_grads(tag, {n: g[n] for n in names})

    def ffn_fwd(h, n, wgu, wd, tag, next_gain=None):
        gate, up, a = _gate_up_fwd(n, wfull[wgu], name=f"{tag}_gate_up")
        mark(a)
        out = _mm(a, wfull[wd], "nn", tk=DFF, scale=FFN_RES, residual=h, norm_gain=next_gain, name=f"{tag}_down")
        return out, (n, gate, up, a)

    def ffn_bwd(dh, h, norm, wgu, wd, saved, tag):
        n, gate, up, a = saved
        dgate, dup = _act_bwd(dh, wfull[wd], gate, up, FFN_RES, name=f"{tag}_d_act")
        g[wd] = _mm(a, dh, "tn", tm=1408, scale=FFN_RES, name=f"{tag}_d_w_down")
        g[wgu] = (_mm(n, dgate, "tn", tn=1408, col_shards=2, name=f"{tag}_d_w_gate"),
                  _mm(n, dup, "tn", tn=1408, col_shards=2, name=f"{tag}_d_w_up"))
        emit(tag, (wgu, wd))
        dh_in, g[norm] = _gate_up_bwd_input(dgate, dup, wfull[wgu], h, gain(norm), dh, name=f"{tag}_d_norm")
        return dh_in

    n1 = _norm_fwd(x2, gain("ffn1_norm"), name="ffn1_norm")
    (h1, u), ffn1_saved = ffn_fwd(x2, n1, "ffn1_w_gate_up", "ffn1_w_down", "ffn1", gain("mix_norm"))
    mark(h1)
    pp = _mm(u, wfull["w_in"], "nn", tm=2048, tn=1152, name="in_proj")
    wa8 = _pad_rows8(small["conv_a_w"])
    ws8 = _pad_rows8(small["ssm_conv_w"])
    conv_b = gain("ssm_conv_b")
    bias128 = jnp.pad(gain("ssm_dt_bias"), ((0, 0), (0, 128 - NH)))
    ya_pre = _conv_a_fwd(pp, wa8, bl, s, name="conv_a")
    xc = _conv_ssm_fwd(pp, ws8, conv_b, bl, s, name="conv_ssm")
    mark(xc)
    alog = gain("ssm_a_log")
    dtg, dtt = _dt_fwd(pp, bias128, name="dt")
    arow, acol = _group_lanes(alog), alog.reshape(NG, NH // NG, 1)
    dexp = jnp.repeat(gain("ssm_d"), HD, axis=1)
    ng = gain("ssm_norm")
    y, yn, prev = _ssd_fwd(xc, pp, dtg, dtt, arow, acol, dexp, ng, bl, s, name="ssd")
    ya = _mm(ya_pre, wfull["w_out_a"], "nn", tn=1024, out_dtype=BF16, name="out_a")
    yb = _mm(yn, wfull["w_out_ssm"], "nn", tn=1024, tk=DI, out_dtype=BF16, name="out_ssm")
    merged = _merge_fwd(pp, ya, yb, name="merge")
    h2, un = _mm(merged, wfull["w_mix_out"], "nn", residual=h1, norm_gain=gain("xattn_norm"), name="mix_out")
    q = _mm(un, wfull["w_q"], "nn", tn=1024, out_dtype=BF16, name="q_proj")
    mn = _norm_fwd(mem2, gain("mem_norm"), name="mem_norm")
    kv = _mm(mn, wfull["w_kv"], "nn", tn=1024, out_dtype=BF16, name="kv_proj")
    o = _attn_fwd(q, kv, bl, s, name="attn")
    h3, n2 = _mm(o, wfull["w_o_x"], "nn", residual=h2, norm_gain=gain("ffn2_norm"), name="attn_out")
    h4, ffn2_saved = ffn_fwd(h3, n2, "ffn2_w_gate_up", "ffn2_w_down", "ffn2")
    sq_err, dh4, dgf = _final_loss(h4, gain("final_norm"), t2, name="final_loss")
    g["final_norm"] = dgf

    dh3 = ffn_bwd(dh4, h3, "ffn2_norm", "ffn2_w_gate_up", "ffn2_w_down", ffn2_saved, "ffn2")
    do = _mm(dh3, wfull["w_o_x"], "nt", tn=1024, out_dtype=BF16, name="d_attn_o")
    g["w_o_x"] = _mm(o, dh3, "tn",name="d_w_o_x")
    dq, dk, dv = _attn_bwd(q, kv, do, bl, s, name="d_attn")
    dun = _mm(dq, wfull["w_q"], "nt", tn=1024, name="d_xattn_norm_out")
    g["w_q"] = _mm(un, dq, "tn",name="d_w_q")
    dkv = jnp.concatenate([dk, dv], axis=1)
    dmn = _mm(dkv, wfull["w_kv"], "nt", tn=1024, tk=2 * D, name="d_mem_norm_out")
    g["w_kv"] = _mm(mn, dkv, "tn", tn=512, col_shards=4, name="d_w_kv")
    emit("attn", ("w_q", "w_kv", "w_o_x"))
    _, g["mem_norm"] = _norm_bwd(mem2, gain("mem_norm"), dmn, None, name="d_mem_norm")
    dh2, g["xattn_norm"] = _norm_bwd(h2, gain("xattn_norm"), dun, dh3, name="d_xattn_norm")
    dmerged = _mm(dh2, wfull["w_mix_out"], "nt", tn=1024, name="d_merged")
    g["w_mix_out"] = _mm(merged, dh2, "tn",name="d_w_mix_out")
    dya, dyb, dpp = _merge_bwd(pp, ya, yb, dmerged, name="d_merge")
    dya_pre = _mm(dya, wfull["w_out_a"], "nt", tn=1024, name="d_conv_a_out")
    g["w_out_a"] = _mm(ya_pre, dya, "tn",name="d_w_out_a")
    dyn = _mm(dyb, wfull["w_out_ssm"], "nt", tn=DI, name="d_ssd_out")
    g["w_out_ssm"] = _mm(yn, dyb, "tn",name="d_w_out_ssm")
    dpp, dwa8 = _conv_a_bwd(pp, wa8, dya_pre, dpp, bl, s, name="d_conv_a")
    g["conv_a_w"] = dwa8[:3]
    dpp, dxs, dbm, dcm, ddtg, g["ssm_norm"], ddexp, dalg = _ssd_bwd(
        dyn, y, xc, pp, dtg, dtt, arow, acol, dexp, ng, prev, dpp, bl, s, name="d_ssd")
    g["ssm_d"] = ddexp.reshape(NH, HD).sum(axis=1).reshape(1, NH)
    g["ssm_a_log"] = _ungroup_lanes(dalg)
    conv_dw, conv_db = [], []
    for dpart, off, tag in ((dxs, 0, "x"), (dbm, DI, "b"), (dcm, DI + NG * NS, "c")):
        dpp, dw_, db_ = _conv_ssm_bwd(pp, ws8, conv_b, dpart, off, dpp, bl, s, name=f"d_conv_ssm_{tag}")
        conv_dw.append(dw_)
        conv_db.append(db_)
    g["ssm_conv_w"] = jnp.concatenate(conv_dw, axis=1)[:4]
    g["ssm_conv_b"] = jnp.concatenate(conv_db, axis=1)
    dpp, dbias = _dt_bwd(pp, bias128, ddtg, dpp, name="d_dt")
    g["ssm_dt_bias"] = dbias[:, :NH]
    g["w_in"] = _mm(u, dpp, "tn", tn=1152, name="d_w_in")
    emit("mix", ("w_in", "w_out_a", "w_out_ssm", "w_mix_out"))
    du = _mm(dpp, wfull["w_in"], "nt", tk=3456, name="d_mix_norm_out")
    dh1, g["mix_norm"] = _norm_bwd(h1, gain("mix_norm"), du, dh2, name="d_mix_norm")
    dx = ffn_bwd(dh1, x2, "ffn1_norm", "ffn1_w_gate_up", "ffn1_w_down", ffn1_saved, "ffn1")
    return sq_err, dx, g


W_IN_SHARD = NIN // 4


def _w_in_segments():
    segs, p = [], 0
    for t in range(D // CA_TILE):
        for which in range(3):
            segs.append((D * which + CA_TILE * t, p, CA_TILE))
            p += CA_TILE
    for s, n in ((3 * D, O_GA - 3 * D), (O_GA + NH, 2 * D), (O_GA, NH)):
        segs.append((s, p, n))
        p += n
    assert p == NIN and segs[-1][1] == O_DT and segs[-2][1] == O_GA
    return segs


def _pad_w_in_shards(land):
    pieces = []
    for s, _, n in _w_in_segments():
        while n > 0:
            kk, off = divmod(s, W_IN_SHARD)
            take = min(n, W_IN_SHARD - off)
            pieces.append(land[kk][:, off:off + take])
            s, n = s + take, n - take
    return jnp.concatenate(pieces + [jnp.zeros((land.shape[1], NPP - NIN), land.dtype)], axis=1)


class _WInGrad:
    def __init__(self, dwp):
        self.dwp = dwp
        self.shape = (4, dwp.shape[0], W_IN_SHARD)

    def rows(self, start, n):
        part = lax.dynamic_slice_in_dim(self.dwp, start, n, axis=0)
        shards = []
        for kk in range(4):
            n0, n1 = W_IN_SHARD * kk, W_IN_SHARD * (kk + 1)
            cuts = sorted((max(s, n0), p + max(s, n0) - s, min(s + m, n1) - max(s, n0))
                          for s, p, m in _w_in_segments() if min(s + m, n1) > max(s, n0))
            shards.append(jnp.concatenate([part[:, p:p + m] for _, p, m in cuts], axis=1))
        return jnp.stack(shards)


def _pad_w_in(w):
    return _pad_w_in_shards(jnp.stack(jnp.split(w, 4, axis=1)))


def _unpad_w_in(w):
    return jnp.concatenate(list(_WInGrad(w).rows(0, w.shape[0])), axis=1)


def kernel(x, mem, ffn1_norm, ffn1_w_gate_up, ffn1_w_down, mix_norm, w_in, conv_a_w, w_out_a, ssm_conv_w, ssm_conv_b, ssm_dt_bias, ssm_a_log, ssm_d, ssm_norm, w_out_ssm, w_mix_out, xattn_norm, mem_norm, w_q, w_kv, w_o_x, ffn2_norm, ffn2_w_gate_up, ffn2_w_down, final_norm, loss_target, m_ffn1_norm, m_ffn1_w_gate_up, m_ffn1_w_down, m_mix_norm, m_w_in, m_conv_a_w, m_w_out_a, m_ssm_conv_w, m_ssm_conv_b, m_ssm_dt_bias, m_ssm_a_log, m_ssm_d, m_ssm_norm, m_w_out_ssm, m_w_mix_out, m_xattn_norm, m_mem_norm, m_w_q, m_w_kv, m_w_o_x, m_ffn2_norm, m_ffn2_w_gate_up, m_ffn2_w_down, m_final_norm, v_ffn1_norm, v_ffn1_w_gate_up, v_ffn1_w_down, v_mix_norm, v_w_in, v_conv_a_w, v_w_out_a, v_ssm_conv_w, v_ssm_conv_b, v_ssm_dt_bias, v_ssm_a_log, v_ssm_d, v_ssm_norm, v_w_out_ssm, v_w_mix_out, v_xattn_norm, v_mem_norm, v_w_q, v_w_kv, v_w_o_x, v_ffn2_norm, v_ffn2_w_gate_up, v_ffn2_w_down, v_final_norm):
    a = dict(locals())
    xi, yi = lax.axis_index("x"), lax.axis_index("y")
    k = 2 * xi + yi

    conv_vec, conv_offs = _pack([a["conv_a_w"], a["ssm_conv_w"]])
    conv_all = _gather_all(conv_vec, name="gather_conv_w")
    wfull = _GatheredWeights({n: a[n] for n, _ in BIG}, k, conv_all)
    conv_sh = [_unpack(conv_all[2 * kk], conv_offs, [a["conv_a_w"].shape[1:], a["ssm_conv_w"].shape[1:]])
               for kk in range(4)]
    small = {n: a[n] for n in SMALL}
    small["conv_a_w"] = jnp.concatenate([cs[0] for cs in conv_sh], axis=1)
    small["ssm_conv_w"] = jnp.concatenate([cs[1] for cs in conv_sh], axis=1)

    rs_started = []

    def on_grads(tag, grads):
        names = [n for n, _ in BIG if n in grads]
        shard_major = [_WInGrad(grads[n]) if n == "w_in" else _shard_major(grads[n], dict(BIG)[n]) for n in names]
        st, tk = _rs_start(shard_major, tag)
        rs_started.append((tag, names, st))
        return tk[0, 0]

    sq_err, dx, g = _local_step(wfull, small, x, mem, loss_target, wfull.token, on_grads)
    loss = lax.psum(0.5 / D * jnp.sum(sq_err), ("x", "y", "c"))

    ci = lax.axis_index("c")
    out = {}

    def finish(tag, names, st, after):
        g_mine, g_other = _rs_finish(st, after, tag)
        for n, gm, go in zip(names, g_mine, g_other):
            res = _adamw_halves(a[n], gm, go, a["m_" + n], a["v_" + n], ci, name=f"adamw_{n}")
            out[n] = tuple(t.reshape(a[n].shape) for t in res)
        return res[1]

    done = dx
    for grp in rs_started[:-1]:
        done = finish(*grp, dx)

    full_shapes = [g[n].shape for n in SMALL]
    gvec, goffs = _pack([g[n] for n in SMALL])
    gsum = _sum_leading(_gather_all(gvec, name="gather_small_grads", before=[done]), name="sum_small_grads")
    finish(*rs_started[-1], gsum)
    gsmall = dict(zip(SMALL, _unpack(gsum, goffs, full_shapes)))
    for n in ("conv_a_w", "ssm_conv_w"):
        width = a[n].shape[2]
        gsmall[n] = lax.dynamic_slice_in_dim(gsmall[n], k * width, width, axis=1)
    local_shapes = [a[n].shape for n in SMALL]
    packs = [_pack([t[n] for n in SMALL]) for t in
             ({n: a[n] for n in SMALL}, gsmall, {n: a["m_" + n] for n in SMALL}, {n: a["v_" + n] for n in SMALL})]
    offs = packs[0][1]
    res = _adamw(*[p[0] for p in packs], name="adamw_small")
    unp = [_unpack(r, offs, local_shapes) for r in res]
    for i, n in enumerate(SMALL):
        out[n] = (gsmall[n].reshape(a[n].shape), unp[0][i], unp[1][i], unp[2][i])

    grad_x = dx.reshape(x.shape)
    return (loss, grad_x, *[out[n][0] for n in WEIGHTS], *[out[n][1] for n in WEIGHTS],
            *[out[n][2] for n in WEIGHTS], *[out[n][3] for n in WEIGHTS])
```

```python
import functools
import math

import jax
import jax.numpy as jnp
from jax import lax
from jax.experimental import pallas as pl
from jax.experimental.pallas import tpu as pltpu

F32 = jnp.float32
BF16 = jnp.bfloat16
MXU = jnp.bfloat16
HI = lax.Precision.HIGHEST

D = 1024
DFF = 2816
DI = 2048
NH, HD, NG, NS, CH = 32, 64, 4, 128, 128
GW = DI // NG
XH, XD = 4, 256
EPS = 1e-6
NEG = -1e30
CA_TILE = 256
O_CA, O_Z, O_XBC, O_GA, O_GB, O_DT, NPP = 0, 3072, 5120, 8192, 9216, 10240, 10368
NIN = 10272
FFN_RES = 0.5
ADAM_LR, ADAM_B1, ADAM_B2, ADAM_EPS, ADAM_WD, ADAM_STEP = 0.001, 0.9, 0.999, 1e-08, 0.01, 10
VMEM_LIMIT = 56 * 1024 * 1024
EPI_COLS = 256
SSD_EX = 4
MESH = pl.DeviceIdType.MESH
CHIP_FLIPS = ((1, 0), (0, 1), (1, 1))


def _cp(*sem):
    return pltpu.CompilerParams(dimension_semantics=sem, vmem_limit_bytes=VMEM_LIMIT)


def _tile(n, pref, align=128):
    if n <= pref:
        return n
    t = (pref // align) * align
    while t >= align:
        if n % t == 0:
            return t
        t -= align
    raise ValueError((n, pref))


def _dot(a, b, dims, prec=None):
    return lax.dot_general(a, b, (dims, ((), ())), preferred_element_type=F32, precision=prec)


def _nn(a, b, prec=None):
    return _dot(a, b, ((1,), (0,)), prec)


def _nt(a, b):
    return _dot(a, b, ((1,), (1,)))


def _tn(a, b):
    return _dot(a, b, ((0,), (0,)))


def _sig(x):
    return jax.nn.sigmoid(x)


def _mm(a, b, mode, *, name, tm=1024, tn=1024, tk=None, out_dtype=F32, scale=None, residual=None, col_shards=0,
        norm_gain=None, norm_bwd=None):
    if tk is None:
        tk = 2048 if mode == "tn" else 1024
    if mode == "nn":
        (M, K), (K2, N) = a.shape, b.shape
    elif mode == "nt":
        (M, K), (N, K2) = a.shape, b.shape
    else:
        (K, M), (K2, N) = a.shape, b.shape
    assert K == K2, (name, a.shape, b.shape)
    tm, tn, tk = _tile(M, tm), _tile(N, tn), _tile(K, tk)
    nk = K // tk
    if mode == "nn":
        a_spec = pl.BlockSpec((tm, tk), lambda i, j, k: (i, k))
        b_spec = pl.BlockSpec((tk, tn), lambda i, j, k: (k, j))
        dims = ((1,), (0,))
    elif mode == "nt":
        a_spec = pl.BlockSpec((tm, tk), lambda i, j, k: (i, k))
        b_spec = pl.BlockSpec((tn, tk), lambda i, j, k: (j, k))
        dims = ((1,), (1,))
    else:
        a_spec = pl.BlockSpec((tk, tm), lambda i, j, k: (k, i))
        b_spec = pl.BlockSpec((tk, tn), lambda i, j, k: (k, j))
        dims = ((0,), (0,))
    o_spec = pl.BlockSpec((tm, tn), lambda i, j, k: (i, j))
    out_spec, out_shape = o_spec, jax.ShapeDtypeStruct((M, N), out_dtype)
    if col_shards:
        per = N // col_shards // tn
        assert per * tn * col_shards == N, (name, N, tn, col_shards)
        out_spec = pl.BlockSpec((None, tm, tn), lambda i, j, k: (j // per, i, j % per))
        out_shape = jax.ShapeDtypeStruct((col_shards, M, N // col_shards), out_dtype)
    has_res = residual is not None
    has_norm = norm_gain is not None
    has_nb = norm_bwd is not None
    assert not (has_norm or has_nb) or (tn == N and not col_shards)
    assert not (has_norm and has_nb)
    n_in = 2 + has_res + has_norm + 2 * has_nb

    def body(*refs):
        a_ref, b_ref = refs[0], refs[1]
        o_ref = refs[n_in]

        if has_nb:
            @pl.when((pl.program_id(0) == 0) & (pl.program_id(2) == 0))
            def _():
                refs[n_in + 1][...] = jnp.zeros_like(refs[n_in + 1])

        def finish(acc):
            if scale is not None:
                acc = acc * scale
            if has_nb:
                xv, gv = refs[n_in - 2][...], refs[n_in - 1][...]
                r = lax.rsqrt(jnp.mean(xv * xv, axis=-1, keepdims=True) + EPS)
                xh = xv * r
                refs[n_in + 1][...] += jnp.sum(acc * xh, axis=0, keepdims=True)
                dxh = acc * gv
                acc = r * (dxh - xh * jnp.mean(dxh * xh, axis=-1, keepdims=True))
            if has_res:
                acc = acc + refs[2][...]
            o_ref[...] = acc.astype(out_dtype)
            if has_norm:
                rs = lax.rsqrt(jnp.mean(acc * acc, axis=-1, keepdims=True) + EPS)
                refs[n_in + 1][...] = (acc * rs * refs[n_in - 1][...]).astype(BF16)

        part = _dot(a_ref[...].astype(MXU), b_ref[...].astype(MXU), dims)
        if nk == 1:
            finish(part)
            return
        acc_ref = refs[-1]
        k = pl.program_id(2)

        @pl.when(k == 0)
        def _():
            acc_ref[...] = part

        @pl.when(k > 0)
        def _():
            acc_ref[...] += part

        @pl.when(k == nk - 1)
        def _():
            finish(acc_ref[...])

    ins, in_specs = [a, b], [a_spec, b_spec]
    if has_res:
        ins.append(residual)
        in_specs.append(o_spec)
    if has_norm:
        ins.append(norm_gain)
        in_specs.append(pl.BlockSpec((1, tn), lambda i, j, k: (0, j)))
        out_spec, out_shape = [out_spec, o_spec], [out_shape, jax.ShapeDtypeStruct((M, N), BF16)]
    if has_nb:
        vec = pl.BlockSpec((1, tn), lambda i, j, k: (0, j))
        ins += [norm_bwd[0], norm_bwd[1]]
        in_specs += [o_spec, vec]
        out_spec, out_shape = [out_spec, vec], [out_shape, jax.ShapeDtypeStruct((1, N), F32)]
    return pl.pallas_call(
        body, grid=(M // tm, N // tn, nk), in_specs=in_specs, out_specs=out_spec, out_shape=out_shape,
        scratch_shapes=[pltpu.VMEM((tm, tn), F32)] if nk > 1 else [],
        compiler_params=_cp(*(("arbitrary",) * 3 if has_nb else ("parallel", "parallel", "arbitrary"))),
        name=name)(*ins)


def _norm_fwd(x, g, *, name):
    T, d = x.shape
    tr = _tile(T, 512, 8)

    def body(x_ref, g_ref, o_ref):
        xv = x_ref[...]
        r = lax.rsqrt(jnp.mean(xv * xv, axis=-1, keepdims=True) + EPS)
        o_ref[...] = (xv * r * g_ref[...]).astype(BF16)

    return pl.pallas_call(
        body, grid=(T // tr,),
        in_specs=[pl.BlockSpec((tr, d), lambda i: (i, 0)), pl.BlockSpec((1, d), lambda i: (0, 0))],
        out_specs=pl.BlockSpec((tr, d), lambda i: (i, 0)),
        out_shape=jax.ShapeDtypeStruct((T, d), BF16), compiler_params=_cp("parallel"), name=name)(x, g)


def _norm_bwd(x, g, dn, dres, *, name):
    T, d = x.shape
    tr = _tile(T, 512, 8)
    has_res = dres is not None

    def body(*refs):
        x_ref, g_ref, dn_ref = refs[:3]
        dr_ref = refs[3] if has_res else None
        dx_ref, dg_ref = refs[-2], refs[-1]

        @pl.when(pl.program_id(0) == 0)
        def _():
            dg_ref[...] = jnp.zeros_like(dg_ref)

        xv = x_ref[...]
        dnv = dn_ref[...].astype(F32)
        r = lax.rsqrt(jnp.mean(xv * xv, axis=-1, keepdims=True) + EPS)
        xh = xv * r
        dg_ref[...] += jnp.sum(dnv * xh, axis=0, keepdims=True)
        dxh = dnv * g_ref[...]
        dx = r * (dxh - xh * jnp.mean(dxh * xh, axis=-1, keepdims=True))
        if has_res:
            dx = dx + dr_ref[...]
        dx_ref[...] = dx

    row = pl.BlockSpec((tr, d), lambda i: (i, 0))
    vec = pl.BlockSpec((1, d), lambda i: (0, 0))
    ins = [x, g, dn] + ([dres] if has_res else [])
    return pl.pallas_call(
        body, grid=(T // tr,), in_specs=[row, vec, row] + ([row] if has_res else []),
        out_specs=[row, vec],
        out_shape=[jax.ShapeDtypeStruct((T, d), F32), jax.ShapeDtypeStruct((1, d), F32)],
        compiler_params=_cp("arbitrary"), name=name)(*ins)


def _final_loss(h, g, target, *, name):
    T, d = h.shape
    tr = _tile(T, 512, 8)

    def body(h_ref, g_ref, t_ref, l_ref, dh_ref, dg_ref):
        @pl.when(pl.program_id(0) == 0)
        def _():
            l_ref[...] = jnp.zeros_like(l_ref)
            dg_ref[...] = jnp.zeros_like(dg_ref)

        xv = h_ref[...]
        r = lax.rsqrt(jnp.mean(xv * xv, axis=-1, keepdims=True) + EPS)
        xh = xv * r
        e = xh * g_ref[...] - t_ref[...]
        l_ref[...] += jnp.sum(e * e, axis=0, keepdims=True)
        dy = e * (1.0 / d)
        dg_ref[...] += jnp.sum(dy * xh, axis=0, keepdims=True)
        dxh = dy * g_ref[...]
        dh_ref[...] = r * (dxh - xh * jnp.mean(dxh * xh, axis=-1, keepdims=True))

    row = pl.BlockSpec((tr, d), lambda i: (i, 0))
    vec = pl.BlockSpec((1, d), lambda i: (0, 0))
    return pl.pallas_call(
        body, grid=(T // tr,), in_specs=[row, vec, row], out_specs=[vec, row, vec],
        out_shape=[jax.ShapeDtypeStruct((1, d), F32), jax.ShapeDtypeStruct((T, d), F32),
                   jax.ShapeDtypeStruct((1, d), F32)],
        compiler_params=_cp("arbitrary"), name=name)(h, g, target)


def _shard_chunks(width, size=256):
    starts = list(range(0, width, size))
    if width - starts[-1] < EPI_COLS and len(starts) > 1:
        starts.pop()
    return [(o, (starts[i + 1] if i + 1 < len(starts) else width) - o) for i, o in enumerate(starts)]


def _gate_up_fwd(n, wsh, *, name):
    T, d = n.shape
    ws = wsh.shape[2]
    f = 2 * ws
    tm = _tile(T, 512, 8)

    def body(n_ref, wg_ref, wu_ref, g_ref, u_ref, a_ref):
        nv = n_ref[...].astype(MXU)
        for sh in range(2):
            for off, size in _shard_chunks(ws):
                gv = _nn(nv, wg_ref[sh, :, off:off + size].astype(MXU))
                uv = _nn(nv, wu_ref[sh, :, off:off + size].astype(MXU))
                sl = slice(sh * ws + off, sh * ws + off + size)
                g_ref[:, sl] = gv.astype(BF16)
                u_ref[:, sl] = uv.astype(BF16)
                a_ref[:, sl] = (gv * _sig(gv) * uv).astype(BF16)

    out = pl.BlockSpec((tm, f), lambda i: (i, 0))
    act = jax.ShapeDtypeStruct((T, f), BF16)
    return pl.pallas_call(
        body, grid=(T // tm,),
        in_specs=[pl.BlockSpec((tm, d), lambda i: (i, 0)), pl.BlockSpec((2, d, ws), lambda i: (0, 0, 0)),
                  pl.BlockSpec((2, d, ws), lambda i: (1, 0, 0))],
        out_specs=[out, out, out], out_shape=[act, act, act], compiler_params=_cp("parallel"),
        name=name)(n, wsh, wsh)


def _gate_up_bwd_input(dgate, dup, wsh, x, g, dres, *, name):
    T, f = dgate.shape
    four, d, ws = wsh.shape
    tm = _tile(T, 256, 8)

    def body(dg_ref, du_ref, w_ref, x_ref, g_ref, dr_ref, dx_ref, dgain_ref):
        @pl.when(pl.program_id(0) == 0)
        def _():
            dgain_ref[...] = jnp.zeros_like(dgain_ref)

        dn = None
        for sh in range(four):
            src = dg_ref if sh < 2 else du_ref
            part = _nt(src[:, (sh % 2) * ws:(sh % 2 + 1) * ws].astype(MXU), w_ref[sh].astype(MXU))
            dn = part if dn is None else dn + part
        xv = x_ref[...]
        r = lax.rsqrt(jnp.mean(xv * xv, axis=-1, keepdims=True) + EPS)
        xh = xv * r
        dgain_ref[...] += jnp.sum(dn * xh, axis=0, keepdims=True)
        dxh = dn * g_ref[...]
        dx_ref[...] = r * (dxh - xh * jnp.mean(dxh * xh, axis=-1, keepdims=True)) + dr_ref[...]

    act = pl.BlockSpec((tm, f), lambda i: (i, 0))
    row = pl.BlockSpec((tm, d), lambda i: (i, 0))
    vec = pl.BlockSpec((1, d), lambda i: (0, 0))
    return pl.pallas_call(
        body, grid=(T // tm,),
        in_specs=[act, act, pl.BlockSpec((four, d, ws), lambda i: (0, 0, 0)), row, vec, row],
        out_specs=[row, vec],
        out_shape=[jax.ShapeDtypeStruct((T, d), F32), jax.ShapeDtypeStruct((1, d), F32)],
        compiler_params=_cp("arbitrary"), name=name)(dgate, dup, wsh, x, g, dres)


def _act_bwd(dh, wd, gate, up, scale, *, name):
    T, d = dh.shape
    f = wd.shape[0]
    tm, tn = _tile(T, 512, 8), _tile(f, DFF)

    tc = _tile(tn, EPI_COLS)

    def body(dh_ref, wd_ref, g_ref, u_ref, dg_ref, du_ref):
        dhv = dh_ref[...].astype(MXU)
        for j in range(tn // tc):
            sl = slice(j * tc, (j + 1) * tc)
            da = scale * _nt(dhv, wd_ref[sl, :].astype(MXU))
            gv, uv = g_ref[:, sl].astype(F32), u_ref[:, sl].astype(F32)
            s = _sig(gv)
            dg_ref[:, sl] = (da * uv * (s * (1.0 + gv * (1.0 - s)))).astype(BF16)
            du_ref[:, sl] = (da * (gv * s)).astype(BF16)

    tile = pl.BlockSpec((tm, tn), lambda i, j: (i, j))
    act = jax.ShapeDtypeStruct((T, f), BF16)
    return pl.pallas_call(
        body, grid=(T // tm, f // tn),
        in_specs=[pl.BlockSpec((tm, d), lambda i, j: (i, 0)), pl.BlockSpec((tn, d), lambda i, j: (j, 0)), tile, tile],
        out_specs=[tile, tile], out_shape=[act, act], compiler_params=_cp("parallel", "parallel"),
        name=name)(dh, wd, gate, up)


CONV_ROWS = 64
CONV_PAD = 8


def _rows_down(ref, r0, d, cols=slice(None)):
    if r0 - d >= 0:
        return ref[pl.ds(r0 - d, CONV_ROWS), cols]
    assert r0 == 0
    v = ref[pl.ds(0, CONV_ROWS), cols]
    ri = lax.broadcasted_iota(jnp.int32, v.shape, 0)
    return jnp.where(ri >= d, pltpu.roll(v, d, 0), 0.0)


def _fold8(v):
    return jnp.sum(v.reshape(CONV_ROWS // 8, 8, v.shape[1]), axis=0)


def _taps(w_ref, views):
    acc = None
    for k, v in enumerate(views):
        t = w_ref[k:k + 1, :] * v
        acc = t if acc is None else acc + t
    return acc


def _conv_a_fwd(pp, w8, bl, s, *, name):
    tc = CA_TILE
    nb = D // tc
    bcol, ccol, vcol = slice(0, tc), slice(tc, 2 * tc), slice(2 * tc, 3 * tc)

    def body(p_ref, w_ref, o_ref):
        for r0 in range(0, s, CONV_ROWS):
            cv = [_rows_down(p_ref, r0, 2 - k, ccol) * _rows_down(p_ref, r0, 2 - k, vcol) for k in range(3)]
            o_ref[pl.ds(r0, CONV_ROWS), :] = (p_ref[pl.ds(r0, CONV_ROWS), bcol] * _taps(w_ref, cv)).astype(BF16)

    return pl.pallas_call(
        body, grid=(bl, nb),
        in_specs=[pl.BlockSpec((s, 3 * tc), lambda b, j: (b, j)), pl.BlockSpec((8, tc), lambda b, j: (0, j))],
        out_specs=pl.BlockSpec((s, tc), lambda b, j: (b, j)),
        out_shape=jax.ShapeDtypeStruct((bl * s, D), BF16), compiler_params=_cp("parallel", "parallel"),
        name=name)(pp, w8)


def _conv_a_bwd(pp, w8, dya, dpp, bl, s, *, name):
    tc = CA_TILE
    nb = D // tc
    bcol, ccol, vcol = slice(0, tc), slice(tc, 2 * tc), slice(2 * tc, 3 * tc)

    def body(p_ref, w_ref, dy_ref, dpp_in, d_ref, dw_ref, dcp):
        del dpp_in

        @pl.when(pl.program_id(1) == 0)
        def _():
            dw_ref[...] = jnp.zeros_like(dw_ref)

        dcp[pl.ds(s, CONV_PAD), :] = jnp.zeros((CONV_PAD, tc), F32)
        dw_acc = [jnp.zeros((8, tc), F32) for _ in range(3)]
        for r0 in reversed(range(0, s, CONV_ROWS)):
            rows = pl.ds(r0, CONV_ROWS)
            cs = [_rows_down(p_ref, r0, 2 - k, ccol) for k in range(3)]
            vs = [_rows_down(p_ref, r0, 2 - k, vcol) for k in range(3)]
            cv = [c_ * v_ for c_, v_ in zip(cs, vs)]
            dy = dy_ref[rows, :]
            d_ref[rows, bcol] = (dy * _taps(w_ref, cv)).astype(BF16)
            dconv = dy * p_ref[rows, bcol]
            dcp[rows, :] = dconv
            dcv = _taps(w_ref, [dcp[pl.ds(r0 + 2, CONV_ROWS), :], dcp[pl.ds(r0 + 1, CONV_ROWS), :], dconv])
            d_ref[rows, ccol] = (dcv * vs[2]).astype(BF16)
            d_ref[rows, vcol] = (dcv * cs[2]).astype(BF16)
            dw_acc = [acc + _fold8(dconv * cv_) for acc, cv_ in zip(dw_acc, cv)]
        for k in range(3):
            dw_ref[k:k + 1, :] += jnp.sum(dw_acc[k], axis=0, keepdims=True)

    wspec = pl.BlockSpec((8, tc), lambda j, b: (0, j))
    wide = pl.BlockSpec((s, 3 * tc), lambda j, b: (b, j))
    return pl.pallas_call(
        body, grid=(nb, bl),
        in_specs=[wide, wspec, pl.BlockSpec((s, tc), lambda j, b: (b, j)), pl.BlockSpec(memory_space=pl.ANY)],
        out_specs=[wide, wspec], out_shape=[jax.ShapeDtypeStruct(dpp.shape, dpp.dtype), jax.ShapeDtypeStruct((8, D), F32)],
        scratch_shapes=[pltpu.VMEM((s + CONV_PAD, tc), F32)], input_output_aliases={3: 0},
        compiler_params=_cp("parallel", "arbitrary"), name=name)(pp, w8, dya, dpp)


def _conv_ssm_fwd(pp, w8, bias, bl, s, *, name):
    tc = 256
    width = DI + 2 * NG * NS
    nb = width // tc

    def body(x_ref, w_ref, b_ref, o_ref):
        for r0 in range(0, s, CONV_ROWS):
            pre = _taps(w_ref, [_rows_down(x_ref, r0, 3 - k) for k in range(4)]) + b_ref[...]
            o_ref[pl.ds(r0, CONV_ROWS), :] = pre * _sig(pre)

    return pl.pallas_call(
        body, grid=(bl, nb),
        in_specs=[pl.BlockSpec((s, tc), lambda b, j: (b, O_XBC // tc + j)),
                  pl.BlockSpec((8, tc), lambda b, j: (0, j)), pl.BlockSpec((1, tc), lambda b, j: (0, j))],
        out_specs=pl.BlockSpec((s, tc), lambda b, j: (b, j)),
        out_shape=jax.ShapeDtypeStruct((bl * s, width), F32), compiler_params=_cp("parallel", "parallel"),
        name=name)(pp, w8, bias)


def _conv_ssm_bwd(pp, w8, bias, dxc, ch_off, dpp, bl, s, *, name):
    n = dxc.shape[1]
    tc = 256
    nb = n // tc
    o0 = ch_off // tc

    def body(x_ref, w_ref, b_ref, d_ref, dpp_in, dx_ref, dw_ref, db_ref, dp):
        del dpp_in

        @pl.when(pl.program_id(1) == 0)
        def _():
            dw_ref[...] = jnp.zeros_like(dw_ref)
            db_ref[...] = jnp.zeros_like(db_ref)

        dp[pl.ds(s, CONV_PAD), :] = jnp.zeros((CONV_PAD, tc), F32)
        dw_acc = [jnp.zeros((8, tc), F32) for _ in range(4)]
        db_acc = jnp.zeros((8, tc), F32)
        for r0 in reversed(range(0, s, CONV_ROWS)):
            rows = pl.ds(r0, CONV_ROWS)
            xs = [_rows_down(x_ref, r0, 3 - k) for k in range(4)]
            pre = _taps(w_ref, xs) + b_ref[...]
            sg = _sig(pre)
            dpre = d_ref[rows, :] * (sg * (1.0 + pre * (1.0 - sg)))
            dp[rows, :] = dpre
            dx = _taps(w_ref, [dp[pl.ds(r0 + 3 - k, CONV_ROWS), :] for k in range(3)] + [dpre])
            dx_ref[rows, :] = dx.astype(BF16)
            db_acc = db_acc + _fold8(dpre)
            dw_acc = [acc + _fold8(dpre * x_) for acc, x_ in zip(dw_acc, xs)]
        db_ref[...] += jnp.sum(db_acc, axis=0, keepdims=True)
        for k in range(4):
            dw_ref[k:k + 1, :] += jnp.sum(dw_acc[k], axis=0, keepdims=True)

    return pl.pallas_call(
        body, grid=(nb, bl),
        in_specs=[pl.BlockSpec((s, tc), lambda j, b: (b, O_XBC // tc + o0 + j)),
                  pl.BlockSpec((8, tc), lambda j, b: (0, o0 + j)), pl.BlockSpec((1, tc), lambda j, b: (0, o0 + j)),
                  pl.BlockSpec((s, tc), lambda j, b: (b, j)), pl.BlockSpec(memory_space=pl.ANY)],
        out_specs=[pl.BlockSpec((s, tc), lambda j, b: (b, O_XBC // tc + o0 + j)),
                   pl.BlockSpec((8, tc), lambda j, b: (0, j)), pl.BlockSpec((1, tc), lambda j, b: (0, j))],
        out_shape=[jax.ShapeDtypeStruct(dpp.shape, dpp.dtype), jax.ShapeDtypeStruct((8, n), F32),
                   jax.ShapeDtypeStruct((1, n), F32)],
        scratch_shapes=[pltpu.VMEM((s + CONV_PAD, tc), F32)], input_output_aliases={4: 0},
        compiler_params=_cp("parallel", "arbitrary"), name=name)(pp, w8, bias, dxc, dpp)


def _softplus(x):
    return jnp.maximum(x, 0.0) + jnp.log1p(jnp.exp(-jnp.abs(x)))


def _head_group_matrix():
    h = jnp.arange(128)[:, None]
    j = jnp.arange(NG * 128)[None, :]
    per = NH // NG
    return ((h < NH) & (j == (h // per) * 128 + h % per)).astype(F32)


def _dt_fwd(pp, bias128, *, name):
    T = pp.shape[0]
    tr = _tile(T, 1024, 8)
    per = NH // NG

    def body(x_ref, b_ref, p_ref, g_ref, t_ref):
        lane = lax.broadcasted_iota(jnp.int32, (tr, 128), 1)
        dt = jnp.where(lane < NH, _softplus(x_ref[...] + b_ref[...]), 0.0)
        g_ref[...] = _nn(dt, p_ref[...], HI)
        eye = (lax.broadcasted_iota(jnp.int32, (NH, 128), 0)
               == lax.broadcasted_iota(jnp.int32, (NH, 128), 1)).astype(F32)
        t_ref[...] = _dot(eye, dt, ((1,), (1,)), HI).reshape(NG, per, tr)

    vec = pl.BlockSpec((1, 128), lambda i: (0, 0))
    return pl.pallas_call(
        body, grid=(T // tr,),
        in_specs=[pl.BlockSpec((tr, 128), lambda i: (i, O_DT // 128)), vec,
                  pl.BlockSpec((128, NG * 128), lambda i: (0, 0))],
        out_specs=[pl.BlockSpec((tr, NG * 128), lambda i: (i, 0)), pl.BlockSpec((NG, per, tr), lambda i: (0, 0, i))],
        out_shape=[jax.ShapeDtypeStruct((T, NG * 128), F32), jax.ShapeDtypeStruct((NG, per, T), F32)],
        compiler_params=_cp("parallel"), name=name)(pp, bias128, _head_group_matrix())


def _dt_bwd(pp, bias128, ddtg, dpp, *, name):
    T = pp.shape[0]
    tr = _tile(T, 1024, 8)

    def body(x_ref, b_ref, d_ref, p_ref, dpp_in, o_ref, db_ref):
        del dpp_in

        @pl.when(pl.program_id(0) == 0)
        def _():
            db_ref[...] = jnp.zeros_like(db_ref)

        lane = lax.broadcasted_iota(jnp.int32, (tr, 128), 1)
        ddt = _dot(d_ref[...], p_ref[...], ((1,), (1,)), HI)
        dr = jnp.where(lane < NH, ddt * _sig(x_ref[...] + b_ref[...]), 0.0)
        db_ref[...] += jnp.sum(dr, axis=0, keepdims=True)
        o_ref[...] = dr.astype(BF16)

    col = pl.BlockSpec((tr, 128), lambda i: (i, O_DT // 128))
    vec = pl.BlockSpec((1, 128), lambda i: (0, 0))
    return pl.pallas_call(
        body, grid=(T // tr,),
        in_specs=[col, vec, pl.BlockSpec((tr, NG * 128), lambda i: (i, 0)), pl.BlockSpec((128, NG * 128), lambda i: (0, 0)),
                  pl.BlockSpec(memory_space=pl.ANY)],
        out_specs=[col, vec],
        out_shape=[jax.ShapeDtypeStruct(dpp.shape, dpp.dtype), jax.ShapeDtypeStruct((1, 128), F32)],
        input_output_aliases={4: 0}, compiler_params=_cp("arbitrary"),
        name=name)(pp, bias128, ddtg, _head_group_matrix(), dpp)


def _tril():
    return lax.broadcasted_iota(jnp.int32, (CH, CH), 0) >= lax.broadcasted_iota(jnp.int32, (CH, CH), 1)


def _ssd_common(dt, dtt, arow, acol):
    ri = lax.broadcasted_iota(jnp.int32, (CH, CH), 0)
    ci = lax.broadcasted_iota(jnp.int32, (CH, CH), 1)
    tril = ri >= ci
    triu = ri <= ci
    acs_col = _nn(tril.astype(F32), dt * arow, HI)
    acs_row = _nn(dtt * acol, triu.astype(F32), HI)
    return tril, triu, acs_col, acs_row


def _pair_terms(q, dt, acs_col, acs_row, tril, lo):
    ha, hb = 2 * q, 2 * q + 1
    col_a, col_b = acs_col[:, ha:ha + 1], acs_col[:, hb:hb + 1]
    row_a, row_b = acs_row[ha:ha + 1, :], acs_row[hb:hb + 1, :]
    last_a, last_b = acs_col[CH - 1:CH, ha:ha + 1], acs_col[CH - 1:CH, hb:hb + 1]
    out = dict(
        dtsel=jnp.where(lo, dt[:, ha:ha + 1], dt[:, hb:hb + 1]),
        d_a=jnp.exp(jnp.where(tril, col_a - row_a, NEG)), d_b=jnp.exp(jnp.where(tril, col_b - row_b, NEG)),
        esel=jnp.where(lo, jnp.exp(col_a), jnp.exp(col_b)),
        fsel=jnp.where(lo, jnp.exp(last_a - col_a), jnp.exp(last_b - col_b)),
        g_a=jnp.exp(last_a), g_b=jnp.exp(last_b))
    return out


def _ssd_fwd(xc, pp, dtg, dtt, arow, acol, dexp, ng, bl, s, *, name):
    nc = s // CH
    T = bl * s

    ex = SSD_EX if bl % SSD_EX == 0 else 1

    def body(*refs):
        arow_ref, acol_ref, dexp_ref, ng_ref = refs[6 * ex:6 * ex + 4]
        st_ref = refs[-1]

        @pl.when(pl.program_id(2) == 0)
        def _():
            st_ref[...] = jnp.zeros_like(st_ref)

        y_ref, yn_ref, prev_ref = refs[6 * ex + 4:6 * ex + 7]
        for e in range(ex):
            one(*refs[6 * e:6 * e + 6], arow_ref, acol_ref, dexp_ref, ng_ref,
                y_ref.at[e], yn_ref.at[e], prev_ref.at[e], st_ref.at[e])

    def one(xs_ref, bm_ref, cm_ref, z_ref, dt_ref, dtt_ref, arow_ref, acol_ref, dexp_ref, ng_ref,
            y_ref, yn_ref, prev_ref, st_ref):
        dt = dt_ref[...]
        tril, _, acs_col, acs_row = _ssd_common(dt, dtt_ref[...], -jnp.exp(arow_ref[...]), -jnp.exp(acol_ref[...]))
        bm, cm = bm_ref[...].astype(MXU), cm_ref[...].astype(MXU)
        cb = _nt(cm, bm)
        lo = lax.broadcasted_iota(jnp.int32, (CH, 128), 1) < HD
        sub_lo = lax.broadcasted_iota(jnp.int32, (128, NS), 0) < HD
        for q in range(4):
            t = _pair_terms(q, dt, acs_col, acs_row, tril, lo)
            x = xs_ref[:, 128 * q:128 * (q + 1)]
            xd = x * t["dtsel"]
            y = (_nn((cb * t["d_a"]).astype(MXU), jnp.where(lo, xd, 0.0).astype(MXU))
                 + _nn((cb * t["d_b"]).astype(MXU), jnp.where(lo, 0.0, xd).astype(MXU)))
            prev = st_ref[q]
            prev_ref[q] = prev
            y = y + t["esel"] * _nt(cm, prev.astype(MXU))
            st_ref[q] = prev * jnp.where(sub_lo, t["g_a"], t["g_b"]) + _tn((xd * t["fsel"]).astype(MXU), bm)
            y_ref[:, 128 * q:128 * (q + 1)] = y + dexp_ref[:, 128 * q:128 * (q + 1)] * x
        zv = z_ref[...]
        yg = y_ref[...] * (zv * _sig(zv))
        r = lax.rsqrt(jnp.mean(yg * yg, axis=-1, keepdims=True) + EPS)
        yn_ref[...] = (yg * r * ng_ref[...]).astype(BF16)

    def row(e, width, off_blocks):
        return pl.BlockSpec((CH, width), lambda g, b, c: ((b * ex + e) * nc + c, off_blocks + g))

    per_ex_in = [[row(e, GW, 0), row(e, NS, DI // NS), row(e, NS, DI // NS + NG), row(e, GW, O_Z // GW), row(e, 128, 0),
                  pl.BlockSpec((None, 8, CH), lambda g, b, c, e=e: (g, 0, (b * ex + e) * nc + c))] for e in range(ex)]
    by_example = pl.BlockSpec((ex, CH, GW), lambda g, b, c: (b, c, g))
    y, yn, prev = pl.pallas_call(
        body, grid=(NG, bl // ex, nc),
        in_specs=sum(per_ex_in, []) + [pl.BlockSpec((1, 128), lambda g, b, c: (0, g)),
                                       pl.BlockSpec((None, 8, 1), lambda g, b, c: (g, 0, 0)),
                                       pl.BlockSpec((1, GW), lambda g, b, c: (0, g)),
                                       pl.BlockSpec((1, GW), lambda g, b, c: (0, g))],
        out_specs=[by_example, by_example,
                   pl.BlockSpec((ex, None, 4, 128, NS), lambda g, b, c: (b, c, g, 0, 0))],
        out_shape=[jax.ShapeDtypeStruct((bl, s, DI), F32), jax.ShapeDtypeStruct((bl, s, DI), BF16),
                   jax.ShapeDtypeStruct((bl, nc, 16, 128, NS), F32)],
        scratch_shapes=[pltpu.VMEM((ex, 4, 128, NS), F32)],
        compiler_params=_cp("parallel", "parallel", "arbitrary"), name=name,
    )(*([xc, xc, xc, pp, dtg, dtt] * ex), arow, acol, dexp, ng)
    return y.reshape(T, DI), yn.reshape(T, DI), prev.reshape(bl * nc, 16, 128, NS)


def _ssd_bwd(dyn, y, xc, pp, dtg, dtt, arow, acol, dexp, ng, prev, dpp, bl, s, *, name):
    nc = s // CH
    T = bl * s

    def rsum(v):
        return jnp.sum(v, axis=1, keepdims=True)

    def asum(v):
        return jnp.sum(jnp.sum(v, axis=0, keepdims=True), axis=1, keepdims=True)

    ex = 1

    def body(*refs):
        shared = refs[9 * ex:9 * ex + 4]
        dz_ref, dxs_ref, db_ref, dc_ref, ddt_ref, dng_ref, dd_ref, dal_ref = refs[9 * ex + 5:9 * ex + 13]
        dst_ref = refs[-1]

        @pl.when((pl.program_id(1) == 0) & (pl.program_id(2) == 0))
        def _():
            dng_ref[...] = jnp.zeros_like(dng_ref)
            dd_ref[...] = jnp.zeros_like(dd_ref)
            dal_ref[...] = jnp.zeros_like(dal_ref)

        @pl.when(pl.program_id(2) == 0)
        def _():
            dst_ref[...] = jnp.zeros_like(dst_ref)

        for e in range(ex):
            one(*refs[9 * e:9 * e + 8], *shared, refs[9 * e + 8], dz_ref.at[e], dxs_ref.at[e], db_ref.at[e],
                dc_ref.at[e], ddt_ref.at[e], dng_ref, dd_ref, dal_ref, dst_ref.at[e])

    def one(dyn_ref, y_ref, xs_ref, bm_ref, cm_ref, z_ref, dt_ref, dtt_ref, arow_ref, acol_ref, dexp_ref, ng_ref,
            prev_ref, dz_ref, dxs_ref, db_ref, dc_ref, ddt_ref, dng_ref, dd_ref, dal_ref, dst_ref):
        yv, zv, xsv, dexp_v = y_ref[...], z_ref[...], xs_ref[...], dexp_ref[...]
        sz = _sig(zv)
        silu = zv * sz
        yg = yv * silu
        r = lax.rsqrt(jnp.mean(yg * yg, axis=-1, keepdims=True) + EPS)
        yh = yg * r
        dynv = dyn_ref[...]
        dng_ref[...] += jnp.sum(dynv * yh, axis=0, keepdims=True)
        dyh = dynv * ng_ref[...]
        dyg = r * (dyh - yh * jnp.mean(dyh * yh, axis=-1, keepdims=True))
        dz_ref[...] = (dyg * yv * (sz * (1.0 + zv * (1.0 - sz)))).astype(BF16)
        dy_all = dyg * silu
        dd_ref[...] += jnp.sum(dy_all * xsv, axis=0, keepdims=True)

        dt = dt_ref[...]
        arow_v = -jnp.exp(arow_ref[...])
        tril, triu, acs_col, acs_row = _ssd_common(dt, dtt_ref[...], arow_v, -jnp.exp(acol_ref[...]))
        bm, cm = bm_ref[...].astype(MXU), cm_ref[...].astype(MXU)
        cb = _nt(cm, bm)
        lane = lax.broadcasted_iota(jnp.int32, (CH, 128), 1)
        is_last = lax.broadcasted_iota(jnp.int32, (CH, 128), 0) == CH - 1
        lo = lane < HD
        sub_lo = lax.broadcasted_iota(jnp.int32, (128, NS), 0) < HD
        dcb = jnp.zeros((CH, CH), F32)
        dc_acc = jnp.zeros((CH, NS), F32)
        db_acc = jnp.zeros((CH, NS), F32)
        dacs = jnp.zeros((CH, 128), F32)
        ddtx = jnp.zeros((CH, 128), F32)
        csum = jnp.zeros((8, CH), F32)
        sub8 = lax.broadcasted_iota(jnp.int32, (8, CH), 0)
        for q in range(4):
            ha, hb = 2 * q, 2 * q + 1
            sl = slice(128 * q, 128 * (q + 1))
            t = _pair_terms(q, dt, acs_col, acs_row, tril, lo)
            x, dy = xsv[:, sl], dy_all[:, sl]
            xd = x * t["dtsel"]
            xd_m = xd.astype(MXU)
            dy_lo, dy_hi = jnp.where(lo, dy, 0.0).astype(MXU), jnp.where(lo, 0.0, dy).astype(MXU)
            m_a, m_b = cb * t["d_a"], cb * t["d_b"]
            prev_m = prev_ref[q].astype(MXU)
            dnext = dst_ref[q]
            dnext_m = dnext.astype(MXU)
            bds = _nt(bm, dnext_m)
            dxd = _tn(m_a.astype(MXU), dy_lo) + _tn(m_b.astype(MXU), dy_hi) + t["fsel"] * bds
            dye_m = (dy * t["esel"]).astype(MXU)
            dst_ref[q] = dnext * jnp.where(sub_lo, t["g_a"], t["g_b"]) + _tn(dye_m, cm)
            dm_a, dm_b = _nt(dy_lo, xd_m), _nt(dy_hi, xd_m)
            dcb = dcb + dm_a * t["d_a"] + dm_b * t["d_b"]
            g_a, g_b = dm_a * m_a, dm_b * m_b
            csum = (csum + jnp.where(sub8 == ha, jnp.sum(g_a, axis=0, keepdims=True), 0.0)
                    + jnp.where(sub8 == hb, jnp.sum(g_b, axis=0, keepdims=True), 0.0))
            tf = t["fsel"] * xd * bds
            tyf = dy * (t["esel"] * _nt(cm, prev_m)) - tf
            dpp = dnext * prev_ref[q]
            ea = asum(jnp.where(lo, tf, 0.0)) + t["g_a"] * asum(jnp.where(sub_lo, dpp, 0.0))
            eb = asum(jnp.where(lo, 0.0, tf)) + t["g_b"] * asum(jnp.where(sub_lo, 0.0, dpp))
            ra = rsum(g_a + jnp.where(lo, tyf, 0.0)) + jnp.where(is_last, ea, 0.0)
            rb = rsum(g_b + jnp.where(lo, 0.0, tyf)) + jnp.where(is_last, eb, 0.0)
            dacs = dacs + jnp.where(lane == ha, ra, 0.0) + jnp.where(lane == hb, rb, 0.0)
            tx = dxd * x
            ddtx = (ddtx + jnp.where(lane == ha, rsum(jnp.where(lo, tx, 0.0)), 0.0)
                    + jnp.where(lane == hb, rsum(jnp.where(lo, 0.0, tx)), 0.0))
            dxs_ref[:, sl] = dxd * t["dtsel"] + dexp_v[:, sl] * dy
            dc_acc = dc_acc + _nn(dye_m, prev_m)
            db_acc = db_acc + _nn((xd * t["fsel"]).astype(MXU), dnext_m)
        dcb_m = dcb.astype(MXU)
        dc_ref[...] = dc_acc + _nn(dcb_m, bm)
        db_ref[...] = db_acc + _tn(dcb_m, cm)
        dacs = dacs - jnp.concatenate([csum, jnp.zeros((CH - 8, CH), F32)], axis=0).T
        dla = _nn(triu.astype(F32), dacs, HI)
        ddt_ref[...] = dla * arow_v + ddtx
        dal_ref[...] += jnp.sum(dla * dt, axis=0, keepdims=True) * arow_v

    def row(e, width, off_blocks):
        return pl.BlockSpec((CH, width), lambda g, b, c: ((b * ex + e) * nc + nc - 1 - c, off_blocks + g))

    per_ex_in = [[row(e, GW, 0), row(e, GW, 0), row(e, GW, 0), row(e, NS, DI // NS), row(e, NS, DI // NS + NG),
                  row(e, GW, O_Z // GW), row(e, 128, 0),
                  pl.BlockSpec((None, 8, CH), lambda g, b, c, e=e: (g, 0, (b * ex + e) * nc + nc - 1 - c)),
                  pl.BlockSpec((None, 4, 128, NS), lambda g, b, c, e=e: ((b * ex + e) * nc + nc - 1 - c, g, 0, 0))]
                 for e in range(ex)]

    def by_example(width, off_blocks):
        return pl.BlockSpec((ex, CH, width), lambda g, b, c: (b, nc - 1 - c, off_blocks + g))

    gvec = pl.BlockSpec((1, GW), lambda g, b, c: (0, g))
    hvec = pl.BlockSpec((1, 128), lambda g, b, c: (0, g))
    out = pl.pallas_call(
        body, grid=(NG, bl // ex, nc),
        in_specs=sum(per_ex_in, []) + [hvec, pl.BlockSpec((None, 8, 1), lambda g, b, c: (g, 0, 0)), gvec, gvec,
                                       pl.BlockSpec(memory_space=pl.ANY)],
        out_specs=[by_example(GW, O_Z // GW), by_example(GW, 0), by_example(NS, 0), by_example(NS, 0),
                   by_example(128, 0), gvec, gvec, hvec],
        input_output_aliases={9 * ex + 4: 0},
        out_shape=[jax.ShapeDtypeStruct((bl, s, dpp.shape[1]), dpp.dtype), jax.ShapeDtypeStruct((bl, s, DI), F32),
                   jax.ShapeDtypeStruct((bl, s, NG * NS), F32), jax.ShapeDtypeStruct((bl, s, NG * NS), F32),
                   jax.ShapeDtypeStruct((bl, s, NG * 128), F32), jax.ShapeDtypeStruct((1, DI), F32),
                   jax.ShapeDtypeStruct((1, DI), F32), jax.ShapeDtypeStruct((1, NG * 128), F32)],
        scratch_shapes=[pltpu.VMEM((ex, 4, 128, NS), F32)],
        compiler_params=_cp("arbitrary", "arbitrary", "arbitrary"), name=name,
    )(*([dyn, y, xc, xc, xc, pp, dtg, dtt, prev] * ex), arow, acol, dexp, ng, dpp.reshape(bl, s, dpp.shape[1]))
    return (out[0].reshape(T, -1), out[1].reshape(T, DI), out[2].reshape(T, NG * NS), out[3].reshape(T, NG * NS),
            out[4].reshape(T, NG * 128), out[5], out[6], out[7])


def _merge_fwd(pp, ya, yb, *, name):
    T = ya.shape[0]
    tr = _tile(T, 512, 8)

    def body(ga_ref, gb_ref, ya_ref, yb_ref, o_ref):
        o_ref[...] = (_sig(ga_ref[...]) * ya_ref[...].astype(F32)
                      + _sig(gb_ref[...]) * yb_ref[...].astype(F32)).astype(BF16)

    row = pl.BlockSpec((tr, D), lambda i: (i, 0))
    return pl.pallas_call(
        body, grid=(T // tr,),
        in_specs=[pl.BlockSpec((tr, D), lambda i: (i, O_GA // D)), pl.BlockSpec((tr, D), lambda i: (i, O_GB // D)),
                  row, row],
        out_specs=row, out_shape=jax.ShapeDtypeStruct((T, D), BF16), compiler_params=_cp("parallel"),
        name=name)(pp, pp, ya, yb)


def _merge_bwd(pp, ya, yb, dm, *, name):
    T = ya.shape[0]
    tr = _tile(T, 512, 8)
    assert O_GB == O_GA + D and O_GA % (2 * D) == 0

    def body(g_ref, ya_ref, yb_ref, dm_ref, dya_ref, dyb_ref, dg_ref):
        sa, sb, dmv = _sig(g_ref[:, :D]), _sig(g_ref[:, D:]), dm_ref[...]
        dya_ref[...] = (dmv * sa).astype(BF16)
        dyb_ref[...] = (dmv * sb).astype(BF16)
        dg_ref[:, :D] = (dmv * ya_ref[...].astype(F32) * (sa * (1.0 - sa))).astype(BF16)
        dg_ref[:, D:] = (dmv * yb_ref[...].astype(F32) * (sb * (1.0 - sb))).astype(BF16)

    row = pl.BlockSpec((tr, D), lambda i: (i, 0))
    gates = pl.BlockSpec((tr, 2 * D), lambda i: (i, O_GA // (2 * D)))
    act = jax.ShapeDtypeStruct((T, D), BF16)
    return pl.pallas_call(
        body, grid=(T // tr,), in_specs=[gates, row, row, row], out_specs=[row, row, gates],
        out_shape=[act, act, jax.ShapeDtypeStruct((T, NPP), BF16)], compiler_params=_cp("parallel"),
        name=name)(pp, ya, yb, dm)


def _softmax_rows(sc):
    e = jnp.exp(sc - jnp.max(sc, axis=-1, keepdims=True))
    return e / jnp.sum(e, axis=-1, keepdims=True)


def _attn_fwd(q, kv, bl, s, *, name):
    m = kv.shape[0] // bl
    tq = _tile(s, 1024)
    nq = s // tq
    scale = 1.0 / math.sqrt(XD)

    def body(q_ref, k_ref, v_ref, o_ref):
        p = _softmax_rows(_nt(q_ref[...], k_ref[...]) * scale)
        o_ref[...] = _nn(p.astype(MXU), v_ref[...]).astype(BF16)

    qspec = pl.BlockSpec((tq, XD), lambda b, h, i: (b * nq + i, h))
    return pl.pallas_call(
        body, grid=(bl, XH, nq),
        in_specs=[qspec, pl.BlockSpec((m, XD), lambda b, h, i: (b, h)),
                  pl.BlockSpec((m, XD), lambda b, h, i: (b, XH + h))],
        out_specs=qspec, out_shape=jax.ShapeDtypeStruct((bl * s, D), BF16),
        compiler_params=_cp("parallel", "parallel", "parallel"), name=name)(q, kv, kv)


def _attn_bwd(q, kv, do, bl, s, *, name):
    m = kv.shape[0] // bl
    tq = _tile(s, 1024)
    nq = s // tq
    scale = 1.0 / math.sqrt(XD)

    def body(q_ref, k_ref, v_ref, do_ref, dq_ref, dk_ref, dv_ref):
        @pl.when(pl.program_id(2) == 0)
        def _():
            dk_ref[...] = jnp.zeros_like(dk_ref)
            dv_ref[...] = jnp.zeros_like(dv_ref)

        qv, kvv, vv, dov = q_ref[...], k_ref[...], v_ref[...], do_ref[...]
        p = _softmax_rows(_nt(qv, kvv) * scale)
        dp = _nt(dov, vv)
        ds = (p * (dp - jnp.sum(dp * p, axis=-1, keepdims=True)) * scale).astype(MXU)
        dq_ref[...] = _nn(ds, kvv).astype(BF16)
        dk_ref[...] += _tn(ds, qv)
        dv_ref[...] += _tn(p.astype(MXU), dov)

    qspec = pl.BlockSpec((tq, XD), lambda b, h, i: (b * nq + i, h))
    kspec = pl.BlockSpec((m, XD), lambda b, h, i: (b, h))
    return pl.pallas_call(
        body, grid=(bl, XH, nq),
        in_specs=[qspec, kspec, pl.BlockSpec((m, XD), lambda b, h, i: (b, XH + h)), qspec],
        out_specs=[qspec, kspec, kspec],
        out_shape=[jax.ShapeDtypeStruct((bl * s, D), BF16), jax.ShapeDtypeStruct((bl * m, D), F32),
                   jax.ShapeDtypeStruct((bl * m, D), F32)],
        compiler_params=_cp("parallel", "parallel", "arbitrary"), name=name)(q, kv, kv, do)


def _row_tile(r, c, max_elems=512 * 1024, align=16):
    best = None
    for t in range(align, r + 1, align):
        if r % t == 0 and t * c <= max_elems:
            best = t
    return best if best is not None else r


def _addn(a, others, *, name, also_bf16=False):
    r, c = a.shape
    tr = _row_tile(r, c)
    n = len(others)

    def body(*refs):
        acc = refs[0][...].astype(F32)
        for o_ref in refs[1:1 + n]:
            acc = acc + o_ref[...].astype(F32)
        refs[1 + n][...] = acc
        if also_bf16:
            refs[2 + n][...] = acc.astype(BF16)

    spec = pl.BlockSpec((tr, c), lambda i: (i, 0))
    shapes = [jax.ShapeDtypeStruct((r, c), F32)] + ([jax.ShapeDtypeStruct((r, c), BF16)] if also_bf16 else [])
    out = pl.pallas_call(
        body, grid=(r // tr,), in_specs=[spec] * (1 + n), out_specs=[spec] * len(shapes), out_shape=shapes,
        compiler_params=_cp("parallel"), name=name)(a, *others)
    return out if also_bf16 else out[0]


def _sum_leading(a, *, name):
    n, r, c = a.shape

    def body(a_ref, o_ref):
        acc = a_ref[0]
        for i in range(1, n):
            acc = acc + a_ref[i]
        o_ref[...] = acc

    return pl.pallas_call(body, out_shape=jax.ShapeDtypeStruct((r, c), F32), name=name)(a)


def _adamw_math(wv, gv, mv, vv):
    m2 = ADAM_B1 * mv + (1.0 - ADAM_B1) * gv
    v2 = ADAM_B2 * vv + (1.0 - ADAM_B2) * (gv * gv)
    m_hat = m2 / (1.0 - ADAM_B1 ** ADAM_STEP)
    v_hat = v2 / (1.0 - ADAM_B2 ** ADAM_STEP)
    return -ADAM_LR * (m_hat / (jnp.sqrt(v_hat) + ADAM_EPS) + ADAM_WD * wv), m2, v2


def _adamw(w, g, m, v, *, name):
    r, c = w.shape
    tr = _row_tile(r, c, align=8)

    def body(w_ref, g_ref, m_ref, v_ref, d_ref, mo_ref, vo_ref):
        d_ref[...], mo_ref[...], vo_ref[...] = _adamw_math(w_ref[...], g_ref[...], m_ref[...], v_ref[...])

    spec = pl.BlockSpec((tr, c), lambda i: (i, 0))
    shp = jax.ShapeDtypeStruct((r, c), F32)
    return pl.pallas_call(
        body, grid=(r // tr,), in_specs=[spec] * 4, out_specs=[spec] * 3, out_shape=[shp] * 3,
        compiler_params=_cp("parallel"), name=name)(w, g, m, v)


def _adamw_halves(w, g_mine, g_other, m, v, c, *, name):
    _, r, cols = w.shape
    h = r // 2
    tr = _row_tile(h, cols, align=8)
    nh = h // tr

    def body(c_ref, w_ref, gm_ref, go_ref, m_ref, v_ref, g_ref, d_ref, mo_ref, vo_ref):
        gv = jnp.where(pl.program_id(0) // nh == c_ref[0], gm_ref[...], go_ref[...])
        g_ref[...] = gv
        d_ref[...], mo_ref[...], vo_ref[...] = _adamw_math(w_ref[...], gv, m_ref[...], v_ref[...])

    full = pl.BlockSpec((None, tr, cols), lambda i, c_: (0, i, 0))
    half = pl.BlockSpec((tr, cols), lambda i, c_: (i % nh, 0))
    shp = jax.ShapeDtypeStruct((1, r, cols), F32)
    return pl.pallas_call(
        body,
        grid_spec=pltpu.PrefetchScalarGridSpec(num_scalar_prefetch=1, grid=(2 * nh,),
                                               in_specs=[full, half, half, full, full], out_specs=[full] * 4),
        out_shape=[shp] * 4, compiler_params=_cp("parallel"), name=name,
    )(jnp.reshape(c, (1,)).astype(jnp.int32), w, g_mine, g_other, m, v)


def _flip(i, d):
    return 1 - i if d else i


def _comm(name, ins, out_shapes, n_remote, n_local, plan, aliases=None):
    n_in, n_out = len(ins), len(out_shapes)

    def body(*refs):
        in_refs, out_refs = refs[:n_in], refs[n_in:n_in + n_out]
        send_sems, recv_sems = refs[n_in + n_out], refs[n_in + n_out + 1]
        x, y, c = lax.axis_index("x"), lax.axis_index("y"), lax.axis_index("c")
        remote, local = plan(in_refs, out_refs, x, y, c)
        assert len(remote) == n_remote and len(local) == n_local
        copies = []
        if n_local:
            loc_sems = refs[n_in + n_out + 2]
            copies += [pltpu.make_async_copy(s_, d_, loc_sems.at[i]) for i, (s_, d_) in enumerate(local)]
        copies += [pltpu.make_async_remote_copy(src_ref=s_, dst_ref=d_, send_sem=send_sems.at[i],
                                                recv_sem=recv_sems.at[i], device_id=dev, device_id_type=MESH)
                   for i, (s_, d_, dev) in enumerate(remote)]
        for cp in copies:
            cp.start()
        for cp in copies:
            cp.wait()

    hbm = pl.BlockSpec(memory_space=pl.ANY)
    scratch = [pltpu.SemaphoreType.DMA((n_remote,)), pltpu.SemaphoreType.DMA((n_remote,))]
    if n_local:
        scratch.append(pltpu.SemaphoreType.DMA((n_local,)))
    return pl.pallas_call(
        body, in_specs=[hbm] * n_in, out_specs=[hbm] * n_out, out_shape=out_shapes, scratch_shapes=scratch,
        input_output_aliases=aliases or {}, compiler_params=pltpu.CompilerParams(has_side_effects=True),
        name=name)(*ins)


HBM_SPEC = pl.BlockSpec(memory_space=pltpu.HBM)
SEM_SPEC = pl.BlockSpec(memory_space=pltpu.SEMAPHORE)
DATAFLOW = pltpu.SideEffectType.DATAFLOW_SIDE_EFFECTING


def _remote_copies(plan, srcs, lands, send_sems, recv_sems, n_copies):
    x, y, c = lax.axis_index("x"), lax.axis_index("y"), lax.axis_index("c")
    copies = plan(srcs, lands, x, y, c)
    assert len(copies) == n_copies
    return [pltpu.make_async_remote_copy(src_ref=s_, dst_ref=d_, send_sem=send_sems.at[i], recv_sem=recv_sems.at[i],
                                         device_id=dev, device_id_type=MESH) for i, (s_, d_, dev) in enumerate(copies)]


def _split_start(name, srcs, lands, n_copies, plan, after=None):
    ns, nb = len(srcs), len(srcs) + len(lands)
    n_after = 0 if after is None else 1
    n_in = nb + n_after

    def body(*refs):
        for cp in _remote_copies(plan, refs[:ns], refs[ns:nb], refs[n_in], refs[n_in + 1], n_copies):
            cp.start()
        refs[-1][...] = jnp.zeros_like(refs[-1])

    arrays = [pltpu.with_memory_space_constraint(a_, pltpu.HBM) for a_ in list(srcs) + list(lands)]
    out = pl.pallas_call(
        body, name=name,
        out_shape=(pltpu.SemaphoreType.DMA((n_copies,)), pltpu.SemaphoreType.DMA((n_copies,)),
                   *[pltpu.HBM(a_.shape, a_.dtype) for a_ in arrays], jax.ShapeDtypeStruct((8, 128), F32)),
        in_specs=[HBM_SPEC] * nb + [pl.BlockSpec(memory_space=pl.ANY)] * n_after,
        out_specs=(SEM_SPEC, SEM_SPEC, *[HBM_SPEC] * nb, pl.BlockSpec(memory_space=pltpu.VMEM)),
        input_output_aliases={i: 2 + i for i in range(nb)},
        compiler_params=pltpu.CompilerParams(has_side_effects=DATAFLOW))(*arrays, *([after] * n_after))
    return (out[0], out[1], list(out[2:2 + nb])), out[-1]


def _split_wait(name, handle, ns, n_copies, plan, after):
    send_sems, recv_sems, bufs = handle
    nb = len(bufs)

    def body(*refs):
        for cp in _remote_copies(plan, refs[:ns], refs[ns:nb], refs[nb], refs[nb + 1], n_copies):
            cp.wait_send()
            cp.wait_recv()

    out = pl.pallas_call(
        body, name=name, out_shape=[pltpu.HBM(b_.shape, b_.dtype) for b_ in bufs],
        in_specs=[HBM_SPEC] * nb + [SEM_SPEC, SEM_SPEC, pl.BlockSpec(memory_space=pl.ANY)],
        out_specs=[HBM_SPEC] * nb, input_output_aliases={i: i for i in range(nb)},
        compiler_params=pltpu.CompilerParams(has_side_effects=DATAFLOW))(*bufs, send_sems, recv_sems, after)
    return list(out[ns:])


def _gather_start(shards, tag, after=None):
    n = len(shards)
    lands = [lax.empty((4,) + s.shape, s.dtype) for s in shards]

    def plan(srcs, dsts, x, y, c):
        k = 2 * x + y
        copies = []
        for w_ref, o_ref in zip(srcs, dsts):
            h = w_ref.shape[0] // 2
            rows = pl.ds(c * h, h)
            copies += [(w_ref.at[rows], o_ref.at[k, rows], (_flip(x, dx), _flip(y, dy), c)) for dx, dy in CHIP_FLIPS]
        return copies

    handle, token = _split_start(f"gather_{tag}_start", shards, lands, 3 * n, plan, after)
    return (handle, plan, n), token


def _gather_wait(started, after, tag):
    handle, plan, n = started
    return _split_wait(f"gather_{tag}_wait", handle, n, 3 * n, plan, after)


def _gather_d2d(lands, before, tag):
    n = len(lands)

    def plan_d2d(in_refs, out_refs, x, y, c):
        remote = []
        for o_ref in out_refs:
            h = o_ref.shape[1] // 2
            for dx, dy in CHIP_FLIPS:
                half = o_ref.at[2 * _flip(x, dx) + _flip(y, dy), pl.ds(c * h, h)]
                remote.append((half, half, (x, y, 1 - c)))
        return remote, []

    return _comm(f"gather_{tag}_d2d", list(lands) + list(before),
                 [jax.ShapeDtypeStruct(l_.shape, l_.dtype) for l_ in lands], 3 * n, 0, plan_d2d,
                 aliases={i: i for i in range(n)})


def _pair_plan(in_refs, out_refs, x, y, c):
    return [(i_, o_, (x, y, 1 - c)) for i_, o_ in zip(in_refs, out_refs)], []


def _rs_start(grads, tag):
    n = len(grads)
    c = lax.axis_index("c")
    def rows(g, start, h):
        return g.rows(start, h) if isinstance(g, _WInGrad) else lax.dynamic_slice_in_dim(g, start, h, axis=1)

    halves = [g.shape[1] // 2 for g in grads]
    mine = [rows(g, c * h, h) for g, h in zip(grads, halves)]
    send_a = [rows(g, (1 - c) * h, h).astype(BF16) for g, h in zip(grads, halves)]
    recv_a = _comm(f"rs_pair_{tag}", send_a, [jax.ShapeDtypeStruct(s.shape, BF16) for s in send_a], n, 0, _pair_plan)
    pair, pair_b = [], []
    for i, (mi, ra) in enumerate(zip(mine, recv_a)):
        four, h, cols = mi.shape
        p32, p16 = _addn(mi.reshape(four * h, cols), [ra.reshape(four * h, cols)], name=f"rs_pair_sum_{tag}_{i}",
                         also_bf16=True)
        pair.append(p32.reshape(four, h, cols))
        pair_b.append(p16.reshape(four, h, cols))

    def plan(srcs, dsts, x, y, c_):
        copies = []
        for i_, o_ in zip(srcs, dsts):
            for j, (dx, dy) in enumerate(CHIP_FLIPS):
                fx, fy = _flip(x, dx), _flip(y, dy)
                copies.append((i_.at[2 * fx + fy], o_.at[j], (fx, fy, c_)))
        return copies

    lands = [lax.empty((3,) + p.shape[1:], BF16) for p in pair_b]
    handle, token = _split_start(f"rs_chips_{tag}_start", pair_b, lands, 3 * n, plan)
    return (handle, plan, n, pair), token


def _rs_finish(started, after, tag):
    handle, plan, n, pair = started
    recv_b = _split_wait(f"rs_chips_{tag}_wait", handle, n, 3 * n, plan, after)
    k = 2 * lax.axis_index("x") + lax.axis_index("y")
    tot = [_addn(lax.dynamic_index_in_dim(p, k, 0, keepdims=False), [rb[0], rb[1], rb[2]],
                 name=f"rs_chip_sum_{tag}_{i}") for i, (p, rb) in enumerate(zip(pair, recv_b))]
    other = _comm(f"rs_halves_{tag}", tot, [jax.ShapeDtypeStruct(t.shape, F32) for t in tot], n, 0, _pair_plan)
    return tot, other


def _gather_all(vec, *, name, before=()):
    out = jax.ShapeDtypeStruct((8,) + vec.shape, vec.dtype)

    def plan(in_refs, out_refs, x, y, c):
        me = 4 * x + 2 * y + c
        remote = [(in_refs[0], out_refs[0].at[me], (_flip(x, dx), _flip(y, dy), _flip(c, dc)))
                  for dx in (0, 1) for dy in (0, 1) for dc in (0, 1) if (dx, dy, dc) != (0, 0, 0)]
        return remote, [(in_refs[0], out_refs[0].at[me])]

    return _comm(name, [vec] + list(before), [out], 7, 1, plan)[0]


def _pack(parts):
    flat = [p.reshape(-1).astype(F32) for p in parts]
    total = sum(f.shape[0] for f in flat)
    n = -(-total // 1024) * 128
    vec = jnp.concatenate(flat + [jnp.zeros((8 * n - total,), F32)]).reshape(8, n)
    offs, o = [], 0
    for f in flat:
        offs.append((o, f.shape[0]))
        o += f.shape[0]
    return vec, offs


def _unpack(vec, offs, shapes):
    flat = vec.reshape(-1)
    return [flat[o:o + n].reshape(s) for (o, n), s in zip(offs, shapes)]


BIG = (("ffn1_w_gate_up", "col"), ("ffn1_w_down", "row"), ("w_in", "col"), ("w_out_a", "row"), ("w_out_ssm", "row"),
       ("w_mix_out", "row"), ("w_q", "row"), ("w_kv", "col"), ("w_o_x", "row"), ("ffn2_w_gate_up", "col"),
       ("ffn2_w_down", "row"))
SMALL = ("ffn1_norm", "mix_norm", "conv_a_w", "ssm_conv_w", "ssm_conv_b", "ssm_dt_bias", "ssm_a_log", "ssm_d",
         "ssm_norm", "xattn_norm", "mem_norm", "ffn2_norm", "final_norm")
WEIGHTS = ("ffn1_norm", "ffn1_w_gate_up", "ffn1_w_down", "mix_norm", "w_in", "conv_a_w", "w_out_a", "ssm_conv_w",
           "ssm_conv_b", "ssm_dt_bias", "ssm_a_log", "ssm_d", "ssm_norm", "w_out_ssm", "w_mix_out", "xattn_norm",
           "mem_norm", "w_q", "w_kv", "w_o_x", "ffn2_norm", "ffn2_w_gate_up", "ffn2_w_down", "final_norm")


GATHER_GROUPS = (("a", ("ffn1_w_gate_up",)), ("b", ("ffn1_w_down", "w_in")),
                 ("c", ("w_out_a", "w_out_ssm", "w_mix_out", "w_q", "w_kv", "w_o_x", "ffn2_w_gate_up", "ffn2_w_down")))


def _place_own(land, own, k, *, name):
    four, r, cols = land.shape
    tr = _row_tile(r, cols)

    def body(k_ref, own_ref, land_in, o_ref):
        del k_ref, land_in
        o_ref[...] = own_ref[...]

    return pl.pallas_call(
        body,
        grid_spec=pltpu.PrefetchScalarGridSpec(
            num_scalar_prefetch=1, grid=(r // tr,),
            in_specs=[pl.BlockSpec((tr, cols), lambda i, k_: (i, 0)), pl.BlockSpec(memory_space=pl.ANY)],
            out_specs=pl.BlockSpec((None, tr, cols), lambda i, k_: (k_[0], i, 0))),
        out_shape=jax.ShapeDtypeStruct(land.shape, land.dtype), input_output_aliases={2: 0},
        compiler_params=_cp("parallel"), name=name)(jnp.reshape(k, (1,)).astype(jnp.int32), own, land)


def _full_weight(land, own, kind, k, *, name):
    land = _place_own(land, own, k, name=name)
    four, r, cols = land.shape
    if kind == "row":
        return land.reshape(four * r, cols)
    return jnp.transpose(land, (1, 0, 2)).reshape(r, four * cols)


class _GatheredWeights:
    def __init__(self, shards32, k, after):
        first = GATHER_GROUPS[0][1]
        self.shards, self.k = {n: shards32[n].astype(BF16)[0] for n in first}, k
        self.full = {}
        self.n_done = 0
        self.started, token = self._start(0, after)
        self.token = token[0, 0]
        self.shards.update({n: (w + token[0, 0]).astype(BF16)[0] for n, w in shards32.items() if n not in first})
        self.after = jnp.stack([self.shards[n][0, 0] for n in shards32 if n not in first]).astype(F32).reshape(1, -1)

    def _start(self, gi, after):
        tag, names = GATHER_GROUPS[gi]
        return _gather_start([self.shards[n] for n in names], tag, after)

    def mark(self, value):
        self.after = value

    def __getitem__(self, name):
        if name not in self.full:
            tag, names = GATHER_GROUPS[self.n_done]
            assert name in names, (name, tag)
            lands = _gather_wait(self.started, self.after, tag)
            before = []
            if self.n_done + 1 < len(GATHER_GROUPS):
                self.started, token = self._start(self.n_done + 1, lands[0])
                before = [token]
            lands = _gather_d2d(lands, before, tag)
            for n, land in zip(names, lands):
                if n == "w_in":
                    self.full[n] = _pad_w_in_shards(_place_own(land, self.shards[n], self.k, name=f"own_{n}"))
                elif n.endswith("w_gate_up"):
                    self.full[n] = _place_own(land, self.shards[n], self.k, name=f"own_{n}")
                else:
                    self.full[n] = _full_weight(land, self.shards[n], dict(BIG)[n], self.k, name=f"own_{n}")
            self.n_done += 1
        return self.full[name]


def _shard_major(dw, kind):
    if isinstance(dw, tuple):
        return jnp.concatenate(dw, axis=0)
    if dw.ndim == 3:
        return dw
    if kind == "row":
        return dw.reshape(4, dw.shape[0] // 4, dw.shape[1])
    return jnp.transpose(dw.reshape(dw.shape[0], 4, dw.shape[1] // 4), (1, 0, 2))


def _pad_rows8(w):
    return jnp.concatenate([w, jnp.zeros((8 - w.shape[0], w.shape[1]), w.dtype)], axis=0)


def _group_lanes(v):
    r = v.shape[0]
    return jnp.pad(v.reshape(r, NG, NH // NG), ((0, 0), (0, 0), (0, 128 - NH // NG))).reshape(r, NG * 128)


def _ungroup_lanes(v):
    r = v.shape[0]
    return v.reshape(r, NG, 128)[:, :, :NH // NG].reshape(r, NH)


def _local_step(wfull, small, x, mem, target, token=0.0, on_grads=None):
    bl, s, _ = x.shape
    T = bl * s
    x2, t2 = x.reshape(T, D), target.reshape(T, D)
    mem2 = mem.reshape(-1, D)
    g = {}
    tok = [token]
    mark = getattr(wfull, "mark", lambda value: None)

    def gain(name):
        return small[name].reshape(1, -1) + tok[0]

    def emit(tag, names):
        if on_grads is not None:
            tok[0] = tok[0] + on_grads(tag, {n: g[n] for n in names})

    def ffn_fwd(h, n, wgu, wd, tag, next_gain=None):
        gate, up, a = _gate_up_fwd(n, wfull[wgu], name=f"{tag}_gate_up")
        mark(a)
        out = _mm(a, wfull[wd], "nn", tk=DFF, scale=FFN_RES, residual=h, norm_gain=next_gain, name=f"{tag}_down")
        return out, (n, gate, up, a)

    def ffn_bwd(dh, h, norm, wgu, wd, saved, tag):
        n, gate, up, a = saved
        dgate, dup = _act_bwd(dh, wfull[wd], gate, up, FFN_RES, name=f"{tag}_d_act")
        g[wd] = _mm(a, dh, "tn", tm=1408, scale=FFN_RES, name=f"{tag}_d_w_down")
        g[wgu] = (_mm(n, dgate, "tn", tn=1408, col_shards=2, name=f"{tag}_d_w_gate"),
                  _mm(n, dup, "tn", tn=1408, col_shards=2, name=f"{tag}_d_w_up"))
        emit(tag, (wgu, wd))
        dh_in, g[norm] = _gate_up_bwd_input(dgate, dup, wfull[wgu], h, gain(norm), dh, name=f"{tag}_d_norm")
        return dh_in

    n1 = _norm_fwd(x2, gain("ffn1_norm"), name="ffn1_norm")
    (h1, u), ffn1_saved = ffn_fwd(x2, n1, "ffn1_w_gate_up", "ffn1_w_down", "ffn1", gain("mix_norm"))
    mark(h1)
    pp = _mm(u, wfull["w_in"], "nn", tm=2048, tn=1152, name="in_proj")
    wa8 = _pad_rows8(small["conv_a_w"])
    ws8 = _pad_rows8(small["ssm_conv_w"])
    conv_b = gain("ssm_conv_b")
    bias128 = jnp.pad(gain("ssm_dt_bias"), ((0, 0), (0, 128 - NH)))
    ya_pre = _conv_a_fwd(pp, wa8, bl, s, name="conv_a")
    xc = _conv_ssm_fwd(pp, ws8, conv_b, bl, s, name="conv_ssm")
    mark(xc)
    alog = gain("ssm_a_log")
    dtg, dtt = _dt_fwd(pp, bias128, name="dt")
    arow, acol = _group_lanes(alog), alog.reshape(NG, NH // NG, 1)
    dexp = jnp.repeat(gain("ssm_d"), HD, axis=1)
    ng = gain("ssm_norm")
    y, yn, prev = _ssd_fwd(xc, pp, dtg, dtt, arow, acol, dexp, ng, bl, s, name="ssd")
    ya = _mm(ya_pre, wfull["w_out_a"], "nn", tn=1024, out_dtype=BF16, name="out_a")
    yb = _mm(yn, wfull["w_out_ssm"], "nn", tn=1024, tk=DI, out_dtype=BF16, name="out_ssm")
    merged = _merge_fwd(pp, ya, yb, name="merge")
    h2, un = _mm(merged, wfull["w_mix_out"], "nn", residual=h1, norm_gain=gain("xattn_norm"), name="mix_out")
    q = _mm(un, wfull["w_q"], "nn", tn=1024, out_dtype=BF16, name="q_proj")
    mn = _norm_fwd(mem2, gain("mem_norm"), name="mem_norm")
    kv = _mm(mn, wfull["w_kv"], "nn", tn=1024, out_dtype=BF16, name="kv_proj")
    o = _attn_fwd(q, kv, bl, s, name="attn")
    h3, n2 = _mm(o, wfull["w_o_x"], "nn", residual=h2, norm_gain=gain("ffn2_norm"), name="attn_out")
    h4, ffn2_saved = ffn_fwd(h3, n2, "ffn2_w_gate_up", "ffn2_w_down", "ffn2")
    sq_err, dh4, dgf = _final_loss(h4, gain("final_norm"), t2, name="final_loss")
    g["final_norm"] = dgf

    dh3 = ffn_bwd(dh4, h3, "ffn2_norm", "ffn2_w_gate_up", "ffn2_w_down", ffn2_saved, "ffn2")
    do = _mm(dh3, wfull["w_o_x"], "nt", tn=1024, out_dtype=BF16, name="d_attn_o")
    g["w_o_x"] = _mm(o, dh3, "tn",name="d_w_o_x")
    dq, dk, dv = _attn_bwd(q, kv, do, bl, s, name="d_attn")
    g["w_q"] = _mm(un, dq, "tn",name="d_w_q")
    dkv = jnp.concatenate([dk, dv], axis=1)
    dmn = _mm(dkv, wfull["w_kv"], "nt", tn=1024, tk=2 * D, name="d_mem_norm_out")
    g["w_kv"] = _mm(mn, dkv, "tn", tn=512, col_shards=4, name="d_w_kv")
    emit("attn", ("w_q", "w_kv", "w_o_x"))
    _, g["mem_norm"] = _norm_bwd(mem2, gain("mem_norm"), dmn, None, name="d_mem_norm")
    dh2, g["xattn_norm"] = _mm(dq, wfull["w_q"], "nt", tm=512, residual=dh3, norm_bwd=(h2, gain("xattn_norm")),
                               name="d_xattn_norm")
    dmerged = _mm(dh2, wfull["w_mix_out"], "nt", tn=1024, name="d_merged")
    g["w_mix_out"] = _mm(merged, dh2, "tn",name="d_w_mix_out")
    dya, dyb, dpp = _merge_bwd(pp, ya, yb, dmerged, name="d_merge")
    dya_pre = _mm(dya, wfull["w_out_a"], "nt", tn=1024, name="d_conv_a_out")
    g["w_out_a"] = _mm(ya_pre, dya, "tn",name="d_w_out_a")
    dyn = _mm(dyb, wfull["w_out_ssm"], "nt", tn=DI, name="d_ssd_out")
    g["w_out_ssm"] = _mm(yn, dyb, "tn",name="d_w_out_ssm")
    dpp, dwa8 = _conv_a_bwd(pp, wa8, dya_pre, dpp, bl, s, name="d_conv_a")
    g["conv_a_w"] = dwa8[:3]
    dpp, dxs, dbm, dcm, ddtg, g["ssm_norm"], ddexp, dalg = _ssd_bwd(
        dyn, y, xc, pp, dtg, dtt, arow, acol, dexp, ng, prev, dpp, bl, s, name="d_ssd")
    g["ssm_d"] = ddexp.reshape(NH, HD).sum(axis=1).reshape(1, NH)
    g["ssm_a_log"] = _ungroup_lanes(dalg)
    conv_dw, conv_db = [], []
    for dpart, off, tag in ((dxs, 0, "x"), (dbm, DI, "b"), (dcm, DI + NG * NS, "c")):
        dpp, dw_, db_ = _conv_ssm_bwd(pp, ws8, conv_b, dpart, off, dpp, bl, s, name=f"d_conv_ssm_{tag}")
        conv_dw.append(dw_)
        conv_db.append(db_)
    g["ssm_conv_w"] = jnp.concatenate(conv_dw, axis=1)[:4]
    g["ssm_conv_b"] = jnp.concatenate(conv_db, axis=1)
    dpp, dbias = _dt_bwd(pp, bias128, ddtg, dpp, name="d_dt")
    g["ssm_dt_bias"] = dbias[:, :NH]
    g["w_in"] = _mm(u, dpp, "tn", tn=1152, name="d_w_in")
    emit("mix", ("w_in", "w_out_a", "w_out_ssm", "w_mix_out"))
    dh1, g["mix_norm"] = _mm(dpp, wfull["w_in"], "nt", tm=512, tk=3456, residual=dh2,
                             norm_bwd=(h1, gain("mix_norm")), name="d_mix_norm")
    dx = ffn_bwd(dh1, x2, "ffn1_norm", "ffn1_w_gate_up", "ffn1_w_down", ffn1_saved, "ffn1")
    return sq_err, dx, g


W_IN_SHARD = NIN // 4


def _w_in_segments():
    segs, p = [], 0
    for t in range(D // CA_TILE):
        for which in range(3):
            segs.append((D * which + CA_TILE * t, p, CA_TILE))
            p += CA_TILE
    for s, n in ((3 * D, O_GA - 3 * D), (O_GA + NH, 2 * D), (O_GA, NH)):
        segs.append((s, p, n))
        p += n
    assert p == NIN and segs[-1][1] == O_DT and segs[-2][1] == O_GA
    return segs


def _pad_w_in_shards(land):
    pieces = []
    for s, _, n in _w_in_segments():
        while n > 0:
            kk, off = divmod(s, W_IN_SHARD)
            take = min(n, W_IN_SHARD - off)
            pieces.append(land[kk][:, off:off + take])
            s, n = s + take, n - take
    return jnp.concatenate(pieces + [jnp.zeros((land.shape[1], NPP - NIN), land.dtype)], axis=1)


class _WInGrad:
    def __init__(self, dwp):
        self.dwp = dwp
        self.shape = (4, dwp.shape[0], W_IN_SHARD)

    def rows(self, start, n):
        part = lax.dynamic_slice_in_dim(self.dwp, start, n, axis=0)
        shards = []
        for kk in range(4):
            n0, n1 = W_IN_SHARD * kk, W_IN_SHARD * (kk + 1)
            cuts = sorted((max(s, n0), p + max(s, n0) - s, min(s + m, n1) - max(s, n0))
                          for s, p, m in _w_in_segments() if min(s + m, n1) > max(s, n0))
            shards.append(jnp.concatenate([part[:, p:p + m] for _, p, m in cuts], axis=1))
        return jnp.stack(shards)


def _pad_w_in(w):
    return _pad_w_in_shards(jnp.stack(jnp.split(w, 4, axis=1)))


def _unpad_w_in(w):
    return jnp.concatenate(list(_WInGrad(w).rows(0, w.shape[0])), axis=1)


def kernel(x, mem, ffn1_norm, ffn1_w_gate_up, ffn1_w_down, mix_norm, w_in, conv_a_w, w_out_a, ssm_conv_w, ssm_conv_b, ssm_dt_bias, ssm_a_log, ssm_d, ssm_norm, w_out_ssm, w_mix_out, xattn_norm, mem_norm, w_q, w_kv, w_o_x, ffn2_norm, ffn2_w_gate_up, ffn2_w_down, final_norm, loss_target, m_ffn1_norm, m_ffn1_w_gate_up, m_ffn1_w_down, m_mix_norm, m_w_in, m_conv_a_w, m_w_out_a, m_ssm_conv_w, m_ssm_conv_b, m_ssm_dt_bias, m_ssm_a_log, m_ssm_d, m_ssm_norm, m_w_out_ssm, m_w_mix_out, m_xattn_norm, m_mem_norm, m_w_q, m_w_kv, m_w_o_x, m_ffn2_norm, m_ffn2_w_gate_up, m_ffn2_w_down, m_final_norm, v_ffn1_norm, v_ffn1_w_gate_up, v_ffn1_w_down, v_mix_norm, v_w_in, v_conv_a_w, v_w_out_a, v_ssm_conv_w, v_ssm_conv_b, v_ssm_dt_bias, v_ssm_a_log, v_ssm_d, v_ssm_norm, v_w_out_ssm, v_w_mix_out, v_xattn_norm, v_mem_norm, v_w_q, v_w_kv, v_w_o_x, v_ffn2_norm, v_ffn2_w_gate_up, v_ffn2_w_down, v_final_norm):
    a = dict(locals())
    xi, yi = lax.axis_index("x"), lax.axis_index("y")
    k = 2 * xi + yi

    conv_vec, conv_offs = _pack([a["conv_a_w"], a["ssm_conv_w"]])
    conv_all = _gather_all(conv_vec, name="gather_conv_w")
    wfull = _GatheredWeights({n: a[n] for n, _ in BIG}, k, conv_all)
    conv_sh = [_unpack(conv_all[2 * kk], conv_offs, [a["conv_a_w"].shape[1:], a["ssm_conv_w"].shape[1:]])
               for kk in range(4)]
    small = {n: a[n] for n in SMALL}
    small["conv_a_w"] = jnp.concatenate([cs[0] for cs in conv_sh], axis=1)
    small["ssm_conv_w"] = jnp.concatenate([cs[1] for cs in conv_sh], axis=1)

    rs_started = []

    def on_grads(tag, grads):
        names = [n for n, _ in BIG if n in grads]
        shard_major = [_WInGrad(grads[n]) if n == "w_in" else _shard_major(grads[n], dict(BIG)[n]) for n in names]
        st, tk = _rs_start(shard_major, tag)
        rs_started.append((tag, names, st))
        return tk[0, 0]

    sq_err, dx, g = _local_step(wfull, small, x, mem, loss_target, wfull.token, on_grads)
    loss = lax.psum(0.5 / D * jnp.sum(sq_err), ("x", "y", "c"))

    ci = lax.axis_index("c")
    out = {}

    def finish(tag, names, st, after):
        g_mine, g_other = _rs_finish(st, after, tag)
        for n, gm, go in zip(names, g_mine, g_other):
            res = _adamw_halves(a[n], gm, go, a["m_" + n], a["v_" + n], ci, name=f"adamw_{n}")
            out[n] = tuple(t.reshape(a[n].shape) for t in res)
        return res[1]

    done = dx
    for grp in rs_started[:-1]:
        done = finish(*grp, dx)

    full_shapes = [g[n].shape for n in SMALL]
    gvec, goffs = _pack([g[n] for n in SMALL])
    gsum = _sum_leading(_gather_all(gvec, name="gather_small_grads", before=[done]), name="sum_small_grads")
    finish(*rs_started[-1], gsum)
    gsmall = dict(zip(SMALL, _unpack(gsum, goffs, full_shapes)))
    for n in ("conv_a_w", "ssm_conv_w"):
        width = a[n].shape[2]
        gsmall[n] = lax.dynamic_slice_in_dim(gsmall[n], k * width, width, axis=1)
    local_shapes = [a[n].shape for n in SMALL]
    packs = [_pack([t[n] for n in SMALL]) for t in
             ({n: a[n] for n in SMALL}, gsmall, {n: a["m_" + n] for n in SMALL}, {n: a["v_" + n] for n in SMALL})]
    offs = packs[0][1]
    res = _adamw(*[p[0] for p in packs], name="adamw_small")
    unp = [_unpack(r, offs, local_shapes) for r in res]
    for i, n in enumerate(SMALL):
        out[n] = (gsmall[n].reshape(a[n].shape), unp[0][i], unp[1][i], unp[2][i])

    grad_x = dx.reshape(x.shape)
    return (loss, grad_x, *[out[n][0] for n in WEIGHTS], *[out[n][1] for n in WEIGHTS],
            *[out[n][2] for n in WEIGHTS], *[out[n][3] for n in WEIGHTS])
```

```python
import functools
import math

import jax
import jax.numpy as jnp
from jax import lax
from jax.experimental import pallas as pl
from jax.experimental.pallas import tpu as pltpu

F32 = jnp.float32
BF16 = jnp.bfloat16
MXU = jnp.bfloat16
HI = lax.Precision.HIGHEST

D = 1024
DFF = 2816
DI = 2048
NH, HD, NG, NS, CH = 32, 64, 4, 128, 128
GW = DI // NG
XH, XD = 4, 256
EPS = 1e-6
NEG = -1e30
CA_TILE = 256
O_CA, O_Z, O_XBC, O_GA, O_GB, O_DT, NPP = 0, 3072, 5120, 8192, 9216, 10240, 10368
NIN = 10272
FFN_RES = 0.5
ADAM_LR, ADAM_B1, ADAM_B2, ADAM_EPS, ADAM_WD, ADAM_STEP = 0.001, 0.9, 0.999, 1e-08, 0.01, 10
VMEM_LIMIT = 56 * 1024 * 1024
EPI_COLS = 256
SSD_EX = 4
MESH = pl.DeviceIdType.MESH
CHIP_FLIPS = ((1, 0), (0, 1), (1, 1))


def _cp(*sem):
    return pltpu.CompilerParams(dimension_semantics=sem, vmem_limit_bytes=VMEM_LIMIT)


def _tile(n, pref, align=128):
    if n <= pref:
        return n
    t = (pref // align) * align
    while t >= align:
        if n % t == 0:
            return t
        t -= align
    raise ValueError((n, pref))


def _dot(a, b, dims, prec=None):
    return lax.dot_general(a, b, (dims, ((), ())), preferred_element_type=F32, precision=prec)


def _nn(a, b, prec=None):
    return _dot(a, b, ((1,), (0,)), prec)


def _nt(a, b):
    return _dot(a, b, ((1,), (1,)))


def _tn(a, b):
    return _dot(a, b, ((0,), (0,)))


def _sig(x):
    return jax.nn.sigmoid(x)


def _mm(a, b, mode, *, name, tm=1024, tn=1024, tk=None, out_dtype=F32, scale=None, residual=None, col_shards=0,
        norm_gain=None, norm_bwd=None):
    if tk is None:
        tk = 2048 if mode == "tn" else 1024
    if mode == "nn":
        (M, K), (K2, N) = a.shape, b.shape
    elif mode == "nt":
        (M, K), (N, K2) = a.shape, b.shape
    else:
        (K, M), (K2, N) = a.shape, b.shape
    assert K == K2, (name, a.shape, b.shape)
    tm, tn, tk = _tile(M, tm), _tile(N, tn), _tile(K, tk)
    nk = K // tk
    if mode == "nn":
        a_spec = pl.BlockSpec((tm, tk), lambda i, j, k: (i, k))
        b_spec = pl.BlockSpec((tk, tn), lambda i, j, k: (k, j))
        dims = ((1,), (0,))
    elif mode == "nt":
        a_spec = pl.BlockSpec((tm, tk), lambda i, j, k: (i, k))
        b_spec = pl.BlockSpec((tn, tk), lambda i, j, k: (j, k))
        dims = ((1,), (1,))
    else:
        a_spec = pl.BlockSpec((tk, tm), lambda i, j, k: (k, i))
        b_spec = pl.BlockSpec((tk, tn), lambda i, j, k: (k, j))
        dims = ((0,), (0,))
    o_spec = pl.BlockSpec((tm, tn), lambda i, j, k: (i, j))
    out_spec, out_shape = o_spec, jax.ShapeDtypeStruct((M, N), out_dtype)
    if col_shards:
        per = N // col_shards // tn
        assert per * tn * col_shards == N, (name, N, tn, col_shards)
        out_spec = pl.BlockSpec((None, tm, tn), lambda i, j, k: (j // per, i, j % per))
        out_shape = jax.ShapeDtypeStruct((col_shards, M, N // col_shards), out_dtype)
    has_res = residual is not None
    has_norm = norm_gain is not None
    has_nb = norm_bwd is not None
    assert not (has_norm or has_nb) or (tn == N and not col_shards)
    assert not (has_norm and has_nb)
    n_in = 2 + has_res + has_norm + 2 * has_nb

    def body(*refs):
        a_ref, b_ref = refs[0], refs[1]
        o_ref = refs[n_in]

        if has_nb:
            @pl.when((pl.program_id(0) == 0) & (pl.program_id(2) == 0))
            def _():
                refs[n_in + 1][...] = jnp.zeros_like(refs[n_in + 1])

        def finish(acc):
            if scale is not None:
                acc = acc * scale
            if has_nb:
                xv, gv = refs[n_in - 2][...], refs[n_in - 1][...]
                r = lax.rsqrt(jnp.mean(xv * xv, axis=-1, keepdims=True) + EPS)
                xh = xv * r
                refs[n_in + 1][...] += jnp.sum(acc * xh, axis=0, keepdims=True)
                dxh = acc * gv
                acc = r * (dxh - xh * jnp.mean(dxh * xh, axis=-1, keepdims=True))
            if has_res:
                acc = acc + refs[2][...]
            o_ref[...] = acc.astype(out_dtype)
            if has_norm:
                rs = lax.rsqrt(jnp.mean(acc * acc, axis=-1, keepdims=True) + EPS)
                refs[n_in + 1][...] = (acc * rs * refs[n_in - 1][...]).astype(BF16)

        part = _dot(a_ref[...].astype(MXU), b_ref[...].astype(MXU), dims)
        if nk == 1:
            finish(part)
            return
        acc_ref = refs[-1]
        k = pl.program_id(2)

        @pl.when(k == 0)
        def _():
            acc_ref[...] = part

        @pl.when(k > 0)
        def _():
            acc_ref[...] += part

        @pl.when(k == nk - 1)
        def _():
            finish(acc_ref[...])

    ins, in_specs = [a, b], [a_spec, b_spec]
    if has_res:
        ins.append(residual)
        in_specs.append(o_spec)
    if has_norm:
        ins.append(norm_gain)
        in_specs.append(pl.BlockSpec((1, tn), lambda i, j, k: (0, j)))
        out_spec, out_shape = [out_spec, o_spec], [out_shape, jax.ShapeDtypeStruct((M, N), BF16)]
    if has_nb:
        vec = pl.BlockSpec((1, tn), lambda i, j, k: (0, j))
        ins += [norm_bwd[0], norm_bwd[1]]
        in_specs += [o_spec, vec]
        out_spec, out_shape = [out_spec, vec], [out_shape, jax.ShapeDtypeStruct((1, N), F32)]
    return pl.pallas_call(
        body, grid=(M // tm, N // tn, nk), in_specs=in_specs, out_specs=out_spec, out_shape=out_shape,
        scratch_shapes=[pltpu.VMEM((tm, tn), F32)] if nk > 1 else [],
        compiler_params=_cp(*(("arbitrary",) * 3 if has_nb else ("parallel", "parallel", "arbitrary"))),
        name=name)(*ins)


def _norm_fwd(x, g, *, name):
    T, d = x.shape
    tr = _tile(T, 512, 8)

    def body(x_ref, g_ref, o_ref):
        xv = x_ref[...]
        r = lax.rsqrt(jnp.mean(xv * xv, axis=-1, keepdims=True) + EPS)
        o_ref[...] = (xv * r * g_ref[...]).astype(BF16)

    return pl.pallas_call(
        body, grid=(T // tr,),
        in_specs=[pl.BlockSpec((tr, d), lambda i: (i, 0)), pl.BlockSpec((1, d), lambda i: (0, 0))],
        out_specs=pl.BlockSpec((tr, d), lambda i: (i, 0)),
        out_shape=jax.ShapeDtypeStruct((T, d), BF16), compiler_params=_cp("parallel"), name=name)(x, g)


def _norm_bwd(x, g, dn, dres, *, name):
    T, d = x.shape
    tr = _tile(T, 512, 8)
    has_res = dres is not None

    def body(*refs):
        x_ref, g_ref, dn_ref = refs[:3]
        dr_ref = refs[3] if has_res else None
        dx_ref, dg_ref = refs[-2], refs[-1]

        @pl.when(pl.program_id(0) == 0)
        def _():
            dg_ref[...] = jnp.zeros_like(dg_ref)

        xv = x_ref[...]
        dnv = dn_ref[...].astype(F32)
        r = lax.rsqrt(jnp.mean(xv * xv, axis=-1, keepdims=True) + EPS)
        xh = xv * r
        dg_ref[...] += jnp.sum(dnv * xh, axis=0, keepdims=True)
        dxh = dnv * g_ref[...]
        dx = r * (dxh - xh * jnp.mean(dxh * xh, axis=-1, keepdims=True))
        if has_res:
            dx = dx + dr_ref[...]
        dx_ref[...] = dx

    row = pl.BlockSpec((tr, d), lambda i: (i, 0))
    vec = pl.BlockSpec((1, d), lambda i: (0, 0))
    ins = [x, g, dn] + ([dres] if has_res else [])
    return pl.pallas_call(
        body, grid=(T // tr,), in_specs=[row, vec, row] + ([row] if has_res else []),
        out_specs=[row, vec],
        out_shape=[jax.ShapeDtypeStruct((T, d), F32), jax.ShapeDtypeStruct((1, d), F32)],
        compiler_params=_cp("arbitrary"), name=name)(*ins)


def _final_loss(h, g, target, *, name):
    T, d = h.shape
    tr = _tile(T, 512, 8)

    def body(h_ref, g_ref, t_ref, l_ref, dh_ref, dg_ref):
        @pl.when(pl.program_id(0) == 0)
        def _():
            l_ref[...] = jnp.zeros_like(l_ref)
            dg_ref[...] = jnp.zeros_like(dg_ref)

        xv = h_ref[...]
        r = lax.rsqrt(jnp.mean(xv * xv, axis=-1, keepdims=True) + EPS)
        xh = xv * r
        e = xh * g_ref[...] - t_ref[...]
        l_ref[...] += jnp.sum(e * e, axis=0, keepdims=True)
        dy = e * (1.0 / d)
        dg_ref[...] += jnp.sum(dy * xh, axis=0, keepdims=True)
        dxh = dy * g_ref[...]
        dh_ref[...] = r * (dxh - xh * jnp.mean(dxh * xh, axis=-1, keepdims=True))

    row = pl.BlockSpec((tr, d), lambda i: (i, 0))
    vec = pl.BlockSpec((1, d), lambda i: (0, 0))
    return pl.pallas_call(
        body, grid=(T // tr,), in_specs=[row, vec, row], out_specs=[vec, row, vec],
        out_shape=[jax.ShapeDtypeStruct((1, d), F32), jax.ShapeDtypeStruct((T, d), F32),
                   jax.ShapeDtypeStruct((1, d), F32)],
        compiler_params=_cp("arbitrary"), name=name)(h, g, target)


def _shard_chunks(width, size=256):
    starts = list(range(0, width, size))
    if width - starts[-1] < EPI_COLS and len(starts) > 1:
        starts.pop()
    return [(o, (starts[i + 1] if i + 1 < len(starts) else width) - o) for i, o in enumerate(starts)]


def _gate_up_fwd(n, wsh, *, name):
    T, d = n.shape
    ws = wsh.shape[2]
    f = 2 * ws
    tm = _tile(T, 512, 8)

    def body(n_ref, wg_ref, wu_ref, g_ref, u_ref, a_ref):
        nv = n_ref[...].astype(MXU)
        for sh in range(2):
            for off, size in _shard_chunks(ws):
                gv = _nn(nv, wg_ref[sh, :, off:off + size].astype(MXU))
                uv = _nn(nv, wu_ref[sh, :, off:off + size].astype(MXU))
                sl = slice(sh * ws + off, sh * ws + off + size)
                g_ref[:, sl] = gv.astype(BF16)
                u_ref[:, sl] = uv.astype(BF16)
                a_ref[:, sl] = (gv * _sig(gv) * uv).astype(BF16)

    out = pl.BlockSpec((tm, f), lambda i: (i, 0))
    act = jax.ShapeDtypeStruct((T, f), BF16)
    return pl.pallas_call(
        body, grid=(T // tm,),
        in_specs=[pl.BlockSpec((tm, d), lambda i: (i, 0)), pl.BlockSpec((2, d, ws), lambda i: (0, 0, 0)),
                  pl.BlockSpec((2, d, ws), lambda i: (1, 0, 0))],
        out_specs=[out, out, out], out_shape=[act, act, act], compiler_params=_cp("parallel"),
        name=name)(n, wsh, wsh)


def _gate_up_bwd_input(dgate, dup, wsh, x, g, dres, *, name):
    T, f = dgate.shape
    four, d, ws = wsh.shape
    tm = _tile(T, 256, 8)

    def body(dg_ref, du_ref, w_ref, x_ref, g_ref, dr_ref, dx_ref, dgain_ref):
        @pl.when(pl.program_id(0) == 0)
        def _():
            dgain_ref[...] = jnp.zeros_like(dgain_ref)

        dn = None
        for sh in range(four):
            src = dg_ref if sh < 2 else du_ref
            part = _nt(src[:, (sh % 2) * ws:(sh % 2 + 1) * ws].astype(MXU), w_ref[sh].astype(MXU))
            dn = part if dn is None else dn + part
        xv = x_ref[...]
        r = lax.rsqrt(jnp.mean(xv * xv, axis=-1, keepdims=True) + EPS)
        xh = xv * r
        dgain_ref[...] += jnp.sum(dn * xh, axis=0, keepdims=True)
        dxh = dn * g_ref[...]
        dx_ref[...] = r * (dxh - xh * jnp.mean(dxh * xh, axis=-1, keepdims=True)) + dr_ref[...]

    act = pl.BlockSpec((tm, f), lambda i: (i, 0))
    row = pl.BlockSpec((tm, d), lambda i: (i, 0))
    vec = pl.BlockSpec((1, d), lambda i: (0, 0))
    return pl.pallas_call(
        body, grid=(T // tm,),
        in_specs=[act, act, pl.BlockSpec((four, d, ws), lambda i: (0, 0, 0)), row, vec, row],
        out_specs=[row, vec],
        out_shape=[jax.ShapeDtypeStruct((T, d), F32), jax.ShapeDtypeStruct((1, d), F32)],
        compiler_params=_cp("arbitrary"), name=name)(dgate, dup, wsh, x, g, dres)


def _act_bwd(dh, wd, gate, up, scale, *, name):
    T, d = dh.shape
    f = wd.shape[0]
    tm, tn = _tile(T, 512, 8), _tile(f, DFF)

    tc = _tile(tn, EPI_COLS)

    def body(dh_ref, wd_ref, g_ref, u_ref, dg_ref, du_ref):
        dhv = dh_ref[...].astype(MXU)
        for j in range(tn // tc):
            sl = slice(j * tc, (j + 1) * tc)
            da = scale * _nt(dhv, wd_ref[sl, :].astype(MXU))
            gv, uv = g_ref[:, sl].astype(F32), u_ref[:, sl].astype(F32)
            s = _sig(gv)
            dg_ref[:, sl] = (da * uv * (s * (1.0 + gv * (1.0 - s)))).astype(BF16)
            du_ref[:, sl] = (da * (gv * s)).astype(BF16)

    tile = pl.BlockSpec((tm, tn), lambda i, j: (i, j))
    act = jax.ShapeDtypeStruct((T, f), BF16)
    return pl.pallas_call(
        body, grid=(T // tm, f // tn),
        in_specs=[pl.BlockSpec((tm, d), lambda i, j: (i, 0)), pl.BlockSpec((tn, d), lambda i, j: (j, 0)), tile, tile],
        out_specs=[tile, tile], out_shape=[act, act], compiler_params=_cp("parallel", "parallel"),
        name=name)(dh, wd, gate, up)


CONV_ROWS = 64
CONV_PAD = 8


def _rows_down(ref, r0, d, cols=slice(None)):
    if r0 - d >= 0:
        return ref[pl.ds(r0 - d, CONV_ROWS), cols]
    assert r0 == 0
    v = ref[pl.ds(0, CONV_ROWS), cols]
    ri = lax.broadcasted_iota(jnp.int32, v.shape, 0)
    return jnp.where(ri >= d, pltpu.roll(v, d, 0), 0.0)


def _fold8(v):
    return jnp.sum(v.reshape(CONV_ROWS // 8, 8, v.shape[1]), axis=0)


def _taps(w_ref, views):
    acc = None
    for k, v in enumerate(views):
        t = w_ref[k:k + 1, :] * v
        acc = t if acc is None else acc + t
    return acc


def _conv_a_fwd(pp, w8, bl, s, *, name):
    tc = CA_TILE
    nb = D // tc
    bcol, ccol, vcol = slice(0, tc), slice(tc, 2 * tc), slice(2 * tc, 3 * tc)

    def body(p_ref, w_ref, o_ref):
        for r0 in range(0, s, CONV_ROWS):
            cv = [_rows_down(p_ref, r0, 2 - k, ccol) * _rows_down(p_ref, r0, 2 - k, vcol) for k in range(3)]
            o_ref[pl.ds(r0, CONV_ROWS), :] = (p_ref[pl.ds(r0, CONV_ROWS), bcol] * _taps(w_ref, cv)).astype(BF16)

    return pl.pallas_call(
        body, grid=(bl, nb),
        in_specs=[pl.BlockSpec((s, 3 * tc), lambda b, j: (b, j)), pl.BlockSpec((8, tc), lambda b, j: (0, j))],
        out_specs=pl.BlockSpec((s, tc), lambda b, j: (b, j)),
        out_shape=jax.ShapeDtypeStruct((bl * s, D), BF16), compiler_params=_cp("parallel", "parallel"),
        name=name)(pp, w8)


def _conv_a_bwd(pp, w8, dya, dpp, bl, s, *, name):
    tc = CA_TILE
    nb = D // tc
    bcol, ccol, vcol = slice(0, tc), slice(tc, 2 * tc), slice(2 * tc, 3 * tc)

    def body(p_ref, w_ref, dy_ref, dpp_in, d_ref, dw_ref, dcp):
        del dpp_in

        @pl.when(pl.program_id(1) == 0)
        def _():
            dw_ref[...] = jnp.zeros_like(dw_ref)

        dcp[pl.ds(s, CONV_PAD), :] = jnp.zeros((CONV_PAD, tc), F32)
        dw_acc = [jnp.zeros((8, tc), F32) for _ in range(3)]
        for r0 in reversed(range(0, s, CONV_ROWS)):
            rows = pl.ds(r0, CONV_ROWS)
            cs = [_rows_down(p_ref, r0, 2 - k, ccol) for k in range(3)]
            vs = [_rows_down(p_ref, r0, 2 - k, vcol) for k in range(3)]
            cv = [c_ * v_ for c_, v_ in zip(cs, vs)]
            dy = dy_ref[rows, :]
            d_ref[rows, bcol] = (dy * _taps(w_ref, cv)).astype(BF16)
            dconv = dy * p_ref[rows, bcol]
            dcp[rows, :] = dconv
            dcv = _taps(w_ref, [dcp[pl.ds(r0 + 2, CONV_ROWS), :], dcp[pl.ds(r0 + 1, CONV_ROWS), :], dconv])
            d_ref[rows, ccol] = (dcv * vs[2]).astype(BF16)
            d_ref[rows, vcol] = (dcv * cs[2]).astype(BF16)
            dw_acc = [acc + _fold8(dconv * cv_) for acc, cv_ in zip(dw_acc, cv)]
        for k in range(3):
            dw_ref[k:k + 1, :] += jnp.sum(dw_acc[k], axis=0, keepdims=True)

    wspec = pl.BlockSpec((8, tc), lambda j, b: (0, j))
    wide = pl.BlockSpec((s, 3 * tc), lambda j, b: (b, j))
    return pl.pallas_call(
        body, grid=(nb, bl),
        in_specs=[wide, wspec, pl.BlockSpec((s, tc), lambda j, b: (b, j)), pl.BlockSpec(memory_space=pl.ANY)],
        out_specs=[wide, wspec], out_shape=[jax.ShapeDtypeStruct(dpp.shape, dpp.dtype), jax.ShapeDtypeStruct((8, D), F32)],
        scratch_shapes=[pltpu.VMEM((s + CONV_PAD, tc), F32)], input_output_aliases={3: 0},
        compiler_params=_cp("parallel", "arbitrary"), name=name)(pp, w8, dya, dpp)


def _conv_ssm_fwd(pp, w8, bias, bl, s, *, name):
    tc = 256
    width = DI + 2 * NG * NS
    nb = width // tc

    def body(x_ref, w_ref, b_ref, o_ref):
        for r0 in range(0, s, CONV_ROWS):
            pre = _taps(w_ref, [_rows_down(x_ref, r0, 3 - k) for k in range(4)]) + b_ref[...]
            o_ref[pl.ds(r0, CONV_ROWS), :] = pre * _sig(pre)

    return pl.pallas_call(
        body, grid=(bl, nb),
        in_specs=[pl.BlockSpec((s, tc), lambda b, j: (b, O_XBC // tc + j)),
                  pl.BlockSpec((8, tc), lambda b, j: (0, j)), pl.BlockSpec((1, tc), lambda b, j: (0, j))],
        out_specs=pl.BlockSpec((s, tc), lambda b, j: (b, j)),
        out_shape=jax.ShapeDtypeStruct((bl * s, width), F32), compiler_params=_cp("parallel", "parallel"),
        name=name)(pp, w8, bias)


def _conv_ssm_bwd(pp, w8, bias, dxc, ch_off, dpp, bl, s, *, name):
    n = dxc.shape[1]
    tc = 256
    nb = n // tc
    o0 = ch_off // tc

    def body(x_ref, w_ref, b_ref, d_ref, dpp_in, dx_ref, dw_ref, db_ref, dp):
        del dpp_in

        @pl.when(pl.program_id(1) == 0)
        def _():
            dw_ref[...] = jnp.zeros_like(dw_ref)
            db_ref[...] = jnp.zeros_like(db_ref)

        dp[pl.ds(s, CONV_PAD), :] = jnp.zeros((CONV_PAD, tc), F32)
        dw_acc = [jnp.zeros((8, tc), F32) for _ in range(4)]
        db_acc = jnp.zeros((8, tc), F32)
        for r0 in reversed(range(0, s, CONV_ROWS)):
            rows = pl.ds(r0, CONV_ROWS)
            xs = [_rows_down(x_ref, r0, 3 - k) for k in range(4)]
            pre = _taps(w_ref, xs) + b_ref[...]
            sg = _sig(pre)
            dpre = d_ref[rows, :] * (sg * (1.0 + pre * (1.0 - sg)))
            dp[rows, :] = dpre
            dx = _taps(w_ref, [dp[pl.ds(r0 + 3 - k, CONV_ROWS), :] for k in range(3)] + [dpre])
            dx_ref[rows, :] = dx.astype(BF16)
            db_acc = db_acc + _fold8(dpre)
            dw_acc = [acc + _fold8(dpre * x_) for acc, x_ in zip(dw_acc, xs)]
        db_ref[...] += jnp.sum(db_acc, axis=0, keepdims=True)
        for k in range(4):
            dw_ref[k:k + 1, :] += jnp.sum(dw_acc[k], axis=0, keepdims=True)

    return pl.pallas_call(
        body, grid=(nb, bl),
        in_specs=[pl.BlockSpec((s, tc), lambda j, b: (b, O_XBC // tc + o0 + j)),
                  pl.BlockSpec((8, tc), lambda j, b: (0, o0 + j)), pl.BlockSpec((1, tc), lambda j, b: (0, o0 + j)),
                  pl.BlockSpec((s, tc), lambda j, b: (b, j)), pl.BlockSpec(memory_space=pl.ANY)],
        out_specs=[pl.BlockSpec((s, tc), lambda j, b: (b, O_XBC // tc + o0 + j)),
                   pl.BlockSpec((8, tc), lambda j, b: (0, j)), pl.BlockSpec((1, tc), lambda j, b: (0, j))],
        out_shape=[jax.ShapeDtypeStruct(dpp.shape, dpp.dtype), jax.ShapeDtypeStruct((8, n), F32),
                   jax.ShapeDtypeStruct((1, n), F32)],
        scratch_shapes=[pltpu.VMEM((s + CONV_PAD, tc), F32)], input_output_aliases={4: 0},
        compiler_params=_cp("parallel", "arbitrary"), name=name)(pp, w8, bias, dxc, dpp)


def _softplus(x):
    return jnp.maximum(x, 0.0) + jnp.log1p(jnp.exp(-jnp.abs(x)))


def _head_group_matrix():
    h = jnp.arange(128)[:, None]
    j = jnp.arange(NG * 128)[None, :]
    per = NH // NG
    return ((h < NH) & (j == (h // per) * 128 + h % per)).astype(F32)


def _dt_fwd(pp, bias128, *, name):
    T = pp.shape[0]
    tr = _tile(T, 1024, 8)
    per = NH // NG

    def body(x_ref, b_ref, p_ref, g_ref, t_ref):
        lane = lax.broadcasted_iota(jnp.int32, (tr, 128), 1)
        dt = jnp.where(lane < NH, _softplus(x_ref[...] + b_ref[...]), 0.0)
        g_ref[...] = _nn(dt, p_ref[...], HI)
        eye = (lax.broadcasted_iota(jnp.int32, (NH, 128), 0)
               == lax.broadcasted_iota(jnp.int32, (NH, 128), 1)).astype(F32)
        t_ref[...] = _dot(eye, dt, ((1,), (1,)), HI).reshape(NG, per, tr)

    vec = pl.BlockSpec((1, 128), lambda i: (0, 0))
    return pl.pallas_call(
        body, grid=(T // tr,),
        in_specs=[pl.BlockSpec((tr, 128), lambda i: (i, O_DT // 128)), vec,
                  pl.BlockSpec((128, NG * 128), lambda i: (0, 0))],
        out_specs=[pl.BlockSpec((tr, NG * 128), lambda i: (i, 0)), pl.BlockSpec((NG, per, tr), lambda i: (0, 0, i))],
        out_shape=[jax.ShapeDtypeStruct((T, NG * 128), F32), jax.ShapeDtypeStruct((NG, per, T), F32)],
        compiler_params=_cp("parallel"), name=name)(pp, bias128, _head_group_matrix())


def _dt_bwd(pp, bias128, ddtg, dpp, *, name):
    T = pp.shape[0]
    tr = _tile(T, 1024, 8)

    def body(x_ref, b_ref, d_ref, p_ref, dpp_in, o_ref, db_ref):
        del dpp_in

        @pl.when(pl.program_id(0) == 0)
        def _():
            db_ref[...] = jnp.zeros_like(db_ref)

        lane = lax.broadcasted_iota(jnp.int32, (tr, 128), 1)
        ddt = _dot(d_ref[...], p_ref[...], ((1,), (1,)), HI)
        dr = jnp.where(lane < NH, ddt * _sig(x_ref[...] + b_ref[...]), 0.0)
        db_ref[...] += jnp.sum(dr, axis=0, keepdims=True)
        o_ref[...] = dr.astype(BF16)

    col = pl.BlockSpec((tr, 128), lambda i: (i, O_DT // 128))
    vec = pl.BlockSpec((1, 128), lambda i: (0, 0))
    return pl.pallas_call(
        body, grid=(T // tr,),
        in_specs=[col, vec, pl.BlockSpec((tr, NG * 128), lambda i: (i, 0)), pl.BlockSpec((128, NG * 128), lambda i: (0, 0)),
                  pl.BlockSpec(memory_space=pl.ANY)],
        out_specs=[col, vec],
        out_shape=[jax.ShapeDtypeStruct(dpp.shape, dpp.dtype), jax.ShapeDtypeStruct((1, 128), F32)],
        input_output_aliases={4: 0}, compiler_params=_cp("arbitrary"),
        name=name)(pp, bias128, ddtg, _head_group_matrix(), dpp)


def _tril():
    return lax.broadcasted_iota(jnp.int32, (CH, CH), 0) >= lax.broadcasted_iota(jnp.int32, (CH, CH), 1)


def _ssd_common(dt, dtt, arow, acol):
    ri = lax.broadcasted_iota(jnp.int32, (CH, CH), 0)
    ci = lax.broadcasted_iota(jnp.int32, (CH, CH), 1)
    tril = ri >= ci
    triu = ri <= ci
    acs_col = _nn(tril.astype(F32), dt * arow, HI)
    acs_row = _nn(dtt * acol, triu.astype(F32), HI)
    return tril, triu, acs_col, acs_row


def _pair_terms(q, dt, acs_col, acs_row, tril, lo):
    ha, hb = 2 * q, 2 * q + 1
    col_a, col_b = acs_col[:, ha:ha + 1], acs_col[:, hb:hb + 1]
    row_a, row_b = acs_row[ha:ha + 1, :], acs_row[hb:hb + 1, :]
    last_a, last_b = acs_col[CH - 1:CH, ha:ha + 1], acs_col[CH - 1:CH, hb:hb + 1]
    out = dict(
        dtsel=jnp.where(lo, dt[:, ha:ha + 1], dt[:, hb:hb + 1]),
        d_a=jnp.exp(jnp.where(tril, col_a - row_a, NEG)), d_b=jnp.exp(jnp.where(tril, col_b - row_b, NEG)),
        esel=jnp.where(lo, jnp.exp(col_a), jnp.exp(col_b)),
        fsel=jnp.where(lo, jnp.exp(last_a - col_a), jnp.exp(last_b - col_b)),
        g_a=jnp.exp(last_a), g_b=jnp.exp(last_b))
    return out


def _ssd_fwd(xc, pp, dtg, dtt, arow, acol, dexp, ng, bl, s, *, name):
    nc = s // CH
    T = bl * s

    ex = SSD_EX if bl % SSD_EX == 0 else 1

    def body(*refs):
        arow_ref, acol_ref, dexp_ref, ng_ref = refs[6 * ex:6 * ex + 4]
        st_ref = refs[-1]

        @pl.when(pl.program_id(2) == 0)
        def _():
            st_ref[...] = jnp.zeros_like(st_ref)

        y_ref, yn_ref, prev_ref = refs[6 * ex + 4:6 * ex + 7]
        for e in range(ex):
            one(*refs[6 * e:6 * e + 6], arow_ref, acol_ref, dexp_ref, ng_ref,
                y_ref.at[e], yn_ref.at[e], prev_ref.at[e], st_ref.at[e])

    def one(xs_ref, bm_ref, cm_ref, z_ref, dt_ref, dtt_ref, arow_ref, acol_ref, dexp_ref, ng_ref,
            y_ref, yn_ref, prev_ref, st_ref):
        dt = dt_ref[...]
        tril, _, acs_col, acs_row = _ssd_common(dt, dtt_ref[...], -jnp.exp(arow_ref[...]), -jnp.exp(acol_ref[...]))
        bm, cm = bm_ref[...].astype(MXU), cm_ref[...].astype(MXU)
        cb = _nt(cm, bm)
        lo = lax.broadcasted_iota(jnp.int32, (CH, 128), 1) < HD
        sub_lo = lax.broadcasted_iota(jnp.int32, (128, NS), 0) < HD
        for q in range(4):
            t = _pair_terms(q, dt, acs_col, acs_row, tril, lo)
            x = xs_ref[:, 128 * q:128 * (q + 1)]
            xd = x * t["dtsel"]
            y = (_nn((cb * t["d_a"]).astype(MXU), jnp.where(lo, xd, 0.0).astype(MXU))
                 + _nn((cb * t["d_b"]).astype(MXU), jnp.where(lo, 0.0, xd).astype(MXU)))
            prev = st_ref[q]
            prev_ref[q] = prev
            y = y + t["esel"] * _nt(cm, prev.astype(MXU))
            st_ref[q] = prev * jnp.where(sub_lo, t["g_a"], t["g_b"]) + _tn((xd * t["fsel"]).astype(MXU), bm)
            y_ref[:, 128 * q:128 * (q + 1)] = y + dexp_ref[:, 128 * q:128 * (q + 1)] * x
        zv = z_ref[...]
        yg = y_ref[...] * (zv * _sig(zv))
        r = lax.rsqrt(jnp.mean(yg * yg, axis=-1, keepdims=True) + EPS)
        yn_ref[...] = (yg * r * ng_ref[...]).astype(BF16)

    def row(e, width, off_blocks):
        return pl.BlockSpec((CH, width), lambda g, b, c: ((b * ex + e) * nc + c, off_blocks + g))

    per_ex_in = [[row(e, GW, 0), row(e, NS, DI // NS), row(e, NS, DI // NS + NG), row(e, GW, O_Z // GW), row(e, 128, 0),
                  pl.BlockSpec((None, 8, CH), lambda g, b, c, e=e: (g, 0, (b * ex + e) * nc + c))] for e in range(ex)]
    by_example = pl.BlockSpec((ex, CH, GW), lambda g, b, c: (b, c, g))
    y, yn, prev = pl.pallas_call(
        body, grid=(NG, bl // ex, nc),
        in_specs=sum(per_ex_in, []) + [pl.BlockSpec((1, 128), lambda g, b, c: (0, g)),
                                       pl.BlockSpec((None, 8, 1), lambda g, b, c: (g, 0, 0)),
                                       pl.BlockSpec((1, GW), lambda g, b, c: (0, g)),
                                       pl.BlockSpec((1, GW), lambda g, b, c: (0, g))],
        out_specs=[by_example, by_example,
                   pl.BlockSpec((ex, None, 4, 128, NS), lambda g, b, c: (b, c, g, 0, 0))],
        out_shape=[jax.ShapeDtypeStruct((bl, s, DI), F32), jax.ShapeDtypeStruct((bl, s, DI), BF16),
                   jax.ShapeDtypeStruct((bl, nc, 16, 128, NS), F32)],
        scratch_shapes=[pltpu.VMEM((ex, 4, 128, NS), F32)],
        compiler_params=_cp("parallel", "parallel", "arbitrary"), name=name,
    )(*([xc, xc, xc, pp, dtg, dtt] * ex), arow, acol, dexp, ng)
    return y.reshape(T, DI), yn.reshape(T, DI), prev.reshape(bl * nc, 16, 128, NS)


def _ssd_bwd(dyn, y, xc, pp, dtg, dtt, arow, acol, dexp, ng, prev, dpp, bl, s, *, name):
    nc = s // CH
    T = bl * s

    def rsum(v):
        return jnp.sum(v, axis=1, keepdims=True)

    def asum(v):
        return jnp.sum(jnp.sum(v, axis=0, keepdims=True), axis=1, keepdims=True)

    ex = 1

    def body(*refs):
        shared = refs[9 * ex:9 * ex + 4]
        dz_ref, dxs_ref, db_ref, dc_ref, ddt_ref, dng_ref, dd_ref, dal_ref = refs[9 * ex + 5:9 * ex + 13]
        dst_ref = refs[-1]

        @pl.when((pl.program_id(1) == 0) & (pl.program_id(2) == 0))
        def _():
            dng_ref[...] = jnp.zeros_like(dng_ref)
            dd_ref[...] = jnp.zeros_like(dd_ref)
            dal_ref[...] = jnp.zeros_like(dal_ref)

        @pl.when(pl.program_id(2) == 0)
        def _():
            dst_ref[...] = jnp.zeros_like(dst_ref)

        for e in range(ex):
            one(*refs[9 * e:9 * e + 8], *shared, refs[9 * e + 8], dz_ref.at[e], dxs_ref.at[e], db_ref.at[e],
                dc_ref.at[e], ddt_ref.at[e], dng_ref, dd_ref, dal_ref, dst_ref.at[e])

    def one(dyn_ref, y_ref, xs_ref, bm_ref, cm_ref, z_ref, dt_ref, dtt_ref, arow_ref, acol_ref, dexp_ref, ng_ref,
            prev_ref, dz_ref, dxs_ref, db_ref, dc_ref, ddt_ref, dng_ref, dd_ref, dal_ref, dst_ref):
        yv, zv, xsv, dexp_v = y_ref[...], z_ref[...], xs_ref[...], dexp_ref[...]
        sz = _sig(zv)
        silu = zv * sz
        yg = yv * silu
        r = lax.rsqrt(jnp.mean(yg * yg, axis=-1, keepdims=True) + EPS)
        yh = yg * r
        dynv = dyn_ref[...]
        dng_ref[...] += jnp.sum(dynv * yh, axis=0, keepdims=True)
        dyh = dynv * ng_ref[...]
        dyg = r * (dyh - yh * jnp.mean(dyh * yh, axis=-1, keepdims=True))
        dz_ref[...] = (dyg * yv * (sz * (1.0 + zv * (1.0 - sz)))).astype(BF16)
        dy_all = dyg * silu
        dd_ref[...] += jnp.sum(dy_all * xsv, axis=0, keepdims=True)

        dt = dt_ref[...]
        arow_v = -jnp.exp(arow_ref[...])
        tril, triu, acs_col, acs_row = _ssd_common(dt, dtt_ref[...], arow_v, -jnp.exp(acol_ref[...]))
        bm, cm = bm_ref[...].astype(MXU), cm_ref[...].astype(MXU)
        cb = _nt(cm, bm)
        lane = lax.broadcasted_iota(jnp.int32, (CH, 128), 1)
        is_last = lax.broadcasted_iota(jnp.int32, (CH, 128), 0) == CH - 1
        lo = lane < HD
        sub_lo = lax.broadcasted_iota(jnp.int32, (128, NS), 0) < HD
        dcb = jnp.zeros((CH, CH), F32)
        dc_acc = jnp.zeros((CH, NS), F32)
        db_acc = jnp.zeros((CH, NS), F32)
        dacs = jnp.zeros((CH, 128), F32)
        ddtx = jnp.zeros((CH, 128), F32)
        csum = jnp.zeros((8, CH), F32)
        sub8 = lax.broadcasted_iota(jnp.int32, (8, CH), 0)
        for q in range(4):
            ha, hb = 2 * q, 2 * q + 1
            sl = slice(128 * q, 128 * (q + 1))
            t = _pair_terms(q, dt, acs_col, acs_row, tril, lo)
            x, dy = xsv[:, sl], dy_all[:, sl]
            xd = x * t["dtsel"]
            xd_m = xd.astype(MXU)
            dy_lo, dy_hi = jnp.where(lo, dy, 0.0).astype(MXU), jnp.where(lo, 0.0, dy).astype(MXU)
            m_a, m_b = cb * t["d_a"], cb * t["d_b"]
            prev_m = prev_ref[q].astype(MXU)
            dnext = dst_ref[q]
            dnext_m = dnext.astype(MXU)
            bds = _nt(bm, dnext_m)
            dxd = _tn(m_a.astype(MXU), dy_lo) + _tn(m_b.astype(MXU), dy_hi) + t["fsel"] * bds
            dye_m = (dy * t["esel"]).astype(MXU)
            dst_ref[q] = dnext * jnp.where(sub_lo, t["g_a"], t["g_b"]) + _tn(dye_m, cm)
            dm_a, dm_b = _nt(dy_lo, xd_m), _nt(dy_hi, xd_m)
            dcb = dcb + dm_a * t["d_a"] + dm_b * t["d_b"]
            g_a, g_b = dm_a * m_a, dm_b * m_b
            csum = (csum + jnp.where(sub8 == ha, jnp.sum(g_a, axis=0, keepdims=True), 0.0)
                    + jnp.where(sub8 == hb, jnp.sum(g_b, axis=0, keepdims=True), 0.0))
            tf = t["fsel"] * xd * bds
            tyf = dy * (t["esel"] * _nt(cm, prev_m)) - tf
            dpp = dnext * prev_ref[q]
            ea = asum(jnp.where(lo, tf, 0.0)) + t["g_a"] * asum(jnp.where(sub_lo, dpp, 0.0))
            eb = asum(jnp.where(lo, 0.0, tf)) + t["g_b"] * asum(jnp.where(sub_lo, 0.0, dpp))
            ra = rsum(g_a + jnp.where(lo, tyf, 0.0)) + jnp.where(is_last, ea, 0.0)
            rb = rsum(g_b + jnp.where(lo, 0.0, tyf)) + jnp.where(is_last, eb, 0.0)
            dacs = dacs + jnp.where(lane == ha, ra, 0.0) + jnp.where(lane == hb, rb, 0.0)
            tx = dxd * x
            ddtx = (ddtx + jnp.where(lane == ha, rsum(jnp.where(lo, tx, 0.0)), 0.0)
                    + jnp.where(lane == hb, rsum(jnp.where(lo, 0.0, tx)), 0.0))
            dxs_ref[:, sl] = dxd * t["dtsel"] + dexp_v[:, sl] * dy
            dc_acc = dc_acc + _nn(dye_m, prev_m)
            db_acc = db_acc + _nn((xd * t["fsel"]).astype(MXU), dnext_m)
        dcb_m = dcb.astype(MXU)
        dc_ref[...] = dc_acc + _nn(dcb_m, bm)
        db_ref[...] = db_acc + _tn(dcb_m, cm)
        dacs = dacs - jnp.concatenate([csum, jnp.zeros((CH - 8, CH), F32)], axis=0).T
        dla = _nn(triu.astype(F32), dacs, HI)
        ddt_ref[...] = dla * arow_v + ddtx
        dal_ref[...] += jnp.sum(dla * dt, axis=0, keepdims=True) * arow_v

    def row(e, width, off_blocks):
        return pl.BlockSpec((CH, width), lambda g, b, c: ((b * ex + e) * nc + nc - 1 - c, off_blocks + g))

    per_ex_in = [[row(e, GW, 0), row(e, GW, 0), row(e, GW, 0), row(e, NS, DI // NS), row(e, NS, DI // NS + NG),
                  row(e, GW, O_Z // GW), row(e, 128, 0),
                  pl.BlockSpec((None, 8, CH), lambda g, b, c, e=e: (g, 0, (b * ex + e) * nc + nc - 1 - c)),
                  pl.BlockSpec((None, 4, 128, NS), lambda g, b, c, e=e: ((b * ex + e) * nc + nc - 1 - c, g, 0, 0))]
                 for e in range(ex)]

    def by_example(width, off_blocks):
        return pl.BlockSpec((ex, CH, width), lambda g, b, c: (b, nc - 1 - c, off_blocks + g))

    gvec = pl.BlockSpec((1, GW), lambda g, b, c: (0, g))
    hvec = pl.BlockSpec((1, 128), lambda g, b, c: (0, g))
    out = pl.pallas_call(
        body, grid=(NG, bl // ex, nc),
        in_specs=sum(per_ex_in, []) + [hvec, pl.BlockSpec((None, 8, 1), lambda g, b, c: (g, 0, 0)), gvec, gvec,
                                       pl.BlockSpec(memory_space=pl.ANY)],
        out_specs=[by_example(GW, O_Z // GW), by_example(GW, 0), by_example(NS, 0), by_example(NS, 0),
                   by_example(128, 0), gvec, gvec, hvec],
        input_output_aliases={9 * ex + 4: 0},
        out_shape=[jax.ShapeDtypeStruct((bl, s, dpp.shape[1]), dpp.dtype), jax.ShapeDtypeStruct((bl, s, DI), F32),
                   jax.ShapeDtypeStruct((bl, s, NG * NS), F32), jax.ShapeDtypeStruct((bl, s, NG * NS), F32),
                   jax.ShapeDtypeStruct((bl, s, NG * 128), F32), jax.ShapeDtypeStruct((1, DI), F32),
                   jax.ShapeDtypeStruct((1, DI), F32), jax.ShapeDtypeStruct((1, NG * 128), F32)],
        scratch_shapes=[pltpu.VMEM((ex, 4, 128, NS), F32)],
        compiler_params=_cp("arbitrary", "arbitrary", "arbitrary"), name=name,
    )(*([dyn, y, xc, xc, xc, pp, dtg, dtt, prev] * ex), arow, acol, dexp, ng, dpp.reshape(bl, s, dpp.shape[1]))
    return (out[0].reshape(T, -1), out[1].reshape(T, DI), out[2].reshape(T, NG * NS), out[3].reshape(T, NG * NS),
            out[4].reshape(T, NG * 128), out[5], out[6], out[7])


def _merge_fwd(pp, ya, yb, *, name):
    T = ya.shape[0]
    tr = _tile(T, 512, 8)

    def body(ga_ref, gb_ref, ya_ref, yb_ref, o_ref):
        o_ref[...] = (_sig(ga_ref[...]) * ya_ref[...].astype(F32)
                      + _sig(gb_ref[...]) * yb_ref[...].astype(F32)).astype(BF16)

    row = pl.BlockSpec((tr, D), lambda i: (i, 0))
    return pl.pallas_call(
        body, grid=(T // tr,),
        in_specs=[pl.BlockSpec((tr, D), lambda i: (i, O_GA // D)), pl.BlockSpec((tr, D), lambda i: (i, O_GB // D)),
                  row, row],
        out_specs=row, out_shape=jax.ShapeDtypeStruct((T, D), BF16), compiler_params=_cp("parallel"),
        name=name)(pp, pp, ya, yb)


def _merge_bwd(pp, ya, yb, dh, w_mix, *, name):
    T = ya.shape[0]
    tr = _tile(T, 512, 8)
    assert O_GB == O_GA + D and O_GA % (2 * D) == 0

    def body(g_ref, ya_ref, yb_ref, dh_ref, w_ref, dya_ref, dyb_ref, dg_ref):
        dmv = _nt(dh_ref[...].astype(MXU), w_ref[...].astype(MXU))
        sa, sb = _sig(g_ref[:, :D]), _sig(g_ref[:, D:])
        dya_ref[...] = (dmv * sa).astype(BF16)
        dyb_ref[...] = (dmv * sb).astype(BF16)
        dg_ref[:, :D] = (dmv * ya_ref[...].astype(F32) * (sa * (1.0 - sa))).astype(BF16)
        dg_ref[:, D:] = (dmv * yb_ref[...].astype(F32) * (sb * (1.0 - sb))).astype(BF16)

    row = pl.BlockSpec((tr, D), lambda i: (i, 0))
    gates = pl.BlockSpec((tr, 2 * D), lambda i: (i, O_GA // (2 * D)))
    act = jax.ShapeDtypeStruct((T, D), BF16)
    return pl.pallas_call(
        body, grid=(T // tr,),
        in_specs=[gates, row, row, row, pl.BlockSpec(w_mix.shape, lambda i: (0, 0))], out_specs=[row, row, gates],
        out_shape=[act, act, jax.ShapeDtypeStruct((T, NPP), BF16)], compiler_params=_cp("parallel"),
        name=name)(pp, ya, yb, dh, w_mix)


def _softmax_rows(sc):
    e = jnp.exp(sc - jnp.max(sc, axis=-1, keepdims=True))
    return e / jnp.sum(e, axis=-1, keepdims=True)


def _attn_fwd(q, kv, bl, s, *, name):
    m = kv.shape[0] // bl
    tq = _tile(s, 1024)
    nq = s // tq
    scale = 1.0 / math.sqrt(XD)

    def body(q_ref, k_ref, v_ref, o_ref):
        p = _softmax_rows(_nt(q_ref[...], k_ref[...]) * scale)
        o_ref[...] = _nn(p.astype(MXU), v_ref[...]).astype(BF16)

    qspec = pl.BlockSpec((tq, XD), lambda b, h, i: (b * nq + i, h))
    return pl.pallas_call(
        body, grid=(bl, XH, nq),
        in_specs=[qspec, pl.BlockSpec((m, XD), lambda b, h, i: (b, h)),
                  pl.BlockSpec((m, XD), lambda b, h, i: (b, XH + h))],
        out_specs=qspec, out_shape=jax.ShapeDtypeStruct((bl * s, D), BF16),
        compiler_params=_cp("parallel", "parallel", "parallel"), name=name)(q, kv, kv)


def _attn_bwd(q, kv, do, bl, s, *, name):
    m = kv.shape[0] // bl
    tq = _tile(s, 1024)
    nq = s // tq
    scale = 1.0 / math.sqrt(XD)

    def body(q_ref, k_ref, v_ref, do_ref, dq_ref, dk_ref, dv_ref):
        @pl.when(pl.program_id(2) == 0)
        def _():
            dk_ref[...] = jnp.zeros_like(dk_ref)
            dv_ref[...] = jnp.zeros_like(dv_ref)

        qv, kvv, vv, dov = q_ref[...], k_ref[...], v_ref[...], do_ref[...]
        p = _softmax_rows(_nt(qv, kvv) * scale)
        dp = _nt(dov, vv)
        ds = (p * (dp - jnp.sum(dp * p, axis=-1, keepdims=True)) * scale).astype(MXU)
        dq_ref[...] = _nn(ds, kvv).astype(BF16)
        dk_ref[...] += _tn(ds, qv)
        dv_ref[...] += _tn(p.astype(MXU), dov)

    qspec = pl.BlockSpec((tq, XD), lambda b, h, i: (b * nq + i, h))
    kspec = pl.BlockSpec((m, XD), lambda b, h, i: (b, h))
    return pl.pallas_call(
        body, grid=(bl, XH, nq),
        in_specs=[qspec, kspec, pl.BlockSpec((m, XD), lambda b, h, i: (b, XH + h)), qspec],
        out_specs=[qspec, kspec, kspec],
        out_shape=[jax.ShapeDtypeStruct((bl * s, D), BF16), jax.ShapeDtypeStruct((bl * m, D), F32),
                   jax.ShapeDtypeStruct((bl * m, D), F32)],
        compiler_params=_cp("parallel", "parallel", "arbitrary"), name=name)(q, kv, kv, do)


def _row_tile(r, c, max_elems=512 * 1024, align=16):
    best = None
    for t in range(align, r + 1, align):
        if r % t == 0 and t * c <= max_elems:
            best = t
    return best if best is not None else r


def _addn(a, others, *, name, also_bf16=False):
    r, c = a.shape
    tr = _row_tile(r, c)
    n = len(others)

    def body(*refs):
        acc = refs[0][...].astype(F32)
        for o_ref in refs[1:1 + n]:
            acc = acc + o_ref[...].astype(F32)
        refs[1 + n][...] = acc
        if also_bf16:
            refs[2 + n][...] = acc.astype(BF16)

    spec = pl.BlockSpec((tr, c), lambda i: (i, 0))
    shapes = [jax.ShapeDtypeStruct((r, c), F32)] + ([jax.ShapeDtypeStruct((r, c), BF16)] if also_bf16 else [])
    out = pl.pallas_call(
        body, grid=(r // tr,), in_specs=[spec] * (1 + n), out_specs=[spec] * len(shapes), out_shape=shapes,
        compiler_params=_cp("parallel"), name=name)(a, *others)
    return out if also_bf16 else out[0]


def _sum_leading(a, *, name):
    n, r, c = a.shape

    def body(a_ref, o_ref):
        acc = a_ref[0]
        for i in range(1, n):
            acc = acc + a_ref[i]
        o_ref[...] = acc

    return pl.pallas_call(body, out_shape=jax.ShapeDtypeStruct((r, c), F32), name=name)(a)


def _adamw_math(wv, gv, mv, vv):
    m2 = ADAM_B1 * mv + (1.0 - ADAM_B1) * gv
    v2 = ADAM_B2 * vv + (1.0 - ADAM_B2) * (gv * gv)
    m_hat = m2 / (1.0 - ADAM_B1 ** ADAM_STEP)
    v_hat = v2 / (1.0 - ADAM_B2 ** ADAM_STEP)
    return -ADAM_LR * (m_hat / (jnp.sqrt(v_hat) + ADAM_EPS) + ADAM_WD * wv), m2, v2


def _adamw(w, g, m, v, *, name):
    r, c = w.shape
    tr = _row_tile(r, c, align=8)

    def body(w_ref, g_ref, m_ref, v_ref, d_ref, mo_ref, vo_ref):
        d_ref[...], mo_ref[...], vo_ref[...] = _adamw_math(w_ref[...], g_ref[...], m_ref[...], v_ref[...])

    spec = pl.BlockSpec((tr, c), lambda i: (i, 0))
    shp = jax.ShapeDtypeStruct((r, c), F32)
    return pl.pallas_call(
        body, grid=(r // tr,), in_specs=[spec] * 4, out_specs=[spec] * 3, out_shape=[shp] * 3,
        compiler_params=_cp("parallel"), name=name)(w, g, m, v)


def _adamw_halves(w, g_mine, g_other, m, v, c, *, name):
    _, r, cols = w.shape
    h = r // 2
    tr = _row_tile(h, cols, align=8)
    nh = h // tr

    def body(c_ref, w_ref, gm_ref, go_ref, m_ref, v_ref, g_ref, d_ref, mo_ref, vo_ref):
        gv = jnp.where(pl.program_id(0) // nh == c_ref[0], gm_ref[...], go_ref[...])
        g_ref[...] = gv
        d_ref[...], mo_ref[...], vo_ref[...] = _adamw_math(w_ref[...], gv, m_ref[...], v_ref[...])

    full = pl.BlockSpec((None, tr, cols), lambda i, c_: (0, i, 0))
    half = pl.BlockSpec((tr, cols), lambda i, c_: (i % nh, 0))
    shp = jax.ShapeDtypeStruct((1, r, cols), F32)
    return pl.pallas_call(
        body,
        grid_spec=pltpu.PrefetchScalarGridSpec(num_scalar_prefetch=1, grid=(2 * nh,),
                                               in_specs=[full, half, half, full, full], out_specs=[full] * 4),
        out_shape=[shp] * 4, compiler_params=_cp("parallel"), name=name,
    )(jnp.reshape(c, (1,)).astype(jnp.int32), w, g_mine, g_other, m, v)


def _flip(i, d):
    return 1 - i if d else i


def _comm(name, ins, out_shapes, n_remote, n_local, plan, aliases=None):
    n_in, n_out = len(ins), len(out_shapes)

    def body(*refs):
        in_refs, out_refs = refs[:n_in], refs[n_in:n_in + n_out]
        send_sems, recv_sems = refs[n_in + n_out], refs[n_in + n_out + 1]
        x, y, c = lax.axis_index("x"), lax.axis_index("y"), lax.axis_index("c")
        remote, local = plan(in_refs, out_refs, x, y, c)
        assert len(remote) == n_remote and len(local) == n_local
        copies = []
        if n_local:
            loc_sems = refs[n_in + n_out + 2]
            copies += [pltpu.make_async_copy(s_, d_, loc_sems.at[i]) for i, (s_, d_) in enumerate(local)]
        copies += [pltpu.make_async_remote_copy(src_ref=s_, dst_ref=d_, send_sem=send_sems.at[i],
                                                recv_sem=recv_sems.at[i], device_id=dev, device_id_type=MESH)
                   for i, (s_, d_, dev) in enumerate(remote)]
        for cp in copies:
            cp.start()
        for cp in copies:
            cp.wait()

    hbm = pl.BlockSpec(memory_space=pl.ANY)
    scratch = [pltpu.SemaphoreType.DMA((n_remote,)), pltpu.SemaphoreType.DMA((n_remote,))]
    if n_local:
        scratch.append(pltpu.SemaphoreType.DMA((n_local,)))
    return pl.pallas_call(
        body, in_specs=[hbm] * n_in, out_specs=[hbm] * n_out, out_shape=out_shapes, scratch_shapes=scratch,
        input_output_aliases=aliases or {}, compiler_params=pltpu.CompilerParams(has_side_effects=True),
        name=name)(*ins)


HBM_SPEC = pl.BlockSpec(memory_space=pltpu.HBM)
SEM_SPEC = pl.BlockSpec(memory_space=pltpu.SEMAPHORE)
DATAFLOW = pltpu.SideEffectType.DATAFLOW_SIDE_EFFECTING


def _remote_copies(plan, srcs, lands, send_sems, recv_sems, n_copies):
    x, y, c = lax.axis_index("x"), lax.axis_index("y"), lax.axis_index("c")
    copies = plan(srcs, lands, x, y, c)
    assert len(copies) == n_copies
    return [pltpu.make_async_remote_copy(src_ref=s_, dst_ref=d_, send_sem=send_sems.at[i], recv_sem=recv_sems.at[i],
                                         device_id=dev, device_id_type=MESH) for i, (s_, d_, dev) in enumerate(copies)]


def _split_start(name, srcs, lands, n_copies, plan, after=None):
    ns, nb = len(srcs), len(srcs) + len(lands)
    n_after = 0 if after is None else 1
    n_in = nb + n_after

    def body(*refs):
        for cp in _remote_copies(plan, refs[:ns], refs[ns:nb], refs[n_in], refs[n_in + 1], n_copies):
            cp.start()
        refs[-1][...] = jnp.zeros_like(refs[-1])

    arrays = [pltpu.with_memory_space_constraint(a_, pltpu.HBM) for a_ in list(srcs) + list(lands)]
    out = pl.pallas_call(
        body, name=name,
        out_shape=(pltpu.SemaphoreType.DMA((n_copies,)), pltpu.SemaphoreType.DMA((n_copies,)),
                   *[pltpu.HBM(a_.shape, a_.dtype) for a_ in arrays], jax.ShapeDtypeStruct((8, 128), F32)),
        in_specs=[HBM_SPEC] * nb + [pl.BlockSpec(memory_space=pl.ANY)] * n_after,
        out_specs=(SEM_SPEC, SEM_SPEC, *[HBM_SPEC] * nb, pl.BlockSpec(memory_space=pltpu.VMEM)),
        input_output_aliases={i: 2 + i for i in range(nb)},
        compiler_params=pltpu.CompilerParams(has_side_effects=DATAFLOW))(*arrays, *([after] * n_after))
    return (out[0], out[1], list(out[2:2 + nb])), out[-1]


def _split_wait(name, handle, ns, n_copies, plan, after):
    send_sems, recv_sems, bufs = handle
    nb = len(bufs)

    def body(*refs):
        for cp in _remote_copies(plan, refs[:ns], refs[ns:nb], refs[nb], refs[nb + 1], n_copies):
            cp.wait_send()
            cp.wait_recv()

    out = pl.pallas_call(
        body, name=name, out_shape=[pltpu.HBM(b_.shape, b_.dtype) for b_ in bufs],
        in_specs=[HBM_SPEC] * nb + [SEM_SPEC, SEM_SPEC, pl.BlockSpec(memory_space=pl.ANY)],
        out_specs=[HBM_SPEC] * nb, input_output_aliases={i: i for i in range(nb)},
        compiler_params=pltpu.CompilerParams(has_side_effects=DATAFLOW))(*bufs, send_sems, recv_sems, after)
    return list(out[ns:])


def _gather_start(shards, tag, after=None):
    n = len(shards)
    lands = [lax.empty((4,) + s.shape, s.dtype) for s in shards]

    def plan(srcs, dsts, x, y, c):
        k = 2 * x + y
        copies = []
        for w_ref, o_ref in zip(srcs, dsts):
            h = w_ref.shape[0] // 2
            rows = pl.ds(c * h, h)
            copies += [(w_ref.at[rows], o_ref.at[k, rows], (_flip(x, dx), _flip(y, dy), c)) for dx, dy in CHIP_FLIPS]
        return copies

    handle, token = _split_start(f"gather_{tag}_start", shards, lands, 3 * n, plan, after)
    return (handle, plan, n), token


def _gather_wait(started, after, tag):
    handle, plan, n = started
    return _split_wait(f"gather_{tag}_wait", handle, n, 3 * n, plan, after)


def _gather_d2d(lands, before, tag):
    n = len(lands)

    def plan_d2d(in_refs, out_refs, x, y, c):
        remote = []
        for o_ref in out_refs:
            h = o_ref.shape[1] // 2
            for dx, dy in CHIP_FLIPS:
                half = o_ref.at[2 * _flip(x, dx) + _flip(y, dy), pl.ds(c * h, h)]
                remote.append((half, half, (x, y, 1 - c)))
        return remote, []

    return _comm(f"gather_{tag}_d2d", list(lands) + list(before),
                 [jax.ShapeDtypeStruct(l_.shape, l_.dtype) for l_ in lands], 3 * n, 0, plan_d2d,
                 aliases={i: i for i in range(n)})


def _pair_plan(in_refs, out_refs, x, y, c):
    return [(i_, o_, (x, y, 1 - c)) for i_, o_ in zip(in_refs, out_refs)], []


def _rs_start(grads, tag):
    n = len(grads)
    c = lax.axis_index("c")
    def rows(g, start, h):
        return g.rows(start, h) if isinstance(g, _WInGrad) else lax.dynamic_slice_in_dim(g, start, h, axis=1)

    halves = [g.shape[1] // 2 for g in grads]
    mine = [rows(g, c * h, h) for g, h in zip(grads, halves)]
    send_a = [rows(g, (1 - c) * h, h).astype(BF16) for g, h in zip(grads, halves)]
    recv_a = _comm(f"rs_pair_{tag}", send_a, [jax.ShapeDtypeStruct(s.shape, BF16) for s in send_a], n, 0, _pair_plan)
    pair, pair_b = [], []
    for i, (mi, ra) in enumerate(zip(mine, recv_a)):
        four, h, cols = mi.shape
        p32, p16 = _addn(mi.reshape(four * h, cols), [ra.reshape(four * h, cols)], name=f"rs_pair_sum_{tag}_{i}",
                         also_bf16=True)
        pair.append(p32.reshape(four, h, cols))
        pair_b.append(p16.reshape(four, h, cols))

    def plan(srcs, dsts, x, y, c_):
        copies = []
        for i_, o_ in zip(srcs, dsts):
            for j, (dx, dy) in enumerate(CHIP_FLIPS):
                fx, fy = _flip(x, dx), _flip(y, dy)
                copies.append((i_.at[2 * fx + fy], o_.at[j], (fx, fy, c_)))
        return copies

    lands = [lax.empty((3,) + p.shape[1:], BF16) for p in pair_b]
    handle, token = _split_start(f"rs_chips_{tag}_start", pair_b, lands, 3 * n, plan)
    return (handle, plan, n, pair), token


def _rs_finish(started, after, tag):
    handle, plan, n, pair = started
    recv_b = _split_wait(f"rs_chips_{tag}_wait", handle, n, 3 * n, plan, after)
    k = 2 * lax.axis_index("x") + lax.axis_index("y")
    tot = [_addn(lax.dynamic_index_in_dim(p, k, 0, keepdims=False), [rb[0], rb[1], rb[2]],
                 name=f"rs_chip_sum_{tag}_{i}") for i, (p, rb) in enumerate(zip(pair, recv_b))]
    other = _comm(f"rs_halves_{tag}", tot, [jax.ShapeDtypeStruct(t.shape, F32) for t in tot], n, 0, _pair_plan)
    return tot, other


def _gather_all(vec, *, name, before=()):
    out = jax.ShapeDtypeStruct((8,) + vec.shape, vec.dtype)

    def plan(in_refs, out_refs, x, y, c):
        me = 4 * x + 2 * y + c
        remote = [(in_refs[0], out_refs[0].at[me], (_flip(x, dx), _flip(y, dy), _flip(c, dc)))
                  for dx in (0, 1) for dy in (0, 1) for dc in (0, 1) if (dx, dy, dc) != (0, 0, 0)]
        return remote, [(in_refs[0], out_refs[0].at[me])]

    return _comm(name, [vec] + list(before), [out], 7, 1, plan)[0]


def _pack(parts):
    flat = [p.reshape(-1).astype(F32) for p in parts]
    total = sum(f.shape[0] for f in flat)
    n = -(-total // 1024) * 128
    vec = jnp.concatenate(flat + [jnp.zeros((8 * n - total,), F32)]).reshape(8, n)
    offs, o = [], 0
    for f in flat:
        offs.append((o, f.shape[0]))
        o += f.shape[0]
    return vec, offs


def _unpack(vec, offs, shapes):
    flat = vec.reshape(-1)
    return [flat[o:o + n].reshape(s) for (o, n), s in zip(offs, shapes)]


BIG = (("ffn1_w_gate_up", "col"), ("ffn1_w_down", "row"), ("w_in", "col"), ("w_out_a", "row"), ("w_out_ssm", "row"),
       ("w_mix_out", "row"), ("w_q", "row"), ("w_kv", "col"), ("w_o_x", "row"), ("ffn2_w_gate_up", "col"),
       ("ffn2_w_down", "row"))
SMALL = ("ffn1_norm", "mix_norm", "conv_a_w", "ssm_conv_w", "ssm_conv_b", "ssm_dt_bias", "ssm_a_log", "ssm_d",
         "ssm_norm", "xattn_norm", "mem_norm", "ffn2_norm", "final_norm")
WEIGHTS = ("ffn1_norm", "ffn1_w_gate_up", "ffn1_w_down", "mix_norm", "w_in", "conv_a_w", "w_out_a", "ssm_conv_w",
           "ssm_conv_b", "ssm_dt_bias", "ssm_a_log", "ssm_d", "ssm_norm", "w_out_ssm", "w_mix_out", "xattn_norm",
           "mem_norm", "w_q", "w_kv", "w_o_x", "ffn2_norm", "ffn2_w_gate_up", "ffn2_w_down", "final_norm")


GATHER_GROUPS = (("a", ("ffn1_w_gate_up",)), ("b", ("ffn1_w_down", "w_in")),
                 ("c", ("w_out_a", "w_out_ssm", "w_mix_out", "w_q", "w_kv", "w_o_x", "ffn2_w_gate_up", "ffn2_w_down")))


def _place_own(land, own, k, *, name):
    four, r, cols = land.shape
    tr = _row_tile(r, cols)

    def body(k_ref, own_ref, land_in, o_ref):
        del k_ref, land_in
        o_ref[...] = own_ref[...]

    return pl.pallas_call(
        body,
        grid_spec=pltpu.PrefetchScalarGridSpec(
            num_scalar_prefetch=1, grid=(r // tr,),
            in_specs=[pl.BlockSpec((tr, cols), lambda i, k_: (i, 0)), pl.BlockSpec(memory_space=pl.ANY)],
            out_specs=pl.BlockSpec((None, tr, cols), lambda i, k_: (k_[0], i, 0))),
        out_shape=jax.ShapeDtypeStruct(land.shape, land.dtype), input_output_aliases={2: 0},
        compiler_params=_cp("parallel"), name=name)(jnp.reshape(k, (1,)).astype(jnp.int32), own, land)


def _full_weight(land, own, kind, k, *, name):
    land = _place_own(land, own, k, name=name)
    four, r, cols = land.shape
    if kind == "row":
        return land.reshape(four * r, cols)
    return jnp.transpose(land, (1, 0, 2)).reshape(r, four * cols)


class _GatheredWeights:
    def __init__(self, shards32, k, after):
        first = GATHER_GROUPS[0][1]
        self.shards, self.k = {n: shards32[n].astype(BF16)[0] for n in first}, k
        self.full = {}
        self.n_done = 0
        self.started, token = self._start(0, after)
        self.token = token[0, 0]
        self.shards.update({n: (w + token[0, 0]).astype(BF16)[0] for n, w in shards32.items() if n not in first})
        self.after = jnp.stack([self.shards[n][0, 0] for n in shards32 if n not in first]).astype(F32).reshape(1, -1)

    def _start(self, gi, after):
        tag, names = GATHER_GROUPS[gi]
        return _gather_start([self.shards[n] for n in names], tag, after)

    def mark(self, value):
        self.after = value

    def __getitem__(self, name):
        if name not in self.full:
            tag, names = GATHER_GROUPS[self.n_done]
            assert name in names, (name, tag)
            lands = _gather_wait(self.started, self.after, tag)
            before = []
            if self.n_done + 1 < len(GATHER_GROUPS):
                self.started, token = self._start(self.n_done + 1, lands[0])
                before = [token]
            lands = _gather_d2d(lands, before, tag)
            for n, land in zip(names, lands):
                if n == "w_in":
                    self.full[n] = _pad_w_in_shards(_place_own(land, self.shards[n], self.k, name=f"own_{n}"))
                elif n.endswith("w_gate_up"):
                    self.full[n] = _place_own(land, self.shards[n], self.k, name=f"own_{n}")
                else:
                    self.full[n] = _full_weight(land, self.shards[n], dict(BIG)[n], self.k, name=f"own_{n}")
            self.n_done += 1
        return self.full[name]


def _shard_major(dw, kind):
    if isinstance(dw, tuple):
        return jnp.concatenate(dw, axis=0)
    if dw.ndim == 3:
        return dw
    if kind == "row":
        return dw.reshape(4, dw.shape[0] // 4, dw.shape[1])
    return jnp.transpose(dw.reshape(dw.shape[0], 4, dw.shape[1] // 4), (1, 0, 2))


def _pad_rows8(w):
    return jnp.concatenate([w, jnp.zeros((8 - w.shape[0], w.shape[1]), w.dtype)], axis=0)


def _group_lanes(v):
    r = v.shape[0]
    return jnp.pad(v.reshape(r, NG, NH // NG), ((0, 0), (0, 0), (0, 128 - NH // NG))).reshape(r, NG * 128)


def _ungroup_lanes(v):
    r = v.shape[0]
    return v.reshape(r, NG, 128)[:, :, :NH // NG].reshape(r, NH)


def _local_step(wfull, small, x, mem, target, token=0.0, on_grads=None):
    bl, s, _ = x.shape
    T = bl * s
    x2, t2 = x.reshape(T, D), target.reshape(T, D)
    mem2 = mem.reshape(-1, D)
    g = {}
    tok = [token]
    mark = getattr(wfull, "mark", lambda value: None)

    def gain(name):
        return small[name].reshape(1, -1) + tok[0]

    def emit(tag, names):
        if on_grads is not None:
            tok[0] = tok[0] + on_grads(tag, {n: g[n] for n in names})

    def ffn_fwd(h, n, wgu, wd, tag, next_gain=None):
        gate, up, a = _gate_up_fwd(n, wfull[wgu], name=f"{tag}_gate_up")
        mark(a)
        out = _mm(a, wfull[wd], "nn", tk=DFF, scale=FFN_RES, residual=h, norm_gain=next_gain, name=f"{tag}_down")
        return out, (n, gate, up, a)

    def ffn_bwd(dh, h, norm, wgu, wd, saved, tag):
        n, gate, up, a = saved
        dgate, dup = _act_bwd(dh, wfull[wd], gate, up, FFN_RES, name=f"{tag}_d_act")
        g[wd] = _mm(a, dh, "tn", tm=1408, scale=FFN_RES, name=f"{tag}_d_w_down")
        g[wgu] = (_mm(n, dgate, "tn", tn=1408, col_shards=2, name=f"{tag}_d_w_gate"),
                  _mm(n, dup, "tn", tn=1408, col_shards=2, name=f"{tag}_d_w_up"))
        emit(tag, (wgu, wd))
        dh_in, g[norm] = _gate_up_bwd_input(dgate, dup, wfull[wgu], h, gain(norm), dh, name=f"{tag}_d_norm")
        return dh_in

    n1 = _norm_fwd(x2, gain("ffn1_norm"), name="ffn1_norm")
    (h1, u), ffn1_saved = ffn_fwd(x2, n1, "ffn1_w_gate_up", "ffn1_w_down", "ffn1", gain("mix_norm"))
    mark(h1)
    pp = _mm(u, wfull["w_in"], "nn", tm=2048, tn=1152, name="in_proj")
    wa8 = _pad_rows8(small["conv_a_w"])
    ws8 = _pad_rows8(small["ssm_conv_w"])
    conv_b = gain("ssm_conv_b")
    bias128 = jnp.pad(gain("ssm_dt_bias"), ((0, 0), (0, 128 - NH)))
    ya_pre = _conv_a_fwd(pp, wa8, bl, s, name="conv_a")
    xc = _conv_ssm_fwd(pp, ws8, conv_b, bl, s, name="conv_ssm")
    mark(xc)
    alog = gain("ssm_a_log")
    dtg, dtt = _dt_fwd(pp, bias128, name="dt")
    arow, acol = _group_lanes(alog), alog.reshape(NG, NH // NG, 1)
    dexp = jnp.repeat(gain("ssm_d"), HD, axis=1)
    ng = gain("ssm_norm")
    y, yn, prev = _ssd_fwd(xc, pp, dtg, dtt, arow, acol, dexp, ng, bl, s, name="ssd")
    ya = _mm(ya_pre, wfull["w_out_a"], "nn", tn=1024, out_dtype=BF16, name="out_a")
    yb = _mm(yn, wfull["w_out_ssm"], "nn", tn=1024, tk=DI, out_dtype=BF16, name="out_ssm")
    merged = _merge_fwd(pp, ya, yb, name="merge")
    h2, un = _mm(merged, wfull["w_mix_out"], "nn", residual=h1, norm_gain=gain("xattn_norm"), name="mix_out")
    q = _mm(un, wfull["w_q"], "nn", tn=1024, out_dtype=BF16, name="q_proj")
    mn = _norm_fwd(mem2, gain("mem_norm"), name="mem_norm")
    kv = _mm(mn, wfull["w_kv"], "nn", tn=1024, out_dtype=BF16, name="kv_proj")
    o = _attn_fwd(q, kv, bl, s, name="attn")
    h3, n2 = _mm(o, wfull["w_o_x"], "nn", residual=h2, norm_gain=gain("ffn2_norm"), name="attn_out")
    h4, ffn2_saved = ffn_fwd(h3, n2, "ffn2_w_gate_up", "ffn2_w_down", "ffn2")
    sq_err, dh4, dgf = _final_loss(h4, gain("final_norm"), t2, name="final_loss")
    g["final_norm"] = dgf

    dh3 = ffn_bwd(dh4, h3, "ffn2_norm", "ffn2_w_gate_up", "ffn2_w_down", ffn2_saved, "ffn2")
    do = _mm(dh3, wfull["w_o_x"], "nt", tn=1024, out_dtype=BF16, name="d_attn_o")
    g["w_o_x"] = _mm(o, dh3, "tn",name="d_w_o_x")
    dq, dk, dv = _attn_bwd(q, kv, do, bl, s, name="d_attn")
    g["w_q"] = _mm(un, dq, "tn",name="d_w_q")
    dkv = jnp.concatenate([dk, dv], axis=1)
    dmn = _mm(dkv, wfull["w_kv"], "nt", tn=1024, tk=2 * D, name="d_mem_norm_out")
    g["w_kv"] = _mm(mn, dkv, "tn", tn=512, col_shards=4, name="d_w_kv")
    emit("attn", ("w_q", "w_kv", "w_o_x"))
    _, g["mem_norm"] = _norm_bwd(mem2, gain("mem_norm"), dmn, None, name="d_mem_norm")
    dh2, g["xattn_norm"] = _mm(dq, wfull["w_q"], "nt", tm=512, residual=dh3, norm_bwd=(h2, gain("xattn_norm")),
                               name="d_xattn_norm")
    g["w_mix_out"] = _mm(merged, dh2, "tn",name="d_w_mix_out")
    dya, dyb, dpp = _merge_bwd(pp, ya, yb, dh2, wfull["w_mix_out"], name="d_merge")
    dya_pre = _mm(dya, wfull["w_out_a"], "nt", tn=1024, name="d_conv_a_out")
    g["w_out_a"] = _mm(ya_pre, dya, "tn",name="d_w_out_a")
    dyn = _mm(dyb, wfull["w_out_ssm"], "nt", tn=DI, name="d_ssd_out")
    g["w_out_ssm"] = _mm(yn, dyb, "tn",name="d_w_out_ssm")
    dpp, dwa8 = _conv_a_bwd(pp, wa8, dya_pre, dpp, bl, s, name="d_conv_a")
    g["conv_a_w"] = dwa8[:3]
    dpp, dxs, dbm, dcm, ddtg, g["ssm_norm"], ddexp, dalg = _ssd_bwd(
        dyn, y, xc, pp, dtg, dtt, arow, acol, dexp, ng, prev, dpp, bl, s, name="d_ssd")
    g["ssm_d"] = ddexp.reshape(NH, HD).sum(axis=1).reshape(1, NH)
    g["ssm_a_log"] = _ungroup_lanes(dalg)
    conv_dw, conv_db = [], []
    for dpart, off, tag in ((dxs, 0, "x"), (dbm, DI, "b"), (dcm, DI + NG * NS, "c")):
        dpp, dw_, db_ = _conv_ssm_bwd(pp, ws8, conv_b, dpart, off, dpp, bl, s, name=f"d_conv_ssm_{tag}")
        conv_dw.append(dw_)
        conv_db.append(db_)
    g["ssm_conv_w"] = jnp.concatenate(conv_dw, axis=1)[:4]
    g["ssm_conv_b"] = jnp.concatenate(conv_db, axis=1)
    dpp, dbias = _dt_bwd(pp, bias128, ddtg, dpp, name="d_dt")
    g["ssm_dt_bias"] = dbias[:, :NH]
    g["w_in"] = _mm(u, dpp, "tn", tn=1152, name="d_w_in")
    emit("mix", ("w_in", "w_out_a", "w_out_ssm", "w_mix_out"))
    dh1, g["mix_norm"] = _mm(dpp, wfull["w_in"], "nt", tm=512, tk=3456, residual=dh2,
                             norm_bwd=(h1, gain("mix_norm")), name="d_mix_norm")
    dx = ffn_bwd(dh1, x2, "ffn1_norm", "ffn1_w_gate_up", "ffn1_w_down", ffn1_saved, "ffn1")
    return sq_err, dx, g


W_IN_SHARD = NIN // 4


def _w_in_segments():
    segs, p = [], 0
    for t in range(D // CA_TILE):
        for which in range(3):
            segs.append((D * which + CA_TILE * t, p, CA_TILE))
            p += CA_TILE
    for s, n in ((3 * D, O_GA - 3 * D), (O_GA + NH, 2 * D), (O_GA, NH)):
        segs.append((s, p, n))
        p += n
    assert p == NIN and segs[-1][1] == O_DT and segs[-2][1] == O_GA
    return segs


def _pad_w_in_shards(land):
    pieces = []
    for s, _, n in _w_in_segments():
        while n > 0:
            kk, off = divmod(s, W_IN_SHARD)
            take = min(n, W_IN_SHARD - off)
            pieces.append(land[kk][:, off:off + take])
            s, n = s + take, n - take
    return jnp.concatenate(pieces + [jnp.zeros((land.shape[1], NPP - NIN), land.dtype)], axis=1)


class _WInGrad:
    def __init__(self, dwp):
        self.dwp = dwp
        self.shape = (4, dwp.shape[0], W_IN_SHARD)

    def rows(self, start, n):
        part = lax.dynamic_slice_in_dim(self.dwp, start, n, axis=0)
        shards = []
        for kk in range(4):
            n0, n1 = W_IN_SHARD * kk, W_IN_SHARD * (kk + 1)
            cuts = sorted((max(s, n0), p + max(s, n0) - s, min(s + m, n1) - max(s, n0))
                          for s, p, m in _w_in_segments() if min(s + m, n1) > max(s, n0))
            shards.append(jnp.concatenate([part[:, p:p + m] for _, p, m in cuts], axis=1))
        return jnp.stack(shards)


def _pad_w_in(w):
    return _pad_w_in_shards(jnp.stack(jnp.split(w, 4, axis=1)))


def _unpad_w_in(w):
    return jnp.concatenate(list(_WInGrad(w).rows(0, w.shape[0])), axis=1)


def kernel(x, mem, ffn1_norm, ffn1_w_gate_up, ffn1_w_down, mix_norm, w_in, conv_a_w, w_out_a, ssm_conv_w, ssm_conv_b, ssm_dt_bias, ssm_a_log, ssm_d, ssm_norm, w_out_ssm, w_mix_out, xattn_norm, mem_norm, w_q, w_kv, w_o_x, ffn2_norm, ffn2_w_gate_up, ffn2_w_down, final_norm, loss_target, m_ffn1_norm, m_ffn1_w_gate_up, m_ffn1_w_down, m_mix_norm, m_w_in, m_conv_a_w, m_w_out_a, m_ssm_conv_w, m_ssm_conv_b, m_ssm_dt_bias, m_ssm_a_log, m_ssm_d, m_ssm_norm, m_w_out_ssm, m_w_mix_out, m_xattn_norm, m_mem_norm, m_w_q, m_w_kv, m_w_o_x, m_ffn2_norm, m_ffn2_w_gate_up, m_ffn2_w_down, m_final_norm, v_ffn1_norm, v_ffn1_w_gate_up, v_ffn1_w_down, v_mix_norm, v_w_in, v_conv_a_w, v_w_out_a, v_ssm_conv_w, v_ssm_conv_b, v_ssm_dt_bias, v_ssm_a_log, v_ssm_d, v_ssm_norm, v_w_out_ssm, v_w_mix_out, v_xattn_norm, v_mem_norm, v_w_q, v_w_kv, v_w_o_x, v_ffn2_norm, v_ffn2_w_gate_up, v_ffn2_w_down, v_final_norm):
    a = dict(locals())
    xi, yi = lax.axis_index("x"), lax.axis_index("y")
    k = 2 * xi + yi

    conv_vec, conv_offs = _pack([a["conv_a_w"], a["ssm_conv_w"]])
    conv_all = _gather_all(conv_vec, name="gather_conv_w")
    wfull = _GatheredWeights({n: a[n] for n, _ in BIG}, k, conv_all)
    conv_sh = [_unpack(conv_all[2 * kk], conv_offs, [a["conv_a_w"].shape[1:], a["ssm_conv_w"].shape[1:]])
               for kk in range(4)]
    small = {n: a[n] for n in SMALL}
    small["conv_a_w"] = jnp.concatenate([cs[0] for cs in conv_sh], axis=1)
    small["ssm_conv_w"] = jnp.concatenate([cs[1] for cs in conv_sh], axis=1)

    rs_started = []

    def on_grads(tag, grads):
        names = [n for n, _ in BIG if n in grads]
        shard_major = [_WInGrad(grads[n]) if n == "w_in" else _shard_major(grads[n], dict(BIG)[n]) for n in names]
        st, tk = _rs_start(shard_major, tag)
        rs_started.append((tag, names, st))
        return tk[0, 0]

    sq_err, dx, g = _local_step(wfull, small, x, mem, loss_target, wfull.token, on_grads)
    loss = lax.psum(0.5 / D * jnp.sum(sq_err), ("x", "y", "c"))

    ci = lax.axis_index("c")
    out = {}

    def finish(tag, names, st, after):
        g_mine, g_other = _rs_finish(st, after, tag)
        for n, gm, go in zip(names, g_mine, g_other):
            res = _adamw_halves(a[n], gm, go, a["m_" + n], a["v_" + n], ci, name=f"adamw_{n}")
            out[n] = tuple(t.reshape(a[n].shape) for t in res)
        return res[1]

    done = dx
    for grp in rs_started[:-1]:
        done = finish(*grp, dx)

    full_shapes = [g[n].shape for n in SMALL]
    gvec, goffs = _pack([g[n] for n in SMALL])
    gsum = _sum_leading(_gather_all(gvec, name="gather_small_grads", before=[done]), name="sum_small_grads")
    finish(*rs_started[-1], gsum)
    gsmall = dict(zip(SMALL, _unpack(gsum, goffs, full_shapes)))
    for n in ("conv_a_w", "ssm_conv_w"):
        width = a[n].shape[2]
        gsmall[n] = lax.dynamic_slice_in_dim(gsmall[n], k * width, width, axis=1)
    local_shapes = [a[n].shape for n in SMALL]
    packs = [_pack([t[n] for n in SMALL]) for t in
             ({n: a[n] for n in SMALL}, gsmall, {n: a["m_" + n] for n in SMALL}, {n: a["v_" + n] for n in SMALL})]
    offs = packs[0][1]
    res = _adamw(*[p[0] for p in packs], name="adamw_small")
    unp = [_unpack(r, offs, local_shapes) for r in res]
    for i, n in enumerate(SMALL):
        out[n] = (gsmall[n].reshape(a[n].shape), unp[0][i], unp[1][i], unp[2][i])

    grad_x = dx.reshape(x.shape)
    return (loss, grad_x, *[out[n][0] for n in WEIGHTS], *[out[n][1] for n in WEIGHTS],
            *[out[n][2] for n in WEIGHTS], *[out[n][3] for n in WEIGHTS])
```

```python
import functools
import math

import jax
import jax.numpy as jnp
from jax import lax
from jax.experimental import pallas as pl
from jax.experimental.pallas import tpu as pltpu

F32 = jnp.float32
BF16 = jnp.bfloat16
MXU = jnp.bfloat16
HI = lax.Precision.HIGHEST

D = 1024
DFF = 2816
DI = 2048
NH, HD, NG, NS, CH = 32, 64, 4, 128, 128
GW = DI // NG
XH, XD = 4, 256
EPS = 1e-6
NEG = -1e30
CA_TILE = 256
O_CA, O_Z, O_XBC, O_GA, O_GB, O_DT, NPP = 0, 3072, 5120, 8192, 9216, 10240, 10368
NIN = 10272
FFN_RES = 0.5
ADAM_LR, ADAM_B1, ADAM_B2, ADAM_EPS, ADAM_WD, ADAM_STEP = 0.001, 0.9, 0.999, 1e-08, 0.01, 10
VMEM_LIMIT = 56 * 1024 * 1024
EPI_COLS = 256
SSD_EX = 4
MESH = pl.DeviceIdType.MESH
CHIP_FLIPS = ((1, 0), (0, 1), (1, 1))


def _cp(*sem):
    return pltpu.CompilerParams(dimension_semantics=sem, vmem_limit_bytes=VMEM_LIMIT)


def _tile(n, pref, align=128):
    if n <= pref:
        return n
    t = (pref // align) * align
    while t >= align:
        if n % t == 0:
            return t
        t -= align
    raise ValueError((n, pref))


def _dot(a, b, dims, prec=None):
    return lax.dot_general(a, b, (dims, ((), ())), preferred_element_type=F32, precision=prec)


def _nn(a, b, prec=None):
    return _dot(a, b, ((1,), (0,)), prec)


def _nt(a, b):
    return _dot(a, b, ((1,), (1,)))


def _tn(a, b):
    return _dot(a, b, ((0,), (0,)))


def _sig(x):
    return jax.nn.sigmoid(x)


def _mm(a, b, mode, *, name, tm=1024, tn=1024, tk=None, out_dtype=F32, scale=None, residual=None, col_shards=0,
        norm_gain=None, norm_bwd=None):
    if tk is None:
        tk = 2048 if mode == "tn" else 1024
    if mode == "nn":
        (M, K), (K2, N) = a.shape, b.shape
    elif mode == "nt":
        (M, K), (N, K2) = a.shape, b.shape
    else:
        (K, M), (K2, N) = a.shape, b.shape
    assert K == K2, (name, a.shape, b.shape)
    tm, tn, tk = _tile(M, tm), _tile(N, tn), _tile(K, tk)
    nk = K // tk
    if mode == "nn":
        a_spec = pl.BlockSpec((tm, tk), lambda i, j, k: (i, k))
        b_spec = pl.BlockSpec((tk, tn), lambda i, j, k: (k, j))
        dims = ((1,), (0,))
    elif mode == "nt":
        a_spec = pl.BlockSpec((tm, tk), lambda i, j, k: (i, k))
        b_spec = pl.BlockSpec((tn, tk), lambda i, j, k: (j, k))
        dims = ((1,), (1,))
    else:
        a_spec = pl.BlockSpec((tk, tm), lambda i, j, k: (k, i))
        b_spec = pl.BlockSpec((tk, tn), lambda i, j, k: (k, j))
        dims = ((0,), (0,))
    o_spec = pl.BlockSpec((tm, tn), lambda i, j, k: (i, j))
    out_spec, out_shape = o_spec, jax.ShapeDtypeStruct((M, N), out_dtype)
    if col_shards:
        per = N // col_shards // tn
        assert per * tn * col_shards == N, (name, N, tn, col_shards)
        out_spec = pl.BlockSpec((None, tm, tn), lambda i, j, k: (j // per, i, j % per))
        out_shape = jax.ShapeDtypeStruct((col_shards, M, N // col_shards), out_dtype)
    has_res = residual is not None
    has_norm = norm_gain is not None
    has_nb = norm_bwd is not None
    assert not (has_norm or has_nb) or (tn == N and not col_shards)
    assert not (has_norm and has_nb)
    n_in = 2 + has_res + has_norm + 2 * has_nb

    def body(*refs):
        a_ref, b_ref = refs[0], refs[1]
        o_ref = refs[n_in]

        if has_nb:
            @pl.when((pl.program_id(0) == 0) & (pl.program_id(2) == 0))
            def _():
                refs[n_in + 1][...] = jnp.zeros_like(refs[n_in + 1])

        def finish(acc):
            if scale is not None:
                acc = acc * scale
            if has_nb:
                xv, gv = refs[n_in - 2][...], refs[n_in - 1][...]
                r = lax.rsqrt(jnp.mean(xv * xv, axis=-1, keepdims=True) + EPS)
                xh = xv * r
                refs[n_in + 1][...] += jnp.sum(acc * xh, axis=0, keepdims=True)
                dxh = acc * gv
                acc = r * (dxh - xh * jnp.mean(dxh * xh, axis=-1, keepdims=True))
            if has_res:
                acc = acc + refs[2][...]
            o_ref[...] = acc.astype(out_dtype)
            if has_norm:
                rs = lax.rsqrt(jnp.mean(acc * acc, axis=-1, keepdims=True) + EPS)
                refs[n_in + 1][...] = (acc * rs * refs[n_in - 1][...]).astype(BF16)

        part = _dot(a_ref[...].astype(MXU), b_ref[...].astype(MXU), dims)
        if nk == 1:
            finish(part)
            return
        acc_ref = refs[-1]
        k = pl.program_id(2)

        @pl.when(k == 0)
        def _():
            acc_ref[...] = part

        @pl.when(k > 0)
        def _():
            acc_ref[...] += part

        @pl.when(k == nk - 1)
        def _():
            finish(acc_ref[...])

    ins, in_specs = [a, b], [a_spec, b_spec]
    if has_res:
        ins.append(residual)
        in_specs.append(o_spec)
    if has_norm:
        ins.append(norm_gain)
        in_specs.append(pl.BlockSpec((1, tn), lambda i, j, k: (0, j)))
        out_spec, out_shape = [out_spec, o_spec], [out_shape, jax.ShapeDtypeStruct((M, N), BF16)]
    if has_nb:
        vec = pl.BlockSpec((1, tn), lambda i, j, k: (0, j))
        ins += [norm_bwd[0], norm_bwd[1]]
        in_specs += [o_spec, vec]
        out_spec, out_shape = [out_spec, vec], [out_shape, jax.ShapeDtypeStruct((1, N), F32)]
    return pl.pallas_call(
        body, grid=(M // tm, N // tn, nk), in_specs=in_specs, out_specs=out_spec, out_shape=out_shape,
        scratch_shapes=[pltpu.VMEM((tm, tn), F32)] if nk > 1 else [],
        compiler_params=_cp(*(("arbitrary",) * 3 if has_nb else ("parallel", "parallel", "arbitrary"))),
        name=name)(*ins)


def _norm_fwd(x, g, *, name):
    T, d = x.shape
    tr = _tile(T, 512, 8)

    def body(x_ref, g_ref, o_ref):
        xv = x_ref[...]
        r = lax.rsqrt(jnp.mean(xv * xv, axis=-1, keepdims=True) + EPS)
        o_ref[...] = (xv * r * g_ref[...]).astype(BF16)

    return pl.pallas_call(
        body, grid=(T // tr,),
        in_specs=[pl.BlockSpec((tr, d), lambda i: (i, 0)), pl.BlockSpec((1, d), lambda i: (0, 0))],
        out_specs=pl.BlockSpec((tr, d), lambda i: (i, 0)),
        out_shape=jax.ShapeDtypeStruct((T, d), BF16), compiler_params=_cp("parallel"), name=name)(x, g)


def _norm_bwd(x, g, dn, dres, *, name):
    T, d = x.shape
    tr = _tile(T, 512, 8)
    has_res = dres is not None

    def body(*refs):
        x_ref, g_ref, dn_ref = refs[:3]
        dr_ref = refs[3] if has_res else None
        dx_ref, dg_ref = refs[-2], refs[-1]

        @pl.when(pl.program_id(0) == 0)
        def _():
            dg_ref[...] = jnp.zeros_like(dg_ref)

        xv = x_ref[...]
        dnv = dn_ref[...].astype(F32)
        r = lax.rsqrt(jnp.mean(xv * xv, axis=-1, keepdims=True) + EPS)
        xh = xv * r
        dg_ref[...] += jnp.sum(dnv * xh, axis=0, keepdims=True)
        dxh = dnv * g_ref[...]
        dx = r * (dxh - xh * jnp.mean(dxh * xh, axis=-1, keepdims=True))
        if has_res:
            dx = dx + dr_ref[...]
        dx_ref[...] = dx

    row = pl.BlockSpec((tr, d), lambda i: (i, 0))
    vec = pl.BlockSpec((1, d), lambda i: (0, 0))
    ins = [x, g, dn] + ([dres] if has_res else [])
    return pl.pallas_call(
        body, grid=(T // tr,), in_specs=[row, vec, row] + ([row] if has_res else []),
        out_specs=[row, vec],
        out_shape=[jax.ShapeDtypeStruct((T, d), F32), jax.ShapeDtypeStruct((1, d), F32)],
        compiler_params=_cp("arbitrary"), name=name)(*ins)


def _final_loss(h, g, target, *, name):
    T, d = h.shape
    tr = _tile(T, 512, 8)

    def body(h_ref, g_ref, t_ref, l_ref, dh_ref, dg_ref):
        @pl.when(pl.program_id(0) == 0)
        def _():
            l_ref[...] = jnp.zeros_like(l_ref)
            dg_ref[...] = jnp.zeros_like(dg_ref)

        xv = h_ref[...]
        r = lax.rsqrt(jnp.mean(xv * xv, axis=-1, keepdims=True) + EPS)
        xh = xv * r
        e = xh * g_ref[...] - t_ref[...]
        l_ref[...] += jnp.sum(e * e, axis=0, keepdims=True)
        dy = e * (1.0 / d)
        dg_ref[...] += jnp.sum(dy * xh, axis=0, keepdims=True)
        dxh = dy * g_ref[...]
        dh_ref[...] = r * (dxh - xh * jnp.mean(dxh * xh, axis=-1, keepdims=True))

    row = pl.BlockSpec((tr, d), lambda i: (i, 0))
    vec = pl.BlockSpec((1, d), lambda i: (0, 0))
    return pl.pallas_call(
        body, grid=(T // tr,), in_specs=[row, vec, row], out_specs=[vec, row, vec],
        out_shape=[jax.ShapeDtypeStruct((1, d), F32), jax.ShapeDtypeStruct((T, d), F32),
                   jax.ShapeDtypeStruct((1, d), F32)],
        compiler_params=_cp("arbitrary"), name=name)(h, g, target)


def _shard_chunks(width, size=256):
    starts = list(range(0, width, size))
    if width - starts[-1] < EPI_COLS and len(starts) > 1:
        starts.pop()
    return [(o, (starts[i + 1] if i + 1 < len(starts) else width) - o) for i, o in enumerate(starts)]


def _gate_up_fwd(n, wsh, *, name):
    T, d = n.shape
    ws = wsh.shape[2]
    f = 2 * ws
    tm = _tile(T, 512, 8)

    def body(n_ref, wg_ref, wu_ref, g_ref, u_ref, a_ref):
        nv = n_ref[...].astype(MXU)
        for sh in range(2):
            for off, size in _shard_chunks(ws):
                gv = _nn(nv, wg_ref[sh, :, off:off + size].astype(MXU))
                uv = _nn(nv, wu_ref[sh, :, off:off + size].astype(MXU))
                sl = slice(sh * ws + off, sh * ws + off + size)
                g_ref[:, sl] = gv.astype(BF16)
                u_ref[:, sl] = uv.astype(BF16)
                a_ref[:, sl] = (gv * _sig(gv) * uv).astype(BF16)

    out = pl.BlockSpec((tm, f), lambda i: (i, 0))
    act = jax.ShapeDtypeStruct((T, f), BF16)
    return pl.pallas_call(
        body, grid=(T // tm,),
        in_specs=[pl.BlockSpec((tm, d), lambda i: (i, 0)), pl.BlockSpec((2, d, ws), lambda i: (0, 0, 0)),
                  pl.BlockSpec((2, d, ws), lambda i: (1, 0, 0))],
        out_specs=[out, out, out], out_shape=[act, act, act], compiler_params=_cp("parallel"),
        name=name)(n, wsh, wsh)


def _gate_up_bwd_input(dgate, dup, wsh, x, g, dres, *, name):
    T, f = dgate.shape
    four, d, ws = wsh.shape
    tm = _tile(T, 256, 8)

    def body(dg_ref, du_ref, w_ref, x_ref, g_ref, dr_ref, dx_ref, dgain_ref):
        @pl.when(pl.program_id(0) == 0)
        def _():
            dgain_ref[...] = jnp.zeros_like(dgain_ref)

        dn = None
        for sh in range(four):
            src = dg_ref if sh < 2 else du_ref
            part = _nt(src[:, (sh % 2) * ws:(sh % 2 + 1) * ws].astype(MXU), w_ref[sh].astype(MXU))
            dn = part if dn is None else dn + part
        xv = x_ref[...]
        r = lax.rsqrt(jnp.mean(xv * xv, axis=-1, keepdims=True) + EPS)
        xh = xv * r
        dgain_ref[...] += jnp.sum(dn * xh, axis=0, keepdims=True)
        dxh = dn * g_ref[...]
        dx_ref[...] = r * (dxh - xh * jnp.mean(dxh * xh, axis=-1, keepdims=True)) + dr_ref[...]

    act = pl.BlockSpec((tm, f), lambda i: (i, 0))
    row = pl.BlockSpec((tm, d), lambda i: (i, 0))
    vec = pl.BlockSpec((1, d), lambda i: (0, 0))
    return pl.pallas_call(
        body, grid=(T // tm,),
        in_specs=[act, act, pl.BlockSpec((four, d, ws), lambda i: (0, 0, 0)), row, vec, row],
        out_specs=[row, vec],
        out_shape=[jax.ShapeDtypeStruct((T, d), F32), jax.ShapeDtypeStruct((1, d), F32)],
        compiler_params=_cp("arbitrary"), name=name)(dgate, dup, wsh, x, g, dres)


def _act_bwd(dh, wd, gate, up, scale, *, name):
    T, d = dh.shape
    f = wd.shape[0]
    tm, tn = _tile(T, 512, 8), _tile(f, DFF)

    tc = _tile(tn, EPI_COLS)

    def body(dh_ref, wd_ref, g_ref, u_ref, dg_ref, du_ref):
        dhv = dh_ref[...].astype(MXU)
        for j in range(tn // tc):
            sl = slice(j * tc, (j + 1) * tc)
            da = scale * _nt(dhv, wd_ref[sl, :].astype(MXU))
            gv, uv = g_ref[:, sl].astype(F32), u_ref[:, sl].astype(F32)
            s = _sig(gv)
            dg_ref[:, sl] = (da * uv * (s * (1.0 + gv * (1.0 - s)))).astype(BF16)
            du_ref[:, sl] = (da * (gv * s)).astype(BF16)

    tile = pl.BlockSpec((tm, tn), lambda i, j: (i, j))
    act = jax.ShapeDtypeStruct((T, f), BF16)
    return pl.pallas_call(
        body, grid=(T // tm, f // tn),
        in_specs=[pl.BlockSpec((tm, d), lambda i, j: (i, 0)), pl.BlockSpec((tn, d), lambda i, j: (j, 0)), tile, tile],
        out_specs=[tile, tile], out_shape=[act, act], compiler_params=_cp("parallel", "parallel"),
        name=name)(dh, wd, gate, up)


CONV_ROWS = 64
CONV_PAD = 8


def _rows_down(ref, r0, d, cols=slice(None)):
    if r0 - d >= 0:
        return ref[pl.ds(r0 - d, CONV_ROWS), cols]
    assert r0 == 0
    v = ref[pl.ds(0, CONV_ROWS), cols]
    ri = lax.broadcasted_iota(jnp.int32, v.shape, 0)
    return jnp.where(ri >= d, pltpu.roll(v, d, 0), 0.0)


def _fold8(v):
    return jnp.sum(v.reshape(CONV_ROWS // 8, 8, v.shape[1]), axis=0)


def _taps(w_ref, views):
    acc = None
    for k, v in enumerate(views):
        t = w_ref[k:k + 1, :] * v
        acc = t if acc is None else acc + t
    return acc


def _conv_a_fwd(pp, w8, bl, s, *, name):
    tc = CA_TILE
    nb = D // tc
    bcol, ccol, vcol = slice(0, tc), slice(tc, 2 * tc), slice(2 * tc, 3 * tc)

    def body(p_ref, w_ref, o_ref):
        for r0 in range(0, s, CONV_ROWS):
            cv = [_rows_down(p_ref, r0, 2 - k, ccol) * _rows_down(p_ref, r0, 2 - k, vcol) for k in range(3)]
            o_ref[pl.ds(r0, CONV_ROWS), :] = (p_ref[pl.ds(r0, CONV_ROWS), bcol] * _taps(w_ref, cv)).astype(BF16)

    return pl.pallas_call(
        body, grid=(bl, nb),
        in_specs=[pl.BlockSpec((s, 3 * tc), lambda b, j: (b, j)), pl.BlockSpec((8, tc), lambda b, j: (0, j))],
        out_specs=pl.BlockSpec((s, tc), lambda b, j: (b, j)),
        out_shape=jax.ShapeDtypeStruct((bl * s, D), BF16), compiler_params=_cp("parallel", "parallel"),
        name=name)(pp, w8)


def _conv_a_bwd(pp, w8, dya, dpp, bl, s, *, name):
    tc = CA_TILE
    nb = D // tc
    bcol, ccol, vcol = slice(0, tc), slice(tc, 2 * tc), slice(2 * tc, 3 * tc)

    def body(p_ref, w_ref, dy_ref, dpp_in, d_ref, dw_ref, dcp):
        del dpp_in

        @pl.when(pl.program_id(1) == 0)
        def _():
            dw_ref[...] = jnp.zeros_like(dw_ref)

        dcp[pl.ds(s, CONV_PAD), :] = jnp.zeros((CONV_PAD, tc), F32)
        dw_acc = [jnp.zeros((8, tc), F32) for _ in range(3)]
        for r0 in reversed(range(0, s, CONV_ROWS)):
            rows = pl.ds(r0, CONV_ROWS)
            cs = [_rows_down(p_ref, r0, 2 - k, ccol) for k in range(3)]
            vs = [_rows_down(p_ref, r0, 2 - k, vcol) for k in range(3)]
            cv = [c_ * v_ for c_, v_ in zip(cs, vs)]
            dy = dy_ref[rows, :]
            d_ref[rows, bcol] = (dy * _taps(w_ref, cv)).astype(BF16)
            dconv = dy * p_ref[rows, bcol]
            dcp[rows, :] = dconv
            dcv = _taps(w_ref, [dcp[pl.ds(r0 + 2, CONV_ROWS), :], dcp[pl.ds(r0 + 1, CONV_ROWS), :], dconv])
            d_ref[rows, ccol] = (dcv * vs[2]).astype(BF16)
            d_ref[rows, vcol] = (dcv * cs[2]).astype(BF16)
            dw_acc = [acc + _fold8(dconv * cv_) for acc, cv_ in zip(dw_acc, cv)]
        for k in range(3):
            dw_ref[k:k + 1, :] += jnp.sum(dw_acc[k], axis=0, keepdims=True)

    wspec = pl.BlockSpec((8, tc), lambda j, b: (0, j))
    wide = pl.BlockSpec((s, 3 * tc), lambda j, b: (b, j))
    return pl.pallas_call(
        body, grid=(nb, bl),
        in_specs=[wide, wspec, pl.BlockSpec((s, tc), lambda j, b: (b, j)), pl.BlockSpec(memory_space=pl.ANY)],
        out_specs=[wide, wspec], out_shape=[jax.ShapeDtypeStruct(dpp.shape, dpp.dtype), jax.ShapeDtypeStruct((8, D), F32)],
        scratch_shapes=[pltpu.VMEM((s + CONV_PAD, tc), F32)], input_output_aliases={3: 0},
        compiler_params=_cp("parallel", "arbitrary"), name=name)(pp, w8, dya, dpp)


def _conv_ssm_fwd(pp, w8, bias, bl, s, *, name):
    tc = 256
    width = DI + 2 * NG * NS
    nb = width // tc

    def body(x_ref, w_ref, b_ref, o_ref):
        for r0 in range(0, s, CONV_ROWS):
            pre = _taps(w_ref, [_rows_down(x_ref, r0, 3 - k) for k in range(4)]) + b_ref[...]
            o_ref[pl.ds(r0, CONV_ROWS), :] = pre * _sig(pre)

    return pl.pallas_call(
        body, grid=(bl, nb),
        in_specs=[pl.BlockSpec((s, tc), lambda b, j: (b, O_XBC // tc + j)),
                  pl.BlockSpec((8, tc), lambda b, j: (0, j)), pl.BlockSpec((1, tc), lambda b, j: (0, j))],
        out_specs=pl.BlockSpec((s, tc), lambda b, j: (b, j)),
        out_shape=jax.ShapeDtypeStruct((bl * s, width), F32), compiler_params=_cp("parallel", "parallel"),
        name=name)(pp, w8, bias)


def _conv_ssm_bwd(pp, w8, bias, dxc, ch_off, dpp, bl, s, *, name):
    n = dxc.shape[1]
    tc = 256
    nb = n // tc
    o0 = ch_off // tc

    def body(x_ref, w_ref, b_ref, d_ref, dpp_in, dx_ref, dw_ref, db_ref, dp):
        del dpp_in

        @pl.when(pl.program_id(1) == 0)
        def _():
            dw_ref[...] = jnp.zeros_like(dw_ref)
            db_ref[...] = jnp.zeros_like(db_ref)

        dp[pl.ds(s, CONV_PAD), :] = jnp.zeros((CONV_PAD, tc), F32)
        dw_acc = [jnp.zeros((8, tc), F32) for _ in range(4)]
        db_acc = jnp.zeros((8, tc), F32)
        for r0 in reversed(range(0, s, CONV_ROWS)):
            rows = pl.ds(r0, CONV_ROWS)
            xs = [_rows_down(x_ref, r0, 3 - k) for k in range(4)]
            pre = _taps(w_ref, xs) + b_ref[...]
            sg = _sig(pre)
            dpre = d_ref[rows, :] * (sg * (1.0 + pre * (1.0 - sg)))
            dp[rows, :] = dpre
            dx = _taps(w_ref, [dp[pl.ds(r0 + 3 - k, CONV_ROWS), :] for k in range(3)] + [dpre])
            dx_ref[rows, :] = dx.astype(BF16)
            db_acc = db_acc + _fold8(dpre)
            dw_acc = [acc + _fold8(dpre * x_) for acc, x_ in zip(dw_acc, xs)]
        db_ref[...] += jnp.sum(db_acc, axis=0, keepdims=True)
        for k in range(4):
            dw_ref[k:k + 1, :] += jnp.sum(dw_acc[k], axis=0, keepdims=True)

    return pl.pallas_call(
        body, grid=(nb, bl),
        in_specs=[pl.BlockSpec((s, tc), lambda j, b: (b, O_XBC // tc + o0 + j)),
                  pl.BlockSpec((8, tc), lambda j, b: (0, o0 + j)), pl.BlockSpec((1, tc), lambda j, b: (0, o0 + j)),
                  pl.BlockSpec((s, tc), lambda j, b: (b, j)), pl.BlockSpec(memory_space=pl.ANY)],
        out_specs=[pl.BlockSpec((s, tc), lambda j, b: (b, O_XBC // tc + o0 + j)),
                   pl.BlockSpec((8, tc), lambda j, b: (0, j)), pl.BlockSpec((1, tc), lambda j, b: (0, j))],
        out_shape=[jax.ShapeDtypeStruct(dpp.shape, dpp.dtype), jax.ShapeDtypeStruct((8, n), F32),
                   jax.ShapeDtypeStruct((1, n), F32)],
        scratch_shapes=[pltpu.VMEM((s + CONV_PAD, tc), F32)], input_output_aliases={4: 0},
        compiler_params=_cp("parallel", "arbitrary"), name=name)(pp, w8, bias, dxc, dpp)


def _softplus(x):
    return jnp.maximum(x, 0.0) + jnp.log1p(jnp.exp(-jnp.abs(x)))


def _head_group_matrix():
    h = jnp.arange(128)[:, None]
    j = jnp.arange(NG * 128)[None, :]
    per = NH // NG
    return ((h < NH) & (j == (h // per) * 128 + h % per)).astype(F32)


def _dt_fwd(pp, bias128, *, name):
    T = pp.shape[0]
    tr = _tile(T, 1024, 8)
    per = NH // NG

    def body(x_ref, b_ref, p_ref, g_ref, t_ref):
        lane = lax.broadcasted_iota(jnp.int32, (tr, 128), 1)
        dt = jnp.where(lane < NH, _softplus(x_ref[...] + b_ref[...]), 0.0)
        g_ref[...] = _nn(dt, p_ref[...], HI)
        eye = (lax.broadcasted_iota(jnp.int32, (NH, 128), 0)
               == lax.broadcasted_iota(jnp.int32, (NH, 128), 1)).astype(F32)
        t_ref[...] = _dot(eye, dt, ((1,), (1,)), HI).reshape(NG, per, tr)

    vec = pl.BlockSpec((1, 128), lambda i: (0, 0))
    return pl.pallas_call(
        body, grid=(T // tr,),
        in_specs=[pl.BlockSpec((tr, 128), lambda i: (i, O_DT // 128)), vec,
                  pl.BlockSpec((128, NG * 128), lambda i: (0, 0))],
        out_specs=[pl.BlockSpec((tr, NG * 128), lambda i: (i, 0)), pl.BlockSpec((NG, per, tr), lambda i: (0, 0, i))],
        out_shape=[jax.ShapeDtypeStruct((T, NG * 128), F32), jax.ShapeDtypeStruct((NG, per, T), F32)],
        compiler_params=_cp("parallel"), name=name)(pp, bias128, _head_group_matrix())


def _dt_bwd(pp, bias128, ddtg, dpp, *, name):
    T = pp.shape[0]
    tr = _tile(T, 1024, 8)

    def body(x_ref, b_ref, d_ref, p_ref, dpp_in, o_ref, db_ref):
        del dpp_in

        @pl.when(pl.program_id(0) == 0)
        def _():
            db_ref[...] = jnp.zeros_like(db_ref)

        lane = lax.broadcasted_iota(jnp.int32, (tr, 128), 1)
        ddt = _dot(d_ref[...], p_ref[...], ((1,), (1,)), HI)
        dr = jnp.where(lane < NH, ddt * _sig(x_ref[...] + b_ref[...]), 0.0)
        db_ref[...] += jnp.sum(dr, axis=0, keepdims=True)
        o_ref[...] = dr.astype(BF16)

    col = pl.BlockSpec((tr, 128), lambda i: (i, O_DT // 128))
    vec = pl.BlockSpec((1, 128), lambda i: (0, 0))
    return pl.pallas_call(
        body, grid=(T // tr,),
        in_specs=[col, vec, pl.BlockSpec((tr, NG * 128), lambda i: (i, 0)), pl.BlockSpec((128, NG * 128), lambda i: (0, 0)),
                  pl.BlockSpec(memory_space=pl.ANY)],
        out_specs=[col, vec],
        out_shape=[jax.ShapeDtypeStruct(dpp.shape, dpp.dtype), jax.ShapeDtypeStruct((1, 128), F32)],
        input_output_aliases={4: 0}, compiler_params=_cp("arbitrary"),
        name=name)(pp, bias128, ddtg, _head_group_matrix(), dpp)


def _tril():
    return lax.broadcasted_iota(jnp.int32, (CH, CH), 0) >= lax.broadcasted_iota(jnp.int32, (CH, CH), 1)


def _ssd_common(dt, dtt, arow, acol):
    ri = lax.broadcasted_iota(jnp.int32, (CH, CH), 0)
    ci = lax.broadcasted_iota(jnp.int32, (CH, CH), 1)
    tril = ri >= ci
    triu = ri <= ci
    acs_col = _nn(tril.astype(F32), dt * arow, HI)
    acs_row = _nn(dtt * acol, triu.astype(F32), HI)
    return tril, triu, acs_col, acs_row


def _pair_terms(q, dt, acs_col, acs_row, tril, lo):
    ha, hb = 2 * q, 2 * q + 1
    col_a, col_b = acs_col[:, ha:ha + 1], acs_col[:, hb:hb + 1]
    row_a, row_b = acs_row[ha:ha + 1, :], acs_row[hb:hb + 1, :]
    last_a, last_b = acs_col[CH - 1:CH, ha:ha + 1], acs_col[CH - 1:CH, hb:hb + 1]
    out = dict(
        dtsel=jnp.where(lo, dt[:, ha:ha + 1], dt[:, hb:hb + 1]),
        d_a=jnp.exp(jnp.where(tril, col_a - row_a, NEG)), d_b=jnp.exp(jnp.where(tril, col_b - row_b, NEG)),
        esel=jnp.where(lo, jnp.exp(col_a), jnp.exp(col_b)),
        fsel=jnp.where(lo, jnp.exp(last_a - col_a), jnp.exp(last_b - col_b)),
        g_a=jnp.exp(last_a), g_b=jnp.exp(last_b))
    return out


def _ssd_fwd(xc, pp, dtg, dtt, arow, acol, dexp, ng, bl, s, *, name):
    nc = s // CH
    T = bl * s

    ex = SSD_EX if bl % SSD_EX == 0 else 1

    def body(*refs):
        arow_ref, acol_ref, dexp_ref, ng_ref = refs[6 * ex:6 * ex + 4]
        st_ref = refs[-1]

        @pl.when(pl.program_id(2) == 0)
        def _():
            st_ref[...] = jnp.zeros_like(st_ref)

        y_ref, yn_ref, prev_ref = refs[6 * ex + 4:6 * ex + 7]
        for e in range(ex):
            one(*refs[6 * e:6 * e + 6], arow_ref, acol_ref, dexp_ref, ng_ref,
                y_ref.at[e], yn_ref.at[e], prev_ref.at[e], st_ref.at[e])

    def one(xs_ref, bm_ref, cm_ref, z_ref, dt_ref, dtt_ref, arow_ref, acol_ref, dexp_ref, ng_ref,
            y_ref, yn_ref, prev_ref, st_ref):
        dt = dt_ref[...]
        tril, _, acs_col, acs_row = _ssd_common(dt, dtt_ref[...], -jnp.exp(arow_ref[...]), -jnp.exp(acol_ref[...]))
        bm, cm = bm_ref[...].astype(MXU), cm_ref[...].astype(MXU)
        cb = _nt(cm, bm)
        lo = lax.broadcasted_iota(jnp.int32, (CH, 128), 1) < HD
        sub_lo = lax.broadcasted_iota(jnp.int32, (128, NS), 0) < HD
        for q in range(4):
            t = _pair_terms(q, dt, acs_col, acs_row, tril, lo)
            x = xs_ref[:, 128 * q:128 * (q + 1)]
            xd = x * t["dtsel"]
            y = (_nn((cb * t["d_a"]).astype(MXU), jnp.where(lo, xd, 0.0).astype(MXU))
                 + _nn((cb * t["d_b"]).astype(MXU), jnp.where(lo, 0.0, xd).astype(MXU)))
            prev = st_ref[q]
            prev_ref[q] = prev
            y = y + t["esel"] * _nt(cm, prev.astype(MXU))
            st_ref[q] = prev * jnp.where(sub_lo, t["g_a"], t["g_b"]) + _tn((xd * t["fsel"]).astype(MXU), bm)
            y_ref[:, 128 * q:128 * (q + 1)] = y + dexp_ref[:, 128 * q:128 * (q + 1)] * x
        zv = z_ref[...]
        yg = y_ref[...] * (zv * _sig(zv))
        r = lax.rsqrt(jnp.mean(yg * yg, axis=-1, keepdims=True) + EPS)
        yn_ref[...] = (yg * r * ng_ref[...]).astype(BF16)

    def row(e, width, off_blocks):
        return pl.BlockSpec((CH, width), lambda g, b, c: ((b * ex + e) * nc + c, off_blocks + g))

    per_ex_in = [[row(e, GW, 0), row(e, NS, DI // NS), row(e, NS, DI // NS + NG), row(e, GW, O_Z // GW), row(e, 128, 0),
                  pl.BlockSpec((None, 8, CH), lambda g, b, c, e=e: (g, 0, (b * ex + e) * nc + c))] for e in range(ex)]
    by_example = pl.BlockSpec((ex, CH, GW), lambda g, b, c: (b, c, g))
    y, yn, prev = pl.pallas_call(
        body, grid=(NG, bl // ex, nc),
        in_specs=sum(per_ex_in, []) + [pl.BlockSpec((1, 128), lambda g, b, c: (0, g)),
                                       pl.BlockSpec((None, 8, 1), lambda g, b, c: (g, 0, 0)),
                                       pl.BlockSpec((1, GW), lambda g, b, c: (0, g)),
                                       pl.BlockSpec((1, GW), lambda g, b, c: (0, g))],
        out_specs=[by_example, by_example,
                   pl.BlockSpec((ex, None, 4, 128, NS), lambda g, b, c: (b, c, g, 0, 0))],
        out_shape=[jax.ShapeDtypeStruct((bl, s, DI), F32), jax.ShapeDtypeStruct((bl, s, DI), BF16),
                   jax.ShapeDtypeStruct((bl, nc, 16, 128, NS), F32)],
        scratch_shapes=[pltpu.VMEM((ex, 4, 128, NS), F32)],
        compiler_params=_cp("parallel", "parallel", "arbitrary"), name=name,
    )(*([xc, xc, xc, pp, dtg, dtt] * ex), arow, acol, dexp, ng)
    return y.reshape(T, DI), yn.reshape(T, DI), prev.reshape(bl * nc, 16, 128, NS)


def _ssd_bwd(dyn, y, xc, pp, dtg, dtt, arow, acol, dexp, ng, prev, dpp, bl, s, *, name):
    nc = s // CH
    T = bl * s

    def rsum(v):
        return jnp.sum(v, axis=1, keepdims=True)

    def asum(v):
        return jnp.sum(jnp.sum(v, axis=0, keepdims=True), axis=1, keepdims=True)

    ex = 1

    def body(*refs):
        shared = refs[9 * ex:9 * ex + 4]
        dz_ref, dxs_ref, db_ref, dc_ref, ddt_ref, dng_ref, dd_ref, dal_ref = refs[9 * ex + 5:9 * ex + 13]
        dst_ref = refs[-1]

        @pl.when((pl.program_id(1) == 0) & (pl.program_id(2) == 0))
        def _():
            dng_ref[...] = jnp.zeros_like(dng_ref)
            dd_ref[...] = jnp.zeros_like(dd_ref)
            dal_ref[...] = jnp.zeros_like(dal_ref)

        @pl.when(pl.program_id(2) == 0)
        def _():
            dst_ref[...] = jnp.zeros_like(dst_ref)

        for e in range(ex):
            one(*refs[9 * e:9 * e + 8], *shared, refs[9 * e + 8], dz_ref.at[e], dxs_ref.at[e], db_ref.at[e],
                dc_ref.at[e], ddt_ref.at[e], dng_ref, dd_ref, dal_ref, dst_ref.at[e])

    def one(dyn_ref, y_ref, xs_ref, bm_ref, cm_ref, z_ref, dt_ref, dtt_ref, arow_ref, acol_ref, dexp_ref, ng_ref,
            prev_ref, dz_ref, dxs_ref, db_ref, dc_ref, ddt_ref, dng_ref, dd_ref, dal_ref, dst_ref):
        yv, zv, xsv, dexp_v = y_ref[...], z_ref[...], xs_ref[...], dexp_ref[...]
        sz = _sig(zv)
        silu = zv * sz
        yg = yv * silu
        r = lax.rsqrt(jnp.mean(yg * yg, axis=-1, keepdims=True) + EPS)
        yh = yg * r
        dynv = dyn_ref[...]
        dng_ref[...] += jnp.sum(dynv * yh, axis=0, keepdims=True)
        dyh = dynv * ng_ref[...]
        dyg = r * (dyh - yh * jnp.mean(dyh * yh, axis=-1, keepdims=True))
        dz_ref[...] = (dyg * yv * (sz * (1.0 + zv * (1.0 - sz)))).astype(BF16)
        dy_all = dyg * silu
        dd_ref[...] += jnp.sum(dy_all * xsv, axis=0, keepdims=True)

        dt = dt_ref[...]
        arow_v = -jnp.exp(arow_ref[...])
        tril, triu, acs_col, acs_row = _ssd_common(dt, dtt_ref[...], arow_v, -jnp.exp(acol_ref[...]))
        bm, cm = bm_ref[...].astype(MXU), cm_ref[...].astype(MXU)
        cb = _nt(cm, bm)
        lane = lax.broadcasted_iota(jnp.int32, (CH, 128), 1)
        is_last = lax.broadcasted_iota(jnp.int32, (CH, 128), 0) == CH - 1
        lo = lane < HD
        sub_lo = lax.broadcasted_iota(jnp.int32, (128, NS), 0) < HD
        dcb = jnp.zeros((CH, CH), F32)
        dc_acc = jnp.zeros((CH, NS), F32)
        db_acc = jnp.zeros((CH, NS), F32)
        dacs = jnp.zeros((CH, 128), F32)
        ddtx = jnp.zeros((CH, 128), F32)
        csum = jnp.zeros((8, CH), F32)
        sub8 = lax.broadcasted_iota(jnp.int32, (8, CH), 0)
        for q in range(4):
            ha, hb = 2 * q, 2 * q + 1
            sl = slice(128 * q, 128 * (q + 1))
            t = _pair_terms(q, dt, acs_col, acs_row, tril, lo)
            x, dy = xsv[:, sl], dy_all[:, sl]
            xd = x * t["dtsel"]
            xd_m = xd.astype(MXU)
            dy_lo, dy_hi = jnp.where(lo, dy, 0.0).astype(MXU), jnp.where(lo, 0.0, dy).astype(MXU)
            m_a, m_b = cb * t["d_a"], cb * t["d_b"]
            prev_m = prev_ref[q].astype(MXU)
            dnext = dst_ref[q]
            dnext_m = dnext.astype(MXU)
            bds = _nt(bm, dnext_m)
            dxd = _tn(m_a.astype(MXU), dy_lo) + _tn(m_b.astype(MXU), dy_hi) + t["fsel"] * bds
            dye_m = (dy * t["esel"]).astype(MXU)
            dst_ref[q] = dnext * jnp.where(sub_lo, t["g_a"], t["g_b"]) + _tn(dye_m, cm)
            dm_a, dm_b = _nt(dy_lo, xd_m), _nt(dy_hi, xd_m)
            dcb = dcb + dm_a * t["d_a"] + dm_b * t["d_b"]
            g_a, g_b = dm_a * m_a, dm_b * m_b
            csum = (csum + jnp.where(sub8 == ha, jnp.sum(g_a, axis=0, keepdims=True), 0.0)
                    + jnp.where(sub8 == hb, jnp.sum(g_b, axis=0, keepdims=True), 0.0))
            tf = t["fsel"] * xd * bds
            tyf = dy * (t["esel"] * _nt(cm, prev_m)) - tf
            dpp = dnext * prev_ref[q]
            ea = asum(jnp.where(lo, tf, 0.0)) + t["g_a"] * asum(jnp.where(sub_lo, dpp, 0.0))
            eb = asum(jnp.where(lo, 0.0, tf)) + t["g_b"] * asum(jnp.where(sub_lo, 0.0, dpp))
            ra = rsum(g_a + jnp.where(lo, tyf, 0.0)) + jnp.where(is_last, ea, 0.0)
            rb = rsum(g_b + jnp.where(lo, 0.0, tyf)) + jnp.where(is_last, eb, 0.0)
            dacs = dacs + jnp.where(lane == ha, ra, 0.0) + jnp.where(lane == hb, rb, 0.0)
            tx = dxd * x
            ddtx = (ddtx + jnp.where(lane == ha, rsum(jnp.where(lo, tx, 0.0)), 0.0)
                    + jnp.where(lane == hb, rsum(jnp.where(lo, 0.0, tx)), 0.0))
            dxs_ref[:, sl] = dxd * t["dtsel"] + dexp_v[:, sl] * dy
            dc_acc = dc_acc + _nn(dye_m, prev_m)
            db_acc = db_acc + _nn((xd * t["fsel"]).astype(MXU), dnext_m)
        dcb_m = dcb.astype(MXU)
        dc_ref[...] = dc_acc + _nn(dcb_m, bm)
        db_ref[...] = db_acc + _tn(dcb_m, cm)
        dacs = dacs - jnp.concatenate([csum, jnp.zeros((CH - 8, CH), F32)], axis=0).T
        dla = _nn(triu.astype(F32), dacs, HI)
        ddt_ref[...] = dla * arow_v + ddtx
        dal_ref[...] += jnp.sum(dla * dt, axis=0, keepdims=True) * arow_v

    def row(e, width, off_blocks):
        return pl.BlockSpec((CH, width), lambda g, b, c: ((b * ex + e) * nc + nc - 1 - c, off_blocks + g))

    per_ex_in = [[row(e, GW, 0), row(e, GW, 0), row(e, GW, 0), row(e, NS, DI // NS), row(e, NS, DI // NS + NG),
                  row(e, GW, O_Z // GW), row(e, 128, 0),
                  pl.BlockSpec((None, 8, CH), lambda g, b, c, e=e: (g, 0, (b * ex + e) * nc + nc - 1 - c)),
                  pl.BlockSpec((None, 4, 128, NS), lambda g, b, c, e=e: ((b * ex + e) * nc + nc - 1 - c, g, 0, 0))]
                 for e in range(ex)]

    def by_example(width, off_blocks):
        return pl.BlockSpec((ex, CH, width), lambda g, b, c: (b, nc - 1 - c, off_blocks + g))

    gvec = pl.BlockSpec((1, GW), lambda g, b, c: (0, g))
    hvec = pl.BlockSpec((1, 128), lambda g, b, c: (0, g))
    out = pl.pallas_call(
        body, grid=(NG, bl // ex, nc),
        in_specs=sum(per_ex_in, []) + [hvec, pl.BlockSpec((None, 8, 1), lambda g, b, c: (g, 0, 0)), gvec, gvec,
                                       pl.BlockSpec(memory_space=pl.ANY)],
        out_specs=[by_example(GW, O_Z // GW), by_example(GW, 0), by_example(NS, 0), by_example(NS, 0),
                   by_example(128, 0), gvec, gvec, hvec],
        input_output_aliases={9 * ex + 4: 0},
        out_shape=[jax.ShapeDtypeStruct((bl, s, dpp.shape[1]), dpp.dtype), jax.ShapeDtypeStruct((bl, s, DI), F32),
                   jax.ShapeDtypeStruct((bl, s, NG * NS), F32), jax.ShapeDtypeStruct((bl, s, NG * NS), F32),
                   jax.ShapeDtypeStruct((bl, s, NG * 128), F32), jax.ShapeDtypeStruct((1, DI), F32),
                   jax.ShapeDtypeStruct((1, DI), F32), jax.ShapeDtypeStruct((1, NG * 128), F32)],
        scratch_shapes=[pltpu.VMEM((ex, 4, 128, NS), F32)],
        compiler_params=_cp("arbitrary", "arbitrary", "arbitrary"), name=name,
    )(*([dyn, y, xc, xc, xc, pp, dtg, dtt, prev] * ex), arow, acol, dexp, ng, dpp.reshape(bl, s, dpp.shape[1]))
    return (out[0].reshape(T, -1), out[1].reshape(T, DI), out[2].reshape(T, NG * NS), out[3].reshape(T, NG * NS),
            out[4].reshape(T, NG * 128), out[5], out[6], out[7])


def _merge_fwd(pp, ya, yb, *, name):
    T = ya.shape[0]
    tr = _tile(T, 512, 8)

    def body(ga_ref, gb_ref, ya_ref, yb_ref, o_ref):
        o_ref[...] = (_sig(ga_ref[...]) * ya_ref[...].astype(F32)
                      + _sig(gb_ref[...]) * yb_ref[...].astype(F32)).astype(BF16)

    row = pl.BlockSpec((tr, D), lambda i: (i, 0))
    return pl.pallas_call(
        body, grid=(T // tr,),
        in_specs=[pl.BlockSpec((tr, D), lambda i: (i, O_GA // D)), pl.BlockSpec((tr, D), lambda i: (i, O_GB // D)),
                  row, row],
        out_specs=row, out_shape=jax.ShapeDtypeStruct((T, D), BF16), compiler_params=_cp("parallel"),
        name=name)(pp, pp, ya, yb)


def _merge_bwd(pp, ya, yb, dh, w_mix, *, name):
    T = ya.shape[0]
    tr = _tile(T, 512, 8)
    assert O_GB == O_GA + D and O_GA % (2 * D) == 0

    def body(g_ref, ya_ref, yb_ref, dh_ref, w_ref, dya_ref, dyb_ref, dg_ref):
        dmv = _nt(dh_ref[...].astype(MXU), w_ref[...].astype(MXU))
        sa, sb = _sig(g_ref[:, :D]), _sig(g_ref[:, D:])
        dya_ref[...] = (dmv * sa).astype(BF16)
        dyb_ref[...] = (dmv * sb).astype(BF16)
        dg_ref[:, :D] = (dmv * ya_ref[...].astype(F32) * (sa * (1.0 - sa))).astype(BF16)
        dg_ref[:, D:] = (dmv * yb_ref[...].astype(F32) * (sb * (1.0 - sb))).astype(BF16)

    row = pl.BlockSpec((tr, D), lambda i: (i, 0))
    gates = pl.BlockSpec((tr, 2 * D), lambda i: (i, O_GA // (2 * D)))
    act = jax.ShapeDtypeStruct((T, D), BF16)
    return pl.pallas_call(
        body, grid=(T // tr,),
        in_specs=[gates, row, row, row, pl.BlockSpec(w_mix.shape, lambda i: (0, 0))], out_specs=[row, row, gates],
        out_shape=[act, act, jax.ShapeDtypeStruct((T, NPP), BF16)], compiler_params=_cp("parallel"),
        name=name)(pp, ya, yb, dh, w_mix)


def _softmax_rows(sc):
    e = jnp.exp(sc - jnp.max(sc, axis=-1, keepdims=True))
    return e / jnp.sum(e, axis=-1, keepdims=True)


def _attn_fwd(q, kv, bl, s, *, name):
    m = kv.shape[0] // bl
    tq = _tile(s, 2048)
    nq = s // tq
    scale = 1.0 / math.sqrt(XD)

    def body(q_ref, k_ref, v_ref, o_ref):
        p = _softmax_rows(_nt(q_ref[...], k_ref[...]) * scale)
        o_ref[...] = _nn(p.astype(MXU), v_ref[...]).astype(BF16)

    qspec = pl.BlockSpec((tq, XD), lambda b, h, i: (b * nq + i, h))
    return pl.pallas_call(
        body, grid=(bl, XH, nq),
        in_specs=[qspec, pl.BlockSpec((m, XD), lambda b, h, i: (b, h)),
                  pl.BlockSpec((m, XD), lambda b, h, i: (b, XH + h))],
        out_specs=qspec, out_shape=jax.ShapeDtypeStruct((bl * s, D), BF16),
        compiler_params=_cp("parallel", "parallel", "parallel"), name=name)(q, kv, kv)


def _attn_bwd(q, kv, do, bl, s, *, name):
    m = kv.shape[0] // bl
    tq = _tile(s, 2048)
    nq = s // tq
    scale = 1.0 / math.sqrt(XD)

    def body(q_ref, k_ref, v_ref, do_ref, dq_ref, dk_ref, dv_ref):
        @pl.when(pl.program_id(2) == 0)
        def _():
            dk_ref[...] = jnp.zeros_like(dk_ref)
            dv_ref[...] = jnp.zeros_like(dv_ref)

        qv, kvv, vv, dov = q_ref[...], k_ref[...], v_ref[...], do_ref[...]
        p = _softmax_rows(_nt(qv, kvv) * scale)
        dp = _nt(dov, vv)
        ds = (p * (dp - jnp.sum(dp * p, axis=-1, keepdims=True)) * scale).astype(MXU)
        dq_ref[...] = _nn(ds, kvv).astype(BF16)
        dk_ref[...] += _tn(ds, qv)
        dv_ref[...] += _tn(p.astype(MXU), dov)

    qspec = pl.BlockSpec((tq, XD), lambda b, h, i: (b * nq + i, h))
    kspec = pl.BlockSpec((m, XD), lambda b, h, i: (b, h))
    return pl.pallas_call(
        body, grid=(bl, XH, nq),
        in_specs=[qspec, kspec, pl.BlockSpec((m, XD), lambda b, h, i: (b, XH + h)), qspec],
        out_specs=[qspec, kspec, kspec],
        out_shape=[jax.ShapeDtypeStruct((bl * s, D), BF16), jax.ShapeDtypeStruct((bl * m, D), F32),
                   jax.ShapeDtypeStruct((bl * m, D), F32)],
        compiler_params=_cp("parallel", "parallel", "arbitrary"), name=name)(q, kv, kv, do)


def _row_tile(r, c, max_elems=512 * 1024, align=16):
    best = None
    for t in range(align, r + 1, align):
        if r % t == 0 and t * c <= max_elems:
            best = t
    return best if best is not None else r


def _addn(a, others, *, name, also_bf16=False):
    r, c = a.shape
    tr = _row_tile(r, c)
    n = len(others)

    def body(*refs):
        acc = refs[0][...].astype(F32)
        for o_ref in refs[1:1 + n]:
            acc = acc + o_ref[...].astype(F32)
        refs[1 + n][...] = acc
        if also_bf16:
            refs[2 + n][...] = acc.astype(BF16)

    spec = pl.BlockSpec((tr, c), lambda i: (i, 0))
    shapes = [jax.ShapeDtypeStruct((r, c), F32)] + ([jax.ShapeDtypeStruct((r, c), BF16)] if also_bf16 else [])
    out = pl.pallas_call(
        body, grid=(r // tr,), in_specs=[spec] * (1 + n), out_specs=[spec] * len(shapes), out_shape=shapes,
        compiler_params=_cp("parallel"), name=name)(a, *others)
    return out if also_bf16 else out[0]


def _sum_leading(a, *, name):
    n, r, c = a.shape

    def body(a_ref, o_ref):
        acc = a_ref[0]
        for i in range(1, n):
            acc = acc + a_ref[i]
        o_ref[...] = acc

    return pl.pallas_call(body, out_shape=jax.ShapeDtypeStruct((r, c), F32), name=name)(a)


def _adamw_math(wv, gv, mv, vv):
    m2 = ADAM_B1 * mv + (1.0 - ADAM_B1) * gv
    v2 = ADAM_B2 * vv + (1.0 - ADAM_B2) * (gv * gv)
    m_hat = m2 / (1.0 - ADAM_B1 ** ADAM_STEP)
    v_hat = v2 / (1.0 - ADAM_B2 ** ADAM_STEP)
    return -ADAM_LR * (m_hat / (jnp.sqrt(v_hat) + ADAM_EPS) + ADAM_WD * wv), m2, v2


def _adamw(w, g, m, v, *, name):
    r, c = w.shape
    tr = _row_tile(r, c, align=8)

    def body(w_ref, g_ref, m_ref, v_ref, d_ref, mo_ref, vo_ref):
        d_ref[...], mo_ref[...], vo_ref[...] = _adamw_math(w_ref[...], g_ref[...], m_ref[...], v_ref[...])

    spec = pl.BlockSpec((tr, c), lambda i: (i, 0))
    shp = jax.ShapeDtypeStruct((r, c), F32)
    return pl.pallas_call(
        body, grid=(r // tr,), in_specs=[spec] * 4, out_specs=[spec] * 3, out_shape=[shp] * 3,
        compiler_params=_cp("parallel"), name=name)(w, g, m, v)


def _adamw_halves(w, g_mine, g_other, m, v, c, *, name):
    _, r, cols = w.shape
    h = r // 2
    tr = _row_tile(h, cols, align=8)
    nh = h // tr

    def body(c_ref, w_ref, gm_ref, go_ref, m_ref, v_ref, g_ref, d_ref, mo_ref, vo_ref):
        gv = jnp.where(pl.program_id(0) // nh == c_ref[0], gm_ref[...], go_ref[...])
        g_ref[...] = gv
        d_ref[...], mo_ref[...], vo_ref[...] = _adamw_math(w_ref[...], gv, m_ref[...], v_ref[...])

    full = pl.BlockSpec((None, tr, cols), lambda i, c_: (0, i, 0))
    half = pl.BlockSpec((tr, cols), lambda i, c_: (i % nh, 0))
    shp = jax.ShapeDtypeStruct((1, r, cols), F32)
    return pl.pallas_call(
        body,
        grid_spec=pltpu.PrefetchScalarGridSpec(num_scalar_prefetch=1, grid=(2 * nh,),
                                               in_specs=[full, half, half, full, full], out_specs=[full] * 4),
        out_shape=[shp] * 4, compiler_params=_cp("parallel"), name=name,
    )(jnp.reshape(c, (1,)).astype(jnp.int32), w, g_mine, g_other, m, v)


def _flip(i, d):
    return 1 - i if d else i


def _comm(name, ins, out_shapes, n_remote, n_local, plan, aliases=None):
    n_in, n_out = len(ins), len(out_shapes)

    def body(*refs):
        in_refs, out_refs = refs[:n_in], refs[n_in:n_in + n_out]
        send_sems, recv_sems = refs[n_in + n_out], refs[n_in + n_out + 1]
        x, y, c = lax.axis_index("x"), lax.axis_index("y"), lax.axis_index("c")
        remote, local = plan(in_refs, out_refs, x, y, c)
        assert len(remote) == n_remote and len(local) == n_local
        copies = []
        if n_local:
            loc_sems = refs[n_in + n_out + 2]
            copies += [pltpu.make_async_copy(s_, d_, loc_sems.at[i]) for i, (s_, d_) in enumerate(local)]
        copies += [pltpu.make_async_remote_copy(src_ref=s_, dst_ref=d_, send_sem=send_sems.at[i],
                                                recv_sem=recv_sems.at[i], device_id=dev, device_id_type=MESH)
                   for i, (s_, d_, dev) in enumerate(remote)]
        for cp in copies:
            cp.start()
        for cp in copies:
            cp.wait()

    hbm = pl.BlockSpec(memory_space=pl.ANY)
    scratch = [pltpu.SemaphoreType.DMA((n_remote,)), pltpu.SemaphoreType.DMA((n_remote,))]
    if n_local:
        scratch.append(pltpu.SemaphoreType.DMA((n_local,)))
    return pl.pallas_call(
        body, in_specs=[hbm] * n_in, out_specs=[hbm] * n_out, out_shape=out_shapes, scratch_shapes=scratch,
        input_output_aliases=aliases or {}, compiler_params=pltpu.CompilerParams(has_side_effects=True),
        name=name)(*ins)


HBM_SPEC = pl.BlockSpec(memory_space=pltpu.HBM)
SEM_SPEC = pl.BlockSpec(memory_space=pltpu.SEMAPHORE)
DATAFLOW = pltpu.SideEffectType.DATAFLOW_SIDE_EFFECTING


def _remote_copies(plan, srcs, lands, send_sems, recv_sems, n_copies):
    x, y, c = lax.axis_index("x"), lax.axis_index("y"), lax.axis_index("c")
    copies = plan(srcs, lands, x, y, c)
    assert len(copies) == n_copies
    return [pltpu.make_async_remote_copy(src_ref=s_, dst_ref=d_, send_sem=send_sems.at[i], recv_sem=recv_sems.at[i],
                                         device_id=dev, device_id_type=MESH) for i, (s_, d_, dev) in enumerate(copies)]


def _split_start(name, srcs, lands, n_copies, plan, after=None):
    ns, nb = len(srcs), len(srcs) + len(lands)
    n_after = 0 if after is None else 1
    n_in = nb + n_after

    def body(*refs):
        for cp in _remote_copies(plan, refs[:ns], refs[ns:nb], refs[n_in], refs[n_in + 1], n_copies):
            cp.start()
        refs[-1][...] = jnp.zeros_like(refs[-1])

    arrays = [pltpu.with_memory_space_constraint(a_, pltpu.HBM) for a_ in list(srcs) + list(lands)]
    out = pl.pallas_call(
        body, name=name,
        out_shape=(pltpu.SemaphoreType.DMA((n_copies,)), pltpu.SemaphoreType.DMA((n_copies,)),
                   *[pltpu.HBM(a_.shape, a_.dtype) for a_ in arrays], jax.ShapeDtypeStruct((8, 128), F32)),
        in_specs=[HBM_SPEC] * nb + [pl.BlockSpec(memory_space=pl.ANY)] * n_after,
        out_specs=(SEM_SPEC, SEM_SPEC, *[HBM_SPEC] * nb, pl.BlockSpec(memory_space=pltpu.VMEM)),
        input_output_aliases={i: 2 + i for i in range(nb)},
        compiler_params=pltpu.CompilerParams(has_side_effects=DATAFLOW))(*arrays, *([after] * n_after))
    return (out[0], out[1], list(out[2:2 + nb])), out[-1]


def _split_wait(name, handle, ns, n_copies, plan, after):
    send_sems, recv_sems, bufs = handle
    nb = len(bufs)

    def body(*refs):
        for cp in _remote_copies(plan, refs[:ns], refs[ns:nb], refs[nb], refs[nb + 1], n_copies):
            cp.wait_send()
            cp.wait_recv()

    out = pl.pallas_call(
        body, name=name, out_shape=[pltpu.HBM(b_.shape, b_.dtype) for b_ in bufs],
        in_specs=[HBM_SPEC] * nb + [SEM_SPEC, SEM_SPEC, pl.BlockSpec(memory_space=pl.ANY)],
        out_specs=[HBM_SPEC] * nb, input_output_aliases={i: i for i in range(nb)},
        compiler_params=pltpu.CompilerParams(has_side_effects=DATAFLOW))(*bufs, send_sems, recv_sems, after)
    return list(out[ns:])


def _gather_start(shards, tag, after=None):
    n = len(shards)
    lands = [lax.empty((4,) + s.shape, s.dtype) for s in shards]

    def plan(srcs, dsts, x, y, c):
        k = 2 * x + y
        copies = []
        for w_ref, o_ref in zip(srcs, dsts):
            h = w_ref.shape[0] // 2
            rows = pl.ds(c * h, h)
            copies += [(w_ref.at[rows], o_ref.at[k, rows], (_flip(x, dx), _flip(y, dy), c)) for dx, dy in CHIP_FLIPS]
        return copies

    handle, token = _split_start(f"gather_{tag}_start", shards, lands, 3 * n, plan, after)
    return (handle, plan, n), token


def _gather_wait(started, after, tag):
    handle, plan, n = started
    return _split_wait(f"gather_{tag}_wait", handle, n, 3 * n, plan, after)


def _gather_d2d(lands, before, tag):
    n = len(lands)

    def plan_d2d(in_refs, out_refs, x, y, c):
        remote = []
        for o_ref in out_refs:
            h = o_ref.shape[1] // 2
            for dx, dy in CHIP_FLIPS:
                half = o_ref.at[2 * _flip(x, dx) + _flip(y, dy), pl.ds(c * h, h)]
                remote.append((half, half, (x, y, 1 - c)))
        return remote, []

    return _comm(f"gather_{tag}_d2d", list(lands) + list(before),
                 [jax.ShapeDtypeStruct(l_.shape, l_.dtype) for l_ in lands], 3 * n, 0, plan_d2d,
                 aliases={i: i for i in range(n)})


def _pair_plan(in_refs, out_refs, x, y, c):
    return [(i_, o_, (x, y, 1 - c)) for i_, o_ in zip(in_refs, out_refs)], []


def _rs_start(grads, tag):
    n = len(grads)
    c = lax.axis_index("c")
    def rows(g, start, h):
        return g.rows(start, h) if isinstance(g, _WInGrad) else lax.dynamic_slice_in_dim(g, start, h, axis=1)

    halves = [g.shape[1] // 2 for g in grads]
    mine = [rows(g, c * h, h) for g, h in zip(grads, halves)]
    send_a = [rows(g, (1 - c) * h, h).astype(BF16) for g, h in zip(grads, halves)]
    recv_a = _comm(f"rs_pair_{tag}", send_a, [jax.ShapeDtypeStruct(s.shape, BF16) for s in send_a], n, 0, _pair_plan)
    pair, pair_b = [], []
    for i, (mi, ra) in enumerate(zip(mine, recv_a)):
        four, h, cols = mi.shape
        p32, p16 = _addn(mi.reshape(four * h, cols), [ra.reshape(four * h, cols)], name=f"rs_pair_sum_{tag}_{i}",
                         also_bf16=True)
        pair.append(p32.reshape(four, h, cols))
        pair_b.append(p16.reshape(four, h, cols))

    def plan(srcs, dsts, x, y, c_):
        copies = []
        for i_, o_ in zip(srcs, dsts):
            for j, (dx, dy) in enumerate(CHIP_FLIPS):
                fx, fy = _flip(x, dx), _flip(y, dy)
                copies.append((i_.at[2 * fx + fy], o_.at[j], (fx, fy, c_)))
        return copies

    lands = [lax.empty((3,) + p.shape[1:], BF16) for p in pair_b]
    handle, token = _split_start(f"rs_chips_{tag}_start", pair_b, lands, 3 * n, plan)
    return (handle, plan, n, pair), token


def _rs_finish(started, after, tag):
    handle, plan, n, pair = started
    recv_b = _split_wait(f"rs_chips_{tag}_wait", handle, n, 3 * n, plan, after)
    k = 2 * lax.axis_index("x") + lax.axis_index("y")
    tot = [_addn(lax.dynamic_index_in_dim(p, k, 0, keepdims=False), [rb[0], rb[1], rb[2]],
                 name=f"rs_chip_sum_{tag}_{i}") for i, (p, rb) in enumerate(zip(pair, recv_b))]
    other = _comm(f"rs_halves_{tag}", tot, [jax.ShapeDtypeStruct(t.shape, F32) for t in tot], n, 0, _pair_plan)
    return tot, other


def _gather_all(vec, *, name, before=()):
    out = jax.ShapeDtypeStruct((8,) + vec.shape, vec.dtype)

    def plan(in_refs, out_refs, x, y, c):
        me = 4 * x + 2 * y + c
        remote = [(in_refs[0], out_refs[0].at[me], (_flip(x, dx), _flip(y, dy), _flip(c, dc)))
                  for dx in (0, 1) for dy in (0, 1) for dc in (0, 1) if (dx, dy, dc) != (0, 0, 0)]
        return remote, [(in_refs[0], out_refs[0].at[me])]

    return _comm(name, [vec] + list(before), [out], 7, 1, plan)[0]


def _pack(parts):
    flat = [p.reshape(-1).astype(F32) for p in parts]
    total = sum(f.shape[0] for f in flat)
    n = -(-total // 1024) * 128
    vec = jnp.concatenate(flat + [jnp.zeros((8 * n - total,), F32)]).reshape(8, n)
    offs, o = [], 0
    for f in flat:
        offs.append((o, f.shape[0]))
        o += f.shape[0]
    return vec, offs


def _unpack(vec, offs, shapes):
    flat = vec.reshape(-1)
    return [flat[o:o + n].reshape(s) for (o, n), s in zip(offs, shapes)]


BIG = (("ffn1_w_gate_up", "col"), ("ffn1_w_down", "row"), ("w_in", "col"), ("w_out_a", "row"), ("w_out_ssm", "row"),
       ("w_mix_out", "row"), ("w_q", "row"), ("w_kv", "col"), ("w_o_x", "row"), ("ffn2_w_gate_up", "col"),
       ("ffn2_w_down", "row"))
SMALL = ("ffn1_norm", "mix_norm", "conv_a_w", "ssm_conv_w", "ssm_conv_b", "ssm_dt_bias", "ssm_a_log", "ssm_d",
         "ssm_norm", "xattn_norm", "mem_norm", "ffn2_norm", "final_norm")
WEIGHTS = ("ffn1_norm", "ffn1_w_gate_up", "ffn1_w_down", "mix_norm", "w_in", "conv_a_w", "w_out_a", "ssm_conv_w",
           "ssm_conv_b", "ssm_dt_bias", "ssm_a_log", "ssm_d", "ssm_norm", "w_out_ssm", "w_mix_out", "xattn_norm",
           "mem_norm", "w_q", "w_kv", "w_o_x", "ffn2_norm", "ffn2_w_gate_up", "ffn2_w_down", "final_norm")


GATHER_GROUPS = (("a", ("ffn1_w_gate_up",)), ("b", ("ffn1_w_down", "w_in")),
                 ("c", ("w_out_a", "w_out_ssm", "w_mix_out", "w_q", "w_kv", "w_o_x", "ffn2_w_gate_up", "ffn2_w_down")))


def _place_own(land, own, k, *, name):
    four, r, cols = land.shape
    tr = _row_tile(r, cols)

    def body(k_ref, own_ref, land_in, o_ref):
        del k_ref, land_in
        o_ref[...] = own_ref[...]

    return pl.pallas_call(
        body,
        grid_spec=pltpu.PrefetchScalarGridSpec(
            num_scalar_prefetch=1, grid=(r // tr,),
            in_specs=[pl.BlockSpec((tr, cols), lambda i, k_: (i, 0)), pl.BlockSpec(memory_space=pl.ANY)],
            out_specs=pl.BlockSpec((None, tr, cols), lambda i, k_: (k_[0], i, 0))),
        out_shape=jax.ShapeDtypeStruct(land.shape, land.dtype), input_output_aliases={2: 0},
        compiler_params=_cp("parallel"), name=name)(jnp.reshape(k, (1,)).astype(jnp.int32), own, land)


def _full_weight(land, own, kind, k, *, name):
    land = _place_own(land, own, k, name=name)
    four, r, cols = land.shape
    if kind == "row":
        return land.reshape(four * r, cols)
    return jnp.transpose(land, (1, 0, 2)).reshape(r, four * cols)


class _GatheredWeights:
    def __init__(self, shards32, k, after):
        first = GATHER_GROUPS[0][1]
        self.shards, self.k = {n: shards32[n].astype(BF16)[0] for n in first}, k
        self.full = {}
        self.n_done = 0
        self.started, token = self._start(0, after)
        self.token = token[0, 0]
        self.shards.update({n: (w + token[0, 0]).astype(BF16)[0] for n, w in shards32.items() if n not in first})
        self.after = jnp.stack([self.shards[n][0, 0] for n in shards32 if n not in first]).astype(F32).reshape(1, -1)

    def _start(self, gi, after):
        tag, names = GATHER_GROUPS[gi]
        return _gather_start([self.shards[n] for n in names], tag, after)

    def mark(self, value):
        self.after = value

    def __getitem__(self, name):
        if name not in self.full:
            tag, names = GATHER_GROUPS[self.n_done]
            assert name in names, (name, tag)
            lands = _gather_wait(self.started, self.after, tag)
            before = []
            if self.n_done + 1 < len(GATHER_GROUPS):
                self.started, token = self._start(self.n_done + 1, lands[0])
                before = [token]
            lands = _gather_d2d(lands, before, tag)
            for n, land in zip(names, lands):
                if n == "w_in":
                    self.full[n] = _pad_w_in_shards(_place_own(land, self.shards[n], self.k, name=f"own_{n}"))
                elif n.endswith("w_gate_up"):
                    self.full[n] = _place_own(land, self.shards[n], self.k, name=f"own_{n}")
                else:
                    self.full[n] = _full_weight(land, self.shards[n], dict(BIG)[n], self.k, name=f"own_{n}")
            self.n_done += 1
        return self.full[name]


def _shard_major(dw, kind):
    if isinstance(dw, tuple):
        return jnp.concatenate(dw, axis=0)
    if dw.ndim == 3:
        return dw
    if kind == "row":
        return dw.reshape(4, dw.shape[0] // 4, dw.shape[1])
    return jnp.transpose(dw.reshape(dw.shape[0], 4, dw.shape[1] // 4), (1, 0, 2))


def _pad_rows8(w):
    return jnp.concatenate([w, jnp.zeros((8 - w.shape[0], w.shape[1]), w.dtype)], axis=0)


def _group_lanes(v):
    r = v.shape[0]
    return jnp.pad(v.reshape(r, NG, NH // NG), ((0, 0), (0, 0), (0, 128 - NH // NG))).reshape(r, NG * 128)


def _ungroup_lanes(v):
    r = v.shape[0]
    return v.reshape(r, NG, 128)[:, :, :NH // NG].reshape(r, NH)


def _local_step(wfull, small, x, mem, target, token=0.0, on_grads=None):
    bl, s, _ = x.shape
    T = bl * s
    x2, t2 = x.reshape(T, D), target.reshape(T, D)
    mem2 = mem.reshape(-1, D)
    g = {}
    tok = [token]
    mark = getattr(wfull, "mark", lambda value: None)

    def gain(name):
        return small[name].reshape(1, -1) + tok[0]

    def emit(tag, names):
        if on_grads is not None:
            tok[0] = tok[0] + on_grads(tag, {n: g[n] for n in names})

    def ffn_fwd(h, n, wgu, wd, tag, next_gain=None):
        gate, up, a = _gate_up_fwd(n, wfull[wgu], name=f"{tag}_gate_up")
        mark(a)
        out = _mm(a, wfull[wd], "nn", tk=DFF, scale=FFN_RES, residual=h, norm_gain=next_gain, name=f"{tag}_down")
        return out, (n, gate, up, a)

    def ffn_bwd(dh, h, norm, wgu, wd, saved, tag):
        n, gate, up, a = saved
        dgate, dup = _act_bwd(dh, wfull[wd], gate, up, FFN_RES, name=f"{tag}_d_act")
        g[wd] = _mm(a, dh, "tn", tm=1408, scale=FFN_RES, name=f"{tag}_d_w_down")
        g[wgu] = (_mm(n, dgate, "tn", tn=1408, col_shards=2, name=f"{tag}_d_w_gate"),
                  _mm(n, dup, "tn", tn=1408, col_shards=2, name=f"{tag}_d_w_up"))
        emit(tag, (wgu, wd))
        dh_in, g[norm] = _gate_up_bwd_input(dgate, dup, wfull[wgu], h, gain(norm), dh, name=f"{tag}_d_norm")
        return dh_in

    n1 = _norm_fwd(x2, gain("ffn1_norm"), name="ffn1_norm")
    (h1, u), ffn1_saved = ffn_fwd(x2, n1, "ffn1_w_gate_up", "ffn1_w_down", "ffn1", gain("mix_norm"))
    mark(h1)
    pp = _mm(u, wfull["w_in"], "nn", tm=2048, tn=1152, name="in_proj")
    wa8 = _pad_rows8(small["conv_a_w"])
    ws8 = _pad_rows8(small["ssm_conv_w"])
    conv_b = gain("ssm_conv_b")
    bias128 = jnp.pad(gain("ssm_dt_bias"), ((0, 0), (0, 128 - NH)))
    ya_pre = _conv_a_fwd(pp, wa8, bl, s, name="conv_a")
    xc = _conv_ssm_fwd(pp, ws8, conv_b, bl, s, name="conv_ssm")
    mark(xc)
    alog = gain("ssm_a_log")
    dtg, dtt = _dt_fwd(pp, bias128, name="dt")
    arow, acol = _group_lanes(alog), alog.reshape(NG, NH // NG, 1)
    dexp = jnp.repeat(gain("ssm_d"), HD, axis=1)
    ng = gain("ssm_norm")
    y, yn, prev = _ssd_fwd(xc, pp, dtg, dtt, arow, acol, dexp, ng, bl, s, name="ssd")
    ya = _mm(ya_pre, wfull["w_out_a"], "nn", tn=1024, out_dtype=BF16, name="out_a")
    yb = _mm(yn, wfull["w_out_ssm"], "nn", tn=1024, tk=DI, out_dtype=BF16, name="out_ssm")
    merged = _merge_fwd(pp, ya, yb, name="merge")
    h2, un = _mm(merged, wfull["w_mix_out"], "nn", residual=h1, norm_gain=gain("xattn_norm"), name="mix_out")
    q = _mm(un, wfull["w_q"], "nn", tn=1024, out_dtype=BF16, name="q_proj")
    mn = _norm_fwd(mem2, gain("mem_norm"), name="mem_norm")
    kv = _mm(mn, wfull["w_kv"], "nn", tn=1024, out_dtype=BF16, name="kv_proj")
    o = _attn_fwd(q, kv, bl, s, name="attn")
    h3, n2 = _mm(o, wfull["w_o_x"], "nn", residual=h2, norm_gain=gain("ffn2_norm"), name="attn_out")
    h4, ffn2_saved = ffn_fwd(h3, n2, "ffn2_w_gate_up", "ffn2_w_down", "ffn2")
    sq_err, dh4, dgf = _final_loss(h4, gain("final_norm"), t2, name="final_loss")
    g["final_norm"] = dgf

    dh3 = ffn_bwd(dh4, h3, "ffn2_norm", "ffn2_w_gate_up", "ffn2_w_down", ffn2_saved, "ffn2")
    do = _mm(dh3, wfull["w_o_x"], "nt", tn=1024, out_dtype=BF16, name="d_attn_o")
    g["w_o_x"] = _mm(o, dh3, "tn",name="d_w_o_x")
    dq, dk, dv = _attn_bwd(q, kv, do, bl, s, name="d_attn")
    g["w_q"] = _mm(un, dq, "tn",name="d_w_q")
    dkv = jnp.concatenate([dk, dv], axis=1)
    dmn = _mm(dkv, wfull["w_kv"], "nt", tn=1024, tk=2 * D, name="d_mem_norm_out")
    g["w_kv"] = _mm(mn, dkv, "tn", tn=512, col_shards=4, name="d_w_kv")
    emit("attn", ("w_q", "w_kv", "w_o_x"))
    _, g["mem_norm"] = _norm_bwd(mem2, gain("mem_norm"), dmn, None, name="d_mem_norm")
    dh2, g["xattn_norm"] = _mm(dq, wfull["w_q"], "nt", tm=512, residual=dh3, norm_bwd=(h2, gain("xattn_norm")),
                               name="d_xattn_norm")
    g["w_mix_out"] = _mm(merged, dh2, "tn",name="d_w_mix_out")
    dya, dyb, dpp = _merge_bwd(pp, ya, yb, dh2, wfull["w_mix_out"], name="d_merge")
    dya_pre = _mm(dya, wfull["w_out_a"], "nt", tn=1024, name="d_conv_a_out")
    g["w_out_a"] = _mm(ya_pre, dya, "tn",name="d_w_out_a")
    dyn = _mm(dyb, wfull["w_out_ssm"], "nt", tn=DI, name="d_ssd_out")
    g["w_out_ssm"] = _mm(yn, dyb, "tn",name="d_w_out_ssm")
    dpp, dwa8 = _conv_a_bwd(pp, wa8, dya_pre, dpp, bl, s, name="d_conv_a")
    g["conv_a_w"] = dwa8[:3]
    dpp, dxs, dbm, dcm, ddtg, g["ssm_norm"], ddexp, dalg = _ssd_bwd(
        dyn, y, xc, pp, dtg, dtt, arow, acol, dexp, ng, prev, dpp, bl, s, name="d_ssd")
    g["ssm_d"] = ddexp.reshape(NH, HD).sum(axis=1).reshape(1, NH)
    g["ssm_a_log"] = _ungroup_lanes(dalg)
    conv_dw, conv_db = [], []
    for dpart, off, tag in ((dxs, 0, "x"), (dbm, DI, "b"), (dcm, DI + NG * NS, "c")):
        dpp, dw_, db_ = _conv_ssm_bwd(pp, ws8, conv_b, dpart, off, dpp, bl, s, name=f"d_conv_ssm_{tag}")
        conv_dw.append(dw_)
        conv_db.append(db_)
    g["ssm_conv_w"] = jnp.concatenate(conv_dw, axis=1)[:4]
    g["ssm_conv_b"] = jnp.concatenate(conv_db, axis=1)
    dpp, dbias = _dt_bwd(pp, bias128, ddtg, dpp, name="d_dt")
    g["ssm_dt_bias"] = dbias[:, :NH]
    g["w_in"] = _mm(u, dpp, "tn", tn=1152, name="d_w_in")
    emit("mix", ("w_in", "w_out_a", "w_out_ssm", "w_mix_out"))
    dh1, g["mix_norm"] = _mm(dpp, wfull["w_in"], "nt", tm=512, tk=3456, residual=dh2,
                             norm_bwd=(h1, gain("mix_norm")), name="d_mix_norm")
    dx = ffn_bwd(dh1, x2, "ffn1_norm", "ffn1_w_gate_up", "ffn1_w_down", ffn1_saved, "ffn1")
    return sq_err, dx, g


W_IN_SHARD = NIN // 4


def _w_in_segments():
    segs, p = [], 0
    for t in range(D // CA_TILE):
        for which in range(3):
            segs.append((D * which + CA_TILE * t, p, CA_TILE))
            p += CA_TILE
    for s, n in ((3 * D, O_GA - 3 * D), (O_GA + NH, 2 * D), (O_GA, NH)):
        segs.append((s, p, n))
        p += n
    assert p == NIN and segs[-1][1] == O_DT and segs[-2][1] == O_GA
    return segs


def _pad_w_in_shards(land):
    pieces = []
    for s, _, n in _w_in_segments():
        while n > 0:
            kk, off = divmod(s, W_IN_SHARD)
            take = min(n, W_IN_SHARD - off)
            pieces.append(land[kk][:, off:off + take])
            s, n = s + take, n - take
    return jnp.concatenate(pieces + [jnp.zeros((land.shape[1], NPP - NIN), land.dtype)], axis=1)


class _WInGrad:
    def __init__(self, dwp):
        self.dwp = dwp
        self.shape = (4, dwp.shape[0], W_IN_SHARD)

    def rows(self, start, n):
        part = lax.dynamic_slice_in_dim(self.dwp, start, n, axis=0)
        shards = []
        for kk in range(4):
            n0, n1 = W_IN_SHARD * kk, W_IN_SHARD * (kk + 1)
            cuts = sorted((max(s, n0), p + max(s, n0) - s, min(s + m, n1) - max(s, n0))
                          for s, p, m in _w_in_segments() if min(s + m, n1) > max(s, n0))
            shards.append(jnp.concatenate([part[:, p:p + m] for _, p, m in cuts], axis=1))
        return jnp.stack(shards)


def _pad_w_in(w):
    return _pad_w_in_shards(jnp.stack(jnp.split(w, 4, axis=1)))


def _unpad_w_in(w):
    return jnp.concatenate(list(_WInGrad(w).rows(0, w.shape[0])), axis=1)


def kernel(x, mem, ffn1_norm, ffn1_w_gate_up, ffn1_w_down, mix_norm, w_in, conv_a_w, w_out_a, ssm_conv_w, ssm_conv_b, ssm_dt_bias, ssm_a_log, ssm_d, ssm_norm, w_out_ssm, w_mix_out, xattn_norm, mem_norm, w_q, w_kv, w_o_x, ffn2_norm, ffn2_w_gate_up, ffn2_w_down, final_norm, loss_target, m_ffn1_norm, m_ffn1_w_gate_up, m_ffn1_w_down, m_mix_norm, m_w_in, m_conv_a_w, m_w_out_a, m_ssm_conv_w, m_ssm_conv_b, m_ssm_dt_bias, m_ssm_a_log, m_ssm_d, m_ssm_norm, m_w_out_ssm, m_w_mix_out, m_xattn_norm, m_mem_norm, m_w_q, m_w_kv, m_w_o_x, m_ffn2_norm, m_ffn2_w_gate_up, m_ffn2_w_down, m_final_norm, v_ffn1_norm, v_ffn1_w_gate_up, v_ffn1_w_down, v_mix_norm, v_w_in, v_conv_a_w, v_w_out_a, v_ssm_conv_w, v_ssm_conv_b, v_ssm_dt_bias, v_ssm_a_log, v_ssm_d, v_ssm_norm, v_w_out_ssm, v_w_mix_out, v_xattn_norm, v_mem_norm, v_w_q, v_w_kv, v_w_o_x, v_ffn2_norm, v_ffn2_w_gate_up, v_ffn2_w_down, v_final_norm):
    a = dict(locals())
    xi, yi = lax.axis_index("x"), lax.axis_index("y")
    k = 2 * xi + yi

    conv_vec, conv_offs = _pack([a["conv_a_w"], a["ssm_conv_w"]])
    conv_all = _gather_all(conv_vec, name="gather_conv_w")
    wfull = _GatheredWeights({n: a[n] for n, _ in BIG}, k, conv_all)
    conv_sh = [_unpack(conv_all[2 * kk], conv_offs, [a["conv_a_w"].shape[1:], a["ssm_conv_w"].shape[1:]])
               for kk in range(4)]
    small = {n: a[n] for n in SMALL}
    small["conv_a_w"] = jnp.concatenate([cs[0] for cs in conv_sh], axis=1)
    small["ssm_conv_w"] = jnp.concatenate([cs[1] for cs in conv_sh], axis=1)

    rs_started = []

    def on_grads(tag, grads):
        names = [n for n, _ in BIG if n in grads]
        shard_major = [_WInGrad(grads[n]) if n == "w_in" else _shard_major(grads[n], dict(BIG)[n]) for n in names]
        st, tk = _rs_start(shard_major, tag)
        rs_started.append((tag, names, st))
        return tk[0, 0]

    sq_err, dx, g = _local_step(wfull, small, x, mem, loss_target, wfull.token, on_grads)
    loss = lax.psum(0.5 / D * jnp.sum(sq_err), ("x", "y", "c"))

    ci = lax.axis_index("c")
    out = {}

    def finish(tag, names, st, after):
        g_mine, g_other = _rs_finish(st, after, tag)
        for n, gm, go in zip(names, g_mine, g_other):
            res = _adamw_halves(a[n], gm, go, a["m_" + n], a["v_" + n], ci, name=f"adamw_{n}")
            out[n] = tuple(t.reshape(a[n].shape) for t in res)
        return res[1]

    done = dx
    for grp in rs_started[:-1]:
        done = finish(*grp, dx)

    full_shapes = [g[n].shape for n in SMALL]
    gvec, goffs = _pack([g[n] for n in SMALL])
    gsum = _sum_leading(_gather_all(gvec, name="gather_small_grads", before=[done]), name="sum_small_grads")
    finish(*rs_started[-1], gsum)
    gsmall = dict(zip(SMALL, _unpack(gsum, goffs, full_shapes)))
    for n in ("conv_a_w", "ssm_conv_w"):
        width = a[n].shape[2]
        gsmall[n] = lax.dynamic_slice_in_dim(gsmall[n], k * width, width, axis=1)
    local_shapes = [a[n].shape for n in SMALL]
    packs = [_pack([t[n] for n in SMALL]) for t in
             ({n: a[n] for n in SMALL}, gsmall, {n: a["m_" + n] for n in SMALL}, {n: a["v_" + n] for n in SMALL})]
    offs = packs[0][1]
    res = _adamw(*[p[0] for p in packs], name="adamw_small")
    unp = [_unpack(r, offs, local_shapes) for r in res]
    for i, n in enumerate(SMALL):
        out[n] = (gsmall[n].reshape(a[n].shape), unp[0][i], unp[1][i], unp[2][i])

    grad_x = dx.reshape(x.shape)
    return (loss, grad_x, *[out[n][0] for n in WEIGHTS], *[out[n][1] for n in WEIGHTS],
            *[out[n][2] for n in WEIGHTS], *[out[n][3] for n in WEIGHTS])
```

```python
import functools
import math

import jax
import jax.numpy as jnp
from jax import lax
from jax.experimental import pallas as pl
from jax.experimental.pallas import tpu as pltpu

F32 = jnp.float32
BF16 = jnp.bfloat16
MXU = jnp.bfloat16
HI = lax.Precision.HIGHEST

D = 1024
DFF = 2816
DI = 2048
NH, HD, NG, NS, CH = 32, 64, 4, 128, 128
GW = DI // NG
XH, XD = 4, 256
EPS = 1e-6
NEG = -1e30
CA_TILE = 256
O_CA, O_Z, O_XBC, O_GA, O_GB, O_DT, NPP = 0, 3072, 5120, 8192, 9216, 10240, 10368
NIN = 10272
FFN_RES = 0.5
ADAM_LR, ADAM_B1, ADAM_B2, ADAM_EPS, ADAM_WD, ADAM_STEP = 0.001, 0.9, 0.999, 1e-08, 0.01, 10
VMEM_LIMIT = 56 * 1024 * 1024
EPI_COLS = 256
SSD_EX = 4
MESH = pl.DeviceIdType.MESH
CHIP_FLIPS = ((1, 0), (0, 1), (1, 1))


def _cp(*sem):
    return pltpu.CompilerParams(dimension_semantics=sem, vmem_limit_bytes=VMEM_LIMIT)


def _tile(n, pref, align=128):
    if n <= pref:
        return n
    t = (pref // align) * align
    while t >= align:
        if n % t == 0:
            return t
        t -= align
    raise ValueError((n, pref))


def _dot(a, b, dims, prec=None):
    return lax.dot_general(a, b, (dims, ((), ())), preferred_element_type=F32, precision=prec)


def _nn(a, b, prec=None):
    return _dot(a, b, ((1,), (0,)), prec)


def _nt(a, b):
    return _dot(a, b, ((1,), (1,)))


def _tn(a, b):
    return _dot(a, b, ((0,), (0,)))


def _sig(x):
    return jax.nn.sigmoid(x)


def _mm(a, b, mode, *, name, tm=1024, tn=1024, tk=None, out_dtype=F32, scale=None, residual=None, col_shards=0,
        norm_gain=None, norm_bwd=None):
    if tk is None:
        tk = 2048 if mode == "tn" else 1024
    if mode == "nn":
        (M, K), (K2, N) = a.shape, b.shape
    elif mode == "nt":
        (M, K), (N, K2) = a.shape, b.shape
    else:
        (K, M), (K2, N) = a.shape, b.shape
    assert K == K2, (name, a.shape, b.shape)
    tm, tn, tk = _tile(M, tm), _tile(N, tn), _tile(K, tk)
    nk = K // tk
    if mode == "nn":
        a_spec = pl.BlockSpec((tm, tk), lambda i, j, k: (i, k))
        b_spec = pl.BlockSpec((tk, tn), lambda i, j, k: (k, j))
        dims = ((1,), (0,))
    elif mode == "nt":
        a_spec = pl.BlockSpec((tm, tk), lambda i, j, k: (i, k))
        b_spec = pl.BlockSpec((tn, tk), lambda i, j, k: (j, k))
        dims = ((1,), (1,))
    else:
        a_spec = pl.BlockSpec((tk, tm), lambda i, j, k: (k, i))
        b_spec = pl.BlockSpec((tk, tn), lambda i, j, k: (k, j))
        dims = ((0,), (0,))
    o_spec = pl.BlockSpec((tm, tn), lambda i, j, k: (i, j))
    out_spec, out_shape = o_spec, jax.ShapeDtypeStruct((M, N), out_dtype)
    if col_shards:
        per = N // col_shards // tn
        assert per * tn * col_shards == N, (name, N, tn, col_shards)
        out_spec = pl.BlockSpec((None, tm, tn), lambda i, j, k: (j // per, i, j % per))
        out_shape = jax.ShapeDtypeStruct((col_shards, M, N // col_shards), out_dtype)
    has_res = residual is not None
    has_norm = norm_gain is not None
    has_nb = norm_bwd is not None
    assert not (has_norm or has_nb) or (tn == N and not col_shards)
    assert not (has_norm and has_nb)
    n_in = 2 + has_res + has_norm + 2 * has_nb

    def body(*refs):
        a_ref, b_ref = refs[0], refs[1]
        o_ref = refs[n_in]

        if has_nb:
            @pl.when((pl.program_id(0) == 0) & (pl.program_id(2) == 0))
            def _():
                refs[n_in + 1][...] = jnp.zeros_like(refs[n_in + 1])

        def finish(acc):
            if scale is not None:
                acc = acc * scale
            if has_nb:
                xv, gv = refs[n_in - 2][...], refs[n_in - 1][...]
                r = lax.rsqrt(jnp.mean(xv * xv, axis=-1, keepdims=True) + EPS)
                xh = xv * r
                refs[n_in + 1][...] += jnp.sum(acc * xh, axis=0, keepdims=True)
                dxh = acc * gv
                acc = r * (dxh - xh * jnp.mean(dxh * xh, axis=-1, keepdims=True))
            if has_res:
                acc = acc + refs[2][...]
            o_ref[...] = acc.astype(out_dtype)
            if has_norm:
                rs = lax.rsqrt(jnp.mean(acc * acc, axis=-1, keepdims=True) + EPS)
                refs[n_in + 1][...] = (acc * rs * refs[n_in - 1][...]).astype(BF16)

        part = _dot(a_ref[...].astype(MXU), b_ref[...].astype(MXU), dims)
        if nk == 1:
            finish(part)
            return
        acc_ref = refs[-1]
        k = pl.program_id(2)

        @pl.when(k == 0)
        def _():
            acc_ref[...] = part

        @pl.when(k > 0)
        def _():
            acc_ref[...] += part

        @pl.when(k == nk - 1)
        def _():
            finish(acc_ref[...])

    ins, in_specs = [a, b], [a_spec, b_spec]
    if has_res:
        ins.append(residual)
        in_specs.append(o_spec)
    if has_norm:
        ins.append(norm_gain)
        in_specs.append(pl.BlockSpec((1, tn), lambda i, j, k: (0, j)))
        out_spec, out_shape = [out_spec, o_spec], [out_shape, jax.ShapeDtypeStruct((M, N), BF16)]
    if has_nb:
        vec = pl.BlockSpec((1, tn), lambda i, j, k: (0, j))
        ins += [norm_bwd[0], norm_bwd[1]]
        in_specs += [o_spec, vec]
        out_spec, out_shape = [out_spec, vec], [out_shape, jax.ShapeDtypeStruct((1, N), F32)]
    return pl.pallas_call(
        body, grid=(M // tm, N // tn, nk), in_specs=in_specs, out_specs=out_spec, out_shape=out_shape,
        scratch_shapes=[pltpu.VMEM((tm, tn), F32)] if nk > 1 else [],
        compiler_params=_cp(*(("arbitrary",) * 3 if has_nb else ("parallel", "parallel", "arbitrary"))),
        name=name)(*ins)


def _norm_fwd(x, g, *, name):
    T, d = x.shape
    tr = _tile(T, 512, 8)

    def body(x_ref, g_ref, o_ref):
        xv = x_ref[...]
        r = lax.rsqrt(jnp.mean(xv * xv, axis=-1, keepdims=True) + EPS)
        o_ref[...] = (xv * r * g_ref[...]).astype(BF16)

    return pl.pallas_call(
        body, grid=(T // tr,),
        in_specs=[pl.BlockSpec((tr, d), lambda i: (i, 0)), pl.BlockSpec((1, d), lambda i: (0, 0))],
        out_specs=pl.BlockSpec((tr, d), lambda i: (i, 0)),
        out_shape=jax.ShapeDtypeStruct((T, d), BF16), compiler_params=_cp("parallel"), name=name)(x, g)


def _norm_bwd(x, g, dn, dres, *, name):
    T, d = x.shape
    tr = _tile(T, 512, 8)
    has_res = dres is not None

    def body(*refs):
        x_ref, g_ref, dn_ref = refs[:3]
        dr_ref = refs[3] if has_res else None
        dx_ref, dg_ref = refs[-2], refs[-1]

        @pl.when(pl.program_id(0) == 0)
        def _():
            dg_ref[...] = jnp.zeros_like(dg_ref)

        xv = x_ref[...]
        dnv = dn_ref[...].astype(F32)
        r = lax.rsqrt(jnp.mean(xv * xv, axis=-1, keepdims=True) + EPS)
        xh = xv * r
        dg_ref[...] += jnp.sum(dnv * xh, axis=0, keepdims=True)
        dxh = dnv * g_ref[...]
        dx = r * (dxh - xh * jnp.mean(dxh * xh, axis=-1, keepdims=True))
        if has_res:
            dx = dx + dr_ref[...]
        dx_ref[...] = dx

    row = pl.BlockSpec((tr, d), lambda i: (i, 0))
    vec = pl.BlockSpec((1, d), lambda i: (0, 0))
    ins = [x, g, dn] + ([dres] if has_res else [])
    return pl.pallas_call(
        body, grid=(T // tr,), in_specs=[row, vec, row] + ([row] if has_res else []),
        out_specs=[row, vec],
        out_shape=[jax.ShapeDtypeStruct((T, d), F32), jax.ShapeDtypeStruct((1, d), F32)],
        compiler_params=_cp("arbitrary"), name=name)(*ins)


def _final_loss(h, g, target, *, name):
    T, d = h.shape
    tr = _tile(T, 512, 8)

    def body(h_ref, g_ref, t_ref, l_ref, dh_ref, dg_ref):
        @pl.when(pl.program_id(0) == 0)
        def _():
            l_ref[...] = jnp.zeros_like(l_ref)
            dg_ref[...] = jnp.zeros_like(dg_ref)

        xv = h_ref[...]
        r = lax.rsqrt(jnp.mean(xv * xv, axis=-1, keepdims=True) + EPS)
        xh = xv * r
        e = xh * g_ref[...] - t_ref[...]
        l_ref[...] += jnp.sum(e * e, axis=0, keepdims=True)
        dy = e * (1.0 / d)
        dg_ref[...] += jnp.sum(dy * xh, axis=0, keepdims=True)
        dxh = dy * g_ref[...]
        dh_ref[...] = r * (dxh - xh * jnp.mean(dxh * xh, axis=-1, keepdims=True))

    row = pl.BlockSpec((tr, d), lambda i: (i, 0))
    vec = pl.BlockSpec((1, d), lambda i: (0, 0))
    return pl.pallas_call(
        body, grid=(T // tr,), in_specs=[row, vec, row], out_specs=[vec, row, vec],
        out_shape=[jax.ShapeDtypeStruct((1, d), F32), jax.ShapeDtypeStruct((T, d), F32),
                   jax.ShapeDtypeStruct((1, d), F32)],
        compiler_params=_cp("arbitrary"), name=name)(h, g, target)


def _shard_chunks(width, size=256):
    starts = list(range(0, width, size))
    if width - starts[-1] < EPI_COLS and len(starts) > 1:
        starts.pop()
    return [(o, (starts[i + 1] if i + 1 < len(starts) else width) - o) for i, o in enumerate(starts)]


def _gate_up_fwd(n, wsh, *, name):
    T, d = n.shape
    ws = wsh.shape[2]
    f = 2 * ws
    tm = _tile(T, 512, 8)

    def body(n_ref, wg_ref, wu_ref, g_ref, u_ref, a_ref):
        nv = n_ref[...].astype(MXU)
        for sh in range(2):
            for off, size in _shard_chunks(ws):
                gv = _nn(nv, wg_ref[sh, :, off:off + size].astype(MXU))
                uv = _nn(nv, wu_ref[sh, :, off:off + size].astype(MXU))
                sl = slice(sh * ws + off, sh * ws + off + size)
                g_ref[:, sl] = gv.astype(BF16)
                u_ref[:, sl] = uv.astype(BF16)
                a_ref[:, sl] = (gv * _sig(gv) * uv).astype(BF16)

    out = pl.BlockSpec((tm, f), lambda i: (i, 0))
    act = jax.ShapeDtypeStruct((T, f), BF16)
    return pl.pallas_call(
        body, grid=(T // tm,),
        in_specs=[pl.BlockSpec((tm, d), lambda i: (i, 0)), pl.BlockSpec((2, d, ws), lambda i: (0, 0, 0)),
                  pl.BlockSpec((2, d, ws), lambda i: (1, 0, 0))],
        out_specs=[out, out, out], out_shape=[act, act, act], compiler_params=_cp("parallel"),
        name=name)(n, wsh, wsh)


def _gate_up_bwd_input(dgate, dup, wsh, x, g, dres, *, name):
    T, f = dgate.shape
    four, d, ws = wsh.shape
    tm = _tile(T, 256, 8)

    def body(dg_ref, du_ref, w_ref, x_ref, g_ref, dr_ref, dx_ref, dgain_ref):
        @pl.when(pl.program_id(0) == 0)
        def _():
            dgain_ref[...] = jnp.zeros_like(dgain_ref)

        dn = None
        for sh in range(four):
            src = dg_ref if sh < 2 else du_ref
            part = _nt(src[:, (sh % 2) * ws:(sh % 2 + 1) * ws].astype(MXU), w_ref[sh].astype(MXU))
            dn = part if dn is None else dn + part
        xv = x_ref[...]
        r = lax.rsqrt(jnp.mean(xv * xv, axis=-1, keepdims=True) + EPS)
        xh = xv * r
        dgain_ref[...] += jnp.sum(dn * xh, axis=0, keepdims=True)
        dxh = dn * g_ref[...]
        dx_ref[...] = r * (dxh - xh * jnp.mean(dxh * xh, axis=-1, keepdims=True)) + dr_ref[...]

    act = pl.BlockSpec((tm, f), lambda i: (i, 0))
    row = pl.BlockSpec((tm, d), lambda i: (i, 0))
    vec = pl.BlockSpec((1, d), lambda i: (0, 0))
    return pl.pallas_call(
        body, grid=(T // tm,),
        in_specs=[act, act, pl.BlockSpec((four, d, ws), lambda i: (0, 0, 0)), row, vec, row],
        out_specs=[row, vec],
        out_shape=[jax.ShapeDtypeStruct((T, d), F32), jax.ShapeDtypeStruct((1, d), F32)],
        compiler_params=_cp("arbitrary"), name=name)(dgate, dup, wsh, x, g, dres)


def _act_bwd(dh, wd, gate, up, scale, *, name):
    T, d = dh.shape
    f = wd.shape[0]
    tm, tn = _tile(T, 512, 8), _tile(f, DFF)

    tc = _tile(tn, EPI_COLS)

    def body(dh_ref, wd_ref, g_ref, u_ref, dg_ref, du_ref):
        dhv = dh_ref[...].astype(MXU)
        for j in range(tn // tc):
            sl = slice(j * tc, (j + 1) * tc)
            da = scale * _nt(dhv, wd_ref[sl, :].astype(MXU))
            gv, uv = g_ref[:, sl].astype(F32), u_ref[:, sl].astype(F32)
            s = _sig(gv)
            dg_ref[:, sl] = (da * uv * (s * (1.0 + gv * (1.0 - s)))).astype(BF16)
            du_ref[:, sl] = (da * (gv * s)).astype(BF16)

    tile = pl.BlockSpec((tm, tn), lambda i, j: (i, j))
    act = jax.ShapeDtypeStruct((T, f), BF16)
    return pl.pallas_call(
        body, grid=(T // tm, f // tn),
        in_specs=[pl.BlockSpec((tm, d), lambda i, j: (i, 0)), pl.BlockSpec((tn, d), lambda i, j: (j, 0)), tile, tile],
        out_specs=[tile, tile], out_shape=[act, act], compiler_params=_cp("parallel", "parallel"),
        name=name)(dh, wd, gate, up)


CONV_ROWS = 64
CONV_PAD = 8


def _rows_down(ref, r0, d, cols=slice(None)):
    if r0 - d >= 0:
        return ref[pl.ds(r0 - d, CONV_ROWS), cols]
    assert r0 == 0
    v = ref[pl.ds(0, CONV_ROWS), cols]
    ri = lax.broadcasted_iota(jnp.int32, v.shape, 0)
    return jnp.where(ri >= d, pltpu.roll(v, d, 0), 0.0)


def _fold8(v):
    return jnp.sum(v.reshape(CONV_ROWS // 8, 8, v.shape[1]), axis=0)


def _taps(w_ref, views):
    acc = None
    for k, v in enumerate(views):
        t = w_ref[k:k + 1, :] * v
        acc = t if acc is None else acc + t
    return acc


def _conv_a_fwd(pp, w8, bl, s, *, name):
    tc = CA_TILE
    nb = D // tc
    bcol, ccol, vcol = slice(0, tc), slice(tc, 2 * tc), slice(2 * tc, 3 * tc)

    def body(p_ref, w_ref, o_ref):
        for r0 in range(0, s, CONV_ROWS):
            cv = [_rows_down(p_ref, r0, 2 - k, ccol) * _rows_down(p_ref, r0, 2 - k, vcol) for k in range(3)]
            o_ref[pl.ds(r0, CONV_ROWS), :] = (p_ref[pl.ds(r0, CONV_ROWS), bcol] * _taps(w_ref, cv)).astype(BF16)

    return pl.pallas_call(
        body, grid=(bl, nb),
        in_specs=[pl.BlockSpec((s, 3 * tc), lambda b, j: (b, j)), pl.BlockSpec((8, tc), lambda b, j: (0, j))],
        out_specs=pl.BlockSpec((s, tc), lambda b, j: (b, j)),
        out_shape=jax.ShapeDtypeStruct((bl * s, D), BF16), compiler_params=_cp("parallel", "parallel"),
        name=name)(pp, w8)


def _conv_a_bwd(pp, w8, dya, dpp, bl, s, *, name):
    tc = CA_TILE
    nb = D // tc
    bcol, ccol, vcol = slice(0, tc), slice(tc, 2 * tc), slice(2 * tc, 3 * tc)

    def body(p_ref, w_ref, dy_ref, dpp_in, d_ref, dw_ref, dcp):
        del dpp_in

        @pl.when(pl.program_id(1) == 0)
        def _():
            dw_ref[...] = jnp.zeros_like(dw_ref)

        dcp[pl.ds(s, CONV_PAD), :] = jnp.zeros((CONV_PAD, tc), F32)
        dw_acc = [jnp.zeros((8, tc), F32) for _ in range(3)]
        for r0 in reversed(range(0, s, CONV_ROWS)):
            rows = pl.ds(r0, CONV_ROWS)
            cs = [_rows_down(p_ref, r0, 2 - k, ccol) for k in range(3)]
            vs = [_rows_down(p_ref, r0, 2 - k, vcol) for k in range(3)]
            cv = [c_ * v_ for c_, v_ in zip(cs, vs)]
            dy = dy_ref[rows, :]
            d_ref[rows, bcol] = (dy * _taps(w_ref, cv)).astype(BF16)
            dconv = dy * p_ref[rows, bcol]
            dcp[rows, :] = dconv
            dcv = _taps(w_ref, [dcp[pl.ds(r0 + 2, CONV_ROWS), :], dcp[pl.ds(r0 + 1, CONV_ROWS), :], dconv])
            d_ref[rows, ccol] = (dcv * vs[2]).astype(BF16)
            d_ref[rows, vcol] = (dcv * cs[2]).astype(BF16)
            dw_acc = [acc + _fold8(dconv * cv_) for acc, cv_ in zip(dw_acc, cv)]
        for k in range(3):
            dw_ref[k:k + 1, :] += jnp.sum(dw_acc[k], axis=0, keepdims=True)

    wspec = pl.BlockSpec((8, tc), lambda j, b: (0, j))
    wide = pl.BlockSpec((s, 3 * tc), lambda j, b: (b, j))
    return pl.pallas_call(
        body, grid=(nb, bl),
        in_specs=[wide, wspec, pl.BlockSpec((s, tc), lambda j, b: (b, j)), pl.BlockSpec(memory_space=pl.ANY)],
        out_specs=[wide, wspec], out_shape=[jax.ShapeDtypeStruct(dpp.shape, dpp.dtype), jax.ShapeDtypeStruct((8, D), F32)],
        scratch_shapes=[pltpu.VMEM((s + CONV_PAD, tc), F32)], input_output_aliases={3: 0},
        compiler_params=_cp("parallel", "arbitrary"), name=name)(pp, w8, dya, dpp)


def _conv_ssm_fwd(pp, w8, bias, bl, s, *, name):
    tc = 256
    width = DI + 2 * NG * NS
    nb = width // tc

    def body(x_ref, w_ref, b_ref, o_ref):
        for r0 in range(0, s, CONV_ROWS):
            pre = _taps(w_ref, [_rows_down(x_ref, r0, 3 - k) for k in range(4)]) + b_ref[...]
            o_ref[pl.ds(r0, CONV_ROWS), :] = pre * _sig(pre)

    return pl.pallas_call(
        body, grid=(bl, nb),
        in_specs=[pl.BlockSpec((s, tc), lambda b, j: (b, O_XBC // tc + j)),
                  pl.BlockSpec((8, tc), lambda b, j: (0, j)), pl.BlockSpec((1, tc), lambda b, j: (0, j))],
        out_specs=pl.BlockSpec((s, tc), lambda b, j: (b, j)),
        out_shape=jax.ShapeDtypeStruct((bl * s, width), F32), compiler_params=_cp("parallel", "parallel"),
        name=name)(pp, w8, bias)


def _conv_ssm_bwd(pp, w8, bias, dxc, ch_off, dpp, bl, s, *, name):
    n = dxc.shape[1]
    tc = 256
    nb = n // tc
    o0 = ch_off // tc

    def body(x_ref, w_ref, b_ref, d_ref, dpp_in, dx_ref, dw_ref, db_ref, dp):
        del dpp_in

        @pl.when(pl.program_id(1) == 0)
        def _():
            dw_ref[...] = jnp.zeros_like(dw_ref)
            db_ref[...] = jnp.zeros_like(db_ref)

        dp[pl.ds(s, CONV_PAD), :] = jnp.zeros((CONV_PAD, tc), F32)
        dw_acc = [jnp.zeros((8, tc), F32) for _ in range(4)]
        db_acc = jnp.zeros((8, tc), F32)
        for r0 in reversed(range(0, s, CONV_ROWS)):
            rows = pl.ds(r0, CONV_ROWS)
            xs = [_rows_down(x_ref, r0, 3 - k) for k in range(4)]
            pre = _taps(w_ref, xs) + b_ref[...]
            sg = _sig(pre)
            dpre = d_ref[rows, :] * (sg * (1.0 + pre * (1.0 - sg)))
            dp[rows, :] = dpre
            dx = _taps(w_ref, [dp[pl.ds(r0 + 3 - k, CONV_ROWS), :] for k in range(3)] + [dpre])
            dx_ref[rows, :] = dx.astype(BF16)
            db_acc = db_acc + _fold8(dpre)
            dw_acc = [acc + _fold8(dpre * x_) for acc, x_ in zip(dw_acc, xs)]
        db_ref[...] += jnp.sum(db_acc, axis=0, keepdims=True)
        for k in range(4):
            dw_ref[k:k + 1, :] += jnp.sum(dw_acc[k], axis=0, keepdims=True)

    return pl.pallas_call(
        body, grid=(nb, bl),
        in_specs=[pl.BlockSpec((s, tc), lambda j, b: (b, O_XBC // tc + o0 + j)),
                  pl.BlockSpec((8, tc), lambda j, b: (0, o0 + j)), pl.BlockSpec((1, tc), lambda j, b: (0, o0 + j)),
                  pl.BlockSpec((s, tc), lambda j, b: (b, j)), pl.BlockSpec(memory_space=pl.ANY)],
        out_specs=[pl.BlockSpec((s, tc), lambda j, b: (b, O_XBC // tc + o0 + j)),
                   pl.BlockSpec((8, tc), lambda j, b: (0, j)), pl.BlockSpec((1, tc), lambda j, b: (0, j))],
        out_shape=[jax.ShapeDtypeStruct(dpp.shape, dpp.dtype), jax.ShapeDtypeStruct((8, n), F32),
                   jax.ShapeDtypeStruct((1, n), F32)],
        scratch_shapes=[pltpu.VMEM((s + CONV_PAD, tc), F32)], input_output_aliases={4: 0},
        compiler_params=_cp("parallel", "arbitrary"), name=name)(pp, w8, bias, dxc, dpp)


def _softplus(x):
    return jnp.maximum(x, 0.0) + jnp.log1p(jnp.exp(-jnp.abs(x)))


def _head_group_matrix():
    h = jnp.arange(128)[:, None]
    j = jnp.arange(NG * 128)[None, :]
    per = NH // NG
    return ((h < NH) & (j == (h // per) * 128 + h % per)).astype(F32)


def _dt_fwd(pp, bias128, *, name):
    T = pp.shape[0]
    tr = _tile(T, 1024, 8)
    per = NH // NG

    def body(x_ref, b_ref, p_ref, g_ref, t_ref):
        lane = lax.broadcasted_iota(jnp.int32, (tr, 128), 1)
        dt = jnp.where(lane < NH, _softplus(x_ref[...] + b_ref[...]), 0.0)
        g_ref[...] = _nn(dt, p_ref[...], HI)
        eye = (lax.broadcasted_iota(jnp.int32, (NH, 128), 0)
               == lax.broadcasted_iota(jnp.int32, (NH, 128), 1)).astype(F32)
        t_ref[...] = _dot(eye, dt, ((1,), (1,)), HI).reshape(NG, per, tr)

    vec = pl.BlockSpec((1, 128), lambda i: (0, 0))
    return pl.pallas_call(
        body, grid=(T // tr,),
        in_specs=[pl.BlockSpec((tr, 128), lambda i: (i, O_DT // 128)), vec,
                  pl.BlockSpec((128, NG * 128), lambda i: (0, 0))],
        out_specs=[pl.BlockSpec((tr, NG * 128), lambda i: (i, 0)), pl.BlockSpec((NG, per, tr), lambda i: (0, 0, i))],
        out_shape=[jax.ShapeDtypeStruct((T, NG * 128), F32), jax.ShapeDtypeStruct((NG, per, T), F32)],
        compiler_params=_cp("parallel"), name=name)(pp, bias128, _head_group_matrix())


def _dt_bwd(pp, bias128, ddtg, dpp, *, name):
    T = pp.shape[0]
    tr = _tile(T, 1024, 8)

    def body(x_ref, b_ref, d_ref, p_ref, dpp_in, o_ref, db_ref):
        del dpp_in

        @pl.when(pl.program_id(0) == 0)
        def _():
            db_ref[...] = jnp.zeros_like(db_ref)

        lane = lax.broadcasted_iota(jnp.int32, (tr, 128), 1)
        ddt = _dot(d_ref[...], p_ref[...], ((1,), (1,)), HI)
        dr = jnp.where(lane < NH, ddt * _sig(x_ref[...] + b_ref[...]), 0.0)
        db_ref[...] += jnp.sum(dr, axis=0, keepdims=True)
        o_ref[...] = dr.astype(BF16)

    col = pl.BlockSpec((tr, 128), lambda i: (i, O_DT // 128))
    vec = pl.BlockSpec((1, 128), lambda i: (0, 0))
    return pl.pallas_call(
        body, grid=(T // tr,),
        in_specs=[col, vec, pl.BlockSpec((tr, NG * 128), lambda i: (i, 0)), pl.BlockSpec((128, NG * 128), lambda i: (0, 0)),
                  pl.BlockSpec(memory_space=pl.ANY)],
        out_specs=[col, vec],
        out_shape=[jax.ShapeDtypeStruct(dpp.shape, dpp.dtype), jax.ShapeDtypeStruct((1, 128), F32)],
        input_output_aliases={4: 0}, compiler_params=_cp("arbitrary"),
        name=name)(pp, bias128, ddtg, _head_group_matrix(), dpp)


def _tril():
    return lax.broadcasted_iota(jnp.int32, (CH, CH), 0) >= lax.broadcasted_iota(jnp.int32, (CH, CH), 1)


def _ssd_common(dt, dtt, arow, acol):
    ri = lax.broadcasted_iota(jnp.int32, (CH, CH), 0)
    ci = lax.broadcasted_iota(jnp.int32, (CH, CH), 1)
    tril = ri >= ci
    triu = ri <= ci
    acs_col = _nn(tril.astype(F32), dt * arow, HI)
    acs_row = _nn(dtt * acol, triu.astype(F32), HI)
    return tril, triu, acs_col, acs_row


def _pair_terms(q, dt, acs_col, acs_row, tril, lo):
    ha, hb = 2 * q, 2 * q + 1
    col_a, col_b = acs_col[:, ha:ha + 1], acs_col[:, hb:hb + 1]
    row_a, row_b = acs_row[ha:ha + 1, :], acs_row[hb:hb + 1, :]
    last_a, last_b = acs_col[CH - 1:CH, ha:ha + 1], acs_col[CH - 1:CH, hb:hb + 1]
    out = dict(
        dtsel=jnp.where(lo, dt[:, ha:ha + 1], dt[:, hb:hb + 1]),
        d_a=jnp.exp(jnp.where(tril, col_a - row_a, NEG)), d_b=jnp.exp(jnp.where(tril, col_b - row_b, NEG)),
        esel=jnp.where(lo, jnp.exp(col_a), jnp.exp(col_b)),
        fsel=jnp.where(lo, jnp.exp(last_a - col_a), jnp.exp(last_b - col_b)),
        g_a=jnp.exp(last_a), g_b=jnp.exp(last_b))
    return out


def _ssd_fwd(xc, pp, dtg, dtt, arow, acol, dexp, ng, bl, s, *, name):
    nc = s // CH
    T = bl * s

    ex = SSD_EX if bl % SSD_EX == 0 else 1

    def body(*refs):
        arow_ref, acol_ref, dexp_ref, ng_ref = refs[6 * ex:6 * ex + 4]
        st_ref = refs[-1]

        @pl.when(pl.program_id(2) == 0)
        def _():
            st_ref[...] = jnp.zeros_like(st_ref)

        y_ref, yn_ref, prev_ref = refs[6 * ex + 4:6 * ex + 7]
        for e in range(ex):
            one(*refs[6 * e:6 * e + 6], arow_ref, acol_ref, dexp_ref, ng_ref,
                y_ref.at[e], yn_ref.at[e], prev_ref.at[e], st_ref.at[e])

    def one(xs_ref, bm_ref, cm_ref, z_ref, dt_ref, dtt_ref, arow_ref, acol_ref, dexp_ref, ng_ref,
            y_ref, yn_ref, prev_ref, st_ref):
        dt = dt_ref[...]
        tril, _, acs_col, acs_row = _ssd_common(dt, dtt_ref[...], -jnp.exp(arow_ref[...]), -jnp.exp(acol_ref[...]))
        bm, cm = bm_ref[...].astype(MXU), cm_ref[...].astype(MXU)
        cb = _nt(cm, bm)
        lo = lax.broadcasted_iota(jnp.int32, (CH, 128), 1) < HD
        sub_lo = lax.broadcasted_iota(jnp.int32, (128, NS), 0) < HD
        for q in range(4):
            t = _pair_terms(q, dt, acs_col, acs_row, tril, lo)
            x = xs_ref[:, 128 * q:128 * (q + 1)]
            xd = x * t["dtsel"]
            y = (_nn((cb * t["d_a"]).astype(MXU), jnp.where(lo, xd, 0.0).astype(MXU))
                 + _nn((cb * t["d_b"]).astype(MXU), jnp.where(lo, 0.0, xd).astype(MXU)))
            prev = st_ref[q]
            prev_ref[q] = prev
            y = y + t["esel"] * _nt(cm, prev.astype(MXU))
            st_ref[q] = prev * jnp.where(sub_lo, t["g_a"], t["g_b"]) + _tn((xd * t["fsel"]).astype(MXU), bm)
            y_ref[:, 128 * q:128 * (q + 1)] = y + dexp_ref[:, 128 * q:128 * (q + 1)] * x
        zv = z_ref[...]
        yg = y_ref[...] * (zv * _sig(zv))
        r = lax.rsqrt(jnp.mean(yg * yg, axis=-1, keepdims=True) + EPS)
        yn_ref[...] = (yg * r * ng_ref[...]).astype(BF16)

    def row(e, width, off_blocks):
        return pl.BlockSpec((CH, width), lambda g, b, c: ((b * ex + e) * nc + c, off_blocks + g))

    per_ex_in = [[row(e, GW, 0), row(e, NS, DI // NS), row(e, NS, DI // NS + NG), row(e, GW, O_Z // GW), row(e, 128, 0),
                  pl.BlockSpec((None, 8, CH), lambda g, b, c, e=e: (g, 0, (b * ex + e) * nc + c))] for e in range(ex)]
    by_example = pl.BlockSpec((ex, CH, GW), lambda g, b, c: (b, c, g))
    y, yn, prev = pl.pallas_call(
        body, grid=(NG, bl // ex, nc),
        in_specs=sum(per_ex_in, []) + [pl.BlockSpec((1, 128), lambda g, b, c: (0, g)),
                                       pl.BlockSpec((None, 8, 1), lambda g, b, c: (g, 0, 0)),
                                       pl.BlockSpec((1, GW), lambda g, b, c: (0, g)),
                                       pl.BlockSpec((1, GW), lambda g, b, c: (0, g))],
        out_specs=[by_example, by_example,
                   pl.BlockSpec((ex, None, 4, 128, NS), lambda g, b, c: (b, c, g, 0, 0))],
        out_shape=[jax.ShapeDtypeStruct((bl, s, DI), F32), jax.ShapeDtypeStruct((bl, s, DI), BF16),
                   jax.ShapeDtypeStruct((bl, nc, 16, 128, NS), F32)],
        scratch_shapes=[pltpu.VMEM((ex, 4, 128, NS), F32)],
        compiler_params=_cp("parallel", "parallel", "arbitrary"), name=name,
    )(*([xc, xc, xc, pp, dtg, dtt] * ex), arow, acol, dexp, ng)
    return y.reshape(T, DI), yn.reshape(T, DI), prev.reshape(bl * nc, 16, 128, NS)


def _ssd_bwd(dyn, y, xc, pp, dtg, dtt, arow, acol, dexp, ng, prev, dpp, bl, s, *, name):
    nc = s // CH
    T = bl * s

    def rsum(v):
        return jnp.sum(v, axis=1, keepdims=True)

    def asum(v):
        return jnp.sum(jnp.sum(v, axis=0, keepdims=True), axis=1, keepdims=True)

    ex = 1

    def body(*refs):
        shared = refs[9 * ex:9 * ex + 4]
        dz_ref, dxs_ref, db_ref, dc_ref, ddt_ref, dng_ref, dd_ref, dal_ref = refs[9 * ex + 5:9 * ex + 13]
        dst_ref = refs[-1]

        @pl.when((pl.program_id(1) == 0) & (pl.program_id(2) == 0))
        def _():
            dng_ref[...] = jnp.zeros_like(dng_ref)
            dd_ref[...] = jnp.zeros_like(dd_ref)
            dal_ref[...] = jnp.zeros_like(dal_ref)

        @pl.when(pl.program_id(2) == 0)
        def _():
            dst_ref[...] = jnp.zeros_like(dst_ref)

        for e in range(ex):
            one(*refs[9 * e:9 * e + 8], *shared, refs[9 * e + 8], dz_ref.at[e], dxs_ref.at[e], db_ref.at[e],
                dc_ref.at[e], ddt_ref.at[e], dng_ref, dd_ref, dal_ref, dst_ref.at[e])

    def one(dyn_ref, y_ref, xs_ref, bm_ref, cm_ref, z_ref, dt_ref, dtt_ref, arow_ref, acol_ref, dexp_ref, ng_ref,
            prev_ref, dz_ref, dxs_ref, db_ref, dc_ref, ddt_ref, dng_ref, dd_ref, dal_ref, dst_ref):
        yv, zv = y_ref[...], z_ref[...]
        sz = _sig(zv)
        silu = zv * sz
        yg = yv * silu
        r = lax.rsqrt(jnp.mean(yg * yg, axis=-1, keepdims=True) + EPS)
        yh = yg * r
        dynv = dyn_ref[...]
        dng_ref[...] += jnp.sum(dynv * yh, axis=0, keepdims=True)
        dyh = dynv * ng_ref[...]
        dyg = r * (dyh - yh * jnp.mean(dyh * yh, axis=-1, keepdims=True))
        dz_ref[...] = (dyg * yv * (sz * (1.0 + zv * (1.0 - sz)))).astype(BF16)
        dxs_ref[...] = dyg * silu
        dd_ref[...] += jnp.sum(dxs_ref[...] * xs_ref[...], axis=0, keepdims=True)

        dt = dt_ref[...]
        arow_v = -jnp.exp(arow_ref[...])
        tril, triu, acs_col, acs_row = _ssd_common(dt, dtt_ref[...], arow_v, -jnp.exp(acol_ref[...]))
        bm, cm = bm_ref[...].astype(MXU), cm_ref[...].astype(MXU)
        cb = _nt(cm, bm)
        lane = lax.broadcasted_iota(jnp.int32, (CH, 128), 1)
        is_last = lax.broadcasted_iota(jnp.int32, (CH, 128), 0) == CH - 1
        lo = lane < HD
        sub_lo = lax.broadcasted_iota(jnp.int32, (128, NS), 0) < HD
        dcb = jnp.zeros((CH, CH), F32)
        dc_acc = jnp.zeros((CH, NS), F32)
        db_acc = jnp.zeros((CH, NS), F32)
        dacs = jnp.zeros((CH, 128), F32)
        ddtx = jnp.zeros((CH, 128), F32)
        csum = jnp.zeros((8, CH), F32)
        sub8 = lax.broadcasted_iota(jnp.int32, (8, CH), 0)
        for q in range(4):
            ha, hb = 2 * q, 2 * q + 1
            sl = slice(128 * q, 128 * (q + 1))
            t = _pair_terms(q, dt, acs_col, acs_row, tril, lo)
            x, dy = xs_ref[:, sl], dxs_ref[:, sl]
            xd = x * t["dtsel"]
            xd_m = xd.astype(MXU)
            dy_lo, dy_hi = jnp.where(lo, dy, 0.0).astype(MXU), jnp.where(lo, 0.0, dy).astype(MXU)
            m_a, m_b = cb * t["d_a"], cb * t["d_b"]
            prev_m = prev_ref[q].astype(MXU)
            dnext = dst_ref[q]
            dnext_m = dnext.astype(MXU)
            bds = _nt(bm, dnext_m)
            dxd = _tn(m_a.astype(MXU), dy_lo) + _tn(m_b.astype(MXU), dy_hi) + t["fsel"] * bds
            dye_m = (dy * t["esel"]).astype(MXU)
            dst_ref[q] = dnext * jnp.where(sub_lo, t["g_a"], t["g_b"]) + _tn(dye_m, cm)
            dm_a, dm_b = _nt(dy_lo, xd_m), _nt(dy_hi, xd_m)
            dcb = dcb + dm_a * t["d_a"] + dm_b * t["d_b"]
            g_a, g_b = dm_a * m_a, dm_b * m_b
            csum = (csum + jnp.where(sub8 == ha, jnp.sum(g_a, axis=0, keepdims=True), 0.0)
                    + jnp.where(sub8 == hb, jnp.sum(g_b, axis=0, keepdims=True), 0.0))
            tf = t["fsel"] * xd * bds
            tyf = dy * (t["esel"] * _nt(cm, prev_m)) - tf
            dpp = dnext * prev_ref[q]
            ea = asum(jnp.where(lo, tf, 0.0)) + t["g_a"] * asum(jnp.where(sub_lo, dpp, 0.0))
            eb = asum(jnp.where(lo, 0.0, tf)) + t["g_b"] * asum(jnp.where(sub_lo, 0.0, dpp))
            ra = rsum(g_a + jnp.where(lo, tyf, 0.0)) + jnp.where(is_last, ea, 0.0)
            rb = rsum(g_b + jnp.where(lo, 0.0, tyf)) + jnp.where(is_last, eb, 0.0)
            dacs = dacs + jnp.where(lane == ha, ra, 0.0) + jnp.where(lane == hb, rb, 0.0)
            tx = dxd * x
            ddtx = (ddtx + jnp.where(lane == ha, rsum(jnp.where(lo, tx, 0.0)), 0.0)
                    + jnp.where(lane == hb, rsum(jnp.where(lo, 0.0, tx)), 0.0))
            dxs_ref[:, sl] = dxd * t["dtsel"] + dexp_ref[:, sl] * dy
            dc_acc = dc_acc + _nn(dye_m, prev_m)
            db_acc = db_acc + _nn((xd * t["fsel"]).astype(MXU), dnext_m)
        dcb_m = dcb.astype(MXU)
        dc_ref[...] = dc_acc + _nn(dcb_m, bm)
        db_ref[...] = db_acc + _tn(dcb_m, cm)
        dacs = dacs - jnp.concatenate([csum, jnp.zeros((CH - 8, CH), F32)], axis=0).T
        dla = _nn(triu.astype(F32), dacs, HI)
        ddt_ref[...] = dla * arow_v + ddtx
        dal_ref[...] += jnp.sum(dla * dt, axis=0, keepdims=True) * arow_v

    def row(e, width, off_blocks):
        return pl.BlockSpec((CH, width), lambda g, b, c: ((b * ex + e) * nc + nc - 1 - c, off_blocks + g))

    per_ex_in = [[row(e, GW, 0), row(e, GW, 0), row(e, GW, 0), row(e, NS, DI // NS), row(e, NS, DI // NS + NG),
                  row(e, GW, O_Z // GW), row(e, 128, 0),
                  pl.BlockSpec((None, 8, CH), lambda g, b, c, e=e: (g, 0, (b * ex + e) * nc + nc - 1 - c)),
                  pl.BlockSpec((None, 4, 128, NS), lambda g, b, c, e=e: ((b * ex + e) * nc + nc - 1 - c, g, 0, 0))]
                 for e in range(ex)]

    def by_example(width, off_blocks):
        return pl.BlockSpec((ex, CH, width), lambda g, b, c: (b, nc - 1 - c, off_blocks + g))

    gvec = pl.BlockSpec((1, GW), lambda g, b, c: (0, g))
    hvec = pl.BlockSpec((1, 128), lambda g, b, c: (0, g))
    out = pl.pallas_call(
        body, grid=(NG, bl // ex, nc),
        in_specs=sum(per_ex_in, []) + [hvec, pl.BlockSpec((None, 8, 1), lambda g, b, c: (g, 0, 0)), gvec, gvec,
                                       pl.BlockSpec(memory_space=pl.ANY)],
        out_specs=[by_example(GW, O_Z // GW), by_example(GW, 0), by_example(NS, 0), by_example(NS, 0),
                   by_example(128, 0), gvec, gvec, hvec],
        input_output_aliases={9 * ex + 4: 0},
        out_shape=[jax.ShapeDtypeStruct((bl, s, dpp.shape[1]), dpp.dtype), jax.ShapeDtypeStruct((bl, s, DI), F32),
                   jax.ShapeDtypeStruct((bl, s, NG * NS), F32), jax.ShapeDtypeStruct((bl, s, NG * NS), F32),
                   jax.ShapeDtypeStruct((bl, s, NG * 128), F32), jax.ShapeDtypeStruct((1, DI), F32),
                   jax.ShapeDtypeStruct((1, DI), F32), jax.ShapeDtypeStruct((1, NG * 128), F32)],
        scratch_shapes=[pltpu.VMEM((ex, 4, 128, NS), F32)],
        compiler_params=_cp("arbitrary", "arbitrary", "arbitrary"), name=name,
    )(*([dyn, y, xc, xc, xc, pp, dtg, dtt, prev] * ex), arow, acol, dexp, ng, dpp.reshape(bl, s, dpp.shape[1]))
    return (out[0].reshape(T, -1), out[1].reshape(T, DI), out[2].reshape(T, NG * NS), out[3].reshape(T, NG * NS),
            out[4].reshape(T, NG * 128), out[5], out[6], out[7])


def _merge_fwd(pp, ya, yb, *, name):
    T = ya.shape[0]
    tr = _tile(T, 512, 8)

    def body(ga_ref, gb_ref, ya_ref, yb_ref, o_ref):
        o_ref[...] = (_sig(ga_ref[...]) * ya_ref[...].astype(F32)
                      + _sig(gb_ref[...]) * yb_ref[...].astype(F32)).astype(BF16)

    row = pl.BlockSpec((tr, D), lambda i: (i, 0))
    return pl.pallas_call(
        body, grid=(T // tr,),
        in_specs=[pl.BlockSpec((tr, D), lambda i: (i, O_GA // D)), pl.BlockSpec((tr, D), lambda i: (i, O_GB // D)),
                  row, row],
        out_specs=row, out_shape=jax.ShapeDtypeStruct((T, D), BF16), compiler_params=_cp("parallel"),
        name=name)(pp, pp, ya, yb)


def _merge_bwd(pp, ya, yb, dh, w_mix, *, name):
    T = ya.shape[0]
    tr = _tile(T, 512, 8)
    assert O_GB == O_GA + D and O_GA % (2 * D) == 0

    def body(g_ref, ya_ref, yb_ref, dh_ref, w_ref, dya_ref, dyb_ref, dg_ref):
        dmv = _nt(dh_ref[...].astype(MXU), w_ref[...].astype(MXU))
        sa, sb = _sig(g_ref[:, :D]), _sig(g_ref[:, D:])
        dya_ref[...] = (dmv * sa).astype(BF16)
        dyb_ref[...] = (dmv * sb).astype(BF16)
        dg_ref[:, :D] = (dmv * ya_ref[...].astype(F32) * (sa * (1.0 - sa))).astype(BF16)
        dg_ref[:, D:] = (dmv * yb_ref[...].astype(F32) * (sb * (1.0 - sb))).astype(BF16)

    row = pl.BlockSpec((tr, D), lambda i: (i, 0))
    gates = pl.BlockSpec((tr, 2 * D), lambda i: (i, O_GA // (2 * D)))
    act = jax.ShapeDtypeStruct((T, D), BF16)
    return pl.pallas_call(
        body, grid=(T // tr,),
        in_specs=[gates, row, row, row, pl.BlockSpec(w_mix.shape, lambda i: (0, 0))], out_specs=[row, row, gates],
        out_shape=[act, act, jax.ShapeDtypeStruct((T, NPP), BF16)], compiler_params=_cp("parallel"),
        name=name)(pp, ya, yb, dh, w_mix)


def _softmax_rows(sc):
    e = jnp.exp(sc - jnp.max(sc, axis=-1, keepdims=True))
    return e / jnp.sum(e, axis=-1, keepdims=True)


def _attn_fwd(q, kv, bl, s, *, name):
    m = kv.shape[0] // bl
    tq = _tile(s, 2048)
    nq = s // tq
    scale = 1.0 / math.sqrt(XD)

    def body(q_ref, k_ref, v_ref, o_ref):
        p = _softmax_rows(_nt(q_ref[...], k_ref[...]) * scale)
        o_ref[...] = _nn(p.astype(MXU), v_ref[...]).astype(BF16)

    qspec = pl.BlockSpec((tq, XD), lambda b, h, i: (b * nq + i, h))
    return pl.pallas_call(
        body, grid=(bl, XH, nq),
        in_specs=[qspec, pl.BlockSpec((m, XD), lambda b, h, i: (b, h)),
                  pl.BlockSpec((m, XD), lambda b, h, i: (b, XH + h))],
        out_specs=qspec, out_shape=jax.ShapeDtypeStruct((bl * s, D), BF16),
        compiler_params=_cp("parallel", "parallel", "parallel"), name=name)(q, kv, kv)


def _attn_bwd(q, kv, do, bl, s, *, name):
    m = kv.shape[0] // bl
    tq = _tile(s, 2048)
    nq = s // tq
    scale = 1.0 / math.sqrt(XD)

    def body(q_ref, k_ref, v_ref, do_ref, dq_ref, dk_ref, dv_ref):
        @pl.when(pl.program_id(2) == 0)
        def _():
            dk_ref[...] = jnp.zeros_like(dk_ref)
            dv_ref[...] = jnp.zeros_like(dv_ref)

        qv, kvv, vv, dov = q_ref[...], k_ref[...], v_ref[...], do_ref[...]
        p = _softmax_rows(_nt(qv, kvv) * scale)
        dp = _nt(dov, vv)
        ds = (p * (dp - jnp.sum(dp * p, axis=-1, keepdims=True)) * scale).astype(MXU)
        dq_ref[...] = _nn(ds, kvv).astype(BF16)
        dk_ref[...] += _tn(ds, qv)
        dv_ref[...] += _tn(p.astype(MXU), dov)

    qspec = pl.BlockSpec((tq, XD), lambda b, h, i: (b * nq + i, h))
    kspec = pl.BlockSpec((m, XD), lambda b, h, i: (b, h))
    return pl.pallas_call(
        body, grid=(bl, XH, nq),
        in_specs=[qspec, kspec, pl.BlockSpec((m, XD), lambda b, h, i: (b, XH + h)), qspec],
        out_specs=[qspec, kspec, kspec],
        out_shape=[jax.ShapeDtypeStruct((bl * s, D), BF16), jax.ShapeDtypeStruct((bl * m, D), F32),
                   jax.ShapeDtypeStruct((bl * m, D), F32)],
        compiler_params=_cp("parallel", "parallel", "arbitrary"), name=name)(q, kv, kv, do)


def _row_tile(r, c, max_elems=512 * 1024, align=16):
    best = None
    for t in range(align, r + 1, align):
        if r % t == 0 and t * c <= max_elems:
            best = t
    return best if best is not None else r


def _addn(a, others, *, name, also_bf16=False):
    r, c = a.shape
    tr = _row_tile(r, c)
    n = len(others)

    def body(*refs):
        acc = refs[0][...].astype(F32)
        for o_ref in refs[1:1 + n]:
            acc = acc + o_ref[...].astype(F32)
        refs[1 + n][...] = acc
        if also_bf16:
            refs[2 + n][...] = acc.astype(BF16)

    spec = pl.BlockSpec((tr, c), lambda i: (i, 0))
    shapes = [jax.ShapeDtypeStruct((r, c), F32)] + ([jax.ShapeDtypeStruct((r, c), BF16)] if also_bf16 else [])
    out = pl.pallas_call(
        body, grid=(r // tr,), in_specs=[spec] * (1 + n), out_specs=[spec] * len(shapes), out_shape=shapes,
        compiler_params=_cp("parallel"), name=name)(a, *others)
    return out if also_bf16 else out[0]


def _sum_leading(a, *, name):
    n, r, c = a.shape

    def body(a_ref, o_ref):
        acc = a_ref[0]
        for i in range(1, n):
            acc = acc + a_ref[i]
        o_ref[...] = acc

    return pl.pallas_call(body, out_shape=jax.ShapeDtypeStruct((r, c), F32), name=name)(a)


def _adamw_math(wv, gv, mv, vv):
    m2 = ADAM_B1 * mv + (1.0 - ADAM_B1) * gv
    v2 = ADAM_B2 * vv + (1.0 - ADAM_B2) * (gv * gv)
    m_hat = m2 / (1.0 - ADAM_B1 ** ADAM_STEP)
    v_hat = v2 / (1.0 - ADAM_B2 ** ADAM_STEP)
    return -ADAM_LR * (m_hat / (jnp.sqrt(v_hat) + ADAM_EPS) + ADAM_WD * wv), m2, v2


def _adamw(w, g, m, v, *, name):
    r, c = w.shape
    tr = _row_tile(r, c, align=8)

    def body(w_ref, g_ref, m_ref, v_ref, d_ref, mo_ref, vo_ref):
        d_ref[...], mo_ref[...], vo_ref[...] = _adamw_math(w_ref[...], g_ref[...], m_ref[...], v_ref[...])

    spec = pl.BlockSpec((tr, c), lambda i: (i, 0))
    shp = jax.ShapeDtypeStruct((r, c), F32)
    return pl.pallas_call(
        body, grid=(r // tr,), in_specs=[spec] * 4, out_specs=[spec] * 3, out_shape=[shp] * 3,
        compiler_params=_cp("parallel"), name=name)(w, g, m, v)


def _adamw_halves(w, g_mine, g_other, m, v, c, *, name):
    _, r, cols = w.shape
    h = r // 2
    tr = _row_tile(h, cols, align=8)
    nh = h // tr

    def body(c_ref, w_ref, gm_ref, go_ref, m_ref, v_ref, g_ref, d_ref, mo_ref, vo_ref):
        gv = jnp.where(pl.program_id(0) // nh == c_ref[0], gm_ref[...], go_ref[...])
        g_ref[...] = gv
        d_ref[...], mo_ref[...], vo_ref[...] = _adamw_math(w_ref[...], gv, m_ref[...], v_ref[...])

    full = pl.BlockSpec((None, tr, cols), lambda i, c_: (0, i, 0))
    half = pl.BlockSpec((tr, cols), lambda i, c_: (i % nh, 0))
    shp = jax.ShapeDtypeStruct((1, r, cols), F32)
    return pl.pallas_call(
        body,
        grid_spec=pltpu.PrefetchScalarGridSpec(num_scalar_prefetch=1, grid=(2 * nh,),
                                               in_specs=[full, half, half, full, full], out_specs=[full] * 4),
        out_shape=[shp] * 4, compiler_params=_cp("parallel"), name=name,
    )(jnp.reshape(c, (1,)).astype(jnp.int32), w, g_mine, g_other, m, v)


def _flip(i, d):
    return 1 - i if d else i


def _comm(name, ins, out_shapes, n_remote, n_local, plan, aliases=None):
    n_in, n_out = len(ins), len(out_shapes)

    def body(*refs):
        in_refs, out_refs = refs[:n_in], refs[n_in:n_in + n_out]
        send_sems, recv_sems = refs[n_in + n_out], refs[n_in + n_out + 1]
        x, y, c = lax.axis_index("x"), lax.axis_index("y"), lax.axis_index("c")
        remote, local = plan(in_refs, out_refs, x, y, c)
        assert len(remote) == n_remote and len(local) == n_local
        copies = []
        if n_local:
            loc_sems = refs[n_in + n_out + 2]
            copies += [pltpu.make_async_copy(s_, d_, loc_sems.at[i]) for i, (s_, d_) in enumerate(local)]
        copies += [pltpu.make_async_remote_copy(src_ref=s_, dst_ref=d_, send_sem=send_sems.at[i],
                                                recv_sem=recv_sems.at[i], device_id=dev, device_id_type=MESH)
                   for i, (s_, d_, dev) in enumerate(remote)]
        for cp in copies:
            cp.start()
        for cp in copies:
            cp.wait()

    hbm = pl.BlockSpec(memory_space=pl.ANY)
    scratch = [pltpu.SemaphoreType.DMA((n_remote,)), pltpu.SemaphoreType.DMA((n_remote,))]
    if n_local:
        scratch.append(pltpu.SemaphoreType.DMA((n_local,)))
    return pl.pallas_call(
        body, in_specs=[hbm] * n_in, out_specs=[hbm] * n_out, out_shape=out_shapes, scratch_shapes=scratch,
        input_output_aliases=aliases or {}, compiler_params=pltpu.CompilerParams(has_side_effects=True),
        name=name)(*ins)


HBM_SPEC = pl.BlockSpec(memory_space=pltpu.HBM)
SEM_SPEC = pl.BlockSpec(memory_space=pltpu.SEMAPHORE)
DATAFLOW = pltpu.SideEffectType.DATAFLOW_SIDE_EFFECTING


def _remote_copies(plan, srcs, lands, send_sems, recv_sems, n_copies):
    x, y, c = lax.axis_index("x"), lax.axis_index("y"), lax.axis_index("c")
    copies = plan(srcs, lands, x, y, c)
    assert len(copies) == n_copies
    return [pltpu.make_async_remote_copy(src_ref=s_, dst_ref=d_, send_sem=send_sems.at[i], recv_sem=recv_sems.at[i],
                                         device_id=dev, device_id_type=MESH) for i, (s_, d_, dev) in enumerate(copies)]


def _split_start(name, srcs, lands, n_copies, plan, after=None):
    ns, nb = len(srcs), len(srcs) + len(lands)
    n_after = 0 if after is None else 1
    n_in = nb + n_after

    def body(*refs):
        for cp in _remote_copies(plan, refs[:ns], refs[ns:nb], refs[n_in], refs[n_in + 1], n_copies):
            cp.start()
        refs[-1][...] = jnp.zeros_like(refs[-1])

    arrays = [pltpu.with_memory_space_constraint(a_, pltpu.HBM) for a_ in list(srcs) + list(lands)]
    out = pl.pallas_call(
        body, name=name,
        out_shape=(pltpu.SemaphoreType.DMA((n_copies,)), pltpu.SemaphoreType.DMA((n_copies,)),
                   *[pltpu.HBM(a_.shape, a_.dtype) for a_ in arrays], jax.ShapeDtypeStruct((8, 128), F32)),
        in_specs=[HBM_SPEC] * nb + [pl.BlockSpec(memory_space=pl.ANY)] * n_after,
        out_specs=(SEM_SPEC, SEM_SPEC, *[HBM_SPEC] * nb, pl.BlockSpec(memory_space=pltpu.VMEM)),
        input_output_aliases={i: 2 + i for i in range(nb)},
        compiler_params=pltpu.CompilerParams(has_side_effects=DATAFLOW))(*arrays, *([after] * n_after))
    return (out[0], out[1], list(out[2:2 + nb])), out[-1]


def _split_wait(name, handle, ns, n_copies, plan, after):
    send_sems, recv_sems, bufs = handle
    nb = len(bufs)

    def body(*refs):
        for cp in _remote_copies(plan, refs[:ns], refs[ns:nb], refs[nb], refs[nb + 1], n_copies):
            cp.wait_send()
            cp.wait_recv()

    out = pl.pallas_call(
        body, name=name, out_shape=[pltpu.HBM(b_.shape, b_.dtype) for b_ in bufs],
        in_specs=[HBM_SPEC] * nb + [SEM_SPEC, SEM_SPEC, pl.BlockSpec(memory_space=pl.ANY)],
        out_specs=[HBM_SPEC] * nb, input_output_aliases={i: i for i in range(nb)},
        compiler_params=pltpu.CompilerParams(has_side_effects=DATAFLOW))(*bufs, send_sems, recv_sems, after)
    return list(out[ns:])


def _gather_start(shards, tag, after=None):
    n = len(shards)
    lands = [lax.empty((4,) + s.shape, s.dtype) for s in shards]

    def plan(srcs, dsts, x, y, c):
        k = 2 * x + y
        copies = []
        for w_ref, o_ref in zip(srcs, dsts):
            h = w_ref.shape[0] // 2
            rows = pl.ds(c * h, h)
            copies += [(w_ref.at[rows], o_ref.at[k, rows], (_flip(x, dx), _flip(y, dy), c)) for dx, dy in CHIP_FLIPS]
        return copies

    handle, token = _split_start(f"gather_{tag}_start", shards, lands, 3 * n, plan, after)
    return (handle, plan, n), token


def _gather_wait(started, after, tag):
    handle, plan, n = started
    return _split_wait(f"gather_{tag}_wait", handle, n, 3 * n, plan, after)


def _gather_d2d(lands, before, tag):
    n = len(lands)

    def plan_d2d(in_refs, out_refs, x, y, c):
        remote = []
        for o_ref in out_refs:
            h = o_ref.shape[1] // 2
            for dx, dy in CHIP_FLIPS:
                half = o_ref.at[2 * _flip(x, dx) + _flip(y, dy), pl.ds(c * h, h)]
                remote.append((half, half, (x, y, 1 - c)))
        return remote, []

    return _comm(f"gather_{tag}_d2d", list(lands) + list(before),
                 [jax.ShapeDtypeStruct(l_.shape, l_.dtype) for l_ in lands], 3 * n, 0, plan_d2d,
                 aliases={i: i for i in range(n)})


def _pair_plan(in_refs, out_refs, x, y, c):
    return [(i_, o_, (x, y, 1 - c)) for i_, o_ in zip(in_refs, out_refs)], []


def _rs_start(grads, tag):
    n = len(grads)
    c = lax.axis_index("c")
    def rows(g, start, h):
        return g.rows(start, h) if isinstance(g, _WInGrad) else lax.dynamic_slice_in_dim(g, start, h, axis=1)

    halves = [g.shape[1] // 2 for g in grads]
    mine = [rows(g, c * h, h) for g, h in zip(grads, halves)]
    send_a = [rows(g, (1 - c) * h, h).astype(BF16) for g, h in zip(grads, halves)]
    recv_a = _comm(f"rs_pair_{tag}", send_a, [jax.ShapeDtypeStruct(s.shape, BF16) for s in send_a], n, 0, _pair_plan)
    pair, pair_b = [], []
    for i, (mi, ra) in enumerate(zip(mine, recv_a)):
        four, h, cols = mi.shape
        p32, p16 = _addn(mi.reshape(four * h, cols), [ra.reshape(four * h, cols)], name=f"rs_pair_sum_{tag}_{i}",
                         also_bf16=True)
        pair.append(p32.reshape(four, h, cols))
        pair_b.append(p16.reshape(four, h, cols))

    def plan(srcs, dsts, x, y, c_):
        copies = []
        for i_, o_ in zip(srcs, dsts):
            for j, (dx, dy) in enumerate(CHIP_FLIPS):
                fx, fy = _flip(x, dx), _flip(y, dy)
                copies.append((i_.at[2 * fx + fy], o_.at[j], (fx, fy, c_)))
        return copies

    lands = [lax.empty((3,) + p.shape[1:], BF16) for p in pair_b]
    handle, token = _split_start(f"rs_chips_{tag}_start", pair_b, lands, 3 * n, plan)
    return (handle, plan, n, pair), token


def _rs_finish(started, after, tag):
    handle, plan, n, pair = started
    recv_b = _split_wait(f"rs_chips_{tag}_wait", handle, n, 3 * n, plan, after)
    k = 2 * lax.axis_index("x") + lax.axis_index("y")
    tot = [_addn(lax.dynamic_index_in_dim(p, k, 0, keepdims=False), [rb[0], rb[1], rb[2]],
                 name=f"rs_chip_sum_{tag}_{i}") for i, (p, rb) in enumerate(zip(pair, recv_b))]
    other = _comm(f"rs_halves_{tag}", tot, [jax.ShapeDtypeStruct(t.shape, F32) for t in tot], n, 0, _pair_plan)
    return tot, other


def _gather_all(vec, *, name, before=()):
    out = jax.ShapeDtypeStruct((8,) + vec.shape, vec.dtype)

    def plan(in_refs, out_refs, x, y, c):
        me = 4 * x + 2 * y + c
        remote = [(in_refs[0], out_refs[0].at[me], (_flip(x, dx), _flip(y, dy), _flip(c, dc)))
                  for dx in (0, 1) for dy in (0, 1) for dc in (0, 1) if (dx, dy, dc) != (0, 0, 0)]
        return remote, [(in_refs[0], out_refs[0].at[me])]

    return _comm(name, [vec] + list(before), [out], 7, 1, plan)[0]


def _pack(parts):
    flat = [p.reshape(-1).astype(F32) for p in parts]
    total = sum(f.shape[0] for f in flat)
    n = -(-total // 1024) * 128
    vec = jnp.concatenate(flat + [jnp.zeros((8 * n - total,), F32)]).reshape(8, n)
    offs, o = [], 0
    for f in flat:
        offs.append((o, f.shape[0]))
        o += f.shape[0]
    return vec, offs


def _unpack(vec, offs, shapes):
    flat = vec.reshape(-1)
    return [flat[o:o + n].reshape(s) for (o, n), s in zip(offs, shapes)]


BIG = (("ffn1_w_gate_up", "col"), ("ffn1_w_down", "row"), ("w_in", "col"), ("w_out_a", "row"), ("w_out_ssm", "row"),
       ("w_mix_out", "row"), ("w_q", "row"), ("w_kv", "col"), ("w_o_x", "row"), ("ffn2_w_gate_up", "col"),
       ("ffn2_w_down", "row"))
SMALL = ("ffn1_norm", "mix_norm", "conv_a_w", "ssm_conv_w", "ssm_conv_b", "ssm_dt_bias", "ssm_a_log", "ssm_d",
         "ssm_norm", "xattn_norm", "mem_norm", "ffn2_norm", "final_norm")
WEIGHTS = ("ffn1_norm", "ffn1_w_gate_up", "ffn1_w_down", "mix_norm", "w_in", "conv_a_w", "w_out_a", "ssm_conv_w",
           "ssm_conv_b", "ssm_dt_bias", "ssm_a_log", "ssm_d", "ssm_norm", "w_out_ssm", "w_mix_out", "xattn_norm",
           "mem_norm", "w_q", "w_kv", "w_o_x", "ffn2_norm", "ffn2_w_gate_up", "ffn2_w_down", "final_norm")


GATHER_GROUPS = (("a", ("ffn1_w_gate_up",)), ("b", ("ffn1_w_down", "w_in")),
                 ("c", ("w_out_a", "w_out_ssm", "w_mix_out", "w_q", "w_kv", "w_o_x", "ffn2_w_gate_up", "ffn2_w_down")))


def _place_own(land, own, k, *, name):
    four, r, cols = land.shape
    tr = _row_tile(r, cols)

    def body(k_ref, own_ref, land_in, o_ref):
        del k_ref, land_in
        o_ref[...] = own_ref[...]

    return pl.pallas_call(
        body,
        grid_spec=pltpu.PrefetchScalarGridSpec(
            num_scalar_prefetch=1, grid=(r // tr,),
            in_specs=[pl.BlockSpec((tr, cols), lambda i, k_: (i, 0)), pl.BlockSpec(memory_space=pl.ANY)],
            out_specs=pl.BlockSpec((None, tr, cols), lambda i, k_: (k_[0], i, 0))),
        out_shape=jax.ShapeDtypeStruct(land.shape, land.dtype), input_output_aliases={2: 0},
        compiler_params=_cp("parallel"), name=name)(jnp.reshape(k, (1,)).astype(jnp.int32), own, land)


def _full_weight(land, own, kind, k, *, name):
    land = _place_own(land, own, k, name=name)
    four, r, cols = land.shape
    if kind == "row":
        return land.reshape(four * r, cols)
    return jnp.transpose(land, (1, 0, 2)).reshape(r, four * cols)


class _GatheredWeights:
    def __init__(self, shards32, k, after):
        first = GATHER_GROUPS[0][1]
        self.shards, self.k = {n: shards32[n].astype(BF16)[0] for n in first}, k
        self.full = {}
        self.n_done = 0
        self.started, token = self._start(0, after)
        self.token = token[0, 0]
        self.shards.update({n: (w + token[0, 0]).astype(BF16)[0] for n, w in shards32.items() if n not in first})
        self.after = jnp.stack([self.shards[n][0, 0] for n in shards32 if n not in first]).astype(F32).reshape(1, -1)

    def _start(self, gi, after):
        tag, names = GATHER_GROUPS[gi]
        return _gather_start([self.shards[n] for n in names], tag, after)

    def mark(self, value):
        self.after = value

    def __getitem__(self, name):
        if name not in self.full:
            tag, names = GATHER_GROUPS[self.n_done]
            assert name in names, (name, tag)
            lands = _gather_wait(self.started, self.after, tag)
            before = []
            if self.n_done + 1 < len(GATHER_GROUPS):
                self.started, token = self._start(self.n_done + 1, lands[0])
                before = [token]
            lands = _gather_d2d(lands, before, tag)
            for n, land in zip(names, lands):
                if n == "w_in":
                    self.full[n] = _pad_w_in_shards(_place_own(land, self.shards[n], self.k, name=f"own_{n}"))
                elif n.endswith("w_gate_up"):
                    self.full[n] = _place_own(land, self.shards[n], self.k, name=f"own_{n}")
                else:
                    self.full[n] = _full_weight(land, self.shards[n], dict(BIG)[n], self.k, name=f"own_{n}")
            self.n_done += 1
        return self.full[name]


def _shard_major(dw, kind):
    if isinstance(dw, tuple):
        return jnp.concatenate(dw, axis=0)
    if dw.ndim == 3:
        return dw
    if kind == "row":
        return dw.reshape(4, dw.shape[0] // 4, dw.shape[1])
    return jnp.transpose(dw.reshape(dw.shape[0], 4, dw.shape[1] // 4), (1, 0, 2))


def _pad_rows8(w):
    return jnp.concatenate([w, jnp.zeros((8 - w.shape[0], w.shape[1]), w.dtype)], axis=0)


def _group_lanes(v):
    r = v.shape[0]
    return jnp.pad(v.reshape(r, NG, NH // NG), ((0, 0), (0, 0), (0, 128 - NH // NG))).reshape(r, NG * 128)


def _ungroup_lanes(v):
    r = v.shape[0]
    return v.reshape(r, NG, 128)[:, :, :NH // NG].reshape(r, NH)


def _local_step(wfull, small, x, mem, target, token=0.0, on_grads=None):
    bl, s, _ = x.shape
    T = bl * s
    x2, t2 = x.reshape(T, D), target.reshape(T, D)
    mem2 = mem.reshape(-1, D)
    g = {}
    tok = [token]
    mark = getattr(wfull, "mark", lambda value: None)

    def gain(name):
        return small[name].reshape(1, -1) + tok[0]

    def emit(tag, names):
        if on_grads is not None:
            tok[0] = tok[0] + on_grads(tag, {n: g[n] for n in names})

    def ffn_fwd(h, n, wgu, wd, tag, next_gain=None):
        gate, up, a = _gate_up_fwd(n, wfull[wgu], name=f"{tag}_gate_up")
        mark(a)
        out = _mm(a, wfull[wd], "nn", tk=DFF, scale=FFN_RES, residual=h, norm_gain=next_gain, name=f"{tag}_down")
        return out, (n, gate, up, a)

    def ffn_bwd(dh, h, norm, wgu, wd, saved, tag):
        n, gate, up, a = saved
        dgate, dup = _act_bwd(dh, wfull[wd], gate, up, FFN_RES, name=f"{tag}_d_act")
        g[wd] = _mm(a, dh, "tn", tm=1408, scale=FFN_RES, name=f"{tag}_d_w_down")
        g[wgu] = (_mm(n, dgate, "tn", tn=1408, col_shards=2, name=f"{tag}_d_w_gate"),
                  _mm(n, dup, "tn", tn=1408, col_shards=2, name=f"{tag}_d_w_up"))
        emit(tag, (wgu, wd))
        dh_in, g[norm] = _gate_up_bwd_input(dgate, dup, wfull[wgu], h, gain(norm), dh, name=f"{tag}_d_norm")
        return dh_in

    n1 = _norm_fwd(x2, gain("ffn1_norm"), name="ffn1_norm")
    (h1, u), ffn1_saved = ffn_fwd(x2, n1, "ffn1_w_gate_up", "ffn1_w_down", "ffn1", gain("mix_norm"))
    mark(h1)
    pp = _mm(u, wfull["w_in"], "nn", tm=2048, tn=1152, name="in_proj")
    wa8 = _pad_rows8(small["conv_a_w"])
    ws8 = _pad_rows8(small["ssm_conv_w"])
    conv_b = gain("ssm_conv_b")
    bias128 = jnp.pad(gain("ssm_dt_bias"), ((0, 0), (0, 128 - NH)))
    ya_pre = _conv_a_fwd(pp, wa8, bl, s, name="conv_a")
    xc = _conv_ssm_fwd(pp, ws8, conv_b, bl, s, name="conv_ssm")
    mark(xc)
    alog = gain("ssm_a_log")
    dtg, dtt = _dt_fwd(pp, bias128, name="dt")
    arow, acol = _group_lanes(alog), alog.reshape(NG, NH // NG, 1)
    dexp = jnp.repeat(gain("ssm_d"), HD, axis=1)
    ng = gain("ssm_norm")
    y, yn, prev = _ssd_fwd(xc, pp, dtg, dtt, arow, acol, dexp, ng, bl, s, name="ssd")
    ya = _mm(ya_pre, wfull["w_out_a"], "nn", tn=1024, out_dtype=BF16, name="out_a")
    yb = _mm(yn, wfull["w_out_ssm"], "nn", tn=1024, tk=DI, out_dtype=BF16, name="out_ssm")
    merged = _merge_fwd(pp, ya, yb, name="merge")
    h2, un = _mm(merged, wfull["w_mix_out"], "nn", residual=h1, norm_gain=gain("xattn_norm"), name="mix_out")
    q = _mm(un, wfull["w_q"], "nn", tn=1024, out_dtype=BF16, name="q_proj")
    mn = _norm_fwd(mem2, gain("mem_norm"), name="mem_norm")
    kv = _mm(mn, wfull["w_kv"], "nn", tn=1024, out_dtype=BF16, name="kv_proj")
    o = _attn_fwd(q, kv, bl, s, name="attn")
    h3, n2 = _mm(o, wfull["w_o_x"], "nn", residual=h2, norm_gain=gain("ffn2_norm"), name="attn_out")
    h4, ffn2_saved = ffn_fwd(h3, n2, "ffn2_w_gate_up", "ffn2_w_down", "ffn2")
    sq_err, dh4, dgf = _final_loss(h4, gain("final_norm"), t2, name="final_loss")
    g["final_norm"] = dgf

    dh3 = ffn_bwd(dh4, h3, "ffn2_norm", "ffn2_w_gate_up", "ffn2_w_down", ffn2_saved, "ffn2")
    do = _mm(dh3, wfull["w_o_x"], "nt", tn=1024, out_dtype=BF16, name="d_attn_o")
    g["w_o_x"] = _mm(o, dh3, "tn",name="d_w_o_x")
    dq, dk, dv = _attn_bwd(q, kv, do, bl, s, name="d_attn")
    g["w_q"] = _mm(un, dq, "tn",name="d_w_q")
    dkv = jnp.concatenate([dk, dv], axis=1)
    dmn = _mm(dkv, wfull["w_kv"], "nt", tn=1024, tk=2 * D, name="d_mem_norm_out")
    g["w_kv"] = _mm(mn, dkv, "tn", tn=512, col_shards=4, name="d_w_kv")
    emit("attn", ("w_q", "w_kv", "w_o_x"))
    _, g["mem_norm"] = _norm_bwd(mem2, gain("mem_norm"), dmn, None, name="d_mem_norm")
    dh2, g["xattn_norm"] = _mm(dq, wfull["w_q"], "nt", tm=512, residual=dh3, norm_bwd=(h2, gain("xattn_norm")),
                               name="d_xattn_norm")
    g["w_mix_out"] = _mm(merged, dh2, "tn",name="d_w_mix_out")
    dya, dyb, dpp = _merge_bwd(pp, ya, yb, dh2, wfull["w_mix_out"], name="d_merge")
    dya_pre = _mm(dya, wfull["w_out_a"], "nt", tn=1024, name="d_conv_a_out")
    g["w_out_a"] = _mm(ya_pre, dya, "tn",name="d_w_out_a")
    dyn = _mm(dyb, wfull["w_out_ssm"], "nt", tn=DI, name="d_ssd_out")
    g["w_out_ssm"] = _mm(yn, dyb, "tn",name="d_w_out_ssm")
    dpp, dwa8 = _conv_a_bwd(pp, wa8, dya_pre, dpp, bl, s, name="d_conv_a")
    g["conv_a_w"] = dwa8[:3]
    dpp, dxs, dbm, dcm, ddtg, g["ssm_norm"], ddexp, dalg = _ssd_bwd(
        dyn, y, xc, pp, dtg, dtt, arow, acol, dexp, ng, prev, dpp, bl, s, name="d_ssd")
    g["ssm_d"] = ddexp.reshape(NH, HD).sum(axis=1).reshape(1, NH)
    g["ssm_a_log"] = _ungroup_lanes(dalg)
    conv_dw, conv_db = [], []
    for dpart, off, tag in ((dxs, 0, "x"), (dbm, DI, "b"), (dcm, DI + NG * NS, "c")):
        dpp, dw_, db_ = _conv_ssm_bwd(pp, ws8, conv_b, dpart, off, dpp, bl, s, name=f"d_conv_ssm_{tag}")
        conv_dw.append(dw_)
        conv_db.append(db_)
    g["ssm_conv_w"] = jnp.concatenate(conv_dw, axis=1)[:4]
    g["ssm_conv_b"] = jnp.concatenate(conv_db, axis=1)
    dpp, dbias = _dt_bwd(pp, bias128, ddtg, dpp, name="d_dt")
    g["ssm_dt_bias"] = dbias[:, :NH]
    g["w_in"] = _mm(u, dpp, "tn", tn=1152, name="d_w_in")
    emit("mix", ("w_in", "w_out_a", "w_out_ssm", "w_mix_out"))
    dh1, g["mix_norm"] = _mm(dpp, wfull["w_in"], "nt", tm=512, tk=3456, residual=dh2,
                             norm_bwd=(h1, gain("mix_norm")), name="d_mix_norm")
    dx = ffn_bwd(dh1, x2, "ffn1_norm", "ffn1_w_gate_up", "ffn1_w_down", ffn1_saved, "ffn1")
    return sq_err, dx, g


W_IN_SHARD = NIN // 4


def _w_in_segments():
    segs, p = [], 0
    for t in range(D // CA_TILE):
        for which in range(3):
            segs.append((D * which + CA_TILE * t, p, CA_TILE))
            p += CA_TILE
    for s, n in ((3 * D, O_GA - 3 * D), (O_GA + NH, 2 * D), (O_GA, NH)):
        segs.append((s, p, n))
        p += n
    assert p == NIN and segs[-1][1] == O_DT and segs[-2][1] == O_GA
    return segs


def _pad_w_in_shards(land):
    pieces = []
    for s, _, n in _w_in_segments():
        while n > 0:
            kk, off = divmod(s, W_IN_SHARD)
            take = min(n, W_IN_SHARD - off)
            pieces.append(land[kk][:, off:off + take])
            s, n = s + take, n - take
    return jnp.concatenate(pieces + [jnp.zeros((land.shape[1], NPP - NIN), land.dtype)], axis=1)


class _WInGrad:
    def __init__(self, dwp):
        self.dwp = dwp
        self.shape = (4, dwp.shape[0], W_IN_SHARD)

    def rows(self, start, n):
        part = lax.dynamic_slice_in_dim(self.dwp, start, n, axis=0)
        shards = []
        for kk in range(4):
            n0, n1 = W_IN_SHARD * kk, W_IN_SHARD * (kk + 1)
            cuts = sorted((max(s, n0), p + max(s, n0) - s, min(s + m, n1) - max(s, n0))
                          for s, p, m in _w_in_segments() if min(s + m, n1) > max(s, n0))
            shards.append(jnp.concatenate([part[:, p:p + m] for _, p, m in cuts], axis=1))
        return jnp.stack(shards)


def _pad_w_in(w):
    return _pad_w_in_shards(jnp.stack(jnp.split(w, 4, axis=1)))


def _unpad_w_in(w):
    return jnp.concatenate(list(_WInGrad(w).rows(0, w.shape[0])), axis=1)


def kernel(x, mem, ffn1_norm, ffn1_w_gate_up, ffn1_w_down, mix_norm, w_in, conv_a_w, w_out_a, ssm_conv_w, ssm_conv_b, ssm_dt_bias, ssm_a_log, ssm_d, ssm_norm, w_out_ssm, w_mix_out, xattn_norm, mem_norm, w_q, w_kv, w_o_x, ffn2_norm, ffn2_w_gate_up, ffn2_w_down, final_norm, loss_target, m_ffn1_norm, m_ffn1_w_gate_up, m_ffn1_w_down, m_mix_norm, m_w_in, m_conv_a_w, m_w_out_a, m_ssm_conv_w, m_ssm_conv_b, m_ssm_dt_bias, m_ssm_a_log, m_ssm_d, m_ssm_norm, m_w_out_ssm, m_w_mix_out, m_xattn_norm, m_mem_norm, m_w_q, m_w_kv, m_w_o_x, m_ffn2_norm, m_ffn2_w_gate_up, m_ffn2_w_down, m_final_norm, v_ffn1_norm, v_ffn1_w_gate_up, v_ffn1_w_down, v_mix_norm, v_w_in, v_conv_a_w, v_w_out_a, v_ssm_conv_w, v_ssm_conv_b, v_ssm_dt_bias, v_ssm_a_log, v_ssm_d, v_ssm_norm, v_w_out_ssm, v_w_mix_out, v_xattn_norm, v_mem_norm, v_w_q, v_w_kv, v_w_o_x, v_ffn2_norm, v_ffn2_w_gate_up, v_ffn2_w_down, v_final_norm):
    a = dict(locals())
    xi, yi = lax.axis_index("x"), lax.axis_index("y")
    k = 2 * xi + yi

    conv_vec, conv_offs = _pack([a["conv_a_w"], a["ssm_conv_w"]])
    conv_all = _gather_all(conv_vec, name="gather_conv_w")
    wfull = _GatheredWeights({n: a[n] for n, _ in BIG}, k, conv_all)
    conv_sh = [_unpack(conv_all[2 * kk], conv_offs, [a["conv_a_w"].shape[1:], a["ssm_conv_w"].shape[1:]])
               for kk in range(4)]
    small = {n: a[n] for n in SMALL}
    small["conv_a_w"] = jnp.concatenate([cs[0] for cs in conv_sh], axis=1)
    small["ssm_conv_w"] = jnp.concatenate([cs[1] for cs in conv_sh], axis=1)

    rs_started = []

    def on_grads(tag, grads):
        names = [n for n, _ in BIG if n in grads]
        shard_major = [_WInGrad(grads[n]) if n == "w_in" else _shard_major(grads[n], dict(BIG)[n]) for n in names]
        st, tk = _rs_start(shard_major, tag)
        rs_started.append((tag, names, st))
        return tk[0, 0]

    sq_err, dx, g = _local_step(wfull, small, x, mem, loss_target, wfull.token, on_grads)
    loss = lax.psum(0.5 / D * jnp.sum(sq_err), ("x", "y", "c"))

    ci = lax.axis_index("c")
    out = {}

    def finish(tag, names, st, after):
        g_mine, g_other = _rs_finish(st, after, tag)
        for n, gm, go in zip(names, g_mine, g_other):
            res = _adamw_halves(a[n], gm, go, a["m_" + n], a["v_" + n], ci, name=f"adamw_{n}")
            out[n] = tuple(t.reshape(a[n].shape) for t in res)
        return res[1]

    done = dx
    for grp in rs_started[:-1]:
        done = finish(*grp, dx)

    full_shapes = [g[n].shape for n in SMALL]
    gvec, goffs = _pack([g[n] for n in SMALL])
    gsum = _sum_leading(_gather_all(gvec, name="gather_small_grads", before=[done]), name="sum_small_grads")
    finish(*rs_started[-1], gsum)
    gsmall = dict(zip(SMALL, _unpack(gsum, goffs, full_shapes)))
    for n in ("conv_a_w", "ssm_conv_w"):
        width = a[n].shape[2]
        gsmall[n] = lax.dynamic_slice_in_dim(gsmall[n], k * width, width, axis=1)
    local_shapes = [a[n].shape for n in SMALL]
    packs = [_pack([t[n] for n in SMALL]) for t in
             ({n: a[n] for n in SMALL}, gsmall, {n: a["m_" + n] for n in SMALL}, {n: a["v_" + n] for n in SMALL})]
    offs = packs[0][1]
    res = _adamw(*[p[0] for p in packs], name="adamw_small")
    unp = [_unpack(r, offs, local_shapes) for r in res]
    for i, n in enumerate(SMALL):
        out[n] = (gsmall[n].reshape(a[n].shape), unp[0][i], unp[1][i], unp[2][i])

    grad_x = dx.reshape(x.shape)
    return (loss, grad_x, *[out[n][0] for n in WEIGHTS], *[out[n][1] for n in WEIGHTS],
            *[out[n][2] for n in WEIGHTS], *[out[n][3] for n in WEIGHTS])
```
